```python
import math
import jax, jax.numpy as jnp
from jax import lax
import numpy as np

D_MODEL = 1024
BATCH = 8
SEQ = 4096
DEPTH = 4

HEAD_DIM = 64
SB_HEADS = 8
SWA_HEADS = 8
SWA_KV_HEADS = 2
WINDOW = 128
BLOCK = 128
D_FF = 4 * D_MODEL
ROPE_THETA = 10000.0
NORM_EPS = 1e-6
N_BRANCHES = 2

SB_WIDTH = SB_HEADS * HEAD_DIM
SWA_Q_WIDTH = SWA_HEADS * HEAD_DIM
SWA_KV_WIDTH = SWA_KV_HEADS * HEAD_DIM
IN_WIDTH = 3 * SB_WIDTH + SWA_Q_WIDTH + 2 * SWA_KV_WIDTH + N_BRANCHES * D_MODEL

kernel_name = "hybrid_stickbreak_swa_sink_gated_trunk"


def rms_norm(x, g):
    xf = x.astype(jnp.float32)
    y = xf * lax.rsqrt(jnp.mean(xf * xf, axis=-1, keepdims=True) + NORM_EPS)
    return (y * g.astype(jnp.float32)).astype(x.dtype)


def rope_tables(seq):
    inv_freq = 1.0 / (ROPE_THETA ** (jnp.arange(0, HEAD_DIM, 2, dtype=jnp.float32) / HEAD_DIM))
    ang = jnp.arange(seq, dtype=jnp.float32)[:, None] * inv_freq[None, :]
    return jnp.cos(ang), jnp.sin(ang)


def apply_rope(x, cos, sin):
    c = cos[None, :, None, :].astype(x.dtype)
    s = sin[None, :, None, :].astype(x.dtype)
    x1, x2 = jnp.split(x, 2, axis=-1)
    return jnp.concatenate([x1 * c - x2 * s, x2 * c + x1 * s], axis=-1)


def stick_breaking_attention(q, k, v):
    B, S, H, d = q.shape
    scale = d ** -0.5
    outs = []
    for blk in range(S // BLOCK):
        t0, t1 = blk * BLOCK, (blk + 1) * BLOCK
        qb = q[:, t0:t1]
        kb = k[:, :t1]
        vb = v[:, :t1]
        z = jnp.einsum('bqhd,bshd->bhqs', qb, kb).astype(jnp.float32) * scale
        t_idx = jnp.arange(t0, t1)[:, None]
        s_idx = jnp.arange(t1)[None, :]
        strict = s_idx < t_idx
        log_keep = jnp.where(strict, -jax.nn.softplus(z), 0.0)
        tail = lax.cumsum(log_keep, axis=3, reverse=True) - log_keep
        w = jnp.where(strict, jnp.exp(jax.nn.log_sigmoid(z) + tail), 0.0)
        outs.append(jnp.einsum('bhqs,bshd->bqhd', w.astype(v.dtype), vb))
    return jnp.concatenate(outs, axis=1)


def sliding_window_sink_attention(q, k, v, sinks):
    B, S, Hq, d = q.shape
    G = Hq // SWA_KV_HEADS
    nb = S // BLOCK
    scale = d ** -0.5
    qb = q.reshape(B, nb, BLOCK, SWA_KV_HEADS, G, d)
    pad = ((0, 0), (BLOCK, 0), (0, 0), (0, 0))
    kb = jnp.pad(k, pad).reshape(B, nb + 1, BLOCK, SWA_KV_HEADS, d)
    vb = jnp.pad(v, pad).reshape(B, nb + 1, BLOCK, SWA_KV_HEADS, d)
    k_band = jnp.concatenate([kb[:, :-1], kb[:, 1:]], axis=2)
    v_band = jnp.concatenate([vb[:, :-1], vb[:, 1:]], axis=2)
    scores = jnp.einsum('bnqkgd,bnskd->bnkgqs', qb, k_band).astype(jnp.float32) * scale
    i = jnp.arange(BLOCK)[:, None]
    j = jnp.arange(2 * BLOCK)[None, :]
    rel = j - BLOCK - i
    in_window = (rel <= 0) & (rel > -WINDOW)
    key_pos = jnp.arange(nb)[:, None, None] * BLOCK + j[None] - BLOCK
    valid = in_window[None] & (key_pos >= 0)
    scores = jnp.where(valid[None, :, None, None], scores, -jnp.inf)
    sink = jnp.broadcast_to(sinks.astype(jnp.float32).reshape(1, 1, SWA_KV_HEADS, G, 1, 1),
                            scores.shape[:-1] + (1,))
    probs = jax.nn.softmax(jnp.concatenate([scores, sink], axis=-1), axis=-1)[..., :-1]
    out = jnp.einsum('bnkgqs,bnskd->bnqkgd', probs.astype(v.dtype), v_band)
    return out.reshape(B, S, Hq * d)


def _fwd_setup_inputs(seed: int = 0) -> dict:
    key = jax.random.key(seed)
    ks = jax.random.split(key, 12)
    nrm = lambda k, shape, scale: jax.random.normal(k, shape, jnp.float32) * scale
    return {
        "x": nrm(ks[0], (BATCH, SEQ, D_MODEL), 1.0),
        "mix_norm_g": 1.0 + nrm(ks[1], (DEPTH, D_MODEL), 0.02),
        "w_in": nrm(ks[2], (DEPTH, D_MODEL, IN_WIDTH), D_MODEL ** -0.5),
        "q_norm_g": 1.0 + nrm(ks[3], (DEPTH, HEAD_DIM), 0.02),
        "k_norm_g": 1.0 + nrm(ks[4], (DEPTH, HEAD_DIM), 0.02),
        "sinks": nrm(ks[5], (DEPTH, SWA_HEADS), 0.5),
        "w_branch_sb": nrm(ks[6], (DEPTH, SB_WIDTH, D_MODEL), SB_WIDTH ** -0.5),
        "w_branch_swa": nrm(ks[7], (DEPTH, SWA_Q_WIDTH, D_MODEL), SWA_Q_WIDTH ** -0.5),
        "w_out": nrm(ks[8], (DEPTH, D_MODEL, D_MODEL), D_MODEL ** -0.5),
        "mlp_norm_g": 1.0 + nrm(ks[9], (DEPTH, D_MODEL), 0.02),
        "w_up": nrm(ks[10], (DEPTH, D_MODEL, D_FF), D_MODEL ** -0.5),
        "w_down": nrm(ks[11], (DEPTH, D_FF, D_MODEL), D_FF ** -0.5),
    }


def _fwd_reference(x, mix_norm_g, w_in, q_norm_g, k_norm_g, sinks, w_branch_sb, w_branch_swa,
              w_out, mlp_norm_g, w_up, w_down):
    B, S, D = x.shape
    cos, sin = rope_tables(S)
    split_at = np.cumsum([SB_WIDTH, SB_WIDTH, SB_WIDTH, SWA_Q_WIDTH, SWA_KV_WIDTH, SWA_KV_WIDTH]).tolist()
    for l in range(DEPTH):
        h = rms_norm(x, mix_norm_g[l])
        proj = h @ w_in[l]
        sb_q, sb_k, sb_v, sw_q, sw_k, sw_v, gate_logits = jnp.split(proj, split_at, axis=-1)

        to_heads = lambda t, n: t.reshape(B, S, n, HEAD_DIM)
        o_sb = stick_breaking_attention(to_heads(sb_q, SB_HEADS), to_heads(sb_k, SB_HEADS),
                                        to_heads(sb_v, SB_HEADS)).reshape(B, S, SB_WIDTH)
        y_sb = o_sb @ w_branch_sb[l]

        q = apply_rope(rms_norm(to_heads(sw_q, SWA_HEADS), q_norm_g[l]), cos, sin)
        k = apply_rope(rms_norm(to_heads(sw_k, SWA_KV_HEADS), k_norm_g[l]), cos, sin)
        v = to_heads(sw_v, SWA_KV_HEADS)
        y_swa = sliding_window_sink_attention(q, k, v, sinks[l]) @ w_branch_swa[l]

        gates = jax.nn.sigmoid(gate_logits.astype(jnp.float32)).astype(x.dtype).reshape(B, S, N_BRANCHES, D)
        merged = gates[:, :, 0] * y_sb + gates[:, :, 1] * y_swa
        x = x + merged @ w_out[l]

        h2 = rms_norm(x, mlp_norm_g[l])
        x = x + jnp.square(jax.nn.relu(h2 @ w_up[l])) @ w_down[l]
    return x


import jax as _jax
import jax.numpy as _jnp

TWIN_FORMAT = 'train_step'
FWD_PARAMS = ['x', 'mix_norm_g', 'w_in', 'q_norm_g', 'k_norm_g', 'sinks', 'w_branch_sb', 'w_branch_swa', 'w_out', 'mlp_norm_g', 'w_up', 'w_down']
TWIN_WEIGHTS = ['mix_norm_g', 'w_in', 'q_norm_g', 'k_norm_g', 'sinks', 'w_branch_sb', 'w_branch_swa', 'w_out', 'mlp_norm_g', 'w_up', 'w_down']
TWIN_DIFF_INPUT = 'x'
TWIN_INPUTS = ['x', 'mix_norm_g', 'w_in', 'q_norm_g', 'k_norm_g', 'sinks', 'w_branch_sb', 'w_branch_swa', 'w_out', 'mlp_norm_g', 'w_up', 'w_down', 'loss_target', 'm_mix_norm_g', 'm_w_in', 'm_q_norm_g', 'm_k_norm_g', 'm_sinks', 'm_w_branch_sb', 'm_w_branch_swa', 'm_w_out', 'm_mlp_norm_g', 'm_w_up', 'm_w_down', 'v_mix_norm_g', 'v_w_in', 'v_q_norm_g', 'v_k_norm_g', 'v_sinks', 'v_w_branch_sb', 'v_w_branch_swa', 'v_w_out', 'v_mlp_norm_g', 'v_w_up', 'v_w_down']
TWIN_OUTPUTS = ['loss', 'grad_x', 'grad_mix_norm_g', 'grad_w_in', 'grad_q_norm_g', 'grad_k_norm_g', 'grad_sinks', 'grad_w_branch_sb', 'grad_w_branch_swa', 'grad_w_out', 'grad_mlp_norm_g', 'grad_w_up', 'grad_w_down', 'delta_mix_norm_g', 'delta_w_in', 'delta_q_norm_g', 'delta_k_norm_g', 'delta_sinks', 'delta_w_branch_sb', 'delta_w_branch_swa', 'delta_w_out', 'delta_mlp_norm_g', 'delta_w_up', 'delta_w_down', 'new_m_mix_norm_g', 'new_m_w_in', 'new_m_q_norm_g', 'new_m_k_norm_g', 'new_m_sinks', 'new_m_w_branch_sb', 'new_m_w_branch_swa', 'new_m_w_out', 'new_m_mlp_norm_g', 'new_m_w_up', 'new_m_w_down', 'new_v_mix_norm_g', 'new_v_w_in', 'new_v_q_norm_g', 'new_v_k_norm_g', 'new_v_sinks', 'new_v_w_branch_sb', 'new_v_w_branch_swa', 'new_v_w_out', 'new_v_mlp_norm_g', 'new_v_w_up', 'new_v_w_down']
TWIN_LEAF_KINDS = {'loss': 'loss', 'grad_x': 'grad_x', 'grad_mix_norm_g': 'grad_w', 'grad_w_in': 'grad_w', 'grad_q_norm_g': 'grad_w', 'grad_k_norm_g': 'grad_w', 'grad_sinks': 'grad_w', 'grad_w_branch_sb': 'grad_w', 'grad_w_branch_swa': 'grad_w', 'grad_w_out': 'grad_w', 'grad_mlp_norm_g': 'grad_w', 'grad_w_up': 'grad_w', 'grad_w_down': 'grad_w', 'delta_mix_norm_g': 'delta_w', 'delta_w_in': 'delta_w', 'delta_q_norm_g': 'delta_w', 'delta_k_norm_g': 'delta_w', 'delta_sinks': 'delta_w', 'delta_w_branch_sb': 'delta_w', 'delta_w_branch_swa': 'delta_w', 'delta_w_out': 'delta_w', 'delta_mlp_norm_g': 'delta_w', 'delta_w_up': 'delta_w', 'delta_w_down': 'delta_w', 'new_m_mix_norm_g': 'new_m', 'new_m_w_in': 'new_m', 'new_m_q_norm_g': 'new_m', 'new_m_k_norm_g': 'new_m', 'new_m_sinks': 'new_m', 'new_m_w_branch_sb': 'new_m', 'new_m_w_branch_swa': 'new_m', 'new_m_w_out': 'new_m', 'new_m_mlp_norm_g': 'new_m', 'new_m_w_up': 'new_m', 'new_m_w_down': 'new_m', 'new_v_mix_norm_g': 'new_v', 'new_v_w_in': 'new_v', 'new_v_q_norm_g': 'new_v', 'new_v_k_norm_g': 'new_v', 'new_v_sinks': 'new_v', 'new_v_w_branch_sb': 'new_v', 'new_v_w_branch_swa': 'new_v', 'new_v_w_out': 'new_v', 'new_v_mlp_norm_g': 'new_v', 'new_v_w_up': 'new_v', 'new_v_w_down': 'new_v'}


def _forward(args):
    return _fwd_reference(*[args[k] for k in FWD_PARAMS])


def _output_shape():
    out = _jax.eval_shape(lambda: _forward(_fwd_setup_inputs(0)))
    return out.shape, out.dtype

N_MICROBATCH = 1
ADAM_LR = 0.001
ADAM_B1 = 0.9
ADAM_B2 = 0.999
ADAM_EPS = 1e-08
ADAM_WD = 0.01
ADAM_STEP = 10
PER_EXAMPLE_BATCH_AXIS = {'x': 0, 'loss_target': 0}
SHARED_INPUTS = []
_WEIGHT_DTYPES = {'mix_norm_g': _jnp.float32, 'w_in': _jnp.float32, 'q_norm_g': _jnp.float32, 'k_norm_g': _jnp.float32, 'sinks': _jnp.float32, 'w_branch_sb': _jnp.float32, 'w_branch_swa': _jnp.float32, 'w_out': _jnp.float32, 'mlp_norm_g': _jnp.float32, 'w_up': _jnp.float32, 'w_down': _jnp.float32}
MOMENT_SCALE = {'mix_norm_g': 1.590050e+01, 'w_in': 6.246719e+00, 'q_norm_g': 1.353175e+00, 'k_norm_g': 1.317651e+00, 'sinks': 8.282041e-01, 'w_branch_sb': 8.871294e+00, 'w_branch_swa': 9.162030e+00, 'w_out': 1.251007e+01, 'mlp_norm_g': 9.915140e+01, 'w_up': 8.513653e+00, 'w_down': 3.190968e+01}


def _to_microbatches(a, axis):
    t = _jnp.moveaxis(a, axis, 0)
    t = t.reshape((N_MICROBATCH, t.shape[0] // N_MICROBATCH) + t.shape[1:])
    return _jnp.moveaxis(t, 1, axis + 1)


def setup_inputs(seed: int = 0) -> dict:
    inp = _fwd_setup_inputs(seed)
    key = _jax.random.fold_in(_jax.random.key(seed), 7919)
    shape, _ = _output_shape()
    out = dict(inp)
    out["loss_target"] = _jax.random.normal(_jax.random.fold_in(key, 0), shape, _jnp.float32)
    for i, name in enumerate(TWIN_WEIGHTS):
        w = inp[name].astype(_jnp.float32)
        if MOMENT_SCALE is None:
            s = _jnp.sqrt(_jnp.mean(_jnp.square(w)) + 1e-30)
        else:
            s = MOMENT_SCALE[name]
        km, kv = _jax.random.split(_jax.random.fold_in(key, i + 1))
        out[name] = w
        out["m_" + name] = s * _jax.random.normal(km, w.shape, _jnp.float32)
        out["v_" + name] = (s * s) * _jax.random.uniform(kv, w.shape, _jnp.float32, 0.5, 1.5)
    if N_MICROBATCH > 1:
        for name, axis in PER_EXAMPLE_BATCH_AXIS.items():
            out[name] = _to_microbatches(out[name], axis)
    return {'x': out['x'], 'mix_norm_g': out['mix_norm_g'], 'w_in': out['w_in'], 'q_norm_g': out['q_norm_g'], 'k_norm_g': out['k_norm_g'], 'sinks': out['sinks'], 'w_branch_sb': out['w_branch_sb'], 'w_branch_swa': out['w_branch_swa'], 'w_out': out['w_out'], 'mlp_norm_g': out['mlp_norm_g'], 'w_up': out['w_up'], 'w_down': out['w_down'], 'loss_target': out['loss_target'], 'm_mix_norm_g': out['m_mix_norm_g'], 'm_w_in': out['m_w_in'], 'm_q_norm_g': out['m_q_norm_g'], 'm_k_norm_g': out['m_k_norm_g'], 'm_sinks': out['m_sinks'], 'm_w_branch_sb': out['m_w_branch_sb'], 'm_w_branch_swa': out['m_w_branch_swa'], 'm_w_out': out['m_w_out'], 'm_mlp_norm_g': out['m_mlp_norm_g'], 'm_w_up': out['m_w_up'], 'm_w_down': out['m_w_down'], 'v_mix_norm_g': out['v_mix_norm_g'], 'v_w_in': out['v_w_in'], 'v_q_norm_g': out['v_q_norm_g'], 'v_k_norm_g': out['v_k_norm_g'], 'v_sinks': out['v_sinks'], 'v_w_branch_sb': out['v_w_branch_sb'], 'v_w_branch_swa': out['v_w_branch_swa'], 'v_w_out': out['v_w_out'], 'v_mlp_norm_g': out['v_mlp_norm_g'], 'v_w_up': out['v_w_up'], 'v_w_down': out['v_w_down']}


def _loss(weights, diff, rest, loss_target):
    with _jax.named_scope("forward"):
        args = {**rest, TWIN_DIFF_INPUT: diff, **{k: w.astype(_WEIGHT_DTYPES[k]) for k, w in weights.items()}}
        y = _forward(args)
    with _jax.named_scope("loss_head"):
        err = _jnp.square(y.astype(_jnp.float32) - loss_target)
        return 0.5 * _jnp.sum(_jnp.mean(err, axis=-1)) if err.ndim else 0.5 * err


def _adamw(w, g, m, v):
    m = ADAM_B1 * m + (1.0 - ADAM_B1) * g
    v = ADAM_B2 * v + (1.0 - ADAM_B2) * _jnp.square(g)
    m_hat = m / (1.0 - ADAM_B1 ** ADAM_STEP)
    v_hat = v / (1.0 - ADAM_B2 ** ADAM_STEP)
    delta = -ADAM_LR * (m_hat / (_jnp.sqrt(v_hat) + ADAM_EPS) + ADAM_WD * w)
    return delta, m, v


def reference(x, mix_norm_g, w_in, q_norm_g, k_norm_g, sinks, w_branch_sb, w_branch_swa, w_out, mlp_norm_g, w_up, w_down, loss_target, m_mix_norm_g, m_w_in, m_q_norm_g, m_k_norm_g, m_sinks, m_w_branch_sb, m_w_branch_swa, m_w_out, m_mlp_norm_g, m_w_up, m_w_down, v_mix_norm_g, v_w_in, v_q_norm_g, v_k_norm_g, v_sinks, v_w_branch_sb, v_w_branch_swa, v_w_out, v_mlp_norm_g, v_w_up, v_w_down):
    given = dict(x=x, mix_norm_g=mix_norm_g, w_in=w_in, q_norm_g=q_norm_g, k_norm_g=k_norm_g, sinks=sinks, w_branch_sb=w_branch_sb, w_branch_swa=w_branch_swa, w_out=w_out, mlp_norm_g=mlp_norm_g, w_up=w_up, w_down=w_down, loss_target=loss_target, m_mix_norm_g=m_mix_norm_g, m_w_in=m_w_in, m_q_norm_g=m_q_norm_g, m_k_norm_g=m_k_norm_g, m_sinks=m_sinks, m_w_branch_sb=m_w_branch_sb, m_w_branch_swa=m_w_branch_swa, m_w_out=m_w_out, m_mlp_norm_g=m_mlp_norm_g, m_w_up=m_w_up, m_w_down=m_w_down, v_mix_norm_g=v_mix_norm_g, v_w_in=v_w_in, v_q_norm_g=v_q_norm_g, v_k_norm_g=v_k_norm_g, v_sinks=v_sinks, v_w_branch_sb=v_w_branch_sb, v_w_branch_swa=v_w_branch_swa, v_w_out=v_w_out, v_mlp_norm_g=v_mlp_norm_g, v_w_up=v_w_up, v_w_down=v_w_down)
    weights = {n: given[n] for n in TWIN_WEIGHTS}
    shared = {n: given[n] for n in SHARED_INPUTS}
    per_example = {n: given[n] for n in ['x']}
    grad_fn = _jax.value_and_grad(_loss, argnums=(0, 1))

    def one_microbatch(ex, loss_target):
        ex = dict(ex)
        diff = ex.pop(TWIN_DIFF_INPUT)
        return grad_fn(weights, diff, {**shared, **ex}, loss_target)

    if N_MICROBATCH == 1:
        loss, (grad_w, grad_x) = one_microbatch(per_example, given["loss_target"])
    else:
        def body(carry, xs):
            loss_sum, grad_sum = carry
            l_k, (gw_k, gx_k) = one_microbatch(xs[0], xs[1])
            with _jax.named_scope("update"):
                return (loss_sum + l_k, _jax.tree.map(_jnp.add, grad_sum, gw_k)), gx_k

        init = (_jnp.zeros((), _jnp.float32), _jax.tree.map(_jnp.zeros_like, weights))
        (loss, grad_w), grad_x = _jax.lax.scan(body, init, (per_example, given["loss_target"]))
    with _jax.named_scope("update"):
        delta_w, new_m, new_v = {}, {}, {}
        for n in TWIN_WEIGHTS:
            delta_w[n], new_m[n], new_v[n] = _adamw(weights[n], grad_w[n], given["m_" + n], given["v_" + n])
    return (loss, grad_x, *[grad_w[n] for n in TWIN_WEIGHTS], *[delta_w[n] for n in TWIN_WEIGHTS],
            *[new_m[n] for n in TWIN_WEIGHTS], *[new_v[n] for n in TWIN_WEIGHTS])
```

```python
import functools

import jax
import jax.numpy as jnp
from jax import lax
from jax.experimental import pallas as pl
from jax.experimental.pallas import tpu as pltpu

F32 = jnp.float32
BF16 = jnp.bfloat16

DEPTH = 4
D_MODEL = 1024
HEAD_DIM = 64
LANES = 128
WINDOW = 128
SB_WIDTH = 512
SWA_Q_WIDTH = 512
SWA_KV_WIDTH = 128
ATTN_WIDTH = 3 * SB_WIDTH + SWA_Q_WIDTH + 2 * SWA_KV_WIDTH
D_FF = 4096
ROPE_THETA = 10000.0
NORM_EPS = 1e-6
SCALE = HEAD_DIM ** -0.5
NEG = -1e30
N_DEV = 8

ADAM_LR = 0.001
ADAM_B1 = 0.9
ADAM_B2 = 0.999
ADAM_EPS = 1e-08
ADAM_WD = 0.01
ADAM_STEP = 10

SB_TILE = 256
SB_CUTOFF = -104.0
SWA_TQ = 128
SWA_TK = 256
ROW_TILE = 256
PACK_TILE = 1536
VMEM_LIMIT = 56 * 1024 * 1024

MATRIX_NAMES = ("w_in", "w_branch_sb", "w_branch_swa", "w_out", "w_up", "w_down")
ROW_SHARDED = ("w_out", "w_down")
SMALL_NAMES = ("mix_norm_g", "q_norm_g", "k_norm_g", "sinks", "mlp_norm_g")
WEIGHT_ORDER = ("mix_norm_g", "w_in", "q_norm_g", "k_norm_g", "sinks", "w_branch_sb", "w_branch_swa", "w_out",
                "mlp_norm_g", "w_up", "w_down")
MESH_AXES = ("x", "y", "c")


def _params(*sem):
    return pltpu.CompilerParams(dimension_semantics=sem, vmem_limit_bytes=VMEM_LIMIT)


def _dot(a, b):
    return jnp.dot(a, b, preferred_element_type=F32)


def _dot_nt(a, b):
    return lax.dot_general(a, b, (((1,), (1,)), ((), ())), preferred_element_type=F32)


def _dot_tn(a, b):
    return lax.dot_general(a, b, (((0,), (0,)), ((), ())), preferred_element_type=F32)


def _split_bf16(x):
    hi = x.astype(BF16)
    lo = (x - hi.astype(F32)).astype(BF16)
    return hi, lo


def _dot2(x, b):
    hi, lo = _split_bf16(x)
    return _dot(hi, b) + _dot(lo, b)


def _dot2_nt(x, b):
    hi, lo = _split_bf16(x)
    return _dot_nt(hi, b) + _dot_nt(lo, b)


def _rsqrt_ms(x):
    return lax.rsqrt(jnp.mean(x * x, axis=-1, keepdims=True) + NORM_EPS)


def norm_matmul(x, g, w, *, gate_split, name):
    s, d = x.shape
    n = w.shape[1]
    tm = min(ROW_TILE, s)

    def body(x_ref, g_ref, w_ref, h_ref, *outs):
        xv = x_ref[...]
        h = ((xv * _rsqrt_ms(xv)) * g_ref[...]).astype(BF16)
        h_ref[...] = h
        p = _dot(h, w_ref[...])
        if gate_split is None:
            outs[0][...] = p.astype(BF16)
        else:
            outs[0][...] = p[:, :gate_split].astype(BF16)
            outs[1][...] = (1.0 / (1.0 + jnp.exp(-p[:, gate_split:]))).astype(BF16)

    row = lambda i: (i, 0)
    fixed = lambda i: (0, 0)
    if gate_split is None:
        out_shape = (jax.ShapeDtypeStruct((s, d), BF16), jax.ShapeDtypeStruct((s, n), BF16))
        out_specs = (pl.BlockSpec((tm, d), row), pl.BlockSpec((tm, n), row))
    else:
        out_shape = (jax.ShapeDtypeStruct((s, d), BF16), jax.ShapeDtypeStruct((s, gate_split), BF16),
                     jax.ShapeDtypeStruct((s, n - gate_split), BF16))
        out_specs = (pl.BlockSpec((tm, d), row), pl.BlockSpec((tm, gate_split), row),
                     pl.BlockSpec((tm, n - gate_split), row))
    return pl.pallas_call(
        body, name=name, grid=(s // tm,), out_shape=out_shape,
        in_specs=[pl.BlockSpec((tm, d), row), pl.BlockSpec((1, d), fixed), pl.BlockSpec((d, n), fixed)],
        out_specs=out_specs, compiler_params=_params("parallel"),
    )(x, g, w)


def merge_out_fwd(x, o_sb, o_sw, gates, w_bsb, w_bsw, w_o, *, name):
    s, d = x.shape
    tm = min(ROW_TILE, s)

    def body(x_ref, osb_ref, osw_ref, g_ref, wsb_ref, wsw_ref, wo_ref, x1_ref, ysb_ref, ysw_ref, mg_ref):
        y_sb = _dot(osb_ref[...].astype(BF16), wsb_ref[...])
        y_sw = _dot(osw_ref[...].astype(BF16), wsw_ref[...])
        g = g_ref[...].astype(F32)
        merged = (g[:, :d] * y_sb + g[:, d:] * y_sw).astype(BF16)
        ysb_ref[...] = y_sb.astype(BF16)
        ysw_ref[...] = y_sw.astype(BF16)
        mg_ref[...] = merged
        x1_ref[...] = x_ref[...] + _dot(merged, wo_ref[...])

    row = lambda i: (i, 0)
    fixed = lambda i: (0, 0)
    wd = o_sb.shape[1]
    return pl.pallas_call(
        body, name=name, grid=(s // tm,),
        out_shape=(jax.ShapeDtypeStruct((s, d), F32),) + (jax.ShapeDtypeStruct((s, d), BF16),) * 3,
        in_specs=[pl.BlockSpec((tm, d), row), pl.BlockSpec((tm, wd), row), pl.BlockSpec((tm, wd), row),
                  pl.BlockSpec((tm, 2 * d), row), pl.BlockSpec((wd, d), fixed), pl.BlockSpec((wd, d), fixed),
                  pl.BlockSpec((d, d), fixed)],
        out_specs=(pl.BlockSpec((tm, d), row),) * 4, compiler_params=_params("parallel"),
    )(x, o_sb, o_sw, gates, w_bsb, w_bsw, w_o)


def mlp_down_fwd(x1, u, w_down, *, name):
    s, d = x1.shape
    f = u.shape[1]
    tm = min(ROW_TILE, s)

    def body(x_ref, u_ref, w_ref, o_ref):
        a = jnp.maximum(u_ref[...].astype(F32), 0.0)
        o_ref[...] = x_ref[...] + _dot((a * a).astype(BF16), w_ref[...])

    row = lambda i: (i, 0)
    return pl.pallas_call(
        body, name=name, grid=(s // tm,), out_shape=jax.ShapeDtypeStruct((s, d), F32),
        in_specs=[pl.BlockSpec((tm, d), row), pl.BlockSpec((tm, f), row), pl.BlockSpec((f, d), lambda i: (0, 0))],
        out_specs=pl.BlockSpec((tm, d), row), compiler_params=_params("parallel"),
    )(x1, u, w_down)


def loss_head(y, target, *, name):
    s, d = y.shape
    tm = min(ROW_TILE, s)

    def body(y_ref, t_ref, dy_ref, loss_ref):
        @pl.when(pl.program_id(0) == 0)
        def _():
            loss_ref[...] = jnp.zeros_like(loss_ref)

        e = y_ref[...] - t_ref[...]
        dy_ref[...] = e * (1.0 / d)
        per_row = jnp.sum(e * e, axis=1, keepdims=True) * (0.5 / d)
        loss_ref[...] += jnp.sum(per_row, axis=0, keepdims=True)

    row = lambda i: (i, 0)
    return pl.pallas_call(
        body, name=name, grid=(s // tm,),
        out_shape=(jax.ShapeDtypeStruct((s, d), F32), jax.ShapeDtypeStruct((1, 1), F32)),
        in_specs=[pl.BlockSpec((tm, d), row), pl.BlockSpec((tm, d), row)],
        out_specs=(pl.BlockSpec((tm, d), row), pl.BlockSpec((1, 1), lambda i: (0, 0))),
        compiler_params=_params("arbitrary"),
    )(y, target)


def mlp_bwd_up(dx2, u, w_down, *, name):
    s, d = dx2.shape
    f = u.shape[1]
    tm = min(ROW_TILE, s)

    def body(dx_ref, u_ref, w_ref, du_ref):
        da = _dot_nt(dx_ref[...].astype(BF16), w_ref[...])
        du_ref[...] = (da * (2.0 * jnp.maximum(u_ref[...].astype(F32), 0.0))).astype(BF16)

    row = lambda i: (i, 0)
    return pl.pallas_call(
        body, name=name, grid=(s // tm,), out_shape=jax.ShapeDtypeStruct((s, f), BF16),
        in_specs=[pl.BlockSpec((tm, d), row), pl.BlockSpec((tm, f), row), pl.BlockSpec((f, d), lambda i: (0, 0))],
        out_specs=pl.BlockSpec((tm, f), row), compiler_params=_params("parallel"),
    )(dx2, u, w_down)


def matmul_nt_norm_bwd(dy, w, x, g, dres, *, name):
    s, n = dy.shape
    d = x.shape[1]
    tm = min(ROW_TILE, s)

    def body(dy_ref, w_ref, x_ref, g_ref, dres_ref, dx_ref, dg_ref):
        @pl.when(pl.program_id(0) == 0)
        def _():
            dg_ref[...] = jnp.zeros_like(dg_ref)

        dh = _dot_nt(dy_ref[...], w_ref[...])
        xv = x_ref[...]
        r = _rsqrt_ms(xv)
        dyg = dh * g_ref[...]
        dx_ref[...] = dres_ref[...] + r * dyg - xv * ((r * r * r) * jnp.mean(dyg * xv, axis=-1, keepdims=True))
        dg_ref[...] += jnp.sum(dh * (xv * r), axis=0, keepdims=True)

    row = lambda i: (i, 0)
    fixed = lambda i: (0, 0)
    return pl.pallas_call(
        body, name=name, grid=(s // tm,),
        out_shape=(jax.ShapeDtypeStruct((s, d), F32), jax.ShapeDtypeStruct((1, d), F32)),
        in_specs=[pl.BlockSpec((tm, n), row), pl.BlockSpec((d, n), fixed), pl.BlockSpec((tm, d), row),
                  pl.BlockSpec((1, d), fixed), pl.BlockSpec((tm, d), row)],
        out_specs=(pl.BlockSpec((tm, d), row), pl.BlockSpec((1, d), fixed)),
        compiler_params=_params("arbitrary"),
    )(dy, w, x, g, dres)


def out_bwd(dx1, w_o, gates, y_sb, y_sw, w_bsb, w_bsw, *, name):
    s, d = dx1.shape
    wd = w_bsb.shape[0]
    tm = min(ROW_TILE, s)

    def body(dx_ref, wo_ref, g_ref, ysb_ref, ysw_ref, wsb_ref, wsw_ref, dysb_ref, dysw_ref, dosb_ref, dosw_ref, dgl_ref):
        dm = _dot_nt(dx_ref[...].astype(BF16), wo_ref[...])
        g = g_ref[...].astype(F32)
        g0, g1 = g[:, :d], g[:, d:]
        dy_sb = (dm * g0).astype(BF16)
        dy_sw = (dm * g1).astype(BF16)
        dysb_ref[...] = dy_sb
        dysw_ref[...] = dy_sw
        dosb_ref[...] = _dot_nt(dy_sb, wsb_ref[...])
        dosw_ref[...] = _dot_nt(dy_sw, wsw_ref[...])
        dgl_ref[:, :d] = (dm * ysb_ref[...].astype(F32) * (g0 * (1.0 - g0))).astype(BF16)
        dgl_ref[:, d:] = (dm * ysw_ref[...].astype(F32) * (g1 * (1.0 - g1))).astype(BF16)

    row = lambda i: (i, 0)
    fixed = lambda i: (0, 0)
    return pl.pallas_call(
        body, name=name, grid=(s // tm,),
        out_shape=(jax.ShapeDtypeStruct((s, d), BF16), jax.ShapeDtypeStruct((s, d), BF16),
                   jax.ShapeDtypeStruct((s, wd), F32), jax.ShapeDtypeStruct((s, wd), F32),
                   jax.ShapeDtypeStruct((s, 2 * d), BF16)),
        in_specs=[pl.BlockSpec((tm, d), row), pl.BlockSpec((d, d), fixed), pl.BlockSpec((tm, 2 * d), row),
                  pl.BlockSpec((tm, d), row), pl.BlockSpec((tm, d), row), pl.BlockSpec((wd, d), fixed),
                  pl.BlockSpec((wd, d), fixed)],
        out_specs=(pl.BlockSpec((tm, d), row), pl.BlockSpec((tm, d), row), pl.BlockSpec((tm, wd), row),
                   pl.BlockSpec((tm, wd), row), pl.BlockSpec((tm, 2 * d), row)),
        compiler_params=_params("parallel"),
    )(dx1, w_o, gates, y_sb, y_sw, w_bsb, w_bsw)


def matmul_tn(a, b, *, relu2, name):
    s, m = a.shape
    n = b.shape[1]
    tmo = min(512, m)
    ts = min(512, s)

    def body(a_ref, b_ref, o_ref):
        @pl.when(pl.program_id(1) == 0)
        def _():
            o_ref[...] = jnp.zeros_like(o_ref)

        av = a_ref[...]
        if relu2:
            af = jnp.maximum(av.astype(F32), 0.0)
            av = af * af
        o_ref[...] += _dot_tn(av.astype(BF16), b_ref[...].astype(BF16))

    return pl.pallas_call(
        body, name=name, grid=(m // tmo, s // ts), out_shape=jax.ShapeDtypeStruct((m, n), F32),
        in_specs=[pl.BlockSpec((ts, tmo), lambda i, k: (k, i)), pl.BlockSpec((ts, n), lambda i, k: (k, 0))],
        out_specs=pl.BlockSpec((tmo, n), lambda i, k: (i, 0)), compiler_params=_params("parallel", "arbitrary"),
    )(a, b)


def _softplus(z):
    return jnp.maximum(z, 0.0) + jnp.log1p(jnp.exp(-jnp.abs(z)))


def _head_mask(h):
    return (lax.broadcasted_iota(jnp.int32, (1, LANES), 1) // HEAD_DIM) == h


def _strict_lower(t):
    return lax.broadcasted_iota(jnp.int32, (t, t), 1) < lax.broadcasted_iota(jnp.int32, (t, t), 0)


def sb_attn_fwd(proj, tri, *, name):
    s = proj.shape[0]
    t = min(SB_TILE, s)
    nq = s // t
    n_pairs = SB_WIDTH // LANES

    def body(q_ref, k_ref, v_ref, tri_ref, o_ref):
        strict = _strict_lower(t)

        def q_tile(qb, carry):
            q0 = pl.multiple_of(qb * t, t)
            q = q_ref[pl.ds(q0, t), :]
            out = jnp.zeros((t, LANES), F32)
            for h in range(2):
                hm = _head_mask(h)
                qh = jnp.where(hm, q, jnp.zeros_like(q))
                z = _dot_nt(qh, k_ref[pl.ds(q0, t), :]) * SCALE
                sp = _softplus(z)
                lk = jnp.where(strict, -sp, 0.0)
                w = jnp.where(strict, jnp.exp(z - sp + _dot2(lk, tri_ref[...])), 0.0)
                acc = _dot2(w, v_ref[pl.ds(q0, t), :])
                c = jnp.sum(lk, axis=1, keepdims=True)

                def cond(st):
                    return jnp.logical_and(st[0] >= 0, st[3] > SB_CUTOFF)

                def step(st):
                    kb, c, acc, _ = st
                    k0 = pl.multiple_of(kb * t, t)
                    z = _dot_nt(qh, k_ref[pl.ds(k0, t), :]) * SCALE
                    lk = -_softplus(z)
                    w = jnp.exp(z + lk + _dot2(lk, tri_ref[...]) + c)
                    acc = acc + _dot2(w, v_ref[pl.ds(k0, t), :])
                    c = c + jnp.sum(lk, axis=1, keepdims=True)
                    return kb - 1, c, acc, jnp.max(c)

                _, _, acc, _ = lax.while_loop(cond, step, (qb - 1, c, acc, jnp.max(c)))
                out = out + jnp.where(hm, acc, 0.0)
            o_ref[pl.ds(q0, t), :] = out
            return carry

        lax.fori_loop(0, nq, q_tile, 0)

    def col(j):
        return pl.BlockSpec((s, LANES), lambda p: (0, j * n_pairs + p))

    return pl.pallas_call(
        body, name=name, grid=(n_pairs,), out_shape=jax.ShapeDtypeStruct((s, SB_WIDTH), F32),
        in_specs=[col(0), col(1), col(2), pl.BlockSpec((t, t), lambda p: (0, 0))],
        out_specs=pl.BlockSpec((s, LANES), lambda p: (0, p)), compiler_params=_params("parallel"),
    )(proj, proj, proj, tri)


def sb_attn_bwd(proj, tri, o, do, *, name):
    s = proj.shape[0]
    t = min(SB_TILE, s)
    nq = s // t
    n_pairs = SB_WIDTH // LANES

    def body(q_ref, k_ref, v_ref, tri_ref, o_ref, do_ref, dq_ref, dk_ref, dv_ref, dk_acc, dv_acc):
        strict = _strict_lower(t)
        dk_acc[...] = jnp.zeros_like(dk_acc)
        dv_acc[...] = jnp.zeros_like(dv_acc)

        def block(qh, doh, doh_b, dd, k0, c, ce, diag):
            kt = k_ref[pl.ds(k0, t), :]
            vt = v_ref[pl.ds(k0, t), :]
            z = _dot_nt(qh, kt) * SCALE
            sp = _softplus(z)
            lb = z - sp
            lk = jnp.where(strict, -sp, 0.0) if diag else -sp
            w = jnp.exp(lb + _dot2(lk, tri_ref[...]) + c)
            if diag:
                w = jnp.where(strict, w, 0.0)
            e = w * _dot2_nt(doh, vt)
            dz = e - jnp.exp(lb) * (dd - ce - _dot2(e, tri_ref[...]))
            if diag:
                dz = jnp.where(strict, dz, 0.0)
            dzb = (dz * SCALE).astype(BF16)
            dk_acc[pl.ds(k0, t), :] += _dot_tn(dzb, qh)
            dv_acc[pl.ds(k0, t), :] += _dot_tn(w.astype(BF16), doh_b)
            return (_dot(dzb, kt), c + jnp.sum(lk, axis=1, keepdims=True), ce + jnp.sum(e, axis=1, keepdims=True))

        def q_tile(qb, carry):
            q0 = pl.multiple_of(qb * t, t)
            q = q_ref[pl.ds(q0, t), :]
            dov = do_ref[pl.ds(q0, t), :]
            ov = o_ref[pl.ds(q0, t), :]
            out = jnp.zeros((t, LANES), F32)
            for h in range(2):
                hm = _head_mask(h)
                qh = jnp.where(hm, q, jnp.zeros_like(q))
                doh = jnp.where(hm, dov, 0.0)
                doh_b = doh.astype(BF16)
                dd = jnp.sum(doh * ov, axis=1, keepdims=True)
                zero = jnp.zeros((t, 1), F32)
                dq, c, ce = block(qh, doh, doh_b, dd, q0, zero, zero, True)

                def cond(st):
                    return jnp.logical_and(st[0] >= 0, st[4] > SB_CUTOFF)

                def step(st):
                    kb, c, ce, dq, _ = st
                    ddq, c, ce = block(qh, doh, doh_b, dd, pl.multiple_of(kb * t, t), c, ce, False)
                    return kb - 1, c, ce, dq + ddq, jnp.max(c)

                st = lax.while_loop(cond, step, (qb - 1, c, ce, dq, jnp.max(c)))
                out = out + jnp.where(hm, st[3], 0.0)
            dq_ref[pl.ds(q0, t), :] = out.astype(BF16)
            return carry

        lax.fori_loop(0, nq, q_tile, 0)
        dk_ref[...] = dk_acc[...].astype(BF16)
        dv_ref[...] = dv_acc[...].astype(BF16)

    def col(j):
        return pl.BlockSpec((s, LANES), lambda p: (0, j * n_pairs + p))

    pair = pl.BlockSpec((s, LANES), lambda p: (0, p))
    return pl.pallas_call(
        body, name=name, grid=(n_pairs,), out_shape=(jax.ShapeDtypeStruct((s, SB_WIDTH), BF16),) * 3,
        in_specs=[col(0), col(1), col(2), pl.BlockSpec((t, t), lambda p: (0, 0)), pair, pair],
        out_specs=(pair, pair, pair),
        scratch_shapes=[pltpu.VMEM((s, LANES), F32), pltpu.VMEM((s, LANES), F32)],
        compiler_params=_params("parallel"),
    )(proj, proj, proj, tri, o, do)


def _lane_lo():
    return lax.broadcasted_iota(jnp.int32, (1, LANES), 1) < HEAD_DIM


def _swap_halves(x):
    return pltpu.roll(x, HEAD_DIM, 1)


def _rot_half(y):
    first = (lax.broadcasted_iota(jnp.int32, (1, LANES), 1) % HEAD_DIM) < (HEAD_DIM // 2)
    return jnp.where(first, pltpu.roll(y, LANES - HEAD_DIM // 2, 1), pltpu.roll(y, HEAD_DIM // 2, 1))


def _head_mean(v):
    lo = _lane_lo()
    s0 = jnp.sum(jnp.where(lo, v, 0.0), axis=1, keepdims=True)
    s1 = jnp.sum(jnp.where(lo, 0.0, v), axis=1, keepdims=True)
    return jnp.where(lo, s0, s1) * (1.0 / HEAD_DIM)


def swa_prep_fwd(proj, cos_p, sin_p, gq, gk, *, name):
    s = proj.shape[0]
    tm = min(512, s)
    q_blk = (3 * SB_WIDTH) // SWA_Q_WIDTH
    k_blk = (3 * SB_WIDTH + SWA_Q_WIDTH) // LANES

    def norm_rope(xv, g, cosv, sinv):
        y = (xv * lax.rsqrt(_head_mean(xv * xv) + NORM_EPS)) * g
        return y * cosv + _rot_half(y) * sinv

    def body(q_ref, k_ref, cos_ref, sin_ref, gq_ref, gk_ref, qn_ref, kn_ref):
        cosv, sinv = cos_ref[...], sin_ref[...]
        for j in range(SWA_Q_WIDTH // LANES):
            sl = slice(j * LANES, (j + 1) * LANES)
            qn_ref[:, sl] = norm_rope(q_ref[:, sl].astype(F32), gq_ref[...], cosv, sinv).astype(BF16)
        kn_ref[...] = norm_rope(k_ref[...].astype(F32), gk_ref[...], cosv, sinv).astype(BF16)

    row = lambda i: (i, 0)
    fixed = lambda i: (0, 0)
    return pl.pallas_call(
        body, name=name, grid=(s // tm,),
        out_shape=(jax.ShapeDtypeStruct((s, SWA_Q_WIDTH), BF16), jax.ShapeDtypeStruct((s, LANES), BF16)),
        in_specs=[pl.BlockSpec((tm, SWA_Q_WIDTH), lambda i: (i, q_blk)), pl.BlockSpec((tm, LANES), lambda i: (i, k_blk)),
                  pl.BlockSpec((tm, LANES), row), pl.BlockSpec((tm, LANES), row),
                  pl.BlockSpec((1, LANES), fixed), pl.BlockSpec((1, LANES), fixed)],
        out_specs=(pl.BlockSpec((tm, SWA_Q_WIDTH), row), pl.BlockSpec((tm, LANES), row)),
        compiler_params=_params("parallel"),
    )(proj, proj, cos_p, sin_p, gq, gk)


def swa_prep_bwd(proj, cos_p, sin_p, gq, gk, dqn, dkn, *, name):
    s = proj.shape[0]
    tm = min(512, s)
    q_blk = (3 * SB_WIDTH) // SWA_Q_WIDTH
    k_blk = (3 * SB_WIDTH + SWA_Q_WIDTH) // LANES

    def bwd(xv, g, cosv, sinv, dout):
        dy = dout * cosv + _rot_half(dout * sinv)
        r = lax.rsqrt(_head_mean(xv * xv) + NORM_EPS)
        dyg = dy * g
        dx = r * dyg - xv * ((r * r * r) * _head_mean(dyg * xv))
        return dx, jnp.sum(dy * (xv * r), axis=0, keepdims=True)

    def body(q_ref, k_ref, cos_ref, sin_ref, gq_ref, gk_ref, dqn_ref, dkn_ref, dq_ref, dk_ref, dgq_ref, dgk_ref):
        @pl.when(pl.program_id(0) == 0)
        def _():
            dgq_ref[...] = jnp.zeros_like(dgq_ref)
            dgk_ref[...] = jnp.zeros_like(dgk_ref)

        cosv, sinv = cos_ref[...], sin_ref[...]
        for j in range(SWA_Q_WIDTH // LANES):
            sl = slice(j * LANES, (j + 1) * LANES)
            dx, dg = bwd(q_ref[:, sl].astype(F32), gq_ref[...], cosv, sinv, dqn_ref[:, sl])
            dq_ref[:, sl] = dx.astype(BF16)
            dgq_ref[:, sl] += dg
        dx, dg = bwd(k_ref[...].astype(F32), gk_ref[...], cosv, sinv, dkn_ref[...])
        dk_ref[...] = dx.astype(BF16)
        dgk_ref[...] += dg

    row = lambda i: (i, 0)
    fixed = lambda i: (0, 0)
    return pl.pallas_call(
        body, name=name, grid=(s // tm,),
        out_shape=(jax.ShapeDtypeStruct((s, SWA_Q_WIDTH), BF16), jax.ShapeDtypeStruct((s, LANES), BF16),
                   jax.ShapeDtypeStruct((1, SWA_Q_WIDTH), F32), jax.ShapeDtypeStruct((1, LANES), F32)),
        in_specs=[pl.BlockSpec((tm, SWA_Q_WIDTH), lambda i: (i, q_blk)), pl.BlockSpec((tm, LANES), lambda i: (i, k_blk)),
                  pl.BlockSpec((tm, LANES), row), pl.BlockSpec((tm, LANES), row),
                  pl.BlockSpec((1, LANES), fixed), pl.BlockSpec((1, LANES), fixed),
                  pl.BlockSpec((tm, SWA_Q_WIDTH), row), pl.BlockSpec((tm, LANES), row)],
        out_specs=(pl.BlockSpec((tm, SWA_Q_WIDTH), row), pl.BlockSpec((tm, LANES), row),
                   pl.BlockSpec((1, SWA_Q_WIDTH), fixed), pl.BlockSpec((1, LANES), fixed)),
        compiler_params=_params("arbitrary"),
    )(proj, proj, cos_p, sin_p, gq, gk, dqn, dkn)


def _swa_tile(i, q_ref, k_ref, v_ref, second_kv):
    q0 = pl.multiple_of(i * SWA_TQ, SWA_TQ)
    k0 = pl.multiple_of(jnp.maximum(i - 1, 0) * SWA_TQ, SWA_TQ)
    keep = jnp.logical_xor(_lane_lo(), second_kv)
    kf = k_ref[pl.ds(k0, SWA_TK), :].astype(F32)
    vf = v_ref[pl.ds(k0, SWA_TK), :].astype(F32)
    kg = jnp.where(keep, kf, _swap_halves(kf)).astype(BF16)
    vg = jnp.where(keep, vf, _swap_halves(vf)).astype(BF16)
    tpos = q0 + lax.broadcasted_iota(jnp.int32, (SWA_TQ, SWA_TK), 0)
    spos = k0 + lax.broadcasted_iota(jnp.int32, (SWA_TQ, SWA_TK), 1)
    valid = jnp.logical_and(spos <= tpos, spos > tpos - WINDOW)
    return q0, k0, kg, vg, valid


def _swa_probs(qh, kg, valid, sink):
    z = jnp.where(valid, _dot_nt(qh, kg) * SCALE, NEG)
    m = jnp.maximum(jnp.max(z, axis=1, keepdims=True), sink)
    pexp = jnp.exp(z - m)
    psink = jnp.exp(sink - m)
    inv = 1.0 / (jnp.sum(pexp, axis=1, keepdims=True) + psink)
    return pexp * inv, psink * inv


def _head_sink(sink_row, hm):
    return jnp.sum(jnp.where(hm, sink_row, 0.0), axis=1, keepdims=True) * (1.0 / HEAD_DIM)


def swa_attn_fwd(qn, kn, proj, sink_p, *, name):
    s = qn.shape[0]
    nq = s // SWA_TQ
    v_blk = (3 * SB_WIDTH + SWA_Q_WIDTH + SWA_KV_WIDTH) // LANES

    def body(q_ref, k_ref, v_ref, s_ref, o_ref):
        second_kv = (pl.program_id(0) // 2) == 1
        sink_row = s_ref[...]

        def tile(i, carry):
            q0, _, kg, vg, valid = _swa_tile(i, q_ref, k_ref, v_ref, second_kv)
            q = q_ref[pl.ds(q0, SWA_TQ), :]
            out = jnp.zeros((SWA_TQ, LANES), F32)
            for h in range(2):
                hm = _head_mask(h)
                probs, _ = _swa_probs(jnp.where(hm, q, jnp.zeros_like(q)), kg, valid, _head_sink(sink_row, hm))
                out = out + jnp.where(hm, _dot(probs.astype(BF16), vg), 0.0)
            o_ref[pl.ds(q0, SWA_TQ), :] = out
            return carry

        lax.fori_loop(0, nq, tile, 0)

    pair = pl.BlockSpec((s, LANES), lambda p: (0, p))
    whole = pl.BlockSpec((s, LANES), lambda p: (0, 0))
    return pl.pallas_call(
        body, name=name, grid=(SWA_Q_WIDTH // LANES,), out_shape=jax.ShapeDtypeStruct((s, SWA_Q_WIDTH), F32),
        in_specs=[pair, whole, pl.BlockSpec((s, LANES), lambda p: (0, v_blk)),
                  pl.BlockSpec((None, 1, LANES), lambda p: (p, 0, 0))],
        out_specs=pair, compiler_params=_params("parallel"),
    )(qn, kn, proj, sink_p)


def swa_attn_bwd(qn, kn, proj, sink_p, o, do, *, name):
    s = qn.shape[0]
    nq = s // SWA_TQ
    v_blk = (3 * SB_WIDTH + SWA_Q_WIDTH + SWA_KV_WIDTH) // LANES
    fold_rows = min(512, s)

    def body(q_ref, k_ref, v_ref, s_ref, o_ref, do_ref, dq_ref, dk_ref, dv_ref, ds_ref, acc_k, acc_v):
        p = pl.program_id(0)
        second_kv = (p // 2) == 1
        sink_row = s_ref[...]

        @pl.when(p % 2 == 0)
        def _():
            acc_k[...] = jnp.zeros_like(acc_k)
            acc_v[...] = jnp.zeros_like(acc_v)

        def tile(i, dsink):
            q0, k0, kg, vg, valid = _swa_tile(i, q_ref, k_ref, v_ref, second_kv)
            q = q_ref[pl.ds(q0, SWA_TQ), :]
            dov = do_ref[pl.ds(q0, SWA_TQ), :]
            ov = o_ref[pl.ds(q0, SWA_TQ), :]
            out = jnp.zeros((SWA_TQ, LANES), F32)
            for h in range(2):
                hm = _head_mask(h)
                qh = jnp.where(hm, q, jnp.zeros_like(q))
                doh = jnp.where(hm, dov, 0.0)
                doh_b = doh.astype(BF16)
                delta = jnp.sum(doh * ov, axis=1, keepdims=True)
                probs, psink = _swa_probs(qh, kg, valid, _head_sink(sink_row, hm))
                dz = probs * (_dot_nt(doh_b, vg) - delta)
                dzb = (dz * SCALE).astype(BF16)
                out = out + jnp.where(hm, _dot(dzb, kg), 0.0)
                acc_k[pl.ds(k0, SWA_TK), :] += _dot_tn(dzb, qh)
                acc_v[pl.ds(k0, SWA_TK), :] += _dot_tn(probs.astype(BF16), doh_b)
                dsink = dsink - jnp.where(hm, jnp.sum(psink * delta, axis=0, keepdims=True), 0.0)
            dq_ref[pl.ds(q0, SWA_TQ), :] = out
            return dsink

        ds_ref[...] = lax.fori_loop(0, nq, tile, jnp.zeros((1, LANES), F32))

        def fold_into(first_head):
            def fold(r, carry):
                rows = pl.ds(pl.multiple_of(r * fold_rows, fold_rows), fold_rows)
                for acc, out in ((acc_k, dk_ref), (acc_v, dv_ref)):
                    a = acc[rows, :]
                    both = a + _swap_halves(a)
                    if first_head:
                        out[rows, :] = jnp.where(_lane_lo(), both, 0.0)
                    else:
                        out[rows, :] = jnp.where(_lane_lo(), out[rows, :], both)
                return carry

            lax.fori_loop(0, s // fold_rows, fold, 0)

        @pl.when(p == 1)
        def _():
            fold_into(True)

        @pl.when(p == 3)
        def _():
            fold_into(False)

    pair = pl.BlockSpec((s, LANES), lambda p: (0, p))
    whole = pl.BlockSpec((s, LANES), lambda p: (0, 0))
    return pl.pallas_call(
        body, name=name, grid=(SWA_Q_WIDTH // LANES,),
        out_shape=(jax.ShapeDtypeStruct((s, SWA_Q_WIDTH), F32), jax.ShapeDtypeStruct((s, LANES), F32),
                   jax.ShapeDtypeStruct((s, LANES), F32), jax.ShapeDtypeStruct((SWA_Q_WIDTH // LANES, 1, LANES), F32)),
        in_specs=[pair, whole, pl.BlockSpec((s, LANES), lambda p: (0, v_blk)),
                  pl.BlockSpec((None, 1, LANES), lambda p: (p, 0, 0)), pair, pair],
        out_specs=(pair, whole, whole, pl.BlockSpec((None, 1, LANES), lambda p: (p, 0, 0))),
        scratch_shapes=[pltpu.VMEM((s, LANES), F32), pltpu.VMEM((s, LANES), F32)],
        compiler_params=_params("arbitrary"),
    )(qn, kn, proj, sink_p, o, do)


def _rope_tables(s):
    inv_freq = 1.0 / (ROPE_THETA ** (jnp.arange(0, HEAD_DIM, 2, dtype=F32) / HEAD_DIM))
    ang = jnp.arange(s, dtype=F32)[:, None] * inv_freq[None, :]
    cos, sin = jnp.cos(ang), jnp.sin(ang)
    cos_p = jnp.tile(jnp.concatenate([cos, cos], axis=1), (1, LANES // HEAD_DIM))
    sin_p = jnp.tile(jnp.concatenate([-sin, sin], axis=1), (1, LANES // HEAD_DIM))
    return cos_p, sin_p


def _lane_tile(v, reps):
    return jnp.tile(v.reshape(1, -1), (1, reps))


def local_fwd_bwd(x, target, small, mats):
    s = x.shape[0]
    cos_p, sin_p = _rope_tables(s)
    tri = (jnp.arange(min(SB_TILE, s))[:, None] > jnp.arange(min(SB_TILE, s))[None, :]).astype(BF16)
    saved = []
    for l in range(DEPTH):
        g_mix = small["mix_norm_g"][l].reshape(1, D_MODEL)
        g_mlp = small["mlp_norm_g"][l].reshape(1, D_MODEL)
        gq = _lane_tile(small["q_norm_g"][l], LANES // HEAD_DIM)
        gk = _lane_tile(small["k_norm_g"][l], LANES // HEAD_DIM)
        sink_p = jnp.repeat(small["sinks"][l].reshape(SWA_Q_WIDTH // LANES, 2), HEAD_DIM, axis=1)
        sink_p = sink_p.reshape(SWA_Q_WIDTH // LANES, 1, LANES)
        h, proj, gates = norm_matmul(x, g_mix, mats["w_in"][l], gate_split=ATTN_WIDTH, name="in_proj")
        o_sb = sb_attn_fwd(proj, tri, name="sb_fwd")
        qn, kn = swa_prep_fwd(proj, cos_p, sin_p, gq, gk, name="swa_prep")
        o_sw = swa_attn_fwd(qn, kn, proj, sink_p, name="swa_fwd")
        x1, y_sb, y_sw, merged = merge_out_fwd(x, o_sb, o_sw, gates, mats["w_branch_sb"][l], mats["w_branch_swa"][l],
                                               mats["w_out"][l], name="merge_out")
        h2, u = norm_matmul(x1, g_mlp, mats["w_up"][l], gate_split=None, name="mlp_up")
        x2 = mlp_down_fwd(x1, u, mats["w_down"][l], name="mlp_down")
        saved.append(dict(x=x, h=h, proj=proj, gates=gates, o_sb=o_sb, qn=qn, kn=kn, o_sw=o_sw, y_sb=y_sb, y_sw=y_sw,
                          merged=merged, x1=x1, h2=h2, u=u, g_mix=g_mix, g_mlp=g_mlp, gq=gq, gk=gk, sink_p=sink_p))
        x = x2

    dx, loss = loss_head(x, target, name="loss_head")

    grads = {n: [None] * DEPTH for n in MATRIX_NAMES + SMALL_NAMES}
    for l in reversed(range(DEPTH)):
        a = saved[l]
        du = mlp_bwd_up(dx, a["u"], mats["w_down"][l], name="mlp_bwd_up")
        grads["w_down"][l] = matmul_tn(a["u"], dx, relu2=True, name="dw_down")
        grads["w_up"][l] = matmul_tn(a["h2"], du, relu2=False, name="dw_up")
        dx1, dg_mlp = matmul_nt_norm_bwd(du, mats["w_up"][l], a["x1"], a["g_mlp"], dx, name="mlp_bwd_norm")
        grads["mlp_norm_g"][l] = dg_mlp.reshape(D_MODEL)
        grads["w_out"][l] = matmul_tn(a["merged"], dx1, relu2=False, name="dw_out")
        dy_sb, dy_sw, do_sb, do_sw, dgl = out_bwd(dx1, mats["w_out"][l], a["gates"], a["y_sb"], a["y_sw"],
                                                  mats["w_branch_sb"][l], mats["w_branch_swa"][l], name="out_bwd")
        grads["w_branch_sb"][l] = matmul_tn(a["o_sb"], dy_sb, relu2=False, name="dw_branch_sb")
        grads["w_branch_swa"][l] = matmul_tn(a["o_sw"], dy_sw, relu2=False, name="dw_branch_swa")
        dq_sb, dk_sb, dv_sb = sb_attn_bwd(a["proj"], tri, a["o_sb"], do_sb, name="sb_bwd")
        dqn, dkn, dv_sw, dsink = swa_attn_bwd(a["qn"], a["kn"], a["proj"], a["sink_p"], a["o_sw"], do_sw, name="swa_bwd")
        dq_sw, dk_sw, dgq, dgk = swa_prep_bwd(a["proj"], cos_p, sin_p, a["gq"], a["gk"], dqn, dkn, name="swa_prep_bwd")
        grads["q_norm_g"][l] = dgq.reshape(SWA_Q_WIDTH // HEAD_DIM, HEAD_DIM).sum(0)
        grads["k_norm_g"][l] = dgk.reshape(LANES // HEAD_DIM, HEAD_DIM).sum(0)
        grads["sinks"][l] = dsink[:, 0, ::HEAD_DIM].reshape(SWA_Q_WIDTH // HEAD_DIM)
        dproj = jnp.concatenate([dq_sb, dk_sb, dv_sb, dq_sw, dk_sw, dv_sw.astype(BF16), dgl], axis=1)
        grads["w_in"][l] = matmul_tn(a["h"], dproj, relu2=False, name="dw_in")
        dx, dg_mix = matmul_nt_norm_bwd(dproj, mats["w_in"][l], a["x"], a["g_mix"], dx1, name="in_proj_bwd")
        grads["mix_norm_g"][l] = dg_mix.reshape(D_MODEL)
    return loss, dx, {n: jnp.stack(v) for n, v in grads.items()}


def _pack_rows(shard):
    return shard.reshape(-1, LANES)


def pack_shards(shards, dtype):
    return jnp.concatenate([_pack_rows(shards[n]).astype(dtype) for n in MATRIX_NAMES], axis=0)


def _shard_shape(name, full_shape):
    _, r, c = full_shape
    return (DEPTH, r // N_DEV, c) if name in ROW_SHARDED else (DEPTH, r, c // N_DEV)


def unpack_gathered(gathered, full_shapes):
    out, off = {}, 0
    for n in MATRIX_NAMES:
        _, r, c = _shard_shape(n, full_shapes[n])
        rows = DEPTH * r * c // LANES
        seg = gathered[:, off:off + rows, :].reshape(N_DEV, DEPTH, r, c)
        off += rows
        if n in ROW_SHARDED:
            out[n] = [seg[:, l].reshape(N_DEV * r, c) for l in range(DEPTH)]
        else:
            out[n] = [jnp.transpose(seg[:, l], (1, 0, 2)).reshape(r, N_DEV * c) for l in range(DEPTH)]
    return out


def pack_full_grads(grads, full_shapes):
    parts = []
    for n in MATRIX_NAMES:
        _, r, c = _shard_shape(n, full_shapes[n])
        g = grads[n]
        if n in ROW_SHARDED:
            g = jnp.transpose(g.reshape(DEPTH, N_DEV, r, c), (1, 0, 2, 3))
        else:
            g = jnp.transpose(g.reshape(DEPTH, r, N_DEV, c), (2, 0, 1, 3))
        parts.append(g.astype(BF16).reshape(N_DEV, -1, LANES))
    return jnp.concatenate(parts, axis=1)


def unpack_shards(packed, full_shapes):
    out, off = {}, 0
    for n in MATRIX_NAMES:
        shape = _shard_shape(n, full_shapes[n])
        rows = shape[0] * shape[1] * shape[2] // LANES
        out[n] = packed[off:off + rows].reshape(shape)
        off += rows
    return out


ANY = pl.BlockSpec(memory_space=pl.ANY)
MESH = pl.DeviceIdType.MESH


def _place():
    return lax.axis_index("x"), lax.axis_index("y"), lax.axis_index("c")


def all_gather_blocks(block, *, name):
    def body(x_ref, out_ref, send_sems, recv_sems, local_sem):
        x, y, c = _place()
        me, sibling = (x, y, c), (x, y, 1 - c)
        chips = [(1 - x, y), (x, 1 - y), (1 - x, 1 - y)]

        def slot(px, py, pc):
            return out_ref.at[4 * px + 2 * py + pc]

        def copy(k, blk, to, src=None):
            return pltpu.make_async_remote_copy(
                src_ref=slot(*blk) if src is None else src, dst_ref=slot(*blk),
                send_sem=send_sems.at[k], recv_sem=recv_sems.at[k], device_id=to, device_id_type=MESH)

        mine = pltpu.make_async_copy(x_ref, slot(*me), local_sem)
        mine.start()
        first = [copy(0, me, sibling, src=x_ref)]
        first += [copy(1 + j, me, (*chip, c), src=x_ref) for j, chip in enumerate(chips)]
        for cp in first:
            cp.start()
        passed = [copy(4 + j, (*chip, c), sibling) for j, chip in enumerate(chips)]
        for j, chip in enumerate(chips):
            copy(1 + j, (*chip, c), me).wait_recv()
            passed[j].start()
        copy(0, sibling, me).wait_recv()
        for j, chip in enumerate(chips):
            copy(4 + j, (*chip, 1 - c), me).wait_recv()
        for cp in first + passed:
            cp.wait_send()
        mine.wait()

    return pl.pallas_call(
        body, name=name, out_shape=jax.ShapeDtypeStruct((N_DEV,) + block.shape, block.dtype),
        in_specs=[ANY], out_specs=ANY,
        scratch_shapes=[pltpu.SemaphoreType.DMA((7,)), pltpu.SemaphoreType.DMA((7,)), pltpu.SemaphoreType.DMA],
        compiler_params=pltpu.CompilerParams(has_side_effects=True),
    )(block)


def sibling_exchange(packed, *, name):
    def body(p_ref, land_ref, send_sems, recv_sems):
        x, y, c = _place()
        copies = [pltpu.make_async_remote_copy(
            src_ref=p_ref.at[2 * k + (1 - c)], dst_ref=land_ref.at[k], send_sem=send_sems.at[k],
            recv_sem=recv_sems.at[k], device_id=(x, y, 1 - c), device_id_type=MESH) for k in range(4)]
        for cp in copies:
            cp.start()
        for cp in copies:
            cp.wait_recv()
        for cp in copies:
            cp.wait_send()

    return pl.pallas_call(
        body, name=name, out_shape=jax.ShapeDtypeStruct((4,) + packed.shape[1:], packed.dtype),
        in_specs=[ANY], out_specs=ANY,
        scratch_shapes=[pltpu.SemaphoreType.DMA((4,)), pltpu.SemaphoreType.DMA((4,))],
        compiler_params=pltpu.CompilerParams(has_side_effects=True),
    )(packed)


def pair_sum(packed, landed, core, *, name):
    _, r, _ = packed.shape
    tr = PACK_TILE

    def body(c_ref, a_ref, b_ref, o_ref):
        o_ref[...] = (a_ref[...].astype(F32) + b_ref[...].astype(F32)).astype(BF16)

    return pl.pallas_call(
        body, name=name, out_shape=jax.ShapeDtypeStruct((4, r, LANES), BF16),
        grid_spec=pltpu.PrefetchScalarGridSpec(
            num_scalar_prefetch=1, grid=(4, r // tr),
            in_specs=[pl.BlockSpec((None, tr, LANES), lambda k, i, c_ref: (2 * k + c_ref[0], i, 0)),
                      pl.BlockSpec((None, tr, LANES), lambda k, i, c_ref: (k, i, 0))],
            out_specs=pl.BlockSpec((None, tr, LANES), lambda k, i, c_ref: (k, i, 0))),
        compiler_params=_params("parallel", "parallel"),
    )(core, packed, landed)


def chip_exchange(partial, *, name):
    def body(p_ref, land_ref, send_sems, recv_sems):
        x, y, c = _place()
        chips = [(1 - x, y), (x, 1 - y), (1 - x, 1 - y)]
        copies = [pltpu.make_async_remote_copy(
            src_ref=p_ref.at[2 * px + py], dst_ref=land_ref.at[j], send_sem=send_sems.at[j],
            recv_sem=recv_sems.at[j], device_id=(px, py, c), device_id_type=MESH) for j, (px, py) in enumerate(chips)]
        for cp in copies:
            cp.start()
        for cp in copies:
            cp.wait_recv()
        for cp in copies:
            cp.wait_send()

    return pl.pallas_call(
        body, name=name, out_shape=jax.ShapeDtypeStruct((3,) + partial.shape[1:], partial.dtype),
        in_specs=[ANY], out_specs=ANY,
        scratch_shapes=[pltpu.SemaphoreType.DMA((3,)), pltpu.SemaphoreType.DMA((3,))],
        compiler_params=pltpu.CompilerParams(has_side_effects=True),
    )(partial)


def _adamw(w, g, m, v):
    m = ADAM_B1 * m + (1.0 - ADAM_B1) * g
    v = ADAM_B2 * v + (1.0 - ADAM_B2) * (g * g)
    m_hat = m / (1.0 - ADAM_B1 ** ADAM_STEP)
    v_hat = v / (1.0 - ADAM_B2 ** ADAM_STEP)
    delta = -ADAM_LR * (m_hat / (jnp.sqrt(v_hat) + ADAM_EPS) + ADAM_WD * w)
    return delta, m, v


def reduce_adamw(partial, landed, chip, w, m, v, *, name):
    r = w.shape[0]
    tr = PACK_TILE

    def body(k_ref, own_ref, l0_ref, l1_ref, l2_ref, w_ref, m_ref, v_ref, g_out, d_out, m_out, v_out):
        g = own_ref[...].astype(F32) + l0_ref[...].astype(F32) + l1_ref[...].astype(F32) + l2_ref[...].astype(F32)
        delta, m_new, v_new = _adamw(w_ref[...], g, m_ref[...], v_ref[...])
        g_out[...] = g
        d_out[...] = delta
        m_out[...] = m_new
        v_out[...] = v_new

    row = pl.BlockSpec((tr, LANES), lambda i, k_ref: (i, 0))

    def slot(j):
        return pl.BlockSpec((None, tr, LANES), lambda i, k_ref: (j, i, 0))

    return pl.pallas_call(
        body, name=name, out_shape=(jax.ShapeDtypeStruct((r, LANES), F32),) * 4,
        grid_spec=pltpu.PrefetchScalarGridSpec(
            num_scalar_prefetch=1, grid=(r // tr,),
            in_specs=[pl.BlockSpec((None, tr, LANES), lambda i, k_ref: (k_ref[0], i, 0)), slot(0), slot(1), slot(2),
                      row, row, row],
            out_specs=(row, row, row, row)),
        compiler_params=_params("parallel"),
    )(chip, partial, landed, landed, landed, w, m, v)


def small_adamw(gathered, w, m, v, *, name):
    def body(g_ref, w_ref, m_ref, v_ref, g_out, d_out, m_out, v_out):
        g = g_ref[0]
        for d in range(1, N_DEV):
            g = g + g_ref[d]
        delta, m_new, v_new = _adamw(w_ref[...], g, m_ref[...], v_ref[...])
        g_out[...] = g
        d_out[...] = delta
        m_out[...] = m_new
        v_out[...] = v_new

    return pl.pallas_call(
        body, name=name, out_shape=(jax.ShapeDtypeStruct(w.shape, F32),) * 4,
    )(gathered, w, m, v)


def _pack_small(tree):
    flat = jnp.concatenate([tree[n].reshape(-1) for n in SMALL_NAMES])
    rows = -(-flat.shape[0] // (8 * LANES)) * 8
    return jnp.pad(flat, (0, rows * LANES - flat.shape[0])).reshape(rows, LANES)


def _unpack_small(packed, shapes):
    flat, out, off = packed.reshape(-1), {}, 0
    for n in SMALL_NAMES:
        size = shapes[n][0] * shapes[n][1]
        out[n] = flat[off:off + size].reshape(shapes[n])
        off += size
    return out


def kernel(x, mix_norm_g, w_in, q_norm_g, k_norm_g, sinks, w_branch_sb, w_branch_swa, w_out, mlp_norm_g, w_up, w_down, loss_target, m_mix_norm_g, m_w_in, m_q_norm_g, m_k_norm_g, m_sinks, m_w_branch_sb, m_w_branch_swa, m_w_out, m_mlp_norm_g, m_w_up, m_w_down, v_mix_norm_g, v_w_in, v_q_norm_g, v_k_norm_g, v_sinks, v_w_branch_sb, v_w_branch_swa, v_w_out, v_mlp_norm_g, v_w_up, v_w_down):
    weights = dict(mix_norm_g=mix_norm_g, w_in=w_in, q_norm_g=q_norm_g, k_norm_g=k_norm_g, sinks=sinks,
                   w_branch_sb=w_branch_sb, w_branch_swa=w_branch_swa, w_out=w_out, mlp_norm_g=mlp_norm_g, w_up=w_up,
                   w_down=w_down)
    mom_m = dict(mix_norm_g=m_mix_norm_g, w_in=m_w_in, q_norm_g=m_q_norm_g, k_norm_g=m_k_norm_g, sinks=m_sinks,
                 w_branch_sb=m_w_branch_sb, w_branch_swa=m_w_branch_swa, w_out=m_w_out, mlp_norm_g=m_mlp_norm_g,
                 w_up=m_w_up, w_down=m_w_down)
    mom_v = dict(mix_norm_g=v_mix_norm_g, w_in=v_w_in, q_norm_g=v_q_norm_g, k_norm_g=v_k_norm_g, sinks=v_sinks,
                 w_branch_sb=v_w_branch_sb, w_branch_swa=v_w_branch_swa, w_out=v_w_out, mlp_norm_g=v_mlp_norm_g,
                 w_up=v_w_up, w_down=v_w_down)
    full_shapes = {}
    for n in MATRIX_NAMES:
        _, r, c = weights[n].shape
        full_shapes[n] = (DEPTH, r * N_DEV, c) if n in ROW_SHARDED else (DEPTH, r, c * N_DEV)
    small_shapes = {n: weights[n].shape for n in SMALL_NAMES}

    gathered = all_gather_blocks(pack_shards(weights, BF16), name="gather_weights")
    mats = unpack_gathered(gathered, full_shapes)
    small = {n: weights[n] for n in SMALL_NAMES}
    loss_part, grad_x, grads = local_fwd_bwd(x[0], loss_target[0], small, mats)
    loss = lax.psum(loss_part[0, 0], MESH_AXES)

    core = lax.axis_index("c").astype(jnp.int32).reshape(1)
    chip = (2 * lax.axis_index("x") + lax.axis_index("y")).astype(jnp.int32).reshape(1)
    packed = pack_full_grads(grads, full_shapes)
    landed_pair = sibling_exchange(packed, name="grad_pair_exchange")
    partial = pair_sum(packed, landed_pair, core, name="grad_pair_sum")
    landed_chip = chip_exchange(partial, name="grad_chip_exchange")
    g_p, d_p, m_p, v_p = reduce_adamw(partial, landed_chip, chip, pack_shards(weights, F32), pack_shards(mom_m, F32),
                                      pack_shards(mom_v, F32), name="grad_sum_adamw")
    out_g, out_d = unpack_shards(g_p, full_shapes), unpack_shards(d_p, full_shapes)
    out_m, out_v = unpack_shards(m_p, full_shapes), unpack_shards(v_p, full_shapes)

    small_all = all_gather_blocks(_pack_small(grads), name="gather_small_grads")
    sg, sd, sm, sv = small_adamw(small_all, _pack_small(weights), _pack_small(mom_m), _pack_small(mom_v),
                                 name="small_adamw")
    for tree, packed_small in ((out_g, sg), (out_d, sd), (out_m, sm), (out_v, sv)):
        tree.update(_unpack_small(packed_small, small_shapes))

    return (loss, grad_x[None], *[out_g[n] for n in WEIGHT_ORDER], *[out_d[n] for n in WEIGHT_ORDER],
            *[out_m[n] for n in WEIGHT_ORDER], *[out_v[n] for n in WEIGHT_ORDER])
```

```python
import functools

import jax
import jax.numpy as jnp
from jax import lax
from jax.experimental import pallas as pl
from jax.experimental.pallas import tpu as pltpu

F32 = jnp.float32
BF16 = jnp.bfloat16

DEPTH = 4
D_MODEL = 1024
HEAD_DIM = 64
LANES = 128
WINDOW = 128
SB_WIDTH = 512
SWA_Q_WIDTH = 512
SWA_KV_WIDTH = 128
ATTN_WIDTH = 3 * SB_WIDTH + SWA_Q_WIDTH + 2 * SWA_KV_WIDTH
IN_WIDTH = ATTN_WIDTH + 2 * D_MODEL
ROPE_THETA = 10000.0
NORM_EPS = 1e-6
SCALE = HEAD_DIM ** -0.5
NEG = -1e30
N_DEV = 8
N_CHIPS = 4

ADAM_LR = 0.001
ADAM_B1 = 0.9
ADAM_B2 = 0.999
ADAM_EPS = 1e-08
ADAM_WD = 0.01
ADAM_STEP = 10

SB_TILE = 256
SB_CUTOFF = -104.0
SWA_TQ = 128
SWA_TK = 256
ROW_TILE = 256
VMEM_LIMIT = 56 * 1024 * 1024

MATRIX_NAMES = ("w_in", "w_branch_sb", "w_branch_swa", "w_out", "w_up", "w_down")
ROW_SHARDED = ("w_out", "w_down")
SMALL_NAMES = ("mix_norm_g", "q_norm_g", "k_norm_g", "sinks", "mlp_norm_g")
WEIGHT_ORDER = ("mix_norm_g", "w_in", "q_norm_g", "k_norm_g", "sinks", "w_branch_sb", "w_branch_swa", "w_out",
                "mlp_norm_g", "w_up", "w_down")
MESH_AXES = ("x", "y", "c")
N_MATS = len(MATRIX_NAMES)

ANY = pl.BlockSpec(memory_space=pl.ANY)
MESH = pl.DeviceIdType.MESH


def _params(*sem):
    return pltpu.CompilerParams(dimension_semantics=sem, vmem_limit_bytes=VMEM_LIMIT)


def _dot(a, b):
    return jnp.dot(a, b, preferred_element_type=F32)


def _dot_nt(a, b):
    return lax.dot_general(a, b, (((1,), (1,)), ((), ())), preferred_element_type=F32)


def _dot_tn(a, b):
    return lax.dot_general(a, b, (((0,), (0,)), ((), ())), preferred_element_type=F32)


def _split_bf16(x):
    hi = x.astype(BF16)
    lo = (x - hi.astype(F32)).astype(BF16)
    return hi, lo


def _dot2(x, b):
    hi, lo = _split_bf16(x)
    return _dot(hi, b) + _dot(lo, b)


def _dot2_nt(x, b):
    hi, lo = _split_bf16(x)
    return _dot_nt(hi, b) + _dot_nt(lo, b)


def _rsqrt_ms(x):
    return lax.rsqrt(jnp.mean(x * x, axis=-1, keepdims=True) + NORM_EPS)


def _place():
    return lax.axis_index("x"), lax.axis_index("y"), lax.axis_index("c")


GATHER_SCRATCH = [pltpu.SemaphoreType.DMA((7, N_MATS)), pltpu.SemaphoreType.DMA((7, N_MATS)),
                  pltpu.SemaphoreType.DMA((N_MATS,))]


class _Gather:
    def __init__(self, x_refs, out_refs, send_sems, recv_sems, local_sems):
        self.x_refs, self.out_refs = x_refs, out_refs
        self.send_sems, self.recv_sems, self.local_sems = send_sems, recv_sems, local_sems
        x, y, c = _place()
        self.c = c
        self.me, self.sibling = (x, y, c), (x, y, 1 - c)
        self.chips = [(1 - x, y), (x, 1 - y), (1 - x, 1 - y)]

    def _copy(self, k, w, blk, to, own=False):
        dst = self.out_refs[w].at[4 * blk[0] + 2 * blk[1] + blk[2]]
        return pltpu.make_async_remote_copy(
            src_ref=self.x_refs[w] if own else dst, dst_ref=dst, send_sem=self.send_sems.at[k, w],
            recv_sem=self.recv_sems.at[k, w], device_id=to, device_id_type=MESH)

    def _mine(self, w):
        me = self.me
        return pltpu.make_async_copy(self.x_refs[w], self.out_refs[w].at[4 * me[0] + 2 * me[1] + me[2]],
                                     self.local_sems.at[w])

    def _first(self, w):
        return [self._copy(0, w, self.me, self.sibling, own=True)] + [
            self._copy(1 + j, w, self.me, (*chip, self.c), own=True) for j, chip in enumerate(self.chips)]

    def _passed(self, j, w):
        return self._copy(4 + j, w, (*self.chips[j], self.c), self.sibling)

    def start(self):
        for w in range(N_MATS):
            self._mine(w).start()
            for cp in self._first(w):
                cp.start()

    def forward(self):
        for j, chip in enumerate(self.chips):
            for w in range(N_MATS):
                self._copy(1 + j, w, (*chip, self.c), self.me).wait_recv()
                self._passed(j, w).start()

    def finish(self):
        for w in range(N_MATS):
            self._copy(0, w, self.sibling, self.me).wait_recv()
            for j, chip in enumerate(self.chips):
                self._copy(4 + j, w, (*chip, 1 - self.c), self.me).wait_recv()
            for cp in self._first(w):
                cp.wait_send()
            for j in range(3):
                self._passed(j, w).wait_send()
            self._mine(w).wait()


def _gather_out_shapes(shards):
    return tuple(jax.ShapeDtypeStruct((N_DEV,) + s.shape, s.dtype) for s in shards)


def gather_layer(shards, *, name):
    def body(*refs):
        g = _Gather(refs[:N_MATS], refs[N_MATS:2 * N_MATS], *refs[2 * N_MATS:])
        g.start()
        g.forward()
        g.finish()

    return pl.pallas_call(
        body, name=name, out_shape=_gather_out_shapes(shards), in_specs=[ANY] * N_MATS, out_specs=(ANY,) * N_MATS,
        scratch_shapes=GATHER_SCRATCH, compiler_params=pltpu.CompilerParams(has_side_effects=True),
    )(*shards)


CHIP_SCRATCH = [pltpu.SemaphoreType.DMA((3, N_MATS)), pltpu.SemaphoreType.DMA((3, N_MATS))]


class _ChipExchange:
    def __init__(self, layer, part_refs, land_refs, send_sems, recv_sems):
        x, y, c = _place()
        chips = [(1 - x, y), (x, 1 - y), (1 - x, 1 - y)]
        self.copies = [pltpu.make_async_remote_copy(
            src_ref=part_refs[w].at[layer, 2 * px + py], dst_ref=land_refs[w].at[layer, j],
            send_sem=send_sems.at[j, w], recv_sem=recv_sems.at[j, w], device_id=(px, py, c), device_id_type=MESH)
            for w in range(N_MATS) for j, (px, py) in enumerate(chips)]

    def start(self):
        for cp in self.copies:
            cp.start()

    def finish(self):
        for cp in self.copies:
            cp.wait_recv()
        for cp in self.copies:
            cp.wait_send()


def chip_exchange(layer, parts, lands, *, name):
    def body(*refs):
        ex = _ChipExchange(layer, refs[:N_MATS], refs[2 * N_MATS:3 * N_MATS], *refs[3 * N_MATS:])
        ex.start()
        ex.finish()

    return pl.pallas_call(
        body, name=name, out_shape=tuple(jax.ShapeDtypeStruct(a.shape, a.dtype) for a in lands),
        in_specs=[ANY] * (2 * N_MATS), out_specs=(ANY,) * N_MATS,
        input_output_aliases={N_MATS + w: w for w in range(N_MATS)}, scratch_shapes=CHIP_SCRATCH,
        compiler_params=pltpu.CompilerParams(has_side_effects=True),
    )(*parts, *lands)


def pair_exchange(grads, *, name):
    def body(*refs):
        g_refs, land_refs = refs[:N_MATS], refs[N_MATS:2 * N_MATS]
        send_sems, recv_sems = refs[2 * N_MATS:]
        x, y, c = _place()
        copies = [pltpu.make_async_remote_copy(
            src_ref=g_refs[w].at[2 * k + (1 - c)], dst_ref=land_refs[w].at[k], send_sem=send_sems.at[k, w],
            recv_sem=recv_sems.at[k, w], device_id=(x, y, 1 - c), device_id_type=MESH)
            for w in range(N_MATS) for k in range(N_CHIPS)]
        for cp in copies:
            cp.start()
        for cp in copies:
            cp.wait_recv()
        for cp in copies:
            cp.wait_send()

    return pl.pallas_call(
        body, name=name,
        out_shape=tuple(jax.ShapeDtypeStruct((N_CHIPS,) + g.shape[1:], g.dtype) for g in grads),
        in_specs=[ANY] * N_MATS, out_specs=(ANY,) * N_MATS,
        scratch_shapes=[pltpu.SemaphoreType.DMA((N_CHIPS, N_MATS)), pltpu.SemaphoreType.DMA((N_CHIPS, N_MATS))],
        compiler_params=pltpu.CompilerParams(has_side_effects=True),
    )(*grads)


PAIR_SUM_CHUNKS = 8


def pair_sum(layer, grads, landed, parts, core, *, name):
    def body(c_ref, *refs):
        g_refs, l_refs, o_refs = refs[:N_MATS], refs[N_MATS:2 * N_MATS], refs[3 * N_MATS:]
        for w in range(N_MATS):
            o_refs[w][...] = (g_refs[w][...].astype(F32) + l_refs[w][...].astype(F32)).astype(BF16)

    def blk(g):
        return (None, g.shape[1] // PAIR_SUM_CHUNKS, g.shape[2])

    in_specs = [pl.BlockSpec(blk(g), lambda k, i, c_ref: (2 * k + c_ref[0], i, 0)) for g in grads]
    in_specs += [pl.BlockSpec(blk(g), lambda k, i, c_ref: (k, i, 0)) for g in grads]
    in_specs += [ANY] * N_MATS
    out_specs = tuple(pl.BlockSpec((None,) + blk(g), lambda k, i, c_ref: (layer, k, i, 0)) for g in grads)
    return pl.pallas_call(
        body, name=name, out_shape=tuple(jax.ShapeDtypeStruct(p.shape, p.dtype) for p in parts),
        grid_spec=pltpu.PrefetchScalarGridSpec(num_scalar_prefetch=1, grid=(N_CHIPS, PAIR_SUM_CHUNKS),
                                               in_specs=in_specs, out_specs=out_specs),
        input_output_aliases={1 + 2 * N_MATS + w: w for w in range(N_MATS)},
        compiler_params=_params("parallel", "parallel"),
    )(core, *grads, *landed, *parts)


def _adamw(w, g, m, v):
    m = ADAM_B1 * m + (1.0 - ADAM_B1) * g
    v = ADAM_B2 * v + (1.0 - ADAM_B2) * (g * g)
    m_hat = m / (1.0 - ADAM_B1 ** ADAM_STEP)
    v_hat = v / (1.0 - ADAM_B2 ** ADAM_STEP)
    delta = -ADAM_LR * (m_hat / (jnp.sqrt(v_hat) + ADAM_EPS) + ADAM_WD * w)
    return delta, m, v


def reduce_adamw(part, land, chip, w, m, v, *, name):
    _, r, c = w.shape
    tr = min(r, 256)

    def body(k_ref, own_ref, l0_ref, l1_ref, l2_ref, w_ref, m_ref, v_ref, g_out, d_out, m_out, v_out):
        g = own_ref[...].astype(F32) + l0_ref[...].astype(F32) + l1_ref[...].astype(F32) + l2_ref[...].astype(F32)
        delta, m_new, v_new = _adamw(w_ref[...], g, m_ref[...], v_ref[...])
        g_out[...] = g
        d_out[...] = delta
        m_out[...] = m_new
        v_out[...] = v_new

    row = pl.BlockSpec((None, tr, c), lambda l, i, k_ref: (l, i, 0))

    def slot(j):
        return pl.BlockSpec((None, None, tr, c), lambda l, i, k_ref: (l, j, i, 0))

    return pl.pallas_call(
        body, name=name, out_shape=(jax.ShapeDtypeStruct(w.shape, F32),) * 4,
        grid_spec=pltpu.PrefetchScalarGridSpec(
            num_scalar_prefetch=1, grid=(DEPTH, r // tr),
            in_specs=[pl.BlockSpec((None, None, tr, c), lambda l, i, k_ref: (l, k_ref[0], i, 0)), slot(0), slot(1),
                      slot(2), row, row, row],
            out_specs=(row, row, row, row)),
        compiler_params=_params("parallel", "parallel"),
    )(chip, part, land, land, land, w, m, v)


def gather_small(block, *, name):
    def body(x_ref, out_ref, send_sems, recv_sems, local_sem):
        x, y, c = _place()
        me = 4 * x + 2 * y + c
        mine = pltpu.make_async_copy(x_ref, out_ref.at[me], local_sem)
        mine.start()
        peers = [(x ^ (k >> 2), y ^ ((k >> 1) & 1), c ^ (k & 1)) for k in range(1, N_DEV)]
        copies = [pltpu.make_async_remote_copy(
            src_ref=x_ref, dst_ref=out_ref.at[me], send_sem=send_sems.at[k], recv_sem=recv_sems.at[k],
            device_id=peer, device_id_type=MESH) for k, peer in enumerate(peers)]
        for cp in copies:
            cp.start()
        for k, (px, py, pc) in enumerate(peers):
            pltpu.make_async_remote_copy(
                src_ref=x_ref, dst_ref=out_ref.at[4 * px + 2 * py + pc], send_sem=send_sems.at[k],
                recv_sem=recv_sems.at[k], device_id=(px, py, pc), device_id_type=MESH).wait_recv()
        for cp in copies:
            cp.wait_send()
        mine.wait()

    return pl.pallas_call(
        body, name=name, out_shape=jax.ShapeDtypeStruct((N_DEV,) + block.shape, block.dtype),
        in_specs=[ANY], out_specs=ANY,
        scratch_shapes=[pltpu.SemaphoreType.DMA((7,)), pltpu.SemaphoreType.DMA((7,)), pltpu.SemaphoreType.DMA],
        compiler_params=pltpu.CompilerParams(has_side_effects=True),
    )(block)


def small_adamw(gathered, w, m, v, *, name):
    def body(g_ref, w_ref, m_ref, v_ref, g_out, d_out, m_out, v_out):
        g = g_ref[0]
        for d in range(1, N_DEV):
            g = g + g_ref[d]
        delta, m_new, v_new = _adamw(w_ref[...], g, m_ref[...], v_ref[...])
        g_out[...] = g
        d_out[...] = delta
        m_out[...] = m_new
        v_out[...] = v_new

    return pl.pallas_call(
        body, name=name, out_shape=(jax.ShapeDtypeStruct(w.shape, F32),) * 4,
    )(gathered, w, m, v)


def norm_matmul(x, g, w, *, gate_split, name):
    s, d = x.shape
    tm = min(ROW_TILE, s)
    blocked = w.ndim == 3
    n = w.shape[1] if not blocked else w.shape[0] * w.shape[2]

    def body(x_ref, g_ref, w_ref, h_ref, *outs):
        xv = x_ref[...]
        h = ((xv * _rsqrt_ms(xv)) * g_ref[...]).astype(BF16)
        h_ref[...] = h
        if blocked:
            nb = w_ref.shape[2]
            for j in range(w_ref.shape[0]):
                outs[0][:, j * nb:(j + 1) * nb] = _dot(h, w_ref[j]).astype(BF16)
        else:
            p = _dot(h, w_ref[...])
            outs[0][...] = p[:, :gate_split].astype(BF16)
            outs[1][...] = (1.0 / (1.0 + jnp.exp(-p[:, gate_split:]))).astype(BF16)

    row = lambda i: (i, 0)
    fixed = lambda i: (0, 0)
    if blocked:
        out_shape = (jax.ShapeDtypeStruct((s, d), BF16), jax.ShapeDtypeStruct((s, n), BF16))
        out_specs = (pl.BlockSpec((tm, d), row), pl.BlockSpec((tm, n), row))
        w_spec = pl.BlockSpec(w.shape, lambda i: (0, 0, 0))
    else:
        out_shape = (jax.ShapeDtypeStruct((s, d), BF16), jax.ShapeDtypeStruct((s, gate_split), BF16),
                     jax.ShapeDtypeStruct((s, n - gate_split), BF16))
        out_specs = (pl.BlockSpec((tm, d), row), pl.BlockSpec((tm, gate_split), row),
                     pl.BlockSpec((tm, n - gate_split), row))
        w_spec = pl.BlockSpec((d, n), fixed)
    return pl.pallas_call(
        body, name=name, grid=(s // tm,), out_shape=out_shape,
        in_specs=[pl.BlockSpec((tm, d), row), pl.BlockSpec((1, d), fixed), w_spec],
        out_specs=out_specs, compiler_params=_params("parallel"),
    )(x, g, w)


def merge_out_fwd(x, o_sb, o_sw, gates, w_bsb, w_bsw, w_o, *, name):
    s, d = x.shape
    tm = min(ROW_TILE, s)

    def body(x_ref, osb_ref, osw_ref, g_ref, wsb_ref, wsw_ref, wo_ref, x1_ref, ysb_ref, ysw_ref, mg_ref):
        y_sb = _dot(osb_ref[...].astype(BF16), wsb_ref[...])
        y_sw = _dot(osw_ref[...].astype(BF16), wsw_ref[...])
        g = g_ref[...].astype(F32)
        merged = (g[:, :d] * y_sb + g[:, d:] * y_sw).astype(BF16)
        ysb_ref[...] = y_sb.astype(BF16)
        ysw_ref[...] = y_sw.astype(BF16)
        mg_ref[...] = merged
        x1_ref[...] = x_ref[...] + _dot(merged, wo_ref[...])

    row = lambda i: (i, 0)
    fixed = lambda i: (0, 0)
    wd = o_sb.shape[1]
    return pl.pallas_call(
        body, name=name, grid=(s // tm,),
        out_shape=(jax.ShapeDtypeStruct((s, d), F32),) + (jax.ShapeDtypeStruct((s, d), BF16),) * 3,
        in_specs=[pl.BlockSpec((tm, d), row), pl.BlockSpec((tm, wd), row), pl.BlockSpec((tm, wd), row),
                  pl.BlockSpec((tm, 2 * d), row), pl.BlockSpec((wd, d), fixed), pl.BlockSpec((wd, d), fixed),
                  pl.BlockSpec((d, d), fixed)],
        out_specs=(pl.BlockSpec((tm, d), row),) * 4, compiler_params=_params("parallel"),
    )(x, o_sb, o_sw, gates, w_bsb, w_bsw, w_o)


def mlp_down_fwd(x1, u, w_down, *, name):
    s, d = x1.shape
    f = u.shape[1]
    tm = min(ROW_TILE, s)

    def body(x_ref, u_ref, w_ref, o_ref):
        a = jnp.maximum(u_ref[...].astype(F32), 0.0)
        o_ref[...] = x_ref[...] + _dot((a * a).astype(BF16), w_ref[...])

    row = lambda i: (i, 0)
    return pl.pallas_call(
        body, name=name, grid=(s // tm,), out_shape=jax.ShapeDtypeStruct((s, d), F32),
        in_specs=[pl.BlockSpec((tm, d), row), pl.BlockSpec((tm, f), row), pl.BlockSpec((f, d), lambda i: (0, 0))],
        out_specs=pl.BlockSpec((tm, d), row), compiler_params=_params("parallel"),
    )(x1, u, w_down)


def loss_head(y, target, *, name):
    s, d = y.shape
    tm = min(ROW_TILE, s)

    def body(y_ref, t_ref, dy_ref, dyb_ref, loss_ref):
        @pl.when(pl.program_id(0) == 0)
        def _():
            loss_ref[...] = jnp.zeros_like(loss_ref)

        e = y_ref[...] - t_ref[...]
        dy = e * (1.0 / d)
        dy_ref[...] = dy
        dyb_ref[...] = dy.astype(BF16)
        per_row = jnp.sum(e * e, axis=1, keepdims=True) * (0.5 / d)
        loss_ref[...] += jnp.sum(per_row, axis=0, keepdims=True)

    row = lambda i: (i, 0)
    return pl.pallas_call(
        body, name=name, grid=(s // tm,),
        out_shape=(jax.ShapeDtypeStruct((s, d), F32), jax.ShapeDtypeStruct((s, d), BF16),
                   jax.ShapeDtypeStruct((1, 1), F32)),
        in_specs=[pl.BlockSpec((tm, d), row), pl.BlockSpec((tm, d), row)],
        out_specs=(pl.BlockSpec((tm, d), row), pl.BlockSpec((tm, d), row), pl.BlockSpec((1, 1), lambda i: (0, 0))),
        compiler_params=_params("arbitrary"),
    )(y, target)


def mlp_bwd_up(dxb, u, w_down, *, name):
    s, d = dxb.shape
    f = u.shape[1]
    tm = min(ROW_TILE, s)

    def body(dx_ref, u_ref, w_ref, du_ref):
        da = _dot_nt(dx_ref[...], w_ref[...])
        du_ref[...] = (da * (2.0 * jnp.maximum(u_ref[...].astype(F32), 0.0))).astype(BF16)

    row = lambda i: (i, 0)
    return pl.pallas_call(
        body, name=name, grid=(s // tm,), out_shape=jax.ShapeDtypeStruct((s, f), BF16),
        in_specs=[pl.BlockSpec((tm, d), row), pl.BlockSpec((tm, f), row), pl.BlockSpec((f, d), lambda i: (0, 0))],
        out_specs=pl.BlockSpec((tm, f), row), compiler_params=_params("parallel"),
    )(dxb, u, w_down)


def matmul_nt_norm_bwd(pieces, w, x, g, dres, *, name):
    s = x.shape[0]
    d = x.shape[1]
    tm = min(ROW_TILE, s)
    blocked = w.ndim == 3
    n_pieces = len(pieces)
    widths = [p.shape[1] for p in pieces]

    def body(*refs):
        p_refs = refs[:n_pieces]
        w_ref, x_ref, g_ref, dres_ref, dx_ref, dxb_ref, dg_ref = refs[n_pieces:]

        @pl.when(pl.program_id(0) == 0)
        def _():
            dg_ref[...] = jnp.zeros_like(dg_ref)

        if blocked:
            nb = w_ref.shape[2]
            dh = _dot_nt(p_refs[0][:, :nb], w_ref[0])
            for j in range(1, w_ref.shape[0]):
                dh = dh + _dot_nt(p_refs[0][:, j * nb:(j + 1) * nb], w_ref[j])
        else:
            dh, off = None, 0
            for p_ref, width in zip(p_refs, widths):
                part = _dot_nt(p_ref[...], w_ref[:, off:off + width])
                dh = part if dh is None else dh + part
                off += width
        xv = x_ref[...]
        r = _rsqrt_ms(xv)
        dyg = dh * g_ref[...]
        dx = dres_ref[...] + r * dyg - xv * ((r * r * r) * jnp.mean(dyg * xv, axis=-1, keepdims=True))
        dx_ref[...] = dx
        dxb_ref[...] = dx.astype(BF16)
        dg_ref[...] += jnp.sum(dh * (xv * r), axis=0, keepdims=True)

    row = lambda i: (i, 0)
    fixed = lambda i: (0, 0)
    w_spec = pl.BlockSpec(w.shape, (lambda i: (0, 0, 0)) if blocked else fixed)
    return pl.pallas_call(
        body, name=name, grid=(s // tm,),
        out_shape=(jax.ShapeDtypeStruct((s, d), F32), jax.ShapeDtypeStruct((s, d), BF16),
                   jax.ShapeDtypeStruct((1, d), F32)),
        in_specs=[pl.BlockSpec((tm, width), row) for width in widths] + [
            w_spec, pl.BlockSpec((tm, d), row), pl.BlockSpec((1, d), fixed), pl.BlockSpec((tm, d), row)],
        out_specs=(pl.BlockSpec((tm, d), row), pl.BlockSpec((tm, d), row), pl.BlockSpec((1, d), fixed)),
        compiler_params=_params("arbitrary"),
    )(*pieces, w, x, g, dres)


def out_bwd(dx1b, w_o, gates, y_sb, y_sw, w_bsb, w_bsw, *, name):
    s, d = dx1b.shape
    wd = w_bsb.shape[0]
    tm = min(ROW_TILE, s)

    def body(dx_ref, wo_ref, g_ref, ysb_ref, ysw_ref, wsb_ref, wsw_ref, dysb_ref, dysw_ref, dosb_ref, dosw_ref, dgl_ref):
        dm = _dot_nt(dx_ref[...], wo_ref[...])
        g = g_ref[...].astype(F32)
        g0, g1 = g[:, :d], g[:, d:]
        dy_sb = (dm * g0).astype(BF16)
        dy_sw = (dm * g1).astype(BF16)
        dysb_ref[...] = dy_sb
        dysw_ref[...] = dy_sw
        dosb_ref[...] = _dot_nt(dy_sb, wsb_ref[...])
        dosw_ref[...] = _dot_nt(dy_sw, wsw_ref[...])
        dgl_ref[:, :d] = (dm * ysb_ref[...].astype(F32) * (g0 * (1.0 - g0))).astype(BF16)
        dgl_ref[:, d:] = (dm * ysw_ref[...].astype(F32) * (g1 * (1.0 - g1))).astype(BF16)

    row = lambda i: (i, 0)
    fixed = lambda i: (0, 0)
    return pl.pallas_call(
        body, name=name, grid=(s // tm,),
        out_shape=(jax.ShapeDtypeStruct((s, d), BF16), jax.ShapeDtypeStruct((s, d), BF16),
                   jax.ShapeDtypeStruct((s, wd), F32), jax.ShapeDtypeStruct((s, wd), F32),
                   jax.ShapeDtypeStruct((s, 2 * d), BF16)),
        in_specs=[pl.BlockSpec((tm, d), row), pl.BlockSpec((d, d), fixed), pl.BlockSpec((tm, 2 * d), row),
                  pl.BlockSpec((tm, d), row), pl.BlockSpec((tm, d), row), pl.BlockSpec((wd, d), fixed),
                  pl.BlockSpec((wd, d), fixed)],
        out_specs=(pl.BlockSpec((tm, d), row), pl.BlockSpec((tm, d), row), pl.BlockSpec((tm, wd), row),
                   pl.BlockSpec((tm, wd), row), pl.BlockSpec((tm, 2 * d), row)),
        compiler_params=_params("parallel"),
    )(dx1b, w_o, gates, y_sb, y_sw, w_bsb, w_bsw)


def matmul_tn(a, pieces, *, a_block, b_block, relu2, name):
    s, m = a.shape
    widths = [p.shape[1] for p in pieces]
    n = sum(widths)
    n_pieces = len(pieces)
    ts = min(512, s)
    if a_block is not None:
        nblk, bm, bn = m // a_block, a_block, n
        a_spec = pl.BlockSpec((ts, a_block), lambda i, k: (k, i))
        b_specs = [pl.BlockSpec((ts, width), lambda i, k: (k, 0)) for width in widths]
    else:
        nblk, bm, bn = n // b_block, m, b_block
        a_spec = pl.BlockSpec((ts, m), lambda i, k: (k, 0))
        b_specs = [pl.BlockSpec((ts, b_block), lambda i, k: (k, i))]

    def body(a_ref, *refs):
        b_refs, o_ref, acc = refs[:n_pieces], refs[n_pieces], refs[n_pieces + 1]
        k = pl.program_id(1)

        @pl.when(k == 0)
        def _():
            acc[...] = jnp.zeros_like(acc)

        av = a_ref[...]
        if relu2:
            af = jnp.maximum(av.astype(F32), 0.0)
            av = af * af
        av = av.astype(BF16)
        off = 0
        for b_ref in b_refs:
            width = b_ref.shape[1]
            acc[:, off:off + width] += _dot_tn(av, b_ref[...].astype(BF16))
            off += width

        @pl.when(k == pl.num_programs(1) - 1)
        def _():
            o_ref[...] = acc[...].astype(BF16)

    return pl.pallas_call(
        body, name=name, grid=(nblk, s // ts), out_shape=jax.ShapeDtypeStruct((nblk, bm, bn), BF16),
        in_specs=[a_spec] + b_specs, out_specs=pl.BlockSpec((None, bm, bn), lambda i, k: (i, 0, 0)),
        scratch_shapes=[pltpu.VMEM((bm, bn), F32)], compiler_params=_params("parallel", "arbitrary"),
    )(a, *pieces)


def _softplus(z):
    return jnp.maximum(z, 0.0) + jnp.log1p(jnp.exp(-jnp.abs(z)))


def _head_mask(h):
    return (lax.broadcasted_iota(jnp.int32, (1, LANES), 1) // HEAD_DIM) == h


def _stack_heads(x):
    zero = jnp.zeros_like(x)
    return jnp.concatenate([jnp.where(_head_mask(0), x, zero), jnp.where(_head_mask(1), x, zero)], axis=0)


def _unstack_heads(r, t):
    return jnp.where(_head_mask(0), r[:t], r[t:])


def _strict_lower2(t):
    row = lax.broadcasted_iota(jnp.int32, (2 * t, t), 0)
    col = lax.broadcasted_iota(jnp.int32, (2 * t, t), 1)
    return col < jnp.where(row >= t, row - t, row)


def sb_attn_fwd(proj, tri, next_shards, *, name):
    s = proj.shape[0]
    t = min(SB_TILE, s)
    nq = s // t
    n_pairs = SB_WIDTH // LANES
    hosted = next_shards is not None

    def body(q_ref, k_ref, v_ref, tri_ref, *refs):
        if hosted:
            gather = _Gather(refs[:N_MATS], refs[N_MATS + 1:2 * N_MATS + 1], *refs[2 * N_MATS + 1:])
            o_ref = refs[N_MATS]
            p = pl.program_id(0)

            @pl.when(p == 0)
            def _():
                gather.start()

            @pl.when(p == n_pairs - 1)
            def _():
                gather.forward()
        else:
            o_ref = refs[0]
        strict = _strict_lower2(t)

        def q_tile(qb, carry):
            q0 = pl.multiple_of(qb * t, t)
            qh = _stack_heads(q_ref[pl.ds(q0, t), :])
            z = _dot_nt(qh, k_ref[pl.ds(q0, t), :]) * SCALE
            sp = _softplus(z)
            lk = jnp.where(strict, -sp, 0.0)
            w = jnp.where(strict, jnp.exp(z - sp + _dot2(lk, tri_ref[...])), 0.0)
            acc = _dot2(w, v_ref[pl.ds(q0, t), :])
            c = jnp.sum(lk, axis=1, keepdims=True)

            def cond(st):
                return jnp.logical_and(st[0] >= 0, st[3] > SB_CUTOFF)

            def step(st):
                kb, c, acc, _ = st
                k0 = pl.multiple_of(kb * t, t)
                z = _dot_nt(qh, k_ref[pl.ds(k0, t), :]) * SCALE
                lk = -_softplus(z)
                w = jnp.exp(z + lk + _dot2(lk, tri_ref[...]) + c)
                acc = acc + _dot2(w, v_ref[pl.ds(k0, t), :])
                c = c + jnp.sum(lk, axis=1, keepdims=True)
                return kb - 1, c, acc, jnp.max(c)

            _, _, acc, _ = lax.while_loop(cond, step, (qb - 1, c, acc, jnp.max(c)))
            o_ref[pl.ds(q0, t), :] = _unstack_heads(acc, t)
            return carry

        lax.fori_loop(0, nq, q_tile, 0)

        if hosted:
            @pl.when(p == n_pairs - 1)
            def _():
                gather.finish()

    def col(j):
        return pl.BlockSpec((s, LANES), lambda p: (0, j * n_pairs + p))

    o_shape = jax.ShapeDtypeStruct((s, SB_WIDTH), F32)
    o_spec = pl.BlockSpec((s, LANES), lambda p: (0, p))
    in_specs = [col(0), col(1), col(2), pl.BlockSpec((t, t), lambda p: (0, 0))]
    if not hosted:
        return pl.pallas_call(
            body, name=name, grid=(n_pairs,), out_shape=o_shape, in_specs=in_specs, out_specs=o_spec,
            compiler_params=_params("arbitrary"),
        )(proj, proj, proj, tri)
    outs = pl.pallas_call(
        body, name=name, grid=(n_pairs,), out_shape=(o_shape,) + _gather_out_shapes(next_shards),
        in_specs=in_specs + [ANY] * N_MATS, out_specs=(o_spec,) + (ANY,) * N_MATS, scratch_shapes=GATHER_SCRATCH,
        compiler_params=pltpu.CompilerParams(dimension_semantics=("arbitrary",), vmem_limit_bytes=VMEM_LIMIT,
                                             has_side_effects=True),
    )(proj, proj, proj, tri, *next_shards)
    return outs[0], list(outs[1:])


def sb_attn_bwd(proj, tri, o, do, exchange, *, name):
    s = proj.shape[0]
    t = min(SB_TILE, s)
    nq = s // t
    n_pairs = SB_WIDTH // LANES
    hosted = exchange is not None

    def body(q_ref, k_ref, v_ref, tri_ref, o_ref, do_ref, *refs):
        if hosted:
            dq_ref, dk_ref, dv_ref = refs[2 * N_MATS:2 * N_MATS + 3]
            land_refs = refs[2 * N_MATS + 3:3 * N_MATS + 3]
            send_sems, recv_sems, dk_acc, dv_acc = refs[3 * N_MATS + 3:]
            ex = _ChipExchange(exchange[0], refs[:N_MATS], land_refs, send_sems, recv_sems)
            p = pl.program_id(0)

            @pl.when(p == 0)
            def _():
                ex.start()
        else:
            dq_ref, dk_ref, dv_ref, dk_acc, dv_acc = refs
        strict = _strict_lower2(t)
        dk_acc[...] = jnp.zeros_like(dk_acc)
        dv_acc[...] = jnp.zeros_like(dv_acc)

        def block(qh, doh, doh_b, dd, k0, c, ce, diag):
            kt = k_ref[pl.ds(k0, t), :]
            vt = v_ref[pl.ds(k0, t), :]
            z = _dot_nt(qh, kt) * SCALE
            sp = _softplus(z)
            lb = z - sp
            lk = jnp.where(strict, -sp, 0.0) if diag else -sp
            w = jnp.exp(lb + _dot2(lk, tri_ref[...]) + c)
            if diag:
                w = jnp.where(strict, w, 0.0)
            e = w * _dot2_nt(doh, vt)
            dz = e - jnp.exp(lb) * (dd - ce - _dot2(e, tri_ref[...]))
            if diag:
                dz = jnp.where(strict, dz, 0.0)
            dzb = (dz * SCALE).astype(BF16)
            dk_acc[pl.ds(k0, t), :] += _dot_tn(dzb, qh)
            dv_acc[pl.ds(k0, t), :] += _dot_tn(w.astype(BF16), doh_b)
            return (_dot(dzb, kt), c + jnp.sum(lk, axis=1, keepdims=True), ce + jnp.sum(e, axis=1, keepdims=True))

        def q_tile(qb, carry):
            q0 = pl.multiple_of(qb * t, t)
            qh = _stack_heads(q_ref[pl.ds(q0, t), :])
            doh = _stack_heads(do_ref[pl.ds(q0, t), :])
            doh_b = doh.astype(BF16)
            ov = o_ref[pl.ds(q0, t), :]
            dd = jnp.sum(doh * jnp.concatenate([ov, ov], axis=0), axis=1, keepdims=True)
            zero = jnp.zeros((2 * t, 1), F32)
            dq, c, ce = block(qh, doh, doh_b, dd, q0, zero, zero, True)

            def cond(st):
                return jnp.logical_and(st[0] >= 0, st[4] > SB_CUTOFF)

            def step(st):
                kb, c, ce, dq, _ = st
                ddq, c, ce = block(qh, doh, doh_b, dd, pl.multiple_of(kb * t, t), c, ce, False)
                return kb - 1, c, ce, dq + ddq, jnp.max(c)

            st = lax.while_loop(cond, step, (qb - 1, c, ce, dq, jnp.max(c)))
            dq_ref[pl.ds(q0, t), :] = _unstack_heads(st[3], t).astype(BF16)
            return carry

        lax.fori_loop(0, nq, q_tile, 0)
        dk_ref[...] = dk_acc[...].astype(BF16)
        dv_ref[...] = dv_acc[...].astype(BF16)

        if hosted:
            @pl.when(p == n_pairs - 1)
            def _():
                ex.finish()

    def col(j):
        return pl.BlockSpec((s, LANES), lambda p: (0, j * n_pairs + p))

    pair = pl.BlockSpec((s, LANES), lambda p: (0, p))
    in_specs = [col(0), col(1), col(2), pl.BlockSpec((t, t), lambda p: (0, 0)), pair, pair]
    d_shapes = (jax.ShapeDtypeStruct((s, SB_WIDTH), BF16),) * 3
    acc_scratch = [pltpu.VMEM((s, LANES), F32), pltpu.VMEM((s, LANES), F32)]
    if not hosted:
        return pl.pallas_call(
            body, name=name, grid=(n_pairs,), out_shape=d_shapes, in_specs=in_specs, out_specs=(pair, pair, pair),
            scratch_shapes=acc_scratch, compiler_params=_params("arbitrary"),
        )(proj, proj, proj, tri, o, do)
    _, parts, lands = exchange
    outs = pl.pallas_call(
        body, name=name, grid=(n_pairs,),
        out_shape=d_shapes + tuple(jax.ShapeDtypeStruct(a.shape, a.dtype) for a in lands),
        in_specs=in_specs + [ANY] * (2 * N_MATS), out_specs=(pair, pair, pair) + (ANY,) * N_MATS,
        input_output_aliases={6 + N_MATS + w: 3 + w for w in range(N_MATS)},
        scratch_shapes=CHIP_SCRATCH + acc_scratch,
        compiler_params=pltpu.CompilerParams(dimension_semantics=("arbitrary",), vmem_limit_bytes=VMEM_LIMIT,
                                             has_side_effects=True),
    )(proj, proj, proj, tri, o, do, *parts, *lands)
    return outs[0], outs[1], outs[2], list(outs[3:])


def _lane_lo():
    return lax.broadcasted_iota(jnp.int32, (1, LANES), 1) < HEAD_DIM


def _swap_halves(x):
    return pltpu.roll(x, HEAD_DIM, 1)


def _rot_half(y):
    first = (lax.broadcasted_iota(jnp.int32, (1, LANES), 1) % HEAD_DIM) < (HEAD_DIM // 2)
    return jnp.where(first, pltpu.roll(y, LANES - HEAD_DIM // 2, 1), pltpu.roll(y, HEAD_DIM // 2, 1))


def _head_mean(v):
    lo = _lane_lo()
    s0 = jnp.sum(jnp.where(lo, v, 0.0), axis=1, keepdims=True)
    s1 = jnp.sum(jnp.where(lo, 0.0, v), axis=1, keepdims=True)
    return jnp.where(lo, s0, s1) * (1.0 / HEAD_DIM)


def swa_prep_fwd(proj, cos_p, sin_p, gq, gk, *, name):
    s = proj.shape[0]
    tm = min(512, s)
    q_blk = (3 * SB_WIDTH) // SWA_Q_WIDTH
    k_blk = (3 * SB_WIDTH + SWA_Q_WIDTH) // LANES

    def norm_rope(xv, g, cosv, sinv):
        y = (xv * lax.rsqrt(_head_mean(xv * xv) + NORM_EPS)) * g
        return y * cosv + _rot_half(y) * sinv

    def body(q_ref, k_ref, cos_ref, sin_ref, gq_ref, gk_ref, qn_ref, kn_ref):
        cosv, sinv = cos_ref[...], sin_ref[...]
        for j in range(SWA_Q_WIDTH // LANES):
            sl = slice(j * LANES, (j + 1) * LANES)
            qn_ref[:, sl] = norm_rope(q_ref[:, sl].astype(F32), gq_ref[...], cosv, sinv).astype(BF16)
        kn_ref[...] = norm_rope(k_ref[...].astype(F32), gk_ref[...], cosv, sinv).astype(BF16)

    row = lambda i: (i, 0)
    fixed = lambda i: (0, 0)
    return pl.pallas_call(
        body, name=name, grid=(s // tm,),
        out_shape=(jax.ShapeDtypeStruct((s, SWA_Q_WIDTH), BF16), jax.ShapeDtypeStruct((s, LANES), BF16)),
        in_specs=[pl.BlockSpec((tm, SWA_Q_WIDTH), lambda i: (i, q_blk)), pl.BlockSpec((tm, LANES), lambda i: (i, k_blk)),
                  pl.BlockSpec((tm, LANES), row), pl.BlockSpec((tm, LANES), row),
                  pl.BlockSpec((1, LANES), fixed), pl.BlockSpec((1, LANES), fixed)],
        out_specs=(pl.BlockSpec((tm, SWA_Q_WIDTH), row), pl.BlockSpec((tm, LANES), row)),
        compiler_params=_params("parallel"),
    )(proj, proj, cos_p, sin_p, gq, gk)


def swa_prep_bwd(proj, cos_p, sin_p, gq, gk, dqn, dkn, dv, *, name):
    s = proj.shape[0]
    tm = min(512, s)
    q_blk = (3 * SB_WIDTH) // SWA_Q_WIDTH
    k_blk = (3 * SB_WIDTH + SWA_Q_WIDTH) // LANES

    def bwd(xv, g, cosv, sinv, dout):
        dy = dout * cosv + _rot_half(dout * sinv)
        r = lax.rsqrt(_head_mean(xv * xv) + NORM_EPS)
        dyg = dy * g
        dx = r * dyg - xv * ((r * r * r) * _head_mean(dyg * xv))
        return dx, jnp.sum(dy * (xv * r), axis=0, keepdims=True)

    def body(q_ref, k_ref, cos_ref, sin_ref, gq_ref, gk_ref, dqn_ref, dkn_ref, dv_ref, dq_ref, dk_ref, dvb_ref,
             dgq_ref, dgk_ref):
        @pl.when(pl.program_id(0) == 0)
        def _():
            dgq_ref[...] = jnp.zeros_like(dgq_ref)
            dgk_ref[...] = jnp.zeros_like(dgk_ref)

        cosv, sinv = cos_ref[...], sin_ref[...]
        for j in range(SWA_Q_WIDTH // LANES):
            sl = slice(j * LANES, (j + 1) * LANES)
            dx, dg = bwd(q_ref[:, sl].astype(F32), gq_ref[...], cosv, sinv, dqn_ref[:, sl])
            dq_ref[:, sl] = dx.astype(BF16)
            dgq_ref[:, sl] += dg
        dx, dg = bwd(k_ref[...].astype(F32), gk_ref[...], cosv, sinv, dkn_ref[...])
        dk_ref[...] = dx.astype(BF16)
        dgk_ref[...] += dg
        dvb_ref[...] = dv_ref[...].astype(BF16)

    row = lambda i: (i, 0)
    fixed = lambda i: (0, 0)
    lane_row = pl.BlockSpec((tm, LANES), row)
    return pl.pallas_call(
        body, name=name, grid=(s // tm,),
        out_shape=(jax.ShapeDtypeStruct((s, SWA_Q_WIDTH), BF16), jax.ShapeDtypeStruct((s, LANES), BF16),
                   jax.ShapeDtypeStruct((s, LANES), BF16),
                   jax.ShapeDtypeStruct((1, SWA_Q_WIDTH), F32), jax.ShapeDtypeStruct((1, LANES), F32)),
        in_specs=[pl.BlockSpec((tm, SWA_Q_WIDTH), lambda i: (i, q_blk)), pl.BlockSpec((tm, LANES), lambda i: (i, k_blk)),
                  lane_row, lane_row, pl.BlockSpec((1, LANES), fixed), pl.BlockSpec((1, LANES), fixed),
                  pl.BlockSpec((tm, SWA_Q_WIDTH), row), lane_row, lane_row],
        out_specs=(pl.BlockSpec((tm, SWA_Q_WIDTH), row), lane_row, lane_row,
                   pl.BlockSpec((1, SWA_Q_WIDTH), fixed), pl.BlockSpec((1, LANES), fixed)),
        compiler_params=_params("arbitrary"),
    )(proj, proj, cos_p, sin_p, gq, gk, dqn, dkn, dv)


def _swa_tile(i, k_ref, v_ref, second_kv):
    q0 = pl.multiple_of(i * SWA_TQ, SWA_TQ)
    k0 = pl.multiple_of(jnp.maximum(i - 1, 0) * SWA_TQ, SWA_TQ)
    keep = jnp.logical_xor(_lane_lo(), second_kv)
    kf = k_ref[pl.ds(k0, SWA_TK), :].astype(F32)
    vf = v_ref[pl.ds(k0, SWA_TK), :].astype(F32)
    kg = jnp.where(keep, kf, _swap_halves(kf)).astype(BF16)
    vg = jnp.where(keep, vf, _swap_halves(vf)).astype(BF16)
    row = lax.broadcasted_iota(jnp.int32, (2 * SWA_TQ, SWA_TK), 0)
    tpos = q0 + jnp.where(row >= SWA_TQ, row - SWA_TQ, row)
    spos = k0 + lax.broadcasted_iota(jnp.int32, (2 * SWA_TQ, SWA_TK), 1)
    valid = jnp.logical_and(spos <= tpos, spos > tpos - WINDOW)
    return q0, k0, kg, vg, valid


def _swa_probs(qh, kg, valid, sink):
    z = jnp.where(valid, _dot_nt(qh, kg) * SCALE, NEG)
    m = jnp.maximum(jnp.max(z, axis=1, keepdims=True), sink)
    pexp = jnp.exp(z - m)
    psink = jnp.exp(sink - m)
    inv = 1.0 / (jnp.sum(pexp, axis=1, keepdims=True) + psink)
    return pexp * inv, psink * inv


def _stacked_sink(sink_row):
    s0 = jnp.sum(jnp.where(_head_mask(0), sink_row, 0.0), axis=1, keepdims=True) * (1.0 / HEAD_DIM)
    s1 = jnp.sum(jnp.where(_head_mask(1), sink_row, 0.0), axis=1, keepdims=True) * (1.0 / HEAD_DIM)
    top = lax.broadcasted_iota(jnp.int32, (2 * SWA_TQ, 1), 0) < SWA_TQ
    return jnp.where(top, s0, s1)


def swa_attn_fwd(qn, kn, proj, sink_p, *, name):
    s = qn.shape[0]
    nq = s // SWA_TQ
    v_blk = (3 * SB_WIDTH + SWA_Q_WIDTH + SWA_KV_WIDTH) // LANES

    def body(q_ref, k_ref, v_ref, s_ref, o_ref):
        second_kv = (pl.program_id(0) // 2) == 1
        sink = _stacked_sink(s_ref[...])

        def tile(i, carry):
            q0, _, kg, vg, valid = _swa_tile(i, k_ref, v_ref, second_kv)
            probs, _ = _swa_probs(_stack_heads(q_ref[pl.ds(q0, SWA_TQ), :]), kg, valid, sink)
            o_ref[pl.ds(q0, SWA_TQ), :] = _unstack_heads(_dot(probs.astype(BF16), vg), SWA_TQ)
            return carry

        lax.fori_loop(0, nq, tile, 0, unroll=2)

    pair = pl.BlockSpec((s, LANES), lambda p: (0, p))
    whole = pl.BlockSpec((s, LANES), lambda p: (0, 0))
    return pl.pallas_call(
        body, name=name, grid=(SWA_Q_WIDTH // LANES,), out_shape=jax.ShapeDtypeStruct((s, SWA_Q_WIDTH), F32),
        in_specs=[pair, whole, pl.BlockSpec((s, LANES), lambda p: (0, v_blk)),
                  pl.BlockSpec((None, 1, LANES), lambda p: (p, 0, 0))],
        out_specs=pair, compiler_params=_params("parallel"),
    )(qn, kn, proj, sink_p)


def swa_attn_bwd(qn, kn, proj, sink_p, o, do, *, name):
    s = qn.shape[0]
    nq = s // SWA_TQ
    v_blk = (3 * SB_WIDTH + SWA_Q_WIDTH + SWA_KV_WIDTH) // LANES
    fold_rows = min(512, s)

    def body(q_ref, k_ref, v_ref, s_ref, o_ref, do_ref, dq_ref, dk_ref, dv_ref, ds_ref, acc_k, acc_v):
        p = pl.program_id(0)
        second_kv = (p // 2) == 1
        sink = _stacked_sink(s_ref[...])

        @pl.when(p % 2 == 0)
        def _():
            acc_k[...] = jnp.zeros_like(acc_k)
            acc_v[...] = jnp.zeros_like(acc_v)

        def tile(i, dsink):
            q0, k0, kg, vg, valid = _swa_tile(i, k_ref, v_ref, second_kv)
            qh = _stack_heads(q_ref[pl.ds(q0, SWA_TQ), :])
            doh = _stack_heads(do_ref[pl.ds(q0, SWA_TQ), :])
            doh_b = doh.astype(BF16)
            ov = o_ref[pl.ds(q0, SWA_TQ), :]
            delta = jnp.sum(doh * jnp.concatenate([ov, ov], axis=0), axis=1, keepdims=True)
            probs, psink = _swa_probs(qh, kg, valid, sink)
            dz = probs * (_dot_nt(doh_b, vg) - delta)
            dzb = (dz * SCALE).astype(BF16)
            dq_ref[pl.ds(q0, SWA_TQ), :] = _unstack_heads(_dot(dzb, kg), SWA_TQ)
            acc_k[pl.ds(k0, SWA_TK), :] += _dot_tn(dzb, qh)
            acc_v[pl.ds(k0, SWA_TK), :] += _dot_tn(probs.astype(BF16), doh_b)
            pd = psink * delta
            return dsink - jnp.where(_head_mask(0), jnp.sum(pd[:SWA_TQ], axis=0, keepdims=True),
                                     jnp.sum(pd[SWA_TQ:], axis=0, keepdims=True))

        ds_ref[...] = lax.fori_loop(0, nq, tile, jnp.zeros((1, LANES), F32), unroll=2)

        def fold_into(first_head):
            def fold(r, carry):
                rows = pl.ds(pl.multiple_of(r * fold_rows, fold_rows), fold_rows)
                for acc, out in ((acc_k, dk_ref), (acc_v, dv_ref)):
                    a = acc[rows, :]
                    both = a + _swap_halves(a)
                    if first_head:
                        out[rows, :] = jnp.where(_lane_lo(), both, 0.0)
                    else:
                        out[rows, :] = jnp.where(_lane_lo(), out[rows, :], both)
                return carry

            lax.fori_loop(0, s // fold_rows, fold, 0)

        @pl.when(p == 1)
        def _():
            fold_into(True)

        @pl.when(p == 3)
        def _():
            fold_into(False)

    pair = pl.BlockSpec((s, LANES), lambda p: (0, p))
    whole = pl.BlockSpec((s, LANES), lambda p: (0, 0))
    return pl.pallas_call(
        body, name=name, grid=(SWA_Q_WIDTH // LANES,),
        out_shape=(jax.ShapeDtypeStruct((s, SWA_Q_WIDTH), F32), jax.ShapeDtypeStruct((s, LANES), F32),
                   jax.ShapeDtypeStruct((s, LANES), F32), jax.ShapeDtypeStruct((SWA_Q_WIDTH // LANES, 1, LANES), F32)),
        in_specs=[pair, whole, pl.BlockSpec((s, LANES), lambda p: (0, v_blk)),
                  pl.BlockSpec((None, 1, LANES), lambda p: (p, 0, 0)), pair, pair],
        out_specs=(pair, whole, whole, pl.BlockSpec((None, 1, LANES), lambda p: (p, 0, 0))),
        scratch_shapes=[pltpu.VMEM((s, LANES), F32), pltpu.VMEM((s, LANES), F32)],
        compiler_params=_params("arbitrary"),
    )(qn, kn, proj, sink_p, o, do)


def _rope_tables(s):
    inv_freq = 1.0 / (ROPE_THETA ** (jnp.arange(0, HEAD_DIM, 2, dtype=F32) / HEAD_DIM))
    ang = jnp.arange(s, dtype=F32)[:, None] * inv_freq[None, :]
    cos, sin = jnp.cos(ang), jnp.sin(ang)
    cos_p = jnp.tile(jnp.concatenate([cos, cos], axis=1), (1, LANES // HEAD_DIM))
    sin_p = jnp.tile(jnp.concatenate([-sin, sin], axis=1), (1, LANES // HEAD_DIM))
    return cos_p, sin_p


def _lane_tile(v, reps):
    return jnp.tile(v.reshape(1, -1), (1, reps))


def _natural(stack, name):
    n, r, c = stack.shape
    if name in ROW_SHARDED:
        return stack.reshape(n * r, c)
    if name == "w_up":
        return stack
    return jnp.transpose(stack, (1, 0, 2)).reshape(r, n * c)


def _pack_small(tree):
    flat = jnp.concatenate([tree[n].reshape(-1) for n in SMALL_NAMES])
    rows = -(-flat.shape[0] // (8 * LANES)) * 8
    return jnp.pad(flat, (0, rows * LANES - flat.shape[0])).reshape(rows, LANES)


def _unpack_small(packed, shapes):
    flat, out, off = packed.reshape(-1), {}, 0
    for n in SMALL_NAMES:
        size = shapes[n][0] * shapes[n][1]
        out[n] = flat[off:off + size].reshape(shapes[n])
        off += size
    return out


def train_step(x, target, weights, mom_m, mom_v):
    s = x.shape[0]
    cos_p, sin_p = _rope_tables(s)
    tri = (jnp.arange(min(SB_TILE, s))[:, None] > jnp.arange(min(SB_TILE, s))[None, :]).astype(BF16)
    shards = [[weights[n][l].astype(BF16) for n in MATRIX_NAMES] for l in range(DEPTH)]
    core = lax.axis_index("c").astype(jnp.int32).reshape(1)
    chip = (2 * lax.axis_index("x") + lax.axis_index("y")).astype(jnp.int32).reshape(1)

    stacks = gather_layer(shards[0], name="gather_layer0")
    saved = []
    for l in range(DEPTH):
        mats = {n: _natural(stacks[i], n) for i, n in enumerate(MATRIX_NAMES)}
        g_mix = weights["mix_norm_g"][l].reshape(1, D_MODEL)
        g_mlp = weights["mlp_norm_g"][l].reshape(1, D_MODEL)
        gq = _lane_tile(weights["q_norm_g"][l], LANES // HEAD_DIM)
        gk = _lane_tile(weights["k_norm_g"][l], LANES // HEAD_DIM)
        sink_p = jnp.repeat(weights["sinks"][l].reshape(SWA_Q_WIDTH // LANES, 2), HEAD_DIM, axis=1)
        sink_p = sink_p.reshape(SWA_Q_WIDTH // LANES, 1, LANES)
        h, proj, gates = norm_matmul(x, g_mix, mats["w_in"], gate_split=ATTN_WIDTH, name="in_proj")
        if l + 1 < DEPTH:
            o_sb, stacks = sb_attn_fwd(proj, tri, shards[l + 1], name="sb_fwd_gather")
        else:
            o_sb = sb_attn_fwd(proj, tri, None, name="sb_fwd")
        qn, kn = swa_prep_fwd(proj, cos_p, sin_p, gq, gk, name="swa_prep")
        o_sw = swa_attn_fwd(qn, kn, proj, sink_p, name="swa_fwd")
        x1, y_sb, y_sw, merged = merge_out_fwd(x, o_sb, o_sw, gates, mats["w_branch_sb"], mats["w_branch_swa"],
                                               mats["w_out"], name="merge_out")
        h2, u = norm_matmul(x1, g_mlp, mats["w_up"], gate_split=None, name="mlp_up")
        x2 = mlp_down_fwd(x1, u, mats["w_down"], name="mlp_down")
        saved.append(dict(x=x, h=h, proj=proj, gates=gates, o_sb=o_sb, qn=qn, kn=kn, o_sw=o_sw, y_sb=y_sb, y_sw=y_sw,
                          merged=merged, x1=x1, h2=h2, u=u, g_mix=g_mix, g_mlp=g_mlp, gq=gq, gk=gk, sink_p=sink_p,
                          mats=mats))
        x = x2

    dx, dxb, loss = loss_head(x, target, name="loss_head")

    shard_shapes = [weights[n].shape[1:] for n in MATRIX_NAMES]
    parts = [lax.empty((DEPTH, N_CHIPS) + sh, BF16) for sh in shard_shapes]
    lands = [lax.empty((DEPTH, 3) + sh, BF16) for sh in shard_shapes]
    small_grads = {n: [None] * DEPTH for n in SMALL_NAMES}
    pending = None
    for l in reversed(range(DEPTH)):
        a = saved[l]
        mats = a["mats"]
        du = mlp_bwd_up(dxb, a["u"], mats["w_down"], name="mlp_bwd_up")
        dw_down = matmul_tn(a["u"], [dxb], a_block=D_MODEL // 2, b_block=None, relu2=True, name="dw_down")
        dw_up = matmul_tn(a["h2"], [du], a_block=None, b_block=du.shape[1] // N_DEV, relu2=False, name="dw_up")
        dx1, dx1b, dg_mlp = matmul_nt_norm_bwd([du], mats["w_up"], a["x1"], a["g_mlp"], dx, name="mlp_bwd_norm")
        small_grads["mlp_norm_g"][l] = dg_mlp.reshape(D_MODEL)
        dw_out = matmul_tn(a["merged"], [dx1b], a_block=D_MODEL // 2, b_block=None, relu2=False, name="dw_out")
        dy_sb, dy_sw, do_sb, do_sw, dgl = out_bwd(dx1b, mats["w_out"], a["gates"], a["y_sb"], a["y_sw"],
                                                  mats["w_branch_sb"], mats["w_branch_swa"], name="out_bwd")
        dw_bsb = matmul_tn(a["o_sb"], [dy_sb], a_block=None, b_block=D_MODEL // N_DEV, relu2=False, name="dw_branch_sb")
        dw_bsw = matmul_tn(a["o_sw"], [dy_sw], a_block=None, b_block=D_MODEL // N_DEV, relu2=False, name="dw_branch_swa")
        if pending is None:
            dq_sb, dk_sb, dv_sb = sb_attn_bwd(a["proj"], tri, a["o_sb"], do_sb, None, name="sb_bwd")
        else:
            dq_sb, dk_sb, dv_sb, lands = sb_attn_bwd(a["proj"], tri, a["o_sb"], do_sb, (pending, parts, lands),
                                                     name="sb_bwd_exchange")
        dqn, dkn, dv_sw, dsink = swa_attn_bwd(a["qn"], a["kn"], a["proj"], a["sink_p"], a["o_sw"], do_sw, name="swa_bwd")
        dq_sw, dk_sw, dv_swb, dgq, dgk = swa_prep_bwd(a["proj"], cos_p, sin_p, a["gq"], a["gk"], dqn, dkn, dv_sw,
                                                      name="swa_prep_bwd")
        small_grads["q_norm_g"][l] = dgq.reshape(SWA_Q_WIDTH // HEAD_DIM, HEAD_DIM).sum(0)
        small_grads["k_norm_g"][l] = dgk.reshape(LANES // HEAD_DIM, HEAD_DIM).sum(0)
        small_grads["sinks"][l] = dsink[:, 0, ::HEAD_DIM].reshape(SWA_Q_WIDTH // HEAD_DIM)
        pieces = [dq_sb, dk_sb, dv_sb, dq_sw, dk_sw, dv_swb, dgl]
        dw_in = matmul_tn(a["h"], pieces, a_block=D_MODEL // 2, b_block=None, relu2=False, name="dw_in")
        dx, dxb, dg_mix = matmul_nt_norm_bwd(pieces, mats["w_in"], a["x"], a["g_mix"], dx1, name="in_proj_bwd")
        small_grads["mix_norm_g"][l] = dg_mix.reshape(D_MODEL)

        c_in = IN_WIDTH // N_DEV
        grads = [jnp.transpose(dw_in.reshape(D_MODEL, N_DEV, c_in), (1, 0, 2)), dw_bsb, dw_bsw,
                 dw_out.reshape((N_DEV,) + shard_shapes[3]), dw_up, dw_down.reshape((N_DEV,) + shard_shapes[5])]
        landed = pair_exchange(grads, name="grad_pair_exchange")
        parts = pair_sum(l, grads, landed, parts, core, name="grad_pair_sum")
        pending = l
    lands = chip_exchange(pending, parts, lands, name="grad_chip_exchange")

    out_g, out_d, out_m, out_v = {}, {}, {}, {}
    for i, n in enumerate(MATRIX_NAMES):
        out_g[n], out_d[n], out_m[n], out_v[n] = reduce_adamw(parts[i], lands[i], chip, weights[n], mom_m[n], mom_v[n],
                                                              name="adamw_" + n)
    small_shapes = {n: weights[n].shape for n in SMALL_NAMES}
    small_all = gather_small(_pack_small({n: jnp.stack(v) for n, v in small_grads.items()}), name="gather_small_grads")
    sg, sd, sm, sv = small_adamw(small_all, _pack_small(weights), _pack_small(mom_m), _pack_small(mom_v),
                                 name="small_adamw")
    for tree, packed_small in ((out_g, sg), (out_d, sd), (out_m, sm), (out_v, sv)):
        tree.update(_unpack_small(packed_small, small_shapes))
    return loss, dx, (out_g, out_d, out_m, out_v)


def kernel(x, mix_norm_g, w_in, q_norm_g, k_norm_g, sinks, w_branch_sb, w_branch_swa, w_out, mlp_norm_g, w_up, w_down, loss_target, m_mix_norm_g, m_w_in, m_q_norm_g, m_k_norm_g, m_sinks, m_w_branch_sb, m_w_branch_swa, m_w_out, m_mlp_norm_g, m_w_up, m_w_down, v_mix_norm_g, v_w_in, v_q_norm_g, v_k_norm_g, v_sinks, v_w_branch_sb, v_w_branch_swa, v_w_out, v_mlp_norm_g, v_w_up, v_w_down):
    weights = dict(mix_norm_g=mix_norm_g, w_in=w_in, q_norm_g=q_norm_g, k_norm_g=k_norm_g, sinks=sinks,
                   w_branch_sb=w_branch_sb, w_branch_swa=w_branch_swa, w_out=w_out, mlp_norm_g=mlp_norm_g, w_up=w_up,
                   w_down=w_down)
    mom_m = dict(mix_norm_g=m_mix_norm_g, w_in=m_w_in, q_norm_g=m_q_norm_g, k_norm_g=m_k_norm_g, sinks=m_sinks,
                 w_branch_sb=m_w_branch_sb, w_branch_swa=m_w_branch_swa, w_out=m_w_out, mlp_norm_g=m_mlp_norm_g,
                 w_up=m_w_up, w_down=m_w_down)
    mom_v = dict(mix_norm_g=v_mix_norm_g, w_in=v_w_in, q_norm_g=v_q_norm_g, k_norm_g=v_k_norm_g, sinks=v_sinks,
                 w_branch_sb=v_w_branch_sb, w_branch_swa=v_w_branch_swa, w_out=v_w_out, mlp_norm_g=v_mlp_norm_g,
                 w_up=v_w_up, w_down=v_w_down)
    loss_part, grad_x, outs = train_step(x[0], loss_target[0], weights, mom_m, mom_v)
    loss = lax.psum(loss_part[0, 0], MESH_AXES)
    return (loss, grad_x[None], *[outs[0][n] for n in WEIGHT_ORDER], *[outs[1][n] for n in WEIGHT_ORDER],
            *[outs[2][n] for n in WEIGHT_ORDER], *[outs[3][n] for n in WEIGHT_ORDER])
```

```python
import functools

import jax
import jax.numpy as jnp
from jax import lax
from jax.experimental import pallas as pl
from jax.experimental.pallas import tpu as pltpu

F32 = jnp.float32
BF16 = jnp.bfloat16

DEPTH = 4
D_MODEL = 1024
HEAD_DIM = 64
LANES = 128
WINDOW = 128
SB_WIDTH = 512
SWA_Q_WIDTH = 512
SWA_KV_WIDTH = 128
ATTN_WIDTH = 3 * SB_WIDTH + SWA_Q_WIDTH + 2 * SWA_KV_WIDTH
IN_WIDTH = ATTN_WIDTH + 2 * D_MODEL
ROPE_THETA = 10000.0
NORM_EPS = 1e-6
SCALE = HEAD_DIM ** -0.5
NEG = -1e30
N_DEV = 8
N_CHIPS = 4

ADAM_LR = 0.001
ADAM_B1 = 0.9
ADAM_B2 = 0.999
ADAM_EPS = 1e-08
ADAM_WD = 0.01
ADAM_STEP = 10

SB_TILE = 256
SB_CUTOFF = -104.0
SWA_TQ = 128
SWA_TK = 256
ROW_TILE = 256
VMEM_LIMIT = 56 * 1024 * 1024

MATRIX_NAMES = ("w_in", "w_branch_sb", "w_branch_swa", "w_out", "w_up", "w_down")
ROW_SHARDED = ("w_out", "w_down")
SMALL_NAMES = ("mix_norm_g", "q_norm_g", "k_norm_g", "sinks", "mlp_norm_g")
WEIGHT_ORDER = ("mix_norm_g", "w_in", "q_norm_g", "k_norm_g", "sinks", "w_branch_sb", "w_branch_swa", "w_out",
                "mlp_norm_g", "w_up", "w_down")
MESH_AXES = ("x", "y", "c")
N_MATS = len(MATRIX_NAMES)

ANY = pl.BlockSpec(memory_space=pl.ANY)
MESH = pl.DeviceIdType.MESH


def _params(*sem):
    return pltpu.CompilerParams(dimension_semantics=sem, vmem_limit_bytes=VMEM_LIMIT)


def _dot(a, b):
    return jnp.dot(a, b, preferred_element_type=F32)


def _dot_nt(a, b):
    return lax.dot_general(a, b, (((1,), (1,)), ((), ())), preferred_element_type=F32)


def _dot_tn(a, b):
    return lax.dot_general(a, b, (((0,), (0,)), ((), ())), preferred_element_type=F32)


def _split_bf16(x):
    hi = x.astype(BF16)
    lo = (x - hi.astype(F32)).astype(BF16)
    return hi, lo


def _dot2(x, b):
    hi, lo = _split_bf16(x)
    return _dot(hi, b) + _dot(lo, b)


def _dot2_nt(x, b):
    hi, lo = _split_bf16(x)
    return _dot_nt(hi, b) + _dot_nt(lo, b)


def _rsqrt_ms(x):
    return lax.rsqrt(jnp.mean(x * x, axis=-1, keepdims=True) + NORM_EPS)


def _place():
    return lax.axis_index("x"), lax.axis_index("y"), lax.axis_index("c")


def _gather_scratch(n):
    return [pltpu.SemaphoreType.DMA((7, n)), pltpu.SemaphoreType.DMA((7, n)), pltpu.SemaphoreType.DMA((n,))]


class _Gather:
    def __init__(self, x_refs, out_refs, send_sems, recv_sems, local_sems):
        self.x_refs, self.out_refs = x_refs, out_refs
        self.send_sems, self.recv_sems, self.local_sems = send_sems, recv_sems, local_sems
        self.n = len(x_refs)
        x, y, c = _place()
        self.c = c
        self.me, self.sibling = (x, y, c), (x, y, 1 - c)
        self.chips = [(1 - x, y), (x, 1 - y), (1 - x, 1 - y)]

    def _copy(self, k, w, blk, to, own=False):
        dst = self.out_refs[w].at[4 * blk[0] + 2 * blk[1] + blk[2]]
        return pltpu.make_async_remote_copy(
            src_ref=self.x_refs[w] if own else dst, dst_ref=dst, send_sem=self.send_sems.at[k, w],
            recv_sem=self.recv_sems.at[k, w], device_id=to, device_id_type=MESH)

    def _mine(self, w):
        me = self.me
        return pltpu.make_async_copy(self.x_refs[w], self.out_refs[w].at[4 * me[0] + 2 * me[1] + me[2]],
                                     self.local_sems.at[w])

    def _first(self, w):
        return [self._copy(0, w, self.me, self.sibling, own=True)] + [
            self._copy(1 + j, w, self.me, (*chip, self.c), own=True) for j, chip in enumerate(self.chips)]

    def _passed(self, j, w):
        return self._copy(4 + j, w, (*self.chips[j], self.c), self.sibling)

    def start(self):
        for w in range(self.n):
            self._mine(w).start()
            for cp in self._first(w):
                cp.start()

    def forward(self):
        for j, chip in enumerate(self.chips):
            for w in range(self.n):
                self._copy(1 + j, w, (*chip, self.c), self.me).wait_recv()
                self._passed(j, w).start()

    def finish(self):
        for w in range(self.n):
            self._copy(0, w, self.sibling, self.me).wait_recv()
            for j, chip in enumerate(self.chips):
                self._copy(4 + j, w, (*chip, 1 - self.c), self.me).wait_recv()
            for cp in self._first(w):
                cp.wait_send()
            for j in range(3):
                self._passed(j, w).wait_send()
            self._mine(w).wait()


def _gather_out_shapes(shards):
    return tuple(jax.ShapeDtypeStruct((N_DEV,) + s.shape, s.dtype) for s in shards)


def gather_shards(shards, *, name):
    n = len(shards)

    def body(*refs):
        g = _Gather(refs[:n], refs[n:2 * n], *refs[2 * n:])
        g.start()
        g.forward()
        g.finish()

    return pl.pallas_call(
        body, name=name, out_shape=_gather_out_shapes(shards), in_specs=[ANY] * n, out_specs=(ANY,) * n,
        scratch_shapes=_gather_scratch(n), compiler_params=pltpu.CompilerParams(has_side_effects=True),
    )(*shards)


CHIP_SCRATCH = [pltpu.SemaphoreType.DMA((3, N_MATS)), pltpu.SemaphoreType.DMA((3, N_MATS))]


class _ChipExchange:
    def __init__(self, items, part_refs, land_refs, send_sems, recv_sems):
        x, y, c = _place()
        chips = [(1 - x, y), (x, 1 - y), (1 - x, 1 - y)]
        self.copies = [pltpu.make_async_remote_copy(
            src_ref=part_refs[w].at[layer, 2 * px + py], dst_ref=land_refs[w].at[layer, j],
            send_sem=send_sems.at[j, w], recv_sem=recv_sems.at[j, w], device_id=(px, py, c), device_id_type=MESH)
            for layer, w in items for j, (px, py) in enumerate(chips)]

    def start(self):
        for cp in self.copies:
            cp.start()

    def finish(self):
        for cp in self.copies:
            cp.wait_recv()
        for cp in self.copies:
            cp.wait_send()


def chip_exchange(items, parts, lands, *, name):
    def body(*refs):
        ex = _ChipExchange(items, refs[:N_MATS], refs[2 * N_MATS:3 * N_MATS], *refs[3 * N_MATS:])
        ex.start()
        ex.finish()

    return pl.pallas_call(
        body, name=name, out_shape=tuple(jax.ShapeDtypeStruct(a.shape, a.dtype) for a in lands),
        in_specs=[ANY] * (2 * N_MATS), out_specs=(ANY,) * N_MATS,
        input_output_aliases={N_MATS + w: w for w in range(N_MATS)}, scratch_shapes=CHIP_SCRATCH,
        compiler_params=pltpu.CompilerParams(has_side_effects=True),
    )(*parts, *lands)


def pair_exchange(grads, *, name):
    n = len(grads)

    def body(*refs):
        g_refs, land_refs = refs[:n], refs[n:2 * n]
        send_sems, recv_sems = refs[2 * n:]
        x, y, c = _place()
        copies = [pltpu.make_async_remote_copy(
            src_ref=g_refs[w].at[2 * k + (1 - c)], dst_ref=land_refs[w].at[k], send_sem=send_sems.at[k, w],
            recv_sem=recv_sems.at[k, w], device_id=(x, y, 1 - c), device_id_type=MESH)
            for w in range(n) for k in range(N_CHIPS)]
        for cp in copies:
            cp.start()
        for cp in copies:
            cp.wait_recv()
        for cp in copies:
            cp.wait_send()

    return pl.pallas_call(
        body, name=name,
        out_shape=tuple(jax.ShapeDtypeStruct((N_CHIPS,) + g.shape[1:], g.dtype) for g in grads),
        in_specs=[ANY] * n, out_specs=(ANY,) * n,
        scratch_shapes=[pltpu.SemaphoreType.DMA((N_CHIPS, n)), pltpu.SemaphoreType.DMA((N_CHIPS, n))],
        compiler_params=pltpu.CompilerParams(has_side_effects=True),
    )(*grads)


PAIR_SUM_CHUNKS = 8


def pair_sum(layer, grads, landed, parts, core, *, name):
    n = len(grads)

    def body(c_ref, *refs):
        g_refs, l_refs, o_refs = refs[:n], refs[n:2 * n], refs[3 * n:]
        for w in range(n):
            o_refs[w][...] = (g_refs[w][...].astype(F32) + l_refs[w][...].astype(F32)).astype(BF16)

    def blk(g):
        return (None, g.shape[1] // PAIR_SUM_CHUNKS, g.shape[2])

    in_specs = [pl.BlockSpec(blk(g), lambda k, i, c_ref: (2 * k + c_ref[0], i, 0)) for g in grads]
    in_specs += [pl.BlockSpec(blk(g), lambda k, i, c_ref: (k, i, 0)) for g in grads]
    in_specs += [ANY] * n
    out_specs = tuple(pl.BlockSpec((None,) + blk(g), lambda k, i, c_ref: (layer, k, i, 0)) for g in grads)
    return list(pl.pallas_call(
        body, name=name, out_shape=tuple(jax.ShapeDtypeStruct(p.shape, p.dtype) for p in parts),
        grid_spec=pltpu.PrefetchScalarGridSpec(num_scalar_prefetch=1, grid=(N_CHIPS, PAIR_SUM_CHUNKS),
                                               in_specs=in_specs, out_specs=out_specs),
        input_output_aliases={1 + 2 * n + w: w for w in range(n)},
        compiler_params=_params("parallel", "parallel"),
    )(core, *grads, *landed, *parts))


def _adamw(w, g, m, v):
    m = ADAM_B1 * m + (1.0 - ADAM_B1) * g
    v = ADAM_B2 * v + (1.0 - ADAM_B2) * (g * g)
    m_hat = m / (1.0 - ADAM_B1 ** ADAM_STEP)
    v_hat = v / (1.0 - ADAM_B2 ** ADAM_STEP)
    delta = -ADAM_LR * (m_hat / (jnp.sqrt(v_hat) + ADAM_EPS) + ADAM_WD * w)
    return delta, m, v


def reduce_adamw(part, land, chip, w, m, v, *, name):
    _, r, c = w.shape
    tr = min(r, 256)

    def body(k_ref, own_ref, l0_ref, l1_ref, l2_ref, w_ref, m_ref, v_ref, g_out, d_out, m_out, v_out):
        g = own_ref[...].astype(F32) + l0_ref[...].astype(F32) + l1_ref[...].astype(F32) + l2_ref[...].astype(F32)
        delta, m_new, v_new = _adamw(w_ref[...], g, m_ref[...], v_ref[...])
        g_out[...] = g
        d_out[...] = delta
        m_out[...] = m_new
        v_out[...] = v_new

    row = pl.BlockSpec((None, tr, c), lambda l, i, k_ref: (l, i, 0))

    def slot(j):
        return pl.BlockSpec((None, None, tr, c), lambda l, i, k_ref: (l, j, i, 0))

    return pl.pallas_call(
        body, name=name, out_shape=(jax.ShapeDtypeStruct(w.shape, F32),) * 4,
        grid_spec=pltpu.PrefetchScalarGridSpec(
            num_scalar_prefetch=1, grid=(DEPTH, r // tr),
            in_specs=[pl.BlockSpec((None, None, tr, c), lambda l, i, k_ref: (l, k_ref[0], i, 0)), slot(0), slot(1),
                      slot(2), row, row, row],
            out_specs=(row, row, row, row)),
        compiler_params=_params("parallel", "parallel"),
    )(chip, part, land, land, land, w, m, v)


def gather_small(block, *, name):
    def body(x_ref, out_ref, send_sems, recv_sems, local_sem):
        x, y, c = _place()
        me = 4 * x + 2 * y + c
        mine = pltpu.make_async_copy(x_ref, out_ref.at[me], local_sem)
        mine.start()
        peers = [(x ^ (k >> 2), y ^ ((k >> 1) & 1), c ^ (k & 1)) for k in range(1, N_DEV)]
        copies = [pltpu.make_async_remote_copy(
            src_ref=x_ref, dst_ref=out_ref.at[me], send_sem=send_sems.at[k], recv_sem=recv_sems.at[k],
            device_id=peer, device_id_type=MESH) for k, peer in enumerate(peers)]
        for cp in copies:
            cp.start()
        for k, (px, py, pc) in enumerate(peers):
            pltpu.make_async_remote_copy(
                src_ref=x_ref, dst_ref=out_ref.at[4 * px + 2 * py + pc], send_sem=send_sems.at[k],
                recv_sem=recv_sems.at[k], device_id=(px, py, pc), device_id_type=MESH).wait_recv()
        for cp in copies:
            cp.wait_send()
        mine.wait()

    return pl.pallas_call(
        body, name=name, out_shape=jax.ShapeDtypeStruct((N_DEV,) + block.shape, block.dtype),
        in_specs=[ANY], out_specs=ANY,
        scratch_shapes=[pltpu.SemaphoreType.DMA((7,)), pltpu.SemaphoreType.DMA((7,)), pltpu.SemaphoreType.DMA],
        compiler_params=pltpu.CompilerParams(has_side_effects=True),
    )(block)


def small_adamw(gathered, w, m, v, *, name):
    def body(g_ref, w_ref, m_ref, v_ref, g_out, d_out, m_out, v_out):
        g = g_ref[0]
        for d in range(1, N_DEV):
            g = g + g_ref[d]
        delta, m_new, v_new = _adamw(w_ref[...], g, m_ref[...], v_ref[...])
        g_out[...] = g
        d_out[...] = delta
        m_out[...] = m_new
        v_out[...] = v_new

    return pl.pallas_call(
        body, name=name, out_shape=(jax.ShapeDtypeStruct(w.shape, F32),) * 4,
    )(gathered, w, m, v)


def norm_matmul(x, g, w, *, gate_split, name):
    s, d = x.shape
    tm = min(ROW_TILE, s)
    blocked = w.ndim == 3
    n = w.shape[1] if not blocked else w.shape[0] * w.shape[2]

    def body(x_ref, g_ref, w_ref, h_ref, *outs):
        xv = x_ref[...]
        h = ((xv * _rsqrt_ms(xv)) * g_ref[...]).astype(BF16)
        h_ref[...] = h
        if blocked:
            nb = w_ref.shape[2]
            for j in range(w_ref.shape[0]):
                outs[0][:, j * nb:(j + 1) * nb] = _dot(h, w_ref[j]).astype(BF16)
        else:
            p = _dot(h, w_ref[...])
            outs[0][...] = p[:, :gate_split].astype(BF16)
            outs[1][...] = (1.0 / (1.0 + jnp.exp(-p[:, gate_split:]))).astype(BF16)

    row = lambda i: (i, 0)
    fixed = lambda i: (0, 0)
    if blocked:
        out_shape = (jax.ShapeDtypeStruct((s, d), BF16), jax.ShapeDtypeStruct((s, n), BF16))
        out_specs = (pl.BlockSpec((tm, d), row), pl.BlockSpec((tm, n), row))
        w_spec = pl.BlockSpec(w.shape, lambda i: (0, 0, 0))
    else:
        out_shape = (jax.ShapeDtypeStruct((s, d), BF16), jax.ShapeDtypeStruct((s, gate_split), BF16),
                     jax.ShapeDtypeStruct((s, n - gate_split), BF16))
        out_specs = (pl.BlockSpec((tm, d), row), pl.BlockSpec((tm, gate_split), row),
                     pl.BlockSpec((tm, n - gate_split), row))
        w_spec = pl.BlockSpec((d, n), fixed)
    return pl.pallas_call(
        body, name=name, grid=(s // tm,), out_shape=out_shape,
        in_specs=[pl.BlockSpec((tm, d), row), pl.BlockSpec((1, d), fixed), w_spec],
        out_specs=out_specs, compiler_params=_params("parallel"),
    )(x, g, w)


def merge_out_fwd(x, o_sb, o_sw, gates, w_bsb, w_bsw, w_o, *, name):
    s, d = x.shape
    tm = min(ROW_TILE, s)

    def body(x_ref, osb_ref, osw_ref, g_ref, wsb_ref, wsw_ref, wo_ref, x1_ref, ysb_ref, ysw_ref, mg_ref):
        y_sb = _dot(osb_ref[...].astype(BF16), wsb_ref[...])
        y_sw = _dot(osw_ref[...].astype(BF16), wsw_ref[...])
        g = g_ref[...].astype(F32)
        merged = (g[:, :d] * y_sb + g[:, d:] * y_sw).astype(BF16)
        ysb_ref[...] = y_sb.astype(BF16)
        ysw_ref[...] = y_sw.astype(BF16)
        mg_ref[...] = merged
        x1_ref[...] = x_ref[...] + _dot(merged, wo_ref[...])

    row = lambda i: (i, 0)
    fixed = lambda i: (0, 0)
    wd = o_sb.shape[1]
    return pl.pallas_call(
        body, name=name, grid=(s // tm,),
        out_shape=(jax.ShapeDtypeStruct((s, d), F32),) + (jax.ShapeDtypeStruct((s, d), BF16),) * 3,
        in_specs=[pl.BlockSpec((tm, d), row), pl.BlockSpec((tm, wd), row), pl.BlockSpec((tm, wd), row),
                  pl.BlockSpec((tm, 2 * d), row), pl.BlockSpec((wd, d), fixed), pl.BlockSpec((wd, d), fixed),
                  pl.BlockSpec((d, d), fixed)],
        out_specs=(pl.BlockSpec((tm, d), row),) * 4, compiler_params=_params("parallel"),
    )(x, o_sb, o_sw, gates, w_bsb, w_bsw, w_o)


def mlp_down_fwd(x1, u, w_down, *, name):
    s, d = x1.shape
    f = u.shape[1]
    tm = min(ROW_TILE, s)

    def body(x_ref, u_ref, w_ref, o_ref):
        a = jnp.maximum(u_ref[...].astype(F32), 0.0)
        o_ref[...] = x_ref[...] + _dot((a * a).astype(BF16), w_ref[...])

    row = lambda i: (i, 0)
    return pl.pallas_call(
        body, name=name, grid=(s // tm,), out_shape=jax.ShapeDtypeStruct((s, d), F32),
        in_specs=[pl.BlockSpec((tm, d), row), pl.BlockSpec((tm, f), row), pl.BlockSpec((f, d), lambda i: (0, 0))],
        out_specs=pl.BlockSpec((tm, d), row), compiler_params=_params("parallel"),
    )(x1, u, w_down)


def loss_head(y, target, *, name):
    s, d = y.shape
    tm = min(ROW_TILE, s)

    def body(y_ref, t_ref, dy_ref, dyb_ref, loss_ref):
        @pl.when(pl.program_id(0) == 0)
        def _():
            loss_ref[...] = jnp.zeros_like(loss_ref)

        e = y_ref[...] - t_ref[...]
        dy = e * (1.0 / d)
        dy_ref[...] = dy
        dyb_ref[...] = dy.astype(BF16)
        per_row = jnp.sum(e * e, axis=1, keepdims=True) * (0.5 / d)
        loss_ref[...] += jnp.sum(per_row, axis=0, keepdims=True)

    row = lambda i: (i, 0)
    return pl.pallas_call(
        body, name=name, grid=(s // tm,),
        out_shape=(jax.ShapeDtypeStruct((s, d), F32), jax.ShapeDtypeStruct((s, d), BF16),
                   jax.ShapeDtypeStruct((1, 1), F32)),
        in_specs=[pl.BlockSpec((tm, d), row), pl.BlockSpec((tm, d), row)],
        out_specs=(pl.BlockSpec((tm, d), row), pl.BlockSpec((tm, d), row), pl.BlockSpec((1, 1), lambda i: (0, 0))),
        compiler_params=_params("arbitrary"),
    )(y, target)


def mlp_bwd_up(dxb, u, w_down, *, name):
    s, d = dxb.shape
    f = u.shape[1]
    tm = min(ROW_TILE, s)

    def body(dx_ref, u_ref, w_ref, du_ref):
        da = _dot_nt(dx_ref[...], w_ref[...])
        du_ref[...] = (da * (2.0 * jnp.maximum(u_ref[...].astype(F32), 0.0))).astype(BF16)

    row = lambda i: (i, 0)
    return pl.pallas_call(
        body, name=name, grid=(s // tm,), out_shape=jax.ShapeDtypeStruct((s, f), BF16),
        in_specs=[pl.BlockSpec((tm, d), row), pl.BlockSpec((tm, f), row), pl.BlockSpec((f, d), lambda i: (0, 0))],
        out_specs=pl.BlockSpec((tm, f), row), compiler_params=_params("parallel"),
    )(dxb, u, w_down)


def matmul_nt_norm_bwd(pieces, w, x, g, dres, *, name):
    s = x.shape[0]
    d = x.shape[1]
    tm = min(ROW_TILE, s)
    blocked = w.ndim == 3
    n_pieces = len(pieces)
    widths = [p.shape[1] for p in pieces]

    def body(*refs):
        p_refs = refs[:n_pieces]
        w_ref, x_ref, g_ref, dres_ref, dx_ref, dxb_ref, dg_ref = refs[n_pieces:]

        @pl.when(pl.program_id(0) == 0)
        def _():
            dg_ref[...] = jnp.zeros_like(dg_ref)

        if blocked:
            nb = w_ref.shape[2]
            dh = _dot_nt(p_refs[0][:, :nb], w_ref[0])
            for j in range(1, w_ref.shape[0]):
                dh = dh + _dot_nt(p_refs[0][:, j * nb:(j + 1) * nb], w_ref[j])
        else:
            dh, off = None, 0
            for p_ref, width in zip(p_refs, widths):
                part = _dot_nt(p_ref[...], w_ref[:, off:off + width])
                dh = part if dh is None else dh + part
                off += width
        xv = x_ref[...]
        r = _rsqrt_ms(xv)
        dyg = dh * g_ref[...]
        dx = dres_ref[...] + r * dyg - xv * ((r * r * r) * jnp.mean(dyg * xv, axis=-1, keepdims=True))
        dx_ref[...] = dx
        dxb_ref[...] = dx.astype(BF16)
        dg_ref[...] += jnp.sum(dh * (xv * r), axis=0, keepdims=True)

    row = lambda i: (i, 0)
    fixed = lambda i: (0, 0)
    w_spec = pl.BlockSpec(w.shape, (lambda i: (0, 0, 0)) if blocked else fixed)
    return pl.pallas_call(
        body, name=name, grid=(s // tm,),
        out_shape=(jax.ShapeDtypeStruct((s, d), F32), jax.ShapeDtypeStruct((s, d), BF16),
                   jax.ShapeDtypeStruct((1, d), F32)),
        in_specs=[pl.BlockSpec((tm, width), row) for width in widths] + [
            w_spec, pl.BlockSpec((tm, d), row), pl.BlockSpec((1, d), fixed), pl.BlockSpec((tm, d), row)],
        out_specs=(pl.BlockSpec((tm, d), row), pl.BlockSpec((tm, d), row), pl.BlockSpec((1, d), fixed)),
        compiler_params=_params("arbitrary"),
    )(*pieces, w, x, g, dres)


def out_bwd(dx1b, w_o, gates, y_sb, y_sw, w_bsb, w_bsw, *, name):
    s, d = dx1b.shape
    wd = w_bsb.shape[0]
    tm = min(ROW_TILE, s)

    def body(dx_ref, wo_ref, g_ref, ysb_ref, ysw_ref, wsb_ref, wsw_ref, dysb_ref, dysw_ref, dosb_ref, dosw_ref, dgl_ref):
        dm = _dot_nt(dx_ref[...], wo_ref[...])
        g = g_ref[...].astype(F32)
        g0, g1 = g[:, :d], g[:, d:]
        dy_sb = (dm * g0).astype(BF16)
        dy_sw = (dm * g1).astype(BF16)
        dysb_ref[...] = dy_sb
        dysw_ref[...] = dy_sw
        dosb_ref[...] = _dot_nt(dy_sb, wsb_ref[...])
        dosw_ref[...] = _dot_nt(dy_sw, wsw_ref[...])
        dgl_ref[:, :d] = (dm * ysb_ref[...].astype(F32) * (g0 * (1.0 - g0))).astype(BF16)
        dgl_ref[:, d:] = (dm * ysw_ref[...].astype(F32) * (g1 * (1.0 - g1))).astype(BF16)

    row = lambda i: (i, 0)
    fixed = lambda i: (0, 0)
    return pl.pallas_call(
        body, name=name, grid=(s // tm,),
        out_shape=(jax.ShapeDtypeStruct((s, d), BF16), jax.ShapeDtypeStruct((s, d), BF16),
                   jax.ShapeDtypeStruct((s, wd), F32), jax.ShapeDtypeStruct((s, wd), F32),
                   jax.ShapeDtypeStruct((s, 2 * d), BF16)),
        in_specs=[pl.BlockSpec((tm, d), row), pl.BlockSpec((d, d), fixed), pl.BlockSpec((tm, 2 * d), row),
                  pl.BlockSpec((tm, d), row), pl.BlockSpec((tm, d), row), pl.BlockSpec((wd, d), fixed),
                  pl.BlockSpec((wd, d), fixed)],
        out_specs=(pl.BlockSpec((tm, d), row), pl.BlockSpec((tm, d), row), pl.BlockSpec((tm, wd), row),
                   pl.BlockSpec((tm, wd), row), pl.BlockSpec((tm, 2 * d), row)),
        compiler_params=_params("parallel"),
    )(dx1b, w_o, gates, y_sb, y_sw, w_bsb, w_bsw)


def matmul_tn(a, pieces, *, a_block, out_cols, relu2, name):
    s, m = a.shape
    widths = [p.shape[1] for p in pieces]
    n = sum(widths)
    n_pieces = len(pieces)
    ts = min(512 if n >= 4096 else 2048, s)
    n_steps = s // ts
    if out_cols is None:
        out_shape = jax.ShapeDtypeStruct((m // a_block, a_block, n), BF16)
        out_spec = pl.BlockSpec((None, a_block, n), lambda i, k: (i, 0, 0))
    else:
        out_shape = jax.ShapeDtypeStruct((n // out_cols, m, out_cols), BF16)
        out_spec = pl.BlockSpec((n // out_cols, a_block, out_cols), lambda i, k: (0, i, 0))

    def body(a_ref, *refs):
        b_refs, o_ref, acc = refs[:n_pieces], refs[n_pieces], refs[n_pieces + 1]
        k = pl.program_id(1)

        @pl.when(k == 0)
        def _():
            acc[...] = jnp.zeros_like(acc)

        av = a_ref[...]
        if relu2:
            af = jnp.maximum(av.astype(F32), 0.0)
            av = af * af
        av = av.astype(BF16)
        off = 0
        for b_ref in b_refs:
            width = b_ref.shape[1]
            acc[:, off:off + width] += _dot_tn(av, b_ref[...].astype(BF16))
            off += width

        @pl.when(k == n_steps - 1)
        def _():
            if out_cols is None:
                o_ref[...] = acc[...].astype(BF16)
            else:
                for j in range(n // out_cols):
                    o_ref[j] = acc[:, j * out_cols:(j + 1) * out_cols].astype(BF16)

    return pl.pallas_call(
        body, name=name, grid=(m // a_block, n_steps), out_shape=out_shape,
        in_specs=[pl.BlockSpec((ts, a_block), lambda i, k: (k, i))] + [
            pl.BlockSpec((ts, width), lambda i, k: (k, 0)) for width in widths],
        out_specs=out_spec, scratch_shapes=[pltpu.VMEM((a_block, n), F32)],
        compiler_params=_params("parallel", "arbitrary"),
    )(a, *pieces)


def _softplus(z):
    return jnp.maximum(z, 0.0) + jnp.log1p(jnp.exp(-jnp.abs(z)))


def _head_mask(h):
    return (lax.broadcasted_iota(jnp.int32, (1, LANES), 1) // HEAD_DIM) == h


def _stack_heads(x):
    zero = jnp.zeros_like(x)
    return jnp.concatenate([jnp.where(_head_mask(0), x, zero), jnp.where(_head_mask(1), x, zero)], axis=0)


def _unstack_heads(r, t):
    return jnp.where(_head_mask(0), r[:t], r[t:])


def _strict_lower2(t):
    row = lax.broadcasted_iota(jnp.int32, (2 * t, t), 0)
    col = lax.broadcasted_iota(jnp.int32, (2 * t, t), 1)
    return col < jnp.where(row >= t, row - t, row)


def sb_attn_fwd(proj, tri, next_shards, *, name):
    s = proj.shape[0]
    t = min(SB_TILE, s)
    nq = s // t
    n_pairs = SB_WIDTH // LANES
    hosted = next_shards is not None
    n_sh = len(next_shards) if hosted else 0

    def body(q_ref, k_ref, v_ref, tri_ref, *refs):
        if hosted:
            gather = _Gather(refs[:n_sh], refs[n_sh + 1:2 * n_sh + 1], *refs[2 * n_sh + 1:])
            o_ref = refs[n_sh]
            p = pl.program_id(0)

            @pl.when(p == 0)
            def _():
                gather.start()

            @pl.when(p == n_pairs - 1)
            def _():
                gather.forward()
        else:
            o_ref = refs[0]
        strict = _strict_lower2(t)

        def q_tile(qb, carry):
            q0 = pl.multiple_of(qb * t, t)
            qh = _stack_heads(q_ref[pl.ds(q0, t), :])
            z = _dot_nt(qh, k_ref[pl.ds(q0, t), :]) * SCALE
            sp = _softplus(z)
            lk = jnp.where(strict, -sp, 0.0)
            w = jnp.where(strict, jnp.exp(z - sp + _dot2(lk, tri_ref[...])), 0.0)
            acc = _dot(w.astype(BF16), v_ref[pl.ds(q0, t), :])
            c = jnp.sum(lk, axis=1, keepdims=True)

            def cond(st):
                return jnp.logical_and(st[0] >= 0, st[3] > SB_CUTOFF)

            def step(st):
                kb, c, acc, _ = st
                k0 = pl.multiple_of(kb * t, t)
                z = _dot_nt(qh, k_ref[pl.ds(k0, t), :]) * SCALE
                lk = -_softplus(z)
                w = jnp.exp(z + lk + _dot2(lk, tri_ref[...]) + c)
                acc = acc + _dot(w.astype(BF16), v_ref[pl.ds(k0, t), :])
                c = c + jnp.sum(lk, axis=1, keepdims=True)
                return kb - 1, c, acc, jnp.max(c)

            _, _, acc, _ = lax.while_loop(cond, step, (qb - 1, c, acc, jnp.max(c)))
            o_ref[pl.ds(q0, t), :] = _unstack_heads(acc, t)
            return carry

        lax.fori_loop(0, nq, q_tile, 0)

        if hosted:
            @pl.when(p == n_pairs - 1)
            def _():
                gather.finish()

    def col(j):
        return pl.BlockSpec((s, LANES), lambda p: (0, j * n_pairs + p))

    o_shape = jax.ShapeDtypeStruct((s, SB_WIDTH), F32)
    o_spec = pl.BlockSpec((s, LANES), lambda p: (0, p))
    in_specs = [col(0), col(1), col(2), pl.BlockSpec((t, t), lambda p: (0, 0))]
    if not hosted:
        return pl.pallas_call(
            body, name=name, grid=(n_pairs,), out_shape=o_shape, in_specs=in_specs, out_specs=o_spec,
            compiler_params=_params("arbitrary"),
        )(proj, proj, proj, tri)
    outs = pl.pallas_call(
        body, name=name, grid=(n_pairs,), out_shape=(o_shape,) + _gather_out_shapes(next_shards),
        in_specs=in_specs + [ANY] * n_sh, out_specs=(o_spec,) + (ANY,) * n_sh, scratch_shapes=_gather_scratch(n_sh),
        compiler_params=pltpu.CompilerParams(dimension_semantics=("arbitrary",), vmem_limit_bytes=VMEM_LIMIT,
                                             has_side_effects=True),
    )(proj, proj, proj, tri, *next_shards)
    return outs[0], list(outs[1:])


def sb_attn_bwd(proj, tri, o, do, exchange, *, name):
    s = proj.shape[0]
    t = min(SB_TILE, s)
    nq = s // t
    n_pairs = SB_WIDTH // LANES
    hosted = exchange is not None

    def body(q_ref, k_ref, v_ref, tri_ref, o_ref, do_ref, *refs):
        if hosted:
            dq_ref, dk_ref, dv_ref = refs[2 * N_MATS:2 * N_MATS + 3]
            land_refs = refs[2 * N_MATS + 3:3 * N_MATS + 3]
            send_sems, recv_sems, dk_acc, dv_acc = refs[3 * N_MATS + 3:]
            ex = _ChipExchange(exchange[0], refs[:N_MATS], land_refs, send_sems, recv_sems)
            p = pl.program_id(0)

            @pl.when(p == 0)
            def _():
                ex.start()
        else:
            dq_ref, dk_ref, dv_ref, dk_acc, dv_acc = refs
        strict = _strict_lower2(t)
        dk_acc[...] = jnp.zeros_like(dk_acc)
        dv_acc[...] = jnp.zeros_like(dv_acc)

        def block(qh, doh_b, dd, k0, c, ce, diag):
            kt = k_ref[pl.ds(k0, t), :]
            vt = v_ref[pl.ds(k0, t), :]
            z = _dot_nt(qh, kt) * SCALE
            sp = _softplus(z)
            lb = z - sp
            lk = jnp.where(strict, -sp, 0.0) if diag else -sp
            w = jnp.exp(lb + _dot2(lk, tri_ref[...]) + c)
            if diag:
                w = jnp.where(strict, w, 0.0)
            wb = w.astype(BF16)
            e = wb.astype(F32) * _dot_nt(doh_b, vt)
            dz = e - jnp.exp(lb) * (dd - ce - _dot2(e, tri_ref[...]))
            if diag:
                dz = jnp.where(strict, dz, 0.0)
            dzb = (dz * SCALE).astype(BF16)
            dk_acc[pl.ds(k0, t), :] += _dot_tn(dzb, qh)
            dv_acc[pl.ds(k0, t), :] += _dot_tn(wb, doh_b)
            return (_dot(dzb, kt), c + jnp.sum(lk, axis=1, keepdims=True), ce + jnp.sum(e, axis=1, keepdims=True))

        def q_tile(qb, carry):
            q0 = pl.multiple_of(qb * t, t)
            qh = _stack_heads(q_ref[pl.ds(q0, t), :])
            doh_b = _stack_heads(do_ref[pl.ds(q0, t), :].astype(BF16))
            ov = o_ref[pl.ds(q0, t), :]
            dd = jnp.sum(doh_b.astype(F32) * jnp.concatenate([ov, ov], axis=0), axis=1, keepdims=True)
            zero = jnp.zeros((2 * t, 1), F32)
            dq, c, ce = block(qh, doh_b, dd, q0, zero, zero, True)

            def cond(st):
                return jnp.logical_and(st[0] >= 0, st[4] > SB_CUTOFF)

            def step(st):
                kb, c, ce, dq, _ = st
                ddq, c, ce = block(qh, doh_b, dd, pl.multiple_of(kb * t, t), c, ce, False)
                return kb - 1, c, ce, dq + ddq, jnp.max(c)

            st = lax.while_loop(cond, step, (qb - 1, c, ce, dq, jnp.max(c)))
            dq_ref[pl.ds(q0, t), :] = _unstack_heads(st[3], t).astype(BF16)
            return carry

        lax.fori_loop(0, nq, q_tile, 0)
        dk_ref[...] = dk_acc[...].astype(BF16)
        dv_ref[...] = dv_acc[...].astype(BF16)

        if hosted:
            @pl.when(p == n_pairs - 1)
            def _():
                ex.finish()

    def col(j):
        return pl.BlockSpec((s, LANES), lambda p: (0, j * n_pairs + p))

    pair = pl.BlockSpec((s, LANES), lambda p: (0, p))
    in_specs = [col(0), col(1), col(2), pl.BlockSpec((t, t), lambda p: (0, 0)), pair, pair]
    d_shapes = (jax.ShapeDtypeStruct((s, SB_WIDTH), BF16),) * 3
    acc_scratch = [pltpu.VMEM((s, LANES), F32), pltpu.VMEM((s, LANES), F32)]
    if not hosted:
        return pl.pallas_call(
            body, name=name, grid=(n_pairs,), out_shape=d_shapes, in_specs=in_specs, out_specs=(pair, pair, pair),
            scratch_shapes=acc_scratch, compiler_params=_params("arbitrary"),
        )(proj, proj, proj, tri, o, do)
    _, parts, lands = exchange
    outs = pl.pallas_call(
        body, name=name, grid=(n_pairs,),
        out_shape=d_shapes + tuple(jax.ShapeDtypeStruct(a.shape, a.dtype) for a in lands),
        in_specs=in_specs + [ANY] * (2 * N_MATS), out_specs=(pair, pair, pair) + (ANY,) * N_MATS,
        input_output_aliases={6 + N_MATS + w: 3 + w for w in range(N_MATS)},
        scratch_shapes=CHIP_SCRATCH + acc_scratch,
        compiler_params=pltpu.CompilerParams(dimension_semantics=("arbitrary",), vmem_limit_bytes=VMEM_LIMIT,
                                             has_side_effects=True),
    )(proj, proj, proj, tri, o, do, *parts, *lands)
    return outs[0], outs[1], outs[2], list(outs[3:])


def _lane_lo():
    return lax.broadcasted_iota(jnp.int32, (1, LANES), 1) < HEAD_DIM


def _swap_halves(x):
    return pltpu.roll(x, HEAD_DIM, 1)


def _rot_half(y):
    first = (lax.broadcasted_iota(jnp.int32, (1, LANES), 1) % HEAD_DIM) < (HEAD_DIM // 2)
    return jnp.where(first, pltpu.roll(y, LANES - HEAD_DIM // 2, 1), pltpu.roll(y, HEAD_DIM // 2, 1))


def _head_mean(v):
    lo = _lane_lo()
    s0 = jnp.sum(jnp.where(lo, v, 0.0), axis=1, keepdims=True)
    s1 = jnp.sum(jnp.where(lo, 0.0, v), axis=1, keepdims=True)
    return jnp.where(lo, s0, s1) * (1.0 / HEAD_DIM)


def swa_prep_fwd(proj, cos_p, sin_p, gq, gk, *, name):
    s = proj.shape[0]
    tm = min(512, s)
    q_blk = (3 * SB_WIDTH) // SWA_Q_WIDTH
    k_blk = (3 * SB_WIDTH + SWA_Q_WIDTH) // LANES

    def norm_rope(xv, g, cosv, sinv):
        y = (xv * lax.rsqrt(_head_mean(xv * xv) + NORM_EPS)) * g
        return y * cosv + _rot_half(y) * sinv

    def body(q_ref, k_ref, cos_ref, sin_ref, gq_ref, gk_ref, qn_ref, kn_ref):
        cosv, sinv = cos_ref[...], sin_ref[...]
        for j in range(SWA_Q_WIDTH // LANES):
            sl = slice(j * LANES, (j + 1) * LANES)
            qn_ref[:, sl] = norm_rope(q_ref[:, sl].astype(F32), gq_ref[...], cosv, sinv).astype(BF16)
        kn_ref[...] = norm_rope(k_ref[...].astype(F32), gk_ref[...], cosv, sinv).astype(BF16)

    row = lambda i: (i, 0)
    fixed = lambda i: (0, 0)
    return pl.pallas_call(
        body, name=name, grid=(s // tm,),
        out_shape=(jax.ShapeDtypeStruct((s, SWA_Q_WIDTH), BF16), jax.ShapeDtypeStruct((s, LANES), BF16)),
        in_specs=[pl.BlockSpec((tm, SWA_Q_WIDTH), lambda i: (i, q_blk)), pl.BlockSpec((tm, LANES), lambda i: (i, k_blk)),
                  pl.BlockSpec((tm, LANES), row), pl.BlockSpec((tm, LANES), row),
                  pl.BlockSpec((1, LANES), fixed), pl.BlockSpec((1, LANES), fixed)],
        out_specs=(pl.BlockSpec((tm, SWA_Q_WIDTH), row), pl.BlockSpec((tm, LANES), row)),
        compiler_params=_params("parallel"),
    )(proj, proj, cos_p, sin_p, gq, gk)


def swa_prep_bwd(proj, cos_p, sin_p, gq, gk, dqn, dkn, dv, *, name):
    s = proj.shape[0]
    tm = min(512, s)
    q_blk = (3 * SB_WIDTH) // SWA_Q_WIDTH
    k_blk = (3 * SB_WIDTH + SWA_Q_WIDTH) // LANES

    def bwd(xv, g, cosv, sinv, dout):
        dy = dout * cosv + _rot_half(dout * sinv)
        r = lax.rsqrt(_head_mean(xv * xv) + NORM_EPS)
        dyg = dy * g
        dx = r * dyg - xv * ((r * r * r) * _head_mean(dyg * xv))
        return dx, jnp.sum(dy * (xv * r), axis=0, keepdims=True)

    def body(q_ref, k_ref, cos_ref, sin_ref, gq_ref, gk_ref, dqn_ref, dkn_ref, dv_ref, dq_ref, dk_ref, dvb_ref,
             dgq_ref, dgk_ref):
        @pl.when(pl.program_id(0) == 0)
        def _():
            dgq_ref[...] = jnp.zeros_like(dgq_ref)
            dgk_ref[...] = jnp.zeros_like(dgk_ref)

        cosv, sinv = cos_ref[...], sin_ref[...]
        for j in range(SWA_Q_WIDTH // LANES):
            sl = slice(j * LANES, (j + 1) * LANES)
            dx, dg = bwd(q_ref[:, sl].astype(F32), gq_ref[...], cosv, sinv, dqn_ref[:, sl])
            dq_ref[:, sl] = dx.astype(BF16)
            dgq_ref[:, sl] += dg
        dx, dg = bwd(k_ref[...].astype(F32), gk_ref[...], cosv, sinv, dkn_ref[...])
        dk_ref[...] = dx.astype(BF16)
        dgk_ref[...] += dg
        dvb_ref[...] = dv_ref[...].astype(BF16)

    row = lambda i: (i, 0)
    fixed = lambda i: (0, 0)
    lane_row = pl.BlockSpec((tm, LANES), row)
    return pl.pallas_call(
        body, name=name, grid=(s // tm,),
        out_shape=(jax.ShapeDtypeStruct((s, SWA_Q_WIDTH), BF16), jax.ShapeDtypeStruct((s, LANES), BF16),
                   jax.ShapeDtypeStruct((s, LANES), BF16),
                   jax.ShapeDtypeStruct((1, SWA_Q_WIDTH), F32), jax.ShapeDtypeStruct((1, LANES), F32)),
        in_specs=[pl.BlockSpec((tm, SWA_Q_WIDTH), lambda i: (i, q_blk)), pl.BlockSpec((tm, LANES), lambda i: (i, k_blk)),
                  lane_row, lane_row, pl.BlockSpec((1, LANES), fixed), pl.BlockSpec((1, LANES), fixed),
                  pl.BlockSpec((tm, SWA_Q_WIDTH), row), lane_row, lane_row],
        out_specs=(pl.BlockSpec((tm, SWA_Q_WIDTH), row), lane_row, lane_row,
                   pl.BlockSpec((1, SWA_Q_WIDTH), fixed), pl.BlockSpec((1, LANES), fixed)),
        compiler_params=_params("arbitrary"),
    )(proj, proj, cos_p, sin_p, gq, gk, dqn, dkn, dv)


def _swa_tile(i, k_ref, v_ref, second_kv):
    q0 = pl.multiple_of(i * SWA_TQ, SWA_TQ)
    k0 = pl.multiple_of(jnp.maximum(i - 1, 0) * SWA_TQ, SWA_TQ)
    keep = jnp.logical_xor(_lane_lo(), second_kv)
    kf = k_ref[pl.ds(k0, SWA_TK), :].astype(F32)
    vf = v_ref[pl.ds(k0, SWA_TK), :].astype(F32)
    kg = jnp.where(keep, kf, _swap_halves(kf)).astype(BF16)
    vg = jnp.where(keep, vf, _swap_halves(vf)).astype(BF16)
    row = lax.broadcasted_iota(jnp.int32, (2 * SWA_TQ, SWA_TK), 0)
    tpos = q0 + jnp.where(row >= SWA_TQ, row - SWA_TQ, row)
    spos = k0 + lax.broadcasted_iota(jnp.int32, (2 * SWA_TQ, SWA_TK), 1)
    valid = jnp.logical_and(spos <= tpos, spos > tpos - WINDOW)
    return q0, k0, kg, vg, valid


def _swa_probs(qh, kg, valid, sink):
    z = jnp.where(valid, _dot_nt(qh, kg) * SCALE, NEG)
    m = jnp.maximum(jnp.max(z, axis=1, keepdims=True), sink)
    pexp = jnp.exp(z - m)
    psink = jnp.exp(sink - m)
    inv = 1.0 / (jnp.sum(pexp, axis=1, keepdims=True) + psink)
    return pexp * inv, psink * inv


def _stacked_sink(sink_row):
    s0 = jnp.sum(jnp.where(_head_mask(0), sink_row, 0.0), axis=1, keepdims=True) * (1.0 / HEAD_DIM)
    s1 = jnp.sum(jnp.where(_head_mask(1), sink_row, 0.0), axis=1, keepdims=True) * (1.0 / HEAD_DIM)
    top = lax.broadcasted_iota(jnp.int32, (2 * SWA_TQ, 1), 0) < SWA_TQ
    return jnp.where(top, s0, s1)


def swa_attn_fwd(qn, kn, proj, sink_p, next_shards, *, name):
    s = qn.shape[0]
    nq = s // SWA_TQ
    n_pairs = SWA_Q_WIDTH // LANES
    v_blk = (3 * SB_WIDTH + SWA_Q_WIDTH + SWA_KV_WIDTH) // LANES
    hosted = next_shards is not None
    n_sh = len(next_shards) if hosted else 0

    def body(q_ref, k_ref, v_ref, s_ref, *refs):
        p = pl.program_id(0)
        if hosted:
            gather = _Gather(refs[:n_sh], refs[n_sh + 1:2 * n_sh + 1], *refs[2 * n_sh + 1:])
            o_ref = refs[n_sh]

            @pl.when(p == 0)
            def _():
                gather.start()

            @pl.when(p == n_pairs - 1)
            def _():
                gather.forward()
        else:
            o_ref = refs[0]
        second_kv = (p // 2) == 1
        sink = _stacked_sink(s_ref[...])

        def tile(i, carry):
            q0, _, kg, vg, valid = _swa_tile(i, k_ref, v_ref, second_kv)
            probs, _ = _swa_probs(_stack_heads(q_ref[pl.ds(q0, SWA_TQ), :]), kg, valid, sink)
            o_ref[pl.ds(q0, SWA_TQ), :] = _unstack_heads(_dot(probs.astype(BF16), vg), SWA_TQ)
            return carry

        lax.fori_loop(0, nq, tile, 0, unroll=4)

        if hosted:
            @pl.when(p == n_pairs - 1)
            def _():
                gather.finish()

    pair = pl.BlockSpec((s, LANES), lambda p: (0, p))
    whole = pl.BlockSpec((s, LANES), lambda p: (0, 0))
    o_shape = jax.ShapeDtypeStruct((s, SWA_Q_WIDTH), F32)
    in_specs = [pair, whole, pl.BlockSpec((s, LANES), lambda p: (0, v_blk)),
                pl.BlockSpec((None, 1, LANES), lambda p: (p, 0, 0))]
    if not hosted:
        return pl.pallas_call(
            body, name=name, grid=(n_pairs,), out_shape=o_shape, in_specs=in_specs, out_specs=pair,
            compiler_params=_params("arbitrary"),
        )(qn, kn, proj, sink_p)
    outs = pl.pallas_call(
        body, name=name, grid=(n_pairs,), out_shape=(o_shape,) + _gather_out_shapes(next_shards),
        in_specs=in_specs + [ANY] * n_sh, out_specs=(pair,) + (ANY,) * n_sh, scratch_shapes=_gather_scratch(n_sh),
        compiler_params=pltpu.CompilerParams(dimension_semantics=("arbitrary",), vmem_limit_bytes=VMEM_LIMIT,
                                             has_side_effects=True),
    )(qn, kn, proj, sink_p, *next_shards)
    return outs[0], list(outs[1:])


def swa_attn_bwd(qn, kn, proj, sink_p, o, do, *, name):
    s = qn.shape[0]
    nq = s // SWA_TQ
    v_blk = (3 * SB_WIDTH + SWA_Q_WIDTH + SWA_KV_WIDTH) // LANES
    fold_rows = min(512, s)

    def body(q_ref, k_ref, v_ref, s_ref, o_ref, do_ref, dq_ref, dk_ref, dv_ref, ds_ref, acc_k, acc_v):
        p = pl.program_id(0)
        second_kv = (p // 2) == 1
        sink = _stacked_sink(s_ref[...])

        @pl.when(p % 2 == 0)
        def _():
            acc_k[...] = jnp.zeros_like(acc_k)
            acc_v[...] = jnp.zeros_like(acc_v)

        def tile(i, dsink):
            q0, k0, kg, vg, valid = _swa_tile(i, k_ref, v_ref, second_kv)
            qh = _stack_heads(q_ref[pl.ds(q0, SWA_TQ), :])
            doh = _stack_heads(do_ref[pl.ds(q0, SWA_TQ), :])
            doh_b = doh.astype(BF16)
            ov = o_ref[pl.ds(q0, SWA_TQ), :]
            delta = jnp.sum(doh * jnp.concatenate([ov, ov], axis=0), axis=1, keepdims=True)
            probs, psink = _swa_probs(qh, kg, valid, sink)
            dz = probs * (_dot_nt(doh_b, vg) - delta)
            dzb = (dz * SCALE).astype(BF16)
            dq_ref[pl.ds(q0, SWA_TQ), :] = _unstack_heads(_dot(dzb, kg), SWA_TQ)
            acc_k[pl.ds(k0, SWA_TK), :] += _dot_tn(dzb, qh)
            acc_v[pl.ds(k0, SWA_TK), :] += _dot_tn(probs.astype(BF16), doh_b)
            pd = psink * delta
            return dsink - jnp.where(_head_mask(0), jnp.sum(pd[:SWA_TQ], axis=0, keepdims=True),
                                     jnp.sum(pd[SWA_TQ:], axis=0, keepdims=True))

        ds_ref[...] = lax.fori_loop(0, nq, tile, jnp.zeros((1, LANES), F32), unroll=4)

        def fold_into(first_head):
            def fold(r, carry):
                rows = pl.ds(pl.multiple_of(r * fold_rows, fold_rows), fold_rows)
                for acc, out in ((acc_k, dk_ref), (acc_v, dv_ref)):
                    a = acc[rows, :]
                    both = a + _swap_halves(a)
                    if first_head:
                        out[rows, :] = jnp.where(_lane_lo(), both, 0.0)
                    else:
                        out[rows, :] = jnp.where(_lane_lo(), out[rows, :], both)
                return carry

            lax.fori_loop(0, s // fold_rows, fold, 0)

        @pl.when(p == 1)
        def _():
            fold_into(True)

        @pl.when(p == 3)
        def _():
            fold_into(False)

    pair = pl.BlockSpec((s, LANES), lambda p: (0, p))
    whole = pl.BlockSpec((s, LANES), lambda p: (0, 0))
    return pl.pallas_call(
        body, name=name, grid=(SWA_Q_WIDTH // LANES,),
        out_shape=(jax.ShapeDtypeStruct((s, SWA_Q_WIDTH), F32), jax.ShapeDtypeStruct((s, LANES), F32),
                   jax.ShapeDtypeStruct((s, LANES), F32), jax.ShapeDtypeStruct((SWA_Q_WIDTH // LANES, 1, LANES), F32)),
        in_specs=[pair, whole, pl.BlockSpec((s, LANES), lambda p: (0, v_blk)),
                  pl.BlockSpec((None, 1, LANES), lambda p: (p, 0, 0)), pair, pair],
        out_specs=(pair, whole, whole, pl.BlockSpec((None, 1, LANES), lambda p: (p, 0, 0))),
        scratch_shapes=[pltpu.VMEM((s, LANES), F32), pltpu.VMEM((s, LANES), F32)],
        compiler_params=_params("arbitrary"),
    )(qn, kn, proj, sink_p, o, do)


def _rope_tables(s):
    inv_freq = 1.0 / (ROPE_THETA ** (jnp.arange(0, HEAD_DIM, 2, dtype=F32) / HEAD_DIM))
    ang = jnp.arange(s, dtype=F32)[:, None] * inv_freq[None, :]
    cos, sin = jnp.cos(ang), jnp.sin(ang)
    cos_p = jnp.tile(jnp.concatenate([cos, cos], axis=1), (1, LANES // HEAD_DIM))
    sin_p = jnp.tile(jnp.concatenate([-sin, sin], axis=1), (1, LANES // HEAD_DIM))
    return cos_p, sin_p


def _lane_tile(v, reps):
    return jnp.tile(v.reshape(1, -1), (1, reps))


def _natural(stack, name):
    n, r, c = stack.shape
    if name in ROW_SHARDED:
        return stack.reshape(n * r, c)
    if name == "w_up":
        return stack
    return jnp.transpose(stack, (1, 0, 2)).reshape(r, n * c)


def _pack_small(tree):
    flat = jnp.concatenate([tree[n].reshape(-1) for n in SMALL_NAMES])
    rows = -(-flat.shape[0] // (8 * LANES)) * 8
    return jnp.pad(flat, (0, rows * LANES - flat.shape[0])).reshape(rows, LANES)


def _unpack_small(packed, shapes):
    flat, out, off = packed.reshape(-1), {}, 0
    for n in SMALL_NAMES:
        size = shapes[n][0] * shapes[n][1]
        out[n] = flat[off:off + size].reshape(shapes[n])
        off += size
    return out


def train_step(x, target, weights, mom_m, mom_v):
    s = x.shape[0]
    cos_p, sin_p = _rope_tables(s)
    tri = (jnp.arange(min(SB_TILE, s))[:, None] > jnp.arange(min(SB_TILE, s))[None, :]).astype(BF16)
    shards = [[weights[n][l].astype(BF16) for n in MATRIX_NAMES] for l in range(DEPTH)]
    core = lax.axis_index("c").astype(jnp.int32).reshape(1)
    chip = (2 * lax.axis_index("x") + lax.axis_index("y")).astype(jnp.int32).reshape(1)

    stack_in = gather_shards(shards[0][:1], name="gather_w_in0")[0]
    saved = []
    for l in range(DEPTH):
        mats = {"w_in": _natural(stack_in, "w_in")}
        g_mix = weights["mix_norm_g"][l].reshape(1, D_MODEL)
        g_mlp = weights["mlp_norm_g"][l].reshape(1, D_MODEL)
        gq = _lane_tile(weights["q_norm_g"][l], LANES // HEAD_DIM)
        gk = _lane_tile(weights["k_norm_g"][l], LANES // HEAD_DIM)
        sink_p = jnp.repeat(weights["sinks"][l].reshape(SWA_Q_WIDTH // LANES, 2), HEAD_DIM, axis=1)
        sink_p = sink_p.reshape(SWA_Q_WIDTH // LANES, 1, LANES)
        h, proj, gates = norm_matmul(x, g_mix, mats["w_in"], gate_split=ATTN_WIDTH, name="in_proj")
        o_sb, stacks = sb_attn_fwd(proj, tri, shards[l][1:], name="sb_fwd_gather")
        mats.update({n: _natural(stacks[i], n) for i, n in enumerate(MATRIX_NAMES[1:])})
        qn, kn = swa_prep_fwd(proj, cos_p, sin_p, gq, gk, name="swa_prep")
        if l + 1 < DEPTH:
            o_sw, (stack_in,) = swa_attn_fwd(qn, kn, proj, sink_p, shards[l + 1][:1], name="swa_fwd_gather")
        else:
            o_sw = swa_attn_fwd(qn, kn, proj, sink_p, None, name="swa_fwd")
        x1, y_sb, y_sw, merged = merge_out_fwd(x, o_sb, o_sw, gates, mats["w_branch_sb"], mats["w_branch_swa"],
                                               mats["w_out"], name="merge_out")
        h2, u = norm_matmul(x1, g_mlp, mats["w_up"], gate_split=None, name="mlp_up")
        x2 = mlp_down_fwd(x1, u, mats["w_down"], name="mlp_down")
        saved.append(dict(x=x, h=h, proj=proj, gates=gates, o_sb=o_sb, qn=qn, kn=kn, o_sw=o_sw, y_sb=y_sb, y_sw=y_sw,
                          merged=merged, x1=x1, h2=h2, u=u, g_mix=g_mix, g_mlp=g_mlp, gq=gq, gk=gk, sink_p=sink_p,
                          mats=mats))
        x = x2

    dx, dxb, loss = loss_head(x, target, name="loss_head")

    shard_shapes = [weights[n].shape[1:] for n in MATRIX_NAMES]
    parts = [lax.empty((DEPTH, N_CHIPS) + sh, BF16) for sh in shard_shapes]
    lands = [lax.empty((DEPTH, 3) + sh, BF16) for sh in shard_shapes]
    small_grads = {n: [None] * DEPTH for n in SMALL_NAMES}
    half = D_MODEL // 2
    for l in reversed(range(DEPTH)):
        a = saved[l]
        mats = a["mats"]
        du = mlp_bwd_up(dxb, a["u"], mats["w_down"], name="mlp_bwd_up")
        dw_down = matmul_tn(a["u"], [dxb], a_block=half, out_cols=None, relu2=True, name="dw_down")
        dw_up = matmul_tn(a["h2"], [du], a_block=half, out_cols=du.shape[1] // N_DEV, relu2=False, name="dw_up")
        dx1, dx1b, dg_mlp = matmul_nt_norm_bwd([du], mats["w_up"], a["x1"], a["g_mlp"], dx, name="mlp_bwd_norm")
        small_grads["mlp_norm_g"][l] = dg_mlp.reshape(D_MODEL)
        dw_out = matmul_tn(a["merged"], [dx1b], a_block=half, out_cols=None, relu2=False, name="dw_out")
        dy_sb, dy_sw, do_sb, do_sw, dgl = out_bwd(dx1b, mats["w_out"], a["gates"], a["y_sb"], a["y_sw"],
                                                  mats["w_branch_sb"], mats["w_branch_swa"], name="out_bwd")
        dw_bsb = matmul_tn(a["o_sb"], [dy_sb], a_block=half, out_cols=D_MODEL // N_DEV, relu2=False, name="dw_branch_sb")
        dw_bsw = matmul_tn(a["o_sw"], [dy_sw], a_block=half, out_cols=D_MODEL // N_DEV, relu2=False, name="dw_branch_swa")

        rest = [dw_bsb, dw_bsw, dw_out.reshape((N_DEV,) + shard_shapes[3]), dw_up,
                dw_down.reshape((N_DEV,) + shard_shapes[5])]
        landed = pair_exchange(rest, name="grad_pair_exchange_rest")
        parts[1:] = pair_sum(l, rest, landed, parts[1:], core, name="grad_pair_sum_rest")
        items = [(l, w) for w in range(1, N_MATS)] + ([(l + 1, 0)] if l + 1 < DEPTH else [])
        dq_sb, dk_sb, dv_sb, lands = sb_attn_bwd(a["proj"], tri, a["o_sb"], do_sb, (items, parts, lands),
                                                 name="sb_bwd_exchange")
        dqn, dkn, dv_sw, dsink = swa_attn_bwd(a["qn"], a["kn"], a["proj"], a["sink_p"], a["o_sw"], do_sw, name="swa_bwd")
        dq_sw, dk_sw, dv_swb, dgq, dgk = swa_prep_bwd(a["proj"], cos_p, sin_p, a["gq"], a["gk"], dqn, dkn, dv_sw,
                                                      name="swa_prep_bwd")
        small_grads["q_norm_g"][l] = dgq.reshape(SWA_Q_WIDTH // HEAD_DIM, HEAD_DIM).sum(0)
        small_grads["k_norm_g"][l] = dgk.reshape(LANES // HEAD_DIM, HEAD_DIM).sum(0)
        small_grads["sinks"][l] = dsink[:, 0, ::HEAD_DIM].reshape(SWA_Q_WIDTH // HEAD_DIM)
        pieces = [dq_sb, dk_sb, dv_sb, dq_sw, dk_sw, dv_swb, dgl]
        dw_in = matmul_tn(a["h"], pieces, a_block=half, out_cols=None, relu2=False, name="dw_in")
        g_in = [jnp.transpose(dw_in.reshape(D_MODEL, N_DEV, IN_WIDTH // N_DEV), (1, 0, 2))]
        landed = pair_exchange(g_in, name="grad_pair_exchange_in")
        parts[:1] = pair_sum(l, g_in, landed, parts[:1], core, name="grad_pair_sum_in")
        dx, dxb, dg_mix = matmul_nt_norm_bwd(pieces, mats["w_in"], a["x"], a["g_mix"], dx1, name="in_proj_bwd")
        small_grads["mix_norm_g"][l] = dg_mix.reshape(D_MODEL)
    lands = chip_exchange([(0, 0)], parts, lands, name="grad_chip_exchange_in0")

    out_g, out_d, out_m, out_v = {}, {}, {}, {}
    for i, n in enumerate(MATRIX_NAMES):
        out_g[n], out_d[n], out_m[n], out_v[n] = reduce_adamw(parts[i], lands[i], chip, weights[n], mom_m[n], mom_v[n],
                                                              name="adamw_" + n)
    small_shapes = {n: weights[n].shape for n in SMALL_NAMES}
    small_all = gather_small(_pack_small({n: jnp.stack(v) for n, v in small_grads.items()}), name="gather_small_grads")
    sg, sd, sm, sv = small_adamw(small_all, _pack_small(weights), _pack_small(mom_m), _pack_small(mom_v),
                                 name="small_adamw")
    for tree, packed_small in ((out_g, sg), (out_d, sd), (out_m, sm), (out_v, sv)):
        tree.update(_unpack_small(packed_small, small_shapes))
    return loss, dx, (out_g, out_d, out_m, out_v)


def kernel(x, mix_norm_g, w_in, q_norm_g, k_norm_g, sinks, w_branch_sb, w_branch_swa, w_out, mlp_norm_g, w_up, w_down, loss_target, m_mix_norm_g, m_w_in, m_q_norm_g, m_k_norm_g, m_sinks, m_w_branch_sb, m_w_branch_swa, m_w_out, m_mlp_norm_g, m_w_up, m_w_down, v_mix_norm_g, v_w_in, v_q_norm_g, v_k_norm_g, v_sinks, v_w_branch_sb, v_w_branch_swa, v_w_out, v_mlp_norm_g, v_w_up, v_w_down):
    weights = dict(mix_norm_g=mix_norm_g, w_in=w_in, q_norm_g=q_norm_g, k_norm_g=k_norm_g, sinks=sinks,
                   w_branch_sb=w_branch_sb, w_branch_swa=w_branch_swa, w_out=w_out, mlp_norm_g=mlp_norm_g, w_up=w_up,
                   w_down=w_down)
    mom_m = dict(mix_norm_g=m_mix_norm_g, w_in=m_w_in, q_norm_g=m_q_norm_g, k_norm_g=m_k_norm_g, sinks=m_sinks,
                 w_branch_sb=m_w_branch_sb, w_branch_swa=m_w_branch_swa, w_out=m_w_out, mlp_norm_g=m_mlp_norm_g,
                 w_up=m_w_up, w_down=m_w_down)
    mom_v = dict(mix_norm_g=v_mix_norm_g, w_in=v_w_in, q_norm_g=v_q_norm_g, k_norm_g=v_k_norm_g, sinks=v_sinks,
                 w_branch_sb=v_w_branch_sb, w_branch_swa=v_w_branch_swa, w_out=v_w_out, mlp_norm_g=v_mlp_norm_g,
                 w_up=v_w_up, w_down=v_w_down)
    loss_part, grad_x, outs = train_step(x[0], loss_target[0], weights, mom_m, mom_v)
    loss = lax.psum(loss_part[0, 0], MESH_AXES)
    return (loss, grad_x[None], *[outs[0][n] for n in WEIGHT_ORDER], *[outs[1][n] for n in WEIGHT_ORDER],
            *[outs[2][n] for n in WEIGHT_ORDER], *[outs[3][n] for n in WEIGHT_ORDER])
```

```python
import functools

import jax
import jax.numpy as jnp
from jax import lax
from jax.experimental import pallas as pl
from jax.experimental.pallas import tpu as pltpu

F32 = jnp.float32
BF16 = jnp.bfloat16

DEPTH = 4
D_MODEL = 1024
HEAD_DIM = 64
LANES = 128
WINDOW = 128
SB_WIDTH = 512
SWA_Q_WIDTH = 512
SWA_KV_WIDTH = 128
ATTN_WIDTH = 3 * SB_WIDTH + SWA_Q_WIDTH + 2 * SWA_KV_WIDTH
IN_WIDTH = ATTN_WIDTH + 2 * D_MODEL
ROPE_THETA = 10000.0
NORM_EPS = 1e-6
SCALE = HEAD_DIM ** -0.5
NEG = -1e30
N_DEV = 8
N_CHIPS = 4

ADAM_LR = 0.001
ADAM_B1 = 0.9
ADAM_B2 = 0.999
ADAM_EPS = 1e-08
ADAM_WD = 0.01
ADAM_STEP = 10

SB_TQ = 128
SB_TK = 256
SB_CUTOFF = -88.0
SWA_TQ = 128
SWA_TK = 256
ROW_TILE = 256
VMEM_LIMIT = 56 * 1024 * 1024

MATRIX_NAMES = ("w_in", "w_branch_sb", "w_branch_swa", "w_out", "w_up", "w_down")
ROW_SHARDED = ("w_out", "w_down")
SMALL_NAMES = ("mix_norm_g", "q_norm_g", "k_norm_g", "sinks", "mlp_norm_g")
WEIGHT_ORDER = ("mix_norm_g", "w_in", "q_norm_g", "k_norm_g", "sinks", "w_branch_sb", "w_branch_swa", "w_out",
                "mlp_norm_g", "w_up", "w_down")
MESH_AXES = ("x", "y", "c")
N_MATS = len(MATRIX_NAMES)

ANY = pl.BlockSpec(memory_space=pl.ANY)
MESH = pl.DeviceIdType.MESH


def _params(*sem):
    return pltpu.CompilerParams(dimension_semantics=sem, vmem_limit_bytes=VMEM_LIMIT)


def _dot(a, b):
    return jnp.dot(a, b, preferred_element_type=F32)


def _dot_nt(a, b):
    return lax.dot_general(a, b, (((1,), (1,)), ((), ())), preferred_element_type=F32)


def _dot_tn(a, b):
    return lax.dot_general(a, b, (((0,), (0,)), ((), ())), preferred_element_type=F32)


def _split_bf16(x):
    hi = x.astype(BF16)
    lo = (x - hi.astype(F32)).astype(BF16)
    return hi, lo


def _dot2(x, b):
    hi, lo = _split_bf16(x)
    return _dot(hi, b) + _dot(lo, b)


def _dot2_nt(x, b):
    hi, lo = _split_bf16(x)
    return _dot_nt(hi, b) + _dot_nt(lo, b)


def _rsqrt_ms(x):
    return lax.rsqrt(jnp.mean(x * x, axis=-1, keepdims=True) + NORM_EPS)


def _place():
    return lax.axis_index("x"), lax.axis_index("y"), lax.axis_index("c")


def _gather_scratch(n):
    return [pltpu.SemaphoreType.DMA((7, n)), pltpu.SemaphoreType.DMA((7, n)), pltpu.SemaphoreType.DMA((n,))]


class _Gather:
    def __init__(self, x_refs, out_refs, send_sems, recv_sems, local_sems):
        self.x_refs, self.out_refs = x_refs, out_refs
        self.send_sems, self.recv_sems, self.local_sems = send_sems, recv_sems, local_sems
        self.n = len(x_refs)
        x, y, c = _place()
        self.c = c
        self.me, self.sibling = (x, y, c), (x, y, 1 - c)
        self.chips = [(1 - x, y), (x, 1 - y), (1 - x, 1 - y)]

    def _copy(self, k, w, blk, to, own=False):
        dst = self.out_refs[w].at[4 * blk[0] + 2 * blk[1] + blk[2]]
        return pltpu.make_async_remote_copy(
            src_ref=self.x_refs[w] if own else dst, dst_ref=dst, send_sem=self.send_sems.at[k, w],
            recv_sem=self.recv_sems.at[k, w], device_id=to, device_id_type=MESH)

    def _mine(self, w):
        me = self.me
        return pltpu.make_async_copy(self.x_refs[w], self.out_refs[w].at[4 * me[0] + 2 * me[1] + me[2]],
                                     self.local_sems.at[w])

    def _first(self, w):
        return [self._copy(0, w, self.me, self.sibling, own=True)] + [
            self._copy(1 + j, w, self.me, (*chip, self.c), own=True) for j, chip in enumerate(self.chips)]

    def _passed(self, j, w):
        return self._copy(4 + j, w, (*self.chips[j], self.c), self.sibling)

    def start(self):
        for w in range(self.n):
            self._mine(w).start()
            for cp in self._first(w):
                cp.start()

    def forward(self):
        for j, chip in enumerate(self.chips):
            for w in range(self.n):
                self._copy(1 + j, w, (*chip, self.c), self.me).wait_recv()
                self._passed(j, w).start()

    def finish(self):
        for w in range(self.n):
            self._copy(0, w, self.sibling, self.me).wait_recv()
            for j, chip in enumerate(self.chips):
                self._copy(4 + j, w, (*chip, 1 - self.c), self.me).wait_recv()
            for cp in self._first(w):
                cp.wait_send()
            for j in range(3):
                self._passed(j, w).wait_send()
            self._mine(w).wait()


def _gather_out_shapes(shards):
    return tuple(jax.ShapeDtypeStruct((N_DEV,) + s.shape, s.dtype) for s in shards)


def gather_shards(shards, *, name):
    n = len(shards)

    def body(*refs):
        g = _Gather(refs[:n], refs[n:2 * n], *refs[2 * n:])
        g.start()
        g.forward()
        g.finish()

    return pl.pallas_call(
        body, name=name, out_shape=_gather_out_shapes(shards), in_specs=[ANY] * n, out_specs=(ANY,) * n,
        scratch_shapes=_gather_scratch(n), compiler_params=pltpu.CompilerParams(has_side_effects=True),
    )(*shards)


CHIP_SCRATCH = [pltpu.SemaphoreType.DMA((3, N_MATS)), pltpu.SemaphoreType.DMA((3, N_MATS))]


class _ChipExchange:
    def __init__(self, items, part_refs, land_refs, send_sems, recv_sems):
        x, y, c = _place()
        chips = [(1 - x, y), (x, 1 - y), (1 - x, 1 - y)]
        self.copies = [pltpu.make_async_remote_copy(
            src_ref=part_refs[w].at[layer, 2 * px + py], dst_ref=land_refs[w].at[layer, j],
            send_sem=send_sems.at[j, w], recv_sem=recv_sems.at[j, w], device_id=(px, py, c), device_id_type=MESH)
            for layer, w in items for j, (px, py) in enumerate(chips)]

    def start(self):
        for cp in self.copies:
            cp.start()

    def finish(self):
        for cp in self.copies:
            cp.wait_recv()
        for cp in self.copies:
            cp.wait_send()


def chip_exchange(items, parts, lands, *, name):
    def body(*refs):
        ex = _ChipExchange(items, refs[:N_MATS], refs[2 * N_MATS:3 * N_MATS], *refs[3 * N_MATS:])
        ex.start()
        ex.finish()

    return pl.pallas_call(
        body, name=name, out_shape=tuple(jax.ShapeDtypeStruct(a.shape, a.dtype) for a in lands),
        in_specs=[ANY] * (2 * N_MATS), out_specs=(ANY,) * N_MATS,
        input_output_aliases={N_MATS + w: w for w in range(N_MATS)}, scratch_shapes=CHIP_SCRATCH,
        compiler_params=pltpu.CompilerParams(has_side_effects=True),
    )(*parts, *lands)


def pair_exchange(grads, *, name):
    n = len(grads)

    def body(*refs):
        g_refs, land_refs = refs[:n], refs[n:2 * n]
        send_sems, recv_sems = refs[2 * n:]
        x, y, c = _place()
        copies = [pltpu.make_async_remote_copy(
            src_ref=g_refs[w].at[2 * k + (1 - c)], dst_ref=land_refs[w].at[k], send_sem=send_sems.at[k, w],
            recv_sem=recv_sems.at[k, w], device_id=(x, y, 1 - c), device_id_type=MESH)
            for w in range(n) for k in range(N_CHIPS)]
        for cp in copies:
            cp.start()
        for cp in copies:
            cp.wait_recv()
        for cp in copies:
            cp.wait_send()

    return pl.pallas_call(
        body, name=name,
        out_shape=tuple(jax.ShapeDtypeStruct((N_CHIPS,) + g.shape[1:], g.dtype) for g in grads),
        in_specs=[ANY] * n, out_specs=(ANY,) * n,
        scratch_shapes=[pltpu.SemaphoreType.DMA((N_CHIPS, n)), pltpu.SemaphoreType.DMA((N_CHIPS, n))],
        compiler_params=pltpu.CompilerParams(has_side_effects=True),
    )(*grads)


PAIR_SUM_CHUNKS = 8


def pair_sum(layer, grads, landed, parts, core, *, name):
    n = len(grads)

    def body(c_ref, *refs):
        g_refs, l_refs, o_refs = refs[:n], refs[n:2 * n], refs[3 * n:]
        for w in range(n):
            o_refs[w][...] = (g_refs[w][...].astype(F32) + l_refs[w][...].astype(F32)).astype(BF16)

    def blk(g):
        return (None, g.shape[1] // PAIR_SUM_CHUNKS, g.shape[2])

    in_specs = [pl.BlockSpec(blk(g), lambda k, i, c_ref: (2 * k + c_ref[0], i, 0)) for g in grads]
    in_specs += [pl.BlockSpec(blk(g), lambda k, i, c_ref: (k, i, 0)) for g in grads]
    in_specs += [ANY] * n
    out_specs = tuple(pl.BlockSpec((None,) + blk(g), lambda k, i, c_ref: (layer, k, i, 0)) for g in grads)
    return list(pl.pallas_call(
        body, name=name, out_shape=tuple(jax.ShapeDtypeStruct(p.shape, p.dtype) for p in parts),
        grid_spec=pltpu.PrefetchScalarGridSpec(num_scalar_prefetch=1, grid=(N_CHIPS, PAIR_SUM_CHUNKS),
                                               in_specs=in_specs, out_specs=out_specs),
        input_output_aliases={1 + 2 * n + w: w for w in range(n)},
        compiler_params=_params("parallel", "parallel"),
    )(core, *grads, *landed, *parts))


def _adamw(w, g, m, v):
    m = ADAM_B1 * m + (1.0 - ADAM_B1) * g
    v = ADAM_B2 * v + (1.0 - ADAM_B2) * (g * g)
    m_hat = m / (1.0 - ADAM_B1 ** ADAM_STEP)
    v_hat = v / (1.0 - ADAM_B2 ** ADAM_STEP)
    delta = -ADAM_LR * (m_hat / (jnp.sqrt(v_hat) + ADAM_EPS) + ADAM_WD * w)
    return delta, m, v


def reduce_adamw(part, land, chip, w, m, v, *, name):
    _, r, c = w.shape
    tr = min(r, 256)

    def body(k_ref, own_ref, l0_ref, l1_ref, l2_ref, w_ref, m_ref, v_ref, g_out, d_out, m_out, v_out):
        g = own_ref[...].astype(F32) + l0_ref[...].astype(F32) + l1_ref[...].astype(F32) + l2_ref[...].astype(F32)
        delta, m_new, v_new = _adamw(w_ref[...], g, m_ref[...], v_ref[...])
        g_out[...] = g
        d_out[...] = delta
        m_out[...] = m_new
        v_out[...] = v_new

    row = pl.BlockSpec((None, tr, c), lambda l, i, k_ref: (l, i, 0))

    def slot(j):
        return pl.BlockSpec((None, None, tr, c), lambda l, i, k_ref: (l, j, i, 0))

    return pl.pallas_call(
        body, name=name, out_shape=(jax.ShapeDtypeStruct(w.shape, F32),) * 4,
        grid_spec=pltpu.PrefetchScalarGridSpec(
            num_scalar_prefetch=1, grid=(DEPTH, r // tr),
            in_specs=[pl.BlockSpec((None, None, tr, c), lambda l, i, k_ref: (l, k_ref[0], i, 0)), slot(0), slot(1),
                      slot(2), row, row, row],
            out_specs=(row, row, row, row)),
        compiler_params=_params("parallel", "parallel"),
    )(chip, part, land, land, land, w, m, v)


def gather_small(block, *, name):
    def body(x_ref, out_ref, send_sems, recv_sems, local_sem):
        x, y, c = _place()
        me = 4 * x + 2 * y + c
        mine = pltpu.make_async_copy(x_ref, out_ref.at[me], local_sem)
        mine.start()
        peers = [(x ^ (k >> 2), y ^ ((k >> 1) & 1), c ^ (k & 1)) for k in range(1, N_DEV)]
        copies = [pltpu.make_async_remote_copy(
            src_ref=x_ref, dst_ref=out_ref.at[me], send_sem=send_sems.at[k], recv_sem=recv_sems.at[k],
            device_id=peer, device_id_type=MESH) for k, peer in enumerate(peers)]
        for cp in copies:
            cp.start()
        for k, (px, py, pc) in enumerate(peers):
            pltpu.make_async_remote_copy(
                src_ref=x_ref, dst_ref=out_ref.at[4 * px + 2 * py + pc], send_sem=send_sems.at[k],
                recv_sem=recv_sems.at[k], device_id=(px, py, pc), device_id_type=MESH).wait_recv()
        for cp in copies:
            cp.wait_send()
        mine.wait()

    return pl.pallas_call(
        body, name=name, out_shape=jax.ShapeDtypeStruct((N_DEV,) + block.shape, block.dtype),
        in_specs=[ANY], out_specs=ANY,
        scratch_shapes=[pltpu.SemaphoreType.DMA((7,)), pltpu.SemaphoreType.DMA((7,)), pltpu.SemaphoreType.DMA],
        compiler_params=pltpu.CompilerParams(has_side_effects=True),
    )(block)


def small_adamw(gathered, w, m, v, *, name):
    def body(g_ref, w_ref, m_ref, v_ref, g_out, d_out, m_out, v_out):
        g = g_ref[0]
        for d in range(1, N_DEV):
            g = g + g_ref[d]
        delta, m_new, v_new = _adamw(w_ref[...], g, m_ref[...], v_ref[...])
        g_out[...] = g
        d_out[...] = delta
        m_out[...] = m_new
        v_out[...] = v_new

    return pl.pallas_call(
        body, name=name, out_shape=(jax.ShapeDtypeStruct(w.shape, F32),) * 4,
    )(gathered, w, m, v)


def norm_matmul(x, g, w, *, gate_split, name):
    s, d = x.shape
    tm = min(ROW_TILE, s)
    blocked = w.ndim == 3
    n = w.shape[1] if not blocked else w.shape[0] * w.shape[2]

    def body(x_ref, g_ref, w_ref, h_ref, *outs):
        xv = x_ref[...]
        h = ((xv * _rsqrt_ms(xv)) * g_ref[...]).astype(BF16)
        h_ref[...] = h
        if blocked:
            nb = w_ref.shape[2]
            for j in range(w_ref.shape[0]):
                outs[0][:, j * nb:(j + 1) * nb] = _dot(h, w_ref[j]).astype(BF16)
        else:
            p = _dot(h, w_ref[...])
            outs[0][...] = p[:, :gate_split].astype(BF16)
            outs[1][...] = (1.0 / (1.0 + jnp.exp(-p[:, gate_split:]))).astype(BF16)

    row = lambda i: (i, 0)
    fixed = lambda i: (0, 0)
    if blocked:
        out_shape = (jax.ShapeDtypeStruct((s, d), BF16), jax.ShapeDtypeStruct((s, n), BF16))
        out_specs = (pl.BlockSpec((tm, d), row), pl.BlockSpec((tm, n), row))
        w_spec = pl.BlockSpec(w.shape, lambda i: (0, 0, 0))
    else:
        out_shape = (jax.ShapeDtypeStruct((s, d), BF16), jax.ShapeDtypeStruct((s, gate_split), BF16),
                     jax.ShapeDtypeStruct((s, n - gate_split), BF16))
        out_specs = (pl.BlockSpec((tm, d), row), pl.BlockSpec((tm, gate_split), row),
                     pl.BlockSpec((tm, n - gate_split), row))
        w_spec = pl.BlockSpec((d, n), fixed)
    return pl.pallas_call(
        body, name=name, grid=(s // tm,), out_shape=out_shape,
        in_specs=[pl.BlockSpec((tm, d), row), pl.BlockSpec((1, d), fixed), w_spec],
        out_specs=out_specs, compiler_params=_params("parallel"),
    )(x, g, w)


def merge_out_fwd(x, o_sb, o_sw, gates, w_bsb, w_bsw, w_o, *, name):
    s, d = x.shape
    tm = min(ROW_TILE, s)

    def body(x_ref, osb_ref, osw_ref, g_ref, wsb_ref, wsw_ref, wo_ref, x1_ref, ysb_ref, ysw_ref, mg_ref):
        y_sb = _dot(osb_ref[...].astype(BF16), wsb_ref[...])
        y_sw = _dot(osw_ref[...].astype(BF16), wsw_ref[...])
        g = g_ref[...].astype(F32)
        merged = (g[:, :d] * y_sb + g[:, d:] * y_sw).astype(BF16)
        ysb_ref[...] = y_sb.astype(BF16)
        ysw_ref[...] = y_sw.astype(BF16)
        mg_ref[...] = merged
        x1_ref[...] = x_ref[...] + _dot(merged, wo_ref[...])

    row = lambda i: (i, 0)
    fixed = lambda i: (0, 0)
    wd = o_sb.shape[1]
    return pl.pallas_call(
        body, name=name, grid=(s // tm,),
        out_shape=(jax.ShapeDtypeStruct((s, d), F32),) + (jax.ShapeDtypeStruct((s, d), BF16),) * 3,
        in_specs=[pl.BlockSpec((tm, d), row), pl.BlockSpec((tm, wd), row), pl.BlockSpec((tm, wd), row),
                  pl.BlockSpec((tm, 2 * d), row), pl.BlockSpec((wd, d), fixed), pl.BlockSpec((wd, d), fixed),
                  pl.BlockSpec((d, d), fixed)],
        out_specs=(pl.BlockSpec((tm, d), row),) * 4, compiler_params=_params("parallel"),
    )(x, o_sb, o_sw, gates, w_bsb, w_bsw, w_o)


def mlp_down_fwd(x1, u, w_down, *, name):
    s, d = x1.shape
    f = u.shape[1]
    tm = min(ROW_TILE, s)

    def body(x_ref, u_ref, w_ref, o_ref):
        a = jnp.maximum(u_ref[...].astype(F32), 0.0)
        o_ref[...] = x_ref[...] + _dot((a * a).astype(BF16), w_ref[...])

    row = lambda i: (i, 0)
    return pl.pallas_call(
        body, name=name, grid=(s // tm,), out_shape=jax.ShapeDtypeStruct((s, d), F32),
        in_specs=[pl.BlockSpec((tm, d), row), pl.BlockSpec((tm, f), row), pl.BlockSpec((f, d), lambda i: (0, 0))],
        out_specs=pl.BlockSpec((tm, d), row), compiler_params=_params("parallel"),
    )(x1, u, w_down)


def loss_head(y, target, *, name):
    s, d = y.shape
    tm = min(ROW_TILE, s)

    def body(y_ref, t_ref, dy_ref, dyb_ref, loss_ref):
        @pl.when(pl.program_id(0) == 0)
        def _():
            loss_ref[...] = jnp.zeros_like(loss_ref)

        e = y_ref[...] - t_ref[...]
        dy = e * (1.0 / d)
        dy_ref[...] = dy
        dyb_ref[...] = dy.astype(BF16)
        per_row = jnp.sum(e * e, axis=1, keepdims=True) * (0.5 / d)
        loss_ref[...] += jnp.sum(per_row, axis=0, keepdims=True)

    row = lambda i: (i, 0)
    return pl.pallas_call(
        body, name=name, grid=(s // tm,),
        out_shape=(jax.ShapeDtypeStruct((s, d), F32), jax.ShapeDtypeStruct((s, d), BF16),
                   jax.ShapeDtypeStruct((1, 1), F32)),
        in_specs=[pl.BlockSpec((tm, d), row), pl.BlockSpec((tm, d), row)],
        out_specs=(pl.BlockSpec((tm, d), row), pl.BlockSpec((tm, d), row), pl.BlockSpec((1, 1), lambda i: (0, 0))),
        compiler_params=_params("arbitrary"),
    )(y, target)


def mlp_bwd_up(dxb, u, w_down, *, name):
    s, d = dxb.shape
    f = u.shape[1]
    tm = min(ROW_TILE, s)

    def body(dx_ref, u_ref, w_ref, du_ref):
        da = _dot_nt(dx_ref[...], w_ref[...])
        du_ref[...] = (da * (2.0 * jnp.maximum(u_ref[...].astype(F32), 0.0))).astype(BF16)

    row = lambda i: (i, 0)
    return pl.pallas_call(
        body, name=name, grid=(s // tm,), out_shape=jax.ShapeDtypeStruct((s, f), BF16),
        in_specs=[pl.BlockSpec((tm, d), row), pl.BlockSpec((tm, f), row), pl.BlockSpec((f, d), lambda i: (0, 0))],
        out_specs=pl.BlockSpec((tm, f), row), compiler_params=_params("parallel"),
    )(dxb, u, w_down)


def matmul_nt_norm_bwd(pieces, w, x, g, dres, *, name):
    s = x.shape[0]
    d = x.shape[1]
    tm = min(ROW_TILE, s)
    blocked = w.ndim == 3
    n_pieces = len(pieces)
    widths = [p.shape[1] for p in pieces]

    def body(*refs):
        p_refs = refs[:n_pieces]
        w_ref, x_ref, g_ref, dres_ref, dx_ref, dxb_ref, dg_ref = refs[n_pieces:]

        @pl.when(pl.program_id(0) == 0)
        def _():
            dg_ref[...] = jnp.zeros_like(dg_ref)

        if blocked:
            nb = w_ref.shape[2]
            dh = _dot_nt(p_refs[0][:, :nb], w_ref[0])
            for j in range(1, w_ref.shape[0]):
                dh = dh + _dot_nt(p_refs[0][:, j * nb:(j + 1) * nb], w_ref[j])
        else:
            dh, off = None, 0
            for p_ref, width in zip(p_refs, widths):
                part = _dot_nt(p_ref[...], w_ref[:, off:off + width])
                dh = part if dh is None else dh + part
                off += width
        xv = x_ref[...]
        r = _rsqrt_ms(xv)
        dyg = dh * g_ref[...]
        dx = dres_ref[...] + r * dyg - xv * ((r * r * r) * jnp.mean(dyg * xv, axis=-1, keepdims=True))
        dx_ref[...] = dx
        dxb_ref[...] = dx.astype(BF16)
        dg_ref[...] += jnp.sum(dh * (xv * r), axis=0, keepdims=True)

    row = lambda i: (i, 0)
    fixed = lambda i: (0, 0)
    w_spec = pl.BlockSpec(w.shape, (lambda i: (0, 0, 0)) if blocked else fixed)
    return pl.pallas_call(
        body, name=name, grid=(s // tm,),
        out_shape=(jax.ShapeDtypeStruct((s, d), F32), jax.ShapeDtypeStruct((s, d), BF16),
                   jax.ShapeDtypeStruct((1, d), F32)),
        in_specs=[pl.BlockSpec((tm, width), row) for width in widths] + [
            w_spec, pl.BlockSpec((tm, d), row), pl.BlockSpec((1, d), fixed), pl.BlockSpec((tm, d), row)],
        out_specs=(pl.BlockSpec((tm, d), row), pl.BlockSpec((tm, d), row), pl.BlockSpec((1, d), fixed)),
        compiler_params=_params("arbitrary"),
    )(*pieces, w, x, g, dres)


def out_bwd(dx1b, w_o, gates, y_sb, y_sw, w_bsb, w_bsw, *, name):
    s, d = dx1b.shape
    wd = w_bsb.shape[0]
    tm = min(ROW_TILE, s)

    def body(dx_ref, wo_ref, g_ref, ysb_ref, ysw_ref, wsb_ref, wsw_ref, dysb_ref, dysw_ref, dosb_ref, dosw_ref, dgl_ref):
        dm = _dot_nt(dx_ref[...], wo_ref[...])
        g = g_ref[...].astype(F32)
        g0, g1 = g[:, :d], g[:, d:]
        dy_sb = (dm * g0).astype(BF16)
        dy_sw = (dm * g1).astype(BF16)
        dysb_ref[...] = dy_sb
        dysw_ref[...] = dy_sw
        dosb_ref[...] = _dot_nt(dy_sb, wsb_ref[...])
        dosw_ref[...] = _dot_nt(dy_sw, wsw_ref[...])
        dgl_ref[:, :d] = (dm * ysb_ref[...].astype(F32) * (g0 * (1.0 - g0))).astype(BF16)
        dgl_ref[:, d:] = (dm * ysw_ref[...].astype(F32) * (g1 * (1.0 - g1))).astype(BF16)

    row = lambda i: (i, 0)
    fixed = lambda i: (0, 0)
    return pl.pallas_call(
        body, name=name, grid=(s // tm,),
        out_shape=(jax.ShapeDtypeStruct((s, d), BF16), jax.ShapeDtypeStruct((s, d), BF16),
                   jax.ShapeDtypeStruct((s, wd), F32), jax.ShapeDtypeStruct((s, wd), F32),
                   jax.ShapeDtypeStruct((s, 2 * d), BF16)),
        in_specs=[pl.BlockSpec((tm, d), row), pl.BlockSpec((d, d), fixed), pl.BlockSpec((tm, 2 * d), row),
                  pl.BlockSpec((tm, d), row), pl.BlockSpec((tm, d), row), pl.BlockSpec((wd, d), fixed),
                  pl.BlockSpec((wd, d), fixed)],
        out_specs=(pl.BlockSpec((tm, d), row), pl.BlockSpec((tm, d), row), pl.BlockSpec((tm, wd), row),
                   pl.BlockSpec((tm, wd), row), pl.BlockSpec((tm, 2 * d), row)),
        compiler_params=_params("parallel"),
    )(dx1b, w_o, gates, y_sb, y_sw, w_bsb, w_bsw)


def matmul_tn(a, pieces, *, a_block, out_cols, relu2, name):
    s, m = a.shape
    widths = [p.shape[1] for p in pieces]
    n = sum(widths)
    n_pieces = len(pieces)
    ts = min(512 if n >= 4096 else 2048, s)
    n_steps = s // ts
    if out_cols is None:
        out_shape = jax.ShapeDtypeStruct((m // a_block, a_block, n), BF16)
        out_spec = pl.BlockSpec((None, a_block, n), lambda i, k: (i, 0, 0))
    else:
        out_shape = jax.ShapeDtypeStruct((n // out_cols, m, out_cols), BF16)
        out_spec = pl.BlockSpec((n // out_cols, a_block, out_cols), lambda i, k: (0, i, 0))

    def body(a_ref, *refs):
        b_refs, o_ref, acc = refs[:n_pieces], refs[n_pieces], refs[n_pieces + 1]
        k = pl.program_id(1)

        @pl.when(k == 0)
        def _():
            acc[...] = jnp.zeros_like(acc)

        av = a_ref[...]
        if relu2:
            af = jnp.maximum(av.astype(F32), 0.0)
            av = af * af
        av = av.astype(BF16)
        off = 0
        for b_ref in b_refs:
            width = b_ref.shape[1]
            acc[:, off:off + width] += _dot_tn(av, b_ref[...].astype(BF16))
            off += width

        @pl.when(k == n_steps - 1)
        def _():
            if out_cols is None:
                o_ref[...] = acc[...].astype(BF16)
            else:
                for j in range(n // out_cols):
                    o_ref[j] = acc[:, j * out_cols:(j + 1) * out_cols].astype(BF16)

    return pl.pallas_call(
        body, name=name, grid=(m // a_block, n_steps), out_shape=out_shape,
        in_specs=[pl.BlockSpec((ts, a_block), lambda i, k: (k, i))] + [
            pl.BlockSpec((ts, width), lambda i, k: (k, 0)) for width in widths],
        out_specs=out_spec, scratch_shapes=[pltpu.VMEM((a_block, n), F32)],
        compiler_params=_params("parallel", "arbitrary"),
    )(a, *pieces)


def _softplus(z):
    return jnp.maximum(z, 0.0) + jnp.log(1.0 + jnp.exp(-jnp.abs(z)))


def _suffix_sums(x, tri2):
    hi, lo = _split_bf16(x)
    return _dot(jnp.concatenate([hi, lo], axis=1), tri2)


def _head_mask(h):
    return (lax.broadcasted_iota(jnp.int32, (1, LANES), 1) // HEAD_DIM) == h


def _stack_heads(x):
    zero = jnp.zeros_like(x)
    return jnp.concatenate([jnp.where(_head_mask(0), x, zero), jnp.where(_head_mask(1), x, zero)], axis=0)


def _unstack_heads(r, t):
    return jnp.where(_head_mask(0), r[:t], r[t:])


def _sb_positions(q0):
    row = lax.broadcasted_iota(jnp.int32, (2 * SB_TQ, SB_TK), 0)
    col = lax.broadcasted_iota(jnp.int32, (2 * SB_TQ, SB_TK), 1)
    return q0 + jnp.where(row >= SB_TQ, row - SB_TQ, row), col


def _sb_first_key(q0):
    return pl.multiple_of(jnp.maximum(q0 + SB_TQ - SB_TK, 0), SB_TQ)


def _sb_next_key(k_prev):
    return pl.multiple_of(jnp.maximum(k_prev - SB_TK, 0), SB_TQ)


def sb_attn_fwd(proj, tri2, next_shards, *, name):
    s = proj.shape[0]
    nq = s // SB_TQ
    n_pairs = SB_WIDTH // LANES
    hosted = next_shards is not None
    n_sh = len(next_shards) if hosted else 0

    def body(q_ref, k_ref, v_ref, tri_ref, *refs):
        if hosted:
            gather = _Gather(refs[:n_sh], refs[n_sh + 1:2 * n_sh + 1], *refs[2 * n_sh + 1:])
            o_ref = refs[n_sh]
            p = pl.program_id(0)

            @pl.when(p == 0)
            def _():
                gather.start()

            @pl.when(p == n_pairs - 1)
            def _():
                gather.forward()
        else:
            o_ref = refs[0]

        def q_tile(qb, carry):
            q0 = pl.multiple_of(qb * SB_TQ, SB_TQ)
            qh = _stack_heads(q_ref[pl.ds(q0, SB_TQ), :]) * SCALE
            tpos, col = _sb_positions(q0)

            def block(k0, live, c):
                z = _dot_nt(qh, k_ref[pl.ds(k0, SB_TK), :])
                sp = _softplus(z)
                lk = jnp.where(live, -sp, 0.0)
                w = jnp.where(live, jnp.exp(z - sp + _suffix_sums(lk, tri_ref[...]) + c), 0.0)
                return _dot(w.astype(BF16), v_ref[pl.ds(k0, SB_TK), :]), c + jnp.sum(lk, axis=1, keepdims=True)

            k0 = _sb_first_key(q0)
            acc, c = block(k0, k0 + col < tpos, jnp.zeros((2 * SB_TQ, 1), F32))

            def cond(st):
                return jnp.logical_and(st[0] > 0, st[3] > SB_CUTOFF)

            def step(st):
                k_prev, c, acc, _ = st
                k0 = _sb_next_key(k_prev)
                part, c = block(k0, k0 + col < k_prev, c)
                return k0, c, acc + part, jnp.max(c)

            _, _, acc, _ = lax.while_loop(cond, step, (k0, c, acc, jnp.max(c)))
            o_ref[pl.ds(q0, SB_TQ), :] = _unstack_heads(acc, SB_TQ)
            return carry

        lax.fori_loop(0, nq, q_tile, 0)

        if hosted:
            @pl.when(p == n_pairs - 1)
            def _():
                gather.finish()

    def col(j):
        return pl.BlockSpec((s, LANES), lambda p: (0, j * n_pairs + p))

    o_shape = jax.ShapeDtypeStruct((s, SB_WIDTH), F32)
    o_spec = pl.BlockSpec((s, LANES), lambda p: (0, p))
    in_specs = [col(0), col(1), col(2), pl.BlockSpec((2 * SB_TK, SB_TK), lambda p: (0, 0))]
    if not hosted:
        return pl.pallas_call(
            body, name=name, grid=(n_pairs,), out_shape=o_shape, in_specs=in_specs, out_specs=o_spec,
            compiler_params=_params("arbitrary"),
        )(proj, proj, proj, tri2)
    outs = pl.pallas_call(
        body, name=name, grid=(n_pairs,), out_shape=(o_shape,) + _gather_out_shapes(next_shards),
        in_specs=in_specs + [ANY] * n_sh, out_specs=(o_spec,) + (ANY,) * n_sh, scratch_shapes=_gather_scratch(n_sh),
        compiler_params=pltpu.CompilerParams(dimension_semantics=("arbitrary",), vmem_limit_bytes=VMEM_LIMIT,
                                             has_side_effects=True),
    )(proj, proj, proj, tri2, *next_shards)
    return outs[0], list(outs[1:])


def sb_attn_bwd(proj, tri2, o, do, exchange, *, name):
    s = proj.shape[0]
    nq = s // SB_TQ
    n_pairs = SB_WIDTH // LANES
    hosted = exchange is not None

    def body(q_ref, k_ref, v_ref, tri_ref, o_ref, do_ref, *refs):
        if hosted:
            dq_ref, dk_ref, dv_ref = refs[2 * N_MATS:2 * N_MATS + 3]
            land_refs = refs[2 * N_MATS + 3:3 * N_MATS + 3]
            send_sems, recv_sems, dk_acc, dv_acc = refs[3 * N_MATS + 3:]
            ex = _ChipExchange(exchange[0], refs[:N_MATS], land_refs, send_sems, recv_sems)
            p = pl.program_id(0)

            @pl.when(p == 0)
            def _():
                ex.start()
        else:
            dq_ref, dk_ref, dv_ref, dk_acc, dv_acc = refs
        dk_acc[...] = jnp.zeros_like(dk_acc)
        dv_acc[...] = jnp.zeros_like(dv_acc)

        def q_tile(qb, carry):
            q0 = pl.multiple_of(qb * SB_TQ, SB_TQ)
            qh = _stack_heads(q_ref[pl.ds(q0, SB_TQ), :]) * SCALE
            doh_b = _stack_heads(do_ref[pl.ds(q0, SB_TQ), :].astype(BF16))
            ov = o_ref[pl.ds(q0, SB_TQ), :]
            dd = jnp.sum(doh_b.astype(F32) * jnp.concatenate([ov, ov], axis=0), axis=1, keepdims=True)
            tpos, col = _sb_positions(q0)

            def block(k0, live, c, ce):
                kt = k_ref[pl.ds(k0, SB_TK), :]
                z = _dot_nt(qh, kt)
                sp = _softplus(z)
                lb = z - sp
                lk = jnp.where(live, -sp, 0.0)
                wb = jnp.where(live, jnp.exp(lb + _suffix_sums(lk, tri_ref[...]) + c), 0.0).astype(BF16)
                e = wb.astype(F32) * _dot_nt(doh_b, v_ref[pl.ds(k0, SB_TK), :])
                dz = jnp.where(live, e - jnp.exp(lb) * (dd - ce - _suffix_sums(e, tri_ref[...])), 0.0)
                dzb = dz.astype(BF16)
                dk_acc[pl.ds(k0, SB_TK), :] += _dot_tn(dzb, qh)
                dv_acc[pl.ds(k0, SB_TK), :] += _dot_tn(wb, doh_b)
                return (_dot(dzb, kt), c + jnp.sum(lk, axis=1, keepdims=True), ce + jnp.sum(e, axis=1, keepdims=True))

            k0 = _sb_first_key(q0)
            zero = jnp.zeros((2 * SB_TQ, 1), F32)
            dq, c, ce = block(k0, k0 + col < tpos, zero, zero)

            def cond(st):
                return jnp.logical_and(st[0] > 0, st[4] > SB_CUTOFF)

            def step(st):
                k_prev, c, ce, dq, _ = st
                k0 = _sb_next_key(k_prev)
                part, c, ce = block(k0, k0 + col < k_prev, c, ce)
                return k0, c, ce, dq + part, jnp.max(c)

            st = lax.while_loop(cond, step, (k0, c, ce, dq, jnp.max(c)))
            dq_ref[pl.ds(q0, SB_TQ), :] = (_unstack_heads(st[3], SB_TQ) * SCALE).astype(BF16)
            return carry

        lax.fori_loop(0, nq, q_tile, 0)
        dk_ref[...] = dk_acc[...].astype(BF16)
        dv_ref[...] = dv_acc[...].astype(BF16)

        if hosted:
            @pl.when(p == n_pairs - 1)
            def _():
                ex.finish()

    def col(j):
        return pl.BlockSpec((s, LANES), lambda p: (0, j * n_pairs + p))

    pair = pl.BlockSpec((s, LANES), lambda p: (0, p))
    in_specs = [col(0), col(1), col(2), pl.BlockSpec((2 * SB_TK, SB_TK), lambda p: (0, 0)), pair, pair]
    d_shapes = (jax.ShapeDtypeStruct((s, SB_WIDTH), BF16),) * 3
    acc_scratch = [pltpu.VMEM((s, LANES), F32), pltpu.VMEM((s, LANES), F32)]
    if not hosted:
        return pl.pallas_call(
            body, name=name, grid=(n_pairs,), out_shape=d_shapes, in_specs=in_specs, out_specs=(pair, pair, pair),
            scratch_shapes=acc_scratch, compiler_params=_params("arbitrary"),
        )(proj, proj, proj, tri2, o, do)
    _, parts, lands = exchange
    outs = pl.pallas_call(
        body, name=name, grid=(n_pairs,),
        out_shape=d_shapes + tuple(jax.ShapeDtypeStruct(a.shape, a.dtype) for a in lands),
        in_specs=in_specs + [ANY] * (2 * N_MATS), out_specs=(pair, pair, pair) + (ANY,) * N_MATS,
        input_output_aliases={6 + N_MATS + w: 3 + w for w in range(N_MATS)},
        scratch_shapes=CHIP_SCRATCH + acc_scratch,
        compiler_params=pltpu.CompilerParams(dimension_semantics=("arbitrary",), vmem_limit_bytes=VMEM_LIMIT,
                                             has_side_effects=True),
    )(proj, proj, proj, tri2, o, do, *parts, *lands)
    return outs[0], outs[1], outs[2], list(outs[3:])


def _lane_lo():
    return lax.broadcasted_iota(jnp.int32, (1, LANES), 1) < HEAD_DIM


def _swap_halves(x):
    return pltpu.roll(x, HEAD_DIM, 1)


def _rot_half(y):
    first = (lax.broadcasted_iota(jnp.int32, (1, LANES), 1) % HEAD_DIM) < (HEAD_DIM // 2)
    return jnp.where(first, pltpu.roll(y, LANES - HEAD_DIM // 2, 1), pltpu.roll(y, HEAD_DIM // 2, 1))


def _head_mean(v):
    lo = _lane_lo()
    s0 = jnp.sum(jnp.where(lo, v, 0.0), axis=1, keepdims=True)
    s1 = jnp.sum(jnp.where(lo, 0.0, v), axis=1, keepdims=True)
    return jnp.where(lo, s0, s1) * (1.0 / HEAD_DIM)


def swa_prep_fwd(proj, cos_p, sin_p, gq, gk, *, name):
    s = proj.shape[0]
    tm = min(512, s)
    q_blk = (3 * SB_WIDTH) // SWA_Q_WIDTH
    k_blk = (3 * SB_WIDTH + SWA_Q_WIDTH) // LANES

    def norm_rope(xv, g, cosv, sinv):
        y = (xv * lax.rsqrt(_head_mean(xv * xv) + NORM_EPS)) * g
        return y * cosv + _rot_half(y) * sinv

    def body(q_ref, k_ref, cos_ref, sin_ref, gq_ref, gk_ref, qn_ref, kn_ref):
        cosv, sinv = cos_ref[...], sin_ref[...]
        for j in range(SWA_Q_WIDTH // LANES):
            sl = slice(j * LANES, (j + 1) * LANES)
            qn_ref[:, sl] = norm_rope(q_ref[:, sl].astype(F32), gq_ref[...], cosv, sinv).astype(BF16)
        kn_ref[...] = norm_rope(k_ref[...].astype(F32), gk_ref[...], cosv, sinv).astype(BF16)

    row = lambda i: (i, 0)
    fixed = lambda i: (0, 0)
    return pl.pallas_call(
        body, name=name, grid=(s // tm,),
        out_shape=(jax.ShapeDtypeStruct((s, SWA_Q_WIDTH), BF16), jax.ShapeDtypeStruct((s, LANES), BF16)),
        in_specs=[pl.BlockSpec((tm, SWA_Q_WIDTH), lambda i: (i, q_blk)), pl.BlockSpec((tm, LANES), lambda i: (i, k_blk)),
                  pl.BlockSpec((tm, LANES), row), pl.BlockSpec((tm, LANES), row),
                  pl.BlockSpec((1, LANES), fixed), pl.BlockSpec((1, LANES), fixed)],
        out_specs=(pl.BlockSpec((tm, SWA_Q_WIDTH), row), pl.BlockSpec((tm, LANES), row)),
        compiler_params=_params("parallel"),
    )(proj, proj, cos_p, sin_p, gq, gk)


def swa_prep_bwd(proj, cos_p, sin_p, gq, gk, dqn, dkn, dv, *, name):
    s = proj.shape[0]
    tm = min(512, s)
    q_blk = (3 * SB_WIDTH) // SWA_Q_WIDTH
    k_blk = (3 * SB_WIDTH + SWA_Q_WIDTH) // LANES

    def bwd(xv, g, cosv, sinv, dout):
        dy = dout * cosv + _rot_half(dout * sinv)
        r = lax.rsqrt(_head_mean(xv * xv) + NORM_EPS)
        dyg = dy * g
        dx = r * dyg - xv * ((r * r * r) * _head_mean(dyg * xv))
        return dx, jnp.sum(dy * (xv * r), axis=0, keepdims=True)

    def body(q_ref, k_ref, cos_ref, sin_ref, gq_ref, gk_ref, dqn_ref, dkn_ref, dv_ref, dq_ref, dk_ref, dvb_ref,
             dgq_ref, dgk_ref):
        @pl.when(pl.program_id(0) == 0)
        def _():
            dgq_ref[...] = jnp.zeros_like(dgq_ref)
            dgk_ref[...] = jnp.zeros_like(dgk_ref)

        cosv, sinv = cos_ref[...], sin_ref[...]
        for j in range(SWA_Q_WIDTH // LANES):
            sl = slice(j * LANES, (j + 1) * LANES)
            dx, dg = bwd(q_ref[:, sl].astype(F32), gq_ref[...], cosv, sinv, dqn_ref[:, sl])
            dq_ref[:, sl] = dx.astype(BF16)
            dgq_ref[:, sl] += dg
        dx, dg = bwd(k_ref[...].astype(F32), gk_ref[...], cosv, sinv, dkn_ref[...])
        dk_ref[...] = dx.astype(BF16)
        dgk_ref[...] += dg
        dvb_ref[...] = dv_ref[...].astype(BF16)

    row = lambda i: (i, 0)
    fixed = lambda i: (0, 0)
    lane_row = pl.BlockSpec((tm, LANES), row)
    return pl.pallas_call(
        body, name=name, grid=(s // tm,),
        out_shape=(jax.ShapeDtypeStruct((s, SWA_Q_WIDTH), BF16), jax.ShapeDtypeStruct((s, LANES), BF16),
                   jax.ShapeDtypeStruct((s, LANES), BF16),
                   jax.ShapeDtypeStruct((1, SWA_Q_WIDTH), F32), jax.ShapeDtypeStruct((1, LANES), F32)),
        in_specs=[pl.BlockSpec((tm, SWA_Q_WIDTH), lambda i: (i, q_blk)), pl.BlockSpec((tm, LANES), lambda i: (i, k_blk)),
                  lane_row, lane_row, pl.BlockSpec((1, LANES), fixed), pl.BlockSpec((1, LANES), fixed),
                  pl.BlockSpec((tm, SWA_Q_WIDTH), row), lane_row, lane_row],
        out_specs=(pl.BlockSpec((tm, SWA_Q_WIDTH), row), lane_row, lane_row,
                   pl.BlockSpec((1, SWA_Q_WIDTH), fixed), pl.BlockSpec((1, LANES), fixed)),
        compiler_params=_params("arbitrary"),
    )(proj, proj, cos_p, sin_p, gq, gk, dqn, dkn, dv)


def _swa_tile(i, k_ref, v_ref, second_kv):
    q0 = pl.multiple_of(i * SWA_TQ, SWA_TQ)
    k0 = pl.multiple_of(jnp.maximum(i - 1, 0) * SWA_TQ, SWA_TQ)
    keep = jnp.logical_xor(_lane_lo(), second_kv)
    kf = k_ref[pl.ds(k0, SWA_TK), :].astype(F32)
    vf = v_ref[pl.ds(k0, SWA_TK), :].astype(F32)
    kg = jnp.where(keep, kf, _swap_halves(kf)).astype(BF16)
    vg = jnp.where(keep, vf, _swap_halves(vf)).astype(BF16)
    row = lax.broadcasted_iota(jnp.int32, (2 * SWA_TQ, SWA_TK), 0)
    tpos = q0 + jnp.where(row >= SWA_TQ, row - SWA_TQ, row)
    spos = k0 + lax.broadcasted_iota(jnp.int32, (2 * SWA_TQ, SWA_TK), 1)
    valid = jnp.logical_and(spos <= tpos, spos > tpos - WINDOW)
    return q0, k0, kg, vg, valid


def _swa_probs(qh, kg, valid, sink):
    z = jnp.where(valid, _dot_nt(qh, kg) * SCALE, NEG)
    m = jnp.maximum(jnp.max(z, axis=1, keepdims=True), sink)
    pexp = jnp.exp(z - m)
    psink = jnp.exp(sink - m)
    inv = 1.0 / (jnp.sum(pexp, axis=1, keepdims=True) + psink)
    return pexp * inv, psink * inv


def _stacked_sink(sink_row):
    s0 = jnp.sum(jnp.where(_head_mask(0), sink_row, 0.0), axis=1, keepdims=True) * (1.0 / HEAD_DIM)
    s1 = jnp.sum(jnp.where(_head_mask(1), sink_row, 0.0), axis=1, keepdims=True) * (1.0 / HEAD_DIM)
    top = lax.broadcasted_iota(jnp.int32, (2 * SWA_TQ, 1), 0) < SWA_TQ
    return jnp.where(top, s0, s1)


def swa_attn_fwd(qn, kn, proj, sink_p, next_shards, *, name):
    s = qn.shape[0]
    nq = s // SWA_TQ
    n_pairs = SWA_Q_WIDTH // LANES
    v_blk = (3 * SB_WIDTH + SWA_Q_WIDTH + SWA_KV_WIDTH) // LANES
    hosted = next_shards is not None
    n_sh = len(next_shards) if hosted else 0

    def body(q_ref, k_ref, v_ref, s_ref, *refs):
        p = pl.program_id(0)
        if hosted:
            gather = _Gather(refs[:n_sh], refs[n_sh + 1:2 * n_sh + 1], *refs[2 * n_sh + 1:])
            o_ref = refs[n_sh]

            @pl.when(p == 0)
            def _():
                gather.start()

            @pl.when(p == n_pairs - 1)
            def _():
                gather.forward()
        else:
            o_ref = refs[0]
        second_kv = (p // 2) == 1
        sink = _stacked_sink(s_ref[...])

        def tile(i, carry):
            q0, _, kg, vg, valid = _swa_tile(i, k_ref, v_ref, second_kv)
            probs, _ = _swa_probs(_stack_heads(q_ref[pl.ds(q0, SWA_TQ), :]), kg, valid, sink)
            o_ref[pl.ds(q0, SWA_TQ), :] = _unstack_heads(_dot(probs.astype(BF16), vg), SWA_TQ)
            return carry

        lax.fori_loop(0, nq, tile, 0, unroll=4)

        if hosted:
            @pl.when(p == n_pairs - 1)
            def _():
                gather.finish()

    pair = pl.BlockSpec((s, LANES), lambda p: (0, p))
    whole = pl.BlockSpec((s, LANES), lambda p: (0, 0))
    o_shape = jax.ShapeDtypeStruct((s, SWA_Q_WIDTH), F32)
    in_specs = [pair, whole, pl.BlockSpec((s, LANES), lambda p: (0, v_blk)),
                pl.BlockSpec((None, 1, LANES), lambda p: (p, 0, 0))]
    if not hosted:
        return pl.pallas_call(
            body, name=name, grid=(n_pairs,), out_shape=o_shape, in_specs=in_specs, out_specs=pair,
            compiler_params=_params("arbitrary"),
        )(qn, kn, proj, sink_p)
    outs = pl.pallas_call(
        body, name=name, grid=(n_pairs,), out_shape=(o_shape,) + _gather_out_shapes(next_shards),
        in_specs=in_specs + [ANY] * n_sh, out_specs=(pair,) + (ANY,) * n_sh, scratch_shapes=_gather_scratch(n_sh),
        compiler_params=pltpu.CompilerParams(dimension_semantics=("arbitrary",), vmem_limit_bytes=VMEM_LIMIT,
                                             has_side_effects=True),
    )(qn, kn, proj, sink_p, *next_shards)
    return outs[0], list(outs[1:])


def swa_attn_bwd(qn, kn, proj, sink_p, o, do, *, name):
    s = qn.shape[0]
    nq = s // SWA_TQ
    v_blk = (3 * SB_WIDTH + SWA_Q_WIDTH + SWA_KV_WIDTH) // LANES
    fold_rows = min(512, s)

    def body(q_ref, k_ref, v_ref, s_ref, o_ref, do_ref, dq_ref, dk_ref, dv_ref, ds_ref, acc_k, acc_v):
        p = pl.program_id(0)
        second_kv = (p // 2) == 1
        sink = _stacked_sink(s_ref[...])

        @pl.when(p % 2 == 0)
        def _():
            acc_k[...] = jnp.zeros_like(acc_k)
            acc_v[...] = jnp.zeros_like(acc_v)

        def tile(i, dsink):
            q0, k0, kg, vg, valid = _swa_tile(i, k_ref, v_ref, second_kv)
            qh = _stack_heads(q_ref[pl.ds(q0, SWA_TQ), :])
            doh = _stack_heads(do_ref[pl.ds(q0, SWA_TQ), :])
            doh_b = doh.astype(BF16)
            ov = o_ref[pl.ds(q0, SWA_TQ), :]
            delta = jnp.sum(doh * jnp.concatenate([ov, ov], axis=0), axis=1, keepdims=True)
            probs, psink = _swa_probs(qh, kg, valid, sink)
            dz = probs * (_dot_nt(doh_b, vg) - delta)
            dzb = (dz * SCALE).astype(BF16)
            dq_ref[pl.ds(q0, SWA_TQ), :] = _unstack_heads(_dot(dzb, kg), SWA_TQ)
            acc_k[pl.ds(k0, SWA_TK), :] += _dot_tn(dzb, qh)
            acc_v[pl.ds(k0, SWA_TK), :] += _dot_tn(probs.astype(BF16), doh_b)
            pd = psink * delta
            return dsink - jnp.where(_head_mask(0), jnp.sum(pd[:SWA_TQ], axis=0, keepdims=True),
                                     jnp.sum(pd[SWA_TQ:], axis=0, keepdims=True))

        ds_ref[...] = lax.fori_loop(0, nq, tile, jnp.zeros((1, LANES), F32), unroll=4)

        def fold_into(first_head):
            def fold(r, carry):
                rows = pl.ds(pl.multiple_of(r * fold_rows, fold_rows), fold_rows)
                for acc, out in ((acc_k, dk_ref), (acc_v, dv_ref)):
                    a = acc[rows, :]
                    both = a + _swap_halves(a)
                    if first_head:
                        out[rows, :] = jnp.where(_lane_lo(), both, 0.0)
                    else:
                        out[rows, :] = jnp.where(_lane_lo(), out[rows, :], both)
                return carry

            lax.fori_loop(0, s // fold_rows, fold, 0)

        @pl.when(p == 1)
        def _():
            fold_into(True)

        @pl.when(p == 3)
        def _():
            fold_into(False)

    pair = pl.BlockSpec((s, LANES), lambda p: (0, p))
    whole = pl.BlockSpec((s, LANES), lambda p: (0, 0))
    return pl.pallas_call(
        body, name=name, grid=(SWA_Q_WIDTH // LANES,),
        out_shape=(jax.ShapeDtypeStruct((s, SWA_Q_WIDTH), F32), jax.ShapeDtypeStruct((s, LANES), F32),
                   jax.ShapeDtypeStruct((s, LANES), F32), jax.ShapeDtypeStruct((SWA_Q_WIDTH // LANES, 1, LANES), F32)),
        in_specs=[pair, whole, pl.BlockSpec((s, LANES), lambda p: (0, v_blk)),
                  pl.BlockSpec((None, 1, LANES), lambda p: (p, 0, 0)), pair, pair],
        out_specs=(pair, whole, whole, pl.BlockSpec((None, 1, LANES), lambda p: (p, 0, 0))),
        scratch_shapes=[pltpu.VMEM((s, LANES), F32), pltpu.VMEM((s, LANES), F32)],
        compiler_params=_params("arbitrary"),
    )(qn, kn, proj, sink_p, o, do)


def _rope_tables(s):
    inv_freq = 1.0 / (ROPE_THETA ** (jnp.arange(0, HEAD_DIM, 2, dtype=F32) / HEAD_DIM))
    ang = jnp.arange(s, dtype=F32)[:, None] * inv_freq[None, :]
    cos, sin = jnp.cos(ang), jnp.sin(ang)
    cos_p = jnp.tile(jnp.concatenate([cos, cos], axis=1), (1, LANES // HEAD_DIM))
    sin_p = jnp.tile(jnp.concatenate([-sin, sin], axis=1), (1, LANES // HEAD_DIM))
    return cos_p, sin_p


def _lane_tile(v, reps):
    return jnp.tile(v.reshape(1, -1), (1, reps))


def _natural(stack, name):
    n, r, c = stack.shape
    if name in ROW_SHARDED:
        return stack.reshape(n * r, c)
    if name == "w_up":
        return stack
    return jnp.transpose(stack, (1, 0, 2)).reshape(r, n * c)


def _pack_small(tree):
    flat = jnp.concatenate([tree[n].reshape(-1) for n in SMALL_NAMES])
    rows = -(-flat.shape[0] // (8 * LANES)) * 8
    return jnp.pad(flat, (0, rows * LANES - flat.shape[0])).reshape(rows, LANES)


def _unpack_small(packed, shapes):
    flat, out, off = packed.reshape(-1), {}, 0
    for n in SMALL_NAMES:
        size = shapes[n][0] * shapes[n][1]
        out[n] = flat[off:off + size].reshape(shapes[n])
        off += size
    return out


def train_step(x, target, weights, mom_m, mom_v):
    s = x.shape[0]
    cos_p, sin_p = _rope_tables(s)
    tri = (jnp.arange(SB_TK)[:, None] > jnp.arange(SB_TK)[None, :]).astype(BF16)
    tri = jnp.concatenate([tri, tri], axis=0)
    shards = [[weights[n][l].astype(BF16) for n in MATRIX_NAMES] for l in range(DEPTH)]
    core = lax.axis_index("c").astype(jnp.int32).reshape(1)
    chip = (2 * lax.axis_index("x") + lax.axis_index("y")).astype(jnp.int32).reshape(1)

    stack_in = gather_shards(shards[0][:1], name="gather_w_in0")[0]
    saved = []
    for l in range(DEPTH):
        mats = {"w_in": _natural(stack_in, "w_in")}
        g_mix = weights["mix_norm_g"][l].reshape(1, D_MODEL)
        g_mlp = weights["mlp_norm_g"][l].reshape(1, D_MODEL)
        gq = _lane_tile(weights["q_norm_g"][l], LANES // HEAD_DIM)
        gk = _lane_tile(weights["k_norm_g"][l], LANES // HEAD_DIM)
        sink_p = jnp.repeat(weights["sinks"][l].reshape(SWA_Q_WIDTH // LANES, 2), HEAD_DIM, axis=1)
        sink_p = sink_p.reshape(SWA_Q_WIDTH // LANES, 1, LANES)
        h, proj, gates = norm_matmul(x, g_mix, mats["w_in"], gate_split=ATTN_WIDTH, name="in_proj")
        o_sb, stacks = sb_attn_fwd(proj, tri, shards[l][1:], name="sb_fwd_gather")
        mats.update({n: _natural(stacks[i], n) for i, n in enumerate(MATRIX_NAMES[1:])})
        qn, kn = swa_prep_fwd(proj, cos_p, sin_p, gq, gk, name="swa_prep")
        if l + 1 < DEPTH:
            o_sw, (stack_in,) = swa_attn_fwd(qn, kn, proj, sink_p, shards[l + 1][:1], name="swa_fwd_gather")
        else:
            o_sw = swa_attn_fwd(qn, kn, proj, sink_p, None, name="swa_fwd")
        x1, y_sb, y_sw, merged = merge_out_fwd(x, o_sb, o_sw, gates, mats["w_branch_sb"], mats["w_branch_swa"],
                                               mats["w_out"], name="merge_out")
        h2, u = norm_matmul(x1, g_mlp, mats["w_up"], gate_split=None, name="mlp_up")
        x2 = mlp_down_fwd(x1, u, mats["w_down"], name="mlp_down")
        saved.append(dict(x=x, h=h, proj=proj, gates=gates, o_sb=o_sb, qn=qn, kn=kn, o_sw=o_sw, y_sb=y_sb, y_sw=y_sw,
                          merged=merged, x1=x1, h2=h2, u=u, g_mix=g_mix, g_mlp=g_mlp, gq=gq, gk=gk, sink_p=sink_p,
                          mats=mats))
        x = x2

    dx, dxb, loss = loss_head(x, target, name="loss_head")

    shard_shapes = [weights[n].shape[1:] for n in MATRIX_NAMES]
    parts = [lax.empty((DEPTH, N_CHIPS) + sh, BF16) for sh in shard_shapes]
    lands = [lax.empty((DEPTH, 3) + sh, BF16) for sh in shard_shapes]
    small_grads = {n: [None] * DEPTH for n in SMALL_NAMES}
    half = D_MODEL // 2
    for l in reversed(range(DEPTH)):
        a = saved[l]
        mats = a["mats"]
        du = mlp_bwd_up(dxb, a["u"], mats["w_down"], name="mlp_bwd_up")
        dw_down = matmul_tn(a["u"], [dxb], a_block=half, out_cols=None, relu2=True, name="dw_down")
        dw_up = matmul_tn(a["h2"], [du], a_block=half, out_cols=du.shape[1] // N_DEV, relu2=False, name="dw_up")
        dx1, dx1b, dg_mlp = matmul_nt_norm_bwd([du], mats["w_up"], a["x1"], a["g_mlp"], dx, name="mlp_bwd_norm")
        small_grads["mlp_norm_g"][l] = dg_mlp.reshape(D_MODEL)
        dw_out = matmul_tn(a["merged"], [dx1b], a_block=half, out_cols=None, relu2=False, name="dw_out")
        dy_sb, dy_sw, do_sb, do_sw, dgl = out_bwd(dx1b, mats["w_out"], a["gates"], a["y_sb"], a["y_sw"],
                                                  mats["w_branch_sb"], mats["w_branch_swa"], name="out_bwd")
        dw_bsb = matmul_tn(a["o_sb"], [dy_sb], a_block=half, out_cols=D_MODEL // N_DEV, relu2=False, name="dw_branch_sb")
        dw_bsw = matmul_tn(a["o_sw"], [dy_sw], a_block=half, out_cols=D_MODEL // N_DEV, relu2=False, name="dw_branch_swa")

        rest = [dw_bsb, dw_bsw, dw_out.reshape((N_DEV,) + shard_shapes[3]), dw_up,
                dw_down.reshape((N_DEV,) + shard_shapes[5])]
        landed = pair_exchange(rest, name="grad_pair_exchange_rest")
        parts[1:] = pair_sum(l, rest, landed, parts[1:], core, name="grad_pair_sum_rest")
        items = [(l, w) for w in range(1, N_MATS)] + ([(l + 1, 0)] if l + 1 < DEPTH else [])
        dq_sb, dk_sb, dv_sb, lands = sb_attn_bwd(a["proj"], tri, a["o_sb"], do_sb, (items, parts, lands),
                                                 name="sb_bwd_exchange")
        dqn, dkn, dv_sw, dsink = swa_attn_bwd(a["qn"], a["kn"], a["proj"], a["sink_p"], a["o_sw"], do_sw, name="swa_bwd")
        dq_sw, dk_sw, dv_swb, dgq, dgk = swa_prep_bwd(a["proj"], cos_p, sin_p, a["gq"], a["gk"], dqn, dkn, dv_sw,
                                                      name="swa_prep_bwd")
        small_grads["q_norm_g"][l] = dgq.reshape(SWA_Q_WIDTH // HEAD_DIM, HEAD_DIM).sum(0)
        small_grads["k_norm_g"][l] = dgk.reshape(LANES // HEAD_DIM, HEAD_DIM).sum(0)
        small_grads["sinks"][l] = dsink[:, 0, ::HEAD_DIM].reshape(SWA_Q_WIDTH // HEAD_DIM)
        pieces = [dq_sb, dk_sb, dv_sb, dq_sw, dk_sw, dv_swb, dgl]
        dw_in = matmul_tn(a["h"], pieces, a_block=half, out_cols=None, relu2=False, name="dw_in")
        g_in = [jnp.transpose(dw_in.reshape(D_MODEL, N_DEV, IN_WIDTH // N_DEV), (1, 0, 2))]
        landed = pair_exchange(g_in, name="grad_pair_exchange_in")
        parts[:1] = pair_sum(l, g_in, landed, parts[:1], core, name="grad_pair_sum_in")
        dx, dxb, dg_mix = matmul_nt_norm_bwd(pieces, mats["w_in"], a["x"], a["g_mix"], dx1, name="in_proj_bwd")
        small_grads["mix_norm_g"][l] = dg_mix.reshape(D_MODEL)
    lands = chip_exchange([(0, 0)], parts, lands, name="grad_chip_exchange_in0")

    out_g, out_d, out_m, out_v = {}, {}, {}, {}
    for i, n in enumerate(MATRIX_NAMES):
        out_g[n], out_d[n], out_m[n], out_v[n] = reduce_adamw(parts[i], lands[i], chip, weights[n], mom_m[n], mom_v[n],
                                                              name="adamw_" + n)
    small_shapes = {n: weights[n].shape for n in SMALL_NAMES}
    small_all = gather_small(_pack_small({n: jnp.stack(v) for n, v in small_grads.items()}), name="gather_small_grads")
    sg, sd, sm, sv = small_adamw(small_all, _pack_small(weights), _pack_small(mom_m), _pack_small(mom_v),
                                 name="small_adamw")
    for tree, packed_small in ((out_g, sg), (out_d, sd), (out_m, sm), (out_v, sv)):
        tree.update(_unpack_small(packed_small, small_shapes))
    return loss, dx, (out_g, out_d, out_m, out_v)


def kernel(x, mix_norm_g, w_in, q_norm_g, k_norm_g, sinks, w_branch_sb, w_branch_swa, w_out, mlp_norm_g, w_up, w_down, loss_target, m_mix_norm_g, m_w_in, m_q_norm_g, m_k_norm_g, m_sinks, m_w_branch_sb, m_w_branch_swa, m_w_out, m_mlp_norm_g, m_w_up, m_w_down, v_mix_norm_g, v_w_in, v_q_norm_g, v_k_norm_g, v_sinks, v_w_branch_sb, v_w_branch_swa, v_w_out, v_mlp_norm_g, v_w_up, v_w_down):
    weights = dict(mix_norm_g=mix_norm_g, w_in=w_in, q_norm_g=q_norm_g, k_norm_g=k_norm_g, sinks=sinks,
                   w_branch_sb=w_branch_sb, w_branch_swa=w_branch_swa, w_out=w_out, mlp_norm_g=mlp_norm_g, w_up=w_up,
                   w_down=w_down)
    mom_m = dict(mix_norm_g=m_mix_norm_g, w_in=m_w_in, q_norm_g=m_q_norm_g, k_norm_g=m_k_norm_g, sinks=m_sinks,
                 w_branch_sb=m_w_branch_sb, w_branch_swa=m_w_branch_swa, w_out=m_w_out, mlp_norm_g=m_mlp_norm_g,
                 w_up=m_w_up, w_down=m_w_down)
    mom_v = dict(mix_norm_g=v_mix_norm_g, w_in=v_w_in, q_norm_g=v_q_norm_g, k_norm_g=v_k_norm_g, sinks=v_sinks,
                 w_branch_sb=v_w_branch_sb, w_branch_swa=v_w_branch_swa, w_out=v_w_out, mlp_norm_g=v_mlp_norm_g,
                 w_up=v_w_up, w_down=v_w_down)
    loss_part, grad_x, outs = train_step(x[0], loss_target[0], weights, mom_m, mom_v)
    loss = lax.psum(loss_part[0, 0], MESH_AXES)
    return (loss, grad_x[None], *[outs[0][n] for n in WEIGHT_ORDER], *[outs[1][n] for n in WEIGHT_ORDER],
            *[outs[2][n] for n in WEIGHT_ORDER], *[outs[3][n] for n in WEIGHT_ORDER])
```

```python
import functools
import math

import jax
import jax.numpy as jnp
from jax import lax
from jax.experimental import pallas as pl
from jax.experimental.pallas import tpu as pltpu

F32 = jnp.float32
BF16 = jnp.bfloat16

DEPTH = 4
D_MODEL = 1024
HEAD_DIM = 64
LANES = 128
WINDOW = 128
SB_WIDTH = 512
SWA_Q_WIDTH = 512
SWA_KV_WIDTH = 128
ATTN_WIDTH = 3 * SB_WIDTH + SWA_Q_WIDTH + 2 * SWA_KV_WIDTH
IN_WIDTH = ATTN_WIDTH + 2 * D_MODEL
ROPE_THETA = 10000.0
NORM_EPS = 1e-6
SCALE = HEAD_DIM ** -0.5
NEG = -1e30
N_DEV = 8
N_CHIPS = 4

ADAM_LR = 0.001
ADAM_B1 = 0.9
ADAM_B2 = 0.999
ADAM_EPS = 1e-08
ADAM_WD = 0.01
ADAM_STEP = 10

SB_TQ = 128
SB_TK = 256
SB_CUTOFF = -88.0
SWA_TQ = 128
SWA_TK = 256
ROW_TILE = 256
VMEM_LIMIT = 56 * 1024 * 1024

MATRIX_NAMES = ("w_in", "w_branch_sb", "w_branch_swa", "w_out", "w_up", "w_down")
W_IN, W_BSB, W_BSW, W_OUT, W_UP, W_DOWN = range(6)
ROW_SHARDED = ("w_out", "w_down")
SMALL_NAMES = ("mix_norm_g", "q_norm_g", "k_norm_g", "sinks", "mlp_norm_g")
WEIGHT_ORDER = ("mix_norm_g", "w_in", "q_norm_g", "k_norm_g", "sinks", "w_branch_sb", "w_branch_swa", "w_out",
                "mlp_norm_g", "w_up", "w_down")
MESH_AXES = ("x", "y", "c")

ANY = pl.BlockSpec(memory_space=pl.ANY)
MESH = pl.DeviceIdType.MESH


def _params(*sem):
    return pltpu.CompilerParams(dimension_semantics=sem, vmem_limit_bytes=VMEM_LIMIT)


def _dot(a, b):
    return jnp.dot(a, b, preferred_element_type=F32)


def _dot_nt(a, b):
    return lax.dot_general(a, b, (((1,), (1,)), ((), ())), preferred_element_type=F32)


def _dot_tn(a, b):
    return lax.dot_general(a, b, (((0,), (0,)), ((), ())), preferred_element_type=F32)


def _split_bf16(x):
    hi = x.astype(BF16)
    lo = (x - hi.astype(F32)).astype(BF16)
    return hi, lo


def _rsqrt_ms(x):
    return lax.rsqrt(jnp.mean(x * x, axis=-1, keepdims=True) + NORM_EPS)


def _place():
    return lax.axis_index("x"), lax.axis_index("y"), lax.axis_index("c")


class _Gather:
    def __init__(self, x_refs, out_refs, send_sems, recv_sems, local_sems):
        self.x_refs, self.out_refs = x_refs, out_refs
        self.send_sems, self.recv_sems, self.local_sems = send_sems, recv_sems, local_sems
        self.n = len(x_refs)
        x, y, c = _place()
        self.c = c
        self.me, self.sibling = (x, y, c), (x, y, 1 - c)
        self.chips = [(1 - x, y), (x, 1 - y), (1 - x, 1 - y)]

    def _copy(self, k, w, blk, to, own=False):
        dst = self.out_refs[w].at[4 * blk[0] + 2 * blk[1] + blk[2]]
        return pltpu.make_async_remote_copy(
            src_ref=self.x_refs[w] if own else dst, dst_ref=dst, send_sem=self.send_sems.at[k, w],
            recv_sem=self.recv_sems.at[k, w], device_id=to, device_id_type=MESH)

    def _mine(self, w):
        me = self.me
        return pltpu.make_async_copy(self.x_refs[w], self.out_refs[w].at[4 * me[0] + 2 * me[1] + me[2]],
                                     self.local_sems.at[w])

    def _first(self, w):
        return [self._copy(0, w, self.me, self.sibling, own=True)] + [
            self._copy(1 + j, w, self.me, (*chip, self.c), own=True) for j, chip in enumerate(self.chips)]

    def _passed(self, j, w):
        return self._copy(4 + j, w, (*self.chips[j], self.c), self.sibling)

    def start(self):
        for w in range(self.n):
            self._mine(w).start()
            for cp in self._first(w):
                cp.start()

    def relay(self):
        for j, chip in enumerate(self.chips):
            for w in range(self.n):
                self._copy(1 + j, w, (*chip, self.c), self.me).wait_recv()
                self._passed(j, w).start()

    def finish(self):
        for w in range(self.n):
            self._copy(0, w, self.sibling, self.me).wait_recv()
            for j, chip in enumerate(self.chips):
                self._copy(4 + j, w, (*chip, 1 - self.c), self.me).wait_recv()
            for cp in self._first(w):
                cp.wait_send()
            for j in range(3):
                self._passed(j, w).wait_send()
            self._mine(w).wait()


class GatherJob:
    def __init__(self, shards):
        n = len(shards)
        self.inputs = list(shards)
        self.out_shapes = [jax.ShapeDtypeStruct((N_DEV,) + s.shape, s.dtype) for s in shards]
        self.aliases = {}
        self.scratch = [pltpu.SemaphoreType.DMA((7, n)), pltpu.SemaphoreType.DMA((7, n)),
                        pltpu.SemaphoreType.DMA((n,))]

    def bind(self, in_refs, out_refs, scratch_refs):
        return _Gather(in_refs, out_refs, *scratch_refs)


class _Copies:
    def __init__(self, copies):
        self.copies = copies

    def start(self):
        for cp in self.copies:
            cp.start()

    def relay(self):
        pass

    def finish(self):
        for cp in self.copies:
            cp.wait_recv()
        for cp in self.copies:
            cp.wait_send()


class ChipJob:
    def __init__(self, items, parts, lands):
        self.ws = sorted({w for _, w in items})
        n = len(self.ws)
        self.items = [(layer, self.ws.index(w)) for layer, w in items]
        self.inputs = [parts[w] for w in self.ws] + [lands[w] for w in self.ws]
        self.out_shapes = [jax.ShapeDtypeStruct(lands[w].shape, lands[w].dtype) for w in self.ws]
        self.aliases = {n + i: i for i in range(n)}
        self.scratch = [pltpu.SemaphoreType.DMA((3, n)), pltpu.SemaphoreType.DMA((3, n))]

    def bind(self, in_refs, out_refs, scratch_refs):
        send_sems, recv_sems = scratch_refs
        x, y, c = _place()
        chips = [(1 - x, y), (x, 1 - y), (1 - x, 1 - y)]
        return _Copies([pltpu.make_async_remote_copy(
            src_ref=in_refs[i].at[layer, 2 * px + py], dst_ref=out_refs[i].at[layer, j],
            send_sem=send_sems.at[j, i], recv_sem=recv_sems.at[j, i], device_id=(px, py, c), device_id_type=MESH)
            for layer, i in self.items for j, (px, py) in enumerate(chips)])


class PairJob:
    def __init__(self, grads):
        n = len(grads)
        self.inputs = list(grads)
        self.out_shapes = [jax.ShapeDtypeStruct((N_CHIPS,) + g.shape[1:], g.dtype) for g in grads]
        self.aliases = {}
        self.scratch = [pltpu.SemaphoreType.DMA((N_CHIPS, n)), pltpu.SemaphoreType.DMA((N_CHIPS, n))]

    def bind(self, in_refs, out_refs, scratch_refs):
        send_sems, recv_sems = scratch_refs
        x, y, c = _place()
        return _Copies([pltpu.make_async_remote_copy(
            src_ref=in_refs[w].at[2 * k + (1 - c)], dst_ref=out_refs[w].at[k], send_sem=send_sems.at[k, w],
            recv_sem=recv_sems.at[k, w], device_id=(x, y, 1 - c), device_id_type=MESH)
            for w in range(len(in_refs)) for k in range(N_CHIPS)])


def _call(body, *, name, grid, in_specs, out_specs, out_shape, args, scratch_shapes=(), ride=()):
    out_specs, out_shape, in_specs = tuple(out_specs), tuple(out_shape), list(in_specs)
    scratch_shapes = list(scratch_shapes)
    order = ("arbitrary",) * len(grid)
    if not ride:
        outs = pl.pallas_call(body, name=name, grid=grid, in_specs=in_specs, out_specs=out_specs, out_shape=out_shape,
                              scratch_shapes=scratch_shapes, compiler_params=_params(*order))(*args)
        return tuple(outs), []
    n_in, n_out, n_scr = len(in_specs), len(out_specs), len(scratch_shapes)
    n_steps = math.prod(grid)
    relay_at = n_steps - max(1, n_steps // 4)

    def split(refs, pos, counts):
        groups = []
        for k in counts:
            groups.append(refs[pos:pos + k])
            pos += k
        return groups, pos

    def wrapped(*refs):
        ins, pos = refs[:n_in], n_in
        job_in, pos = split(refs, pos, [len(j.inputs) for j in ride])
        outs, pos = refs[pos:pos + n_out], pos + n_out
        job_out, pos = split(refs, pos, [len(j.out_shapes) for j in ride])
        scr, pos = refs[pos:pos + n_scr], pos + n_scr
        job_scr, pos = split(refs, pos, [len(j.scratch) for j in ride])
        bound = [j.bind(i, o, s) for j, i, o, s in zip(ride, job_in, job_out, job_scr)]
        step = pl.program_id(0)
        for axis in range(1, len(grid)):
            step = step * grid[axis] + pl.program_id(axis)

        @pl.when(step == 0)
        def _():
            for b in bound:
                b.start()

        @pl.when(step == relay_at)
        def _():
            for b in bound:
                b.relay()

        body(*ins, *outs, *scr)

        @pl.when(step == n_steps - 1)
        def _():
            for b in bound:
                b.finish()

    aliases, in_pos, out_pos = {}, n_in, n_out
    for j in ride:
        aliases.update({in_pos + i: out_pos + o for i, o in j.aliases.items()})
        in_pos += len(j.inputs)
        out_pos += len(j.out_shapes)
    results = pl.pallas_call(
        wrapped, name=name, grid=grid, in_specs=in_specs + [ANY] * (in_pos - n_in),
        out_specs=out_specs + (ANY,) * (out_pos - n_out),
        out_shape=out_shape + tuple(s for j in ride for s in j.out_shapes),
        scratch_shapes=scratch_shapes + [s for j in ride for s in j.scratch], input_output_aliases=aliases,
        compiler_params=pltpu.CompilerParams(dimension_semantics=order, vmem_limit_bytes=VMEM_LIMIT,
                                             has_side_effects=True),
    )(*args, *[a for j in ride for a in j.inputs])
    job_results, pos = split(list(results), n_out, [len(j.out_shapes) for j in ride])
    return tuple(results[:n_out]), job_results


def exchange_alone(job, *, name):
    n_in, n_out = len(job.inputs), len(job.out_shapes)

    def body(*refs):
        b = job.bind(refs[:n_in], refs[n_in:n_in + n_out], refs[n_in + n_out:])
        b.start()
        b.relay()
        b.finish()

    return list(pl.pallas_call(
        body, name=name, out_shape=tuple(job.out_shapes), in_specs=[ANY] * n_in, out_specs=(ANY,) * n_out,
        scratch_shapes=job.scratch, input_output_aliases=job.aliases,
        compiler_params=pltpu.CompilerParams(has_side_effects=True),
    )(*job.inputs))


PAIR_SUM_CHUNKS = 2


def pair_sum(layer, grads, landed, parts, core, *, name):
    n = len(grads)

    def body(c_ref, *refs):
        g_refs, l_refs, o_refs = refs[:n], refs[n:2 * n], refs[3 * n:]
        for w in range(n):
            o_refs[w][...] = (g_refs[w][...].astype(F32) + l_refs[w][...].astype(F32)).astype(BF16)

    def blk(g):
        return (None, g.shape[1] // PAIR_SUM_CHUNKS, g.shape[2])

    in_specs = [pl.BlockSpec(blk(g), lambda k, i, c_ref: (2 * k + c_ref[0], i, 0)) for g in grads]
    in_specs += [pl.BlockSpec(blk(g), lambda k, i, c_ref: (k, i, 0)) for g in grads]
    in_specs += [ANY] * n
    out_specs = tuple(pl.BlockSpec((None,) + blk(g), lambda k, i, c_ref: (layer, k, i, 0)) for g in grads)
    return list(pl.pallas_call(
        body, name=name, out_shape=tuple(jax.ShapeDtypeStruct(p.shape, p.dtype) for p in parts),
        grid_spec=pltpu.PrefetchScalarGridSpec(num_scalar_prefetch=1, grid=(N_CHIPS, PAIR_SUM_CHUNKS),
                                               in_specs=in_specs, out_specs=out_specs),
        input_output_aliases={1 + 2 * n + w: w for w in range(n)},
        compiler_params=_params("parallel", "parallel"),
    )(core, *grads, *landed, *parts))


def _adamw(w, g, m, v):
    m = ADAM_B1 * m + (1.0 - ADAM_B1) * g
    v = ADAM_B2 * v + (1.0 - ADAM_B2) * (g * g)
    m_hat = m / (1.0 - ADAM_B1 ** ADAM_STEP)
    v_hat = v / (1.0 - ADAM_B2 ** ADAM_STEP)
    delta = -ADAM_LR * (m_hat / (jnp.sqrt(v_hat) + ADAM_EPS) + ADAM_WD * w)
    return delta, m, v


def reduce_adamw(part, land, chip, w, m, v, *, name):
    _, r, c = w.shape
    tr = min(r, 256)

    def body(k_ref, own_ref, l0_ref, l1_ref, l2_ref, w_ref, m_ref, v_ref, g_out, d_out, m_out, v_out):
        g = own_ref[...].astype(F32) + l0_ref[...].astype(F32) + l1_ref[...].astype(F32) + l2_ref[...].astype(F32)
        delta, m_new, v_new = _adamw(w_ref[...], g, m_ref[...], v_ref[...])
        g_out[...] = g
        d_out[...] = delta
        m_out[...] = m_new
        v_out[...] = v_new

    row = pl.BlockSpec((None, tr, c), lambda l, i, k_ref: (l, i, 0))

    def slot(j):
        return pl.BlockSpec((None, None, tr, c), lambda l, i, k_ref: (l, j, i, 0))

    return pl.pallas_call(
        body, name=name, out_shape=(jax.ShapeDtypeStruct(w.shape, F32),) * 4,
        grid_spec=pltpu.PrefetchScalarGridSpec(
            num_scalar_prefetch=1, grid=(DEPTH, r // tr),
            in_specs=[pl.BlockSpec((None, None, tr, c), lambda l, i, k_ref: (l, k_ref[0], i, 0)), slot(0), slot(1),
                      slot(2), row, row, row],
            out_specs=(row, row, row, row)),
        compiler_params=_params("parallel", "parallel"),
    )(chip, part, land, land, land, w, m, v)


def gather_small(block, *, name):
    def body(x_ref, out_ref, send_sems, recv_sems, local_sem):
        x, y, c = _place()
        me = 4 * x + 2 * y + c
        mine = pltpu.make_async_copy(x_ref, out_ref.at[me], local_sem)
        mine.start()
        peers = [(x ^ (k >> 2), y ^ ((k >> 1) & 1), c ^ (k & 1)) for k in range(1, N_DEV)]
        copies = [pltpu.make_async_remote_copy(
            src_ref=x_ref, dst_ref=out_ref.at[me], send_sem=send_sems.at[k], recv_sem=recv_sems.at[k],
            device_id=peer, device_id_type=MESH) for k, peer in enumerate(peers)]
        for cp in copies:
            cp.start()
        for k, (px, py, pc) in enumerate(peers):
            pltpu.make_async_remote_copy(
                src_ref=x_ref, dst_ref=out_ref.at[4 * px + 2 * py + pc], send_sem=send_sems.at[k],
                recv_sem=recv_sems.at[k], device_id=(px, py, pc), device_id_type=MESH).wait_recv()
        for cp in copies:
            cp.wait_send()
        mine.wait()

    return pl.pallas_call(
        body, name=name, out_shape=jax.ShapeDtypeStruct((N_DEV,) + block.shape, block.dtype),
        in_specs=[ANY], out_specs=ANY,
        scratch_shapes=[pltpu.SemaphoreType.DMA((7,)), pltpu.SemaphoreType.DMA((7,)), pltpu.SemaphoreType.DMA],
        compiler_params=pltpu.CompilerParams(has_side_effects=True),
    )(block)


def small_adamw(gathered, w, m, v, *, name):
    def body(g_ref, w_ref, m_ref, v_ref, g_out, d_out, m_out, v_out):
        g = g_ref[0]
        for d in range(1, N_DEV):
            g = g + g_ref[d]
        delta, m_new, v_new = _adamw(w_ref[...], g, m_ref[...], v_ref[...])
        g_out[...] = g
        d_out[...] = delta
        m_out[...] = m_new
        v_out[...] = v_new

    return pl.pallas_call(
        body, name=name, out_shape=(jax.ShapeDtypeStruct(w.shape, F32),) * 4,
    )(gathered, w, m, v)


def norm_matmul(x, g, w, *, gate_split, name, ride=()):
    s, d = x.shape
    tm = min(ROW_TILE, s)
    blocked = w.ndim == 3
    n = w.shape[1] if not blocked else w.shape[0] * w.shape[2]

    def body(x_ref, g_ref, w_ref, h_ref, *outs):
        xv = x_ref[...]
        h = ((xv * _rsqrt_ms(xv)) * g_ref[...]).astype(BF16)
        h_ref[...] = h
        if blocked:
            nb = w_ref.shape[2]
            for j in range(w_ref.shape[0]):
                outs[0][:, j * nb:(j + 1) * nb] = _dot(h, w_ref[j]).astype(BF16)
        else:
            p = _dot(h, w_ref[...])
            outs[0][...] = p[:, :gate_split].astype(BF16)
            outs[1][...] = (1.0 / (1.0 + jnp.exp(-p[:, gate_split:]))).astype(BF16)

    row = lambda i: (i, 0)
    fixed = lambda i: (0, 0)
    if blocked:
        out_shape = (jax.ShapeDtypeStruct((s, d), BF16), jax.ShapeDtypeStruct((s, n), BF16))
        out_specs = (pl.BlockSpec((tm, d), row), pl.BlockSpec((tm, n), row))
        w_spec = pl.BlockSpec(w.shape, lambda i: (0, 0, 0))
    else:
        out_shape = (jax.ShapeDtypeStruct((s, d), BF16), jax.ShapeDtypeStruct((s, gate_split), BF16),
                     jax.ShapeDtypeStruct((s, n - gate_split), BF16))
        out_specs = (pl.BlockSpec((tm, d), row), pl.BlockSpec((tm, gate_split), row),
                     pl.BlockSpec((tm, n - gate_split), row))
        w_spec = pl.BlockSpec((d, n), fixed)
    return _call(body, name=name, grid=(s // tm,), out_shape=out_shape, out_specs=out_specs,
                 in_specs=[pl.BlockSpec((tm, d), row), pl.BlockSpec((1, d), fixed), w_spec], args=(x, g, w), ride=ride)


def merge_out_fwd(x, o_sb, o_sw, gates, w_bsb, w_bsw, w_o, *, name):
    s, d = x.shape
    tm = min(ROW_TILE, s)

    def body(x_ref, osb_ref, osw_ref, g_ref, wsb_ref, wsw_ref, wo_ref, x1_ref, ysb_ref, ysw_ref, mg_ref):
        y_sb = _dot(osb_ref[...].astype(BF16), wsb_ref[...])
        y_sw = _dot(osw_ref[...].astype(BF16), wsw_ref[...])
        g = g_ref[...].astype(F32)
        merged = (g[:, :d] * y_sb + g[:, d:] * y_sw).astype(BF16)
        ysb_ref[...] = y_sb.astype(BF16)
        ysw_ref[...] = y_sw.astype(BF16)
        mg_ref[...] = merged
        x1_ref[...] = x_ref[...] + _dot(merged, wo_ref[...])

    row = lambda i: (i, 0)
    fixed = lambda i: (0, 0)
    wd = o_sb.shape[1]
    return pl.pallas_call(
        body, name=name, grid=(s // tm,),
        out_shape=(jax.ShapeDtypeStruct((s, d), F32),) + (jax.ShapeDtypeStruct((s, d), BF16),) * 3,
        in_specs=[pl.BlockSpec((tm, d), row), pl.BlockSpec((tm, wd), row), pl.BlockSpec((tm, wd), row),
                  pl.BlockSpec((tm, 2 * d), row), pl.BlockSpec((wd, d), fixed), pl.BlockSpec((wd, d), fixed),
                  pl.BlockSpec((d, d), fixed)],
        out_specs=(pl.BlockSpec((tm, d), row),) * 4, compiler_params=_params("parallel"),
    )(x, o_sb, o_sw, gates, w_bsb, w_bsw, w_o)


def mlp_down_fwd(x1, u, w_down, *, name):
    s, d = x1.shape
    f = u.shape[1]
    tm = min(ROW_TILE, s)

    def body(x_ref, u_ref, w_ref, o_ref):
        a = jnp.maximum(u_ref[...].astype(F32), 0.0)
        o_ref[...] = x_ref[...] + _dot((a * a).astype(BF16), w_ref[...])

    row = lambda i: (i, 0)
    return pl.pallas_call(
        body, name=name, grid=(s // tm,), out_shape=jax.ShapeDtypeStruct((s, d), F32),
        in_specs=[pl.BlockSpec((tm, d), row), pl.BlockSpec((tm, f), row), pl.BlockSpec((f, d), lambda i: (0, 0))],
        out_specs=pl.BlockSpec((tm, d), row), compiler_params=_params("parallel"),
    )(x1, u, w_down)


def loss_head(y, target, *, name):
    s, d = y.shape
    tm = min(ROW_TILE, s)

    def body(y_ref, t_ref, dy_ref, dyb_ref, loss_ref):
        @pl.when(pl.program_id(0) == 0)
        def _():
            loss_ref[...] = jnp.zeros_like(loss_ref)

        e = y_ref[...] - t_ref[...]
        dy = e * (1.0 / d)
        dy_ref[...] = dy
        dyb_ref[...] = dy.astype(BF16)
        per_row = jnp.sum(e * e, axis=1, keepdims=True) * (0.5 / d)
        loss_ref[...] += jnp.sum(per_row, axis=0, keepdims=True)

    row = lambda i: (i, 0)
    return pl.pallas_call(
        body, name=name, grid=(s // tm,),
        out_shape=(jax.ShapeDtypeStruct((s, d), F32), jax.ShapeDtypeStruct((s, d), BF16),
                   jax.ShapeDtypeStruct((1, 1), F32)),
        in_specs=[pl.BlockSpec((tm, d), row), pl.BlockSpec((tm, d), row)],
        out_specs=(pl.BlockSpec((tm, d), row), pl.BlockSpec((tm, d), row), pl.BlockSpec((1, 1), lambda i: (0, 0))),
        compiler_params=_params("arbitrary"),
    )(y, target)


def mlp_bwd_up(dxb, u, w_down, *, name, ride=()):
    s, d = dxb.shape
    f = u.shape[1]
    tm = min(ROW_TILE, s)

    def body(dx_ref, u_ref, w_ref, du_ref):
        da = _dot_nt(dx_ref[...], w_ref[...])
        du_ref[...] = (da * (2.0 * jnp.maximum(u_ref[...].astype(F32), 0.0))).astype(BF16)

    row = lambda i: (i, 0)
    return _call(body, name=name, grid=(s // tm,), out_shape=(jax.ShapeDtypeStruct((s, f), BF16),),
                 in_specs=[pl.BlockSpec((tm, d), row), pl.BlockSpec((tm, f), row),
                           pl.BlockSpec((f, d), lambda i: (0, 0))],
                 out_specs=(pl.BlockSpec((tm, f), row),), args=(dxb, u, w_down), ride=ride)


def matmul_nt_norm_bwd(pieces, w, x, g, dres, *, name, ride=()):
    s = x.shape[0]
    d = x.shape[1]
    tm = min(ROW_TILE, s)
    blocked = w.ndim == 3
    n_pieces = len(pieces)
    widths = [p.shape[1] for p in pieces]

    def body(*refs):
        p_refs = refs[:n_pieces]
        w_ref, x_ref, g_ref, dres_ref, dx_ref, dxb_ref, dg_ref = refs[n_pieces:]

        @pl.when(pl.program_id(0) == 0)
        def _():
            dg_ref[...] = jnp.zeros_like(dg_ref)

        if blocked:
            nb = w_ref.shape[2]
            dh = _dot_nt(p_refs[0][:, :nb], w_ref[0])
            for j in range(1, w_ref.shape[0]):
                dh = dh + _dot_nt(p_refs[0][:, j * nb:(j + 1) * nb], w_ref[j])
        else:
            dh, off = None, 0
            for p_ref, width in zip(p_refs, widths):
                part = _dot_nt(p_ref[...], w_ref[:, off:off + width])
                dh = part if dh is None else dh + part
                off += width
        xv = x_ref[...]
        r = _rsqrt_ms(xv)
        dyg = dh * g_ref[...]
        dx = dres_ref[...] + r * dyg - xv * ((r * r * r) * jnp.mean(dyg * xv, axis=-1, keepdims=True))
        dx_ref[...] = dx
        dxb_ref[...] = dx.astype(BF16)
        dg_ref[...] += jnp.sum(dh * (xv * r), axis=0, keepdims=True)

    row = lambda i: (i, 0)
    fixed = lambda i: (0, 0)
    w_spec = pl.BlockSpec(w.shape, (lambda i: (0, 0, 0)) if blocked else fixed)
    return _call(
        body, name=name, grid=(s // tm,),
        out_shape=(jax.ShapeDtypeStruct((s, d), F32), jax.ShapeDtypeStruct((s, d), BF16),
                   jax.ShapeDtypeStruct((1, d), F32)),
        in_specs=[pl.BlockSpec((tm, width), row) for width in widths] + [
            w_spec, pl.BlockSpec((tm, d), row), pl.BlockSpec((1, d), fixed), pl.BlockSpec((tm, d), row)],
        out_specs=(pl.BlockSpec((tm, d), row), pl.BlockSpec((tm, d), row), pl.BlockSpec((1, d), fixed)),
        args=(*pieces, w, x, g, dres), ride=ride)


def out_bwd(dx1b, w_o, gates, y_sb, y_sw, w_bsb, w_bsw, *, name):
    s, d = dx1b.shape
    wd = w_bsb.shape[0]
    tm = min(ROW_TILE, s)

    def body(dx_ref, wo_ref, g_ref, ysb_ref, ysw_ref, wsb_ref, wsw_ref, dysb_ref, dysw_ref, dosb_ref, dosw_ref, dgl_ref):
        dm = _dot_nt(dx_ref[...], wo_ref[...])
        g = g_ref[...].astype(F32)
        g0, g1 = g[:, :d], g[:, d:]
        dy_sb = (dm * g0).astype(BF16)
        dy_sw = (dm * g1).astype(BF16)
        dysb_ref[...] = dy_sb
        dysw_ref[...] = dy_sw
        dosb_ref[...] = _dot_nt(dy_sb, wsb_ref[...])
        dosw_ref[...] = _dot_nt(dy_sw, wsw_ref[...])
        dgl_ref[:, :d] = (dm * ysb_ref[...].astype(F32) * (g0 * (1.0 - g0))).astype(BF16)
        dgl_ref[:, d:] = (dm * ysw_ref[...].astype(F32) * (g1 * (1.0 - g1))).astype(BF16)

    row = lambda i: (i, 0)
    fixed = lambda i: (0, 0)
    return pl.pallas_call(
        body, name=name, grid=(s // tm,),
        out_shape=(jax.ShapeDtypeStruct((s, d), BF16), jax.ShapeDtypeStruct((s, d), BF16),
                   jax.ShapeDtypeStruct((s, wd), F32), jax.ShapeDtypeStruct((s, wd), F32),
                   jax.ShapeDtypeStruct((s, 2 * d), BF16)),
        in_specs=[pl.BlockSpec((tm, d), row), pl.BlockSpec((d, d), fixed), pl.BlockSpec((tm, 2 * d), row),
                  pl.BlockSpec((tm, d), row), pl.BlockSpec((tm, d), row), pl.BlockSpec((wd, d), fixed),
                  pl.BlockSpec((wd, d), fixed)],
        out_specs=(pl.BlockSpec((tm, d), row), pl.BlockSpec((tm, d), row), pl.BlockSpec((tm, wd), row),
                   pl.BlockSpec((tm, wd), row), pl.BlockSpec((tm, 2 * d), row)),
        compiler_params=_params("parallel"),
    )(dx1b, w_o, gates, y_sb, y_sw, w_bsb, w_bsw)


def matmul_tn(a, pieces, *, a_block, out_cols, relu2, name):
    s, m = a.shape
    widths = [p.shape[1] for p in pieces]
    n = sum(widths)
    n_pieces = len(pieces)
    ts = min(512 if n >= 4096 else 2048, s)
    n_steps = s // ts
    if out_cols is None:
        out_shape = jax.ShapeDtypeStruct((m // a_block, a_block, n), BF16)
        out_spec = pl.BlockSpec((None, a_block, n), lambda i, k: (i, 0, 0))
    else:
        out_shape = jax.ShapeDtypeStruct((n // out_cols, m, out_cols), BF16)
        out_spec = pl.BlockSpec((n // out_cols, a_block, out_cols), lambda i, k: (0, i, 0))

    def body(a_ref, *refs):
        b_refs, o_ref, acc = refs[:n_pieces], refs[n_pieces], refs[n_pieces + 1]
        k = pl.program_id(1)

        @pl.when(k == 0)
        def _():
            acc[...] = jnp.zeros_like(acc)

        av = a_ref[...]
        if relu2:
            af = jnp.maximum(av.astype(F32), 0.0)
            av = af * af
        av = av.astype(BF16)
        off = 0
        for b_ref in b_refs:
            width = b_ref.shape[1]
            acc[:, off:off + width] += _dot_tn(av, b_ref[...].astype(BF16))
            off += width

        @pl.when(k == n_steps - 1)
        def _():
            if out_cols is None:
                o_ref[...] = acc[...].astype(BF16)
            else:
                for j in range(n // out_cols):
                    o_ref[j] = acc[:, j * out_cols:(j + 1) * out_cols].astype(BF16)

    return pl.pallas_call(
        body, name=name, grid=(m // a_block, n_steps), out_shape=out_shape,
        in_specs=[pl.BlockSpec((ts, a_block), lambda i, k: (k, i))] + [
            pl.BlockSpec((ts, width), lambda i, k: (k, 0)) for width in widths],
        out_specs=out_spec, scratch_shapes=[pltpu.VMEM((a_block, n), F32)],
        compiler_params=_params("parallel", "arbitrary"),
    )(a, *pieces)


def _softplus(z):
    return jnp.maximum(z, 0.0) + jnp.log(1.0 + jnp.exp(-jnp.abs(z)))


def _suffix_sums(x, tri2):
    hi, lo = _split_bf16(x)
    return _dot(jnp.concatenate([hi, lo], axis=1), tri2)


def _head_mask(h):
    return (lax.broadcasted_iota(jnp.int32, (1, LANES), 1) // HEAD_DIM) == h


def _stack_heads(x):
    zero = jnp.zeros_like(x)
    return jnp.concatenate([jnp.where(_head_mask(0), x, zero), jnp.where(_head_mask(1), x, zero)], axis=0)


def _unstack_heads(r, t):
    return jnp.where(_head_mask(0), r[:t], r[t:])


def _sb_positions(q0):
    row = lax.broadcasted_iota(jnp.int32, (2 * SB_TQ, SB_TK), 0)
    col = lax.broadcasted_iota(jnp.int32, (2 * SB_TQ, SB_TK), 1)
    return q0 + jnp.where(row >= SB_TQ, row - SB_TQ, row), col


def _sb_first_key(q0):
    return pl.multiple_of(jnp.maximum(q0 + SB_TQ - SB_TK, 0), SB_TQ)


def _sb_next_key(k_prev):
    return pl.multiple_of(jnp.maximum(k_prev - SB_TK, 0), SB_TQ)


def _sb_rows(q0):
    return pl.ds(pl.multiple_of(2 * q0, 2 * SB_TQ), 2 * SB_TQ)


def sb_attn_fwd(proj, tri2, *, name, ride=()):
    s = proj.shape[0]
    nq = s // SB_TQ
    n_pairs = SB_WIDTH // LANES

    def body(q_ref, k_ref, v_ref, tri_ref, o_ref, c_all):
        def block(qh, k0, live, c):
            z = _dot_nt(qh, k_ref[pl.ds(k0, SB_TK), :])
            sp = _softplus(z)
            lk = jnp.where(live, -sp, 0.0)
            w = jnp.where(live, jnp.exp(z - sp + _suffix_sums(lk, tri_ref[...]) + c), 0.0)
            return _dot(w.astype(BF16), v_ref[pl.ds(k0, SB_TK), :]), c + jnp.sum(lk, axis=1, keepdims=True)

        def load_q(q0):
            return _stack_heads(q_ref[pl.ds(q0, SB_TQ), :]) * SCALE

        def first(qb, carry):
            q0 = pl.multiple_of(qb * SB_TQ, SB_TQ)
            tpos, col = _sb_positions(q0)
            k0 = _sb_first_key(q0)
            acc, c = block(load_q(q0), k0, k0 + col < tpos, jnp.zeros((2 * SB_TQ, 1), F32))
            o_ref[pl.ds(q0, SB_TQ), :] = _unstack_heads(acc, SB_TQ)
            c_all[_sb_rows(q0), :] = jnp.broadcast_to(jnp.where(k0 > 0, c, NEG), (2 * SB_TQ, LANES))
            return carry

        lax.fori_loop(0, nq, first, 0, unroll=2)

        @pl.when(jnp.max(c_all[...]) > SB_CUTOFF)
        def _():
            def more(qb, carry):
                q0 = pl.multiple_of(qb * SB_TQ, SB_TQ)
                c0 = c_all[_sb_rows(q0), 0:1]

                @pl.when(jnp.max(c0) > SB_CUTOFF)
                def _():
                    qh = load_q(q0)
                    _, col = _sb_positions(q0)

                    def cond(st):
                        return jnp.logical_and(st[0] > 0, st[3] > SB_CUTOFF)

                    def step(st):
                        k_prev, c, acc, _ = st
                        k0 = _sb_next_key(k_prev)
                        part, c = block(qh, k0, k0 + col < k_prev, c)
                        return k0, c, acc + part, jnp.max(c)

                    st = lax.while_loop(cond, step, (_sb_first_key(q0), c0, jnp.zeros((2 * SB_TQ, LANES), F32),
                                                     jnp.max(c0)))
                    o_ref[pl.ds(q0, SB_TQ), :] += _unstack_heads(st[2], SB_TQ)

                return carry

            lax.fori_loop(0, nq, more, 0)

    def col_spec(j):
        return pl.BlockSpec((s, LANES), lambda p: (0, j * n_pairs + p))

    (o,), rides = _call(
        body, name=name, grid=(n_pairs,), out_shape=(jax.ShapeDtypeStruct((s, SB_WIDTH), F32),),
        in_specs=[col_spec(0), col_spec(1), col_spec(2), pl.BlockSpec((2 * SB_TK, SB_TK), lambda p: (0, 0))],
        out_specs=(pl.BlockSpec((s, LANES), lambda p: (0, p)),), scratch_shapes=[pltpu.VMEM((2 * s, LANES), F32)],
        args=(proj, proj, proj, tri2), ride=ride)
    return o, rides


def sb_attn_bwd(proj, tri2, o, do, *, name, ride=()):
    s = proj.shape[0]
    nq = s // SB_TQ
    n_pairs = SB_WIDTH // LANES

    def body(q_ref, k_ref, v_ref, tri_ref, o_ref, do_ref, dq_ref, dk_ref, dv_ref, dq_acc, dk_acc, dv_acc, c_all, e_all):
        dk_acc[...] = jnp.zeros_like(dk_acc)
        dv_acc[...] = jnp.zeros_like(dv_acc)

        def load(q0):
            qh = _stack_heads(q_ref[pl.ds(q0, SB_TQ), :]) * SCALE
            doh_b = _stack_heads(do_ref[pl.ds(q0, SB_TQ), :].astype(BF16))
            ov = o_ref[pl.ds(q0, SB_TQ), :]
            dd = jnp.sum(doh_b.astype(F32) * jnp.concatenate([ov, ov], axis=0), axis=1, keepdims=True)
            return qh, doh_b, dd

        def block(qh, doh_b, dd, k0, live, c, ce):
            kt = k_ref[pl.ds(k0, SB_TK), :]
            z = _dot_nt(qh, kt)
            sp = _softplus(z)
            lb = z - sp
            lk = jnp.where(live, -sp, 0.0)
            wb = jnp.where(live, jnp.exp(lb + _suffix_sums(lk, tri_ref[...]) + c), 0.0).astype(BF16)
            e = wb.astype(F32) * _dot_nt(doh_b, v_ref[pl.ds(k0, SB_TK), :])
            dz = jnp.where(live, e - jnp.exp(lb) * (dd - ce - _suffix_sums(e, tri_ref[...])), 0.0)
            dzb = dz.astype(BF16)
            dk_acc[pl.ds(k0, SB_TK), :] += _dot_tn(dzb, qh)
            dv_acc[pl.ds(k0, SB_TK), :] += _dot_tn(wb, doh_b)
            return (_dot(dzb, kt), c + jnp.sum(lk, axis=1, keepdims=True), ce + jnp.sum(e, axis=1, keepdims=True))

        def first(qb, carry):
            q0 = pl.multiple_of(qb * SB_TQ, SB_TQ)
            qh, doh_b, dd = load(q0)
            tpos, col = _sb_positions(q0)
            k0 = _sb_first_key(q0)
            zero = jnp.zeros((2 * SB_TQ, 1), F32)
            dq, c, ce = block(qh, doh_b, dd, k0, k0 + col < tpos, zero, zero)
            dq_acc[pl.ds(q0, SB_TQ), :] = _unstack_heads(dq, SB_TQ)
            c_all[_sb_rows(q0), :] = jnp.broadcast_to(jnp.where(k0 > 0, c, NEG), (2 * SB_TQ, LANES))
            e_all[_sb_rows(q0), :] = jnp.broadcast_to(ce, (2 * SB_TQ, LANES))
            return carry

        lax.fori_loop(0, nq, first, 0, unroll=2)

        @pl.when(jnp.max(c_all[...]) > SB_CUTOFF)
        def _():
            def more(qb, carry):
                q0 = pl.multiple_of(qb * SB_TQ, SB_TQ)
                c0 = c_all[_sb_rows(q0), 0:1]

                @pl.when(jnp.max(c0) > SB_CUTOFF)
                def _():
                    qh, doh_b, dd = load(q0)
                    _, col = _sb_positions(q0)

                    def cond(st):
                        return jnp.logical_and(st[0] > 0, st[4] > SB_CUTOFF)

                    def step(st):
                        k_prev, c, ce, dq, _ = st
                        k0 = _sb_next_key(k_prev)
                        part, c, ce = block(qh, doh_b, dd, k0, k0 + col < k_prev, c, ce)
                        return k0, c, ce, dq + part, jnp.max(c)

                    st = lax.while_loop(cond, step, (_sb_first_key(q0), c0, e_all[_sb_rows(q0), 0:1],
                                                     jnp.zeros((2 * SB_TQ, LANES), F32), jnp.max(c0)))
                    dq_acc[pl.ds(q0, SB_TQ), :] += _unstack_heads(st[3], SB_TQ)

                return carry

            lax.fori_loop(0, nq, more, 0)

        dq_ref[...] = (dq_acc[...] * SCALE).astype(BF16)
        dk_ref[...] = dk_acc[...].astype(BF16)
        dv_ref[...] = dv_acc[...].astype(BF16)

    def col_spec(j):
        return pl.BlockSpec((s, LANES), lambda p: (0, j * n_pairs + p))

    pair = pl.BlockSpec((s, LANES), lambda p: (0, p))
    (dq, dk, dv), rides = _call(
        body, name=name, grid=(n_pairs,), out_shape=(jax.ShapeDtypeStruct((s, SB_WIDTH), BF16),) * 3,
        in_specs=[col_spec(0), col_spec(1), col_spec(2), pl.BlockSpec((2 * SB_TK, SB_TK), lambda p: (0, 0)), pair, pair],
        out_specs=(pair, pair, pair),
        scratch_shapes=[pltpu.VMEM((s, LANES), F32)] * 3 + [pltpu.VMEM((2 * s, LANES), F32)] * 2,
        args=(proj, proj, proj, tri2, o, do), ride=ride)
    return dq, dk, dv, rides


def _lane_lo():
    return lax.broadcasted_iota(jnp.int32, (1, LANES), 1) < HEAD_DIM


def _swap_halves(x):
    return pltpu.roll(x, HEAD_DIM, 1)


def _rot_half(y):
    first = (lax.broadcasted_iota(jnp.int32, (1, LANES), 1) % HEAD_DIM) < (HEAD_DIM // 2)
    return jnp.where(first, pltpu.roll(y, LANES - HEAD_DIM // 2, 1), pltpu.roll(y, HEAD_DIM // 2, 1))


def _head_mean(v):
    lo = _lane_lo()
    s0 = jnp.sum(jnp.where(lo, v, 0.0), axis=1, keepdims=True)
    s1 = jnp.sum(jnp.where(lo, 0.0, v), axis=1, keepdims=True)
    return jnp.where(lo, s0, s1) * (1.0 / HEAD_DIM)


def swa_prep_fwd(proj, cos_p, sin_p, gq, gk, *, name):
    s = proj.shape[0]
    tm = min(512, s)
    q_blk = (3 * SB_WIDTH) // SWA_Q_WIDTH
    k_blk = (3 * SB_WIDTH + SWA_Q_WIDTH) // LANES

    def norm_rope(xv, g, cosv, sinv):
        y = (xv * lax.rsqrt(_head_mean(xv * xv) + NORM_EPS)) * g
        return y * cosv + _rot_half(y) * sinv

    def body(q_ref, k_ref, cos_ref, sin_ref, gq_ref, gk_ref, qn_ref, kn_ref):
        cosv, sinv = cos_ref[...], sin_ref[...]
        for j in range(SWA_Q_WIDTH // LANES):
            sl = slice(j * LANES, (j + 1) * LANES)
            qn_ref[:, sl] = norm_rope(q_ref[:, sl].astype(F32), gq_ref[...], cosv, sinv).astype(BF16)
        kn_ref[...] = norm_rope(k_ref[...].astype(F32), gk_ref[...], cosv, sinv).astype(BF16)

    row = lambda i: (i, 0)
    fixed = lambda i: (0, 0)
    return pl.pallas_call(
        body, name=name, grid=(s // tm,),
        out_shape=(jax.ShapeDtypeStruct((s, SWA_Q_WIDTH), BF16), jax.ShapeDtypeStruct((s, LANES), BF16)),
        in_specs=[pl.BlockSpec((tm, SWA_Q_WIDTH), lambda i: (i, q_blk)), pl.BlockSpec((tm, LANES), lambda i: (i, k_blk)),
                  pl.BlockSpec((tm, LANES), row), pl.BlockSpec((tm, LANES), row),
                  pl.BlockSpec((1, LANES), fixed), pl.BlockSpec((1, LANES), fixed)],
        out_specs=(pl.BlockSpec((tm, SWA_Q_WIDTH), row), pl.BlockSpec((tm, LANES), row)),
        compiler_params=_params("parallel"),
    )(proj, proj, cos_p, sin_p, gq, gk)


def swa_prep_bwd(proj, cos_p, sin_p, gq, gk, dqn, dkn, dv, *, name):
    s = proj.shape[0]
    tm = min(512, s)
    q_blk = (3 * SB_WIDTH) // SWA_Q_WIDTH
    k_blk = (3 * SB_WIDTH + SWA_Q_WIDTH) // LANES

    def bwd(xv, g, cosv, sinv, dout):
        dy = dout * cosv + _rot_half(dout * sinv)
        r = lax.rsqrt(_head_mean(xv * xv) + NORM_EPS)
        dyg = dy * g
        dx = r * dyg - xv * ((r * r * r) * _head_mean(dyg * xv))
        return dx, jnp.sum(dy * (xv * r), axis=0, keepdims=True)

    def body(q_ref, k_ref, cos_ref, sin_ref, gq_ref, gk_ref, dqn_ref, dkn_ref, dv_ref, dq_ref, dk_ref, dvb_ref,
             dgq_ref, dgk_ref):
        @pl.when(pl.program_id(0) == 0)
        def _():
            dgq_ref[...] = jnp.zeros_like(dgq_ref)
            dgk_ref[...] = jnp.zeros_like(dgk_ref)

        cosv, sinv = cos_ref[...], sin_ref[...]
        for j in range(SWA_Q_WIDTH // LANES):
            sl = slice(j * LANES, (j + 1) * LANES)
            dx, dg = bwd(q_ref[:, sl].astype(F32), gq_ref[...], cosv, sinv, dqn_ref[:, sl])
            dq_ref[:, sl] = dx.astype(BF16)
            dgq_ref[:, sl] += dg
        dx, dg = bwd(k_ref[...].astype(F32), gk_ref[...], cosv, sinv, dkn_ref[...])
        dk_ref[...] = dx.astype(BF16)
        dgk_ref[...] += dg
        dvb_ref[...] = dv_ref[...].astype(BF16)

    row = lambda i: (i, 0)
    fixed = lambda i: (0, 0)
    lane_row = pl.BlockSpec((tm, LANES), row)
    return pl.pallas_call(
        body, name=name, grid=(s // tm,),
        out_shape=(jax.ShapeDtypeStruct((s, SWA_Q_WIDTH), BF16), jax.ShapeDtypeStruct((s, LANES), BF16),
                   jax.ShapeDtypeStruct((s, LANES), BF16),
                   jax.ShapeDtypeStruct((1, SWA_Q_WIDTH), F32), jax.ShapeDtypeStruct((1, LANES), F32)),
        in_specs=[pl.BlockSpec((tm, SWA_Q_WIDTH), lambda i: (i, q_blk)), pl.BlockSpec((tm, LANES), lambda i: (i, k_blk)),
                  lane_row, lane_row, pl.BlockSpec((1, LANES), fixed), pl.BlockSpec((1, LANES), fixed),
                  pl.BlockSpec((tm, SWA_Q_WIDTH), row), lane_row, lane_row],
        out_specs=(pl.BlockSpec((tm, SWA_Q_WIDTH), row), lane_row, lane_row,
                   pl.BlockSpec((1, SWA_Q_WIDTH), fixed), pl.BlockSpec((1, LANES), fixed)),
        compiler_params=_params("arbitrary"),
    )(proj, proj, cos_p, sin_p, gq, gk, dqn, dkn, dv)


def _swa_tile(i, k_ref, v_ref, second_kv):
    q0 = pl.multiple_of(i * SWA_TQ, SWA_TQ)
    k0 = pl.multiple_of(jnp.maximum(i - 1, 0) * SWA_TQ, SWA_TQ)
    keep = jnp.logical_xor(_lane_lo(), second_kv)
    kf = k_ref[pl.ds(k0, SWA_TK), :].astype(F32)
    vf = v_ref[pl.ds(k0, SWA_TK), :].astype(F32)
    kg = jnp.where(keep, kf, _swap_halves(kf)).astype(BF16)
    vg = jnp.where(keep, vf, _swap_halves(vf)).astype(BF16)
    row = lax.broadcasted_iota(jnp.int32, (2 * SWA_TQ, SWA_TK), 0)
    tpos = q0 + jnp.where(row >= SWA_TQ, row - SWA_TQ, row)
    spos = k0 + lax.broadcasted_iota(jnp.int32, (2 * SWA_TQ, SWA_TK), 1)
    valid = jnp.logical_and(spos <= tpos, spos > tpos - WINDOW)
    return q0, k0, kg, vg, valid


def _swa_probs(qh, kg, valid, sink):
    z = jnp.where(valid, _dot_nt(qh, kg) * SCALE, NEG)
    m = jnp.maximum(jnp.max(z, axis=1, keepdims=True), sink)
    pexp = jnp.exp(z - m)
    psink = jnp.exp(sink - m)
    inv = 1.0 / (jnp.sum(pexp, axis=1, keepdims=True) + psink)
    return pexp * inv, psink * inv


def _stacked_sink(sink_row):
    s0 = jnp.sum(jnp.where(_head_mask(0), sink_row, 0.0), axis=1, keepdims=True) * (1.0 / HEAD_DIM)
    s1 = jnp.sum(jnp.where(_head_mask(1), sink_row, 0.0), axis=1, keepdims=True) * (1.0 / HEAD_DIM)
    top = lax.broadcasted_iota(jnp.int32, (2 * SWA_TQ, 1), 0) < SWA_TQ
    return jnp.where(top, s0, s1)


def swa_attn_fwd(qn, kn, proj, sink_p, *, name, ride=()):
    s = qn.shape[0]
    nq = s // SWA_TQ
    n_pairs = SWA_Q_WIDTH // LANES
    v_blk = (3 * SB_WIDTH + SWA_Q_WIDTH + SWA_KV_WIDTH) // LANES

    def body(q_ref, k_ref, v_ref, s_ref, o_ref):
        second_kv = (pl.program_id(0) // 2) == 1
        sink = _stacked_sink(s_ref[...])

        def tile(i, carry):
            q0, _, kg, vg, valid = _swa_tile(i, k_ref, v_ref, second_kv)
            probs, _ = _swa_probs(_stack_heads(q_ref[pl.ds(q0, SWA_TQ), :]), kg, valid, sink)
            o_ref[pl.ds(q0, SWA_TQ), :] = _unstack_heads(_dot(probs.astype(BF16), vg), SWA_TQ)
            return carry

        lax.fori_loop(0, nq, tile, 0, unroll=4)

    pair = pl.BlockSpec((s, LANES), lambda p: (0, p))
    whole = pl.BlockSpec((s, LANES), lambda p: (0, 0))
    (o,), rides = _call(
        body, name=name, grid=(n_pairs,), out_shape=(jax.ShapeDtypeStruct((s, SWA_Q_WIDTH), F32),),
        in_specs=[pair, whole, pl.BlockSpec((s, LANES), lambda p: (0, v_blk)),
                  pl.BlockSpec((None, 1, LANES), lambda p: (p, 0, 0))],
        out_specs=(pair,), args=(qn, kn, proj, sink_p), ride=ride)
    return o, rides


def swa_attn_bwd(qn, kn, proj, sink_p, o, do, *, name, ride=()):
    s = qn.shape[0]
    nq = s // SWA_TQ
    n_pairs = SWA_Q_WIDTH // LANES
    v_blk = (3 * SB_WIDTH + SWA_Q_WIDTH + SWA_KV_WIDTH) // LANES
    fold_rows = min(512, s)

    def body(q_ref, k_ref, v_ref, s_ref, o_ref, do_ref, dq_ref, dk_ref, dv_ref, ds_ref, acc_k, acc_v):
        p = pl.program_id(0)
        second_kv = (p // 2) == 1
        sink = _stacked_sink(s_ref[...])

        @pl.when(p % 2 == 0)
        def _():
            acc_k[...] = jnp.zeros_like(acc_k)
            acc_v[...] = jnp.zeros_like(acc_v)

        def tile(i, dsink):
            q0, k0, kg, vg, valid = _swa_tile(i, k_ref, v_ref, second_kv)
            qh = _stack_heads(q_ref[pl.ds(q0, SWA_TQ), :])
            doh = _stack_heads(do_ref[pl.ds(q0, SWA_TQ), :])
            doh_b = doh.astype(BF16)
            ov = o_ref[pl.ds(q0, SWA_TQ), :]
            delta = jnp.sum(doh * jnp.concatenate([ov, ov], axis=0), axis=1, keepdims=True)
            probs, psink = _swa_probs(qh, kg, valid, sink)
            dz = probs * (_dot_nt(doh_b, vg) - delta)
            dzb = (dz * SCALE).astype(BF16)
            dq_ref[pl.ds(q0, SWA_TQ), :] = _unstack_heads(_dot(dzb, kg), SWA_TQ)
            acc_k[pl.ds(k0, SWA_TK), :] += _dot_tn(dzb, qh)
            acc_v[pl.ds(k0, SWA_TK), :] += _dot_tn(probs.astype(BF16), doh_b)
            pd = psink * delta
            return dsink - jnp.where(_head_mask(0), jnp.sum(pd[:SWA_TQ], axis=0, keepdims=True),
                                     jnp.sum(pd[SWA_TQ:], axis=0, keepdims=True))

        ds_ref[...] = lax.fori_loop(0, nq, tile, jnp.zeros((1, LANES), F32), unroll=4)

        def fold_into(first_head):
            def fold(r, carry):
                rows = pl.ds(pl.multiple_of(r * fold_rows, fold_rows), fold_rows)
                for acc, out in ((acc_k, dk_ref), (acc_v, dv_ref)):
                    a = acc[rows, :]
                    both = a + _swap_halves(a)
                    if first_head:
                        out[rows, :] = jnp.where(_lane_lo(), both, 0.0)
                    else:
                        out[rows, :] = jnp.where(_lane_lo(), out[rows, :], both)
                return carry

            lax.fori_loop(0, s // fold_rows, fold, 0)

        @pl.when(p == 1)
        def _():
            fold_into(True)

        @pl.when(p == 3)
        def _():
            fold_into(False)

    pair = pl.BlockSpec((s, LANES), lambda p: (0, p))
    whole = pl.BlockSpec((s, LANES), lambda p: (0, 0))
    sink_spec = pl.BlockSpec((None, 1, LANES), lambda p: (p, 0, 0))
    (dq, dk, dv, dsink), rides = _call(
        body, name=name, grid=(n_pairs,),
        out_shape=(jax.ShapeDtypeStruct((s, SWA_Q_WIDTH), F32), jax.ShapeDtypeStruct((s, LANES), F32),
                   jax.ShapeDtypeStruct((s, LANES), F32), jax.ShapeDtypeStruct((n_pairs, 1, LANES), F32)),
        in_specs=[pair, whole, pl.BlockSpec((s, LANES), lambda p: (0, v_blk)), sink_spec, pair, pair],
        out_specs=(pair, whole, whole, sink_spec),
        scratch_shapes=[pltpu.VMEM((s, LANES), F32), pltpu.VMEM((s, LANES), F32)],
        args=(qn, kn, proj, sink_p, o, do), ride=ride)
    return dq, dk, dv, dsink, rides


def _rope_tables(s):
    inv_freq = 1.0 / (ROPE_THETA ** (jnp.arange(0, HEAD_DIM, 2, dtype=F32) / HEAD_DIM))
    ang = jnp.arange(s, dtype=F32)[:, None] * inv_freq[None, :]
    cos, sin = jnp.cos(ang), jnp.sin(ang)
    cos_p = jnp.tile(jnp.concatenate([cos, cos], axis=1), (1, LANES // HEAD_DIM))
    sin_p = jnp.tile(jnp.concatenate([-sin, sin], axis=1), (1, LANES // HEAD_DIM))
    return cos_p, sin_p


def _lane_tile(v, reps):
    return jnp.tile(v.reshape(1, -1), (1, reps))


def _natural(stack, w):
    n, r, c = stack.shape
    if MATRIX_NAMES[w] in ROW_SHARDED:
        return stack.reshape(n * r, c)
    if w == W_UP:
        return stack
    return jnp.transpose(stack, (1, 0, 2)).reshape(r, n * c)


def _pack_small(tree):
    flat = jnp.concatenate([tree[n].reshape(-1) for n in SMALL_NAMES])
    rows = -(-flat.shape[0] // (8 * LANES)) * 8
    return jnp.pad(flat, (0, rows * LANES - flat.shape[0])).reshape(rows, LANES)


def _unpack_small(packed, shapes):
    flat, out, off = packed.reshape(-1), {}, 0
    for n in SMALL_NAMES:
        size = shapes[n][0] * shapes[n][1]
        out[n] = flat[off:off + size].reshape(shapes[n])
        off += size
    return out


def train_step(x, target, weights, mom_m, mom_v):
    s = x.shape[0]
    cos_p, sin_p = _rope_tables(s)
    tri = (jnp.arange(SB_TK)[:, None] > jnp.arange(SB_TK)[None, :]).astype(BF16)
    tri = jnp.concatenate([tri, tri], axis=0)
    shards = [[weights[n][l].astype(BF16) for n in MATRIX_NAMES] for l in range(DEPTH)]
    core = lax.axis_index("c").astype(jnp.int32).reshape(1)
    chip = (2 * lax.axis_index("x") + lax.axis_index("y")).astype(jnp.int32).reshape(1)

    def gather(l, ws):
        return GatherJob([shards[l][w] for w in ws])

    w_in = _natural(exchange_alone(gather(0, [W_IN]), name="gather_w_in0")[0], W_IN)
    saved = []
    for l in range(DEPTH):
        g_mix = weights["mix_norm_g"][l].reshape(1, D_MODEL)
        g_mlp = weights["mlp_norm_g"][l].reshape(1, D_MODEL)
        gq = _lane_tile(weights["q_norm_g"][l], LANES // HEAD_DIM)
        gk = _lane_tile(weights["k_norm_g"][l], LANES // HEAD_DIM)
        sink_p = jnp.repeat(weights["sinks"][l].reshape(SWA_Q_WIDTH // LANES, 2), HEAD_DIM, axis=1)
        sink_p = sink_p.reshape(SWA_Q_WIDTH // LANES, 1, LANES)
        (h, proj, gates), ((s_bsb, s_bsw, s_out),) = norm_matmul(
            x, g_mix, w_in, gate_split=ATTN_WIDTH, name="in_proj", ride=[gather(l, [W_BSB, W_BSW, W_OUT])])
        o_sb, ((s_up,),) = sb_attn_fwd(proj, tri, name="sb_fwd", ride=[gather(l, [W_UP])])
        qn, kn = swa_prep_fwd(proj, cos_p, sin_p, gq, gk, name="swa_prep")
        o_sw, ((s_down,),) = swa_attn_fwd(qn, kn, proj, sink_p, name="swa_fwd", ride=[gather(l, [W_DOWN])])
        mats = [w_in, _natural(s_bsb, W_BSB), _natural(s_bsw, W_BSW), _natural(s_out, W_OUT), _natural(s_up, W_UP),
                _natural(s_down, W_DOWN)]
        x1, y_sb, y_sw, merged = merge_out_fwd(x, o_sb, o_sw, gates, mats[W_BSB], mats[W_BSW], mats[W_OUT],
                                               name="merge_out")
        if l + 1 < DEPTH:
            (h2, u), ((s_in,),) = norm_matmul(x1, g_mlp, mats[W_UP], gate_split=None, name="mlp_up",
                                              ride=[gather(l + 1, [W_IN])])
            w_in = _natural(s_in, W_IN)
        else:
            (h2, u), _ = norm_matmul(x1, g_mlp, mats[W_UP], gate_split=None, name="mlp_up_last")
        x2 = mlp_down_fwd(x1, u, mats[W_DOWN], name="mlp_down")
        saved.append(dict(x=x, h=h, proj=proj, gates=gates, o_sb=o_sb, qn=qn, kn=kn, o_sw=o_sw, y_sb=y_sb, y_sw=y_sw,
                          merged=merged, x1=x1, h2=h2, u=u, g_mix=g_mix, g_mlp=g_mlp, gq=gq, gk=gk, sink_p=sink_p,
                          mats=mats))
        x = x2

    dx, dxb, loss = loss_head(x, target, name="loss_head")

    shard_shapes = [weights[n].shape[1:] for n in MATRIX_NAMES]
    parts = [lax.empty((DEPTH, N_CHIPS) + sh, BF16) for sh in shard_shapes]
    lands = [lax.empty((DEPTH, 3) + sh, BF16) for sh in shard_shapes]
    small_grads = {n: [None] * DEPTH for n in SMALL_NAMES}
    half = D_MODEL // 2

    def summed(l, ws, grads, landed):
        new = pair_sum(l, grads, landed, [parts[w] for w in ws], core, name="grad_pair_sum")
        for w, p in zip(ws, new):
            parts[w] = p

    def chip_job(items):
        return ChipJob(items, parts, lands)

    def landed_chip(job, outs):
        for w, a in zip(job.ws, outs):
            lands[w] = a

    in_pending = None
    for l in reversed(range(DEPTH)):
        a = saved[l]
        mats = a["mats"]
        (du,), _ = mlp_bwd_up(dxb, a["u"], mats[W_DOWN], name="mlp_bwd_up")
        dw_down = matmul_tn(a["u"], [dxb], a_block=half, out_cols=None, relu2=True, name="dw_down")
        dw_up = matmul_tn(a["h2"], [du], a_block=half, out_cols=du.shape[1] // N_DEV, relu2=False, name="dw_up")
        g_mlp_w = [dw_up, dw_down.reshape((N_DEV,) + shard_shapes[W_DOWN])]
        (dx1, dx1b, dg_mlp), (landed,) = matmul_nt_norm_bwd([du], mats[W_UP], a["x1"], a["g_mlp"], dx,
                                                            name="mlp_bwd_norm", ride=[PairJob(g_mlp_w)])
        summed(l, [W_UP, W_DOWN], g_mlp_w, landed)
        small_grads["mlp_norm_g"][l] = dg_mlp.reshape(D_MODEL)
        dw_out = matmul_tn(a["merged"], [dx1b], a_block=half, out_cols=None, relu2=False, name="dw_out")
        dy_sb, dy_sw, do_sb, do_sw, dgl = out_bwd(dx1b, mats[W_OUT], a["gates"], a["y_sb"], a["y_sw"],
                                                  mats[W_BSB], mats[W_BSW], name="out_bwd")
        dw_bsb = matmul_tn(a["o_sb"], [dy_sb], a_block=half, out_cols=D_MODEL // N_DEV, relu2=False, name="dw_branch_sb")
        dw_bsw = matmul_tn(a["o_sw"], [dy_sw], a_block=half, out_cols=D_MODEL // N_DEV, relu2=False, name="dw_branch_swa")
        g_mix_w = [dw_bsb, dw_bsw, dw_out.reshape((N_DEV,) + shard_shapes[W_OUT])]
        job = chip_job([(l, W_UP), (l, W_DOWN)])
        dq_sb, dk_sb, dv_sb, (outs, landed) = sb_attn_bwd(a["proj"], tri, a["o_sb"], do_sb, name="sb_bwd",
                                                         ride=[job, PairJob(g_mix_w)])
        landed_chip(job, outs)
        summed(l, [W_BSB, W_BSW, W_OUT], g_mix_w, landed)
        job = chip_job([(l, W_BSB), (l, W_BSW), (l, W_OUT)] + ([(in_pending, W_IN)] if in_pending is not None else []))
        dqn, dkn, dv_sw, dsink, (outs,) = swa_attn_bwd(a["qn"], a["kn"], a["proj"], a["sink_p"], a["o_sw"], do_sw,
                                                      name="swa_bwd", ride=[job])
        landed_chip(job, outs)
        dq_sw, dk_sw, dv_swb, dgq, dgk = swa_prep_bwd(a["proj"], cos_p, sin_p, a["gq"], a["gk"], dqn, dkn, dv_sw,
                                                      name="swa_prep_bwd")
        small_grads["q_norm_g"][l] = dgq.reshape(SWA_Q_WIDTH // HEAD_DIM, HEAD_DIM).sum(0)
        small_grads["k_norm_g"][l] = dgk.reshape(LANES // HEAD_DIM, HEAD_DIM).sum(0)
        small_grads["sinks"][l] = dsink[:, 0, ::HEAD_DIM].reshape(SWA_Q_WIDTH // HEAD_DIM)
        pieces = [dq_sb, dk_sb, dv_sb, dq_sw, dk_sw, dv_swb, dgl]
        dw_in = matmul_tn(a["h"], pieces, a_block=half, out_cols=None, relu2=False, name="dw_in")
        g_in = [jnp.transpose(dw_in.reshape(D_MODEL, N_DEV, IN_WIDTH // N_DEV), (1, 0, 2))]
        (dx, dxb, dg_mix), (landed,) = matmul_nt_norm_bwd(pieces, mats[W_IN], a["x"], a["g_mix"], dx1,
                                                         name="in_proj_bwd", ride=[PairJob(g_in)])
        summed(l, [W_IN], g_in, landed)
        small_grads["mix_norm_g"][l] = dg_mix.reshape(D_MODEL)
        in_pending = l
    job = chip_job([(in_pending, W_IN)])
    landed_chip(job, exchange_alone(job, name="grad_chip_exchange_in0"))

    out_g, out_d, out_m, out_v = {}, {}, {}, {}
    for i, n in enumerate(MATRIX_NAMES):
        out_g[n], out_d[n], out_m[n], out_v[n] = reduce_adamw(parts[i], lands[i], chip, weights[n], mom_m[n], mom_v[n],
                                                              name="adamw_" + n)
    small_shapes = {n: weights[n].shape for n in SMALL_NAMES}
    small_all = gather_small(_pack_small({n: jnp.stack(v) for n, v in small_grads.items()}), name="gather_small_grads")
    sg, sd, sm, sv = small_adamw(small_all, _pack_small(weights), _pack_small(mom_m), _pack_small(mom_v),
                                 name="small_adamw")
    for tree, packed_small in ((out_g, sg), (out_d, sd), (out_m, sm), (out_v, sv)):
        tree.update(_unpack_small(packed_small, small_shapes))
    return loss, dx, (out_g, out_d, out_m, out_v)


def kernel(x, mix_norm_g, w_in, q_norm_g, k_norm_g, sinks, w_branch_sb, w_branch_swa, w_out, mlp_norm_g, w_up, w_down, loss_target, m_mix_norm_g, m_w_in, m_q_norm_g, m_k_norm_g, m_sinks, m_w_branch_sb, m_w_branch_swa, m_w_out, m_mlp_norm_g, m_w_up, m_w_down, v_mix_norm_g, v_w_in, v_q_norm_g, v_k_norm_g, v_sinks, v_w_branch_sb, v_w_branch_swa, v_w_out, v_mlp_norm_g, v_w_up, v_w_down):
    weights = dict(mix_norm_g=mix_norm_g, w_in=w_in, q_norm_g=q_norm_g, k_norm_g=k_norm_g, sinks=sinks,
                   w_branch_sb=w_branch_sb, w_branch_swa=w_branch_swa, w_out=w_out, mlp_norm_g=mlp_norm_g, w_up=w_up,
                   w_down=w_down)
    mom_m = dict(mix_norm_g=m_mix_norm_g, w_in=m_w_in, q_norm_g=m_q_norm_g, k_norm_g=m_k_norm_g, sinks=m_sinks,
                 w_branch_sb=m_w_branch_sb, w_branch_swa=m_w_branch_swa, w_out=m_w_out, mlp_norm_g=m_mlp_norm_g,
                 w_up=m_w_up, w_down=m_w_down)
    mom_v = dict(mix_norm_g=v_mix_norm_g, w_in=v_w_in, q_norm_g=v_q_norm_g, k_norm_g=v_k_norm_g, sinks=v_sinks,
                 w_branch_sb=v_w_branch_sb, w_branch_swa=v_w_branch_swa, w_out=v_w_out, mlp_norm_g=v_mlp_norm_g,
                 w_up=v_w_up, w_down=v_w_down)
    loss_part, grad_x, outs = train_step(x[0], loss_target[0], weights, mom_m, mom_v)
    loss = lax.psum(loss_part[0, 0], MESH_AXES)
    return (loss, grad_x[None], *[outs[0][n] for n in WEIGHT_ORDER], *[outs[1][n] for n in WEIGHT_ORDER],
            *[outs[2][n] for n in WEIGHT_ORDER], *[outs[3][n] for n in WEIGHT_ORDER])
```

```python
import functools
import math

import jax
import jax.numpy as jnp
from jax import lax
from jax.experimental import pallas as pl
from jax.experimental.pallas import tpu as pltpu

F32 = jnp.float32
BF16 = jnp.bfloat16

DEPTH = 4
D_MODEL = 1024
HEAD_DIM = 64
LANES = 128
WINDOW = 128
SB_WIDTH = 512
SWA_Q_WIDTH = 512
SWA_KV_WIDTH = 128
ATTN_WIDTH = 3 * SB_WIDTH + SWA_Q_WIDTH + 2 * SWA_KV_WIDTH
IN_WIDTH = ATTN_WIDTH + 2 * D_MODEL
ROPE_THETA = 10000.0
NORM_EPS = 1e-6
SCALE = HEAD_DIM ** -0.5
NEG = -1e30
N_DEV = 8
N_CHIPS = 4

ADAM_LR = 0.001
ADAM_B1 = 0.9
ADAM_B2 = 0.999
ADAM_EPS = 1e-08
ADAM_WD = 0.01
ADAM_STEP = 10

SB_TQ = 128
SB_TK1 = 384
SB_TK = 256
SB_CUTOFF = -88.0
SWA_TQ = 128
SWA_TK = 256
ROW_TILE = 256
VMEM_LIMIT = 56 * 1024 * 1024

MATRIX_NAMES = ("w_in", "w_branch_sb", "w_branch_swa", "w_out", "w_up", "w_down")
W_IN, W_BSB, W_BSW, W_OUT, W_UP, W_DOWN = range(6)
ROW_SHARDED = ("w_out", "w_down")
SMALL_NAMES = ("mix_norm_g", "q_norm_g", "k_norm_g", "sinks", "mlp_norm_g")
WEIGHT_ORDER = ("mix_norm_g", "w_in", "q_norm_g", "k_norm_g", "sinks", "w_branch_sb", "w_branch_swa", "w_out",
                "mlp_norm_g", "w_up", "w_down")
MESH_AXES = ("x", "y", "c")

ANY = pl.BlockSpec(memory_space=pl.ANY)
MESH = pl.DeviceIdType.MESH


def _params(*sem):
    return pltpu.CompilerParams(dimension_semantics=sem, vmem_limit_bytes=VMEM_LIMIT)


def _dot(a, b):
    return jnp.dot(a, b, preferred_element_type=F32)


def _dot_nt(a, b):
    return lax.dot_general(a, b, (((1,), (1,)), ((), ())), preferred_element_type=F32)


def _dot_tn(a, b):
    return lax.dot_general(a, b, (((0,), (0,)), ((), ())), preferred_element_type=F32)


def _split_bf16(x):
    hi = x.astype(BF16)
    lo = (x - hi.astype(F32)).astype(BF16)
    return hi, lo


def _rsqrt_ms(x):
    return lax.rsqrt(jnp.mean(x * x, axis=-1, keepdims=True) + NORM_EPS)


def _place():
    return lax.axis_index("x"), lax.axis_index("y"), lax.axis_index("c")


class _Gather:
    def __init__(self, x_refs, out_refs, send_sems, recv_sems, local_sems):
        self.x_refs, self.out_refs = x_refs, out_refs
        self.send_sems, self.recv_sems, self.local_sems = send_sems, recv_sems, local_sems
        self.n = len(x_refs)
        x, y, c = _place()
        self.c = c
        self.me, self.sibling = (x, y, c), (x, y, 1 - c)
        self.chips = [(1 - x, y), (x, 1 - y), (1 - x, 1 - y)]

    def _copy(self, k, w, blk, to, own=False):
        dst = self.out_refs[w].at[4 * blk[0] + 2 * blk[1] + blk[2]]
        return pltpu.make_async_remote_copy(
            src_ref=self.x_refs[w] if own else dst, dst_ref=dst, send_sem=self.send_sems.at[k, w],
            recv_sem=self.recv_sems.at[k, w], device_id=to, device_id_type=MESH)

    def _mine(self, w):
        me = self.me
        return pltpu.make_async_copy(self.x_refs[w], self.out_refs[w].at[4 * me[0] + 2 * me[1] + me[2]],
                                     self.local_sems.at[w])

    def _first(self, w):
        return [self._copy(0, w, self.me, self.sibling, own=True)] + [
            self._copy(1 + j, w, self.me, (*chip, self.c), own=True) for j, chip in enumerate(self.chips)]

    def _passed(self, j, w):
        return self._copy(4 + j, w, (*self.chips[j], self.c), self.sibling)

    def start(self):
        for w in range(self.n):
            self._mine(w).start()
            for cp in self._first(w):
                cp.start()

    def relay(self):
        for j, chip in enumerate(self.chips):
            for w in range(self.n):
                self._copy(1 + j, w, (*chip, self.c), self.me).wait_recv()
                self._passed(j, w).start()

    def finish(self):
        for w in range(self.n):
            self._copy(0, w, self.sibling, self.me).wait_recv()
            for j, chip in enumerate(self.chips):
                self._copy(4 + j, w, (*chip, 1 - self.c), self.me).wait_recv()
            for cp in self._first(w):
                cp.wait_send()
            for j in range(3):
                self._passed(j, w).wait_send()
            self._mine(w).wait()


class GatherJob:
    def __init__(self, shards):
        n = len(shards)
        self.inputs = list(shards)
        self.out_shapes = [jax.ShapeDtypeStruct((N_DEV,) + s.shape, s.dtype) for s in shards]
        self.aliases = {}
        self.scratch = [pltpu.SemaphoreType.DMA((7, n)), pltpu.SemaphoreType.DMA((7, n)),
                        pltpu.SemaphoreType.DMA((n,))]

    def bind(self, in_refs, out_refs, scratch_refs):
        return _Gather(in_refs, out_refs, *scratch_refs)


class _Copies:
    def __init__(self, copies):
        self.copies = copies

    def start(self):
        for cp in self.copies:
            cp.start()

    def relay(self):
        pass

    def finish(self):
        for cp in self.copies:
            cp.wait_recv()
        for cp in self.copies:
            cp.wait_send()


class ChipJob:
    def __init__(self, items, parts, lands):
        self.ws = sorted({w for _, w in items})
        n = len(self.ws)
        self.items = [(layer, self.ws.index(w)) for layer, w in items]
        self.inputs = [parts[w] for w in self.ws] + [lands[w] for w in self.ws]
        self.out_shapes = [jax.ShapeDtypeStruct(lands[w].shape, lands[w].dtype) for w in self.ws]
        self.aliases = {n + i: i for i in range(n)}
        self.scratch = [pltpu.SemaphoreType.DMA((3, n)), pltpu.SemaphoreType.DMA((3, n))]

    def bind(self, in_refs, out_refs, scratch_refs):
        send_sems, recv_sems = scratch_refs
        x, y, c = _place()
        chips = [(1 - x, y), (x, 1 - y), (1 - x, 1 - y)]
        return _Copies([pltpu.make_async_remote_copy(
            src_ref=in_refs[i].at[layer, 2 * px + py], dst_ref=out_refs[i].at[layer, j],
            send_sem=send_sems.at[j, i], recv_sem=recv_sems.at[j, i], device_id=(px, py, c), device_id_type=MESH)
            for layer, i in self.items for j, (px, py) in enumerate(chips)])


class PairJob:
    def __init__(self, grads):
        n = len(grads)
        self.inputs = list(grads)
        self.out_shapes = [jax.ShapeDtypeStruct((N_CHIPS,) + g.shape[1:], g.dtype) for g in grads]
        self.aliases = {}
        self.scratch = [pltpu.SemaphoreType.DMA((N_CHIPS, n)), pltpu.SemaphoreType.DMA((N_CHIPS, n))]

    def bind(self, in_refs, out_refs, scratch_refs):
        send_sems, recv_sems = scratch_refs
        x, y, c = _place()
        return _Copies([pltpu.make_async_remote_copy(
            src_ref=in_refs[w].at[2 * k + (1 - c)], dst_ref=out_refs[w].at[k], send_sem=send_sems.at[k, w],
            recv_sem=recv_sems.at[k, w], device_id=(x, y, 1 - c), device_id_type=MESH)
            for w in range(len(in_refs)) for k in range(N_CHIPS)])


def _call(body, *, name, grid, in_specs, out_specs, out_shape, args, scratch_shapes=(), ride=()):
    out_specs, out_shape, in_specs = tuple(out_specs), tuple(out_shape), list(in_specs)
    scratch_shapes = list(scratch_shapes)
    order = ("arbitrary",) * len(grid)
    if not ride:
        outs = pl.pallas_call(body, name=name, grid=grid, in_specs=in_specs, out_specs=out_specs, out_shape=out_shape,
                              scratch_shapes=scratch_shapes, compiler_params=_params(*order))(*args)
        return tuple(outs), []
    n_in, n_out, n_scr = len(in_specs), len(out_specs), len(scratch_shapes)
    n_steps = math.prod(grid)
    relay_at = n_steps - max(1, n_steps // 4)

    def split(refs, pos, counts):
        groups = []
        for k in counts:
            groups.append(refs[pos:pos + k])
            pos += k
        return groups, pos

    def wrapped(*refs):
        ins, pos = refs[:n_in], n_in
        job_in, pos = split(refs, pos, [len(j.inputs) for j in ride])
        outs, pos = refs[pos:pos + n_out], pos + n_out
        job_out, pos = split(refs, pos, [len(j.out_shapes) for j in ride])
        scr, pos = refs[pos:pos + n_scr], pos + n_scr
        job_scr, pos = split(refs, pos, [len(j.scratch) for j in ride])
        bound = [j.bind(i, o, s) for j, i, o, s in zip(ride, job_in, job_out, job_scr)]
        step = pl.program_id(0)
        for axis in range(1, len(grid)):
            step = step * grid[axis] + pl.program_id(axis)

        @pl.when(step == 0)
        def _():
            for b in bound:
                b.start()

        @pl.when(step == relay_at)
        def _():
            for b in bound:
                b.relay()

        body(*ins, *outs, *scr)

        @pl.when(step == n_steps - 1)
        def _():
            for b in bound:
                b.finish()

    aliases, in_pos, out_pos = {}, n_in, n_out
    for j in ride:
        aliases.update({in_pos + i: out_pos + o for i, o in j.aliases.items()})
        in_pos += len(j.inputs)
        out_pos += len(j.out_shapes)
    results = pl.pallas_call(
        wrapped, name=name, grid=grid, in_specs=in_specs + [ANY] * (in_pos - n_in),
        out_specs=out_specs + (ANY,) * (out_pos - n_out),
        out_shape=out_shape + tuple(s for j in ride for s in j.out_shapes),
        scratch_shapes=scratch_shapes + [s for j in ride for s in j.scratch], input_output_aliases=aliases,
        compiler_params=pltpu.CompilerParams(dimension_semantics=order, vmem_limit_bytes=VMEM_LIMIT,
                                             has_side_effects=True),
    )(*args, *[a for j in ride for a in j.inputs])
    job_results, pos = split(list(results), n_out, [len(j.out_shapes) for j in ride])
    return tuple(results[:n_out]), job_results


def exchange_alone(job, *, name):
    n_in, n_out = len(job.inputs), len(job.out_shapes)

    def body(*refs):
        b = job.bind(refs[:n_in], refs[n_in:n_in + n_out], refs[n_in + n_out:])
        b.start()
        b.relay()
        b.finish()

    return list(pl.pallas_call(
        body, name=name, out_shape=tuple(job.out_shapes), in_specs=[ANY] * n_in, out_specs=(ANY,) * n_out,
        scratch_shapes=job.scratch, input_output_aliases=job.aliases,
        compiler_params=pltpu.CompilerParams(has_side_effects=True),
    )(*job.inputs))


PAIR_SUM_CHUNKS = 2


def pair_sum(layer, grads, landed, parts, core, *, name):
    n = len(grads)

    def body(c_ref, *refs):
        g_refs, l_refs, o_refs = refs[:n], refs[n:2 * n], refs[3 * n:]
        for w in range(n):
            o_refs[w][...] = (g_refs[w][...].astype(F32) + l_refs[w][...].astype(F32)).astype(BF16)

    def blk(g):
        return (None, g.shape[1] // PAIR_SUM_CHUNKS, g.shape[2])

    in_specs = [pl.BlockSpec(blk(g), lambda k, i, c_ref: (2 * k + c_ref[0], i, 0)) for g in grads]
    in_specs += [pl.BlockSpec(blk(g), lambda k, i, c_ref: (k, i, 0)) for g in grads]
    in_specs += [ANY] * n
    out_specs = tuple(pl.BlockSpec((None,) + blk(g), lambda k, i, c_ref: (layer, k, i, 0)) for g in grads)
    return list(pl.pallas_call(
        body, name=name, out_shape=tuple(jax.ShapeDtypeStruct(p.shape, p.dtype) for p in parts),
        grid_spec=pltpu.PrefetchScalarGridSpec(num_scalar_prefetch=1, grid=(N_CHIPS, PAIR_SUM_CHUNKS),
                                               in_specs=in_specs, out_specs=out_specs),
        input_output_aliases={1 + 2 * n + w: w for w in range(n)},
        compiler_params=_params("parallel", "parallel"),
    )(core, *grads, *landed, *parts))


def _adamw(w, g, m, v):
    m = ADAM_B1 * m + (1.0 - ADAM_B1) * g
    v = ADAM_B2 * v + (1.0 - ADAM_B2) * (g * g)
    m_hat = m / (1.0 - ADAM_B1 ** ADAM_STEP)
    v_hat = v / (1.0 - ADAM_B2 ** ADAM_STEP)
    delta = -ADAM_LR * (m_hat / (jnp.sqrt(v_hat) + ADAM_EPS) + ADAM_WD * w)
    return delta, m, v


def reduce_adamw(part, land, chip, w, m, v, *, name):
    _, r, c = w.shape
    tr = min(r, 256)

    def body(k_ref, own_ref, l0_ref, l1_ref, l2_ref, w_ref, m_ref, v_ref, g_out, d_out, m_out, v_out):
        g = own_ref[...].astype(F32) + l0_ref[...].astype(F32) + l1_ref[...].astype(F32) + l2_ref[...].astype(F32)
        delta, m_new, v_new = _adamw(w_ref[...], g, m_ref[...], v_ref[...])
        g_out[...] = g
        d_out[...] = delta
        m_out[...] = m_new
        v_out[...] = v_new

    row = pl.BlockSpec((None, tr, c), lambda l, i, k_ref: (l, i, 0))

    def slot(j):
        return pl.BlockSpec((None, None, tr, c), lambda l, i, k_ref: (l, j, i, 0))

    return pl.pallas_call(
        body, name=name, out_shape=(jax.ShapeDtypeStruct(w.shape, F32),) * 4,
        grid_spec=pltpu.PrefetchScalarGridSpec(
            num_scalar_prefetch=1, grid=(DEPTH, r // tr),
            in_specs=[pl.BlockSpec((None, None, tr, c), lambda l, i, k_ref: (l, k_ref[0], i, 0)), slot(0), slot(1),
                      slot(2), row, row, row],
            out_specs=(row, row, row, row)),
        compiler_params=_params("parallel", "parallel"),
    )(chip, part, land, land, land, w, m, v)


def gather_small(block, *, name):
    def body(x_ref, out_ref, send_sems, recv_sems, local_sem):
        x, y, c = _place()
        me = 4 * x + 2 * y + c
        mine = pltpu.make_async_copy(x_ref, out_ref.at[me], local_sem)
        mine.start()
        peers = [(x ^ (k >> 2), y ^ ((k >> 1) & 1), c ^ (k & 1)) for k in range(1, N_DEV)]
        copies = [pltpu.make_async_remote_copy(
            src_ref=x_ref, dst_ref=out_ref.at[me], send_sem=send_sems.at[k], recv_sem=recv_sems.at[k],
            device_id=peer, device_id_type=MESH) for k, peer in enumerate(peers)]
        for cp in copies:
            cp.start()
        for k, (px, py, pc) in enumerate(peers):
            pltpu.make_async_remote_copy(
                src_ref=x_ref, dst_ref=out_ref.at[4 * px + 2 * py + pc], send_sem=send_sems.at[k],
                recv_sem=recv_sems.at[k], device_id=(px, py, pc), device_id_type=MESH).wait_recv()
        for cp in copies:
            cp.wait_send()
        mine.wait()

    return pl.pallas_call(
        body, name=name, out_shape=jax.ShapeDtypeStruct((N_DEV,) + block.shape, block.dtype),
        in_specs=[ANY], out_specs=ANY,
        scratch_shapes=[pltpu.SemaphoreType.DMA((7,)), pltpu.SemaphoreType.DMA((7,)), pltpu.SemaphoreType.DMA],
        compiler_params=pltpu.CompilerParams(has_side_effects=True),
    )(block)


def small_adamw(gathered, w, m, v, *, name):
    def body(g_ref, w_ref, m_ref, v_ref, g_out, d_out, m_out, v_out):
        g = g_ref[0]
        for d in range(1, N_DEV):
            g = g + g_ref[d]
        delta, m_new, v_new = _adamw(w_ref[...], g, m_ref[...], v_ref[...])
        g_out[...] = g
        d_out[...] = delta
        m_out[...] = m_new
        v_out[...] = v_new

    return pl.pallas_call(
        body, name=name, out_shape=(jax.ShapeDtypeStruct(w.shape, F32),) * 4,
    )(gathered, w, m, v)


def norm_matmul(x, g, w, *, gate_split, name, ride=()):
    s, d = x.shape
    tm = min(ROW_TILE, s)
    blocked = w.ndim == 3
    n = w.shape[1] if not blocked else w.shape[0] * w.shape[2]

    def body(x_ref, g_ref, w_ref, h_ref, *outs):
        xv = x_ref[...]
        h = ((xv * _rsqrt_ms(xv)) * g_ref[...]).astype(BF16)
        h_ref[...] = h
        if blocked:
            nb = w_ref.shape[2]
            for j in range(w_ref.shape[0]):
                outs[0][:, j * nb:(j + 1) * nb] = _dot(h, w_ref[j]).astype(BF16)
        else:
            p = _dot(h, w_ref[...])
            outs[0][...] = p[:, :gate_split].astype(BF16)
            outs[1][...] = (1.0 / (1.0 + jnp.exp(-p[:, gate_split:]))).astype(BF16)

    row = lambda i: (i, 0)
    fixed = lambda i: (0, 0)
    if blocked:
        out_shape = (jax.ShapeDtypeStruct((s, d), BF16), jax.ShapeDtypeStruct((s, n), BF16))
        out_specs = (pl.BlockSpec((tm, d), row), pl.BlockSpec((tm, n), row))
        w_spec = pl.BlockSpec(w.shape, lambda i: (0, 0, 0))
    else:
        out_shape = (jax.ShapeDtypeStruct((s, d), BF16), jax.ShapeDtypeStruct((s, gate_split), BF16),
                     jax.ShapeDtypeStruct((s, n - gate_split), BF16))
        out_specs = (pl.BlockSpec((tm, d), row), pl.BlockSpec((tm, gate_split), row),
                     pl.BlockSpec((tm, n - gate_split), row))
        w_spec = pl.BlockSpec((d, n), fixed)
    return _call(body, name=name, grid=(s // tm,), out_shape=out_shape, out_specs=out_specs,
                 in_specs=[pl.BlockSpec((tm, d), row), pl.BlockSpec((1, d), fixed), w_spec], args=(x, g, w), ride=ride)


def merge_out_fwd(x, o_sb, o_sw, gates, w_bsb, w_bsw, w_o, *, name):
    s, d = x.shape
    tm = min(ROW_TILE, s)

    def body(x_ref, osb_ref, osw_ref, g_ref, wsb_ref, wsw_ref, wo_ref, x1_ref, ysb_ref, ysw_ref, mg_ref):
        y_sb = _dot(osb_ref[...].astype(BF16), wsb_ref[...])
        y_sw = _dot(osw_ref[...].astype(BF16), wsw_ref[...])
        g = g_ref[...].astype(F32)
        merged = (g[:, :d] * y_sb + g[:, d:] * y_sw).astype(BF16)
        ysb_ref[...] = y_sb.astype(BF16)
        ysw_ref[...] = y_sw.astype(BF16)
        mg_ref[...] = merged
        x1_ref[...] = x_ref[...] + _dot(merged, wo_ref[...])

    row = lambda i: (i, 0)
    fixed = lambda i: (0, 0)
    wd = o_sb.shape[1]
    return pl.pallas_call(
        body, name=name, grid=(s // tm,),
        out_shape=(jax.ShapeDtypeStruct((s, d), F32),) + (jax.ShapeDtypeStruct((s, d), BF16),) * 3,
        in_specs=[pl.BlockSpec((tm, d), row), pl.BlockSpec((tm, wd), row), pl.BlockSpec((tm, wd), row),
                  pl.BlockSpec((tm, 2 * d), row), pl.BlockSpec((wd, d), fixed), pl.BlockSpec((wd, d), fixed),
                  pl.BlockSpec((d, d), fixed)],
        out_specs=(pl.BlockSpec((tm, d), row),) * 4, compiler_params=_params("parallel"),
    )(x, o_sb, o_sw, gates, w_bsb, w_bsw, w_o)


def mlp_down_fwd(x1, u, w_down, *, name):
    s, d = x1.shape
    f = u.shape[1]
    tm = min(ROW_TILE, s)

    def body(x_ref, u_ref, w_ref, o_ref):
        a = jnp.maximum(u_ref[...].astype(F32), 0.0)
        o_ref[...] = x_ref[...] + _dot((a * a).astype(BF16), w_ref[...])

    row = lambda i: (i, 0)
    return pl.pallas_call(
        body, name=name, grid=(s // tm,), out_shape=jax.ShapeDtypeStruct((s, d), F32),
        in_specs=[pl.BlockSpec((tm, d), row), pl.BlockSpec((tm, f), row), pl.BlockSpec((f, d), lambda i: (0, 0))],
        out_specs=pl.BlockSpec((tm, d), row), compiler_params=_params("parallel"),
    )(x1, u, w_down)


def loss_head(y, target, *, name):
    s, d = y.shape
    tm = min(ROW_TILE, s)

    def body(y_ref, t_ref, dy_ref, dyb_ref, loss_ref):
        @pl.when(pl.program_id(0) == 0)
        def _():
            loss_ref[...] = jnp.zeros_like(loss_ref)

        e = y_ref[...] - t_ref[...]
        dy = e * (1.0 / d)
        dy_ref[...] = dy
        dyb_ref[...] = dy.astype(BF16)
        per_row = jnp.sum(e * e, axis=1, keepdims=True) * (0.5 / d)
        loss_ref[...] += jnp.sum(per_row, axis=0, keepdims=True)

    row = lambda i: (i, 0)
    return pl.pallas_call(
        body, name=name, grid=(s // tm,),
        out_shape=(jax.ShapeDtypeStruct((s, d), F32), jax.ShapeDtypeStruct((s, d), BF16),
                   jax.ShapeDtypeStruct((1, 1), F32)),
        in_specs=[pl.BlockSpec((tm, d), row), pl.BlockSpec((tm, d), row)],
        out_specs=(pl.BlockSpec((tm, d), row), pl.BlockSpec((tm, d), row), pl.BlockSpec((1, 1), lambda i: (0, 0))),
        compiler_params=_params("arbitrary"),
    )(y, target)


def mlp_bwd_up(dxb, u, w_down, *, name, ride=()):
    s, d = dxb.shape
    f = u.shape[1]
    tm = min(ROW_TILE, s)

    def body(dx_ref, u_ref, w_ref, du_ref):
        da = _dot_nt(dx_ref[...], w_ref[...])
        du_ref[...] = (da * (2.0 * jnp.maximum(u_ref[...].astype(F32), 0.0))).astype(BF16)

    row = lambda i: (i, 0)
    return _call(body, name=name, grid=(s // tm,), out_shape=(jax.ShapeDtypeStruct((s, f), BF16),),
                 in_specs=[pl.BlockSpec((tm, d), row), pl.BlockSpec((tm, f), row),
                           pl.BlockSpec((f, d), lambda i: (0, 0))],
                 out_specs=(pl.BlockSpec((tm, f), row),), args=(dxb, u, w_down), ride=ride)


def matmul_nt_norm_bwd(pieces, w, x, g, dres, *, name, ride=()):
    s = x.shape[0]
    d = x.shape[1]
    tm = min(ROW_TILE, s)
    blocked = w.ndim == 3
    n_pieces = len(pieces)
    widths = [p.shape[1] for p in pieces]

    def body(*refs):
        p_refs = refs[:n_pieces]
        w_ref, x_ref, g_ref, dres_ref, dx_ref, dxb_ref, dg_ref = refs[n_pieces:]

        @pl.when(pl.program_id(0) == 0)
        def _():
            dg_ref[...] = jnp.zeros_like(dg_ref)

        if blocked:
            nb = w_ref.shape[2]
            dh = _dot_nt(p_refs[0][:, :nb], w_ref[0])
            for j in range(1, w_ref.shape[0]):
                dh = dh + _dot_nt(p_refs[0][:, j * nb:(j + 1) * nb], w_ref[j])
        else:
            dh, off = None, 0
            for p_ref, width in zip(p_refs, widths):
                part = _dot_nt(p_ref[...], w_ref[:, off:off + width])
                dh = part if dh is None else dh + part
                off += width
        xv = x_ref[...]
        r = _rsqrt_ms(xv)
        dyg = dh * g_ref[...]
        dx = dres_ref[...] + r * dyg - xv * ((r * r * r) * jnp.mean(dyg * xv, axis=-1, keepdims=True))
        dx_ref[...] = dx
        dxb_ref[...] = dx.astype(BF16)
        dg_ref[...] += jnp.sum(dh * (xv * r), axis=0, keepdims=True)

    row = lambda i: (i, 0)
    fixed = lambda i: (0, 0)
    w_spec = pl.BlockSpec(w.shape, (lambda i: (0, 0, 0)) if blocked else fixed)
    return _call(
        body, name=name, grid=(s // tm,),
        out_shape=(jax.ShapeDtypeStruct((s, d), F32), jax.ShapeDtypeStruct((s, d), BF16),
                   jax.ShapeDtypeStruct((1, d), F32)),
        in_specs=[pl.BlockSpec((tm, width), row) for width in widths] + [
            w_spec, pl.BlockSpec((tm, d), row), pl.BlockSpec((1, d), fixed), pl.BlockSpec((tm, d), row)],
        out_specs=(pl.BlockSpec((tm, d), row), pl.BlockSpec((tm, d), row), pl.BlockSpec((1, d), fixed)),
        args=(*pieces, w, x, g, dres), ride=ride)


def out_bwd(dx1b, w_o, gates, y_sb, y_sw, w_bsb, w_bsw, *, name):
    s, d = dx1b.shape
    wd = w_bsb.shape[0]
    tm = min(ROW_TILE, s)

    def body(dx_ref, wo_ref, g_ref, ysb_ref, ysw_ref, wsb_ref, wsw_ref, dysb_ref, dysw_ref, dosb_ref, dosw_ref, dgl_ref):
        dm = _dot_nt(dx_ref[...], wo_ref[...])
        g = g_ref[...].astype(F32)
        g0, g1 = g[:, :d], g[:, d:]
        dy_sb = (dm * g0).astype(BF16)
        dy_sw = (dm * g1).astype(BF16)
        dysb_ref[...] = dy_sb
        dysw_ref[...] = dy_sw
        dosb_ref[...] = _dot_nt(dy_sb, wsb_ref[...])
        dosw_ref[...] = _dot_nt(dy_sw, wsw_ref[...])
        dgl_ref[:, :d] = (dm * ysb_ref[...].astype(F32) * (g0 * (1.0 - g0))).astype(BF16)
        dgl_ref[:, d:] = (dm * ysw_ref[...].astype(F32) * (g1 * (1.0 - g1))).astype(BF16)

    row = lambda i: (i, 0)
    fixed = lambda i: (0, 0)
    return pl.pallas_call(
        body, name=name, grid=(s // tm,),
        out_shape=(jax.ShapeDtypeStruct((s, d), BF16), jax.ShapeDtypeStruct((s, d), BF16),
                   jax.ShapeDtypeStruct((s, wd), F32), jax.ShapeDtypeStruct((s, wd), F32),
                   jax.ShapeDtypeStruct((s, 2 * d), BF16)),
        in_specs=[pl.BlockSpec((tm, d), row), pl.BlockSpec((d, d), fixed), pl.BlockSpec((tm, 2 * d), row),
                  pl.BlockSpec((tm, d), row), pl.BlockSpec((tm, d), row), pl.BlockSpec((wd, d), fixed),
                  pl.BlockSpec((wd, d), fixed)],
        out_specs=(pl.BlockSpec((tm, d), row), pl.BlockSpec((tm, d), row), pl.BlockSpec((tm, wd), row),
                   pl.BlockSpec((tm, wd), row), pl.BlockSpec((tm, 2 * d), row)),
        compiler_params=_params("parallel"),
    )(dx1b, w_o, gates, y_sb, y_sw, w_bsb, w_bsw)


def matmul_tn(a, pieces, *, a_block, out_cols, relu2, name):
    s, m = a.shape
    widths = [p.shape[1] for p in pieces]
    n = sum(widths)
    n_pieces = len(pieces)
    ts = min(512 if n >= 4096 else 2048, s)
    n_steps = s // ts
    if out_cols is None:
        out_shape = jax.ShapeDtypeStruct((m // a_block, a_block, n), BF16)
        out_spec = pl.BlockSpec((None, a_block, n), lambda i, k: (i, 0, 0))
    else:
        out_shape = jax.ShapeDtypeStruct((n // out_cols, m, out_cols), BF16)
        out_spec = pl.BlockSpec((n // out_cols, a_block, out_cols), lambda i, k: (0, i, 0))

    def body(a_ref, *refs):
        b_refs, o_ref, acc = refs[:n_pieces], refs[n_pieces], refs[n_pieces + 1]
        k = pl.program_id(1)

        @pl.when(k == 0)
        def _():
            acc[...] = jnp.zeros_like(acc)

        av = a_ref[...]
        if relu2:
            af = jnp.maximum(av.astype(F32), 0.0)
            av = af * af
        av = av.astype(BF16)
        off = 0
        for b_ref in b_refs:
            width = b_ref.shape[1]
            acc[:, off:off + width] += _dot_tn(av, b_ref[...].astype(BF16))
            off += width

        @pl.when(k == n_steps - 1)
        def _():
            if out_cols is None:
                o_ref[...] = acc[...].astype(BF16)
            else:
                for j in range(n // out_cols):
                    o_ref[j] = acc[:, j * out_cols:(j + 1) * out_cols].astype(BF16)

    return pl.pallas_call(
        body, name=name, grid=(m // a_block, n_steps), out_shape=out_shape,
        in_specs=[pl.BlockSpec((ts, a_block), lambda i, k: (k, i))] + [
            pl.BlockSpec((ts, width), lambda i, k: (k, 0)) for width in widths],
        out_specs=out_spec, scratch_shapes=[pltpu.VMEM((a_block, n), F32)],
        compiler_params=_params("parallel", "arbitrary"),
    )(a, *pieces)


def _softplus(z):
    return jnp.maximum(z, 0.0) + jnp.log(1.0 + jnp.exp(-jnp.abs(z)))


def _suffix_sums(x, tri2):
    hi, lo = _split_bf16(x)
    return _dot(jnp.concatenate([hi, lo], axis=1), tri2)


def _head_mask(h):
    return (lax.broadcasted_iota(jnp.int32, (1, LANES), 1) // HEAD_DIM) == h


def _stack_heads(x):
    zero = jnp.zeros_like(x)
    return jnp.concatenate([jnp.where(_head_mask(0), x, zero), jnp.where(_head_mask(1), x, zero)], axis=0)


def _unstack_heads(r, t):
    return jnp.where(_head_mask(0), r[:t], r[t:])


def _sb_positions(q0, tk):
    row = lax.broadcasted_iota(jnp.int32, (2 * SB_TQ, tk), 0)
    col = lax.broadcasted_iota(jnp.int32, (2 * SB_TQ, tk), 1)
    return q0 + jnp.where(row >= SB_TQ, row - SB_TQ, row), col


def _sb_first_key(q0):
    return pl.multiple_of(jnp.maximum(q0 + SB_TQ - SB_TK1, 0), SB_TQ)


def _sb_tri(tri_ref, tk):
    if tk == SB_TK1:
        return tri_ref[...]
    return jnp.concatenate([tri_ref[0:tk, 0:tk], tri_ref[SB_TK1:SB_TK1 + tk, 0:tk]], axis=0)


def _sb_next_key(k_prev):
    return pl.multiple_of(jnp.maximum(k_prev - SB_TK, 0), SB_TQ)


def _sb_rows(q0):
    return pl.ds(pl.multiple_of(2 * q0, 2 * SB_TQ), 2 * SB_TQ)


def sb_attn_fwd(proj, tri2, *, name, ride=()):
    s = proj.shape[0]
    nq = s // SB_TQ
    n_pairs = SB_WIDTH // LANES

    def body(q_ref, k_ref, v_ref, tri_ref, o_ref, c_all):
        def block(qh, k0, tk, live, c):
            z = _dot_nt(qh, k_ref[pl.ds(k0, tk), :])
            sp = _softplus(z)
            lk = jnp.where(live, -sp, 0.0)
            w = jnp.where(live, jnp.exp(z - sp + _suffix_sums(lk, _sb_tri(tri_ref, tk)) + c), 0.0)
            return _dot(w.astype(BF16), v_ref[pl.ds(k0, tk), :]), c + jnp.sum(lk, axis=1, keepdims=True)

        def load_q(q0):
            return _stack_heads(q_ref[pl.ds(q0, SB_TQ), :]) * SCALE

        def first(qb, carry):
            q0 = pl.multiple_of(qb * SB_TQ, SB_TQ)
            tpos, col = _sb_positions(q0, SB_TK1)
            k0 = _sb_first_key(q0)
            acc, c = block(load_q(q0), k0, SB_TK1, k0 + col < tpos, jnp.zeros((2 * SB_TQ, 1), F32))
            o_ref[pl.ds(q0, SB_TQ), :] = _unstack_heads(acc, SB_TQ)
            c_all[_sb_rows(q0), :] = jnp.broadcast_to(jnp.where(k0 > 0, c, NEG), (2 * SB_TQ, LANES))
            return carry

        lax.fori_loop(0, nq, first, 0, unroll=2)

        @pl.when(jnp.max(c_all[...]) > SB_CUTOFF)
        def _():
            def more(qb, carry):
                q0 = pl.multiple_of(qb * SB_TQ, SB_TQ)
                c0 = c_all[_sb_rows(q0), 0:1]

                @pl.when(jnp.max(c0) > SB_CUTOFF)
                def _():
                    qh = load_q(q0)
                    _, col = _sb_positions(q0, SB_TK)

                    def cond(st):
                        return jnp.logical_and(st[0] > 0, st[3] > SB_CUTOFF)

                    def step(st):
                        k_prev, c, acc, _ = st
                        k0 = _sb_next_key(k_prev)
                        part, c = block(qh, k0, SB_TK, k0 + col < k_prev, c)
                        return k0, c, acc + part, jnp.max(c)

                    st = lax.while_loop(cond, step, (_sb_first_key(q0), c0, jnp.zeros((2 * SB_TQ, LANES), F32),
                                                     jnp.max(c0)))
                    o_ref[pl.ds(q0, SB_TQ), :] += _unstack_heads(st[2], SB_TQ)

                return carry

            lax.fori_loop(0, nq, more, 0)

    def col_spec(j):
        return pl.BlockSpec((s, LANES), lambda p: (0, j * n_pairs + p))

    (o,), rides = _call(
        body, name=name, grid=(n_pairs,), out_shape=(jax.ShapeDtypeStruct((s, SB_WIDTH), F32),),
        in_specs=[col_spec(0), col_spec(1), col_spec(2), pl.BlockSpec((2 * SB_TK1, SB_TK1), lambda p: (0, 0))],
        out_specs=(pl.BlockSpec((s, LANES), lambda p: (0, p)),), scratch_shapes=[pltpu.VMEM((2 * s, LANES), F32)],
        args=(proj, proj, proj, tri2), ride=ride)
    return o, rides


def sb_attn_bwd(proj, tri2, o, do, *, name, ride=()):
    s = proj.shape[0]
    nq = s // SB_TQ
    n_pairs = SB_WIDTH // LANES

    def body(q_ref, k_ref, v_ref, tri_ref, o_ref, do_ref, dq_ref, dk_ref, dv_ref, dq_acc, dk_acc, dv_acc, c_all, e_all):
        dk_acc[...] = jnp.zeros_like(dk_acc)
        dv_acc[...] = jnp.zeros_like(dv_acc)

        def load(q0):
            qh = _stack_heads(q_ref[pl.ds(q0, SB_TQ), :]) * SCALE
            doh_b = _stack_heads(do_ref[pl.ds(q0, SB_TQ), :].astype(BF16))
            ov = o_ref[pl.ds(q0, SB_TQ), :]
            dd = jnp.sum(doh_b.astype(F32) * jnp.concatenate([ov, ov], axis=0), axis=1, keepdims=True)
            return qh, doh_b, dd

        def block(qh, doh_b, dd, k0, tk, live, c, ce):
            kt = k_ref[pl.ds(k0, tk), :]
            tri = _sb_tri(tri_ref, tk)
            z = _dot_nt(qh, kt)
            sp = _softplus(z)
            lb = z - sp
            lk = jnp.where(live, -sp, 0.0)
            wb = jnp.where(live, jnp.exp(lb + _suffix_sums(lk, tri) + c), 0.0).astype(BF16)
            e = wb.astype(F32) * _dot_nt(doh_b, v_ref[pl.ds(k0, tk), :])
            dz = jnp.where(live, e - jnp.exp(lb) * (dd - ce - _suffix_sums(e, tri)), 0.0)
            dzb = dz.astype(BF16)
            dk_acc[pl.ds(k0, tk), :] += _dot_tn(dzb, qh)
            dv_acc[pl.ds(k0, tk), :] += _dot_tn(wb, doh_b)
            return (_dot(dzb, kt), c + jnp.sum(lk, axis=1, keepdims=True), ce + jnp.sum(e, axis=1, keepdims=True))

        def first(qb, carry):
            q0 = pl.multiple_of(qb * SB_TQ, SB_TQ)
            qh, doh_b, dd = load(q0)
            tpos, col = _sb_positions(q0, SB_TK1)
            k0 = _sb_first_key(q0)
            zero = jnp.zeros((2 * SB_TQ, 1), F32)
            dq, c, ce = block(qh, doh_b, dd, k0, SB_TK1, k0 + col < tpos, zero, zero)
            dq_acc[pl.ds(q0, SB_TQ), :] = _unstack_heads(dq, SB_TQ)
            c_all[_sb_rows(q0), :] = jnp.broadcast_to(jnp.where(k0 > 0, c, NEG), (2 * SB_TQ, LANES))
            e_all[_sb_rows(q0), :] = jnp.broadcast_to(ce, (2 * SB_TQ, LANES))
            return carry

        lax.fori_loop(0, nq, first, 0, unroll=2)

        @pl.when(jnp.max(c_all[...]) > SB_CUTOFF)
        def _():
            def more(qb, carry):
                q0 = pl.multiple_of(qb * SB_TQ, SB_TQ)
                c0 = c_all[_sb_rows(q0), 0:1]

                @pl.when(jnp.max(c0) > SB_CUTOFF)
                def _():
                    qh, doh_b, dd = load(q0)
                    _, col = _sb_positions(q0, SB_TK)

                    def cond(st):
                        return jnp.logical_and(st[0] > 0, st[4] > SB_CUTOFF)

                    def step(st):
                        k_prev, c, ce, dq, _ = st
                        k0 = _sb_next_key(k_prev)
                        part, c, ce = block(qh, doh_b, dd, k0, SB_TK, k0 + col < k_prev, c, ce)
                        return k0, c, ce, dq + part, jnp.max(c)

                    st = lax.while_loop(cond, step, (_sb_first_key(q0), c0, e_all[_sb_rows(q0), 0:1],
                                                     jnp.zeros((2 * SB_TQ, LANES), F32), jnp.max(c0)))
                    dq_acc[pl.ds(q0, SB_TQ), :] += _unstack_heads(st[3], SB_TQ)

                return carry

            lax.fori_loop(0, nq, more, 0)

        dq_ref[...] = (dq_acc[...] * SCALE).astype(BF16)
        dk_ref[...] = dk_acc[...].astype(BF16)
        dv_ref[...] = dv_acc[...].astype(BF16)

    def col_spec(j):
        return pl.BlockSpec((s, LANES), lambda p: (0, j * n_pairs + p))

    pair = pl.BlockSpec((s, LANES), lambda p: (0, p))
    (dq, dk, dv), rides = _call(
        body, name=name, grid=(n_pairs,), out_shape=(jax.ShapeDtypeStruct((s, SB_WIDTH), BF16),) * 3,
        in_specs=[col_spec(0), col_spec(1), col_spec(2), pl.BlockSpec((2 * SB_TK1, SB_TK1), lambda p: (0, 0)), pair, pair],
        out_specs=(pair, pair, pair),
        scratch_shapes=[pltpu.VMEM((s, LANES), F32)] * 3 + [pltpu.VMEM((2 * s, LANES), F32)] * 2,
        args=(proj, proj, proj, tri2, o, do), ride=ride)
    return dq, dk, dv, rides


def _lane_lo():
    return lax.broadcasted_iota(jnp.int32, (1, LANES), 1) < HEAD_DIM


def _swap_halves(x):
    return pltpu.roll(x, HEAD_DIM, 1)


def _rot_half(y):
    first = (lax.broadcasted_iota(jnp.int32, (1, LANES), 1) % HEAD_DIM) < (HEAD_DIM // 2)
    return jnp.where(first, pltpu.roll(y, LANES - HEAD_DIM // 2, 1), pltpu.roll(y, HEAD_DIM // 2, 1))


def _head_mean(v):
    lo = _lane_lo()
    s0 = jnp.sum(jnp.where(lo, v, 0.0), axis=1, keepdims=True)
    s1 = jnp.sum(jnp.where(lo, 0.0, v), axis=1, keepdims=True)
    return jnp.where(lo, s0, s1) * (1.0 / HEAD_DIM)


def swa_prep_fwd(proj, cos_p, sin_p, gq, gk, *, name):
    s = proj.shape[0]
    tm = min(512, s)
    q_blk = (3 * SB_WIDTH) // SWA_Q_WIDTH
    k_blk = (3 * SB_WIDTH + SWA_Q_WIDTH) // LANES

    def norm_rope(xv, g, cosv, sinv):
        y = (xv * lax.rsqrt(_head_mean(xv * xv) + NORM_EPS)) * g
        return y * cosv + _rot_half(y) * sinv

    def body(q_ref, k_ref, cos_ref, sin_ref, gq_ref, gk_ref, qn_ref, kn_ref):
        cosv, sinv = cos_ref[...], sin_ref[...]
        for j in range(SWA_Q_WIDTH // LANES):
            sl = slice(j * LANES, (j + 1) * LANES)
            qn_ref[:, sl] = norm_rope(q_ref[:, sl].astype(F32), gq_ref[...], cosv, sinv).astype(BF16)
        kn_ref[...] = norm_rope(k_ref[...].astype(F32), gk_ref[...], cosv, sinv).astype(BF16)

    row = lambda i: (i, 0)
    fixed = lambda i: (0, 0)
    return pl.pallas_call(
        body, name=name, grid=(s // tm,),
        out_shape=(jax.ShapeDtypeStruct((s, SWA_Q_WIDTH), BF16), jax.ShapeDtypeStruct((s, LANES), BF16)),
        in_specs=[pl.BlockSpec((tm, SWA_Q_WIDTH), lambda i: (i, q_blk)), pl.BlockSpec((tm, LANES), lambda i: (i, k_blk)),
                  pl.BlockSpec((tm, LANES), row), pl.BlockSpec((tm, LANES), row),
                  pl.BlockSpec((1, LANES), fixed), pl.BlockSpec((1, LANES), fixed)],
        out_specs=(pl.BlockSpec((tm, SWA_Q_WIDTH), row), pl.BlockSpec((tm, LANES), row)),
        compiler_params=_params("parallel"),
    )(proj, proj, cos_p, sin_p, gq, gk)


def swa_prep_bwd(proj, cos_p, sin_p, gq, gk, dqn, dkn, dv, *, name):
    s = proj.shape[0]
    tm = min(512, s)
    q_blk = (3 * SB_WIDTH) // SWA_Q_WIDTH
    k_blk = (3 * SB_WIDTH + SWA_Q_WIDTH) // LANES

    def bwd(xv, g, cosv, sinv, dout):
        dy = dout * cosv + _rot_half(dout * sinv)
        r = lax.rsqrt(_head_mean(xv * xv) + NORM_EPS)
        dyg = dy * g
        dx = r * dyg - xv * ((r * r * r) * _head_mean(dyg * xv))
        return dx, jnp.sum(dy * (xv * r), axis=0, keepdims=True)

    def body(q_ref, k_ref, cos_ref, sin_ref, gq_ref, gk_ref, dqn_ref, dkn_ref, dv_ref, dq_ref, dk_ref, dvb_ref,
             dgq_ref, dgk_ref):
        @pl.when(pl.program_id(0) == 0)
        def _():
            dgq_ref[...] = jnp.zeros_like(dgq_ref)
            dgk_ref[...] = jnp.zeros_like(dgk_ref)

        cosv, sinv = cos_ref[...], sin_ref[...]
        for j in range(SWA_Q_WIDTH // LANES):
            sl = slice(j * LANES, (j + 1) * LANES)
            dx, dg = bwd(q_ref[:, sl].astype(F32), gq_ref[...], cosv, sinv, dqn_ref[:, sl])
            dq_ref[:, sl] = dx.astype(BF16)
            dgq_ref[:, sl] += dg
        dx, dg = bwd(k_ref[...].astype(F32), gk_ref[...], cosv, sinv, dkn_ref[...])
        dk_ref[...] = dx.astype(BF16)
        dgk_ref[...] += dg
        dvb_ref[...] = dv_ref[...].astype(BF16)

    row = lambda i: (i, 0)
    fixed = lambda i: (0, 0)
    lane_row = pl.BlockSpec((tm, LANES), row)
    return pl.pallas_call(
        body, name=name, grid=(s // tm,),
        out_shape=(jax.ShapeDtypeStruct((s, SWA_Q_WIDTH), BF16), jax.ShapeDtypeStruct((s, LANES), BF16),
                   jax.ShapeDtypeStruct((s, LANES), BF16),
                   jax.ShapeDtypeStruct((1, SWA_Q_WIDTH), F32), jax.ShapeDtypeStruct((1, LANES), F32)),
        in_specs=[pl.BlockSpec((tm, SWA_Q_WIDTH), lambda i: (i, q_blk)), pl.BlockSpec((tm, LANES), lambda i: (i, k_blk)),
                  lane_row, lane_row, pl.BlockSpec((1, LANES), fixed), pl.BlockSpec((1, LANES), fixed),
                  pl.BlockSpec((tm, SWA_Q_WIDTH), row), lane_row, lane_row],
        out_specs=(pl.BlockSpec((tm, SWA_Q_WIDTH), row), lane_row, lane_row,
                   pl.BlockSpec((1, SWA_Q_WIDTH), fixed), pl.BlockSpec((1, LANES), fixed)),
        compiler_params=_params("arbitrary"),
    )(proj, proj, cos_p, sin_p, gq, gk, dqn, dkn, dv)


def _swa_tile(i, k_ref, v_ref, second_kv):
    q0 = pl.multiple_of(i * SWA_TQ, SWA_TQ)
    k0 = pl.multiple_of(jnp.maximum(i - 1, 0) * SWA_TQ, SWA_TQ)
    keep = jnp.logical_xor(_lane_lo(), second_kv)
    kf = k_ref[pl.ds(k0, SWA_TK), :].astype(F32)
    vf = v_ref[pl.ds(k0, SWA_TK), :].astype(F32)
    kg = jnp.where(keep, kf, _swap_halves(kf)).astype(BF16)
    vg = jnp.where(keep, vf, _swap_halves(vf)).astype(BF16)
    row = lax.broadcasted_iota(jnp.int32, (2 * SWA_TQ, SWA_TK), 0)
    tpos = q0 + jnp.where(row >= SWA_TQ, row - SWA_TQ, row)
    spos = k0 + lax.broadcasted_iota(jnp.int32, (2 * SWA_TQ, SWA_TK), 1)
    valid = jnp.logical_and(spos <= tpos, spos > tpos - WINDOW)
    return q0, k0, kg, vg, valid


def _swa_probs(qh, kg, valid, sink):
    z = jnp.where(valid, _dot_nt(qh, kg) * SCALE, NEG)
    m = jnp.maximum(jnp.max(z, axis=1, keepdims=True), sink)
    pexp = jnp.exp(z - m)
    psink = jnp.exp(sink - m)
    inv = 1.0 / (jnp.sum(pexp, axis=1, keepdims=True) + psink)
    return pexp * inv, psink * inv


def _stacked_sink(sink_row):
    s0 = jnp.sum(jnp.where(_head_mask(0), sink_row, 0.0), axis=1, keepdims=True) * (1.0 / HEAD_DIM)
    s1 = jnp.sum(jnp.where(_head_mask(1), sink_row, 0.0), axis=1, keepdims=True) * (1.0 / HEAD_DIM)
    top = lax.broadcasted_iota(jnp.int32, (2 * SWA_TQ, 1), 0) < SWA_TQ
    return jnp.where(top, s0, s1)


def swa_attn_fwd(qn, kn, proj, sink_p, *, name, ride=()):
    s = qn.shape[0]
    nq = s // SWA_TQ
    n_pairs = SWA_Q_WIDTH // LANES
    v_blk = (3 * SB_WIDTH + SWA_Q_WIDTH + SWA_KV_WIDTH) // LANES

    def body(q_ref, k_ref, v_ref, s_ref, o_ref):
        second_kv = (pl.program_id(0) // 2) == 1
        sink = _stacked_sink(s_ref[...])

        def tile(i, carry):
            q0, _, kg, vg, valid = _swa_tile(i, k_ref, v_ref, second_kv)
            probs, _ = _swa_probs(_stack_heads(q_ref[pl.ds(q0, SWA_TQ), :]), kg, valid, sink)
            o_ref[pl.ds(q0, SWA_TQ), :] = _unstack_heads(_dot(probs.astype(BF16), vg), SWA_TQ)
            return carry

        lax.fori_loop(0, nq, tile, 0, unroll=4)

    pair = pl.BlockSpec((s, LANES), lambda p: (0, p))
    whole = pl.BlockSpec((s, LANES), lambda p: (0, 0))
    (o,), rides = _call(
        body, name=name, grid=(n_pairs,), out_shape=(jax.ShapeDtypeStruct((s, SWA_Q_WIDTH), F32),),
        in_specs=[pair, whole, pl.BlockSpec((s, LANES), lambda p: (0, v_blk)),
                  pl.BlockSpec((None, 1, LANES), lambda p: (p, 0, 0))],
        out_specs=(pair,), args=(qn, kn, proj, sink_p), ride=ride)
    return o, rides


def swa_attn_bwd(qn, kn, proj, sink_p, o, do, *, name, ride=()):
    s = qn.shape[0]
    nq = s // SWA_TQ
    n_pairs = SWA_Q_WIDTH // LANES
    v_blk = (3 * SB_WIDTH + SWA_Q_WIDTH + SWA_KV_WIDTH) // LANES
    fold_rows = min(512, s)

    def body(q_ref, k_ref, v_ref, s_ref, o_ref, do_ref, dq_ref, dk_ref, dv_ref, ds_ref, acc_k, acc_v):
        p = pl.program_id(0)
        second_kv = (p // 2) == 1
        sink = _stacked_sink(s_ref[...])

        @pl.when(p % 2 == 0)
        def _():
            acc_k[...] = jnp.zeros_like(acc_k)
            acc_v[...] = jnp.zeros_like(acc_v)

        def tile(i, dsink):
            q0, k0, kg, vg, valid = _swa_tile(i, k_ref, v_ref, second_kv)
            qh = _stack_heads(q_ref[pl.ds(q0, SWA_TQ), :])
            doh = _stack_heads(do_ref[pl.ds(q0, SWA_TQ), :])
            doh_b = doh.astype(BF16)
            ov = o_ref[pl.ds(q0, SWA_TQ), :]
            delta = jnp.sum(doh * jnp.concatenate([ov, ov], axis=0), axis=1, keepdims=True)
            probs, psink = _swa_probs(qh, kg, valid, sink)
            dz = probs * (_dot_nt(doh_b, vg) - delta)
            dzb = (dz * SCALE).astype(BF16)
            dq_ref[pl.ds(q0, SWA_TQ), :] = _unstack_heads(_dot(dzb, kg), SWA_TQ)
            acc_k[pl.ds(k0, SWA_TK), :] += _dot_tn(dzb, qh)
            acc_v[pl.ds(k0, SWA_TK), :] += _dot_tn(probs.astype(BF16), doh_b)
            pd = psink * delta
            return dsink - jnp.where(_head_mask(0), jnp.sum(pd[:SWA_TQ], axis=0, keepdims=True),
                                     jnp.sum(pd[SWA_TQ:], axis=0, keepdims=True))

        ds_ref[...] = lax.fori_loop(0, nq, tile, jnp.zeros((1, LANES), F32), unroll=4)

        def fold_into(first_head):
            def fold(r, carry):
                rows = pl.ds(pl.multiple_of(r * fold_rows, fold_rows), fold_rows)
                for acc, out in ((acc_k, dk_ref), (acc_v, dv_ref)):
                    a = acc[rows, :]
                    both = a + _swap_halves(a)
                    if first_head:
                        out[rows, :] = jnp.where(_lane_lo(), both, 0.0)
                    else:
                        out[rows, :] = jnp.where(_lane_lo(), out[rows, :], both)
                return carry

            lax.fori_loop(0, s // fold_rows, fold, 0)

        @pl.when(p == 1)
        def _():
            fold_into(True)

        @pl.when(p == 3)
        def _():
            fold_into(False)

    pair = pl.BlockSpec((s, LANES), lambda p: (0, p))
    whole = pl.BlockSpec((s, LANES), lambda p: (0, 0))
    sink_spec = pl.BlockSpec((None, 1, LANES), lambda p: (p, 0, 0))
    (dq, dk, dv, dsink), rides = _call(
        body, name=name, grid=(n_pairs,),
        out_shape=(jax.ShapeDtypeStruct((s, SWA_Q_WIDTH), F32), jax.ShapeDtypeStruct((s, LANES), F32),
                   jax.ShapeDtypeStruct((s, LANES), F32), jax.ShapeDtypeStruct((n_pairs, 1, LANES), F32)),
        in_specs=[pair, whole, pl.BlockSpec((s, LANES), lambda p: (0, v_blk)), sink_spec, pair, pair],
        out_specs=(pair, whole, whole, sink_spec),
        scratch_shapes=[pltpu.VMEM((s, LANES), F32), pltpu.VMEM((s, LANES), F32)],
        args=(qn, kn, proj, sink_p, o, do), ride=ride)
    return dq, dk, dv, dsink, rides


def _rope_tables(s):
    inv_freq = 1.0 / (ROPE_THETA ** (jnp.arange(0, HEAD_DIM, 2, dtype=F32) / HEAD_DIM))
    ang = jnp.arange(s, dtype=F32)[:, None] * inv_freq[None, :]
    cos, sin = jnp.cos(ang), jnp.sin(ang)
    cos_p = jnp.tile(jnp.concatenate([cos, cos], axis=1), (1, LANES // HEAD_DIM))
    sin_p = jnp.tile(jnp.concatenate([-sin, sin], axis=1), (1, LANES // HEAD_DIM))
    return cos_p, sin_p


def _lane_tile(v, reps):
    return jnp.tile(v.reshape(1, -1), (1, reps))


def _natural(stack, w):
    n, r, c = stack.shape
    if MATRIX_NAMES[w] in ROW_SHARDED:
        return stack.reshape(n * r, c)
    if w == W_UP:
        return stack
    return jnp.transpose(stack, (1, 0, 2)).reshape(r, n * c)


def _pack_small(tree):
    flat = jnp.concatenate([tree[n].reshape(-1) for n in SMALL_NAMES])
    rows = -(-flat.shape[0] // (8 * LANES)) * 8
    return jnp.pad(flat, (0, rows * LANES - flat.shape[0])).reshape(rows, LANES)


def _unpack_small(packed, shapes):
    flat, out, off = packed.reshape(-1), {}, 0
    for n in SMALL_NAMES:
        size = shapes[n][0] * shapes[n][1]
        out[n] = flat[off:off + size].reshape(shapes[n])
        off += size
    return out


def train_step(x, target, weights, mom_m, mom_v):
    s = x.shape[0]
    cos_p, sin_p = _rope_tables(s)
    tri = (jnp.arange(SB_TK1)[:, None] > jnp.arange(SB_TK1)[None, :]).astype(BF16)
    tri = jnp.concatenate([tri, tri], axis=0)
    shards = [[weights[n][l].astype(BF16) for n in MATRIX_NAMES] for l in range(DEPTH)]
    core = lax.axis_index("c").astype(jnp.int32).reshape(1)
    chip = (2 * lax.axis_index("x") + lax.axis_index("y")).astype(jnp.int32).reshape(1)

    def gather(l, ws):
        return GatherJob([shards[l][w] for w in ws])

    w_in = _natural(exchange_alone(gather(0, [W_IN]), name="gather_w_in0")[0], W_IN)
    saved = []
    for l in range(DEPTH):
        g_mix = weights["mix_norm_g"][l].reshape(1, D_MODEL)
        g_mlp = weights["mlp_norm_g"][l].reshape(1, D_MODEL)
        gq = _lane_tile(weights["q_norm_g"][l], LANES // HEAD_DIM)
        gk = _lane_tile(weights["k_norm_g"][l], LANES // HEAD_DIM)
        sink_p = jnp.repeat(weights["sinks"][l].reshape(SWA_Q_WIDTH // LANES, 2), HEAD_DIM, axis=1)
        sink_p = sink_p.reshape(SWA_Q_WIDTH // LANES, 1, LANES)
        (h, proj, gates), ((s_bsb, s_bsw, s_out),) = norm_matmul(
            x, g_mix, w_in, gate_split=ATTN_WIDTH, name="in_proj", ride=[gather(l, [W_BSB, W_BSW, W_OUT])])
        if l + 1 < DEPTH:
            o_sb, ((s_up,), (s_in,)) = sb_attn_fwd(proj, tri, name="sb_fwd",
                                                   ride=[gather(l, [W_UP]), gather(l + 1, [W_IN])])
        else:
            o_sb, ((s_up,),) = sb_attn_fwd(proj, tri, name="sb_fwd_last", ride=[gather(l, [W_UP])])
        qn, kn = swa_prep_fwd(proj, cos_p, sin_p, gq, gk, name="swa_prep")
        o_sw, ((s_down,),) = swa_attn_fwd(qn, kn, proj, sink_p, name="swa_fwd", ride=[gather(l, [W_DOWN])])
        mats = [w_in, _natural(s_bsb, W_BSB), _natural(s_bsw, W_BSW), _natural(s_out, W_OUT), _natural(s_up, W_UP),
                _natural(s_down, W_DOWN)]
        x1, y_sb, y_sw, merged = merge_out_fwd(x, o_sb, o_sw, gates, mats[W_BSB], mats[W_BSW], mats[W_OUT],
                                               name="merge_out")
        (h2, u), _ = norm_matmul(x1, g_mlp, mats[W_UP], gate_split=None, name="mlp_up")
        x2 = mlp_down_fwd(x1, u, mats[W_DOWN], name="mlp_down")
        if l + 1 < DEPTH:
            w_in = _natural(s_in, W_IN)
        saved.append(dict(x=x, h=h, proj=proj, gates=gates, o_sb=o_sb, qn=qn, kn=kn, o_sw=o_sw, y_sb=y_sb, y_sw=y_sw,
                          merged=merged, x1=x1, h2=h2, u=u, g_mix=g_mix, g_mlp=g_mlp, gq=gq, gk=gk, sink_p=sink_p,
                          mats=mats))
        x = x2

    dx, dxb, loss = loss_head(x, target, name="loss_head")

    shard_shapes = [weights[n].shape[1:] for n in MATRIX_NAMES]
    parts = [lax.empty((DEPTH, N_CHIPS) + sh, BF16) for sh in shard_shapes]
    lands = [lax.empty((DEPTH, 3) + sh, BF16) for sh in shard_shapes]
    small_grads = {n: [None] * DEPTH for n in SMALL_NAMES}
    half = D_MODEL // 2

    def summed(l, ws, grads, landed):
        new = pair_sum(l, grads, landed, [parts[w] for w in ws], core, name="grad_pair_sum")
        for w, p in zip(ws, new):
            parts[w] = p

    def chip_job(items):
        return ChipJob(items, parts, lands)

    def landed_chip(job, outs):
        for w, a in zip(job.ws, outs):
            lands[w] = a

    in_pending = None
    for l in reversed(range(DEPTH)):
        a = saved[l]
        mats = a["mats"]
        (du,), _ = mlp_bwd_up(dxb, a["u"], mats[W_DOWN], name="mlp_bwd_up")
        dw_down = matmul_tn(a["u"], [dxb], a_block=half, out_cols=None, relu2=True, name="dw_down")
        dw_up = matmul_tn(a["h2"], [du], a_block=half, out_cols=du.shape[1] // N_DEV, relu2=False, name="dw_up")
        g_mlp_w = [dw_up, dw_down.reshape((N_DEV,) + shard_shapes[W_DOWN])]
        (dx1, dx1b, dg_mlp), (landed,) = matmul_nt_norm_bwd([du], mats[W_UP], a["x1"], a["g_mlp"], dx,
                                                            name="mlp_bwd_norm", ride=[PairJob(g_mlp_w)])
        summed(l, [W_UP, W_DOWN], g_mlp_w, landed)
        small_grads["mlp_norm_g"][l] = dg_mlp.reshape(D_MODEL)
        dw_out = matmul_tn(a["merged"], [dx1b], a_block=half, out_cols=None, relu2=False, name="dw_out")
        dy_sb, dy_sw, do_sb, do_sw, dgl = out_bwd(dx1b, mats[W_OUT], a["gates"], a["y_sb"], a["y_sw"],
                                                  mats[W_BSB], mats[W_BSW], name="out_bwd")
        dw_bsb = matmul_tn(a["o_sb"], [dy_sb], a_block=half, out_cols=D_MODEL // N_DEV, relu2=False, name="dw_branch_sb")
        dw_bsw = matmul_tn(a["o_sw"], [dy_sw], a_block=half, out_cols=D_MODEL // N_DEV, relu2=False, name="dw_branch_swa")
        g_mix_w = [dw_bsb, dw_bsw, dw_out.reshape((N_DEV,) + shard_shapes[W_OUT])]
        job = chip_job([(l, W_UP), (l, W_DOWN)])
        dq_sb, dk_sb, dv_sb, (outs, landed) = sb_attn_bwd(a["proj"], tri, a["o_sb"], do_sb, name="sb_bwd",
                                                         ride=[job, PairJob(g_mix_w)])
        landed_chip(job, outs)
        summed(l, [W_BSB, W_BSW, W_OUT], g_mix_w, landed)
        job = chip_job([(l, W_BSB), (l, W_BSW), (l, W_OUT)] + ([(in_pending, W_IN)] if in_pending is not None else []))
        dqn, dkn, dv_sw, dsink, (outs,) = swa_attn_bwd(a["qn"], a["kn"], a["proj"], a["sink_p"], a["o_sw"], do_sw,
                                                      name="swa_bwd", ride=[job])
        landed_chip(job, outs)
        dq_sw, dk_sw, dv_swb, dgq, dgk = swa_prep_bwd(a["proj"], cos_p, sin_p, a["gq"], a["gk"], dqn, dkn, dv_sw,
                                                      name="swa_prep_bwd")
        small_grads["q_norm_g"][l] = dgq.reshape(SWA_Q_WIDTH // HEAD_DIM, HEAD_DIM).sum(0)
        small_grads["k_norm_g"][l] = dgk.reshape(LANES // HEAD_DIM, HEAD_DIM).sum(0)
        small_grads["sinks"][l] = dsink[:, 0, ::HEAD_DIM].reshape(SWA_Q_WIDTH // HEAD_DIM)
        pieces = [dq_sb, dk_sb, dv_sb, dq_sw, dk_sw, dv_swb, dgl]
        dw_in = matmul_tn(a["h"], pieces, a_block=half, out_cols=None, relu2=False, name="dw_in")
        g_in = [jnp.transpose(dw_in.reshape(D_MODEL, N_DEV, IN_WIDTH // N_DEV), (1, 0, 2))]
        (dx, dxb, dg_mix), (landed,) = matmul_nt_norm_bwd(pieces, mats[W_IN], a["x"], a["g_mix"], dx1,
                                                         name="in_proj_bwd", ride=[PairJob(g_in)])
        summed(l, [W_IN], g_in, landed)
        small_grads["mix_norm_g"][l] = dg_mix.reshape(D_MODEL)
        in_pending = l
    job = chip_job([(in_pending, W_IN)])
    landed_chip(job, exchange_alone(job, name="grad_chip_exchange_in0"))

    out_g, out_d, out_m, out_v = {}, {}, {}, {}
    for i, n in enumerate(MATRIX_NAMES):
        out_g[n], out_d[n], out_m[n], out_v[n] = reduce_adamw(parts[i], lands[i], chip, weights[n], mom_m[n], mom_v[n],
                                                              name="adamw_" + n)
    small_shapes = {n: weights[n].shape for n in SMALL_NAMES}
    small_all = gather_small(_pack_small({n: jnp.stack(v) for n, v in small_grads.items()}), name="gather_small_grads")
    sg, sd, sm, sv = small_adamw(small_all, _pack_small(weights), _pack_small(mom_m), _pack_small(mom_v),
                                 name="small_adamw")
    for tree, packed_small in ((out_g, sg), (out_d, sd), (out_m, sm), (out_v, sv)):
        tree.update(_unpack_small(packed_small, small_shapes))
    return loss, dx, (out_g, out_d, out_m, out_v)


def kernel(x, mix_norm_g, w_in, q_norm_g, k_norm_g, sinks, w_branch_sb, w_branch_swa, w_out, mlp_norm_g, w_up, w_down, loss_target, m_mix_norm_g, m_w_in, m_q_norm_g, m_k_norm_g, m_sinks, m_w_branch_sb, m_w_branch_swa, m_w_out, m_mlp_norm_g, m_w_up, m_w_down, v_mix_norm_g, v_w_in, v_q_norm_g, v_k_norm_g, v_sinks, v_w_branch_sb, v_w_branch_swa, v_w_out, v_mlp_norm_g, v_w_up, v_w_down):
    weights = dict(mix_norm_g=mix_norm_g, w_in=w_in, q_norm_g=q_norm_g, k_norm_g=k_norm_g, sinks=sinks,
                   w_branch_sb=w_branch_sb, w_branch_swa=w_branch_swa, w_out=w_out, mlp_norm_g=mlp_norm_g, w_up=w_up,
                   w_down=w_down)
    mom_m = dict(mix_norm_g=m_mix_norm_g, w_in=m_w_in, q_norm_g=m_q_norm_g, k_norm_g=m_k_norm_g, sinks=m_sinks,
                 w_branch_sb=m_w_branch_sb, w_branch_swa=m_w_branch_swa, w_out=m_w_out, mlp_norm_g=m_mlp_norm_g,
                 w_up=m_w_up, w_down=m_w_down)
    mom_v = dict(mix_norm_g=v_mix_norm_g, w_in=v_w_in, q_norm_g=v_q_norm_g, k_norm_g=v_k_norm_g, sinks=v_sinks,
                 w_branch_sb=v_w_branch_sb, w_branch_swa=v_w_branch_swa, w_out=v_w_out, mlp_norm_g=v_mlp_norm_g,
                 w_up=v_w_up, w_down=v_w_down)
    loss_part, grad_x, outs = train_step(x[0], loss_target[0], weights, mom_m, mom_v)
    loss = lax.psum(loss_part[0, 0], MESH_AXES)
    return (loss, grad_x[None], *[outs[0][n] for n in WEIGHT_ORDER], *[outs[1][n] for n in WEIGHT_ORDER],
            *[outs[2][n] for n in WEIGHT_ORDER], *[outs[3][n] for n in WEIGHT_ORDER])
```

```python
import functools
import math

import jax
import jax.numpy as jnp
from jax import lax
from jax.experimental import pallas as pl
from jax.experimental.pallas import tpu as pltpu

F32 = jnp.float32
BF16 = jnp.bfloat16

DEPTH = 4
D_MODEL = 1024
HEAD_DIM = 64
LANES = 128
WINDOW = 128
SB_WIDTH = 512
SWA_Q_WIDTH = 512
SWA_KV_WIDTH = 128
ATTN_WIDTH = 3 * SB_WIDTH + SWA_Q_WIDTH + 2 * SWA_KV_WIDTH
IN_WIDTH = ATTN_WIDTH + 2 * D_MODEL
ROPE_THETA = 10000.0
NORM_EPS = 1e-6
SCALE = HEAD_DIM ** -0.5
NEG = -1e30
N_DEV = 8
N_CHIPS = 4

ADAM_LR = 0.001
ADAM_B1 = 0.9
ADAM_B2 = 0.999
ADAM_EPS = 1e-08
ADAM_WD = 0.01
ADAM_STEP = 10

SB_TQ = 128
SB_TK1 = 384
SB_TK = 256
SB_CUTOFF = -88.0
SWA_TQ = 128
SWA_TK = 256
ROW_TILE = 512
VMEM_LIMIT = 56 * 1024 * 1024

MATRIX_NAMES = ("w_in", "w_branch_sb", "w_branch_swa", "w_out", "w_up", "w_down")
W_IN, W_BSB, W_BSW, W_OUT, W_UP, W_DOWN = range(6)
ROW_SHARDED = ("w_out", "w_down")
SMALL_NAMES = ("mix_norm_g", "q_norm_g", "k_norm_g", "sinks", "mlp_norm_g")
WEIGHT_ORDER = ("mix_norm_g", "w_in", "q_norm_g", "k_norm_g", "sinks", "w_branch_sb", "w_branch_swa", "w_out",
                "mlp_norm_g", "w_up", "w_down")
MESH_AXES = ("x", "y", "c")

ANY = pl.BlockSpec(memory_space=pl.ANY)
MESH = pl.DeviceIdType.MESH


def _params(*sem):
    return pltpu.CompilerParams(dimension_semantics=sem, vmem_limit_bytes=VMEM_LIMIT)


def _dot(a, b):
    return jnp.dot(a, b, preferred_element_type=F32)


def _dot_nt(a, b):
    return lax.dot_general(a, b, (((1,), (1,)), ((), ())), preferred_element_type=F32)


def _dot_tn(a, b):
    return lax.dot_general(a, b, (((0,), (0,)), ((), ())), preferred_element_type=F32)


def _split_bf16(x):
    hi = x.astype(BF16)
    lo = (x - hi.astype(F32)).astype(BF16)
    return hi, lo


def _rsqrt_ms(x):
    return lax.rsqrt(jnp.mean(x * x, axis=-1, keepdims=True) + NORM_EPS)


def _place():
    return lax.axis_index("x"), lax.axis_index("y"), lax.axis_index("c")


class _Gather:
    def __init__(self, x_refs, out_refs, send_sems, recv_sems, local_sems):
        self.x_refs, self.out_refs = x_refs, out_refs
        self.send_sems, self.recv_sems, self.local_sems = send_sems, recv_sems, local_sems
        self.n = len(x_refs)
        x, y, c = _place()
        self.c = c
        self.me, self.sibling = (x, y, c), (x, y, 1 - c)
        self.chips = [(1 - x, y), (x, 1 - y), (1 - x, 1 - y)]

    def _copy(self, k, w, blk, to, own=False):
        dst = self.out_refs[w].at[4 * blk[0] + 2 * blk[1] + blk[2]]
        return pltpu.make_async_remote_copy(
            src_ref=self.x_refs[w] if own else dst, dst_ref=dst, send_sem=self.send_sems.at[k, w],
            recv_sem=self.recv_sems.at[k, w], device_id=to, device_id_type=MESH)

    def _mine(self, w):
        me = self.me
        return pltpu.make_async_copy(self.x_refs[w], self.out_refs[w].at[4 * me[0] + 2 * me[1] + me[2]],
                                     self.local_sems.at[w])

    def _first(self, w):
        return [self._copy(0, w, self.me, self.sibling, own=True)] + [
            self._copy(1 + j, w, self.me, (*chip, self.c), own=True) for j, chip in enumerate(self.chips)]

    def _passed(self, j, w):
        return self._copy(4 + j, w, (*self.chips[j], self.c), self.sibling)

    def start(self):
        for w in range(self.n):
            self._mine(w).start()
            for cp in self._first(w):
                cp.start()

    def relay(self):
        for j, chip in enumerate(self.chips):
            for w in range(self.n):
                self._copy(1 + j, w, (*chip, self.c), self.me).wait_recv()
                self._passed(j, w).start()

    def finish(self):
        for w in range(self.n):
            self._copy(0, w, self.sibling, self.me).wait_recv()
            for j, chip in enumerate(self.chips):
                self._copy(4 + j, w, (*chip, 1 - self.c), self.me).wait_recv()
            for cp in self._first(w):
                cp.wait_send()
            for j in range(3):
                self._passed(j, w).wait_send()
            self._mine(w).wait()


class GatherJob:
    def __init__(self, shards):
        n = len(shards)
        self.inputs = list(shards)
        self.out_shapes = [jax.ShapeDtypeStruct((N_DEV,) + s.shape, s.dtype) for s in shards]
        self.aliases = {}
        self.scratch = [pltpu.SemaphoreType.DMA((7, n)), pltpu.SemaphoreType.DMA((7, n)),
                        pltpu.SemaphoreType.DMA((n,))]

    def bind(self, in_refs, out_refs, scratch_refs):
        return _Gather(in_refs, out_refs, *scratch_refs)


class _Copies:
    def __init__(self, copies):
        self.copies = copies

    def start(self):
        for cp in self.copies:
            cp.start()

    def relay(self):
        pass

    def finish(self):
        for cp in self.copies:
            cp.wait_recv()
        for cp in self.copies:
            cp.wait_send()


class ChipJob:
    def __init__(self, items, parts, lands):
        self.ws = sorted({w for _, w in items})
        n = len(self.ws)
        self.items = [(layer, self.ws.index(w)) for layer, w in items]
        self.inputs = [parts[w] for w in self.ws] + [lands[w] for w in self.ws]
        self.out_shapes = [jax.ShapeDtypeStruct(lands[w].shape, lands[w].dtype) for w in self.ws]
        self.aliases = {n + i: i for i in range(n)}
        self.scratch = [pltpu.SemaphoreType.DMA((3, n)), pltpu.SemaphoreType.DMA((3, n))]

    def bind(self, in_refs, out_refs, scratch_refs):
        send_sems, recv_sems = scratch_refs
        x, y, c = _place()
        chips = [(1 - x, y), (x, 1 - y), (1 - x, 1 - y)]
        return _Copies([pltpu.make_async_remote_copy(
            src_ref=in_refs[i].at[layer, 2 * px + py], dst_ref=out_refs[i].at[layer, j],
            send_sem=send_sems.at[j, i], recv_sem=recv_sems.at[j, i], device_id=(px, py, c), device_id_type=MESH)
            for layer, i in self.items for j, (px, py) in enumerate(chips)])


class PairJob:
    def __init__(self, grads):
        n = len(grads)
        self.inputs = list(grads)
        self.out_shapes = [jax.ShapeDtypeStruct((N_CHIPS,) + g.shape[1:], g.dtype) for g in grads]
        self.aliases = {}
        self.scratch = [pltpu.SemaphoreType.DMA((N_CHIPS, n)), pltpu.SemaphoreType.DMA((N_CHIPS, n))]

    def bind(self, in_refs, out_refs, scratch_refs):
        send_sems, recv_sems = scratch_refs
        x, y, c = _place()
        return _Copies([pltpu.make_async_remote_copy(
            src_ref=in_refs[w].at[2 * k + (1 - c)], dst_ref=out_refs[w].at[k], send_sem=send_sems.at[k, w],
            recv_sem=recv_sems.at[k, w], device_id=(x, y, 1 - c), device_id_type=MESH)
            for w in range(len(in_refs)) for k in range(N_CHIPS)])


def _call(body, *, name, grid, in_specs, out_specs, out_shape, args, scratch_shapes=(), ride=()):
    out_specs, out_shape, in_specs = tuple(out_specs), tuple(out_shape), list(in_specs)
    scratch_shapes = list(scratch_shapes)
    order = ("arbitrary",) * len(grid)
    if not ride:
        outs = pl.pallas_call(body, name=name, grid=grid, in_specs=in_specs, out_specs=out_specs, out_shape=out_shape,
                              scratch_shapes=scratch_shapes, compiler_params=_params(*order))(*args)
        return tuple(outs), []
    n_in, n_out, n_scr = len(in_specs), len(out_specs), len(scratch_shapes)
    n_steps = math.prod(grid)
    relay_at = n_steps - max(1, n_steps // 4)

    def split(refs, pos, counts):
        groups = []
        for k in counts:
            groups.append(refs[pos:pos + k])
            pos += k
        return groups, pos

    def wrapped(*refs):
        ins, pos = refs[:n_in], n_in
        job_in, pos = split(refs, pos, [len(j.inputs) for j in ride])
        outs, pos = refs[pos:pos + n_out], pos + n_out
        job_out, pos = split(refs, pos, [len(j.out_shapes) for j in ride])
        scr, pos = refs[pos:pos + n_scr], pos + n_scr
        job_scr, pos = split(refs, pos, [len(j.scratch) for j in ride])
        bound = [j.bind(i, o, s) for j, i, o, s in zip(ride, job_in, job_out, job_scr)]
        step = pl.program_id(0)
        for axis in range(1, len(grid)):
            step = step * grid[axis] + pl.program_id(axis)

        @pl.when(step == 0)
        def _():
            for b in bound:
                b.start()

        @pl.when(step == relay_at)
        def _():
            for b in bound:
                b.relay()

        body(*ins, *outs, *scr)

        @pl.when(step == n_steps - 1)
        def _():
            for b in bound:
                b.finish()

    aliases, in_pos, out_pos = {}, n_in, n_out
    for j in ride:
        aliases.update({in_pos + i: out_pos + o for i, o in j.aliases.items()})
        in_pos += len(j.inputs)
        out_pos += len(j.out_shapes)
    results = pl.pallas_call(
        wrapped, name=name, grid=grid, in_specs=in_specs + [ANY] * (in_pos - n_in),
        out_specs=out_specs + (ANY,) * (out_pos - n_out),
        out_shape=out_shape + tuple(s for j in ride for s in j.out_shapes),
        scratch_shapes=scratch_shapes + [s for j in ride for s in j.scratch], input_output_aliases=aliases,
        compiler_params=pltpu.CompilerParams(dimension_semantics=order, vmem_limit_bytes=VMEM_LIMIT,
                                             has_side_effects=True),
    )(*args, *[a for j in ride for a in j.inputs])
    job_results, pos = split(list(results), n_out, [len(j.out_shapes) for j in ride])
    return tuple(results[:n_out]), job_results


def exchange_alone(job, *, name):
    n_in, n_out = len(job.inputs), len(job.out_shapes)

    def body(*refs):
        b = job.bind(refs[:n_in], refs[n_in:n_in + n_out], refs[n_in + n_out:])
        b.start()
        b.relay()
        b.finish()

    return list(pl.pallas_call(
        body, name=name, out_shape=tuple(job.out_shapes), in_specs=[ANY] * n_in, out_specs=(ANY,) * n_out,
        scratch_shapes=job.scratch, input_output_aliases=job.aliases,
        compiler_params=pltpu.CompilerParams(has_side_effects=True),
    )(*job.inputs))


PAIR_SUM_CHUNKS = 2


def pair_sum(layer, grads, landed, parts, core, *, name):
    n = len(grads)

    def body(c_ref, *refs):
        g_refs, l_refs, o_refs = refs[:n], refs[n:2 * n], refs[3 * n:]
        for w in range(n):
            o_refs[w][...] = (g_refs[w][...].astype(F32) + l_refs[w][...].astype(F32)).astype(BF16)

    def blk(g):
        return (None, g.shape[1] // PAIR_SUM_CHUNKS, g.shape[2])

    in_specs = [pl.BlockSpec(blk(g), lambda k, i, c_ref: (2 * k + c_ref[0], i, 0)) for g in grads]
    in_specs += [pl.BlockSpec(blk(g), lambda k, i, c_ref: (k, i, 0)) for g in grads]
    in_specs += [ANY] * n
    out_specs = tuple(pl.BlockSpec((None,) + blk(g), lambda k, i, c_ref: (layer, k, i, 0)) for g in grads)
    return list(pl.pallas_call(
        body, name=name, out_shape=tuple(jax.ShapeDtypeStruct(p.shape, p.dtype) for p in parts),
        grid_spec=pltpu.PrefetchScalarGridSpec(num_scalar_prefetch=1, grid=(N_CHIPS, PAIR_SUM_CHUNKS),
                                               in_specs=in_specs, out_specs=out_specs),
        input_output_aliases={1 + 2 * n + w: w for w in range(n)},
        compiler_params=_params("parallel", "parallel"),
    )(core, *grads, *landed, *parts))


def _adamw(w, g, m, v):
    m = ADAM_B1 * m + (1.0 - ADAM_B1) * g
    v = ADAM_B2 * v + (1.0 - ADAM_B2) * (g * g)
    m_hat = m / (1.0 - ADAM_B1 ** ADAM_STEP)
    v_hat = v / (1.0 - ADAM_B2 ** ADAM_STEP)
    delta = -ADAM_LR * (m_hat / (jnp.sqrt(v_hat) + ADAM_EPS) + ADAM_WD * w)
    return delta, m, v


def reduce_adamw(part, land, chip, w, m, v, *, name):
    _, r, c = w.shape
    tr = min(r, 256)

    def body(k_ref, own_ref, l0_ref, l1_ref, l2_ref, w_ref, m_ref, v_ref, g_out, d_out, m_out, v_out):
        g = own_ref[...].astype(F32) + l0_ref[...].astype(F32) + l1_ref[...].astype(F32) + l2_ref[...].astype(F32)
        delta, m_new, v_new = _adamw(w_ref[...], g, m_ref[...], v_ref[...])
        g_out[...] = g
        d_out[...] = delta
        m_out[...] = m_new
        v_out[...] = v_new

    row = pl.BlockSpec((None, tr, c), lambda l, i, k_ref: (l, i, 0))

    def slot(j):
        return pl.BlockSpec((None, None, tr, c), lambda l, i, k_ref: (l, j, i, 0))

    return pl.pallas_call(
        body, name=name, out_shape=(jax.ShapeDtypeStruct(w.shape, F32),) * 4,
        grid_spec=pltpu.PrefetchScalarGridSpec(
            num_scalar_prefetch=1, grid=(DEPTH, r // tr),
            in_specs=[pl.BlockSpec((None, None, tr, c), lambda l, i, k_ref: (l, k_ref[0], i, 0)), slot(0), slot(1),
                      slot(2), row, row, row],
            out_specs=(row, row, row, row)),
        compiler_params=_params("parallel", "parallel"),
    )(chip, part, land, land, land, w, m, v)


def gather_small(block, *, name):
    def body(x_ref, out_ref, send_sems, recv_sems, local_sem):
        x, y, c = _place()
        me = 4 * x + 2 * y + c
        mine = pltpu.make_async_copy(x_ref, out_ref.at[me], local_sem)
        mine.start()
        peers = [(x ^ (k >> 2), y ^ ((k >> 1) & 1), c ^ (k & 1)) for k in range(1, N_DEV)]
        copies = [pltpu.make_async_remote_copy(
            src_ref=x_ref, dst_ref=out_ref.at[me], send_sem=send_sems.at[k], recv_sem=recv_sems.at[k],
            device_id=peer, device_id_type=MESH) for k, peer in enumerate(peers)]
        for cp in copies:
            cp.start()
        for k, (px, py, pc) in enumerate(peers):
            pltpu.make_async_remote_copy(
                src_ref=x_ref, dst_ref=out_ref.at[4 * px + 2 * py + pc], send_sem=send_sems.at[k],
                recv_sem=recv_sems.at[k], device_id=(px, py, pc), device_id_type=MESH).wait_recv()
        for cp in copies:
            cp.wait_send()
        mine.wait()

    return pl.pallas_call(
        body, name=name, out_shape=jax.ShapeDtypeStruct((N_DEV,) + block.shape, block.dtype),
        in_specs=[ANY], out_specs=ANY,
        scratch_shapes=[pltpu.SemaphoreType.DMA((7,)), pltpu.SemaphoreType.DMA((7,)), pltpu.SemaphoreType.DMA],
        compiler_params=pltpu.CompilerParams(has_side_effects=True),
    )(block)


def small_adamw(gathered, w, m, v, *, name):
    def body(g_ref, w_ref, m_ref, v_ref, g_out, d_out, m_out, v_out):
        g = g_ref[0]
        for d in range(1, N_DEV):
            g = g + g_ref[d]
        delta, m_new, v_new = _adamw(w_ref[...], g, m_ref[...], v_ref[...])
        g_out[...] = g
        d_out[...] = delta
        m_out[...] = m_new
        v_out[...] = v_new

    return pl.pallas_call(
        body, name=name, out_shape=(jax.ShapeDtypeStruct(w.shape, F32),) * 4,
    )(gathered, w, m, v)


def norm_matmul(x, g, w, *, gate_split, name, ride=()):
    s, d = x.shape
    tm = min(ROW_TILE, s)
    blocked = w.ndim == 3
    n = w.shape[1] if not blocked else w.shape[0] * w.shape[2]

    def body(x_ref, g_ref, w_ref, h_ref, *outs):
        xv = x_ref[...]
        h = ((xv * _rsqrt_ms(xv)) * g_ref[...]).astype(BF16)
        h_ref[...] = h
        if blocked:
            nb = w_ref.shape[2]
            for j in range(w_ref.shape[0]):
                outs[0][:, j * nb:(j + 1) * nb] = _dot(h, w_ref[j]).astype(BF16)
        else:
            p = _dot(h, w_ref[...])
            outs[0][...] = p[:, :gate_split].astype(BF16)
            outs[1][...] = (1.0 / (1.0 + jnp.exp(-p[:, gate_split:]))).astype(BF16)

    row = lambda i: (i, 0)
    fixed = lambda i: (0, 0)
    if blocked:
        out_shape = (jax.ShapeDtypeStruct((s, d), BF16), jax.ShapeDtypeStruct((s, n), BF16))
        out_specs = (pl.BlockSpec((tm, d), row), pl.BlockSpec((tm, n), row))
        w_spec = pl.BlockSpec(w.shape, lambda i: (0, 0, 0))
    else:
        out_shape = (jax.ShapeDtypeStruct((s, d), BF16), jax.ShapeDtypeStruct((s, gate_split), BF16),
                     jax.ShapeDtypeStruct((s, n - gate_split), BF16))
        out_specs = (pl.BlockSpec((tm, d), row), pl.BlockSpec((tm, gate_split), row),
                     pl.BlockSpec((tm, n - gate_split), row))
        w_spec = pl.BlockSpec((d, n), fixed)
    return _call(body, name=name, grid=(s // tm,), out_shape=out_shape, out_specs=out_specs,
                 in_specs=[pl.BlockSpec((tm, d), row), pl.BlockSpec((1, d), fixed), w_spec], args=(x, g, w), ride=ride)


def merge_out_fwd(x, o_sb, o_sw, gates, w_bsb, w_bsw, w_o, *, name):
    s, d = x.shape
    tm = min(ROW_TILE, s)

    def body(x_ref, osb_ref, osw_ref, g_ref, wsb_ref, wsw_ref, wo_ref, x1_ref, ysb_ref, ysw_ref, mg_ref):
        y_sb = _dot(osb_ref[...].astype(BF16), wsb_ref[...])
        y_sw = _dot(osw_ref[...].astype(BF16), wsw_ref[...])
        g = g_ref[...].astype(F32)
        merged = (g[:, :d] * y_sb + g[:, d:] * y_sw).astype(BF16)
        ysb_ref[...] = y_sb.astype(BF16)
        ysw_ref[...] = y_sw.astype(BF16)
        mg_ref[...] = merged
        x1_ref[...] = x_ref[...] + _dot(merged, wo_ref[...])

    row = lambda i: (i, 0)
    fixed = lambda i: (0, 0)
    wd = o_sb.shape[1]
    return pl.pallas_call(
        body, name=name, grid=(s // tm,),
        out_shape=(jax.ShapeDtypeStruct((s, d), F32),) + (jax.ShapeDtypeStruct((s, d), BF16),) * 3,
        in_specs=[pl.BlockSpec((tm, d), row), pl.BlockSpec((tm, wd), row), pl.BlockSpec((tm, wd), row),
                  pl.BlockSpec((tm, 2 * d), row), pl.BlockSpec((wd, d), fixed), pl.BlockSpec((wd, d), fixed),
                  pl.BlockSpec((d, d), fixed)],
        out_specs=(pl.BlockSpec((tm, d), row),) * 4, compiler_params=_params("parallel"),
    )(x, o_sb, o_sw, gates, w_bsb, w_bsw, w_o)


def mlp_down_fwd(x1, u, w_down, *, name):
    s, d = x1.shape
    f = u.shape[1]
    tm = min(ROW_TILE, s)

    def body(x_ref, u_ref, w_ref, o_ref):
        a = jnp.maximum(u_ref[...].astype(F32), 0.0)
        o_ref[...] = x_ref[...] + _dot((a * a).astype(BF16), w_ref[...])

    row = lambda i: (i, 0)
    return pl.pallas_call(
        body, name=name, grid=(s // tm,), out_shape=jax.ShapeDtypeStruct((s, d), F32),
        in_specs=[pl.BlockSpec((tm, d), row), pl.BlockSpec((tm, f), row), pl.BlockSpec((f, d), lambda i: (0, 0))],
        out_specs=pl.BlockSpec((tm, d), row), compiler_params=_params("parallel"),
    )(x1, u, w_down)


def loss_head(y, target, *, name):
    s, d = y.shape
    tm = min(ROW_TILE, s)

    def body(y_ref, t_ref, dy_ref, dyb_ref, loss_ref):
        @pl.when(pl.program_id(0) == 0)
        def _():
            loss_ref[...] = jnp.zeros_like(loss_ref)

        e = y_ref[...] - t_ref[...]
        dy = e * (1.0 / d)
        dy_ref[...] = dy
        dyb_ref[...] = dy.astype(BF16)
        per_row = jnp.sum(e * e, axis=1, keepdims=True) * (0.5 / d)
        loss_ref[...] += jnp.sum(per_row, axis=0, keepdims=True)

    row = lambda i: (i, 0)
    return pl.pallas_call(
        body, name=name, grid=(s // tm,),
        out_shape=(jax.ShapeDtypeStruct((s, d), F32), jax.ShapeDtypeStruct((s, d), BF16),
                   jax.ShapeDtypeStruct((1, 1), F32)),
        in_specs=[pl.BlockSpec((tm, d), row), pl.BlockSpec((tm, d), row)],
        out_specs=(pl.BlockSpec((tm, d), row), pl.BlockSpec((tm, d), row), pl.BlockSpec((1, 1), lambda i: (0, 0))),
        compiler_params=_params("arbitrary"),
    )(y, target)


def mlp_bwd_up(dxb, u, w_down, *, name, ride=()):
    s, d = dxb.shape
    f = u.shape[1]
    tm = min(ROW_TILE, s)

    def body(dx_ref, u_ref, w_ref, du_ref):
        da = _dot_nt(dx_ref[...], w_ref[...])
        du_ref[...] = (da * (2.0 * jnp.maximum(u_ref[...].astype(F32), 0.0))).astype(BF16)

    row = lambda i: (i, 0)
    return _call(body, name=name, grid=(s // tm,), out_shape=(jax.ShapeDtypeStruct((s, f), BF16),),
                 in_specs=[pl.BlockSpec((tm, d), row), pl.BlockSpec((tm, f), row),
                           pl.BlockSpec((f, d), lambda i: (0, 0))],
                 out_specs=(pl.BlockSpec((tm, f), row),), args=(dxb, u, w_down), ride=ride)


def matmul_nt_norm_bwd(pieces, w, x, g, dres, *, name, ride=()):
    s = x.shape[0]
    d = x.shape[1]
    tm = min(ROW_TILE, s)
    blocked = w.ndim == 3
    n_pieces = len(pieces)
    widths = [p.shape[1] for p in pieces]

    def body(*refs):
        p_refs = refs[:n_pieces]
        w_ref, x_ref, g_ref, dres_ref, dx_ref, dxb_ref, dg_ref = refs[n_pieces:]

        @pl.when(pl.program_id(0) == 0)
        def _():
            dg_ref[...] = jnp.zeros_like(dg_ref)

        if blocked:
            nb = w_ref.shape[2]
            dh = _dot_nt(p_refs[0][:, :nb], w_ref[0])
            for j in range(1, w_ref.shape[0]):
                dh = dh + _dot_nt(p_refs[0][:, j * nb:(j + 1) * nb], w_ref[j])
        else:
            dh, off = None, 0
            for p_ref, width in zip(p_refs, widths):
                part = _dot_nt(p_ref[...], w_ref[:, off:off + width])
                dh = part if dh is None else dh + part
                off += width
        xv = x_ref[...]
        r = _rsqrt_ms(xv)
        dyg = dh * g_ref[...]
        dx = dres_ref[...] + r * dyg - xv * ((r * r * r) * jnp.mean(dyg * xv, axis=-1, keepdims=True))
        dx_ref[...] = dx
        dxb_ref[...] = dx.astype(BF16)
        dg_ref[...] += jnp.sum(dh * (xv * r), axis=0, keepdims=True)

    row = lambda i: (i, 0)
    fixed = lambda i: (0, 0)
    w_spec = pl.BlockSpec(w.shape, (lambda i: (0, 0, 0)) if blocked else fixed)
    return _call(
        body, name=name, grid=(s // tm,),
        out_shape=(jax.ShapeDtypeStruct((s, d), F32), jax.ShapeDtypeStruct((s, d), BF16),
                   jax.ShapeDtypeStruct((1, d), F32)),
        in_specs=[pl.BlockSpec((tm, width), row) for width in widths] + [
            w_spec, pl.BlockSpec((tm, d), row), pl.BlockSpec((1, d), fixed), pl.BlockSpec((tm, d), row)],
        out_specs=(pl.BlockSpec((tm, d), row), pl.BlockSpec((tm, d), row), pl.BlockSpec((1, d), fixed)),
        args=(*pieces, w, x, g, dres), ride=ride)


def out_bwd(dx1b, w_o, gates, y_sb, y_sw, w_bsb, w_bsw, *, name):
    s, d = dx1b.shape
    wd = w_bsb.shape[0]
    tm = min(ROW_TILE, s)

    def body(dx_ref, wo_ref, g_ref, ysb_ref, ysw_ref, wsb_ref, wsw_ref, dysb_ref, dysw_ref, dosb_ref, dosw_ref, dgl_ref):
        dm = _dot_nt(dx_ref[...], wo_ref[...])
        g = g_ref[...].astype(F32)
        g0, g1 = g[:, :d], g[:, d:]
        dy_sb = (dm * g0).astype(BF16)
        dy_sw = (dm * g1).astype(BF16)
        dysb_ref[...] = dy_sb
        dysw_ref[...] = dy_sw
        dosb_ref[...] = _dot_nt(dy_sb, wsb_ref[...])
        dosw_ref[...] = _dot_nt(dy_sw, wsw_ref[...])
        dgl_ref[:, :d] = (dm * ysb_ref[...].astype(F32) * (g0 * (1.0 - g0))).astype(BF16)
        dgl_ref[:, d:] = (dm * ysw_ref[...].astype(F32) * (g1 * (1.0 - g1))).astype(BF16)

    row = lambda i: (i, 0)
    fixed = lambda i: (0, 0)
    return pl.pallas_call(
        body, name=name, grid=(s // tm,),
        out_shape=(jax.ShapeDtypeStruct((s, d), BF16), jax.ShapeDtypeStruct((s, d), BF16),
                   jax.ShapeDtypeStruct((s, wd), F32), jax.ShapeDtypeStruct((s, wd), F32),
                   jax.ShapeDtypeStruct((s, 2 * d), BF16)),
        in_specs=[pl.BlockSpec((tm, d), row), pl.BlockSpec((d, d), fixed), pl.BlockSpec((tm, 2 * d), row),
                  pl.BlockSpec((tm, d), row), pl.BlockSpec((tm, d), row), pl.BlockSpec((wd, d), fixed),
                  pl.BlockSpec((wd, d), fixed)],
        out_specs=(pl.BlockSpec((tm, d), row), pl.BlockSpec((tm, d), row), pl.BlockSpec((tm, wd), row),
                   pl.BlockSpec((tm, wd), row), pl.BlockSpec((tm, 2 * d), row)),
        compiler_params=_params("parallel"),
    )(dx1b, w_o, gates, y_sb, y_sw, w_bsb, w_bsw)


def matmul_tn(a, pieces, *, a_block, out_cols, relu2, name):
    s, m = a.shape
    widths = [p.shape[1] for p in pieces]
    n = sum(widths)
    n_pieces = len(pieces)
    ts = min(512 if n >= 4096 else 2048, s)
    n_steps = s // ts
    if out_cols is None:
        out_shape = jax.ShapeDtypeStruct((m // a_block, a_block, n), BF16)
        out_spec = pl.BlockSpec((None, a_block, n), lambda i, k: (i, 0, 0))
    else:
        out_shape = jax.ShapeDtypeStruct((n // out_cols, m, out_cols), BF16)
        out_spec = pl.BlockSpec((n // out_cols, a_block, out_cols), lambda i, k: (0, i, 0))

    def body(a_ref, *refs):
        b_refs, o_ref, acc = refs[:n_pieces], refs[n_pieces], refs[n_pieces + 1]
        k = pl.program_id(1)

        @pl.when(k == 0)
        def _():
            acc[...] = jnp.zeros_like(acc)

        av = a_ref[...]
        if relu2:
            af = jnp.maximum(av.astype(F32), 0.0)
            av = af * af
        av = av.astype(BF16)
        off = 0
        for b_ref in b_refs:
            width = b_ref.shape[1]
            acc[:, off:off + width] += _dot_tn(av, b_ref[...].astype(BF16))
            off += width

        @pl.when(k == n_steps - 1)
        def _():
            if out_cols is None:
                o_ref[...] = acc[...].astype(BF16)
            else:
                for j in range(n // out_cols):
                    o_ref[j] = acc[:, j * out_cols:(j + 1) * out_cols].astype(BF16)

    return pl.pallas_call(
        body, name=name, grid=(m // a_block, n_steps), out_shape=out_shape,
        in_specs=[pl.BlockSpec((ts, a_block), lambda i, k: (k, i))] + [
            pl.BlockSpec((ts, width), lambda i, k: (k, 0)) for width in widths],
        out_specs=out_spec, scratch_shapes=[pltpu.VMEM((a_block, n), F32)],
        compiler_params=_params("parallel", "arbitrary"),
    )(a, *pieces)


def _softplus(z):
    return jnp.maximum(z, 0.0) + jnp.log(1.0 + jnp.exp(-jnp.abs(z)))


def _suffix_sums(x, tri2):
    groups = x.shape[1] // LANES
    outs, run = [None] * groups, None
    for g in reversed(range(groups)):
        xg = x[:, g * LANES:(g + 1) * LANES]
        hi, lo = _split_bf16(xg)
        inner = _dot(jnp.concatenate([hi, lo], axis=1), tri2)
        outs[g] = inner if run is None else inner + run
        total = jnp.sum(xg, axis=1, keepdims=True)
        run = total if run is None else run + total
    return jnp.concatenate(outs, axis=1), run


def _head_mask(h):
    return (lax.broadcasted_iota(jnp.int32, (1, LANES), 1) // HEAD_DIM) == h


def _stack_heads(x):
    zero = jnp.zeros_like(x)
    return jnp.concatenate([jnp.where(_head_mask(0), x, zero), jnp.where(_head_mask(1), x, zero)], axis=0)


def _unstack_heads(r, t):
    return jnp.where(_head_mask(0), r[:t], r[t:])


def _sb_positions(q0, tk):
    row = lax.broadcasted_iota(jnp.int32, (2 * SB_TQ, tk), 0)
    col = lax.broadcasted_iota(jnp.int32, (2 * SB_TQ, tk), 1)
    return q0 + jnp.where(row >= SB_TQ, row - SB_TQ, row), col


def _sb_first_key(q0):
    return pl.multiple_of(jnp.maximum(q0 + SB_TQ - SB_TK1, 0), SB_TQ)


def _sb_next_key(k_prev):
    return pl.multiple_of(jnp.maximum(k_prev - SB_TK, 0), SB_TQ)


def _sb_rows(q0):
    return pl.ds(pl.multiple_of(2 * q0, 2 * SB_TQ), 2 * SB_TQ)


def sb_attn_fwd(proj, tri2, *, name, ride=()):
    s = proj.shape[0]
    nq = s // SB_TQ
    n_pairs = SB_WIDTH // LANES

    def body(q_ref, k_ref, v_ref, tri_ref, o_ref, c_all):
        def block(qh, k0, tk, live, c):
            z = _dot_nt(qh, k_ref[pl.ds(k0, tk), :])
            sp = _softplus(z)
            tail, total = _suffix_sums(jnp.where(live, -sp, 0.0), tri_ref[...])
            w = jnp.where(live, jnp.exp(z - sp + tail + c), 0.0)
            return _dot(w.astype(BF16), v_ref[pl.ds(k0, tk), :]), c + total

        def load_q(q0):
            return _stack_heads(q_ref[pl.ds(q0, SB_TQ), :]) * SCALE

        def first(qb, carry):
            q0 = pl.multiple_of(qb * SB_TQ, SB_TQ)
            tpos, col = _sb_positions(q0, SB_TK1)
            k0 = _sb_first_key(q0)
            acc, c = block(load_q(q0), k0, SB_TK1, k0 + col < tpos, jnp.zeros((2 * SB_TQ, 1), F32))
            o_ref[pl.ds(q0, SB_TQ), :] = _unstack_heads(acc, SB_TQ)
            c_all[_sb_rows(q0), :] = jnp.broadcast_to(jnp.where(k0 > 0, c, NEG), (2 * SB_TQ, LANES))
            return carry

        lax.fori_loop(0, nq, first, 0, unroll=2)

        @pl.when(jnp.max(c_all[...]) > SB_CUTOFF)
        def _():
            def more(qb, carry):
                q0 = pl.multiple_of(qb * SB_TQ, SB_TQ)
                c0 = c_all[_sb_rows(q0), 0:1]

                @pl.when(jnp.max(c0) > SB_CUTOFF)
                def _():
                    qh = load_q(q0)
                    _, col = _sb_positions(q0, SB_TK)

                    def cond(st):
                        return jnp.logical_and(st[0] > 0, st[3] > SB_CUTOFF)

                    def step(st):
                        k_prev, c, acc, _ = st
                        k0 = _sb_next_key(k_prev)
                        part, c = block(qh, k0, SB_TK, k0 + col < k_prev, c)
                        return k0, c, acc + part, jnp.max(c)

                    st = lax.while_loop(cond, step, (_sb_first_key(q0), c0, jnp.zeros((2 * SB_TQ, LANES), F32),
                                                     jnp.max(c0)))
                    o_ref[pl.ds(q0, SB_TQ), :] += _unstack_heads(st[2], SB_TQ)

                return carry

            lax.fori_loop(0, nq, more, 0)

    def col_spec(j):
        return pl.BlockSpec((s, LANES), lambda p: (0, j * n_pairs + p))

    (o,), rides = _call(
        body, name=name, grid=(n_pairs,), out_shape=(jax.ShapeDtypeStruct((s, SB_WIDTH), F32),),
        in_specs=[col_spec(0), col_spec(1), col_spec(2), pl.BlockSpec((2 * LANES, LANES), lambda p: (0, 0))],
        out_specs=(pl.BlockSpec((s, LANES), lambda p: (0, p)),), scratch_shapes=[pltpu.VMEM((2 * s, LANES), F32)],
        args=(proj, proj, proj, tri2), ride=ride)
    return o, rides


def sb_attn_bwd(proj, tri2, o, do, *, name, ride=()):
    s = proj.shape[0]
    nq = s // SB_TQ
    n_pairs = SB_WIDTH // LANES

    def body(q_ref, k_ref, v_ref, tri_ref, o_ref, do_ref, dq_ref, dk_ref, dv_ref, dq_acc, dk_acc, dv_acc, c_all, e_all):
        dk_acc[...] = jnp.zeros_like(dk_acc)
        dv_acc[...] = jnp.zeros_like(dv_acc)

        def load(q0):
            qh = _stack_heads(q_ref[pl.ds(q0, SB_TQ), :]) * SCALE
            doh_b = _stack_heads(do_ref[pl.ds(q0, SB_TQ), :].astype(BF16))
            ov = o_ref[pl.ds(q0, SB_TQ), :]
            dd = jnp.sum(doh_b.astype(F32) * jnp.concatenate([ov, ov], axis=0), axis=1, keepdims=True)
            return qh, doh_b, dd

        def block(qh, doh_b, dd, k0, tk, live, c, ce):
            kt = k_ref[pl.ds(k0, tk), :]
            z = _dot_nt(qh, kt)
            sp = _softplus(z)
            lb = z - sp
            tail, total = _suffix_sums(jnp.where(live, -sp, 0.0), tri_ref[...])
            wb = jnp.where(live, jnp.exp(lb + tail + c), 0.0).astype(BF16)
            e = wb.astype(F32) * _dot_nt(doh_b, v_ref[pl.ds(k0, tk), :])
            e_tail, e_total = _suffix_sums(e, tri_ref[...])
            dz = jnp.where(live, e - jnp.exp(lb) * (dd - ce - e_tail), 0.0)
            dzb = dz.astype(BF16)
            dk_acc[pl.ds(k0, tk), :] += _dot_tn(dzb, qh)
            dv_acc[pl.ds(k0, tk), :] += _dot_tn(wb, doh_b)
            return _dot(dzb, kt), c + total, ce + e_total

        def first(qb, carry):
            q0 = pl.multiple_of(qb * SB_TQ, SB_TQ)
            qh, doh_b, dd = load(q0)
            tpos, col = _sb_positions(q0, SB_TK1)
            k0 = _sb_first_key(q0)
            zero = jnp.zeros((2 * SB_TQ, 1), F32)
            dq, c, ce = block(qh, doh_b, dd, k0, SB_TK1, k0 + col < tpos, zero, zero)
            dq_acc[pl.ds(q0, SB_TQ), :] = _unstack_heads(dq, SB_TQ)
            c_all[_sb_rows(q0), :] = jnp.broadcast_to(jnp.where(k0 > 0, c, NEG), (2 * SB_TQ, LANES))
            e_all[_sb_rows(q0), :] = jnp.broadcast_to(ce, (2 * SB_TQ, LANES))
            return carry

        lax.fori_loop(0, nq, first, 0, unroll=2)

        @pl.when(jnp.max(c_all[...]) > SB_CUTOFF)
        def _():
            def more(qb, carry):
                q0 = pl.multiple_of(qb * SB_TQ, SB_TQ)
                c0 = c_all[_sb_rows(q0), 0:1]

                @pl.when(jnp.max(c0) > SB_CUTOFF)
                def _():
                    qh, doh_b, dd = load(q0)
                    _, col = _sb_positions(q0, SB_TK)

                    def cond(st):
                        return jnp.logical_and(st[0] > 0, st[4] > SB_CUTOFF)

                    def step(st):
                        k_prev, c, ce, dq, _ = st
                        k0 = _sb_next_key(k_prev)
                        part, c, ce = block(qh, doh_b, dd, k0, SB_TK, k0 + col < k_prev, c, ce)
                        return k0, c, ce, dq + part, jnp.max(c)

                    st = lax.while_loop(cond, step, (_sb_first_key(q0), c0, e_all[_sb_rows(q0), 0:1],
                                                     jnp.zeros((2 * SB_TQ, LANES), F32), jnp.max(c0)))
                    dq_acc[pl.ds(q0, SB_TQ), :] += _unstack_heads(st[3], SB_TQ)

                return carry

            lax.fori_loop(0, nq, more, 0)

        dq_ref[...] = (dq_acc[...] * SCALE).astype(BF16)
        dk_ref[...] = dk_acc[...].astype(BF16)
        dv_ref[...] = dv_acc[...].astype(BF16)

    def col_spec(j):
        return pl.BlockSpec((s, LANES), lambda p: (0, j * n_pairs + p))

    pair = pl.BlockSpec((s, LANES), lambda p: (0, p))
    (dq, dk, dv), rides = _call(
        body, name=name, grid=(n_pairs,), out_shape=(jax.ShapeDtypeStruct((s, SB_WIDTH), BF16),) * 3,
        in_specs=[col_spec(0), col_spec(1), col_spec(2), pl.BlockSpec((2 * LANES, LANES), lambda p: (0, 0)), pair, pair],
        out_specs=(pair, pair, pair),
        scratch_shapes=[pltpu.VMEM((s, LANES), F32)] * 3 + [pltpu.VMEM((2 * s, LANES), F32)] * 2,
        args=(proj, proj, proj, tri2, o, do), ride=ride)
    return dq, dk, dv, rides


def _lane_lo():
    return lax.broadcasted_iota(jnp.int32, (1, LANES), 1) < HEAD_DIM


def _swap_halves(x):
    return pltpu.roll(x, HEAD_DIM, 1)


def _rot_half(y):
    first = (lax.broadcasted_iota(jnp.int32, (1, LANES), 1) % HEAD_DIM) < (HEAD_DIM // 2)
    return jnp.where(first, pltpu.roll(y, LANES - HEAD_DIM // 2, 1), pltpu.roll(y, HEAD_DIM // 2, 1))


def _head_mean(v):
    lo = _lane_lo()
    s0 = jnp.sum(jnp.where(lo, v, 0.0), axis=1, keepdims=True)
    s1 = jnp.sum(jnp.where(lo, 0.0, v), axis=1, keepdims=True)
    return jnp.where(lo, s0, s1) * (1.0 / HEAD_DIM)


def swa_prep_fwd(proj, cos_p, sin_p, gq, gk, *, name):
    s = proj.shape[0]
    tm = min(512, s)
    q_blk = (3 * SB_WIDTH) // SWA_Q_WIDTH
    k_blk = (3 * SB_WIDTH + SWA_Q_WIDTH) // LANES

    def norm_rope(xv, g, cosv, sinv):
        y = (xv * lax.rsqrt(_head_mean(xv * xv) + NORM_EPS)) * g
        return y * cosv + _rot_half(y) * sinv

    def body(q_ref, k_ref, cos_ref, sin_ref, gq_ref, gk_ref, qn_ref, kn_ref):
        cosv, sinv = cos_ref[...], sin_ref[...]
        for j in range(SWA_Q_WIDTH // LANES):
            sl = slice(j * LANES, (j + 1) * LANES)
            qn_ref[:, sl] = norm_rope(q_ref[:, sl].astype(F32), gq_ref[...], cosv, sinv).astype(BF16)
        kn_ref[...] = norm_rope(k_ref[...].astype(F32), gk_ref[...], cosv, sinv).astype(BF16)

    row = lambda i: (i, 0)
    fixed = lambda i: (0, 0)
    return pl.pallas_call(
        body, name=name, grid=(s // tm,),
        out_shape=(jax.ShapeDtypeStruct((s, SWA_Q_WIDTH), BF16), jax.ShapeDtypeStruct((s, LANES), BF16)),
        in_specs=[pl.BlockSpec((tm, SWA_Q_WIDTH), lambda i: (i, q_blk)), pl.BlockSpec((tm, LANES), lambda i: (i, k_blk)),
                  pl.BlockSpec((tm, LANES), row), pl.BlockSpec((tm, LANES), row),
                  pl.BlockSpec((1, LANES), fixed), pl.BlockSpec((1, LANES), fixed)],
        out_specs=(pl.BlockSpec((tm, SWA_Q_WIDTH), row), pl.BlockSpec((tm, LANES), row)),
        compiler_params=_params("parallel"),
    )(proj, proj, cos_p, sin_p, gq, gk)


def swa_prep_bwd(proj, cos_p, sin_p, gq, gk, dqn, dkn, dv, *, name):
    s = proj.shape[0]
    tm = min(512, s)
    q_blk = (3 * SB_WIDTH) // SWA_Q_WIDTH
    k_blk = (3 * SB_WIDTH + SWA_Q_WIDTH) // LANES

    def bwd(xv, g, cosv, sinv, dout):
        dy = dout * cosv + _rot_half(dout * sinv)
        r = lax.rsqrt(_head_mean(xv * xv) + NORM_EPS)
        dyg = dy * g
        dx = r * dyg - xv * ((r * r * r) * _head_mean(dyg * xv))
        return dx, jnp.sum(dy * (xv * r), axis=0, keepdims=True)

    def body(q_ref, k_ref, cos_ref, sin_ref, gq_ref, gk_ref, dqn_ref, dkn_ref, dv_ref, dq_ref, dk_ref, dvb_ref,
             dgq_ref, dgk_ref):
        @pl.when(pl.program_id(0) == 0)
        def _():
            dgq_ref[...] = jnp.zeros_like(dgq_ref)
            dgk_ref[...] = jnp.zeros_like(dgk_ref)

        cosv, sinv = cos_ref[...], sin_ref[...]
        for j in range(SWA_Q_WIDTH // LANES):
            sl = slice(j * LANES, (j + 1) * LANES)
            dx, dg = bwd(q_ref[:, sl].astype(F32), gq_ref[...], cosv, sinv, dqn_ref[:, sl])
            dq_ref[:, sl] = dx.astype(BF16)
            dgq_ref[:, sl] += dg
        dx, dg = bwd(k_ref[...].astype(F32), gk_ref[...], cosv, sinv, dkn_ref[...])
        dk_ref[...] = dx.astype(BF16)
        dgk_ref[...] += dg
        dvb_ref[...] = dv_ref[...].astype(BF16)

    row = lambda i: (i, 0)
    fixed = lambda i: (0, 0)
    lane_row = pl.BlockSpec((tm, LANES), row)
    return pl.pallas_call(
        body, name=name, grid=(s // tm,),
        out_shape=(jax.ShapeDtypeStruct((s, SWA_Q_WIDTH), BF16), jax.ShapeDtypeStruct((s, LANES), BF16),
                   jax.ShapeDtypeStruct((s, LANES), BF16),
                   jax.ShapeDtypeStruct((1, SWA_Q_WIDTH), F32), jax.ShapeDtypeStruct((1, LANES), F32)),
        in_specs=[pl.BlockSpec((tm, SWA_Q_WIDTH), lambda i: (i, q_blk)), pl.BlockSpec((tm, LANES), lambda i: (i, k_blk)),
                  lane_row, lane_row, pl.BlockSpec((1, LANES), fixed), pl.BlockSpec((1, LANES), fixed),
                  pl.BlockSpec((tm, SWA_Q_WIDTH), row), lane_row, lane_row],
        out_specs=(pl.BlockSpec((tm, SWA_Q_WIDTH), row), lane_row, lane_row,
                   pl.BlockSpec((1, SWA_Q_WIDTH), fixed), pl.BlockSpec((1, LANES), fixed)),
        compiler_params=_params("arbitrary"),
    )(proj, proj, cos_p, sin_p, gq, gk, dqn, dkn, dv)


def _swa_kv_copies(k_ref, v_ref, kg_ref, vg_ref, second_kv):
    s = k_ref.shape[0]
    rows = min(512, s)
    keep = jnp.logical_xor(_lane_lo(), second_kv)

    def chunk(r, carry):
        sl = pl.ds(pl.multiple_of(r * rows, rows), rows)
        for src, dst in ((k_ref, kg_ref), (v_ref, vg_ref)):
            f = src[sl, :].astype(F32)
            dst[sl, :] = jnp.where(keep, f, _swap_halves(f)).astype(BF16)
        return carry

    lax.fori_loop(0, s // rows, chunk, 0)


def _swa_tile(i, kg_ref, vg_ref):
    q0 = pl.multiple_of(i * SWA_TQ, SWA_TQ)
    k0 = pl.multiple_of(jnp.maximum(i - 1, 0) * SWA_TQ, SWA_TQ)
    kg = kg_ref[pl.ds(k0, SWA_TK), :]
    vg = vg_ref[pl.ds(k0, SWA_TK), :]
    row = lax.broadcasted_iota(jnp.int32, (2 * SWA_TQ, SWA_TK), 0)
    tpos = q0 + jnp.where(row >= SWA_TQ, row - SWA_TQ, row)
    spos = k0 + lax.broadcasted_iota(jnp.int32, (2 * SWA_TQ, SWA_TK), 1)
    valid = jnp.logical_and(spos <= tpos, spos > tpos - WINDOW)
    return q0, k0, kg, vg, valid


def _swa_probs(qh, kg, valid, sink):
    z = jnp.where(valid, _dot_nt(qh, kg) * SCALE, NEG)
    m = jnp.maximum(jnp.max(z, axis=1, keepdims=True), sink)
    pexp = jnp.exp(z - m)
    psink = jnp.exp(sink - m)
    inv = 1.0 / (jnp.sum(pexp, axis=1, keepdims=True) + psink)
    return pexp * inv, psink * inv


def _stacked_sink(sink_row):
    s0 = jnp.sum(jnp.where(_head_mask(0), sink_row, 0.0), axis=1, keepdims=True) * (1.0 / HEAD_DIM)
    s1 = jnp.sum(jnp.where(_head_mask(1), sink_row, 0.0), axis=1, keepdims=True) * (1.0 / HEAD_DIM)
    top = lax.broadcasted_iota(jnp.int32, (2 * SWA_TQ, 1), 0) < SWA_TQ
    return jnp.where(top, s0, s1)


def swa_attn_fwd(qn, kn, proj, sink_p, *, name, ride=()):
    s = qn.shape[0]
    nq = s // SWA_TQ
    n_pairs = SWA_Q_WIDTH // LANES
    v_blk = (3 * SB_WIDTH + SWA_Q_WIDTH + SWA_KV_WIDTH) // LANES

    def body(q_ref, k_ref, v_ref, s_ref, o_ref, kg_ref, vg_ref):
        _swa_kv_copies(k_ref, v_ref, kg_ref, vg_ref, (pl.program_id(0) // 2) == 1)
        sink = _stacked_sink(s_ref[...])

        def tile(i, carry):
            q0, _, kg, vg, valid = _swa_tile(i, kg_ref, vg_ref)
            probs, _ = _swa_probs(_stack_heads(q_ref[pl.ds(q0, SWA_TQ), :]), kg, valid, sink)
            o_ref[pl.ds(q0, SWA_TQ), :] = _unstack_heads(_dot(probs.astype(BF16), vg), SWA_TQ)
            return carry

        lax.fori_loop(0, nq, tile, 0, unroll=4)

    pair = pl.BlockSpec((s, LANES), lambda p: (0, p))
    whole = pl.BlockSpec((s, LANES), lambda p: (0, 0))
    (o,), rides = _call(
        body, name=name, grid=(n_pairs,), out_shape=(jax.ShapeDtypeStruct((s, SWA_Q_WIDTH), F32),),
        in_specs=[pair, whole, pl.BlockSpec((s, LANES), lambda p: (0, v_blk)),
                  pl.BlockSpec((None, 1, LANES), lambda p: (p, 0, 0))],
        out_specs=(pair,), scratch_shapes=[pltpu.VMEM((s, LANES), BF16)] * 2, args=(qn, kn, proj, sink_p), ride=ride)
    return o, rides


def swa_attn_bwd(qn, kn, proj, sink_p, o, do, *, name, ride=()):
    s = qn.shape[0]
    nq = s // SWA_TQ
    n_pairs = SWA_Q_WIDTH // LANES
    v_blk = (3 * SB_WIDTH + SWA_Q_WIDTH + SWA_KV_WIDTH) // LANES
    fold_rows = min(512, s)

    def body(q_ref, k_ref, v_ref, s_ref, o_ref, do_ref, dq_ref, dk_ref, dv_ref, ds_ref, acc_k, acc_v, kg_ref, vg_ref):
        p = pl.program_id(0)
        _swa_kv_copies(k_ref, v_ref, kg_ref, vg_ref, (p // 2) == 1)
        sink = _stacked_sink(s_ref[...])

        @pl.when(p % 2 == 0)
        def _():
            acc_k[...] = jnp.zeros_like(acc_k)
            acc_v[...] = jnp.zeros_like(acc_v)

        def tile(i, dsink):
            q0, k0, kg, vg, valid = _swa_tile(i, kg_ref, vg_ref)
            qh = _stack_heads(q_ref[pl.ds(q0, SWA_TQ), :])
            doh = _stack_heads(do_ref[pl.ds(q0, SWA_TQ), :])
            doh_b = doh.astype(BF16)
            ov = o_ref[pl.ds(q0, SWA_TQ), :]
            delta = jnp.sum(doh * jnp.concatenate([ov, ov], axis=0), axis=1, keepdims=True)
            probs, psink = _swa_probs(qh, kg, valid, sink)
            dz = probs * (_dot_nt(doh_b, vg) - delta)
            dzb = (dz * SCALE).astype(BF16)
            dq_ref[pl.ds(q0, SWA_TQ), :] = _unstack_heads(_dot(dzb, kg), SWA_TQ)
            acc_k[pl.ds(k0, SWA_TK), :] += _dot_tn(dzb, qh)
            acc_v[pl.ds(k0, SWA_TK), :] += _dot_tn(probs.astype(BF16), doh_b)
            pd = psink * delta
            return dsink - jnp.where(_head_mask(0), jnp.sum(pd[:SWA_TQ], axis=0, keepdims=True),
                                     jnp.sum(pd[SWA_TQ:], axis=0, keepdims=True))

        ds_ref[...] = lax.fori_loop(0, nq, tile, jnp.zeros((1, LANES), F32), unroll=4)

        def fold_into(first_head):
            def fold(r, carry):
                rows = pl.ds(pl.multiple_of(r * fold_rows, fold_rows), fold_rows)
                for acc, out in ((acc_k, dk_ref), (acc_v, dv_ref)):
                    a = acc[rows, :]
                    both = a + _swap_halves(a)
                    if first_head:
                        out[rows, :] = jnp.where(_lane_lo(), both, 0.0)
                    else:
                        out[rows, :] = jnp.where(_lane_lo(), out[rows, :], both)
                return carry

            lax.fori_loop(0, s // fold_rows, fold, 0)

        @pl.when(p == 1)
        def _():
            fold_into(True)

        @pl.when(p == 3)
        def _():
            fold_into(False)

    pair = pl.BlockSpec((s, LANES), lambda p: (0, p))
    whole = pl.BlockSpec((s, LANES), lambda p: (0, 0))
    sink_spec = pl.BlockSpec((None, 1, LANES), lambda p: (p, 0, 0))
    (dq, dk, dv, dsink), rides = _call(
        body, name=name, grid=(n_pairs,),
        out_shape=(jax.ShapeDtypeStruct((s, SWA_Q_WIDTH), F32), jax.ShapeDtypeStruct((s, LANES), F32),
                   jax.ShapeDtypeStruct((s, LANES), F32), jax.ShapeDtypeStruct((n_pairs, 1, LANES), F32)),
        in_specs=[pair, whole, pl.BlockSpec((s, LANES), lambda p: (0, v_blk)), sink_spec, pair, pair],
        out_specs=(pair, whole, whole, sink_spec),
        scratch_shapes=[pltpu.VMEM((s, LANES), F32)] * 2 + [pltpu.VMEM((s, LANES), BF16)] * 2,
        args=(qn, kn, proj, sink_p, o, do), ride=ride)
    return dq, dk, dv, dsink, rides


def _rope_tables(s):
    inv_freq = 1.0 / (ROPE_THETA ** (jnp.arange(0, HEAD_DIM, 2, dtype=F32) / HEAD_DIM))
    ang = jnp.arange(s, dtype=F32)[:, None] * inv_freq[None, :]
    cos, sin = jnp.cos(ang), jnp.sin(ang)
    cos_p = jnp.tile(jnp.concatenate([cos, cos], axis=1), (1, LANES // HEAD_DIM))
    sin_p = jnp.tile(jnp.concatenate([-sin, sin], axis=1), (1, LANES // HEAD_DIM))
    return cos_p, sin_p


def _lane_tile(v, reps):
    return jnp.tile(v.reshape(1, -1), (1, reps))


def _natural(stack, w):
    n, r, c = stack.shape
    if MATRIX_NAMES[w] in ROW_SHARDED:
        return stack.reshape(n * r, c)
    if w == W_UP:
        return stack
    return jnp.transpose(stack, (1, 0, 2)).reshape(r, n * c)


def _pack_small(tree):
    flat = jnp.concatenate([tree[n].reshape(-1) for n in SMALL_NAMES])
    rows = -(-flat.shape[0] // (8 * LANES)) * 8
    return jnp.pad(flat, (0, rows * LANES - flat.shape[0])).reshape(rows, LANES)


def _unpack_small(packed, shapes):
    flat, out, off = packed.reshape(-1), {}, 0
    for n in SMALL_NAMES:
        size = shapes[n][0] * shapes[n][1]
        out[n] = flat[off:off + size].reshape(shapes[n])
        off += size
    return out


def train_step(x, target, weights, mom_m, mom_v):
    s = x.shape[0]
    cos_p, sin_p = _rope_tables(s)
    tri = (jnp.arange(LANES)[:, None] > jnp.arange(LANES)[None, :]).astype(BF16)
    tri = jnp.concatenate([tri, tri], axis=0)
    shards = [[weights[n][l].astype(BF16) for n in MATRIX_NAMES] for l in range(DEPTH)]
    core = lax.axis_index("c").astype(jnp.int32).reshape(1)
    chip = (2 * lax.axis_index("x") + lax.axis_index("y")).astype(jnp.int32).reshape(1)

    def gather(l, ws):
        return GatherJob([shards[l][w] for w in ws])

    w_in = _natural(exchange_alone(gather(0, [W_IN]), name="gather_w_in0")[0], W_IN)
    saved = []
    for l in range(DEPTH):
        g_mix = weights["mix_norm_g"][l].reshape(1, D_MODEL)
        g_mlp = weights["mlp_norm_g"][l].reshape(1, D_MODEL)
        gq = _lane_tile(weights["q_norm_g"][l], LANES // HEAD_DIM)
        gk = _lane_tile(weights["k_norm_g"][l], LANES // HEAD_DIM)
        sink_p = jnp.repeat(weights["sinks"][l].reshape(SWA_Q_WIDTH // LANES, 2), HEAD_DIM, axis=1)
        sink_p = sink_p.reshape(SWA_Q_WIDTH // LANES, 1, LANES)
        (h, proj, gates), ((s_bsb, s_bsw, s_out),) = norm_matmul(
            x, g_mix, w_in, gate_split=ATTN_WIDTH, name="in_proj", ride=[gather(l, [W_BSB, W_BSW, W_OUT])])
        if l + 1 < DEPTH:
            o_sb, ((s_up,), (s_in,)) = sb_attn_fwd(proj, tri, name="sb_fwd",
                                                   ride=[gather(l, [W_UP]), gather(l + 1, [W_IN])])
        else:
            o_sb, ((s_up,),) = sb_attn_fwd(proj, tri, name="sb_fwd_last", ride=[gather(l, [W_UP])])
        qn, kn = swa_prep_fwd(proj, cos_p, sin_p, gq, gk, name="swa_prep")
        o_sw, ((s_down,),) = swa_attn_fwd(qn, kn, proj, sink_p, name="swa_fwd", ride=[gather(l, [W_DOWN])])
        mats = [w_in, _natural(s_bsb, W_BSB), _natural(s_bsw, W_BSW), _natural(s_out, W_OUT), _natural(s_up, W_UP),
                _natural(s_down, W_DOWN)]
        x1, y_sb, y_sw, merged = merge_out_fwd(x, o_sb, o_sw, gates, mats[W_BSB], mats[W_BSW], mats[W_OUT],
                                               name="merge_out")
        (h2, u), _ = norm_matmul(x1, g_mlp, mats[W_UP], gate_split=None, name="mlp_up")
        x2 = mlp_down_fwd(x1, u, mats[W_DOWN], name="mlp_down")
        if l + 1 < DEPTH:
            w_in = _natural(s_in, W_IN)
        saved.append(dict(x=x, h=h, proj=proj, gates=gates, o_sb=o_sb, qn=qn, kn=kn, o_sw=o_sw, y_sb=y_sb, y_sw=y_sw,
                          merged=merged, x1=x1, h2=h2, u=u, g_mix=g_mix, g_mlp=g_mlp, gq=gq, gk=gk, sink_p=sink_p,
                          mats=mats))
        x = x2

    dx, dxb, loss = loss_head(x, target, name="loss_head")

    shard_shapes = [weights[n].shape[1:] for n in MATRIX_NAMES]
    parts = [lax.empty((DEPTH, N_CHIPS) + sh, BF16) for sh in shard_shapes]
    lands = [lax.empty((DEPTH, 3) + sh, BF16) for sh in shard_shapes]
    small_grads = {n: [None] * DEPTH for n in SMALL_NAMES}
    half = D_MODEL // 2

    def summed(l, ws, grads, landed):
        new = pair_sum(l, grads, landed, [parts[w] for w in ws], core, name="grad_pair_sum")
        for w, p in zip(ws, new):
            parts[w] = p

    def chip_job(items):
        return ChipJob(items, parts, lands)

    def landed_chip(job, outs):
        for w, a in zip(job.ws, outs):
            lands[w] = a

    in_pending = None
    for l in reversed(range(DEPTH)):
        a = saved[l]
        mats = a["mats"]
        (du,), _ = mlp_bwd_up(dxb, a["u"], mats[W_DOWN], name="mlp_bwd_up")
        dw_down = matmul_tn(a["u"], [dxb], a_block=half, out_cols=None, relu2=True, name="dw_down")
        dw_up = matmul_tn(a["h2"], [du], a_block=half, out_cols=du.shape[1] // N_DEV, relu2=False, name="dw_up")
        g_mlp_w = [dw_up, dw_down.reshape((N_DEV,) + shard_shapes[W_DOWN])]
        (dx1, dx1b, dg_mlp), (landed,) = matmul_nt_norm_bwd([du], mats[W_UP], a["x1"], a["g_mlp"], dx,
                                                            name="mlp_bwd_norm", ride=[PairJob(g_mlp_w)])
        summed(l, [W_UP, W_DOWN], g_mlp_w, landed)
        small_grads["mlp_norm_g"][l] = dg_mlp.reshape(D_MODEL)
        dw_out = matmul_tn(a["merged"], [dx1b], a_block=half, out_cols=None, relu2=False, name="dw_out")
        dy_sb, dy_sw, do_sb, do_sw, dgl = out_bwd(dx1b, mats[W_OUT], a["gates"], a["y_sb"], a["y_sw"],
                                                  mats[W_BSB], mats[W_BSW], name="out_bwd")
        dw_bsb = matmul_tn(a["o_sb"], [dy_sb], a_block=half, out_cols=D_MODEL // N_DEV, relu2=False, name="dw_branch_sb")
        dw_bsw = matmul_tn(a["o_sw"], [dy_sw], a_block=half, out_cols=D_MODEL // N_DEV, relu2=False, name="dw_branch_swa")
        g_mix_w = [dw_bsb, dw_bsw, dw_out.reshape((N_DEV,) + shard_shapes[W_OUT])]
        job = chip_job([(l, W_UP), (l, W_DOWN)])
        dq_sb, dk_sb, dv_sb, (outs, landed) = sb_attn_bwd(a["proj"], tri, a["o_sb"], do_sb, name="sb_bwd",
                                                         ride=[job, PairJob(g_mix_w)])
        landed_chip(job, outs)
        summed(l, [W_BSB, W_BSW, W_OUT], g_mix_w, landed)
        job = chip_job([(l, W_BSB), (l, W_BSW), (l, W_OUT)] + ([(in_pending, W_IN)] if in_pending is not None else []))
        dqn, dkn, dv_sw, dsink, (outs,) = swa_attn_bwd(a["qn"], a["kn"], a["proj"], a["sink_p"], a["o_sw"], do_sw,
                                                      name="swa_bwd", ride=[job])
        landed_chip(job, outs)
        dq_sw, dk_sw, dv_swb, dgq, dgk = swa_prep_bwd(a["proj"], cos_p, sin_p, a["gq"], a["gk"], dqn, dkn, dv_sw,
                                                      name="swa_prep_bwd")
        small_grads["q_norm_g"][l] = dgq.reshape(SWA_Q_WIDTH // HEAD_DIM, HEAD_DIM).sum(0)
        small_grads["k_norm_g"][l] = dgk.reshape(LANES // HEAD_DIM, HEAD_DIM).sum(0)
        small_grads["sinks"][l] = dsink[:, 0, ::HEAD_DIM].reshape(SWA_Q_WIDTH // HEAD_DIM)
        pieces = [dq_sb, dk_sb, dv_sb, dq_sw, dk_sw, dv_swb, dgl]
        g_in = [matmul_tn(a["h"], pieces, a_block=half, out_cols=IN_WIDTH // N_DEV, relu2=False, name="dw_in")]
        (dx, dxb, dg_mix), (landed,) = matmul_nt_norm_bwd(pieces, mats[W_IN], a["x"], a["g_mix"], dx1,
                                                         name="in_proj_bwd", ride=[PairJob(g_in)])
        summed(l, [W_IN], g_in, landed)
        small_grads["mix_norm_g"][l] = dg_mix.reshape(D_MODEL)
        in_pending = l
    job = chip_job([(in_pending, W_IN)])
    landed_chip(job, exchange_alone(job, name="grad_chip_exchange_in0"))

    out_g, out_d, out_m, out_v = {}, {}, {}, {}
    for i, n in enumerate(MATRIX_NAMES):
        out_g[n], out_d[n], out_m[n], out_v[n] = reduce_adamw(parts[i], lands[i], chip, weights[n], mom_m[n], mom_v[n],
                                                              name="adamw_" + n)
    small_shapes = {n: weights[n].shape for n in SMALL_NAMES}
    small_all = gather_small(_pack_small({n: jnp.stack(v) for n, v in small_grads.items()}), name="gather_small_grads")
    sg, sd, sm, sv = small_adamw(small_all, _pack_small(weights), _pack_small(mom_m), _pack_small(mom_v),
                                 name="small_adamw")
    for tree, packed_small in ((out_g, sg), (out_d, sd), (out_m, sm), (out_v, sv)):
        tree.update(_unpack_small(packed_small, small_shapes))
    return loss, dx, (out_g, out_d, out_m, out_v)


def kernel(x, mix_norm_g, w_in, q_norm_g, k_norm_g, sinks, w_branch_sb, w_branch_swa, w_out, mlp_norm_g, w_up, w_down, loss_target, m_mix_norm_g, m_w_in, m_q_norm_g, m_k_norm_g, m_sinks, m_w_branch_sb, m_w_branch_swa, m_w_out, m_mlp_norm_g, m_w_up, m_w_down, v_mix_norm_g, v_w_in, v_q_norm_g, v_k_norm_g, v_sinks, v_w_branch_sb, v_w_branch_swa, v_w_out, v_mlp_norm_g, v_w_up, v_w_down):
    weights = dict(mix_norm_g=mix_norm_g, w_in=w_in, q_norm_g=q_norm_g, k_norm_g=k_norm_g, sinks=sinks,
                   w_branch_sb=w_branch_sb, w_branch_swa=w_branch_swa, w_out=w_out, mlp_norm_g=mlp_norm_g, w_up=w_up,
                   w_down=w_down)
    mom_m = dict(mix_norm_g=m_mix_norm_g, w_in=m_w_in, q_norm_g=m_q_norm_g, k_norm_g=m_k_norm_g, sinks=m_sinks,
                 w_branch_sb=m_w_branch_sb, w_branch_swa=m_w_branch_swa, w_out=m_w_out, mlp_norm_g=m_mlp_norm_g,
                 w_up=m_w_up, w_down=m_w_down)
    mom_v = dict(mix_norm_g=v_mix_norm_g, w_in=v_w_in, q_norm_g=v_q_norm_g, k_norm_g=v_k_norm_g, sinks=v_sinks,
                 w_branch_sb=v_w_branch_sb, w_branch_swa=v_w_branch_swa, w_out=v_w_out, mlp_norm_g=v_mlp_norm_g,
                 w_up=v_w_up, w_down=v_w_down)
    loss_part, grad_x, outs = train_step(x[0], loss_target[0], weights, mom_m, mom_v)
    loss = lax.psum(loss_part[0, 0], MESH_AXES)
    return (loss, grad_x[None], *[outs[0][n] for n in WEIGHT_ORDER], *[outs[1][n] for n in WEIGHT_ORDER],
            *[outs[2][n] for n in WEIGHT_ORDER], *[outs[3][n] for n in WEIGHT_ORDER])
```

```python
import functools
import math

import jax
import jax.numpy as jnp
from jax import lax
from jax.experimental import pallas as pl
from jax.experimental.pallas import tpu as pltpu

F32 = jnp.float32
BF16 = jnp.bfloat16

DEPTH = 4
D_MODEL = 1024
HEAD_DIM = 64
LANES = 128
WINDOW = 128
SB_WIDTH = 512
SWA_Q_WIDTH = 512
SWA_KV_WIDTH = 128
ATTN_WIDTH = 3 * SB_WIDTH + SWA_Q_WIDTH + 2 * SWA_KV_WIDTH
IN_WIDTH = ATTN_WIDTH + 2 * D_MODEL
ROPE_THETA = 10000.0
NORM_EPS = 1e-6
SCALE = HEAD_DIM ** -0.5
NEG = -1e30
N_DEV = 8
N_CHIPS = 4

ADAM_LR = 0.001
ADAM_B1 = 0.9
ADAM_B2 = 0.999
ADAM_EPS = 1e-08
ADAM_WD = 0.01
ADAM_STEP = 10

SB_TQ = 128
SB_TK1 = 384
SB_TK = 256
SB_CUTOFF = -88.0
SWA_TQ = 128
SWA_TK = 256
ROW_TILE = 512
VMEM_LIMIT = 56 * 1024 * 1024

MATRIX_NAMES = ("w_in", "w_branch_sb", "w_branch_swa", "w_out", "w_up", "w_down")
W_IN, W_BSB, W_BSW, W_OUT, W_UP, W_DOWN = range(6)
ROW_SHARDED = ("w_out", "w_down")
SMALL_NAMES = ("mix_norm_g", "q_norm_g", "k_norm_g", "sinks", "mlp_norm_g")
WEIGHT_ORDER = ("mix_norm_g", "w_in", "q_norm_g", "k_norm_g", "sinks", "w_branch_sb", "w_branch_swa", "w_out",
                "mlp_norm_g", "w_up", "w_down")
MESH_AXES = ("x", "y", "c")

ANY = pl.BlockSpec(memory_space=pl.ANY)
MESH = pl.DeviceIdType.MESH


def _params(*sem):
    return pltpu.CompilerParams(dimension_semantics=sem, vmem_limit_bytes=VMEM_LIMIT)


def _dot(a, b):
    return jnp.dot(a, b, preferred_element_type=F32)


def _dot_nt(a, b):
    return lax.dot_general(a, b, (((1,), (1,)), ((), ())), preferred_element_type=F32)


def _dot_tn(a, b):
    return lax.dot_general(a, b, (((0,), (0,)), ((), ())), preferred_element_type=F32)


def _split_bf16(x):
    hi = x.astype(BF16)
    lo = (x - hi.astype(F32)).astype(BF16)
    return hi, lo


def _rsqrt_ms(x):
    return lax.rsqrt(jnp.mean(x * x, axis=-1, keepdims=True) + NORM_EPS)


def _place():
    return lax.axis_index("x"), lax.axis_index("y"), lax.axis_index("c")


class _Gather:
    def __init__(self, x_refs, out_refs, send_sems, recv_sems, local_sems):
        self.x_refs, self.out_refs = x_refs, out_refs
        self.send_sems, self.recv_sems, self.local_sems = send_sems, recv_sems, local_sems
        self.n = len(x_refs)
        x, y, c = _place()
        self.c = c
        self.me, self.sibling = (x, y, c), (x, y, 1 - c)
        self.chips = [(1 - x, y), (x, 1 - y), (1 - x, 1 - y)]

    def _copy(self, k, w, blk, to, own=False):
        dst = self.out_refs[w].at[4 * blk[0] + 2 * blk[1] + blk[2]]
        return pltpu.make_async_remote_copy(
            src_ref=self.x_refs[w] if own else dst, dst_ref=dst, send_sem=self.send_sems.at[k, w],
            recv_sem=self.recv_sems.at[k, w], device_id=to, device_id_type=MESH)

    def _mine(self, w):
        me = self.me
        return pltpu.make_async_copy(self.x_refs[w], self.out_refs[w].at[4 * me[0] + 2 * me[1] + me[2]],
                                     self.local_sems.at[w])

    def _first(self, w):
        return [self._copy(0, w, self.me, self.sibling, own=True)] + [
            self._copy(1 + j, w, self.me, (*chip, self.c), own=True) for j, chip in enumerate(self.chips)]

    def _passed(self, j, w):
        return self._copy(4 + j, w, (*self.chips[j], self.c), self.sibling)

    def start(self):
        for w in range(self.n):
            self._mine(w).start()
            for cp in self._first(w):
                cp.start()

    def relay(self):
        for j, chip in enumerate(self.chips):
            for w in range(self.n):
                self._copy(1 + j, w, (*chip, self.c), self.me).wait_recv()
                self._passed(j, w).start()

    def finish(self):
        for w in range(self.n):
            self._copy(0, w, self.sibling, self.me).wait_recv()
            for j, chip in enumerate(self.chips):
                self._copy(4 + j, w, (*chip, 1 - self.c), self.me).wait_recv()
            for cp in self._first(w):
                cp.wait_send()
            for j in range(3):
                self._passed(j, w).wait_send()
            self._mine(w).wait()


class GatherJob:
    def __init__(self, shards):
        n = len(shards)
        self.inputs = list(shards)
        self.out_shapes = [jax.ShapeDtypeStruct((N_DEV,) + s.shape, s.dtype) for s in shards]
        self.aliases = {}
        self.scratch = [pltpu.SemaphoreType.DMA((7, n)), pltpu.SemaphoreType.DMA((7, n)),
                        pltpu.SemaphoreType.DMA((n,))]

    def bind(self, in_refs, out_refs, scratch_refs):
        return _Gather(in_refs, out_refs, *scratch_refs)


class _Copies:
    def __init__(self, copies):
        self.copies = copies

    def start(self):
        for cp in self.copies:
            cp.start()

    def relay(self):
        pass

    def finish(self):
        for cp in self.copies:
            cp.wait_recv()
        for cp in self.copies:
            cp.wait_send()


class ChipJob:
    def __init__(self, items, parts, lands):
        self.ws = sorted({w for _, w in items})
        n = len(self.ws)
        self.items = [(layer, self.ws.index(w)) for layer, w in items]
        self.inputs = [parts[w] for w in self.ws] + [lands[w] for w in self.ws]
        self.out_shapes = [jax.ShapeDtypeStruct(lands[w].shape, lands[w].dtype) for w in self.ws]
        self.aliases = {n + i: i for i in range(n)}
        self.scratch = [pltpu.SemaphoreType.DMA((3, n)), pltpu.SemaphoreType.DMA((3, n))]

    def bind(self, in_refs, out_refs, scratch_refs):
        send_sems, recv_sems = scratch_refs
        x, y, c = _place()
        chips = [(1 - x, y), (x, 1 - y), (1 - x, 1 - y)]
        return _Copies([pltpu.make_async_remote_copy(
            src_ref=in_refs[i].at[layer, 2 * px + py], dst_ref=out_refs[i].at[layer, j],
            send_sem=send_sems.at[j, i], recv_sem=recv_sems.at[j, i], device_id=(px, py, c), device_id_type=MESH)
            for layer, i in self.items for j, (px, py) in enumerate(chips)])


class PairJob:
    def __init__(self, grads):
        n = len(grads)
        self.inputs = list(grads)
        self.out_shapes = [jax.ShapeDtypeStruct((N_CHIPS,) + g.shape[1:], g.dtype) for g in grads]
        self.aliases = {}
        self.scratch = [pltpu.SemaphoreType.DMA((N_CHIPS, n)), pltpu.SemaphoreType.DMA((N_CHIPS, n))]

    def bind(self, in_refs, out_refs, scratch_refs):
        send_sems, recv_sems = scratch_refs
        x, y, c = _place()
        return _Copies([pltpu.make_async_remote_copy(
            src_ref=in_refs[w].at[2 * k + (1 - c)], dst_ref=out_refs[w].at[k], send_sem=send_sems.at[k, w],
            recv_sem=recv_sems.at[k, w], device_id=(x, y, 1 - c), device_id_type=MESH)
            for w in range(len(in_refs)) for k in range(N_CHIPS)])


def _call(body, *, name, grid, in_specs, out_specs, out_shape, args, scratch_shapes=(), ride=()):
    out_specs, out_shape, in_specs = tuple(out_specs), tuple(out_shape), list(in_specs)
    scratch_shapes = list(scratch_shapes)
    order = ("arbitrary",) * len(grid)
    if not ride:
        outs = pl.pallas_call(body, name=name, grid=grid, in_specs=in_specs, out_specs=out_specs, out_shape=out_shape,
                              scratch_shapes=scratch_shapes, compiler_params=_params(*order))(*args)
        return tuple(outs), []
    n_in, n_out, n_scr = len(in_specs), len(out_specs), len(scratch_shapes)
    n_steps = math.prod(grid)
    relay_early = n_steps >= 8
    relay_at = n_steps - n_steps // 4 if relay_early else n_steps - 1

    def split(refs, pos, counts):
        groups = []
        for k in counts:
            groups.append(refs[pos:pos + k])
            pos += k
        return groups, pos

    def wrapped(*refs):
        ins, pos = refs[:n_in], n_in
        job_in, pos = split(refs, pos, [len(j.inputs) for j in ride])
        outs, pos = refs[pos:pos + n_out], pos + n_out
        job_out, pos = split(refs, pos, [len(j.out_shapes) for j in ride])
        scr, pos = refs[pos:pos + n_scr], pos + n_scr
        job_scr, pos = split(refs, pos, [len(j.scratch) for j in ride])
        bound = [j.bind(i, o, s) for j, i, o, s in zip(ride, job_in, job_out, job_scr)]
        step = pl.program_id(0)
        for axis in range(1, len(grid)):
            step = step * grid[axis] + pl.program_id(axis)

        @pl.when(step == 0)
        def _():
            for b in bound:
                b.start()

        if relay_early:
            @pl.when(step == relay_at)
            def _():
                for b in bound:
                    b.relay()

        body(*ins, *outs, *scr)

        @pl.when(step == n_steps - 1)
        def _():
            if not relay_early:
                for b in bound:
                    b.relay()
            for b in bound:
                b.finish()

    aliases, in_pos, out_pos = {}, n_in, n_out
    for j in ride:
        aliases.update({in_pos + i: out_pos + o for i, o in j.aliases.items()})
        in_pos += len(j.inputs)
        out_pos += len(j.out_shapes)
    results = pl.pallas_call(
        wrapped, name=name, grid=grid, in_specs=in_specs + [ANY] * (in_pos - n_in),
        out_specs=out_specs + (ANY,) * (out_pos - n_out),
        out_shape=out_shape + tuple(s for j in ride for s in j.out_shapes),
        scratch_shapes=scratch_shapes + [s for j in ride for s in j.scratch], input_output_aliases=aliases,
        compiler_params=pltpu.CompilerParams(dimension_semantics=order, vmem_limit_bytes=VMEM_LIMIT,
                                             has_side_effects=True),
    )(*args, *[a for j in ride for a in j.inputs])
    job_results, pos = split(list(results), n_out, [len(j.out_shapes) for j in ride])
    return tuple(results[:n_out]), job_results


def exchange_alone(job, *, name):
    n_in, n_out = len(job.inputs), len(job.out_shapes)

    def body(*refs):
        b = job.bind(refs[:n_in], refs[n_in:n_in + n_out], refs[n_in + n_out:])
        b.start()
        b.relay()
        b.finish()

    return list(pl.pallas_call(
        body, name=name, out_shape=tuple(job.out_shapes), in_specs=[ANY] * n_in, out_specs=(ANY,) * n_out,
        scratch_shapes=job.scratch, input_output_aliases=job.aliases,
        compiler_params=pltpu.CompilerParams(has_side_effects=True),
    )(*job.inputs))


PAIR_SUM_CHUNKS = 2


def pair_sum(layer, grads, landed, parts, core, *, name):
    n = len(grads)

    def body(c_ref, *refs):
        g_refs, l_refs, o_refs = refs[:n], refs[n:2 * n], refs[3 * n:]
        for w in range(n):
            o_refs[w][...] = (g_refs[w][...].astype(F32) + l_refs[w][...].astype(F32)).astype(BF16)

    def blk(g):
        return (None, g.shape[1] // PAIR_SUM_CHUNKS, g.shape[2])

    in_specs = [pl.BlockSpec(blk(g), lambda k, i, c_ref: (2 * k + c_ref[0], i, 0)) for g in grads]
    in_specs += [pl.BlockSpec(blk(g), lambda k, i, c_ref: (k, i, 0)) for g in grads]
    in_specs += [ANY] * n
    out_specs = tuple(pl.BlockSpec((None,) + blk(g), lambda k, i, c_ref: (layer, k, i, 0)) for g in grads)
    return list(pl.pallas_call(
        body, name=name, out_shape=tuple(jax.ShapeDtypeStruct(p.shape, p.dtype) for p in parts),
        grid_spec=pltpu.PrefetchScalarGridSpec(num_scalar_prefetch=1, grid=(N_CHIPS, PAIR_SUM_CHUNKS),
                                               in_specs=in_specs, out_specs=out_specs),
        input_output_aliases={1 + 2 * n + w: w for w in range(n)},
        compiler_params=_params("parallel", "parallel"),
    )(core, *grads, *landed, *parts))


def _adamw(w, g, m, v):
    m = ADAM_B1 * m + (1.0 - ADAM_B1) * g
    v = ADAM_B2 * v + (1.0 - ADAM_B2) * (g * g)
    m_hat = m / (1.0 - ADAM_B1 ** ADAM_STEP)
    v_hat = v / (1.0 - ADAM_B2 ** ADAM_STEP)
    delta = -ADAM_LR * (m_hat / (jnp.sqrt(v_hat) + ADAM_EPS) + ADAM_WD * w)
    return delta, m, v


def reduce_adamw(part, land, chip, w, m, v, *, name):
    _, r, c = w.shape
    tr = 256 if r % 256 == 0 else (r // 2 if r > 256 else r)

    def body(k_ref, own_ref, l0_ref, l1_ref, l2_ref, w_ref, m_ref, v_ref, g_out, d_out, m_out, v_out):
        g = own_ref[...].astype(F32) + l0_ref[...].astype(F32) + l1_ref[...].astype(F32) + l2_ref[...].astype(F32)
        delta, m_new, v_new = _adamw(w_ref[...], g, m_ref[...], v_ref[...])
        g_out[...] = g
        d_out[...] = delta
        m_out[...] = m_new
        v_out[...] = v_new

    row = pl.BlockSpec((None, tr, c), lambda l, i, k_ref: (l, i, 0))

    def slot(j):
        return pl.BlockSpec((None, None, tr, c), lambda l, i, k_ref: (l, j, i, 0))

    return pl.pallas_call(
        body, name=name, out_shape=(jax.ShapeDtypeStruct(w.shape, F32),) * 4,
        grid_spec=pltpu.PrefetchScalarGridSpec(
            num_scalar_prefetch=1, grid=(DEPTH, r // tr),
            in_specs=[pl.BlockSpec((None, None, tr, c), lambda l, i, k_ref: (l, k_ref[0], i, 0)), slot(0), slot(1),
                      slot(2), row, row, row],
            out_specs=(row, row, row, row)),
        compiler_params=_params("parallel", "parallel"),
    )(chip, part, land, land, land, w, m, v)


def gather_small(block, *, name):
    def body(x_ref, out_ref, send_sems, recv_sems, local_sem):
        x, y, c = _place()
        me = 4 * x + 2 * y + c
        mine = pltpu.make_async_copy(x_ref, out_ref.at[me], local_sem)
        mine.start()
        peers = [(x ^ (k >> 2), y ^ ((k >> 1) & 1), c ^ (k & 1)) for k in range(1, N_DEV)]
        copies = [pltpu.make_async_remote_copy(
            src_ref=x_ref, dst_ref=out_ref.at[me], send_sem=send_sems.at[k], recv_sem=recv_sems.at[k],
            device_id=peer, device_id_type=MESH) for k, peer in enumerate(peers)]
        for cp in copies:
            cp.start()
        for k, (px, py, pc) in enumerate(peers):
            pltpu.make_async_remote_copy(
                src_ref=x_ref, dst_ref=out_ref.at[4 * px + 2 * py + pc], send_sem=send_sems.at[k],
                recv_sem=recv_sems.at[k], device_id=(px, py, pc), device_id_type=MESH).wait_recv()
        for cp in copies:
            cp.wait_send()
        mine.wait()

    return pl.pallas_call(
        body, name=name, out_shape=jax.ShapeDtypeStruct((N_DEV,) + block.shape, block.dtype),
        in_specs=[ANY], out_specs=ANY,
        scratch_shapes=[pltpu.SemaphoreType.DMA((7,)), pltpu.SemaphoreType.DMA((7,)), pltpu.SemaphoreType.DMA],
        compiler_params=pltpu.CompilerParams(has_side_effects=True),
    )(block)


def small_adamw(gathered, w, m, v, *, name):
    def body(g_ref, w_ref, m_ref, v_ref, g_out, d_out, m_out, v_out):
        g = g_ref[0]
        for d in range(1, N_DEV):
            g = g + g_ref[d]
        delta, m_new, v_new = _adamw(w_ref[...], g, m_ref[...], v_ref[...])
        g_out[...] = g
        d_out[...] = delta
        m_out[...] = m_new
        v_out[...] = v_new

    return pl.pallas_call(
        body, name=name, out_shape=(jax.ShapeDtypeStruct(w.shape, F32),) * 4,
    )(gathered, w, m, v)


def norm_matmul(x, g, w, *, gate_split, name, ride=()):
    s, d = x.shape
    tm = min(ROW_TILE, s)
    blocked = w.ndim == 3
    n = w.shape[0] if not blocked else w.shape[0] * w.shape[2]

    def body(x_ref, g_ref, w_ref, h_ref, *outs):
        xv = x_ref[...]
        h = ((xv * _rsqrt_ms(xv)) * g_ref[...]).astype(BF16)
        h_ref[...] = h
        if blocked:
            nb = w_ref.shape[2]
            for j in range(w_ref.shape[0]):
                outs[0][:, j * nb:(j + 1) * nb] = _dot(h, w_ref[j]).astype(BF16)
        else:
            p = _dot_nt(h, w_ref[...])
            outs[0][...] = p[:, :gate_split].astype(BF16)
            outs[1][...] = (1.0 / (1.0 + jnp.exp(-p[:, gate_split:]))).astype(BF16)

    row = lambda i: (i, 0)
    fixed = lambda i: (0, 0)
    if blocked:
        out_shape = (jax.ShapeDtypeStruct((s, d), BF16), jax.ShapeDtypeStruct((s, n), BF16))
        out_specs = (pl.BlockSpec((tm, d), row), pl.BlockSpec((tm, n), row))
        w_spec = pl.BlockSpec(w.shape, lambda i: (0, 0, 0))
    else:
        out_shape = (jax.ShapeDtypeStruct((s, d), BF16), jax.ShapeDtypeStruct((s, gate_split), BF16),
                     jax.ShapeDtypeStruct((s, n - gate_split), BF16))
        out_specs = (pl.BlockSpec((tm, d), row), pl.BlockSpec((tm, gate_split), row),
                     pl.BlockSpec((tm, n - gate_split), row))
        w_spec = pl.BlockSpec((n, d), fixed)
    return _call(body, name=name, grid=(s // tm,), out_shape=out_shape, out_specs=out_specs,
                 in_specs=[pl.BlockSpec((tm, d), row), pl.BlockSpec((1, d), fixed), w_spec], args=(x, g, w), ride=ride)


def merge_out_fwd(x, o_sb, o_sw, gates, w_bsb, w_bsw, w_o, *, name):
    s, d = x.shape
    tm = min(ROW_TILE, s)

    def body(x_ref, osb_ref, osw_ref, g_ref, wsb_ref, wsw_ref, wo_ref, x1_ref, ysb_ref, ysw_ref, mg_ref):
        y_sb = _dot(osb_ref[...].astype(BF16), wsb_ref[...])
        y_sw = _dot(osw_ref[...].astype(BF16), wsw_ref[...])
        g = g_ref[...].astype(F32)
        merged = (g[:, :d] * y_sb + g[:, d:] * y_sw).astype(BF16)
        ysb_ref[...] = y_sb.astype(BF16)
        ysw_ref[...] = y_sw.astype(BF16)
        mg_ref[...] = merged
        x1_ref[...] = x_ref[...] + _dot(merged, wo_ref[...])

    row = lambda i: (i, 0)
    fixed = lambda i: (0, 0)
    wd = o_sb.shape[1]
    return pl.pallas_call(
        body, name=name, grid=(s // tm,),
        out_shape=(jax.ShapeDtypeStruct((s, d), F32),) + (jax.ShapeDtypeStruct((s, d), BF16),) * 3,
        in_specs=[pl.BlockSpec((tm, d), row), pl.BlockSpec((tm, wd), row), pl.BlockSpec((tm, wd), row),
                  pl.BlockSpec((tm, 2 * d), row), pl.BlockSpec((wd, d), fixed), pl.BlockSpec((wd, d), fixed),
                  pl.BlockSpec((d, d), fixed)],
        out_specs=(pl.BlockSpec((tm, d), row),) * 4, compiler_params=_params("parallel"),
    )(x, o_sb, o_sw, gates, w_bsb, w_bsw, w_o)


def mlp_down_fwd(x1, u, w_down, *, name):
    s, d = x1.shape
    f = u.shape[1]
    tm = min(ROW_TILE, s)

    def body(x_ref, u_ref, w_ref, o_ref):
        a = jnp.maximum(u_ref[...].astype(F32), 0.0)
        o_ref[...] = x_ref[...] + _dot((a * a).astype(BF16), w_ref[...])

    row = lambda i: (i, 0)
    return pl.pallas_call(
        body, name=name, grid=(s // tm,), out_shape=jax.ShapeDtypeStruct((s, d), F32),
        in_specs=[pl.BlockSpec((tm, d), row), pl.BlockSpec((tm, f), row), pl.BlockSpec((f, d), lambda i: (0, 0))],
        out_specs=pl.BlockSpec((tm, d), row), compiler_params=_params("parallel"),
    )(x1, u, w_down)


def loss_head(y, target, *, name):
    s, d = y.shape
    tm = min(ROW_TILE, s)

    def body(y_ref, t_ref, dy_ref, dyb_ref, loss_ref):
        @pl.when(pl.program_id(0) == 0)
        def _():
            loss_ref[...] = jnp.zeros_like(loss_ref)

        e = y_ref[...] - t_ref[...]
        dy = e * (1.0 / d)
        dy_ref[...] = dy
        dyb_ref[...] = dy.astype(BF16)
        per_row = jnp.sum(e * e, axis=1, keepdims=True) * (0.5 / d)
        loss_ref[...] += jnp.sum(per_row, axis=0, keepdims=True)

    row = lambda i: (i, 0)
    return pl.pallas_call(
        body, name=name, grid=(s // tm,),
        out_shape=(jax.ShapeDtypeStruct((s, d), F32), jax.ShapeDtypeStruct((s, d), BF16),
                   jax.ShapeDtypeStruct((1, 1), F32)),
        in_specs=[pl.BlockSpec((tm, d), row), pl.BlockSpec((tm, d), row)],
        out_specs=(pl.BlockSpec((tm, d), row), pl.BlockSpec((tm, d), row), pl.BlockSpec((1, 1), lambda i: (0, 0))),
        compiler_params=_params("arbitrary"),
    )(y, target)


def mlp_bwd_up(dxb, u, w_down, *, name, ride=()):
    s, d = dxb.shape
    f = u.shape[1]
    tm = min(ROW_TILE, s)

    def body(dx_ref, u_ref, w_ref, du_ref):
        da = _dot_nt(dx_ref[...], w_ref[...])
        du_ref[...] = (da * (2.0 * jnp.maximum(u_ref[...].astype(F32), 0.0))).astype(BF16)

    row = lambda i: (i, 0)
    return _call(body, name=name, grid=(s // tm,), out_shape=(jax.ShapeDtypeStruct((s, f), BF16),),
                 in_specs=[pl.BlockSpec((tm, d), row), pl.BlockSpec((tm, f), row),
                           pl.BlockSpec((f, d), lambda i: (0, 0))],
                 out_specs=(pl.BlockSpec((tm, f), row),), args=(dxb, u, w_down), ride=ride)


def matmul_nt_norm_bwd(pieces, w, x, g, dres, *, name, ride=()):
    s = x.shape[0]
    d = x.shape[1]
    tm = min(ROW_TILE, s)
    blocked = w.ndim == 3
    n_pieces = len(pieces)
    widths = [p.shape[1] for p in pieces]

    def body(*refs):
        p_refs = refs[:n_pieces]
        w_ref, x_ref, g_ref, dres_ref, dx_ref, dxb_ref, dg_ref = refs[n_pieces:]

        @pl.when(pl.program_id(0) == 0)
        def _():
            dg_ref[...] = jnp.zeros_like(dg_ref)

        if blocked:
            nb = w_ref.shape[2]
            dh = _dot_nt(p_refs[0][:, :nb], w_ref[0])
            for j in range(1, w_ref.shape[0]):
                dh = dh + _dot_nt(p_refs[0][:, j * nb:(j + 1) * nb], w_ref[j])
        else:
            dh, off = None, 0
            for p_ref, width in zip(p_refs, widths):
                part = _dot(p_ref[...], w_ref[off:off + width, :])
                dh = part if dh is None else dh + part
                off += width
        xv = x_ref[...]
        r = _rsqrt_ms(xv)
        dyg = dh * g_ref[...]
        dx = dres_ref[...] + r * dyg - xv * ((r * r * r) * jnp.mean(dyg * xv, axis=-1, keepdims=True))
        dx_ref[...] = dx
        dxb_ref[...] = dx.astype(BF16)
        dg_ref[...] += jnp.sum(dh * (xv * r), axis=0, keepdims=True)

    row = lambda i: (i, 0)
    fixed = lambda i: (0, 0)
    w_spec = pl.BlockSpec(w.shape, (lambda i: (0, 0, 0)) if blocked else fixed)
    return _call(
        body, name=name, grid=(s // tm,),
        out_shape=(jax.ShapeDtypeStruct((s, d), F32), jax.ShapeDtypeStruct((s, d), BF16),
                   jax.ShapeDtypeStruct((1, d), F32)),
        in_specs=[pl.BlockSpec((tm, width), row) for width in widths] + [
            w_spec, pl.BlockSpec((tm, d), row), pl.BlockSpec((1, d), fixed), pl.BlockSpec((tm, d), row)],
        out_specs=(pl.BlockSpec((tm, d), row), pl.BlockSpec((tm, d), row), pl.BlockSpec((1, d), fixed)),
        args=(*pieces, w, x, g, dres), ride=ride)


def out_bwd(dx1b, w_o, gates, y_sb, y_sw, w_bsb, w_bsw, *, name):
    s, d = dx1b.shape
    wd = w_bsb.shape[0]
    tm = min(ROW_TILE, s)

    def body(dx_ref, wo_ref, g_ref, ysb_ref, ysw_ref, wsb_ref, wsw_ref, dysb_ref, dysw_ref, dosb_ref, dosw_ref, dgl_ref):
        dm = _dot_nt(dx_ref[...], wo_ref[...])
        g = g_ref[...].astype(F32)
        g0, g1 = g[:, :d], g[:, d:]
        dy_sb = (dm * g0).astype(BF16)
        dy_sw = (dm * g1).astype(BF16)
        dysb_ref[...] = dy_sb
        dysw_ref[...] = dy_sw
        dosb_ref[...] = _dot_nt(dy_sb, wsb_ref[...])
        dosw_ref[...] = _dot_nt(dy_sw, wsw_ref[...])
        dgl_ref[:, :d] = (dm * ysb_ref[...].astype(F32) * (g0 * (1.0 - g0))).astype(BF16)
        dgl_ref[:, d:] = (dm * ysw_ref[...].astype(F32) * (g1 * (1.0 - g1))).astype(BF16)

    row = lambda i: (i, 0)
    fixed = lambda i: (0, 0)
    return pl.pallas_call(
        body, name=name, grid=(s // tm,),
        out_shape=(jax.ShapeDtypeStruct((s, d), BF16), jax.ShapeDtypeStruct((s, d), BF16),
                   jax.ShapeDtypeStruct((s, wd), F32), jax.ShapeDtypeStruct((s, wd), F32),
                   jax.ShapeDtypeStruct((s, 2 * d), BF16)),
        in_specs=[pl.BlockSpec((tm, d), row), pl.BlockSpec((d, d), fixed), pl.BlockSpec((tm, 2 * d), row),
                  pl.BlockSpec((tm, d), row), pl.BlockSpec((tm, d), row), pl.BlockSpec((wd, d), fixed),
                  pl.BlockSpec((wd, d), fixed)],
        out_specs=(pl.BlockSpec((tm, d), row), pl.BlockSpec((tm, d), row), pl.BlockSpec((tm, wd), row),
                   pl.BlockSpec((tm, wd), row), pl.BlockSpec((tm, 2 * d), row)),
        compiler_params=_params("parallel"),
    )(dx1b, w_o, gates, y_sb, y_sw, w_bsb, w_bsw)


def matmul_tn(a, pieces, *, a_block, out_cols, relu2, name):
    s, m = a.shape
    widths = [p.shape[1] for p in pieces]
    n = sum(widths)
    n_pieces = len(pieces)
    ts = min(512 if n >= 4096 else 2048, s)
    n_steps = s // ts
    if out_cols is None:
        out_shape = jax.ShapeDtypeStruct((m // a_block, a_block, n), BF16)
        out_spec = pl.BlockSpec((None, a_block, n), lambda i, k: (i, 0, 0))
    else:
        out_shape = jax.ShapeDtypeStruct((n // out_cols, m, out_cols), BF16)
        out_spec = pl.BlockSpec((n // out_cols, a_block, out_cols), lambda i, k: (0, i, 0))

    def body(a_ref, *refs):
        b_refs, o_ref, acc = refs[:n_pieces], refs[n_pieces], refs[n_pieces + 1]
        k = pl.program_id(1)

        @pl.when(k == 0)
        def _():
            acc[...] = jnp.zeros_like(acc)

        av = a_ref[...]
        if relu2:
            af = jnp.maximum(av.astype(F32), 0.0)
            av = af * af
        av = av.astype(BF16)
        off = 0
        for b_ref in b_refs:
            width = b_ref.shape[1]
            acc[:, off:off + width] += _dot_tn(av, b_ref[...].astype(BF16))
            off += width

        @pl.when(k == n_steps - 1)
        def _():
            if out_cols is None:
                o_ref[...] = acc[...].astype(BF16)
            else:
                for j in range(n // out_cols):
                    o_ref[j] = acc[:, j * out_cols:(j + 1) * out_cols].astype(BF16)

    return pl.pallas_call(
        body, name=name, grid=(m // a_block, n_steps), out_shape=out_shape,
        in_specs=[pl.BlockSpec((ts, a_block), lambda i, k: (k, i))] + [
            pl.BlockSpec((ts, width), lambda i, k: (k, 0)) for width in widths],
        out_specs=out_spec, scratch_shapes=[pltpu.VMEM((a_block, n), F32)],
        compiler_params=_params("parallel", "arbitrary"),
    )(a, *pieces)


def matmul_tn_row_blocks(pieces, b, *, n_blocks, name):
    s, n = b.shape
    widths = [p.shape[1] for p in pieces]
    m = sum(widths)
    rows = m // n_blocks
    n_pieces = len(pieces)
    ts = min(512, s)
    n_steps = s // ts
    half = n_blocks // 2

    def body(*refs):
        p_refs, b_ref, o_ref, a_tile, acc = refs[:n_pieces], refs[n_pieces], refs[n_pieces + 1], refs[-2], refs[-1]
        i, k = pl.program_id(0), pl.program_id(1)

        @pl.when(k == 0)
        def _():
            acc[...] = jnp.zeros_like(acc)

        off = 0
        for p_ref, width in zip(p_refs, widths):
            a_tile[:, off:off + width] = p_ref[...]
            off += width
        bv = b_ref[...]
        for side in range(2):
            @pl.when(i == side)
            def _():
                for j in range(half):
                    col = (side * half + j) * rows
                    acc[j] += _dot_tn(a_tile[:, col:col + rows], bv)

        @pl.when(k == n_steps - 1)
        def _():
            o_ref[...] = acc[...].astype(BF16)

    return pl.pallas_call(
        body, name=name, grid=(2, n_steps), out_shape=jax.ShapeDtypeStruct((n_blocks, rows, n), BF16),
        in_specs=[pl.BlockSpec((ts, width), lambda i, k: (k, 0)) for width in widths] + [
            pl.BlockSpec((ts, n), lambda i, k: (k, 0))],
        out_specs=pl.BlockSpec((half, rows, n), lambda i, k: (i, 0, 0)),
        scratch_shapes=[pltpu.VMEM((ts, m), BF16), pltpu.VMEM((half, rows, n), F32)],
        compiler_params=_params("parallel", "arbitrary"),
    )(*pieces, b)


def _softplus(z):
    return jnp.maximum(z, 0.0) + jnp.log(1.0 + jnp.exp(-jnp.abs(z)))


def _suffix_sums(x, tri2):
    groups = x.shape[1] // LANES
    outs, run = [None] * groups, None
    for g in reversed(range(groups)):
        xg = x[:, g * LANES:(g + 1) * LANES]
        hi, lo = _split_bf16(xg)
        inner = _dot(jnp.concatenate([hi, lo], axis=1), tri2)
        outs[g] = inner if run is None else inner + run
        total = jnp.sum(xg, axis=1, keepdims=True)
        run = total if run is None else run + total
    return jnp.concatenate(outs, axis=1), run


def _head_mask(h):
    return (lax.broadcasted_iota(jnp.int32, (1, LANES), 1) // HEAD_DIM) == h


def _stack_heads(x):
    zero = jnp.zeros_like(x)
    return jnp.concatenate([jnp.where(_head_mask(0), x, zero), jnp.where(_head_mask(1), x, zero)], axis=0)


def _unstack_heads(r, t):
    return jnp.where(_head_mask(0), r[:t], r[t:])


def _sb_positions(q0, tk):
    row = lax.broadcasted_iota(jnp.int32, (2 * SB_TQ, tk), 0)
    col = lax.broadcasted_iota(jnp.int32, (2 * SB_TQ, tk), 1)
    return q0 + jnp.where(row >= SB_TQ, row - SB_TQ, row), col


def _sb_first_key(q0):
    return pl.multiple_of(jnp.maximum(q0 + SB_TQ - SB_TK1, 0), SB_TQ)


def _sb_next_key(k_prev):
    return pl.multiple_of(jnp.maximum(k_prev - SB_TK, 0), SB_TQ)


def _sb_rows(q0):
    return pl.ds(pl.multiple_of(2 * q0, 2 * SB_TQ), 2 * SB_TQ)


def sb_attn_fwd(proj, tri2, *, name, ride=()):
    s = proj.shape[0]
    nq = s // SB_TQ
    n_pairs = SB_WIDTH // LANES

    def body(q_ref, k_ref, v_ref, tri_ref, o_ref, c_all):
        def block(qh, k0, tk, live, c):
            z = _dot_nt(qh, k_ref[pl.ds(k0, tk), :])
            sp = _softplus(z)
            tail, total = _suffix_sums(jnp.where(live, -sp, 0.0), tri_ref[...])
            w = jnp.where(live, jnp.exp(z - sp + tail + c), 0.0)
            return _dot(w.astype(BF16), v_ref[pl.ds(k0, tk), :]), c + total

        def load_q(q0):
            return _stack_heads(q_ref[pl.ds(q0, SB_TQ), :]) * SCALE

        def first(qb, carry):
            q0 = pl.multiple_of(qb * SB_TQ, SB_TQ)
            tpos, col = _sb_positions(q0, SB_TK1)
            k0 = _sb_first_key(q0)
            acc, c = block(load_q(q0), k0, SB_TK1, k0 + col < tpos, jnp.zeros((2 * SB_TQ, 1), F32))
            o_ref[pl.ds(q0, SB_TQ), :] = _unstack_heads(acc, SB_TQ)
            c_all[_sb_rows(q0), :] = jnp.broadcast_to(jnp.where(k0 > 0, c, NEG), (2 * SB_TQ, LANES))
            return carry

        lax.fori_loop(0, nq, first, 0, unroll=2)

        @pl.when(jnp.max(c_all[...]) > SB_CUTOFF)
        def _():
            def more(qb, carry):
                q0 = pl.multiple_of(qb * SB_TQ, SB_TQ)
                c0 = c_all[_sb_rows(q0), 0:1]

                @pl.when(jnp.max(c0) > SB_CUTOFF)
                def _():
                    qh = load_q(q0)
                    _, col = _sb_positions(q0, SB_TK)

                    def cond(st):
                        return jnp.logical_and(st[0] > 0, st[3] > SB_CUTOFF)

                    def step(st):
                        k_prev, c, acc, _ = st
                        k0 = _sb_next_key(k_prev)
                        part, c = block(qh, k0, SB_TK, k0 + col < k_prev, c)
                        return k0, c, acc + part, jnp.max(c)

                    st = lax.while_loop(cond, step, (_sb_first_key(q0), c0, jnp.zeros((2 * SB_TQ, LANES), F32),
                                                     jnp.max(c0)))
                    o_ref[pl.ds(q0, SB_TQ), :] += _unstack_heads(st[2], SB_TQ)

                return carry

            lax.fori_loop(0, nq, more, 0)

    def col_spec(j):
        return pl.BlockSpec((s, LANES), lambda p: (0, j * n_pairs + p))

    (o,), rides = _call(
        body, name=name, grid=(n_pairs,), out_shape=(jax.ShapeDtypeStruct((s, SB_WIDTH), F32),),
        in_specs=[col_spec(0), col_spec(1), col_spec(2), pl.BlockSpec((2 * LANES, LANES), lambda p: (0, 0))],
        out_specs=(pl.BlockSpec((s, LANES), lambda p: (0, p)),), scratch_shapes=[pltpu.VMEM((2 * s, LANES), F32)],
        args=(proj, proj, proj, tri2), ride=ride)
    return o, rides


def sb_attn_bwd(proj, tri2, o, do, *, name, ride=()):
    s = proj.shape[0]
    nq = s // SB_TQ
    n_pairs = SB_WIDTH // LANES

    def body(q_ref, k_ref, v_ref, tri_ref, o_ref, do_ref, dq_ref, dk_ref, dv_ref, dq_acc, dk_acc, dv_acc, c_all, e_all):
        dk_acc[...] = jnp.zeros_like(dk_acc)
        dv_acc[...] = jnp.zeros_like(dv_acc)

        def load(q0):
            qh = _stack_heads(q_ref[pl.ds(q0, SB_TQ), :]) * SCALE
            doh_b = _stack_heads(do_ref[pl.ds(q0, SB_TQ), :].astype(BF16))
            ov = o_ref[pl.ds(q0, SB_TQ), :]
            dd = jnp.sum(doh_b.astype(F32) * jnp.concatenate([ov, ov], axis=0), axis=1, keepdims=True)
            return qh, doh_b, dd

        def block(qh, doh_b, dd, k0, tk, live, c, ce):
            kt = k_ref[pl.ds(k0, tk), :]
            z = _dot_nt(qh, kt)
            sp = _softplus(z)
            lb = z - sp
            tail, total = _suffix_sums(jnp.where(live, -sp, 0.0), tri_ref[...])
            wb = jnp.where(live, jnp.exp(lb + tail + c), 0.0).astype(BF16)
            e = wb.astype(F32) * _dot_nt(doh_b, v_ref[pl.ds(k0, tk), :])
            e_tail, e_total = _suffix_sums(e, tri_ref[...])
            dz = jnp.where(live, e - jnp.exp(lb) * (dd - ce - e_tail), 0.0)
            dzb = dz.astype(BF16)
            dk_acc[pl.ds(k0, tk), :] += _dot_tn(dzb, qh)
            dv_acc[pl.ds(k0, tk), :] += _dot_tn(wb, doh_b)
            return _dot(dzb, kt), c + total, ce + e_total

        def first(qb, carry):
            q0 = pl.multiple_of(qb * SB_TQ, SB_TQ)
            qh, doh_b, dd = load(q0)
            tpos, col = _sb_positions(q0, SB_TK1)
            k0 = _sb_first_key(q0)
            zero = jnp.zeros((2 * SB_TQ, 1), F32)
            dq, c, ce = block(qh, doh_b, dd, k0, SB_TK1, k0 + col < tpos, zero, zero)
            dq_acc[pl.ds(q0, SB_TQ), :] = _unstack_heads(dq, SB_TQ)
            c_all[_sb_rows(q0), :] = jnp.broadcast_to(jnp.where(k0 > 0, c, NEG), (2 * SB_TQ, LANES))
            e_all[_sb_rows(q0), :] = jnp.broadcast_to(ce, (2 * SB_TQ, LANES))
            return carry

        lax.fori_loop(0, nq, first, 0, unroll=2)

        @pl.when(jnp.max(c_all[...]) > SB_CUTOFF)
        def _():
            def more(qb, carry):
                q0 = pl.multiple_of(qb * SB_TQ, SB_TQ)
                c0 = c_all[_sb_rows(q0), 0:1]

                @pl.when(jnp.max(c0) > SB_CUTOFF)
                def _():
                    qh, doh_b, dd = load(q0)
                    _, col = _sb_positions(q0, SB_TK)

                    def cond(st):
                        return jnp.logical_and(st[0] > 0, st[4] > SB_CUTOFF)

                    def step(st):
                        k_prev, c, ce, dq, _ = st
                        k0 = _sb_next_key(k_prev)
                        part, c, ce = block(qh, doh_b, dd, k0, SB_TK, k0 + col < k_prev, c, ce)
                        return k0, c, ce, dq + part, jnp.max(c)

                    st = lax.while_loop(cond, step, (_sb_first_key(q0), c0, e_all[_sb_rows(q0), 0:1],
                                                     jnp.zeros((2 * SB_TQ, LANES), F32), jnp.max(c0)))
                    dq_acc[pl.ds(q0, SB_TQ), :] += _unstack_heads(st[3], SB_TQ)

                return carry

            lax.fori_loop(0, nq, more, 0)

        dq_ref[...] = (dq_acc[...] * SCALE).astype(BF16)
        dk_ref[...] = dk_acc[...].astype(BF16)
        dv_ref[...] = dv_acc[...].astype(BF16)

    def col_spec(j):
        return pl.BlockSpec((s, LANES), lambda p: (0, j * n_pairs + p))

    pair = pl.BlockSpec((s, LANES), lambda p: (0, p))
    (dq, dk, dv), rides = _call(
        body, name=name, grid=(n_pairs,), out_shape=(jax.ShapeDtypeStruct((s, SB_WIDTH), BF16),) * 3,
        in_specs=[col_spec(0), col_spec(1), col_spec(2), pl.BlockSpec((2 * LANES, LANES), lambda p: (0, 0)), pair, pair],
        out_specs=(pair, pair, pair),
        scratch_shapes=[pltpu.VMEM((s, LANES), F32)] * 3 + [pltpu.VMEM((2 * s, LANES), F32)] * 2,
        args=(proj, proj, proj, tri2, o, do), ride=ride)
    return dq, dk, dv, rides


def _lane_lo():
    return lax.broadcasted_iota(jnp.int32, (1, LANES), 1) < HEAD_DIM


def _swap_halves(x):
    return pltpu.roll(x, HEAD_DIM, 1)


def _rot_half(y):
    first = (lax.broadcasted_iota(jnp.int32, (1, LANES), 1) % HEAD_DIM) < (HEAD_DIM // 2)
    return jnp.where(first, pltpu.roll(y, LANES - HEAD_DIM // 2, 1), pltpu.roll(y, HEAD_DIM // 2, 1))


def _head_mean(v):
    lo = _lane_lo()
    s0 = jnp.sum(jnp.where(lo, v, 0.0), axis=1, keepdims=True)
    s1 = jnp.sum(jnp.where(lo, 0.0, v), axis=1, keepdims=True)
    return jnp.where(lo, s0, s1) * (1.0 / HEAD_DIM)


def swa_prep_fwd(proj, cos_p, sin_p, gq, gk, *, name):
    s = proj.shape[0]
    tm = min(512, s)
    q_blk = (3 * SB_WIDTH) // SWA_Q_WIDTH
    k_blk = (3 * SB_WIDTH + SWA_Q_WIDTH) // LANES

    def norm_rope(xv, g, cosv, sinv):
        y = (xv * lax.rsqrt(_head_mean(xv * xv) + NORM_EPS)) * g
        return y * cosv + _rot_half(y) * sinv

    def body(q_ref, k_ref, cos_ref, sin_ref, gq_ref, gk_ref, qn_ref, kn_ref):
        cosv, sinv = cos_ref[...], sin_ref[...]
        for j in range(SWA_Q_WIDTH // LANES):
            sl = slice(j * LANES, (j + 1) * LANES)
            qn_ref[:, sl] = norm_rope(q_ref[:, sl].astype(F32), gq_ref[...], cosv, sinv).astype(BF16)
        kn_ref[...] = norm_rope(k_ref[...].astype(F32), gk_ref[...], cosv, sinv).astype(BF16)

    row = lambda i: (i, 0)
    fixed = lambda i: (0, 0)
    return pl.pallas_call(
        body, name=name, grid=(s // tm,),
        out_shape=(jax.ShapeDtypeStruct((s, SWA_Q_WIDTH), BF16), jax.ShapeDtypeStruct((s, LANES), BF16)),
        in_specs=[pl.BlockSpec((tm, SWA_Q_WIDTH), lambda i: (i, q_blk)), pl.BlockSpec((tm, LANES), lambda i: (i, k_blk)),
                  pl.BlockSpec((tm, LANES), row), pl.BlockSpec((tm, LANES), row),
                  pl.BlockSpec((1, LANES), fixed), pl.BlockSpec((1, LANES), fixed)],
        out_specs=(pl.BlockSpec((tm, SWA_Q_WIDTH), row), pl.BlockSpec((tm, LANES), row)),
        compiler_params=_params("parallel"),
    )(proj, proj, cos_p, sin_p, gq, gk)


def swa_prep_bwd(proj, cos_p, sin_p, gq, gk, dqn, dkn, dv, *, name):
    s = proj.shape[0]
    tm = min(512, s)
    q_blk = (3 * SB_WIDTH) // SWA_Q_WIDTH
    k_blk = (3 * SB_WIDTH + SWA_Q_WIDTH) // LANES

    def bwd(xv, g, cosv, sinv, dout):
        dy = dout * cosv + _rot_half(dout * sinv)
        r = lax.rsqrt(_head_mean(xv * xv) + NORM_EPS)
        dyg = dy * g
        dx = r * dyg - xv * ((r * r * r) * _head_mean(dyg * xv))
        return dx, jnp.sum(dy * (xv * r), axis=0, keepdims=True)

    def body(q_ref, k_ref, cos_ref, sin_ref, gq_ref, gk_ref, dqn_ref, dkn_ref, dv_ref, dq_ref, dk_ref, dvb_ref,
             dgq_ref, dgk_ref):
        @pl.when(pl.program_id(0) == 0)
        def _():
            dgq_ref[...] = jnp.zeros_like(dgq_ref)
            dgk_ref[...] = jnp.zeros_like(dgk_ref)

        cosv, sinv = cos_ref[...], sin_ref[...]
        for j in range(SWA_Q_WIDTH // LANES):
            sl = slice(j * LANES, (j + 1) * LANES)
            dx, dg = bwd(q_ref[:, sl].astype(F32), gq_ref[...], cosv, sinv, dqn_ref[:, sl])
            dq_ref[:, sl] = dx.astype(BF16)
            dgq_ref[:, sl] += dg
        dx, dg = bwd(k_ref[...].astype(F32), gk_ref[...], cosv, sinv, dkn_ref[...])
        dk_ref[...] = dx.astype(BF16)
        dgk_ref[...] += dg
        dvb_ref[...] = dv_ref[...].astype(BF16)

    row = lambda i: (i, 0)
    fixed = lambda i: (0, 0)
    lane_row = pl.BlockSpec((tm, LANES), row)
    return pl.pallas_call(
        body, name=name, grid=(s // tm,),
        out_shape=(jax.ShapeDtypeStruct((s, SWA_Q_WIDTH), BF16), jax.ShapeDtypeStruct((s, LANES), BF16),
                   jax.ShapeDtypeStruct((s, LANES), BF16),
                   jax.ShapeDtypeStruct((1, SWA_Q_WIDTH), F32), jax.ShapeDtypeStruct((1, LANES), F32)),
        in_specs=[pl.BlockSpec((tm, SWA_Q_WIDTH), lambda i: (i, q_blk)), pl.BlockSpec((tm, LANES), lambda i: (i, k_blk)),
                  lane_row, lane_row, pl.BlockSpec((1, LANES), fixed), pl.BlockSpec((1, LANES), fixed),
                  pl.BlockSpec((tm, SWA_Q_WIDTH), row), lane_row, lane_row],
        out_specs=(pl.BlockSpec((tm, SWA_Q_WIDTH), row), lane_row, lane_row,
                   pl.BlockSpec((1, SWA_Q_WIDTH), fixed), pl.BlockSpec((1, LANES), fixed)),
        compiler_params=_params("arbitrary"),
    )(proj, proj, cos_p, sin_p, gq, gk, dqn, dkn, dv)


def _swa_kv_copies(k_ref, v_ref, kg_ref, vg_ref, second_kv):
    s = k_ref.shape[0]
    rows = min(512, s)
    keep = jnp.logical_xor(_lane_lo(), second_kv)

    def chunk(r, carry):
        sl = pl.ds(pl.multiple_of(r * rows, rows), rows)
        for src, dst in ((k_ref, kg_ref), (v_ref, vg_ref)):
            f = src[sl, :].astype(F32)
            dst[sl, :] = jnp.where(keep, f, _swap_halves(f)).astype(BF16)
        return carry

    lax.fori_loop(0, s // rows, chunk, 0)


def _swa_tile(i, kg_ref, vg_ref):
    q0 = pl.multiple_of(i * SWA_TQ, SWA_TQ)
    k0 = pl.multiple_of(jnp.maximum(i - 1, 0) * SWA_TQ, SWA_TQ)
    kg = kg_ref[pl.ds(k0, SWA_TK), :]
    vg = vg_ref[pl.ds(k0, SWA_TK), :]
    row = lax.broadcasted_iota(jnp.int32, (2 * SWA_TQ, SWA_TK), 0)
    tpos = q0 + jnp.where(row >= SWA_TQ, row - SWA_TQ, row)
    spos = k0 + lax.broadcasted_iota(jnp.int32, (2 * SWA_TQ, SWA_TK), 1)
    valid = jnp.logical_and(spos <= tpos, spos > tpos - WINDOW)
    return q0, k0, kg, vg, valid


def _swa_probs(qh, kg, valid, sink):
    z = jnp.where(valid, _dot_nt(qh, kg) * SCALE, NEG)
    m = jnp.maximum(jnp.max(z, axis=1, keepdims=True), sink)
    pexp = jnp.exp(z - m)
    psink = jnp.exp(sink - m)
    inv = 1.0 / (jnp.sum(pexp, axis=1, keepdims=True) + psink)
    return pexp * inv, psink * inv


def _stacked_sink(sink_row):
    s0 = jnp.sum(jnp.where(_head_mask(0), sink_row, 0.0), axis=1, keepdims=True) * (1.0 / HEAD_DIM)
    s1 = jnp.sum(jnp.where(_head_mask(1), sink_row, 0.0), axis=1, keepdims=True) * (1.0 / HEAD_DIM)
    top = lax.broadcasted_iota(jnp.int32, (2 * SWA_TQ, 1), 0) < SWA_TQ
    return jnp.where(top, s0, s1)


def swa_attn_fwd(qn, kn, proj, sink_p, *, name, ride=()):
    s = qn.shape[0]
    nq = s // SWA_TQ
    n_pairs = SWA_Q_WIDTH // LANES
    v_blk = (3 * SB_WIDTH + SWA_Q_WIDTH + SWA_KV_WIDTH) // LANES

    def body(q_ref, k_ref, v_ref, s_ref, o_ref, kg_ref, vg_ref):
        _swa_kv_copies(k_ref, v_ref, kg_ref, vg_ref, (pl.program_id(0) // 2) == 1)
        sink = _stacked_sink(s_ref[...])

        def tile(i, carry):
            q0, _, kg, vg, valid = _swa_tile(i, kg_ref, vg_ref)
            probs, _ = _swa_probs(_stack_heads(q_ref[pl.ds(q0, SWA_TQ), :]), kg, valid, sink)
            o_ref[pl.ds(q0, SWA_TQ), :] = _unstack_heads(_dot(probs.astype(BF16), vg), SWA_TQ)
            return carry

        lax.fori_loop(0, nq, tile, 0, unroll=4)

    pair = pl.BlockSpec((s, LANES), lambda p: (0, p))
    whole = pl.BlockSpec((s, LANES), lambda p: (0, 0))
    (o,), rides = _call(
        body, name=name, grid=(n_pairs,), out_shape=(jax.ShapeDtypeStruct((s, SWA_Q_WIDTH), F32),),
        in_specs=[pair, whole, pl.BlockSpec((s, LANES), lambda p: (0, v_blk)),
                  pl.BlockSpec((None, 1, LANES), lambda p: (p, 0, 0))],
        out_specs=(pair,), scratch_shapes=[pltpu.VMEM((s, LANES), BF16)] * 2, args=(qn, kn, proj, sink_p), ride=ride)
    return o, rides


def swa_attn_bwd(qn, kn, proj, sink_p, o, do, *, name, ride=()):
    s = qn.shape[0]
    nq = s // SWA_TQ
    n_pairs = SWA_Q_WIDTH // LANES
    v_blk = (3 * SB_WIDTH + SWA_Q_WIDTH + SWA_KV_WIDTH) // LANES
    fold_rows = min(512, s)

    def body(q_ref, k_ref, v_ref, s_ref, o_ref, do_ref, dq_ref, dk_ref, dv_ref, ds_ref, acc_k, acc_v, kg_ref, vg_ref):
        p = pl.program_id(0)
        _swa_kv_copies(k_ref, v_ref, kg_ref, vg_ref, (p // 2) == 1)
        sink = _stacked_sink(s_ref[...])

        @pl.when(p % 2 == 0)
        def _():
            acc_k[...] = jnp.zeros_like(acc_k)
            acc_v[...] = jnp.zeros_like(acc_v)

        def tile(i, dsink):
            q0, k0, kg, vg, valid = _swa_tile(i, kg_ref, vg_ref)
            qh = _stack_heads(q_ref[pl.ds(q0, SWA_TQ), :])
            doh = _stack_heads(do_ref[pl.ds(q0, SWA_TQ), :])
            doh_b = doh.astype(BF16)
            ov = o_ref[pl.ds(q0, SWA_TQ), :]
            delta = jnp.sum(doh * jnp.concatenate([ov, ov], axis=0), axis=1, keepdims=True)
            probs, psink = _swa_probs(qh, kg, valid, sink)
            dz = probs * (_dot_nt(doh_b, vg) - delta)
            dzb = (dz * SCALE).astype(BF16)
            dq_ref[pl.ds(q0, SWA_TQ), :] = _unstack_heads(_dot(dzb, kg), SWA_TQ)
            acc_k[pl.ds(k0, SWA_TK), :] += _dot_tn(dzb, qh)
            acc_v[pl.ds(k0, SWA_TK), :] += _dot_tn(probs.astype(BF16), doh_b)
            pd = psink * delta
            return dsink - jnp.where(_head_mask(0), jnp.sum(pd[:SWA_TQ], axis=0, keepdims=True),
                                     jnp.sum(pd[SWA_TQ:], axis=0, keepdims=True))

        ds_ref[...] = lax.fori_loop(0, nq, tile, jnp.zeros((1, LANES), F32), unroll=4)

        def fold_into(first_head):
            def fold(r, carry):
                rows = pl.ds(pl.multiple_of(r * fold_rows, fold_rows), fold_rows)
                for acc, out in ((acc_k, dk_ref), (acc_v, dv_ref)):
                    a = acc[rows, :]
                    both = a + _swap_halves(a)
                    if first_head:
                        out[rows, :] = jnp.where(_lane_lo(), both, 0.0)
                    else:
                        out[rows, :] = jnp.where(_lane_lo(), out[rows, :], both)
                return carry

            lax.fori_loop(0, s // fold_rows, fold, 0)

        @pl.when(p == 1)
        def _():
            fold_into(True)

        @pl.when(p == 3)
        def _():
            fold_into(False)

    pair = pl.BlockSpec((s, LANES), lambda p: (0, p))
    whole = pl.BlockSpec((s, LANES), lambda p: (0, 0))
    sink_spec = pl.BlockSpec((None, 1, LANES), lambda p: (p, 0, 0))
    (dq, dk, dv, dsink), rides = _call(
        body, name=name, grid=(n_pairs,),
        out_shape=(jax.ShapeDtypeStruct((s, SWA_Q_WIDTH), F32), jax.ShapeDtypeStruct((s, LANES), F32),
                   jax.ShapeDtypeStruct((s, LANES), F32), jax.ShapeDtypeStruct((n_pairs, 1, LANES), F32)),
        in_specs=[pair, whole, pl.BlockSpec((s, LANES), lambda p: (0, v_blk)), sink_spec, pair, pair],
        out_specs=(pair, whole, whole, sink_spec),
        scratch_shapes=[pltpu.VMEM((s, LANES), F32)] * 2 + [pltpu.VMEM((s, LANES), BF16)] * 2,
        args=(qn, kn, proj, sink_p, o, do), ride=ride)
    return dq, dk, dv, dsink, rides


def _rope_tables(s):
    inv_freq = 1.0 / (ROPE_THETA ** (jnp.arange(0, HEAD_DIM, 2, dtype=F32) / HEAD_DIM))
    ang = jnp.arange(s, dtype=F32)[:, None] * inv_freq[None, :]
    cos, sin = jnp.cos(ang), jnp.sin(ang)
    cos_p = jnp.tile(jnp.concatenate([cos, cos], axis=1), (1, LANES // HEAD_DIM))
    sin_p = jnp.tile(jnp.concatenate([-sin, sin], axis=1), (1, LANES // HEAD_DIM))
    return cos_p, sin_p


def _lane_tile(v, reps):
    return jnp.tile(v.reshape(1, -1), (1, reps))


def _natural(stack, w):
    n, r, c = stack.shape
    if MATRIX_NAMES[w] in ROW_SHARDED or w == W_IN:
        return stack.reshape(n * r, c)
    if w == W_UP:
        return stack
    return jnp.transpose(stack, (1, 0, 2)).reshape(r, n * c)


def _pack_small(tree):
    flat = jnp.concatenate([tree[n].reshape(-1) for n in SMALL_NAMES])
    rows = -(-flat.shape[0] // (8 * LANES)) * 8
    return jnp.pad(flat, (0, rows * LANES - flat.shape[0])).reshape(rows, LANES)


def _unpack_small(packed, shapes):
    flat, out, off = packed.reshape(-1), {}, 0
    for n in SMALL_NAMES:
        size = shapes[n][0] * shapes[n][1]
        out[n] = flat[off:off + size].reshape(shapes[n])
        off += size
    return out


def train_step(x, target, weights, mom_m, mom_v):
    s = x.shape[0]
    cos_p, sin_p = _rope_tables(s)
    tri = (jnp.arange(LANES)[:, None] > jnp.arange(LANES)[None, :]).astype(BF16)
    tri = jnp.concatenate([tri, tri], axis=0)
    local = {n: (jnp.swapaxes(t, 1, 2) if n == "w_in" else t) for n, t in weights.items()}
    local_m = {n: (jnp.swapaxes(t, 1, 2) if n == "w_in" else t) for n, t in mom_m.items()}
    local_v = {n: (jnp.swapaxes(t, 1, 2) if n == "w_in" else t) for n, t in mom_v.items()}
    shards = [[local[n][l].astype(BF16) for n in MATRIX_NAMES] for l in range(DEPTH)]
    core = lax.axis_index("c").astype(jnp.int32).reshape(1)
    chip = (2 * lax.axis_index("x") + lax.axis_index("y")).astype(jnp.int32).reshape(1)

    def gather(l, ws):
        return GatherJob([shards[l][w] for w in ws])

    w_in = _natural(exchange_alone(gather(0, [W_IN]), name="gather_w_in0")[0], W_IN)
    saved = []
    for l in range(DEPTH):
        g_mix = weights["mix_norm_g"][l].reshape(1, D_MODEL)
        g_mlp = weights["mlp_norm_g"][l].reshape(1, D_MODEL)
        gq = _lane_tile(weights["q_norm_g"][l], LANES // HEAD_DIM)
        gk = _lane_tile(weights["k_norm_g"][l], LANES // HEAD_DIM)
        sink_p = jnp.repeat(weights["sinks"][l].reshape(SWA_Q_WIDTH // LANES, 2), HEAD_DIM, axis=1)
        sink_p = sink_p.reshape(SWA_Q_WIDTH // LANES, 1, LANES)
        (h, proj, gates), ((s_bsb, s_bsw, s_out),) = norm_matmul(
            x, g_mix, w_in, gate_split=ATTN_WIDTH, name="in_proj", ride=[gather(l, [W_BSB, W_BSW, W_OUT])])
        if l + 1 < DEPTH:
            o_sb, ((s_up,), (s_in,)) = sb_attn_fwd(proj, tri, name="sb_fwd",
                                                   ride=[gather(l, [W_UP]), gather(l + 1, [W_IN])])
        else:
            o_sb, ((s_up,),) = sb_attn_fwd(proj, tri, name="sb_fwd_last", ride=[gather(l, [W_UP])])
        qn, kn = swa_prep_fwd(proj, cos_p, sin_p, gq, gk, name="swa_prep")
        o_sw, ((s_down,),) = swa_attn_fwd(qn, kn, proj, sink_p, name="swa_fwd", ride=[gather(l, [W_DOWN])])
        mats = [w_in, _natural(s_bsb, W_BSB), _natural(s_bsw, W_BSW), _natural(s_out, W_OUT), _natural(s_up, W_UP),
                _natural(s_down, W_DOWN)]
        x1, y_sb, y_sw, merged = merge_out_fwd(x, o_sb, o_sw, gates, mats[W_BSB], mats[W_BSW], mats[W_OUT],
                                               name="merge_out")
        (h2, u), _ = norm_matmul(x1, g_mlp, mats[W_UP], gate_split=None, name="mlp_up")
        x2 = mlp_down_fwd(x1, u, mats[W_DOWN], name="mlp_down")
        if l + 1 < DEPTH:
            w_in = _natural(s_in, W_IN)
        saved.append(dict(x=x, h=h, proj=proj, gates=gates, o_sb=o_sb, qn=qn, kn=kn, o_sw=o_sw, y_sb=y_sb, y_sw=y_sw,
                          merged=merged, x1=x1, h2=h2, u=u, g_mix=g_mix, g_mlp=g_mlp, gq=gq, gk=gk, sink_p=sink_p,
                          mats=mats))
        x = x2

    dx, dxb, loss = loss_head(x, target, name="loss_head")

    shard_shapes = [local[n].shape[1:] for n in MATRIX_NAMES]
    parts = [lax.empty((DEPTH, N_CHIPS) + sh, BF16) for sh in shard_shapes]
    lands = [lax.empty((DEPTH, 3) + sh, BF16) for sh in shard_shapes]
    small_grads = {n: [None] * DEPTH for n in SMALL_NAMES}
    half = D_MODEL // 2

    def summed(l, ws, grads, landed):
        new = pair_sum(l, grads, landed, [parts[w] for w in ws], core, name="grad_pair_sum")
        for w, p in zip(ws, new):
            parts[w] = p

    def chip_job(items):
        return ChipJob(items, parts, lands)

    def landed_chip(job, outs):
        for w, a in zip(job.ws, outs):
            lands[w] = a

    in_pending = None
    for l in reversed(range(DEPTH)):
        a = saved[l]
        mats = a["mats"]
        (du,), _ = mlp_bwd_up(dxb, a["u"], mats[W_DOWN], name="mlp_bwd_up")
        dw_down = matmul_tn(a["u"], [dxb], a_block=half, out_cols=None, relu2=True, name="dw_down")
        dw_up = matmul_tn(a["h2"], [du], a_block=half, out_cols=du.shape[1] // N_DEV, relu2=False, name="dw_up")
        g_mlp_w = [dw_up, dw_down.reshape((N_DEV,) + shard_shapes[W_DOWN])]
        (dx1, dx1b, dg_mlp), (landed,) = matmul_nt_norm_bwd([du], mats[W_UP], a["x1"], a["g_mlp"], dx,
                                                            name="mlp_bwd_norm", ride=[PairJob(g_mlp_w)])
        summed(l, [W_UP, W_DOWN], g_mlp_w, landed)
        small_grads["mlp_norm_g"][l] = dg_mlp.reshape(D_MODEL)
        dw_out = matmul_tn(a["merged"], [dx1b], a_block=half, out_cols=None, relu2=False, name="dw_out")
        dy_sb, dy_sw, do_sb, do_sw, dgl = out_bwd(dx1b, mats[W_OUT], a["gates"], a["y_sb"], a["y_sw"],
                                                  mats[W_BSB], mats[W_BSW], name="out_bwd")
        dw_bsb = matmul_tn(a["o_sb"], [dy_sb], a_block=half, out_cols=D_MODEL // N_DEV, relu2=False, name="dw_branch_sb")
        dw_bsw = matmul_tn(a["o_sw"], [dy_sw], a_block=half, out_cols=D_MODEL // N_DEV, relu2=False, name="dw_branch_swa")
        g_mix_w = [dw_bsb, dw_bsw, dw_out.reshape((N_DEV,) + shard_shapes[W_OUT])]
        job = chip_job([(l, W_UP), (l, W_DOWN)])
        dq_sb, dk_sb, dv_sb, (outs, landed) = sb_attn_bwd(a["proj"], tri, a["o_sb"], do_sb, name="sb_bwd",
                                                         ride=[job, PairJob(g_mix_w)])
        landed_chip(job, outs)
        summed(l, [W_BSB, W_BSW, W_OUT], g_mix_w, landed)
        job = chip_job([(l, W_BSB), (l, W_BSW), (l, W_OUT)] + ([(in_pending, W_IN)] if in_pending is not None else []))
        dqn, dkn, dv_sw, dsink, (outs,) = swa_attn_bwd(a["qn"], a["kn"], a["proj"], a["sink_p"], a["o_sw"], do_sw,
                                                      name="swa_bwd", ride=[job])
        landed_chip(job, outs)
        dq_sw, dk_sw, dv_swb, dgq, dgk = swa_prep_bwd(a["proj"], cos_p, sin_p, a["gq"], a["gk"], dqn, dkn, dv_sw,
                                                      name="swa_prep_bwd")
        small_grads["q_norm_g"][l] = dgq.reshape(SWA_Q_WIDTH // HEAD_DIM, HEAD_DIM).sum(0)
        small_grads["k_norm_g"][l] = dgk.reshape(LANES // HEAD_DIM, HEAD_DIM).sum(0)
        small_grads["sinks"][l] = dsink[:, 0, ::HEAD_DIM].reshape(SWA_Q_WIDTH // HEAD_DIM)
        pieces = [dq_sb, dk_sb, dv_sb, dq_sw, dk_sw, dv_swb, dgl]
        g_in = [matmul_tn_row_blocks(pieces, a["h"], n_blocks=N_DEV, name="dw_in")]
        (dx, dxb, dg_mix), (landed,) = matmul_nt_norm_bwd(pieces, mats[W_IN], a["x"], a["g_mix"], dx1,
                                                         name="in_proj_bwd", ride=[PairJob(g_in)])
        summed(l, [W_IN], g_in, landed)
        small_grads["mix_norm_g"][l] = dg_mix.reshape(D_MODEL)
        in_pending = l
    job = chip_job([(in_pending, W_IN)])
    landed_chip(job, exchange_alone(job, name="grad_chip_exchange_in0"))

    out_g, out_d, out_m, out_v = {}, {}, {}, {}
    for i, n in enumerate(MATRIX_NAMES):
        outs = reduce_adamw(parts[i], lands[i], chip, local[n], local_m[n], local_v[n], name="adamw_" + n)
        if n == "w_in":
            outs = [jnp.swapaxes(t, 1, 2) for t in outs]
        out_g[n], out_d[n], out_m[n], out_v[n] = outs
    small_shapes = {n: weights[n].shape for n in SMALL_NAMES}
    small_all = gather_small(_pack_small({n: jnp.stack(v) for n, v in small_grads.items()}), name="gather_small_grads")
    sg, sd, sm, sv = small_adamw(small_all, _pack_small(weights), _pack_small(mom_m), _pack_small(mom_v),
                                 name="small_adamw")
    for tree, packed_small in ((out_g, sg), (out_d, sd), (out_m, sm), (out_v, sv)):
        tree.update(_unpack_small(packed_small, small_shapes))
    return loss, dx, (out_g, out_d, out_m, out_v)


def kernel(x, mix_norm_g, w_in, q_norm_g, k_norm_g, sinks, w_branch_sb, w_branch_swa, w_out, mlp_norm_g, w_up, w_down, loss_target, m_mix_norm_g, m_w_in, m_q_norm_g, m_k_norm_g, m_sinks, m_w_branch_sb, m_w_branch_swa, m_w_out, m_mlp_norm_g, m_w_up, m_w_down, v_mix_norm_g, v_w_in, v_q_norm_g, v_k_norm_g, v_sinks, v_w_branch_sb, v_w_branch_swa, v_w_out, v_mlp_norm_g, v_w_up, v_w_down):
    weights = dict(mix_norm_g=mix_norm_g, w_in=w_in, q_norm_g=q_norm_g, k_norm_g=k_norm_g, sinks=sinks,
                   w_branch_sb=w_branch_sb, w_branch_swa=w_branch_swa, w_out=w_out, mlp_norm_g=mlp_norm_g, w_up=w_up,
                   w_down=w_down)
    mom_m = dict(mix_norm_g=m_mix_norm_g, w_in=m_w_in, q_norm_g=m_q_norm_g, k_norm_g=m_k_norm_g, sinks=m_sinks,
                 w_branch_sb=m_w_branch_sb, w_branch_swa=m_w_branch_swa, w_out=m_w_out, mlp_norm_g=m_mlp_norm_g,
                 w_up=m_w_up, w_down=m_w_down)
    mom_v = dict(mix_norm_g=v_mix_norm_g, w_in=v_w_in, q_norm_g=v_q_norm_g, k_norm_g=v_k_norm_g, sinks=v_sinks,
                 w_branch_sb=v_w_branch_sb, w_branch_swa=v_w_branch_swa, w_out=v_w_out, mlp_norm_g=v_mlp_norm_g,
                 w_up=v_w_up, w_down=v_w_down)
    loss_part, grad_x, outs = train_step(x[0], loss_target[0], weights, mom_m, mom_v)
    loss = lax.psum(loss_part[0, 0], MESH_AXES)
    return (loss, grad_x[None], *[outs[0][n] for n in WEIGHT_ORDER], *[outs[1][n] for n in WEIGHT_ORDER],
            *[outs[2][n] for n in WEIGHT_ORDER], *[outs[3][n] for n in WEIGHT_ORDER])
```

```python
import functools
import math

import jax
import jax.numpy as jnp
from jax import lax
from jax.experimental import pallas as pl
from jax.experimental.pallas import tpu as pltpu

F32 = jnp.float32
BF16 = jnp.bfloat16

DEPTH = 4
D_MODEL = 1024
HEAD_DIM = 64
LANES = 128
WINDOW = 128
SB_WIDTH = 512
SWA_Q_WIDTH = 512
SWA_KV_WIDTH = 128
ATTN_WIDTH = 3 * SB_WIDTH + SWA_Q_WIDTH + 2 * SWA_KV_WIDTH
IN_WIDTH = ATTN_WIDTH + 2 * D_MODEL
ROPE_THETA = 10000.0
NORM_EPS = 1e-6
SCALE = HEAD_DIM ** -0.5
NEG = -1e30
N_DEV = 8
N_CHIPS = 4

ADAM_LR = 0.001
ADAM_B1 = 0.9
ADAM_B2 = 0.999
ADAM_EPS = 1e-08
ADAM_WD = 0.01
ADAM_STEP = 10

SB_TQ = 128
SB_TK1 = 384
SB_TK = 256
SB_CUTOFF = -88.0
SWA_TQ = 128
SWA_TK = 256
ROW_TILE = 512
VMEM_LIMIT = 56 * 1024 * 1024

MATRIX_NAMES = ("w_in", "w_branch_sb", "w_branch_swa", "w_out", "w_up", "w_down")
W_IN, W_BSB, W_BSW, W_OUT, W_UP, W_DOWN = range(6)
ROW_SHARDED = ("w_out", "w_down")
SMALL_NAMES = ("mix_norm_g", "q_norm_g", "k_norm_g", "sinks", "mlp_norm_g")
WEIGHT_ORDER = ("mix_norm_g", "w_in", "q_norm_g", "k_norm_g", "sinks", "w_branch_sb", "w_branch_swa", "w_out",
                "mlp_norm_g", "w_up", "w_down")
MESH_AXES = ("x", "y", "c")

ANY = pl.BlockSpec(memory_space=pl.ANY)
MESH = pl.DeviceIdType.MESH


def _params(*sem):
    return pltpu.CompilerParams(dimension_semantics=sem, vmem_limit_bytes=VMEM_LIMIT)


def _dot(a, b):
    return jnp.dot(a, b, preferred_element_type=F32)


def _dot_nt(a, b):
    return lax.dot_general(a, b, (((1,), (1,)), ((), ())), preferred_element_type=F32)


def _dot_tn(a, b):
    return lax.dot_general(a, b, (((0,), (0,)), ((), ())), preferred_element_type=F32)


def _split_bf16(x):
    hi = x.astype(BF16)
    lo = (x - hi.astype(F32)).astype(BF16)
    return hi, lo


def _rsqrt_ms(x):
    return lax.rsqrt(jnp.mean(x * x, axis=-1, keepdims=True) + NORM_EPS)


def _place():
    return lax.axis_index("x"), lax.axis_index("y"), lax.axis_index("c")


class _Gather:
    def __init__(self, x_refs, out_refs, send_sems, recv_sems, local_sems):
        self.x_refs, self.out_refs = x_refs, out_refs
        self.send_sems, self.recv_sems, self.local_sems = send_sems, recv_sems, local_sems
        self.n = len(x_refs)
        x, y, c = _place()
        self.c = c
        self.me, self.sibling = (x, y, c), (x, y, 1 - c)
        self.chips = [(1 - x, y), (x, 1 - y), (1 - x, 1 - y)]

    def _copy(self, k, w, blk, to, own=False):
        dst = self.out_refs[w].at[4 * blk[0] + 2 * blk[1] + blk[2]]
        return pltpu.make_async_remote_copy(
            src_ref=self.x_refs[w] if own else dst, dst_ref=dst, send_sem=self.send_sems.at[k, w],
            recv_sem=self.recv_sems.at[k, w], device_id=to, device_id_type=MESH)

    def _mine(self, w):
        me = self.me
        return pltpu.make_async_copy(self.x_refs[w], self.out_refs[w].at[4 * me[0] + 2 * me[1] + me[2]],
                                     self.local_sems.at[w])

    def _first(self, w):
        return [self._copy(0, w, self.me, self.sibling, own=True)] + [
            self._copy(1 + j, w, self.me, (*chip, self.c), own=True) for j, chip in enumerate(self.chips)]

    def _passed(self, j, w):
        return self._copy(4 + j, w, (*self.chips[j], self.c), self.sibling)

    def start(self):
        for w in range(self.n):
            self._mine(w).start()
            for cp in self._first(w):
                cp.start()

    def relay(self):
        for j, chip in enumerate(self.chips):
            for w in range(self.n):
                self._copy(1 + j, w, (*chip, self.c), self.me).wait_recv()
                self._passed(j, w).start()

    def finish(self):
        for w in range(self.n):
            self._copy(0, w, self.sibling, self.me).wait_recv()
            for j, chip in enumerate(self.chips):
                self._copy(4 + j, w, (*chip, 1 - self.c), self.me).wait_recv()
            for cp in self._first(w):
                cp.wait_send()
            for j in range(3):
                self._passed(j, w).wait_send()
            self._mine(w).wait()


class GatherJob:
    def __init__(self, shards):
        n = len(shards)
        self.inputs = list(shards)
        self.out_shapes = [jax.ShapeDtypeStruct((N_DEV,) + s.shape, s.dtype) for s in shards]
        self.aliases = {}
        self.scratch = [pltpu.SemaphoreType.DMA((7, n)), pltpu.SemaphoreType.DMA((7, n)),
                        pltpu.SemaphoreType.DMA((n,))]

    def bind(self, in_refs, out_refs, scratch_refs):
        return _Gather(in_refs, out_refs, *scratch_refs)


class _Copies:
    def __init__(self, copies):
        self.copies = copies

    def start(self):
        for cp in self.copies:
            cp.start()

    def relay(self):
        pass

    def finish(self):
        for cp in self.copies:
            cp.wait_recv()
        for cp in self.copies:
            cp.wait_send()


class ChipJob:
    def __init__(self, items, parts, lands):
        self.ws = sorted({w for _, w in items})
        n = len(self.ws)
        self.items = [(layer, self.ws.index(w)) for layer, w in items]
        self.inputs = [parts[w] for w in self.ws] + [lands[w] for w in self.ws]
        self.out_shapes = [jax.ShapeDtypeStruct(lands[w].shape, lands[w].dtype) for w in self.ws]
        self.aliases = {n + i: i for i in range(n)}
        self.scratch = [pltpu.SemaphoreType.DMA((3, n)), pltpu.SemaphoreType.DMA((3, n))]

    def bind(self, in_refs, out_refs, scratch_refs):
        send_sems, recv_sems = scratch_refs
        x, y, c = _place()
        chips = [(1 - x, y), (x, 1 - y), (1 - x, 1 - y)]
        return _Copies([pltpu.make_async_remote_copy(
            src_ref=in_refs[i].at[layer, 2 * px + py], dst_ref=out_refs[i].at[layer, j],
            send_sem=send_sems.at[j, i], recv_sem=recv_sems.at[j, i], device_id=(px, py, c), device_id_type=MESH)
            for layer, i in self.items for j, (px, py) in enumerate(chips)])


class PairJob:
    def __init__(self, grads):
        n = len(grads)
        self.inputs = list(grads)
        self.out_shapes = [jax.ShapeDtypeStruct((N_CHIPS,) + g.shape[1:], g.dtype) for g in grads]
        self.aliases = {}
        self.scratch = [pltpu.SemaphoreType.DMA((N_CHIPS, n)), pltpu.SemaphoreType.DMA((N_CHIPS, n))]

    def bind(self, in_refs, out_refs, scratch_refs):
        send_sems, recv_sems = scratch_refs
        x, y, c = _place()
        return _Copies([pltpu.make_async_remote_copy(
            src_ref=in_refs[w].at[2 * k + (1 - c)], dst_ref=out_refs[w].at[k], send_sem=send_sems.at[k, w],
            recv_sem=recv_sems.at[k, w], device_id=(x, y, 1 - c), device_id_type=MESH)
            for w in range(len(in_refs)) for k in range(N_CHIPS)])


def _call(body, *, name, grid, in_specs, out_specs, out_shape, args, scratch_shapes=(), ride=()):
    out_specs, out_shape, in_specs = tuple(out_specs), tuple(out_shape), list(in_specs)
    scratch_shapes = list(scratch_shapes)
    order = ("arbitrary",) * len(grid)
    if not ride:
        outs = pl.pallas_call(body, name=name, grid=grid, in_specs=in_specs, out_specs=out_specs, out_shape=out_shape,
                              scratch_shapes=scratch_shapes, compiler_params=_params(*order))(*args)
        return tuple(outs), []
    n_in, n_out, n_scr = len(in_specs), len(out_specs), len(scratch_shapes)
    n_steps = math.prod(grid)
    relay_early = n_steps >= 8
    relay_at = n_steps - n_steps // 4 if relay_early else n_steps - 1

    def split(refs, pos, counts):
        groups = []
        for k in counts:
            groups.append(refs[pos:pos + k])
            pos += k
        return groups, pos

    def wrapped(*refs):
        ins, pos = refs[:n_in], n_in
        job_in, pos = split(refs, pos, [len(j.inputs) for j in ride])
        outs, pos = refs[pos:pos + n_out], pos + n_out
        job_out, pos = split(refs, pos, [len(j.out_shapes) for j in ride])
        scr, pos = refs[pos:pos + n_scr], pos + n_scr
        job_scr, pos = split(refs, pos, [len(j.scratch) for j in ride])
        bound = [j.bind(i, o, s) for j, i, o, s in zip(ride, job_in, job_out, job_scr)]
        step = pl.program_id(0)
        for axis in range(1, len(grid)):
            step = step * grid[axis] + pl.program_id(axis)

        @pl.when(step == 0)
        def _():
            for b in bound:
                b.start()

        if relay_early:
            @pl.when(step == relay_at)
            def _():
                for b in bound:
                    b.relay()

        body(*ins, *outs, *scr)

        @pl.when(step == n_steps - 1)
        def _():
            if not relay_early:
                for b in bound:
                    b.relay()
            for b in bound:
                b.finish()

    aliases, in_pos, out_pos = {}, n_in, n_out
    for j in ride:
        aliases.update({in_pos + i: out_pos + o for i, o in j.aliases.items()})
        in_pos += len(j.inputs)
        out_pos += len(j.out_shapes)
    results = pl.pallas_call(
        wrapped, name=name, grid=grid, in_specs=in_specs + [ANY] * (in_pos - n_in),
        out_specs=out_specs + (ANY,) * (out_pos - n_out),
        out_shape=out_shape + tuple(s for j in ride for s in j.out_shapes),
        scratch_shapes=scratch_shapes + [s for j in ride for s in j.scratch], input_output_aliases=aliases,
        compiler_params=pltpu.CompilerParams(dimension_semantics=order, vmem_limit_bytes=VMEM_LIMIT,
                                             has_side_effects=True),
    )(*args, *[a for j in ride for a in j.inputs])
    job_results, pos = split(list(results), n_out, [len(j.out_shapes) for j in ride])
    return tuple(results[:n_out]), job_results


def exchange_alone(job, *, name):
    n_in, n_out = len(job.inputs), len(job.out_shapes)

    def body(*refs):
        b = job.bind(refs[:n_in], refs[n_in:n_in + n_out], refs[n_in + n_out:])
        b.start()
        b.relay()
        b.finish()

    return list(pl.pallas_call(
        body, name=name, out_shape=tuple(job.out_shapes), in_specs=[ANY] * n_in, out_specs=(ANY,) * n_out,
        scratch_shapes=job.scratch, input_output_aliases=job.aliases,
        compiler_params=pltpu.CompilerParams(has_side_effects=True),
    )(*job.inputs))


PAIR_SUM_CHUNKS = 1


def pair_sum(layer, grads, landed, parts, core, *, name):
    n = len(grads)

    def body(c_ref, *refs):
        g_refs, l_refs, o_refs = refs[:n], refs[n:2 * n], refs[3 * n:]
        for w in range(n):
            o_refs[w][...] = (g_refs[w][...].astype(F32) + l_refs[w][...].astype(F32)).astype(BF16)

    def blk(g):
        return (None, g.shape[1] // PAIR_SUM_CHUNKS, g.shape[2])

    in_specs = [pl.BlockSpec(blk(g), lambda k, i, c_ref: (2 * k + c_ref[0], i, 0)) for g in grads]
    in_specs += [pl.BlockSpec(blk(g), lambda k, i, c_ref: (k, i, 0)) for g in grads]
    in_specs += [ANY] * n
    out_specs = tuple(pl.BlockSpec((None,) + blk(g), lambda k, i, c_ref: (layer, k, i, 0)) for g in grads)
    return list(pl.pallas_call(
        body, name=name, out_shape=tuple(jax.ShapeDtypeStruct(p.shape, p.dtype) for p in parts),
        grid_spec=pltpu.PrefetchScalarGridSpec(num_scalar_prefetch=1, grid=(N_CHIPS, PAIR_SUM_CHUNKS),
                                               in_specs=in_specs, out_specs=out_specs),
        input_output_aliases={1 + 2 * n + w: w for w in range(n)},
        compiler_params=_params("parallel", "parallel"),
    )(core, *grads, *landed, *parts))


def _adamw(w, g, m, v):
    m = ADAM_B1 * m + (1.0 - ADAM_B1) * g
    v = ADAM_B2 * v + (1.0 - ADAM_B2) * (g * g)
    m_hat = m / (1.0 - ADAM_B1 ** ADAM_STEP)
    v_hat = v / (1.0 - ADAM_B2 ** ADAM_STEP)
    delta = -ADAM_LR * (m_hat / (jnp.sqrt(v_hat) + ADAM_EPS) + ADAM_WD * w)
    return delta, m, v


def reduce_adamw(part, land, chip, w, m, v, *, name):
    _, r, c = w.shape
    tr = 256 if r % 256 == 0 else (r // 2 if r > 256 else r)

    def body(k_ref, own_ref, l0_ref, l1_ref, l2_ref, w_ref, m_ref, v_ref, g_out, d_out, m_out, v_out):
        g = own_ref[...].astype(F32) + l0_ref[...].astype(F32) + l1_ref[...].astype(F32) + l2_ref[...].astype(F32)
        delta, m_new, v_new = _adamw(w_ref[...], g, m_ref[...], v_ref[...])
        g_out[...] = g
        d_out[...] = delta
        m_out[...] = m_new
        v_out[...] = v_new

    row = pl.BlockSpec((None, tr, c), lambda l, i, k_ref: (l, i, 0))

    def slot(j):
        return pl.BlockSpec((None, None, tr, c), lambda l, i, k_ref: (l, j, i, 0))

    return pl.pallas_call(
        body, name=name, out_shape=(jax.ShapeDtypeStruct(w.shape, F32),) * 4,
        grid_spec=pltpu.PrefetchScalarGridSpec(
            num_scalar_prefetch=1, grid=(DEPTH, r // tr),
            in_specs=[pl.BlockSpec((None, None, tr, c), lambda l, i, k_ref: (l, k_ref[0], i, 0)), slot(0), slot(1),
                      slot(2), row, row, row],
            out_specs=(row, row, row, row)),
        compiler_params=_params("parallel", "parallel"),
    )(chip, part, land, land, land, w, m, v)


def gather_small(block, *, name):
    def body(x_ref, out_ref, send_sems, recv_sems, local_sem):
        x, y, c = _place()
        me = 4 * x + 2 * y + c
        mine = pltpu.make_async_copy(x_ref, out_ref.at[me], local_sem)
        mine.start()
        peers = [(x ^ (k >> 2), y ^ ((k >> 1) & 1), c ^ (k & 1)) for k in range(1, N_DEV)]
        copies = [pltpu.make_async_remote_copy(
            src_ref=x_ref, dst_ref=out_ref.at[me], send_sem=send_sems.at[k], recv_sem=recv_sems.at[k],
            device_id=peer, device_id_type=MESH) for k, peer in enumerate(peers)]
        for cp in copies:
            cp.start()
        for k, (px, py, pc) in enumerate(peers):
            pltpu.make_async_remote_copy(
                src_ref=x_ref, dst_ref=out_ref.at[4 * px + 2 * py + pc], send_sem=send_sems.at[k],
                recv_sem=recv_sems.at[k], device_id=(px, py, pc), device_id_type=MESH).wait_recv()
        for cp in copies:
            cp.wait_send()
        mine.wait()

    return pl.pallas_call(
        body, name=name, out_shape=jax.ShapeDtypeStruct((N_DEV,) + block.shape, block.dtype),
        in_specs=[ANY], out_specs=ANY,
        scratch_shapes=[pltpu.SemaphoreType.DMA((7,)), pltpu.SemaphoreType.DMA((7,)), pltpu.SemaphoreType.DMA],
        compiler_params=pltpu.CompilerParams(has_side_effects=True),
    )(block)


def small_adamw(gathered, w, m, v, *, name):
    def body(g_ref, w_ref, m_ref, v_ref, g_out, d_out, m_out, v_out):
        g = g_ref[0]
        for d in range(1, N_DEV):
            g = g + g_ref[d]
        delta, m_new, v_new = _adamw(w_ref[...], g, m_ref[...], v_ref[...])
        g_out[...] = g
        d_out[...] = delta
        m_out[...] = m_new
        v_out[...] = v_new

    return pl.pallas_call(
        body, name=name, out_shape=(jax.ShapeDtypeStruct(w.shape, F32),) * 4,
    )(gathered, w, m, v)


def norm_matmul(x, g, w, *, gate_split, name, ride=()):
    s, d = x.shape
    tm = min(ROW_TILE, s)
    blocked = w.ndim == 3
    n = w.shape[0] if not blocked else w.shape[0] * w.shape[2]

    def body(x_ref, g_ref, w_ref, h_ref, *outs):
        xv = x_ref[...]
        h = ((xv * _rsqrt_ms(xv)) * g_ref[...]).astype(BF16)
        h_ref[...] = h
        if blocked:
            nb = w_ref.shape[2]
            for j in range(w_ref.shape[0]):
                outs[0][:, j * nb:(j + 1) * nb] = _dot(h, w_ref[j]).astype(BF16)
        else:
            p = _dot_nt(h, w_ref[...])
            outs[0][...] = p[:, :gate_split].astype(BF16)
            outs[1][...] = (1.0 / (1.0 + jnp.exp(-p[:, gate_split:]))).astype(BF16)

    row = lambda i: (i, 0)
    fixed = lambda i: (0, 0)
    if blocked:
        out_shape = (jax.ShapeDtypeStruct((s, d), BF16), jax.ShapeDtypeStruct((s, n), BF16))
        out_specs = (pl.BlockSpec((tm, d), row), pl.BlockSpec((tm, n), row))
        w_spec = pl.BlockSpec(w.shape, lambda i: (0, 0, 0))
    else:
        out_shape = (jax.ShapeDtypeStruct((s, d), BF16), jax.ShapeDtypeStruct((s, gate_split), BF16),
                     jax.ShapeDtypeStruct((s, n - gate_split), BF16))
        out_specs = (pl.BlockSpec((tm, d), row), pl.BlockSpec((tm, gate_split), row),
                     pl.BlockSpec((tm, n - gate_split), row))
        w_spec = pl.BlockSpec((n, d), fixed)
    return _call(body, name=name, grid=(s // tm,), out_shape=out_shape, out_specs=out_specs,
                 in_specs=[pl.BlockSpec((tm, d), row), pl.BlockSpec((1, d), fixed), w_spec], args=(x, g, w), ride=ride)


def merge_out_fwd(x, o_sb, o_sw, gates, w_bsb, w_bsw, w_o, *, name):
    s, d = x.shape
    tm = min(ROW_TILE, s)

    def body(x_ref, osb_ref, osw_ref, g_ref, wsb_ref, wsw_ref, wo_ref, x1_ref, ysb_ref, ysw_ref, mg_ref):
        y_sb = _dot(osb_ref[...].astype(BF16), wsb_ref[...])
        y_sw = _dot(osw_ref[...].astype(BF16), wsw_ref[...])
        g = g_ref[...].astype(F32)
        merged = (g[:, :d] * y_sb + g[:, d:] * y_sw).astype(BF16)
        ysb_ref[...] = y_sb.astype(BF16)
        ysw_ref[...] = y_sw.astype(BF16)
        mg_ref[...] = merged
        x1_ref[...] = x_ref[...] + _dot(merged, wo_ref[...])

    row = lambda i: (i, 0)
    fixed = lambda i: (0, 0)
    wd = o_sb.shape[1]
    return pl.pallas_call(
        body, name=name, grid=(s // tm,),
        out_shape=(jax.ShapeDtypeStruct((s, d), F32),) + (jax.ShapeDtypeStruct((s, d), BF16),) * 3,
        in_specs=[pl.BlockSpec((tm, d), row), pl.BlockSpec((tm, wd), row), pl.BlockSpec((tm, wd), row),
                  pl.BlockSpec((tm, 2 * d), row), pl.BlockSpec((wd, d), fixed), pl.BlockSpec((wd, d), fixed),
                  pl.BlockSpec((d, d), fixed)],
        out_specs=(pl.BlockSpec((tm, d), row),) * 4, compiler_params=_params("parallel"),
    )(x, o_sb, o_sw, gates, w_bsb, w_bsw, w_o)


def mlp_down_fwd(x1, u, w_down, *, name):
    s, d = x1.shape
    f = u.shape[1]
    tm = min(ROW_TILE, s)

    def body(x_ref, u_ref, w_ref, o_ref):
        a = jnp.maximum(u_ref[...].astype(F32), 0.0)
        o_ref[...] = x_ref[...] + _dot((a * a).astype(BF16), w_ref[...])

    row = lambda i: (i, 0)
    return pl.pallas_call(
        body, name=name, grid=(s // tm,), out_shape=jax.ShapeDtypeStruct((s, d), F32),
        in_specs=[pl.BlockSpec((tm, d), row), pl.BlockSpec((tm, f), row), pl.BlockSpec((f, d), lambda i: (0, 0))],
        out_specs=pl.BlockSpec((tm, d), row), compiler_params=_params("parallel"),
    )(x1, u, w_down)


def loss_head(y, target, *, name):
    s, d = y.shape
    tm = min(ROW_TILE, s)

    def body(y_ref, t_ref, dy_ref, dyb_ref, loss_ref):
        @pl.when(pl.program_id(0) == 0)
        def _():
            loss_ref[...] = jnp.zeros_like(loss_ref)

        e = y_ref[...] - t_ref[...]
        dy = e * (1.0 / d)
        dy_ref[...] = dy
        dyb_ref[...] = dy.astype(BF16)
        per_row = jnp.sum(e * e, axis=1, keepdims=True) * (0.5 / d)
        loss_ref[...] += jnp.sum(per_row, axis=0, keepdims=True)

    row = lambda i: (i, 0)
    return pl.pallas_call(
        body, name=name, grid=(s // tm,),
        out_shape=(jax.ShapeDtypeStruct((s, d), F32), jax.ShapeDtypeStruct((s, d), BF16),
                   jax.ShapeDtypeStruct((1, 1), F32)),
        in_specs=[pl.BlockSpec((tm, d), row), pl.BlockSpec((tm, d), row)],
        out_specs=(pl.BlockSpec((tm, d), row), pl.BlockSpec((tm, d), row), pl.BlockSpec((1, 1), lambda i: (0, 0))),
        compiler_params=_params("arbitrary"),
    )(y, target)


def mlp_bwd_up(dxb, u, w_down, *, name, ride=()):
    s, d = dxb.shape
    f = u.shape[1]
    tm = min(ROW_TILE, s)

    def body(dx_ref, u_ref, w_ref, du_ref):
        da = _dot_nt(dx_ref[...], w_ref[...])
        du_ref[...] = (da * (2.0 * jnp.maximum(u_ref[...].astype(F32), 0.0))).astype(BF16)

    row = lambda i: (i, 0)
    return _call(body, name=name, grid=(s // tm,), out_shape=(jax.ShapeDtypeStruct((s, f), BF16),),
                 in_specs=[pl.BlockSpec((tm, d), row), pl.BlockSpec((tm, f), row),
                           pl.BlockSpec((f, d), lambda i: (0, 0))],
                 out_specs=(pl.BlockSpec((tm, f), row),), args=(dxb, u, w_down), ride=ride)


def matmul_nt_norm_bwd(pieces, w, x, g, dres, *, name, ride=()):
    s = x.shape[0]
    d = x.shape[1]
    tm = min(ROW_TILE, s)
    blocked = w.ndim == 3
    n_pieces = len(pieces)
    widths = [p.shape[1] for p in pieces]

    def body(*refs):
        p_refs = refs[:n_pieces]
        w_ref, x_ref, g_ref, dres_ref, dx_ref, dxb_ref, dg_ref = refs[n_pieces:]

        @pl.when(pl.program_id(0) == 0)
        def _():
            dg_ref[...] = jnp.zeros_like(dg_ref)

        if blocked:
            nb = w_ref.shape[2]
            dh = _dot_nt(p_refs[0][:, :nb], w_ref[0])
            for j in range(1, w_ref.shape[0]):
                dh = dh + _dot_nt(p_refs[0][:, j * nb:(j + 1) * nb], w_ref[j])
        else:
            dh, off = None, 0
            for p_ref, width in zip(p_refs, widths):
                part = _dot(p_ref[...], w_ref[off:off + width, :])
                dh = part if dh is None else dh + part
                off += width
        xv = x_ref[...]
        r = _rsqrt_ms(xv)
        dyg = dh * g_ref[...]
        dx = dres_ref[...] + r * dyg - xv * ((r * r * r) * jnp.mean(dyg * xv, axis=-1, keepdims=True))
        dx_ref[...] = dx
        dxb_ref[...] = dx.astype(BF16)
        dg_ref[...] += jnp.sum(dh * (xv * r), axis=0, keepdims=True)

    row = lambda i: (i, 0)
    fixed = lambda i: (0, 0)
    w_spec = pl.BlockSpec(w.shape, (lambda i: (0, 0, 0)) if blocked else fixed)
    return _call(
        body, name=name, grid=(s // tm,),
        out_shape=(jax.ShapeDtypeStruct((s, d), F32), jax.ShapeDtypeStruct((s, d), BF16),
                   jax.ShapeDtypeStruct((1, d), F32)),
        in_specs=[pl.BlockSpec((tm, width), row) for width in widths] + [
            w_spec, pl.BlockSpec((tm, d), row), pl.BlockSpec((1, d), fixed), pl.BlockSpec((tm, d), row)],
        out_specs=(pl.BlockSpec((tm, d), row), pl.BlockSpec((tm, d), row), pl.BlockSpec((1, d), fixed)),
        args=(*pieces, w, x, g, dres), ride=ride)


def out_bwd(dx1b, w_o, gates, y_sb, y_sw, w_bsb, w_bsw, *, name):
    s, d = dx1b.shape
    wd = w_bsb.shape[0]
    tm = min(ROW_TILE, s)

    def body(dx_ref, wo_ref, g_ref, ysb_ref, ysw_ref, wsb_ref, wsw_ref, dysb_ref, dysw_ref, dosb_ref, dosw_ref, dgl_ref):
        dm = _dot_nt(dx_ref[...], wo_ref[...])
        g = g_ref[...].astype(F32)
        g0, g1 = g[:, :d], g[:, d:]
        dy_sb = (dm * g0).astype(BF16)
        dy_sw = (dm * g1).astype(BF16)
        dysb_ref[...] = dy_sb
        dysw_ref[...] = dy_sw
        dosb_ref[...] = _dot_nt(dy_sb, wsb_ref[...])
        dosw_ref[...] = _dot_nt(dy_sw, wsw_ref[...])
        dgl_ref[:, :d] = (dm * ysb_ref[...].astype(F32) * (g0 * (1.0 - g0))).astype(BF16)
        dgl_ref[:, d:] = (dm * ysw_ref[...].astype(F32) * (g1 * (1.0 - g1))).astype(BF16)

    row = lambda i: (i, 0)
    fixed = lambda i: (0, 0)
    return pl.pallas_call(
        body, name=name, grid=(s // tm,),
        out_shape=(jax.ShapeDtypeStruct((s, d), BF16), jax.ShapeDtypeStruct((s, d), BF16),
                   jax.ShapeDtypeStruct((s, wd), F32), jax.ShapeDtypeStruct((s, wd), F32),
                   jax.ShapeDtypeStruct((s, 2 * d), BF16)),
        in_specs=[pl.BlockSpec((tm, d), row), pl.BlockSpec((d, d), fixed), pl.BlockSpec((tm, 2 * d), row),
                  pl.BlockSpec((tm, d), row), pl.BlockSpec((tm, d), row), pl.BlockSpec((wd, d), fixed),
                  pl.BlockSpec((wd, d), fixed)],
        out_specs=(pl.BlockSpec((tm, d), row), pl.BlockSpec((tm, d), row), pl.BlockSpec((tm, wd), row),
                   pl.BlockSpec((tm, wd), row), pl.BlockSpec((tm, 2 * d), row)),
        compiler_params=_params("parallel"),
    )(dx1b, w_o, gates, y_sb, y_sw, w_bsb, w_bsw)


def matmul_tn(a, pieces, *, a_block, out_cols, relu2, name):
    s, m = a.shape
    widths = [p.shape[1] for p in pieces]
    n = sum(widths)
    n_pieces = len(pieces)
    ts = min(512 if n >= 4096 else 2048, s)
    n_steps = s // ts
    if out_cols is None:
        out_shape = jax.ShapeDtypeStruct((m // a_block, a_block, n), BF16)
        out_spec = pl.BlockSpec((None, a_block, n), lambda i, k: (i, 0, 0))
    else:
        out_shape = jax.ShapeDtypeStruct((n // out_cols, m, out_cols), BF16)
        out_spec = pl.BlockSpec((n // out_cols, a_block, out_cols), lambda i, k: (0, i, 0))

    def body(a_ref, *refs):
        b_refs, o_ref, acc = refs[:n_pieces], refs[n_pieces], refs[n_pieces + 1]
        k = pl.program_id(1)

        @pl.when(k == 0)
        def _():
            acc[...] = jnp.zeros_like(acc)

        av = a_ref[...]
        if relu2:
            af = jnp.maximum(av.astype(F32), 0.0)
            av = af * af
        av = av.astype(BF16)
        off = 0
        for b_ref in b_refs:
            width = b_ref.shape[1]
            acc[:, off:off + width] += _dot_tn(av, b_ref[...].astype(BF16))
            off += width

        @pl.when(k == n_steps - 1)
        def _():
            if out_cols is None:
                o_ref[...] = acc[...].astype(BF16)
            else:
                for j in range(n // out_cols):
                    o_ref[j] = acc[:, j * out_cols:(j + 1) * out_cols].astype(BF16)

    return pl.pallas_call(
        body, name=name, grid=(m // a_block, n_steps), out_shape=out_shape,
        in_specs=[pl.BlockSpec((ts, a_block), lambda i, k: (k, i))] + [
            pl.BlockSpec((ts, width), lambda i, k: (k, 0)) for width in widths],
        out_specs=out_spec, scratch_shapes=[pltpu.VMEM((a_block, n), F32)],
        compiler_params=_params("parallel", "arbitrary"),
    )(a, *pieces)


def matmul_tn_row_blocks(pieces, b, *, n_blocks, name):
    s, n = b.shape
    widths = [p.shape[1] for p in pieces]
    m = sum(widths)
    rows = m // n_blocks
    n_pieces = len(pieces)
    ts = min(512, s)
    n_steps = s // ts
    half = n_blocks // 2

    def body(*refs):
        p_refs, b_ref, o_ref, a_tile, acc = refs[:n_pieces], refs[n_pieces], refs[n_pieces + 1], refs[-2], refs[-1]
        i, k = pl.program_id(0), pl.program_id(1)

        @pl.when(k == 0)
        def _():
            acc[...] = jnp.zeros_like(acc)

        off = 0
        for p_ref, width in zip(p_refs, widths):
            a_tile[:, off:off + width] = p_ref[...]
            off += width
        bv = b_ref[...]
        for side in range(2):
            @pl.when(i == side)
            def _():
                for j in range(half):
                    col = (side * half + j) * rows
                    acc[j] += _dot_tn(a_tile[:, col:col + rows], bv)

        @pl.when(k == n_steps - 1)
        def _():
            o_ref[...] = acc[...].astype(BF16)

    return pl.pallas_call(
        body, name=name, grid=(2, n_steps), out_shape=jax.ShapeDtypeStruct((n_blocks, rows, n), BF16),
        in_specs=[pl.BlockSpec((ts, width), lambda i, k: (k, 0)) for width in widths] + [
            pl.BlockSpec((ts, n), lambda i, k: (k, 0))],
        out_specs=pl.BlockSpec((half, rows, n), lambda i, k: (i, 0, 0)),
        scratch_shapes=[pltpu.VMEM((ts, m), BF16), pltpu.VMEM((half, rows, n), F32)],
        compiler_params=_params("parallel", "arbitrary"),
    )(*pieces, b)


def _softplus(z):
    return jnp.maximum(z, 0.0) + jnp.log(1.0 + jnp.exp(-jnp.abs(z)))


def _suffix_sums(x, tri2):
    groups = x.shape[1] // LANES
    outs, run = [None] * groups, None
    for g in reversed(range(groups)):
        xg = x[:, g * LANES:(g + 1) * LANES]
        hi, lo = _split_bf16(xg)
        inner = _dot(jnp.concatenate([hi, lo], axis=1), tri2)
        outs[g] = inner if run is None else inner + run
        total = jnp.sum(xg, axis=1, keepdims=True)
        run = total if run is None else run + total
    return jnp.concatenate(outs, axis=1), run


def _head_mask(h):
    return (lax.broadcasted_iota(jnp.int32, (1, LANES), 1) // HEAD_DIM) == h


def _stack_heads(x):
    zero = jnp.zeros_like(x)
    return jnp.concatenate([jnp.where(_head_mask(0), x, zero), jnp.where(_head_mask(1), x, zero)], axis=0)


def _unstack_heads(r, t):
    return jnp.where(_head_mask(0), r[:t], r[t:])


def _sb_positions(q0, tk):
    row = lax.broadcasted_iota(jnp.int32, (2 * SB_TQ, tk), 0)
    col = lax.broadcasted_iota(jnp.int32, (2 * SB_TQ, tk), 1)
    return q0 + jnp.where(row >= SB_TQ, row - SB_TQ, row), col


def _sb_first_key(q0):
    return pl.multiple_of(jnp.maximum(q0 + SB_TQ - SB_TK1, 0), SB_TQ)


def _sb_next_key(k_prev):
    return pl.multiple_of(jnp.maximum(k_prev - SB_TK, 0), SB_TQ)


def _sb_rows(q0):
    return pl.ds(pl.multiple_of(2 * q0, 2 * SB_TQ), 2 * SB_TQ)


def sb_attn_fwd(proj, tri2, *, name, ride=()):
    s = proj.shape[0]
    nq = s // SB_TQ
    n_pairs = SB_WIDTH // LANES

    def body(q_ref, k_ref, v_ref, tri_ref, o_ref, c_all):
        def block(qh, k0, tk, live, c):
            z = _dot_nt(qh, k_ref[pl.ds(k0, tk), :])
            sp = _softplus(z)
            tail, total = _suffix_sums(jnp.where(live, -sp, 0.0), tri_ref[...])
            w = jnp.where(live, jnp.exp(z - sp + tail + c), 0.0)
            return _dot(w.astype(BF16), v_ref[pl.ds(k0, tk), :]), c + total

        def load_q(q0):
            return _stack_heads(q_ref[pl.ds(q0, SB_TQ), :]) * SCALE

        def first(qb, carry):
            q0 = pl.multiple_of(qb * SB_TQ, SB_TQ)
            tpos, col = _sb_positions(q0, SB_TK1)
            k0 = _sb_first_key(q0)
            acc, c = block(load_q(q0), k0, SB_TK1, k0 + col < tpos, jnp.zeros((2 * SB_TQ, 1), F32))
            o_ref[pl.ds(q0, SB_TQ), :] = _unstack_heads(acc, SB_TQ)
            c_all[_sb_rows(q0), :] = jnp.broadcast_to(jnp.where(k0 > 0, c, NEG), (2 * SB_TQ, LANES))
            return carry

        lax.fori_loop(0, nq, first, 0, unroll=2)

        @pl.when(jnp.max(c_all[...]) > SB_CUTOFF)
        def _():
            def more(qb, carry):
                q0 = pl.multiple_of(qb * SB_TQ, SB_TQ)
                c0 = c_all[_sb_rows(q0), 0:1]

                @pl.when(jnp.max(c0) > SB_CUTOFF)
                def _():
                    qh = load_q(q0)
                    _, col = _sb_positions(q0, SB_TK)

                    def cond(st):
                        return jnp.logical_and(st[0] > 0, st[3] > SB_CUTOFF)

                    def step(st):
                        k_prev, c, acc, _ = st
                        k0 = _sb_next_key(k_prev)
                        part, c = block(qh, k0, SB_TK, k0 + col < k_prev, c)
                        return k0, c, acc + part, jnp.max(c)

                    st = lax.while_loop(cond, step, (_sb_first_key(q0), c0, jnp.zeros((2 * SB_TQ, LANES), F32),
                                                     jnp.max(c0)))
                    o_ref[pl.ds(q0, SB_TQ), :] += _unstack_heads(st[2], SB_TQ)

                return carry

            lax.fori_loop(0, nq, more, 0)

    def col_spec(j):
        return pl.BlockSpec((s, LANES), lambda p: (0, j * n_pairs + p))

    (o,), rides = _call(
        body, name=name, grid=(n_pairs,), out_shape=(jax.ShapeDtypeStruct((s, SB_WIDTH), F32),),
        in_specs=[col_spec(0), col_spec(1), col_spec(2), pl.BlockSpec((2 * LANES, LANES), lambda p: (0, 0))],
        out_specs=(pl.BlockSpec((s, LANES), lambda p: (0, p)),), scratch_shapes=[pltpu.VMEM((2 * s, LANES), F32)],
        args=(proj, proj, proj, tri2), ride=ride)
    return o, rides


def sb_attn_bwd(proj, tri2, o, do, *, name, ride=()):
    s = proj.shape[0]
    nq = s // SB_TQ
    n_pairs = SB_WIDTH // LANES

    def body(q_ref, k_ref, v_ref, tri_ref, o_ref, do_ref, dq_ref, dk_ref, dv_ref, dq_acc, dk_acc, dv_acc, c_all, e_all):
        dk_acc[...] = jnp.zeros_like(dk_acc)
        dv_acc[...] = jnp.zeros_like(dv_acc)

        def load(q0):
            qh = _stack_heads(q_ref[pl.ds(q0, SB_TQ), :]) * SCALE
            doh_b = _stack_heads(do_ref[pl.ds(q0, SB_TQ), :].astype(BF16))
            ov = o_ref[pl.ds(q0, SB_TQ), :]
            dd = jnp.sum(doh_b.astype(F32) * jnp.concatenate([ov, ov], axis=0), axis=1, keepdims=True)
            return qh, doh_b, dd

        def block(qh, doh_b, dd, k0, tk, live, c, ce):
            kt = k_ref[pl.ds(k0, tk), :]
            z = _dot_nt(qh, kt)
            sp = _softplus(z)
            lb = z - sp
            tail, total = _suffix_sums(jnp.where(live, -sp, 0.0), tri_ref[...])
            wb = jnp.where(live, jnp.exp(lb + tail + c), 0.0).astype(BF16)
            e = wb.astype(F32) * _dot_nt(doh_b, v_ref[pl.ds(k0, tk), :])
            e_tail, e_total = _suffix_sums(e, tri_ref[...])
            dz = jnp.where(live, e - jnp.exp(lb) * (dd - ce - e_tail), 0.0)
            dzb = dz.astype(BF16)
            dk_acc[pl.ds(k0, tk), :] += _dot_tn(dzb, qh)
            dv_acc[pl.ds(k0, tk), :] += _dot_tn(wb, doh_b)
            return _dot(dzb, kt), c + total, ce + e_total

        def first(qb, carry):
            q0 = pl.multiple_of(qb * SB_TQ, SB_TQ)
            qh, doh_b, dd = load(q0)
            tpos, col = _sb_positions(q0, SB_TK1)
            k0 = _sb_first_key(q0)
            zero = jnp.zeros((2 * SB_TQ, 1), F32)
            dq, c, ce = block(qh, doh_b, dd, k0, SB_TK1, k0 + col < tpos, zero, zero)
            dq_acc[pl.ds(q0, SB_TQ), :] = _unstack_heads(dq, SB_TQ)
            c_all[_sb_rows(q0), :] = jnp.broadcast_to(jnp.where(k0 > 0, c, NEG), (2 * SB_TQ, LANES))
            e_all[_sb_rows(q0), :] = jnp.broadcast_to(ce, (2 * SB_TQ, LANES))
            return carry

        lax.fori_loop(0, nq, first, 0, unroll=2)

        @pl.when(jnp.max(c_all[...]) > SB_CUTOFF)
        def _():
            def more(qb, carry):
                q0 = pl.multiple_of(qb * SB_TQ, SB_TQ)
                c0 = c_all[_sb_rows(q0), 0:1]

                @pl.when(jnp.max(c0) > SB_CUTOFF)
                def _():
                    qh, doh_b, dd = load(q0)
                    _, col = _sb_positions(q0, SB_TK)

                    def cond(st):
                        return jnp.logical_and(st[0] > 0, st[4] > SB_CUTOFF)

                    def step(st):
                        k_prev, c, ce, dq, _ = st
                        k0 = _sb_next_key(k_prev)
                        part, c, ce = block(qh, doh_b, dd, k0, SB_TK, k0 + col < k_prev, c, ce)
                        return k0, c, ce, dq + part, jnp.max(c)

                    st = lax.while_loop(cond, step, (_sb_first_key(q0), c0, e_all[_sb_rows(q0), 0:1],
                                                     jnp.zeros((2 * SB_TQ, LANES), F32), jnp.max(c0)))
                    dq_acc[pl.ds(q0, SB_TQ), :] += _unstack_heads(st[3], SB_TQ)

                return carry

            lax.fori_loop(0, nq, more, 0)

        dq_ref[...] = (dq_acc[...] * SCALE).astype(BF16)
        dk_ref[...] = dk_acc[...].astype(BF16)
        dv_ref[...] = dv_acc[...].astype(BF16)

    def col_spec(j):
        return pl.BlockSpec((s, LANES), lambda p: (0, j * n_pairs + p))

    pair = pl.BlockSpec((s, LANES), lambda p: (0, p))
    (dq, dk, dv), rides = _call(
        body, name=name, grid=(n_pairs,), out_shape=(jax.ShapeDtypeStruct((s, SB_WIDTH), BF16),) * 3,
        in_specs=[col_spec(0), col_spec(1), col_spec(2), pl.BlockSpec((2 * LANES, LANES), lambda p: (0, 0)), pair, pair],
        out_specs=(pair, pair, pair),
        scratch_shapes=[pltpu.VMEM((s, LANES), F32)] * 3 + [pltpu.VMEM((2 * s, LANES), F32)] * 2,
        args=(proj, proj, proj, tri2, o, do), ride=ride)
    return dq, dk, dv, rides


def _lane_lo():
    return lax.broadcasted_iota(jnp.int32, (1, LANES), 1) < HEAD_DIM


def _swap_halves(x):
    return pltpu.roll(x, HEAD_DIM, 1)


def _rot_half(y):
    first = (lax.broadcasted_iota(jnp.int32, (1, LANES), 1) % HEAD_DIM) < (HEAD_DIM // 2)
    return jnp.where(first, pltpu.roll(y, LANES - HEAD_DIM // 2, 1), pltpu.roll(y, HEAD_DIM // 2, 1))


def _head_mean(v, avg):
    hi, lo = _split_bf16(v)
    return _dot(hi, avg) + _dot(lo, avg)


def _head_avg_matrix():
    lane = jnp.arange(LANES) // HEAD_DIM
    return ((lane[:, None] == lane[None, :]).astype(F32) * (1.0 / HEAD_DIM)).astype(BF16)


def swa_prep_fwd(proj, cos_p, sin_p, gq, gk, *, name):
    s = proj.shape[0]
    tm = min(512, s)
    q_blk = (3 * SB_WIDTH) // SWA_Q_WIDTH
    k_blk = (3 * SB_WIDTH + SWA_Q_WIDTH) // LANES

    def body(q_ref, k_ref, cos_ref, sin_ref, gq_ref, gk_ref, avg_ref, qn_ref, kn_ref):
        cosv, sinv, avg = cos_ref[...], sin_ref[...], avg_ref[...]

        def norm_rope(xv, g):
            y = (xv * lax.rsqrt(_head_mean(xv * xv, avg) + NORM_EPS)) * g
            return y * cosv + _rot_half(y) * sinv

        for j in range(SWA_Q_WIDTH // LANES):
            sl = slice(j * LANES, (j + 1) * LANES)
            qn_ref[:, sl] = norm_rope(q_ref[:, sl].astype(F32), gq_ref[...]).astype(BF16)
        kn_ref[...] = norm_rope(k_ref[...].astype(F32), gk_ref[...]).astype(BF16)

    row = lambda i: (i, 0)
    fixed = lambda i: (0, 0)
    return pl.pallas_call(
        body, name=name, grid=(s // tm,),
        out_shape=(jax.ShapeDtypeStruct((s, SWA_Q_WIDTH), BF16), jax.ShapeDtypeStruct((s, LANES), BF16)),
        in_specs=[pl.BlockSpec((tm, SWA_Q_WIDTH), lambda i: (i, q_blk)), pl.BlockSpec((tm, LANES), lambda i: (i, k_blk)),
                  pl.BlockSpec((tm, LANES), row), pl.BlockSpec((tm, LANES), row),
                  pl.BlockSpec((1, LANES), fixed), pl.BlockSpec((1, LANES), fixed), pl.BlockSpec((LANES, LANES), fixed)],
        out_specs=(pl.BlockSpec((tm, SWA_Q_WIDTH), row), pl.BlockSpec((tm, LANES), row)),
        compiler_params=_params("parallel"),
    )(proj, proj, cos_p, sin_p, gq, gk, _head_avg_matrix())


def swa_prep_bwd(proj, cos_p, sin_p, gq, gk, dqn, dkn, dv, *, name):
    s = proj.shape[0]
    tm = min(512, s)
    q_blk = (3 * SB_WIDTH) // SWA_Q_WIDTH
    k_blk = (3 * SB_WIDTH + SWA_Q_WIDTH) // LANES

    def body(q_ref, k_ref, cos_ref, sin_ref, gq_ref, gk_ref, avg_ref, dqn_ref, dkn_ref, dv_ref, dq_ref, dk_ref, dvb_ref,
             dgq_ref, dgk_ref):
        @pl.when(pl.program_id(0) == 0)
        def _():
            dgq_ref[...] = jnp.zeros_like(dgq_ref)
            dgk_ref[...] = jnp.zeros_like(dgk_ref)

        cosv, sinv, avg = cos_ref[...], sin_ref[...], avg_ref[...]

        def bwd(xv, g, dout):
            dy = dout * cosv + _rot_half(dout * sinv)
            r = lax.rsqrt(_head_mean(xv * xv, avg) + NORM_EPS)
            dyg = dy * g
            dx = r * dyg - xv * ((r * r * r) * _head_mean(dyg * xv, avg))
            return dx, jnp.sum(dy * (xv * r), axis=0, keepdims=True)

        for j in range(SWA_Q_WIDTH // LANES):
            sl = slice(j * LANES, (j + 1) * LANES)
            dx, dg = bwd(q_ref[:, sl].astype(F32), gq_ref[...], dqn_ref[:, sl])
            dq_ref[:, sl] = dx.astype(BF16)
            dgq_ref[:, sl] += dg
        dx, dg = bwd(k_ref[...].astype(F32), gk_ref[...], dkn_ref[...])
        dk_ref[...] = dx.astype(BF16)
        dgk_ref[...] += dg
        dvb_ref[...] = dv_ref[...].astype(BF16)

    row = lambda i: (i, 0)
    fixed = lambda i: (0, 0)
    lane_row = pl.BlockSpec((tm, LANES), row)
    return pl.pallas_call(
        body, name=name, grid=(s // tm,),
        out_shape=(jax.ShapeDtypeStruct((s, SWA_Q_WIDTH), BF16), jax.ShapeDtypeStruct((s, LANES), BF16),
                   jax.ShapeDtypeStruct((s, LANES), BF16),
                   jax.ShapeDtypeStruct((1, SWA_Q_WIDTH), F32), jax.ShapeDtypeStruct((1, LANES), F32)),
        in_specs=[pl.BlockSpec((tm, SWA_Q_WIDTH), lambda i: (i, q_blk)), pl.BlockSpec((tm, LANES), lambda i: (i, k_blk)),
                  lane_row, lane_row, pl.BlockSpec((1, LANES), fixed), pl.BlockSpec((1, LANES), fixed),
                  pl.BlockSpec((LANES, LANES), fixed), pl.BlockSpec((tm, SWA_Q_WIDTH), row), lane_row, lane_row],
        out_specs=(pl.BlockSpec((tm, SWA_Q_WIDTH), row), lane_row, lane_row,
                   pl.BlockSpec((1, SWA_Q_WIDTH), fixed), pl.BlockSpec((1, LANES), fixed)),
        compiler_params=_params("arbitrary"),
    )(proj, proj, cos_p, sin_p, gq, gk, _head_avg_matrix(), dqn, dkn, dv)


def _swa_kv_copies(k_ref, v_ref, kg_ref, vg_ref, second_kv):
    s = k_ref.shape[0]
    rows = min(512, s)
    keep = jnp.logical_xor(_lane_lo(), second_kv)

    def chunk(r, carry):
        sl = pl.ds(pl.multiple_of(r * rows, rows), rows)
        for src, dst in ((k_ref, kg_ref), (v_ref, vg_ref)):
            f = src[sl, :].astype(F32)
            dst[sl, :] = jnp.where(keep, f, _swap_halves(f)).astype(BF16)
        return carry

    lax.fori_loop(0, s // rows, chunk, 0)


def _swa_tile(i, kg_ref, vg_ref):
    q0 = pl.multiple_of(i * SWA_TQ, SWA_TQ)
    k0 = pl.multiple_of(jnp.maximum(i - 1, 0) * SWA_TQ, SWA_TQ)
    kg = kg_ref[pl.ds(k0, SWA_TK), :]
    vg = vg_ref[pl.ds(k0, SWA_TK), :]
    row = lax.broadcasted_iota(jnp.int32, (2 * SWA_TQ, SWA_TK), 0)
    tpos = q0 + jnp.where(row >= SWA_TQ, row - SWA_TQ, row)
    spos = k0 + lax.broadcasted_iota(jnp.int32, (2 * SWA_TQ, SWA_TK), 1)
    valid = jnp.logical_and(spos <= tpos, spos > tpos - WINDOW)
    return q0, k0, kg, vg, valid


def _swa_probs(qh, kg, valid, sink):
    z = jnp.where(valid, _dot_nt(qh, kg) * SCALE, NEG)
    m = jnp.maximum(jnp.max(z, axis=1, keepdims=True), sink)
    pexp = jnp.exp(z - m)
    psink = jnp.exp(sink - m)
    inv = 1.0 / (jnp.sum(pexp, axis=1, keepdims=True) + psink)
    return pexp * inv, psink * inv


def _stacked_sink(sink_row):
    s0 = jnp.sum(jnp.where(_head_mask(0), sink_row, 0.0), axis=1, keepdims=True) * (1.0 / HEAD_DIM)
    s1 = jnp.sum(jnp.where(_head_mask(1), sink_row, 0.0), axis=1, keepdims=True) * (1.0 / HEAD_DIM)
    top = lax.broadcasted_iota(jnp.int32, (2 * SWA_TQ, 1), 0) < SWA_TQ
    return jnp.where(top, s0, s1)


def swa_attn_fwd(qn, kn, proj, sink_p, *, name, ride=()):
    s = qn.shape[0]
    nq = s // SWA_TQ
    n_pairs = SWA_Q_WIDTH // LANES
    v_blk = (3 * SB_WIDTH + SWA_Q_WIDTH + SWA_KV_WIDTH) // LANES

    def body(q_ref, k_ref, v_ref, s_ref, o_ref, kg_ref, vg_ref):
        _swa_kv_copies(k_ref, v_ref, kg_ref, vg_ref, (pl.program_id(0) // 2) == 1)
        sink = _stacked_sink(s_ref[...])

        def tile(i, carry):
            q0, _, kg, vg, valid = _swa_tile(i, kg_ref, vg_ref)
            probs, _ = _swa_probs(_stack_heads(q_ref[pl.ds(q0, SWA_TQ), :]), kg, valid, sink)
            o_ref[pl.ds(q0, SWA_TQ), :] = _unstack_heads(_dot(probs.astype(BF16), vg), SWA_TQ)
            return carry

        lax.fori_loop(0, nq, tile, 0, unroll=4)

    pair = pl.BlockSpec((s, LANES), lambda p: (0, p))
    whole = pl.BlockSpec((s, LANES), lambda p: (0, 0))
    (o,), rides = _call(
        body, name=name, grid=(n_pairs,), out_shape=(jax.ShapeDtypeStruct((s, SWA_Q_WIDTH), F32),),
        in_specs=[pair, whole, pl.BlockSpec((s, LANES), lambda p: (0, v_blk)),
                  pl.BlockSpec((None, 1, LANES), lambda p: (p, 0, 0))],
        out_specs=(pair,), scratch_shapes=[pltpu.VMEM((s, LANES), BF16)] * 2, args=(qn, kn, proj, sink_p), ride=ride)
    return o, rides


def swa_attn_bwd(qn, kn, proj, sink_p, o, do, *, name, ride=()):
    s = qn.shape[0]
    nq = s // SWA_TQ
    n_pairs = SWA_Q_WIDTH // LANES
    v_blk = (3 * SB_WIDTH + SWA_Q_WIDTH + SWA_KV_WIDTH) // LANES
    fold_rows = min(512, s)

    def body(q_ref, k_ref, v_ref, s_ref, o_ref, do_ref, dq_ref, dk_ref, dv_ref, ds_ref, acc_k, acc_v, kg_ref, vg_ref):
        p = pl.program_id(0)
        _swa_kv_copies(k_ref, v_ref, kg_ref, vg_ref, (p // 2) == 1)
        sink = _stacked_sink(s_ref[...])

        @pl.when(p % 2 == 0)
        def _():
            acc_k[...] = jnp.zeros_like(acc_k)
            acc_v[...] = jnp.zeros_like(acc_v)

        ds_ref[...] = jnp.zeros_like(ds_ref)

        def tile(i, carry):
            q0, k0, kg, vg, valid = _swa_tile(i, kg_ref, vg_ref)
            qh = _stack_heads(q_ref[pl.ds(q0, SWA_TQ), :])
            doh = _stack_heads(do_ref[pl.ds(q0, SWA_TQ), :])
            doh_b = doh.astype(BF16)
            ov = o_ref[pl.ds(q0, SWA_TQ), :]
            delta = jnp.sum(doh * jnp.concatenate([ov, ov], axis=0), axis=1, keepdims=True)
            probs, psink = _swa_probs(qh, kg, valid, sink)
            dz = probs * (_dot_nt(doh_b, vg) - delta)
            dzb = (dz * SCALE).astype(BF16)
            dq_ref[pl.ds(q0, SWA_TQ), :] = _unstack_heads(_dot(dzb, kg), SWA_TQ)
            acc_k[pl.ds(k0, SWA_TK), :] += _dot_tn(dzb, qh)
            acc_v[pl.ds(k0, SWA_TK), :] += _dot_tn(probs.astype(BF16), doh_b)
            pd = psink * delta
            ds_ref[...] -= jnp.where(_head_mask(0), jnp.sum(pd[:SWA_TQ], axis=0, keepdims=True),
                                     jnp.sum(pd[SWA_TQ:], axis=0, keepdims=True))
            return carry

        lax.fori_loop(0, nq, tile, 0, unroll=4)

        def fold_into(first_head):
            def fold(r, carry):
                rows = pl.ds(pl.multiple_of(r * fold_rows, fold_rows), fold_rows)
                for acc, out in ((acc_k, dk_ref), (acc_v, dv_ref)):
                    a = acc[rows, :]
                    both = a + _swap_halves(a)
                    if first_head:
                        out[rows, :] = jnp.where(_lane_lo(), both, 0.0)
                    else:
                        out[rows, :] = jnp.where(_lane_lo(), out[rows, :], both)
                return carry

            lax.fori_loop(0, s // fold_rows, fold, 0)

        @pl.when(p == 1)
        def _():
            fold_into(True)

        @pl.when(p == 3)
        def _():
            fold_into(False)

    pair = pl.BlockSpec((s, LANES), lambda p: (0, p))
    whole = pl.BlockSpec((s, LANES), lambda p: (0, 0))
    sink_spec = pl.BlockSpec((None, 1, LANES), lambda p: (p, 0, 0))
    (dq, dk, dv, dsink), rides = _call(
        body, name=name, grid=(n_pairs,),
        out_shape=(jax.ShapeDtypeStruct((s, SWA_Q_WIDTH), F32), jax.ShapeDtypeStruct((s, LANES), F32),
                   jax.ShapeDtypeStruct((s, LANES), F32), jax.ShapeDtypeStruct((n_pairs, 1, LANES), F32)),
        in_specs=[pair, whole, pl.BlockSpec((s, LANES), lambda p: (0, v_blk)), sink_spec, pair, pair],
        out_specs=(pair, whole, whole, sink_spec),
        scratch_shapes=[pltpu.VMEM((s, LANES), F32)] * 2 + [pltpu.VMEM((s, LANES), BF16)] * 2,
        args=(qn, kn, proj, sink_p, o, do), ride=ride)
    return dq, dk, dv, dsink, rides


def _rope_tables(s):
    inv_freq = 1.0 / (ROPE_THETA ** (jnp.arange(0, HEAD_DIM, 2, dtype=F32) / HEAD_DIM))
    ang = jnp.arange(s, dtype=F32)[:, None] * inv_freq[None, :]
    cos, sin = jnp.cos(ang), jnp.sin(ang)
    cos_p = jnp.tile(jnp.concatenate([cos, cos], axis=1), (1, LANES // HEAD_DIM))
    sin_p = jnp.tile(jnp.concatenate([-sin, sin], axis=1), (1, LANES // HEAD_DIM))
    return cos_p, sin_p


def _lane_tile(v, reps):
    return jnp.tile(v.reshape(1, -1), (1, reps))


def _natural(stack, w):
    n, r, c = stack.shape
    if MATRIX_NAMES[w] in ROW_SHARDED or w == W_IN:
        return stack.reshape(n * r, c)
    if w == W_UP:
        return stack
    return jnp.transpose(stack, (1, 0, 2)).reshape(r, n * c)


def _pack_small(tree):
    flat = jnp.concatenate([tree[n].reshape(-1) for n in SMALL_NAMES])
    rows = -(-flat.shape[0] // (8 * LANES)) * 8
    return jnp.pad(flat, (0, rows * LANES - flat.shape[0])).reshape(rows, LANES)


def _unpack_small(packed, shapes):
    flat, out, off = packed.reshape(-1), {}, 0
    for n in SMALL_NAMES:
        size = shapes[n][0] * shapes[n][1]
        out[n] = flat[off:off + size].reshape(shapes[n])
        off += size
    return out


def train_step(x, target, weights, mom_m, mom_v):
    s = x.shape[0]
    cos_p, sin_p = _rope_tables(s)
    tri = (jnp.arange(LANES)[:, None] > jnp.arange(LANES)[None, :]).astype(BF16)
    tri = jnp.concatenate([tri, tri], axis=0)
    local = {n: (jnp.swapaxes(t, 1, 2) if n == "w_in" else t) for n, t in weights.items()}
    local_m = {n: (jnp.swapaxes(t, 1, 2) if n == "w_in" else t) for n, t in mom_m.items()}
    local_v = {n: (jnp.swapaxes(t, 1, 2) if n == "w_in" else t) for n, t in mom_v.items()}
    shards = [[local[n][l].astype(BF16) for n in MATRIX_NAMES] for l in range(DEPTH)]
    core = lax.axis_index("c").astype(jnp.int32).reshape(1)
    chip = (2 * lax.axis_index("x") + lax.axis_index("y")).astype(jnp.int32).reshape(1)

    def gather(l, ws):
        return GatherJob([shards[l][w] for w in ws])

    w_in = _natural(exchange_alone(gather(0, [W_IN]), name="gather_w_in0")[0], W_IN)
    saved = []
    for l in range(DEPTH):
        g_mix = weights["mix_norm_g"][l].reshape(1, D_MODEL)
        g_mlp = weights["mlp_norm_g"][l].reshape(1, D_MODEL)
        gq = _lane_tile(weights["q_norm_g"][l], LANES // HEAD_DIM)
        gk = _lane_tile(weights["k_norm_g"][l], LANES // HEAD_DIM)
        sink_p = jnp.repeat(weights["sinks"][l].reshape(SWA_Q_WIDTH // LANES, 2), HEAD_DIM, axis=1)
        sink_p = sink_p.reshape(SWA_Q_WIDTH // LANES, 1, LANES)
        (h, proj, gates), ((s_bsb, s_bsw, s_out),) = norm_matmul(
            x, g_mix, w_in, gate_split=ATTN_WIDTH, name="in_proj", ride=[gather(l, [W_BSB, W_BSW, W_OUT])])
        if l + 1 < DEPTH:
            o_sb, ((s_up,), (s_in,)) = sb_attn_fwd(proj, tri, name="sb_fwd",
                                                   ride=[gather(l, [W_UP]), gather(l + 1, [W_IN])])
        else:
            o_sb, ((s_up,),) = sb_attn_fwd(proj, tri, name="sb_fwd_last", ride=[gather(l, [W_UP])])
        qn, kn = swa_prep_fwd(proj, cos_p, sin_p, gq, gk, name="swa_prep")
        o_sw, ((s_down,),) = swa_attn_fwd(qn, kn, proj, sink_p, name="swa_fwd", ride=[gather(l, [W_DOWN])])
        mats = [w_in, _natural(s_bsb, W_BSB), _natural(s_bsw, W_BSW), _natural(s_out, W_OUT), _natural(s_up, W_UP),
                _natural(s_down, W_DOWN)]
        x1, y_sb, y_sw, merged = merge_out_fwd(x, o_sb, o_sw, gates, mats[W_BSB], mats[W_BSW], mats[W_OUT],
                                               name="merge_out")
        (h2, u), _ = norm_matmul(x1, g_mlp, mats[W_UP], gate_split=None, name="mlp_up")
        x2 = mlp_down_fwd(x1, u, mats[W_DOWN], name="mlp_down")
        if l + 1 < DEPTH:
            w_in = _natural(s_in, W_IN)
        saved.append(dict(x=x, h=h, proj=proj, gates=gates, o_sb=o_sb, qn=qn, kn=kn, o_sw=o_sw, y_sb=y_sb, y_sw=y_sw,
                          merged=merged, x1=x1, h2=h2, u=u, g_mix=g_mix, g_mlp=g_mlp, gq=gq, gk=gk, sink_p=sink_p,
                          mats=mats))
        x = x2

    dx, dxb, loss = loss_head(x, target, name="loss_head")

    shard_shapes = [local[n].shape[1:] for n in MATRIX_NAMES]
    parts = [lax.empty((DEPTH, N_CHIPS) + sh, BF16) for sh in shard_shapes]
    lands = [lax.empty((DEPTH, 3) + sh, BF16) for sh in shard_shapes]
    small_grads = {n: [None] * DEPTH for n in SMALL_NAMES}
    half = D_MODEL // 2

    def summed(l, ws, grads, landed):
        new = pair_sum(l, grads, landed, [parts[w] for w in ws], core, name="grad_pair_sum")
        for w, p in zip(ws, new):
            parts[w] = p

    def chip_job(items):
        return ChipJob(items, parts, lands)

    def landed_chip(job, outs):
        for w, a in zip(job.ws, outs):
            lands[w] = a

    in_pending = None
    for l in reversed(range(DEPTH)):
        a = saved[l]
        mats = a["mats"]
        (du,), _ = mlp_bwd_up(dxb, a["u"], mats[W_DOWN], name="mlp_bwd_up")
        dw_down = matmul_tn(a["u"], [dxb], a_block=half, out_cols=None, relu2=True, name="dw_down")
        dw_up = matmul_tn(a["h2"], [du], a_block=half, out_cols=du.shape[1] // N_DEV, relu2=False, name="dw_up")
        g_mlp_w = [dw_up, dw_down.reshape((N_DEV,) + shard_shapes[W_DOWN])]
        (dx1, dx1b, dg_mlp), (landed,) = matmul_nt_norm_bwd([du], mats[W_UP], a["x1"], a["g_mlp"], dx,
                                                            name="mlp_bwd_norm", ride=[PairJob(g_mlp_w)])
        summed(l, [W_UP, W_DOWN], g_mlp_w, landed)
        small_grads["mlp_norm_g"][l] = dg_mlp.reshape(D_MODEL)
        dw_out = matmul_tn(a["merged"], [dx1b], a_block=half, out_cols=None, relu2=False, name="dw_out")
        dy_sb, dy_sw, do_sb, do_sw, dgl = out_bwd(dx1b, mats[W_OUT], a["gates"], a["y_sb"], a["y_sw"],
                                                  mats[W_BSB], mats[W_BSW], name="out_bwd")
        dw_bsb = matmul_tn(a["o_sb"], [dy_sb], a_block=half, out_cols=D_MODEL // N_DEV, relu2=False, name="dw_branch_sb")
        dw_bsw = matmul_tn(a["o_sw"], [dy_sw], a_block=half, out_cols=D_MODEL // N_DEV, relu2=False, name="dw_branch_swa")
        g_mix_w = [dw_bsb, dw_bsw, dw_out.reshape((N_DEV,) + shard_shapes[W_OUT])]
        job = chip_job([(l, W_UP), (l, W_DOWN)])
        dq_sb, dk_sb, dv_sb, (outs, landed) = sb_attn_bwd(a["proj"], tri, a["o_sb"], do_sb, name="sb_bwd",
                                                         ride=[job, PairJob(g_mix_w)])
        landed_chip(job, outs)
        summed(l, [W_BSB, W_BSW, W_OUT], g_mix_w, landed)
        job = chip_job([(l, W_BSB), (l, W_BSW), (l, W_OUT)] + ([(in_pending, W_IN)] if in_pending is not None else []))
        dqn, dkn, dv_sw, dsink, (outs,) = swa_attn_bwd(a["qn"], a["kn"], a["proj"], a["sink_p"], a["o_sw"], do_sw,
                                                      name="swa_bwd", ride=[job])
        landed_chip(job, outs)
        dq_sw, dk_sw, dv_swb, dgq, dgk = swa_prep_bwd(a["proj"], cos_p, sin_p, a["gq"], a["gk"], dqn, dkn, dv_sw,
                                                      name="swa_prep_bwd")
        small_grads["q_norm_g"][l] = dgq.reshape(SWA_Q_WIDTH // HEAD_DIM, HEAD_DIM).sum(0)
        small_grads["k_norm_g"][l] = dgk.reshape(LANES // HEAD_DIM, HEAD_DIM).sum(0)
        small_grads["sinks"][l] = dsink[:, 0, ::HEAD_DIM].reshape(SWA_Q_WIDTH // HEAD_DIM)
        pieces = [dq_sb, dk_sb, dv_sb, dq_sw, dk_sw, dv_swb, dgl]
        g_in = [matmul_tn_row_blocks(pieces, a["h"], n_blocks=N_DEV, name="dw_in")]
        if l > 0:
            (dx, dxb, dg_mix), (landed,) = matmul_nt_norm_bwd(pieces, mats[W_IN], a["x"], a["g_mix"], dx1,
                                                             name="in_proj_bwd", ride=[PairJob(g_in)])
            summed(l, [W_IN], g_in, landed)
            in_pending = l
        else:
            summed(l, [W_IN], g_in, exchange_alone(PairJob(g_in), name="grad_pair_exchange_in0"))
            job = chip_job([(l, W_IN)])
            (dx, dxb, dg_mix), (outs,) = matmul_nt_norm_bwd(pieces, mats[W_IN], a["x"], a["g_mix"], dx1,
                                                           name="in_proj_bwd_last", ride=[job])
            landed_chip(job, outs)
        small_grads["mix_norm_g"][l] = dg_mix.reshape(D_MODEL)

    out_g, out_d, out_m, out_v = {}, {}, {}, {}
    for i, n in enumerate(MATRIX_NAMES):
        outs = reduce_adamw(parts[i], lands[i], chip, local[n], local_m[n], local_v[n], name="adamw_" + n)
        if n == "w_in":
            outs = [jnp.swapaxes(t, 1, 2) for t in outs]
        out_g[n], out_d[n], out_m[n], out_v[n] = outs
    small_shapes = {n: weights[n].shape for n in SMALL_NAMES}
    small_all = gather_small(_pack_small({n: jnp.stack(v) for n, v in small_grads.items()}), name="gather_small_grads")
    sg, sd, sm, sv = small_adamw(small_all, _pack_small(weights), _pack_small(mom_m), _pack_small(mom_v),
                                 name="small_adamw")
    for tree, packed_small in ((out_g, sg), (out_d, sd), (out_m, sm), (out_v, sv)):
        tree.update(_unpack_small(packed_small, small_shapes))
    return loss, dx, (out_g, out_d, out_m, out_v)


def kernel(x, mix_norm_g, w_in, q_norm_g, k_norm_g, sinks, w_branch_sb, w_branch_swa, w_out, mlp_norm_g, w_up, w_down, loss_target, m_mix_norm_g, m_w_in, m_q_norm_g, m_k_norm_g, m_sinks, m_w_branch_sb, m_w_branch_swa, m_w_out, m_mlp_norm_g, m_w_up, m_w_down, v_mix_norm_g, v_w_in, v_q_norm_g, v_k_norm_g, v_sinks, v_w_branch_sb, v_w_branch_swa, v_w_out, v_mlp_norm_g, v_w_up, v_w_down):
    weights = dict(mix_norm_g=mix_norm_g, w_in=w_in, q_norm_g=q_norm_g, k_norm_g=k_norm_g, sinks=sinks,
                   w_branch_sb=w_branch_sb, w_branch_swa=w_branch_swa, w_out=w_out, mlp_norm_g=mlp_norm_g, w_up=w_up,
                   w_down=w_down)
    mom_m = dict(mix_norm_g=m_mix_norm_g, w_in=m_w_in, q_norm_g=m_q_norm_g, k_norm_g=m_k_norm_g, sinks=m_sinks,
                 w_branch_sb=m_w_branch_sb, w_branch_swa=m_w_branch_swa, w_out=m_w_out, mlp_norm_g=m_mlp_norm_g,
                 w_up=m_w_up, w_down=m_w_down)
    mom_v = dict(mix_norm_g=v_mix_norm_g, w_in=v_w_in, q_norm_g=v_q_norm_g, k_norm_g=v_k_norm_g, sinks=v_sinks,
                 w_branch_sb=v_w_branch_sb, w_branch_swa=v_w_branch_swa, w_out=v_w_out, mlp_norm_g=v_mlp_norm_g,
                 w_up=v_w_up, w_down=v_w_down)
    loss_part, grad_x, outs = train_step(x[0], loss_target[0], weights, mom_m, mom_v)
    loss = lax.psum(loss_part[0, 0], MESH_AXES)
    return (loss, grad_x[None], *[outs[0][n] for n in WEIGHT_ORDER], *[outs[1][n] for n in WEIGHT_ORDER],
            *[outs[2][n] for n in WEIGHT_ORDER], *[outs[3][n] for n in WEIGHT_ORDER])
```

```python
import functools
import math

import jax
import jax.numpy as jnp
from jax import lax
from jax.experimental import pallas as pl
from jax.experimental.pallas import tpu as pltpu

F32 = jnp.float32
BF16 = jnp.bfloat16

DEPTH = 4
D_MODEL = 1024
HEAD_DIM = 64
LANES = 128
WINDOW = 128
SB_WIDTH = 512
SWA_Q_WIDTH = 512
SWA_KV_WIDTH = 128
ATTN_WIDTH = 3 * SB_WIDTH + SWA_Q_WIDTH + 2 * SWA_KV_WIDTH
IN_WIDTH = ATTN_WIDTH + 2 * D_MODEL
ROPE_THETA = 10000.0
NORM_EPS = 1e-6
SCALE = HEAD_DIM ** -0.5
NEG = -1e30
N_DEV = 8
N_CHIPS = 4

ADAM_LR = 0.001
ADAM_B1 = 0.9
ADAM_B2 = 0.999
ADAM_EPS = 1e-08
ADAM_WD = 0.01
ADAM_STEP = 10

SB_TQ = 128
SB_TK1 = 384
SB_TK = 256
SB_CUTOFF = -88.0
SWA_TQ = 128
SWA_TK = 256
ROW_TILE = 512
VMEM_LIMIT = 56 * 1024 * 1024

MATRIX_NAMES = ("w_in", "w_branch_sb", "w_branch_swa", "w_out", "w_up", "w_down")
W_IN, W_BSB, W_BSW, W_OUT, W_UP, W_DOWN = range(6)
ROW_SHARDED = ("w_out", "w_down")
SMALL_NAMES = ("mix_norm_g", "q_norm_g", "k_norm_g", "sinks", "mlp_norm_g")
WEIGHT_ORDER = ("mix_norm_g", "w_in", "q_norm_g", "k_norm_g", "sinks", "w_branch_sb", "w_branch_swa", "w_out",
                "mlp_norm_g", "w_up", "w_down")
MESH_AXES = ("x", "y", "c")

ANY = pl.BlockSpec(memory_space=pl.ANY)
MESH = pl.DeviceIdType.MESH


def _params(*sem):
    return pltpu.CompilerParams(dimension_semantics=sem, vmem_limit_bytes=VMEM_LIMIT)


def _dot(a, b):
    return jnp.dot(a, b, preferred_element_type=F32)


def _dot_nt(a, b):
    return lax.dot_general(a, b, (((1,), (1,)), ((), ())), preferred_element_type=F32)


def _dot_tn(a, b):
    return lax.dot_general(a, b, (((0,), (0,)), ((), ())), preferred_element_type=F32)


def _split_bf16(x):
    hi = lax.bitcast_convert_type(lax.bitcast_convert_type(x, jnp.uint32) & jnp.uint32(0xFFFF0000), F32)
    return hi.astype(BF16), (x - hi).astype(BF16)


def _rsqrt_ms(x):
    return lax.rsqrt(jnp.mean(x * x, axis=-1, keepdims=True) + NORM_EPS)


def _place():
    return lax.axis_index("x"), lax.axis_index("y"), lax.axis_index("c")


class _Gather:
    def __init__(self, x_refs, out_refs, send_sems, recv_sems, local_sems):
        self.x_refs, self.out_refs = x_refs, out_refs
        self.send_sems, self.recv_sems, self.local_sems = send_sems, recv_sems, local_sems
        self.n = len(x_refs)
        x, y, c = _place()
        self.c = c
        self.me, self.sibling = (x, y, c), (x, y, 1 - c)
        self.chips = [(1 - x, y), (x, 1 - y), (1 - x, 1 - y)]

    def _copy(self, k, w, blk, to, own=False):
        dst = self.out_refs[w].at[4 * blk[0] + 2 * blk[1] + blk[2]]
        return pltpu.make_async_remote_copy(
            src_ref=self.x_refs[w] if own else dst, dst_ref=dst, send_sem=self.send_sems.at[k, w],
            recv_sem=self.recv_sems.at[k, w], device_id=to, device_id_type=MESH)

    def _mine(self, w):
        me = self.me
        return pltpu.make_async_copy(self.x_refs[w], self.out_refs[w].at[4 * me[0] + 2 * me[1] + me[2]],
                                     self.local_sems.at[w])

    def _first(self, w):
        return [self._copy(0, w, self.me, self.sibling, own=True)] + [
            self._copy(1 + j, w, self.me, (*chip, self.c), own=True) for j, chip in enumerate(self.chips)]

    def _passed(self, j, w):
        return self._copy(4 + j, w, (*self.chips[j], self.c), self.sibling)

    def start(self):
        for w in range(self.n):
            self._mine(w).start()
            for cp in self._first(w):
                cp.start()

    def relay(self):
        for j, chip in enumerate(self.chips):
            for w in range(self.n):
                self._copy(1 + j, w, (*chip, self.c), self.me).wait_recv()
                self._passed(j, w).start()

    def finish(self):
        for w in range(self.n):
            self._copy(0, w, self.sibling, self.me).wait_recv()
            for j, chip in enumerate(self.chips):
                self._copy(4 + j, w, (*chip, 1 - self.c), self.me).wait_recv()
            for cp in self._first(w):
                cp.wait_send()
            for j in range(3):
                self._passed(j, w).wait_send()
            self._mine(w).wait()


class GatherJob:
    def __init__(self, shards):
        n = len(shards)
        self.inputs = list(shards)
        self.out_shapes = [jax.ShapeDtypeStruct((N_DEV,) + s.shape, s.dtype) for s in shards]
        self.aliases = {}
        self.scratch = [pltpu.SemaphoreType.DMA((7, n)), pltpu.SemaphoreType.DMA((7, n)),
                        pltpu.SemaphoreType.DMA((n,))]

    def bind(self, in_refs, out_refs, scratch_refs):
        return _Gather(in_refs, out_refs, *scratch_refs)


class _Copies:
    def __init__(self, copies):
        self.copies = copies

    def start(self):
        for cp in self.copies:
            cp.start()

    def relay(self):
        pass

    def finish(self):
        for cp in self.copies:
            cp.wait_recv()
        for cp in self.copies:
            cp.wait_send()


class ChipJob:
    def __init__(self, items, parts, lands):
        self.ws = sorted({w for _, w in items})
        n = len(self.ws)
        self.items = [(layer, self.ws.index(w)) for layer, w in items]
        self.inputs = [parts[w] for w in self.ws] + [lands[w] for w in self.ws]
        self.out_shapes = [jax.ShapeDtypeStruct(lands[w].shape, lands[w].dtype) for w in self.ws]
        self.aliases = {n + i: i for i in range(n)}
        self.scratch = [pltpu.SemaphoreType.DMA((3, n)), pltpu.SemaphoreType.DMA((3, n))]

    def bind(self, in_refs, out_refs, scratch_refs):
        send_sems, recv_sems = scratch_refs
        x, y, c = _place()
        chips = [(1 - x, y), (x, 1 - y), (1 - x, 1 - y)]
        return _Copies([pltpu.make_async_remote_copy(
            src_ref=in_refs[i].at[layer, 2 * px + py], dst_ref=out_refs[i].at[layer, j],
            send_sem=send_sems.at[j, i], recv_sem=recv_sems.at[j, i], device_id=(px, py, c), device_id_type=MESH)
            for layer, i in self.items for j, (px, py) in enumerate(chips)])


class PairJob:
    def __init__(self, grads):
        n = len(grads)
        self.inputs = list(grads)
        self.out_shapes = [jax.ShapeDtypeStruct((N_CHIPS,) + g.shape[1:], g.dtype) for g in grads]
        self.aliases = {}
        self.scratch = [pltpu.SemaphoreType.DMA((N_CHIPS, n)), pltpu.SemaphoreType.DMA((N_CHIPS, n))]

    def bind(self, in_refs, out_refs, scratch_refs):
        send_sems, recv_sems = scratch_refs
        x, y, c = _place()
        return _Copies([pltpu.make_async_remote_copy(
            src_ref=in_refs[w].at[2 * k + (1 - c)], dst_ref=out_refs[w].at[k], send_sem=send_sems.at[k, w],
            recv_sem=recv_sems.at[k, w], device_id=(x, y, 1 - c), device_id_type=MESH)
            for w in range(len(in_refs)) for k in range(N_CHIPS)])


def _call(body, *, name, grid, in_specs, out_specs, out_shape, args, scratch_shapes=(), ride=()):
    out_specs, out_shape, in_specs = tuple(out_specs), tuple(out_shape), list(in_specs)
    scratch_shapes = list(scratch_shapes)
    order = ("arbitrary",) * len(grid)
    if not ride:
        outs = pl.pallas_call(body, name=name, grid=grid, in_specs=in_specs, out_specs=out_specs, out_shape=out_shape,
                              scratch_shapes=scratch_shapes, compiler_params=_params(*order))(*args)
        return tuple(outs), []
    n_in, n_out, n_scr = len(in_specs), len(out_specs), len(scratch_shapes)
    n_steps = math.prod(grid)
    relay_early = n_steps >= 8
    relay_at = n_steps - n_steps // 4 if relay_early else n_steps - 1

    def split(refs, pos, counts):
        groups = []
        for k in counts:
            groups.append(refs[pos:pos + k])
            pos += k
        return groups, pos

    def wrapped(*refs):
        ins, pos = refs[:n_in], n_in
        job_in, pos = split(refs, pos, [len(j.inputs) for j in ride])
        outs, pos = refs[pos:pos + n_out], pos + n_out
        job_out, pos = split(refs, pos, [len(j.out_shapes) for j in ride])
        scr, pos = refs[pos:pos + n_scr], pos + n_scr
        job_scr, pos = split(refs, pos, [len(j.scratch) for j in ride])
        bound = [j.bind(i, o, s) for j, i, o, s in zip(ride, job_in, job_out, job_scr)]
        step = pl.program_id(0)
        for axis in range(1, len(grid)):
            step = step * grid[axis] + pl.program_id(axis)

        @pl.when(step == 0)
        def _():
            for b in bound:
                b.start()

        if relay_early:
            @pl.when(step == relay_at)
            def _():
                for b in bound:
                    b.relay()

        body(*ins, *outs, *scr)

        @pl.when(step == n_steps - 1)
        def _():
            if not relay_early:
                for b in bound:
                    b.relay()
            for b in bound:
                b.finish()

    aliases, in_pos, out_pos = {}, n_in, n_out
    for j in ride:
        aliases.update({in_pos + i: out_pos + o for i, o in j.aliases.items()})
        in_pos += len(j.inputs)
        out_pos += len(j.out_shapes)
    results = pl.pallas_call(
        wrapped, name=name, grid=grid, in_specs=in_specs + [ANY] * (in_pos - n_in),
        out_specs=out_specs + (ANY,) * (out_pos - n_out),
        out_shape=out_shape + tuple(s for j in ride for s in j.out_shapes),
        scratch_shapes=scratch_shapes + [s for j in ride for s in j.scratch], input_output_aliases=aliases,
        compiler_params=pltpu.CompilerParams(dimension_semantics=order, vmem_limit_bytes=VMEM_LIMIT,
                                             has_side_effects=True),
    )(*args, *[a for j in ride for a in j.inputs])
    job_results, pos = split(list(results), n_out, [len(j.out_shapes) for j in ride])
    return tuple(results[:n_out]), job_results


def exchange_alone(job, *, name):
    n_in, n_out = len(job.inputs), len(job.out_shapes)

    def body(*refs):
        b = job.bind(refs[:n_in], refs[n_in:n_in + n_out], refs[n_in + n_out:])
        b.start()
        b.relay()
        b.finish()

    return list(pl.pallas_call(
        body, name=name, out_shape=tuple(job.out_shapes), in_specs=[ANY] * n_in, out_specs=(ANY,) * n_out,
        scratch_shapes=job.scratch, input_output_aliases=job.aliases,
        compiler_params=pltpu.CompilerParams(has_side_effects=True),
    )(*job.inputs))


PAIR_SUM_CHUNKS = 1


def pair_sum(layer, grads, landed, parts, core, *, name):
    n = len(grads)

    def body(c_ref, *refs):
        g_refs, l_refs, o_refs = refs[:n], refs[n:2 * n], refs[3 * n:]
        for w in range(n):
            o_refs[w][...] = (g_refs[w][...].astype(F32) + l_refs[w][...].astype(F32)).astype(BF16)

    def blk(g):
        return (None, g.shape[1] // PAIR_SUM_CHUNKS, g.shape[2])

    in_specs = [pl.BlockSpec(blk(g), lambda k, i, c_ref: (2 * k + c_ref[0], i, 0)) for g in grads]
    in_specs += [pl.BlockSpec(blk(g), lambda k, i, c_ref: (k, i, 0)) for g in grads]
    in_specs += [ANY] * n
    out_specs = tuple(pl.BlockSpec((None,) + blk(g), lambda k, i, c_ref: (layer, k, i, 0)) for g in grads)
    return list(pl.pallas_call(
        body, name=name, out_shape=tuple(jax.ShapeDtypeStruct(p.shape, p.dtype) for p in parts),
        grid_spec=pltpu.PrefetchScalarGridSpec(num_scalar_prefetch=1, grid=(N_CHIPS, PAIR_SUM_CHUNKS),
                                               in_specs=in_specs, out_specs=out_specs),
        input_output_aliases={1 + 2 * n + w: w for w in range(n)},
        compiler_params=_params("parallel", "parallel"),
    )(core, *grads, *landed, *parts))


def _adamw(w, g, m, v):
    m = ADAM_B1 * m + (1.0 - ADAM_B1) * g
    v = ADAM_B2 * v + (1.0 - ADAM_B2) * (g * g)
    m_hat = m / (1.0 - ADAM_B1 ** ADAM_STEP)
    v_hat = v / (1.0 - ADAM_B2 ** ADAM_STEP)
    delta = -ADAM_LR * (m_hat / (jnp.sqrt(v_hat) + ADAM_EPS) + ADAM_WD * w)
    return delta, m, v


def reduce_adamw(part, land, chip, w, m, v, *, name):
    _, r, c = w.shape
    tr = 256 if r % 256 == 0 else (r // 2 if r > 256 else r)

    def body(k_ref, own_ref, l0_ref, l1_ref, l2_ref, w_ref, m_ref, v_ref, g_out, d_out, m_out, v_out):
        g = own_ref[...].astype(F32) + l0_ref[...].astype(F32) + l1_ref[...].astype(F32) + l2_ref[...].astype(F32)
        delta, m_new, v_new = _adamw(w_ref[...], g, m_ref[...], v_ref[...])
        g_out[...] = g
        d_out[...] = delta
        m_out[...] = m_new
        v_out[...] = v_new

    row = pl.BlockSpec((None, tr, c), lambda l, i, k_ref: (l, i, 0))

    def slot(j):
        return pl.BlockSpec((None, None, tr, c), lambda l, i, k_ref: (l, j, i, 0))

    return pl.pallas_call(
        body, name=name, out_shape=(jax.ShapeDtypeStruct(w.shape, F32),) * 4,
        grid_spec=pltpu.PrefetchScalarGridSpec(
            num_scalar_prefetch=1, grid=(DEPTH, r // tr),
            in_specs=[pl.BlockSpec((None, None, tr, c), lambda l, i, k_ref: (l, k_ref[0], i, 0)), slot(0), slot(1),
                      slot(2), row, row, row],
            out_specs=(row, row, row, row)),
        compiler_params=_params("parallel", "parallel"),
    )(chip, part, land, land, land, w, m, v)


def gather_small(block, *, name):
    def body(x_ref, out_ref, send_sems, recv_sems, local_sem):
        x, y, c = _place()
        me = 4 * x + 2 * y + c
        mine = pltpu.make_async_copy(x_ref, out_ref.at[me], local_sem)
        mine.start()
        peers = [(x ^ (k >> 2), y ^ ((k >> 1) & 1), c ^ (k & 1)) for k in range(1, N_DEV)]
        copies = [pltpu.make_async_remote_copy(
            src_ref=x_ref, dst_ref=out_ref.at[me], send_sem=send_sems.at[k], recv_sem=recv_sems.at[k],
            device_id=peer, device_id_type=MESH) for k, peer in enumerate(peers)]
        for cp in copies:
            cp.start()
        for k, (px, py, pc) in enumerate(peers):
            pltpu.make_async_remote_copy(
                src_ref=x_ref, dst_ref=out_ref.at[4 * px + 2 * py + pc], send_sem=send_sems.at[k],
                recv_sem=recv_sems.at[k], device_id=(px, py, pc), device_id_type=MESH).wait_recv()
        for cp in copies:
            cp.wait_send()
        mine.wait()

    return pl.pallas_call(
        body, name=name, out_shape=jax.ShapeDtypeStruct((N_DEV,) + block.shape, block.dtype),
        in_specs=[ANY], out_specs=ANY,
        scratch_shapes=[pltpu.SemaphoreType.DMA((7,)), pltpu.SemaphoreType.DMA((7,)), pltpu.SemaphoreType.DMA],
        compiler_params=pltpu.CompilerParams(has_side_effects=True),
    )(block)


def small_adamw(gathered, w, m, v, *, name):
    def body(g_ref, w_ref, m_ref, v_ref, g_out, d_out, m_out, v_out):
        g = g_ref[0]
        for d in range(1, N_DEV):
            g = g + g_ref[d]
        delta, m_new, v_new = _adamw(w_ref[...], g, m_ref[...], v_ref[...])
        g_out[...] = g
        d_out[...] = delta
        m_out[...] = m_new
        v_out[...] = v_new

    return pl.pallas_call(
        body, name=name, out_shape=(jax.ShapeDtypeStruct(w.shape, F32),) * 4,
    )(gathered, w, m, v)


def norm_matmul(x, g, w, *, gate_split, name, ride=()):
    s, d = x.shape
    tm = min(ROW_TILE, s)
    blocked = w.ndim == 3
    n = w.shape[0] if not blocked else w.shape[0] * w.shape[2]

    def body(x_ref, g_ref, w_ref, h_ref, *outs):
        xv = x_ref[...]
        h = ((xv * _rsqrt_ms(xv)) * g_ref[...]).astype(BF16)
        h_ref[...] = h
        if blocked:
            nb = w_ref.shape[2]
            for j in range(w_ref.shape[0]):
                outs[0][:, j * nb:(j + 1) * nb] = _dot(h, w_ref[j]).astype(BF16)
        else:
            p = _dot_nt(h, w_ref[...])
            outs[0][...] = p[:, :gate_split].astype(BF16)
            outs[1][...] = (1.0 / (1.0 + jnp.exp(-p[:, gate_split:]))).astype(BF16)

    row = lambda i: (i, 0)
    fixed = lambda i: (0, 0)
    if blocked:
        out_shape = (jax.ShapeDtypeStruct((s, d), BF16), jax.ShapeDtypeStruct((s, n), BF16))
        out_specs = (pl.BlockSpec((tm, d), row), pl.BlockSpec((tm, n), row))
        w_spec = pl.BlockSpec(w.shape, lambda i: (0, 0, 0))
    else:
        out_shape = (jax.ShapeDtypeStruct((s, d), BF16), jax.ShapeDtypeStruct((s, gate_split), BF16),
                     jax.ShapeDtypeStruct((s, n - gate_split), BF16))
        out_specs = (pl.BlockSpec((tm, d), row), pl.BlockSpec((tm, gate_split), row),
                     pl.BlockSpec((tm, n - gate_split), row))
        w_spec = pl.BlockSpec((n, d), fixed)
    return _call(body, name=name, grid=(s // tm,), out_shape=out_shape, out_specs=out_specs,
                 in_specs=[pl.BlockSpec((tm, d), row), pl.BlockSpec((1, d), fixed), w_spec], args=(x, g, w), ride=ride)


def merge_out_fwd(x, o_sb, o_sw, gates, w_bsb, w_bsw, w_o, *, name):
    s, d = x.shape
    tm = min(ROW_TILE, s)

    def body(x_ref, osb_ref, osw_ref, g_ref, wsb_ref, wsw_ref, wo_ref, x1_ref, ysb_ref, ysw_ref, mg_ref):
        y_sb = _dot(osb_ref[...].astype(BF16), wsb_ref[...])
        y_sw = _dot(osw_ref[...].astype(BF16), wsw_ref[...])
        g = g_ref[...].astype(F32)
        merged = (g[:, :d] * y_sb + g[:, d:] * y_sw).astype(BF16)
        ysb_ref[...] = y_sb.astype(BF16)
        ysw_ref[...] = y_sw.astype(BF16)
        mg_ref[...] = merged
        x1_ref[...] = x_ref[...] + _dot(merged, wo_ref[...])

    row = lambda i: (i, 0)
    fixed = lambda i: (0, 0)
    wd = o_sb.shape[1]
    return pl.pallas_call(
        body, name=name, grid=(s // tm,),
        out_shape=(jax.ShapeDtypeStruct((s, d), F32),) + (jax.ShapeDtypeStruct((s, d), BF16),) * 3,
        in_specs=[pl.BlockSpec((tm, d), row), pl.BlockSpec((tm, wd), row), pl.BlockSpec((tm, wd), row),
                  pl.BlockSpec((tm, 2 * d), row), pl.BlockSpec((wd, d), fixed), pl.BlockSpec((wd, d), fixed),
                  pl.BlockSpec((d, d), fixed)],
        out_specs=(pl.BlockSpec((tm, d), row),) * 4, compiler_params=_params("parallel"),
    )(x, o_sb, o_sw, gates, w_bsb, w_bsw, w_o)


def mlp_down_fwd(x1, u, w_down, *, name):
    s, d = x1.shape
    f = u.shape[1]
    tm = min(ROW_TILE, s)

    def body(x_ref, u_ref, w_ref, o_ref):
        a = jnp.maximum(u_ref[...].astype(F32), 0.0)
        o_ref[...] = x_ref[...] + _dot((a * a).astype(BF16), w_ref[...])

    row = lambda i: (i, 0)
    return pl.pallas_call(
        body, name=name, grid=(s // tm,), out_shape=jax.ShapeDtypeStruct((s, d), F32),
        in_specs=[pl.BlockSpec((tm, d), row), pl.BlockSpec((tm, f), row), pl.BlockSpec((f, d), lambda i: (0, 0))],
        out_specs=pl.BlockSpec((tm, d), row), compiler_params=_params("parallel"),
    )(x1, u, w_down)


def loss_head(y, target, *, name):
    s, d = y.shape
    tm = min(ROW_TILE, s)

    def body(y_ref, t_ref, dy_ref, dyb_ref, loss_ref):
        @pl.when(pl.program_id(0) == 0)
        def _():
            loss_ref[...] = jnp.zeros_like(loss_ref)

        e = y_ref[...] - t_ref[...]
        dy = e * (1.0 / d)
        dy_ref[...] = dy
        dyb_ref[...] = dy.astype(BF16)
        per_row = jnp.sum(e * e, axis=1, keepdims=True) * (0.5 / d)
        loss_ref[...] += jnp.sum(per_row, axis=0, keepdims=True)

    row = lambda i: (i, 0)
    return pl.pallas_call(
        body, name=name, grid=(s // tm,),
        out_shape=(jax.ShapeDtypeStruct((s, d), F32), jax.ShapeDtypeStruct((s, d), BF16),
                   jax.ShapeDtypeStruct((1, 1), F32)),
        in_specs=[pl.BlockSpec((tm, d), row), pl.BlockSpec((tm, d), row)],
        out_specs=(pl.BlockSpec((tm, d), row), pl.BlockSpec((tm, d), row), pl.BlockSpec((1, 1), lambda i: (0, 0))),
        compiler_params=_params("arbitrary"),
    )(y, target)


def mlp_bwd_up(dxb, u, w_down, *, name, ride=()):
    s, d = dxb.shape
    f = u.shape[1]
    tm = min(ROW_TILE, s)

    def body(dx_ref, u_ref, w_ref, du_ref):
        da = _dot_nt(dx_ref[...], w_ref[...])
        du_ref[...] = (da * (2.0 * jnp.maximum(u_ref[...].astype(F32), 0.0))).astype(BF16)

    row = lambda i: (i, 0)
    return _call(body, name=name, grid=(s // tm,), out_shape=(jax.ShapeDtypeStruct((s, f), BF16),),
                 in_specs=[pl.BlockSpec((tm, d), row), pl.BlockSpec((tm, f), row),
                           pl.BlockSpec((f, d), lambda i: (0, 0))],
                 out_specs=(pl.BlockSpec((tm, f), row),), args=(dxb, u, w_down), ride=ride)


def matmul_nt_norm_bwd(pieces, w, x, g, dres, *, name, ride=()):
    s = x.shape[0]
    d = x.shape[1]
    tm = min(ROW_TILE, s)
    blocked = w.ndim == 3
    n_pieces = len(pieces)
    widths = [p.shape[1] for p in pieces]

    def body(*refs):
        p_refs = refs[:n_pieces]
        w_ref, x_ref, g_ref, dres_ref, dx_ref, dxb_ref, dg_ref = refs[n_pieces:]

        @pl.when(pl.program_id(0) == 0)
        def _():
            dg_ref[...] = jnp.zeros_like(dg_ref)

        if blocked:
            nb = w_ref.shape[2]
            dh = _dot_nt(p_refs[0][:, :nb], w_ref[0])
            for j in range(1, w_ref.shape[0]):
                dh = dh + _dot_nt(p_refs[0][:, j * nb:(j + 1) * nb], w_ref[j])
        else:
            dh, off = None, 0
            for p_ref, width in zip(p_refs, widths):
                part = _dot(p_ref[...], w_ref[off:off + width, :])
                dh = part if dh is None else dh + part
                off += width
        xv = x_ref[...]
        r = _rsqrt_ms(xv)
        dyg = dh * g_ref[...]
        dx = dres_ref[...] + r * dyg - xv * ((r * r * r) * jnp.mean(dyg * xv, axis=-1, keepdims=True))
        dx_ref[...] = dx
        dxb_ref[...] = dx.astype(BF16)
        dg_ref[...] += jnp.sum(dh * (xv * r), axis=0, keepdims=True)

    row = lambda i: (i, 0)
    fixed = lambda i: (0, 0)
    w_spec = pl.BlockSpec(w.shape, (lambda i: (0, 0, 0)) if blocked else fixed)
    return _call(
        body, name=name, grid=(s // tm,),
        out_shape=(jax.ShapeDtypeStruct((s, d), F32), jax.ShapeDtypeStruct((s, d), BF16),
                   jax.ShapeDtypeStruct((1, d), F32)),
        in_specs=[pl.BlockSpec((tm, width), row) for width in widths] + [
            w_spec, pl.BlockSpec((tm, d), row), pl.BlockSpec((1, d), fixed), pl.BlockSpec((tm, d), row)],
        out_specs=(pl.BlockSpec((tm, d), row), pl.BlockSpec((tm, d), row), pl.BlockSpec((1, d), fixed)),
        args=(*pieces, w, x, g, dres), ride=ride)


def out_bwd(dx1b, w_o, gates, y_sb, y_sw, w_bsb, w_bsw, *, name):
    s, d = dx1b.shape
    wd = w_bsb.shape[0]
    tm = min(ROW_TILE, s)

    def body(dx_ref, wo_ref, g_ref, ysb_ref, ysw_ref, wsb_ref, wsw_ref, dysb_ref, dysw_ref, dosb_ref, dosw_ref, dgl_ref):
        dm = _dot_nt(dx_ref[...], wo_ref[...])
        g = g_ref[...].astype(F32)
        g0, g1 = g[:, :d], g[:, d:]
        dy_sb = (dm * g0).astype(BF16)
        dy_sw = (dm * g1).astype(BF16)
        dysb_ref[...] = dy_sb
        dysw_ref[...] = dy_sw
        dosb_ref[...] = _dot_nt(dy_sb, wsb_ref[...]).astype(BF16)
        dosw_ref[...] = _dot_nt(dy_sw, wsw_ref[...]).astype(BF16)
        dgl_ref[:, :d] = (dm * ysb_ref[...].astype(F32) * (g0 * (1.0 - g0))).astype(BF16)
        dgl_ref[:, d:] = (dm * ysw_ref[...].astype(F32) * (g1 * (1.0 - g1))).astype(BF16)

    row = lambda i: (i, 0)
    fixed = lambda i: (0, 0)
    return pl.pallas_call(
        body, name=name, grid=(s // tm,),
        out_shape=(jax.ShapeDtypeStruct((s, d), BF16), jax.ShapeDtypeStruct((s, d), BF16),
                   jax.ShapeDtypeStruct((s, wd), BF16), jax.ShapeDtypeStruct((s, wd), BF16),
                   jax.ShapeDtypeStruct((s, 2 * d), BF16)),
        in_specs=[pl.BlockSpec((tm, d), row), pl.BlockSpec((d, d), fixed), pl.BlockSpec((tm, 2 * d), row),
                  pl.BlockSpec((tm, d), row), pl.BlockSpec((tm, d), row), pl.BlockSpec((wd, d), fixed),
                  pl.BlockSpec((wd, d), fixed)],
        out_specs=(pl.BlockSpec((tm, d), row), pl.BlockSpec((tm, d), row), pl.BlockSpec((tm, wd), row),
                   pl.BlockSpec((tm, wd), row), pl.BlockSpec((tm, 2 * d), row)),
        compiler_params=_params("parallel"),
    )(dx1b, w_o, gates, y_sb, y_sw, w_bsb, w_bsw)


def matmul_tn(a, pieces, *, a_block, out_cols, relu2, name):
    s, m = a.shape
    widths = [p.shape[1] for p in pieces]
    n = sum(widths)
    n_pieces = len(pieces)
    ts = min(512 if n >= 4096 else 2048, s)
    n_steps = s // ts
    if out_cols is None:
        out_shape = jax.ShapeDtypeStruct((m // a_block, a_block, n), BF16)
        out_spec = pl.BlockSpec((None, a_block, n), lambda i, k: (i, 0, 0))
    else:
        out_shape = jax.ShapeDtypeStruct((n // out_cols, m, out_cols), BF16)
        out_spec = pl.BlockSpec((n // out_cols, a_block, out_cols), lambda i, k: (0, i, 0))

    def body(a_ref, *refs):
        b_refs, o_ref, acc = refs[:n_pieces], refs[n_pieces], refs[n_pieces + 1]
        k = pl.program_id(1)

        @pl.when(k == 0)
        def _():
            acc[...] = jnp.zeros_like(acc)

        av = a_ref[...]
        if relu2:
            af = jnp.maximum(av.astype(F32), 0.0)
            av = af * af
        av = av.astype(BF16)
        off = 0
        for b_ref in b_refs:
            width = b_ref.shape[1]
            acc[:, off:off + width] += _dot_tn(av, b_ref[...].astype(BF16))
            off += width

        @pl.when(k == n_steps - 1)
        def _():
            if out_cols is None:
                o_ref[...] = acc[...].astype(BF16)
            else:
                for j in range(n // out_cols):
                    o_ref[j] = acc[:, j * out_cols:(j + 1) * out_cols].astype(BF16)

    return pl.pallas_call(
        body, name=name, grid=(m // a_block, n_steps), out_shape=out_shape,
        in_specs=[pl.BlockSpec((ts, a_block), lambda i, k: (k, i))] + [
            pl.BlockSpec((ts, width), lambda i, k: (k, 0)) for width in widths],
        out_specs=out_spec, scratch_shapes=[pltpu.VMEM((a_block, n), F32)],
        compiler_params=_params("parallel", "arbitrary"),
    )(a, *pieces)


def matmul_tn_row_blocks(pieces, b, *, n_blocks, name):
    s, n = b.shape
    widths = [p.shape[1] for p in pieces]
    m = sum(widths)
    rows = m // n_blocks
    n_pieces = len(pieces)
    ts = min(512, s)
    n_steps = s // ts
    half = n_blocks // 2

    def body(*refs):
        p_refs, b_ref, o_ref, a_tile, acc = refs[:n_pieces], refs[n_pieces], refs[n_pieces + 1], refs[-2], refs[-1]
        i, k = pl.program_id(0), pl.program_id(1)

        @pl.when(k == 0)
        def _():
            acc[...] = jnp.zeros_like(acc)

        off = 0
        for p_ref, width in zip(p_refs, widths):
            a_tile[:, off:off + width] = p_ref[...]
            off += width
        bv = b_ref[...]
        for side in range(2):
            @pl.when(i == side)
            def _():
                for j in range(half):
                    col = (side * half + j) * rows
                    acc[j] += _dot_tn(a_tile[:, col:col + rows], bv)

        @pl.when(k == n_steps - 1)
        def _():
            o_ref[...] = acc[...].astype(BF16)

    return pl.pallas_call(
        body, name=name, grid=(2, n_steps), out_shape=jax.ShapeDtypeStruct((n_blocks, rows, n), BF16),
        in_specs=[pl.BlockSpec((ts, width), lambda i, k: (k, 0)) for width in widths] + [
            pl.BlockSpec((ts, n), lambda i, k: (k, 0))],
        out_specs=pl.BlockSpec((half, rows, n), lambda i, k: (i, 0, 0)),
        scratch_shapes=[pltpu.VMEM((ts, m), BF16), pltpu.VMEM((half, rows, n), F32)],
        compiler_params=_params("parallel", "arbitrary"),
    )(*pieces, b)


def _softplus(z):
    return jnp.maximum(z, 0.0) + jnp.log(1.0 + jnp.exp(-jnp.abs(z)))


def _suffix_sums(x, tri2):
    groups = x.shape[1] // LANES
    outs, run = [None] * groups, None
    for g in reversed(range(groups)):
        xg = x[:, g * LANES:(g + 1) * LANES]
        hi, lo = _split_bf16(xg)
        inner = _dot(jnp.concatenate([hi, lo], axis=1), tri2)
        outs[g] = inner if run is None else inner + run
        total = jnp.sum(xg, axis=1, keepdims=True)
        run = total if run is None else run + total
    return jnp.concatenate(outs, axis=1), run


def _head_mask(h):
    return (lax.broadcasted_iota(jnp.int32, (1, LANES), 1) // HEAD_DIM) == h


def _stack_heads(x):
    zero = jnp.zeros_like(x)
    return jnp.concatenate([jnp.where(_head_mask(0), x, zero), jnp.where(_head_mask(1), x, zero)], axis=0)


def _unstack_heads(r, t):
    return jnp.where(_head_mask(0), r[:t], r[t:])


def _sb_positions(q0, tk):
    row = lax.broadcasted_iota(jnp.int32, (2 * SB_TQ, tk), 0)
    col = lax.broadcasted_iota(jnp.int32, (2 * SB_TQ, tk), 1)
    return q0 + jnp.where(row >= SB_TQ, row - SB_TQ, row), col


def _sb_first_key(q0):
    return pl.multiple_of(jnp.maximum(q0 + SB_TQ - SB_TK1, 0), SB_TQ)


def _sb_next_key(k_prev):
    return pl.multiple_of(jnp.maximum(k_prev - SB_TK, 0), SB_TQ)


def _sb_rows(q0):
    return pl.ds(pl.multiple_of(2 * q0, 2 * SB_TQ), 2 * SB_TQ)


def sb_attn_fwd(proj, tri2, *, name, ride=()):
    s = proj.shape[0]
    nq = s // SB_TQ
    n_pairs = SB_WIDTH // LANES

    def body(q_ref, k_ref, v_ref, tri_ref, o_ref, c_all):
        def block(qh, k0, tk, live, c):
            z = _dot_nt(qh, k_ref[pl.ds(k0, tk), :])
            sp = _softplus(z)
            tail, total = _suffix_sums(jnp.where(live, -sp, 0.0), tri_ref[...])
            w = jnp.where(live, jnp.exp(z - sp + tail + c), 0.0)
            return _dot(w.astype(BF16), v_ref[pl.ds(k0, tk), :]), c + total

        def load_q(q0):
            return _stack_heads(q_ref[pl.ds(q0, SB_TQ), :]) * SCALE

        def first(qb, carry):
            q0 = pl.multiple_of(qb * SB_TQ, SB_TQ)
            tpos, col = _sb_positions(q0, SB_TK1)
            k0 = _sb_first_key(q0)
            acc, c = block(load_q(q0), k0, SB_TK1, k0 + col < tpos, jnp.zeros((2 * SB_TQ, 1), F32))
            o_ref[pl.ds(q0, SB_TQ), :] = _unstack_heads(acc, SB_TQ)
            c_all[_sb_rows(q0), :] = jnp.broadcast_to(jnp.where(k0 > 0, c, NEG), (2 * SB_TQ, LANES))
            return carry

        lax.fori_loop(0, nq, first, 0, unroll=4)

        @pl.when(jnp.max(c_all[...]) > SB_CUTOFF)
        def _():
            def more(qb, carry):
                q0 = pl.multiple_of(qb * SB_TQ, SB_TQ)
                c0 = c_all[_sb_rows(q0), 0:1]

                @pl.when(jnp.max(c0) > SB_CUTOFF)
                def _():
                    qh = load_q(q0)
                    _, col = _sb_positions(q0, SB_TK)

                    def cond(st):
                        return jnp.logical_and(st[0] > 0, st[3] > SB_CUTOFF)

                    def step(st):
                        k_prev, c, acc, _ = st
                        k0 = _sb_next_key(k_prev)
                        part, c = block(qh, k0, SB_TK, k0 + col < k_prev, c)
                        return k0, c, acc + part, jnp.max(c)

                    st = lax.while_loop(cond, step, (_sb_first_key(q0), c0, jnp.zeros((2 * SB_TQ, LANES), F32),
                                                     jnp.max(c0)))
                    o_ref[pl.ds(q0, SB_TQ), :] += _unstack_heads(st[2], SB_TQ)

                return carry

            lax.fori_loop(0, nq, more, 0)

    def col_spec(j):
        return pl.BlockSpec((s, LANES), lambda p: (0, j * n_pairs + p))

    (o,), rides = _call(
        body, name=name, grid=(n_pairs,), out_shape=(jax.ShapeDtypeStruct((s, SB_WIDTH), F32),),
        in_specs=[col_spec(0), col_spec(1), col_spec(2), pl.BlockSpec((2 * LANES, LANES), lambda p: (0, 0))],
        out_specs=(pl.BlockSpec((s, LANES), lambda p: (0, p)),), scratch_shapes=[pltpu.VMEM((2 * s, LANES), F32)],
        args=(proj, proj, proj, tri2), ride=ride)
    return o, rides


def sb_attn_bwd(proj, tri2, o, do, *, name, ride=()):
    s = proj.shape[0]
    nq = s // SB_TQ
    n_pairs = SB_WIDTH // LANES

    def body(q_ref, k_ref, v_ref, tri_ref, o_ref, do_ref, dq_ref, dk_ref, dv_ref, dq_acc, dk_acc, dv_acc, c_all, e_all):
        dk_acc[...] = jnp.zeros_like(dk_acc)
        dv_acc[...] = jnp.zeros_like(dv_acc)

        def load(q0):
            qh = _stack_heads(q_ref[pl.ds(q0, SB_TQ), :]) * SCALE
            doh_b = _stack_heads(do_ref[pl.ds(q0, SB_TQ), :])
            ov = o_ref[pl.ds(q0, SB_TQ), :]
            dd = jnp.sum(doh_b.astype(F32) * jnp.concatenate([ov, ov], axis=0), axis=1, keepdims=True)
            return qh, doh_b, dd

        def block(qh, doh_b, dd, k0, tk, live, c, ce):
            kt = k_ref[pl.ds(k0, tk), :]
            z = _dot_nt(qh, kt)
            sp = _softplus(z)
            lb = z - sp
            tail, total = _suffix_sums(jnp.where(live, -sp, 0.0), tri_ref[...])
            wb = jnp.where(live, jnp.exp(lb + tail + c), 0.0).astype(BF16)
            e = wb.astype(F32) * _dot_nt(doh_b, v_ref[pl.ds(k0, tk), :])
            e_tail, e_total = _suffix_sums(e, tri_ref[...])
            dz = jnp.where(live, e - jnp.exp(lb) * (dd - ce - e_tail), 0.0)
            dzb = dz.astype(BF16)
            dk_acc[pl.ds(k0, tk), :] += _dot_tn(dzb, qh)
            dv_acc[pl.ds(k0, tk), :] += _dot_tn(wb, doh_b)
            return _dot(dzb, kt), c + total, ce + e_total

        def first(qb, carry):
            q0 = pl.multiple_of(qb * SB_TQ, SB_TQ)
            qh, doh_b, dd = load(q0)
            tpos, col = _sb_positions(q0, SB_TK1)
            k0 = _sb_first_key(q0)
            zero = jnp.zeros((2 * SB_TQ, 1), F32)
            dq, c, ce = block(qh, doh_b, dd, k0, SB_TK1, k0 + col < tpos, zero, zero)
            dq_acc[pl.ds(q0, SB_TQ), :] = _unstack_heads(dq, SB_TQ)
            c_all[_sb_rows(q0), :] = jnp.broadcast_to(jnp.where(k0 > 0, c, NEG), (2 * SB_TQ, LANES))
            e_all[_sb_rows(q0), :] = jnp.broadcast_to(ce, (2 * SB_TQ, LANES))
            return carry

        lax.fori_loop(0, nq, first, 0, unroll=4)

        @pl.when(jnp.max(c_all[...]) > SB_CUTOFF)
        def _():
            def more(qb, carry):
                q0 = pl.multiple_of(qb * SB_TQ, SB_TQ)
                c0 = c_all[_sb_rows(q0), 0:1]

                @pl.when(jnp.max(c0) > SB_CUTOFF)
                def _():
                    qh, doh_b, dd = load(q0)
                    _, col = _sb_positions(q0, SB_TK)

                    def cond(st):
                        return jnp.logical_and(st[0] > 0, st[4] > SB_CUTOFF)

                    def step(st):
                        k_prev, c, ce, dq, _ = st
                        k0 = _sb_next_key(k_prev)
                        part, c, ce = block(qh, doh_b, dd, k0, SB_TK, k0 + col < k_prev, c, ce)
                        return k0, c, ce, dq + part, jnp.max(c)

                    st = lax.while_loop(cond, step, (_sb_first_key(q0), c0, e_all[_sb_rows(q0), 0:1],
                                                     jnp.zeros((2 * SB_TQ, LANES), F32), jnp.max(c0)))
                    dq_acc[pl.ds(q0, SB_TQ), :] += _unstack_heads(st[3], SB_TQ)

                return carry

            lax.fori_loop(0, nq, more, 0)

        dq_ref[...] = (dq_acc[...] * SCALE).astype(BF16)
        dk_ref[...] = dk_acc[...].astype(BF16)
        dv_ref[...] = dv_acc[...].astype(BF16)

    def col_spec(j):
        return pl.BlockSpec((s, LANES), lambda p: (0, j * n_pairs + p))

    pair = pl.BlockSpec((s, LANES), lambda p: (0, p))
    (dq, dk, dv), rides = _call(
        body, name=name, grid=(n_pairs,), out_shape=(jax.ShapeDtypeStruct((s, SB_WIDTH), BF16),) * 3,
        in_specs=[col_spec(0), col_spec(1), col_spec(2), pl.BlockSpec((2 * LANES, LANES), lambda p: (0, 0)), pair, pair],
        out_specs=(pair, pair, pair),
        scratch_shapes=[pltpu.VMEM((s, LANES), F32)] * 3 + [pltpu.VMEM((2 * s, LANES), F32)] * 2,
        args=(proj, proj, proj, tri2, o, do), ride=ride)
    return dq, dk, dv, rides


def _lane_lo():
    return lax.broadcasted_iota(jnp.int32, (1, LANES), 1) < HEAD_DIM


def _swap_halves(x):
    return pltpu.roll(x, HEAD_DIM, 1)


def _rot_half(y):
    first = (lax.broadcasted_iota(jnp.int32, (1, LANES), 1) % HEAD_DIM) < (HEAD_DIM // 2)
    return jnp.where(first, pltpu.roll(y, LANES - HEAD_DIM // 2, 1), pltpu.roll(y, HEAD_DIM // 2, 1))


def _head_mean(v, avg):
    hi, lo = _split_bf16(v)
    return _dot(hi, avg) + _dot(lo, avg)


def _head_avg_matrix():
    lane = jnp.arange(LANES) // HEAD_DIM
    return ((lane[:, None] == lane[None, :]).astype(F32) * (1.0 / HEAD_DIM)).astype(BF16)


def swa_prep_fwd(proj, cos_p, sin_p, gq, gk, *, name):
    s = proj.shape[0]
    tm = min(512, s)
    q_blk = (3 * SB_WIDTH) // SWA_Q_WIDTH
    k_blk = (3 * SB_WIDTH + SWA_Q_WIDTH) // LANES

    def body(q_ref, k_ref, cos_ref, sin_ref, gq_ref, gk_ref, avg_ref, qn_ref, kn_ref):
        cosv, sinv, avg = cos_ref[...], sin_ref[...], avg_ref[...]

        def norm_rope(xv, g):
            y = (xv * lax.rsqrt(_head_mean(xv * xv, avg) + NORM_EPS)) * g
            return y * cosv + _rot_half(y) * sinv

        for j in range(SWA_Q_WIDTH // LANES):
            sl = slice(j * LANES, (j + 1) * LANES)
            qn_ref[:, sl] = norm_rope(q_ref[:, sl].astype(F32), gq_ref[...]).astype(BF16)
        kn_ref[...] = norm_rope(k_ref[...].astype(F32), gk_ref[...]).astype(BF16)

    row = lambda i: (i, 0)
    fixed = lambda i: (0, 0)
    return pl.pallas_call(
        body, name=name, grid=(s // tm,),
        out_shape=(jax.ShapeDtypeStruct((s, SWA_Q_WIDTH), BF16), jax.ShapeDtypeStruct((s, LANES), BF16)),
        in_specs=[pl.BlockSpec((tm, SWA_Q_WIDTH), lambda i: (i, q_blk)), pl.BlockSpec((tm, LANES), lambda i: (i, k_blk)),
                  pl.BlockSpec((tm, LANES), row), pl.BlockSpec((tm, LANES), row),
                  pl.BlockSpec((1, LANES), fixed), pl.BlockSpec((1, LANES), fixed), pl.BlockSpec((LANES, LANES), fixed)],
        out_specs=(pl.BlockSpec((tm, SWA_Q_WIDTH), row), pl.BlockSpec((tm, LANES), row)),
        compiler_params=_params("parallel"),
    )(proj, proj, cos_p, sin_p, gq, gk, _head_avg_matrix())


def swa_prep_bwd(proj, cos_p, sin_p, gq, gk, dqn, dkn, dv, *, name):
    s = proj.shape[0]
    tm = min(512, s)
    q_blk = (3 * SB_WIDTH) // SWA_Q_WIDTH
    k_blk = (3 * SB_WIDTH + SWA_Q_WIDTH) // LANES

    def body(q_ref, k_ref, cos_ref, sin_ref, gq_ref, gk_ref, avg_ref, dqn_ref, dkn_ref, dv_ref, dq_ref, dk_ref, dvb_ref,
             dgq_ref, dgk_ref):
        @pl.when(pl.program_id(0) == 0)
        def _():
            dgq_ref[...] = jnp.zeros_like(dgq_ref)
            dgk_ref[...] = jnp.zeros_like(dgk_ref)

        cosv, sinv, avg = cos_ref[...], sin_ref[...], avg_ref[...]

        def bwd(xv, g, dout):
            dy = dout * cosv + _rot_half(dout * sinv)
            r = lax.rsqrt(_head_mean(xv * xv, avg) + NORM_EPS)
            dyg = dy * g
            dx = r * dyg - xv * ((r * r * r) * _head_mean(dyg * xv, avg))
            return dx, jnp.sum(dy * (xv * r), axis=0, keepdims=True)

        for j in range(SWA_Q_WIDTH // LANES):
            sl = slice(j * LANES, (j + 1) * LANES)
            dx, dg = bwd(q_ref[:, sl].astype(F32), gq_ref[...], dqn_ref[:, sl])
            dq_ref[:, sl] = dx.astype(BF16)
            dgq_ref[:, sl] += dg
        dx, dg = bwd(k_ref[...].astype(F32), gk_ref[...], dkn_ref[...])
        dk_ref[...] = dx.astype(BF16)
        dgk_ref[...] += dg
        dvb_ref[...] = dv_ref[...].astype(BF16)

    row = lambda i: (i, 0)
    fixed = lambda i: (0, 0)
    lane_row = pl.BlockSpec((tm, LANES), row)
    return pl.pallas_call(
        body, name=name, grid=(s // tm,),
        out_shape=(jax.ShapeDtypeStruct((s, SWA_Q_WIDTH), BF16), jax.ShapeDtypeStruct((s, LANES), BF16),
                   jax.ShapeDtypeStruct((s, LANES), BF16),
                   jax.ShapeDtypeStruct((1, SWA_Q_WIDTH), F32), jax.ShapeDtypeStruct((1, LANES), F32)),
        in_specs=[pl.BlockSpec((tm, SWA_Q_WIDTH), lambda i: (i, q_blk)), pl.BlockSpec((tm, LANES), lambda i: (i, k_blk)),
                  lane_row, lane_row, pl.BlockSpec((1, LANES), fixed), pl.BlockSpec((1, LANES), fixed),
                  pl.BlockSpec((LANES, LANES), fixed), pl.BlockSpec((tm, SWA_Q_WIDTH), row), lane_row, lane_row],
        out_specs=(pl.BlockSpec((tm, SWA_Q_WIDTH), row), lane_row, lane_row,
                   pl.BlockSpec((1, SWA_Q_WIDTH), fixed), pl.BlockSpec((1, LANES), fixed)),
        compiler_params=_params("arbitrary"),
    )(proj, proj, cos_p, sin_p, gq, gk, _head_avg_matrix(), dqn, dkn, dv)


def _swa_kv_copies(k_ref, v_ref, kg_ref, vg_ref, second_kv):
    s = k_ref.shape[0]
    rows = min(512, s)
    keep = jnp.logical_xor(_lane_lo(), second_kv)

    def chunk(r, carry):
        sl = pl.ds(pl.multiple_of(r * rows, rows), rows)
        for src, dst in ((k_ref, kg_ref), (v_ref, vg_ref)):
            f = src[sl, :].astype(F32)
            dst[sl, :] = jnp.where(keep, f, _swap_halves(f)).astype(BF16)
        return carry

    lax.fori_loop(0, s // rows, chunk, 0)


def _swa_tile(i, kg_ref, vg_ref):
    q0 = pl.multiple_of(i * SWA_TQ, SWA_TQ)
    k0 = pl.multiple_of(jnp.maximum(i - 1, 0) * SWA_TQ, SWA_TQ)
    kg = kg_ref[pl.ds(k0, SWA_TK), :]
    vg = vg_ref[pl.ds(k0, SWA_TK), :]
    row = lax.broadcasted_iota(jnp.int32, (2 * SWA_TQ, SWA_TK), 0)
    tpos = q0 + jnp.where(row >= SWA_TQ, row - SWA_TQ, row)
    spos = k0 + lax.broadcasted_iota(jnp.int32, (2 * SWA_TQ, SWA_TK), 1)
    valid = jnp.logical_and(spos <= tpos, spos > tpos - WINDOW)
    return q0, k0, kg, vg, valid


def _swa_probs(qh, kg, valid, sink):
    z = jnp.where(valid, _dot_nt(qh, kg) * SCALE, NEG)
    m = jnp.maximum(jnp.max(z, axis=1, keepdims=True), sink)
    pexp = jnp.exp(z - m)
    psink = jnp.exp(sink - m)
    inv = 1.0 / (jnp.sum(pexp, axis=1, keepdims=True) + psink)
    return pexp * inv, psink * inv


def _stacked_sink(sink_row):
    s0 = jnp.sum(jnp.where(_head_mask(0), sink_row, 0.0), axis=1, keepdims=True) * (1.0 / HEAD_DIM)
    s1 = jnp.sum(jnp.where(_head_mask(1), sink_row, 0.0), axis=1, keepdims=True) * (1.0 / HEAD_DIM)
    top = lax.broadcasted_iota(jnp.int32, (2 * SWA_TQ, 1), 0) < SWA_TQ
    return jnp.where(top, s0, s1)


def swa_attn_fwd(qn, kn, proj, sink_p, *, name, ride=()):
    s = qn.shape[0]
    nq = s // SWA_TQ
    n_pairs = SWA_Q_WIDTH // LANES
    v_blk = (3 * SB_WIDTH + SWA_Q_WIDTH + SWA_KV_WIDTH) // LANES

    def body(q_ref, k_ref, v_ref, s_ref, o_ref, kg_ref, vg_ref):
        _swa_kv_copies(k_ref, v_ref, kg_ref, vg_ref, (pl.program_id(0) // 2) == 1)
        sink = _stacked_sink(s_ref[...])

        def tile(i, carry):
            q0, _, kg, vg, valid = _swa_tile(i, kg_ref, vg_ref)
            probs, _ = _swa_probs(_stack_heads(q_ref[pl.ds(q0, SWA_TQ), :]), kg, valid, sink)
            o_ref[pl.ds(q0, SWA_TQ), :] = _unstack_heads(_dot(probs.astype(BF16), vg), SWA_TQ)
            return carry

        lax.fori_loop(0, nq, tile, 0, unroll=4)

    pair = pl.BlockSpec((s, LANES), lambda p: (0, p))
    whole = pl.BlockSpec((s, LANES), lambda p: (0, 0))
    (o,), rides = _call(
        body, name=name, grid=(n_pairs,), out_shape=(jax.ShapeDtypeStruct((s, SWA_Q_WIDTH), F32),),
        in_specs=[pair, whole, pl.BlockSpec((s, LANES), lambda p: (0, v_blk)),
                  pl.BlockSpec((None, 1, LANES), lambda p: (p, 0, 0))],
        out_specs=(pair,), scratch_shapes=[pltpu.VMEM((s, LANES), BF16)] * 2, args=(qn, kn, proj, sink_p), ride=ride)
    return o, rides


def swa_attn_bwd(qn, kn, proj, sink_p, o, do, *, name, ride=()):
    s = qn.shape[0]
    nq = s // SWA_TQ
    n_pairs = SWA_Q_WIDTH // LANES
    v_blk = (3 * SB_WIDTH + SWA_Q_WIDTH + SWA_KV_WIDTH) // LANES
    fold_rows = min(512, s)

    def body(q_ref, k_ref, v_ref, s_ref, o_ref, do_ref, dq_ref, dk_ref, dv_ref, ds_ref, acc_k, acc_v, kg_ref, vg_ref):
        p = pl.program_id(0)
        _swa_kv_copies(k_ref, v_ref, kg_ref, vg_ref, (p // 2) == 1)
        sink = _stacked_sink(s_ref[...])

        @pl.when(p % 2 == 0)
        def _():
            acc_k[...] = jnp.zeros_like(acc_k)
            acc_v[...] = jnp.zeros_like(acc_v)

        ds_ref[...] = jnp.zeros_like(ds_ref)

        def tile(i, carry):
            q0, k0, kg, vg, valid = _swa_tile(i, kg_ref, vg_ref)
            qh = _stack_heads(q_ref[pl.ds(q0, SWA_TQ), :])
            doh_b = _stack_heads(do_ref[pl.ds(q0, SWA_TQ), :])
            ov = o_ref[pl.ds(q0, SWA_TQ), :]
            delta = jnp.sum(doh_b.astype(F32) * jnp.concatenate([ov, ov], axis=0), axis=1, keepdims=True)
            probs, psink = _swa_probs(qh, kg, valid, sink)
            dz = probs * (_dot_nt(doh_b, vg) - delta)
            dzb = (dz * SCALE).astype(BF16)
            dq_ref[pl.ds(q0, SWA_TQ), :] = _unstack_heads(_dot(dzb, kg), SWA_TQ)
            acc_k[pl.ds(k0, SWA_TK), :] += _dot_tn(dzb, qh)
            acc_v[pl.ds(k0, SWA_TK), :] += _dot_tn(probs.astype(BF16), doh_b)
            pd = psink * delta
            ds_ref[...] -= jnp.where(_head_mask(0), jnp.sum(pd[:SWA_TQ], axis=0, keepdims=True),
                                     jnp.sum(pd[SWA_TQ:], axis=0, keepdims=True))
            return carry

        lax.fori_loop(0, nq, tile, 0, unroll=4)

        def fold_into(first_head):
            def fold(r, carry):
                rows = pl.ds(pl.multiple_of(r * fold_rows, fold_rows), fold_rows)
                for acc, out in ((acc_k, dk_ref), (acc_v, dv_ref)):
                    a = acc[rows, :]
                    both = a + _swap_halves(a)
                    if first_head:
                        out[rows, :] = jnp.where(_lane_lo(), both, 0.0)
                    else:
                        out[rows, :] = jnp.where(_lane_lo(), out[rows, :], both)
                return carry

            lax.fori_loop(0, s // fold_rows, fold, 0)

        @pl.when(p == 1)
        def _():
            fold_into(True)

        @pl.when(p == 3)
        def _():
            fold_into(False)

    pair = pl.BlockSpec((s, LANES), lambda p: (0, p))
    whole = pl.BlockSpec((s, LANES), lambda p: (0, 0))
    sink_spec = pl.BlockSpec((None, 1, LANES), lambda p: (p, 0, 0))
    (dq, dk, dv, dsink), rides = _call(
        body, name=name, grid=(n_pairs,),
        out_shape=(jax.ShapeDtypeStruct((s, SWA_Q_WIDTH), F32), jax.ShapeDtypeStruct((s, LANES), F32),
                   jax.ShapeDtypeStruct((s, LANES), F32), jax.ShapeDtypeStruct((n_pairs, 1, LANES), F32)),
        in_specs=[pair, whole, pl.BlockSpec((s, LANES), lambda p: (0, v_blk)), sink_spec, pair, pair],
        out_specs=(pair, whole, whole, sink_spec),
        scratch_shapes=[pltpu.VMEM((s, LANES), F32)] * 2 + [pltpu.VMEM((s, LANES), BF16)] * 2,
        args=(qn, kn, proj, sink_p, o, do), ride=ride)
    return dq, dk, dv, dsink, rides


def _rope_tables(s):
    inv_freq = 1.0 / (ROPE_THETA ** (jnp.arange(0, HEAD_DIM, 2, dtype=F32) / HEAD_DIM))
    ang = jnp.arange(s, dtype=F32)[:, None] * inv_freq[None, :]
    cos, sin = jnp.cos(ang), jnp.sin(ang)
    cos_p = jnp.tile(jnp.concatenate([cos, cos], axis=1), (1, LANES // HEAD_DIM))
    sin_p = jnp.tile(jnp.concatenate([-sin, sin], axis=1), (1, LANES // HEAD_DIM))
    return cos_p, sin_p


def _lane_tile(v, reps):
    return jnp.tile(v.reshape(1, -1), (1, reps))


def _natural(stack, w):
    n, r, c = stack.shape
    if MATRIX_NAMES[w] in ROW_SHARDED or w == W_IN:
        return stack.reshape(n * r, c)
    if w == W_UP:
        return stack
    return jnp.transpose(stack, (1, 0, 2)).reshape(r, n * c)


def _pack_small(tree):
    flat = jnp.concatenate([tree[n].reshape(-1) for n in SMALL_NAMES])
    rows = -(-flat.shape[0] // (8 * LANES)) * 8
    return jnp.pad(flat, (0, rows * LANES - flat.shape[0])).reshape(rows, LANES)


def _unpack_small(packed, shapes):
    flat, out, off = packed.reshape(-1), {}, 0
    for n in SMALL_NAMES:
        size = shapes[n][0] * shapes[n][1]
        out[n] = flat[off:off + size].reshape(shapes[n])
        off += size
    return out


def train_step(x, target, weights, mom_m, mom_v):
    s = x.shape[0]
    cos_p, sin_p = _rope_tables(s)
    tri = (jnp.arange(LANES)[:, None] > jnp.arange(LANES)[None, :]).astype(BF16)
    tri = jnp.concatenate([tri, tri], axis=0)
    local = {n: (jnp.swapaxes(t, 1, 2) if n == "w_in" else t) for n, t in weights.items()}
    local_m = {n: (jnp.swapaxes(t, 1, 2) if n == "w_in" else t) for n, t in mom_m.items()}
    local_v = {n: (jnp.swapaxes(t, 1, 2) if n == "w_in" else t) for n, t in mom_v.items()}
    shards = [[local[n][l].astype(BF16) for n in MATRIX_NAMES] for l in range(DEPTH)]
    core = lax.axis_index("c").astype(jnp.int32).reshape(1)
    chip = (2 * lax.axis_index("x") + lax.axis_index("y")).astype(jnp.int32).reshape(1)

    def gather(l, ws):
        return GatherJob([shards[l][w] for w in ws])

    w_in = _natural(exchange_alone(gather(0, [W_IN]), name="gather_w_in0")[0], W_IN)
    saved = []
    for l in range(DEPTH):
        g_mix = weights["mix_norm_g"][l].reshape(1, D_MODEL)
        g_mlp = weights["mlp_norm_g"][l].reshape(1, D_MODEL)
        gq = _lane_tile(weights["q_norm_g"][l], LANES // HEAD_DIM)
        gk = _lane_tile(weights["k_norm_g"][l], LANES // HEAD_DIM)
        sink_p = jnp.repeat(weights["sinks"][l].reshape(SWA_Q_WIDTH // LANES, 2), HEAD_DIM, axis=1)
        sink_p = sink_p.reshape(SWA_Q_WIDTH // LANES, 1, LANES)
        (h, proj, gates), ((s_bsb, s_bsw, s_out),) = norm_matmul(
            x, g_mix, w_in, gate_split=ATTN_WIDTH, name="in_proj", ride=[gather(l, [W_BSB, W_BSW, W_OUT])])
        if l + 1 < DEPTH:
            o_sb, ((s_up,), (s_in,)) = sb_attn_fwd(proj, tri, name="sb_fwd",
                                                   ride=[gather(l, [W_UP]), gather(l + 1, [W_IN])])
        else:
            o_sb, ((s_up,),) = sb_attn_fwd(proj, tri, name="sb_fwd_last", ride=[gather(l, [W_UP])])
        qn, kn = swa_prep_fwd(proj, cos_p, sin_p, gq, gk, name="swa_prep")
        o_sw, ((s_down,),) = swa_attn_fwd(qn, kn, proj, sink_p, name="swa_fwd", ride=[gather(l, [W_DOWN])])
        mats = [w_in, _natural(s_bsb, W_BSB), _natural(s_bsw, W_BSW), _natural(s_out, W_OUT), _natural(s_up, W_UP),
                _natural(s_down, W_DOWN)]
        x1, y_sb, y_sw, merged = merge_out_fwd(x, o_sb, o_sw, gates, mats[W_BSB], mats[W_BSW], mats[W_OUT],
                                               name="merge_out")
        (h2, u), _ = norm_matmul(x1, g_mlp, mats[W_UP], gate_split=None, name="mlp_up")
        x2 = mlp_down_fwd(x1, u, mats[W_DOWN], name="mlp_down")
        if l + 1 < DEPTH:
            w_in = _natural(s_in, W_IN)
        saved.append(dict(x=x, h=h, proj=proj, gates=gates, o_sb=o_sb, qn=qn, kn=kn, o_sw=o_sw, y_sb=y_sb, y_sw=y_sw,
                          merged=merged, x1=x1, h2=h2, u=u, g_mix=g_mix, g_mlp=g_mlp, gq=gq, gk=gk, sink_p=sink_p,
                          mats=mats))
        x = x2

    dx, dxb, loss = loss_head(x, target, name="loss_head")

    shard_shapes = [local[n].shape[1:] for n in MATRIX_NAMES]
    parts = [lax.empty((DEPTH, N_CHIPS) + sh, BF16) for sh in shard_shapes]
    lands = [lax.empty((DEPTH, 3) + sh, BF16) for sh in shard_shapes]
    small_grads = {n: [None] * DEPTH for n in SMALL_NAMES}
    half = D_MODEL // 2

    def summed(l, ws, grads, landed):
        new = pair_sum(l, grads, landed, [parts[w] for w in ws], core, name="grad_pair_sum")
        for w, p in zip(ws, new):
            parts[w] = p

    def chip_job(items):
        return ChipJob(items, parts, lands)

    def landed_chip(job, outs):
        for w, a in zip(job.ws, outs):
            lands[w] = a

    in_pending = None
    for l in reversed(range(DEPTH)):
        a = saved[l]
        mats = a["mats"]
        (du,), _ = mlp_bwd_up(dxb, a["u"], mats[W_DOWN], name="mlp_bwd_up")
        dw_down = matmul_tn(a["u"], [dxb], a_block=half, out_cols=None, relu2=True, name="dw_down")
        dw_up = matmul_tn(a["h2"], [du], a_block=half, out_cols=du.shape[1] // N_DEV, relu2=False, name="dw_up")
        g_mlp_w = [dw_up, dw_down.reshape((N_DEV,) + shard_shapes[W_DOWN])]
        (dx1, dx1b, dg_mlp), (landed,) = matmul_nt_norm_bwd([du], mats[W_UP], a["x1"], a["g_mlp"], dx,
                                                            name="mlp_bwd_norm", ride=[PairJob(g_mlp_w)])
        summed(l, [W_UP, W_DOWN], g_mlp_w, landed)
        small_grads["mlp_norm_g"][l] = dg_mlp.reshape(D_MODEL)
        dw_out = matmul_tn(a["merged"], [dx1b], a_block=half, out_cols=None, relu2=False, name="dw_out")
        dy_sb, dy_sw, do_sb, do_sw, dgl = out_bwd(dx1b, mats[W_OUT], a["gates"], a["y_sb"], a["y_sw"],
                                                  mats[W_BSB], mats[W_BSW], name="out_bwd")
        dw_bsb = matmul_tn(a["o_sb"], [dy_sb], a_block=half, out_cols=D_MODEL // N_DEV, relu2=False, name="dw_branch_sb")
        dw_bsw = matmul_tn(a["o_sw"], [dy_sw], a_block=half, out_cols=D_MODEL // N_DEV, relu2=False, name="dw_branch_swa")
        g_mix_w = [dw_bsb, dw_bsw, dw_out.reshape((N_DEV,) + shard_shapes[W_OUT])]
        job = chip_job([(l, W_UP), (l, W_DOWN)])
        dq_sb, dk_sb, dv_sb, (outs, landed) = sb_attn_bwd(a["proj"], tri, a["o_sb"], do_sb, name="sb_bwd",
                                                         ride=[job, PairJob(g_mix_w)])
        landed_chip(job, outs)
        summed(l, [W_BSB, W_BSW, W_OUT], g_mix_w, landed)
        job = chip_job([(l, W_BSB), (l, W_BSW), (l, W_OUT)] + ([(in_pending, W_IN)] if in_pending is not None else []))
        dqn, dkn, dv_sw, dsink, (outs,) = swa_attn_bwd(a["qn"], a["kn"], a["proj"], a["sink_p"], a["o_sw"], do_sw,
                                                      name="swa_bwd", ride=[job])
        landed_chip(job, outs)
        dq_sw, dk_sw, dv_swb, dgq, dgk = swa_prep_bwd(a["proj"], cos_p, sin_p, a["gq"], a["gk"], dqn, dkn, dv_sw,
                                                      name="swa_prep_bwd")
        small_grads["q_norm_g"][l] = dgq.reshape(SWA_Q_WIDTH // HEAD_DIM, HEAD_DIM).sum(0)
        small_grads["k_norm_g"][l] = dgk.reshape(LANES // HEAD_DIM, HEAD_DIM).sum(0)
        small_grads["sinks"][l] = dsink[:, 0, ::HEAD_DIM].reshape(SWA_Q_WIDTH // HEAD_DIM)
        pieces = [dq_sb, dk_sb, dv_sb, dq_sw, dk_sw, dv_swb, dgl]
        g_in = [matmul_tn_row_blocks(pieces, a["h"], n_blocks=N_DEV, name="dw_in")]
        if l > 0:
            (dx, dxb, dg_mix), (landed,) = matmul_nt_norm_bwd(pieces, mats[W_IN], a["x"], a["g_mix"], dx1,
                                                             name="in_proj_bwd", ride=[PairJob(g_in)])
            summed(l, [W_IN], g_in, landed)
            in_pending = l
        else:
            summed(l, [W_IN], g_in, exchange_alone(PairJob(g_in), name="grad_pair_exchange_in0"))
            job = chip_job([(l, W_IN)])
            (dx, dxb, dg_mix), (outs,) = matmul_nt_norm_bwd(pieces, mats[W_IN], a["x"], a["g_mix"], dx1,
                                                           name="in_proj_bwd_last", ride=[job])
            landed_chip(job, outs)
        small_grads["mix_norm_g"][l] = dg_mix.reshape(D_MODEL)

    out_g, out_d, out_m, out_v = {}, {}, {}, {}
    for i, n in enumerate(MATRIX_NAMES):
        outs = reduce_adamw(parts[i], lands[i], chip, local[n], local_m[n], local_v[n], name="adamw_" + n)
        if n == "w_in":
            outs = [jnp.swapaxes(t, 1, 2) for t in outs]
        out_g[n], out_d[n], out_m[n], out_v[n] = outs
    small_shapes = {n: weights[n].shape for n in SMALL_NAMES}
    small_all = gather_small(_pack_small({n: jnp.stack(v) for n, v in small_grads.items()}), name="gather_small_grads")
    sg, sd, sm, sv = small_adamw(small_all, _pack_small(weights), _pack_small(mom_m), _pack_small(mom_v),
                                 name="small_adamw")
    for tree, packed_small in ((out_g, sg), (out_d, sd), (out_m, sm), (out_v, sv)):
        tree.update(_unpack_small(packed_small, small_shapes))
    return loss, dx, (out_g, out_d, out_m, out_v)


def kernel(x, mix_norm_g, w_in, q_norm_g, k_norm_g, sinks, w_branch_sb, w_branch_swa, w_out, mlp_norm_g, w_up, w_down, loss_target, m_mix_norm_g, m_w_in, m_q_norm_g, m_k_norm_g, m_sinks, m_w_branch_sb, m_w_branch_swa, m_w_out, m_mlp_norm_g, m_w_up, m_w_down, v_mix_norm_g, v_w_in, v_q_norm_g, v_k_norm_g, v_sinks, v_w_branch_sb, v_w_branch_swa, v_w_out, v_mlp_norm_g, v_w_up, v_w_down):
    weights = dict(mix_norm_g=mix_norm_g, w_in=w_in, q_norm_g=q_norm_g, k_norm_g=k_norm_g, sinks=sinks,
                   w_branch_sb=w_branch_sb, w_branch_swa=w_branch_swa, w_out=w_out, mlp_norm_g=mlp_norm_g, w_up=w_up,
                   w_down=w_down)
    mom_m = dict(mix_norm_g=m_mix_norm_g, w_in=m_w_in, q_norm_g=m_q_norm_g, k_norm_g=m_k_norm_g, sinks=m_sinks,
                 w_branch_sb=m_w_branch_sb, w_branch_swa=m_w_branch_swa, w_out=m_w_out, mlp_norm_g=m_mlp_norm_g,
                 w_up=m_w_up, w_down=m_w_down)
    mom_v = dict(mix_norm_g=v_mix_norm_g, w_in=v_w_in, q_norm_g=v_q_norm_g, k_norm_g=v_k_norm_g, sinks=v_sinks,
                 w_branch_sb=v_w_branch_sb, w_branch_swa=v_w_branch_swa, w_out=v_w_out, mlp_norm_g=v_mlp_norm_g,
                 w_up=v_w_up, w_down=v_w_down)
    loss_part, grad_x, outs = train_step(x[0], loss_target[0], weights, mom_m, mom_v)
    loss = lax.psum(loss_part[0, 0], MESH_AXES)
    return (loss, grad_x[None], *[outs[0][n] for n in WEIGHT_ORDER], *[outs[1][n] for n in WEIGHT_ORDER],
            *[outs[2][n] for n in WEIGHT_ORDER], *[outs[3][n] for n in WEIGHT_ORDER])
```

```python
import functools
import math

import jax
import jax.numpy as jnp
from jax import lax
from jax.experimental import pallas as pl
from jax.experimental.pallas import tpu as pltpu

F32 = jnp.float32
BF16 = jnp.bfloat16

DEPTH = 4
D_MODEL = 1024
HEAD_DIM = 64
LANES = 128
WINDOW = 128
SB_WIDTH = 512
SWA_Q_WIDTH = 512
SWA_KV_WIDTH = 128
ATTN_WIDTH = 3 * SB_WIDTH + SWA_Q_WIDTH + 2 * SWA_KV_WIDTH
IN_WIDTH = ATTN_WIDTH + 2 * D_MODEL
ROPE_THETA = 10000.0
NORM_EPS = 1e-6
SCALE = HEAD_DIM ** -0.5
NEG = -1e30
N_DEV = 8
N_CHIPS = 4

ADAM_LR = 0.001
ADAM_B1 = 0.9
ADAM_B2 = 0.999
ADAM_EPS = 1e-08
ADAM_WD = 0.01
ADAM_STEP = 10

SB_TQ = 128
SB_TK1 = 384
SB_TK = 256
SB_CUTOFF = -88.0
SWA_TQ = 128
SWA_TK = 256
ROW_TILE = 512
VMEM_LIMIT = 56 * 1024 * 1024

MATRIX_NAMES = ("w_in", "w_branch_sb", "w_branch_swa", "w_out", "w_up", "w_down")
W_IN, W_BSB, W_BSW, W_OUT, W_UP, W_DOWN = range(6)
ROW_SHARDED = ("w_out", "w_down")
SMALL_NAMES = ("mix_norm_g", "q_norm_g", "k_norm_g", "sinks", "mlp_norm_g")
WEIGHT_ORDER = ("mix_norm_g", "w_in", "q_norm_g", "k_norm_g", "sinks", "w_branch_sb", "w_branch_swa", "w_out",
                "mlp_norm_g", "w_up", "w_down")
MESH_AXES = ("x", "y", "c")

ANY = pl.BlockSpec(memory_space=pl.ANY)
MESH = pl.DeviceIdType.MESH


def _params(*sem):
    return pltpu.CompilerParams(dimension_semantics=sem, vmem_limit_bytes=VMEM_LIMIT)


def _dot(a, b):
    return jnp.dot(a, b, preferred_element_type=F32)


def _dot_nt(a, b):
    return lax.dot_general(a, b, (((1,), (1,)), ((), ())), preferred_element_type=F32)


def _dot_tn(a, b):
    return lax.dot_general(a, b, (((0,), (0,)), ((), ())), preferred_element_type=F32)


def _split_bf16(x):
    hi = lax.bitcast_convert_type(lax.bitcast_convert_type(x, jnp.uint32) & jnp.uint32(0xFFFF0000), F32)
    return hi.astype(BF16), (x - hi).astype(BF16)


def _rsqrt_ms(x):
    return lax.rsqrt(jnp.mean(x * x, axis=-1, keepdims=True) + NORM_EPS)


def _place():
    return lax.axis_index("x"), lax.axis_index("y"), lax.axis_index("c")


class _Gather:
    def __init__(self, x_refs, out_refs, send_sems, recv_sems, local_sems, rows=None):
        self.x_refs, self.out_refs = x_refs, out_refs
        self.send_sems, self.recv_sems, self.local_sems = send_sems, recv_sems, local_sems
        self.n = len(x_refs)
        self.rows = rows
        x, y, c = _place()
        self.c = c
        self.me, self.sibling = (x, y, c), (x, y, 1 - c)
        self.chips = [(1 - x, y), (x, 1 - y), (1 - x, 1 - y)]

    def _part(self, ref):
        return ref if self.rows is None else ref.at[pl.ds(*self.rows)]

    def _slot(self, w, blk):
        return self._part(self.out_refs[w].at[4 * blk[0] + 2 * blk[1] + blk[2]])

    def _copy(self, k, w, blk, to, own=False):
        dst = self._slot(w, blk)
        return pltpu.make_async_remote_copy(
            src_ref=self._part(self.x_refs[w]) if own else dst, dst_ref=dst, send_sem=self.send_sems.at[k, w],
            recv_sem=self.recv_sems.at[k, w], device_id=to, device_id_type=MESH)

    def _mine(self, w):
        return pltpu.make_async_copy(self._part(self.x_refs[w]), self._slot(w, self.me), self.local_sems.at[w])

    def _first(self, w):
        return [self._copy(0, w, self.me, self.sibling, own=True)] + [
            self._copy(1 + j, w, self.me, (*chip, self.c), own=True) for j, chip in enumerate(self.chips)]

    def _passed(self, j, w):
        return self._copy(4 + j, w, (*self.chips[j], self.c), self.sibling)

    def start(self):
        for w in range(self.n):
            self._mine(w).start()
            for cp in self._first(w):
                cp.start()

    def relay(self):
        for j, chip in enumerate(self.chips):
            for w in range(self.n):
                self._copy(1 + j, w, (*chip, self.c), self.me).wait_recv()
                self._passed(j, w).start()

    def finish(self):
        for w in range(self.n):
            self._copy(0, w, self.sibling, self.me).wait_recv()
            for j, chip in enumerate(self.chips):
                self._copy(4 + j, w, (*chip, 1 - self.c), self.me).wait_recv()
            for cp in self._first(w):
                cp.wait_send()
            for j in range(3):
                self._passed(j, w).wait_send()
            self._mine(w).wait()


class GatherJob:
    def __init__(self, shards, rows=None, stacks=None):
        n = len(shards)
        self.n, self.rows = n, rows
        self.inputs = list(shards) + (list(stacks) if stacks is not None else [])
        self.out_shapes = [jax.ShapeDtypeStruct((N_DEV,) + s.shape, s.dtype) for s in shards]
        self.aliases = {n + i: i for i in range(n)} if stacks is not None else {}
        self.scratch = [pltpu.SemaphoreType.DMA((7, n)), pltpu.SemaphoreType.DMA((7, n)),
                        pltpu.SemaphoreType.DMA((n,))]

    def bind(self, in_refs, out_refs, scratch_refs):
        return _Gather(in_refs[:self.n], out_refs, *scratch_refs, rows=self.rows)


class _Copies:
    def __init__(self, copies):
        self.copies = copies

    def start(self):
        for cp in self.copies:
            cp.start()

    def relay(self):
        pass

    def finish(self):
        for cp in self.copies:
            cp.wait_recv()
        for cp in self.copies:
            cp.wait_send()


class ChipJob:
    def __init__(self, items, parts, lands):
        self.ws = sorted({item[1] for item in items})
        n = len(self.ws)
        self.items = [(item[0], self.ws.index(item[1]), item[2] if len(item) > 2 else None) for item in items]
        self.inputs = [parts[w] for w in self.ws] + [lands[w] for w in self.ws]
        self.out_shapes = [jax.ShapeDtypeStruct(lands[w].shape, lands[w].dtype) for w in self.ws]
        self.aliases = {n + i: i for i in range(n)}
        self.scratch = [pltpu.SemaphoreType.DMA((3, n)), pltpu.SemaphoreType.DMA((3, n))]

    def bind(self, in_refs, out_refs, scratch_refs):
        send_sems, recv_sems = scratch_refs
        x, y, c = _place()
        chips = [(1 - x, y), (x, 1 - y), (1 - x, 1 - y)]

        def part(ref, rows):
            return ref if rows is None else ref.at[pl.ds(*rows)]

        return _Copies([pltpu.make_async_remote_copy(
            src_ref=part(in_refs[i].at[layer, 2 * px + py], rows), dst_ref=part(out_refs[i].at[layer, j], rows),
            send_sem=send_sems.at[j, i], recv_sem=recv_sems.at[j, i], device_id=(px, py, c), device_id_type=MESH)
            for layer, i, rows in self.items for j, (px, py) in enumerate(chips)])


class PairJob:
    def __init__(self, grads):
        n = len(grads)
        self.inputs = list(grads)
        self.out_shapes = [jax.ShapeDtypeStruct((N_CHIPS,) + g.shape[1:], g.dtype) for g in grads]
        self.aliases = {}
        self.scratch = [pltpu.SemaphoreType.DMA((N_CHIPS, n)), pltpu.SemaphoreType.DMA((N_CHIPS, n))]

    def bind(self, in_refs, out_refs, scratch_refs):
        send_sems, recv_sems = scratch_refs
        x, y, c = _place()
        return _Copies([pltpu.make_async_remote_copy(
            src_ref=in_refs[w].at[2 * k + (1 - c)], dst_ref=out_refs[w].at[k], send_sem=send_sems.at[k, w],
            recv_sem=recv_sems.at[k, w], device_id=(x, y, 1 - c), device_id_type=MESH)
            for w in range(len(in_refs)) for k in range(N_CHIPS)])


def _call(body, *, name, grid, in_specs, out_specs, out_shape, args, scratch_shapes=(), ride=()):
    out_specs, out_shape, in_specs = tuple(out_specs), tuple(out_shape), list(in_specs)
    scratch_shapes = list(scratch_shapes)
    order = ("arbitrary",) * len(grid)
    if not ride:
        outs = pl.pallas_call(body, name=name, grid=grid, in_specs=in_specs, out_specs=out_specs, out_shape=out_shape,
                              scratch_shapes=scratch_shapes, compiler_params=_params(*order))(*args)
        return tuple(outs), []
    n_in, n_out, n_scr = len(in_specs), len(out_specs), len(scratch_shapes)
    n_steps = math.prod(grid)
    relay_early = n_steps >= 8
    relay_at = n_steps - n_steps // 4 if relay_early else n_steps - 1

    def split(refs, pos, counts):
        groups = []
        for k in counts:
            groups.append(refs[pos:pos + k])
            pos += k
        return groups, pos

    def wrapped(*refs):
        ins, pos = refs[:n_in], n_in
        job_in, pos = split(refs, pos, [len(j.inputs) for j in ride])
        outs, pos = refs[pos:pos + n_out], pos + n_out
        job_out, pos = split(refs, pos, [len(j.out_shapes) for j in ride])
        scr, pos = refs[pos:pos + n_scr], pos + n_scr
        job_scr, pos = split(refs, pos, [len(j.scratch) for j in ride])
        bound = [j.bind(i, o, s) for j, i, o, s in zip(ride, job_in, job_out, job_scr)]
        step = pl.program_id(0)
        for axis in range(1, len(grid)):
            step = step * grid[axis] + pl.program_id(axis)

        @pl.when(step == 0)
        def _():
            for b in bound:
                b.start()

        if relay_early:
            @pl.when(step == relay_at)
            def _():
                for b in bound:
                    b.relay()

        body(*ins, *outs, *scr)

        @pl.when(step == n_steps - 1)
        def _():
            if not relay_early:
                for b in bound:
                    b.relay()
            for b in bound:
                b.finish()

    aliases, in_pos, out_pos = {}, n_in, n_out
    for j in ride:
        aliases.update({in_pos + i: out_pos + o for i, o in j.aliases.items()})
        in_pos += len(j.inputs)
        out_pos += len(j.out_shapes)
    results = pl.pallas_call(
        wrapped, name=name, grid=grid, in_specs=in_specs + [ANY] * (in_pos - n_in),
        out_specs=out_specs + (ANY,) * (out_pos - n_out),
        out_shape=out_shape + tuple(s for j in ride for s in j.out_shapes),
        scratch_shapes=scratch_shapes + [s for j in ride for s in j.scratch], input_output_aliases=aliases,
        compiler_params=pltpu.CompilerParams(dimension_semantics=order, vmem_limit_bytes=VMEM_LIMIT,
                                             has_side_effects=True),
    )(*args, *[a for j in ride for a in j.inputs])
    job_results, pos = split(list(results), n_out, [len(j.out_shapes) for j in ride])
    return tuple(results[:n_out]), job_results


def exchange_alone(job, *, name):
    n_in, n_out = len(job.inputs), len(job.out_shapes)

    def body(*refs):
        b = job.bind(refs[:n_in], refs[n_in:n_in + n_out], refs[n_in + n_out:])
        b.start()
        b.relay()
        b.finish()

    return list(pl.pallas_call(
        body, name=name, out_shape=tuple(job.out_shapes), in_specs=[ANY] * n_in, out_specs=(ANY,) * n_out,
        scratch_shapes=job.scratch, input_output_aliases=job.aliases,
        compiler_params=pltpu.CompilerParams(has_side_effects=True),
    )(*job.inputs))


PAIR_SUM_CHUNKS = 1


def pair_sum(layer, grads, landed, parts, core, *, name):
    n = len(grads)

    def body(c_ref, *refs):
        g_refs, l_refs, o_refs = refs[:n], refs[n:2 * n], refs[3 * n:]
        for w in range(n):
            o_refs[w][...] = (g_refs[w][...].astype(F32) + l_refs[w][...].astype(F32)).astype(BF16)

    def blk(g):
        return (None, g.shape[1] // PAIR_SUM_CHUNKS, g.shape[2])

    in_specs = [pl.BlockSpec(blk(g), lambda k, i, c_ref: (2 * k + c_ref[0], i, 0)) for g in grads]
    in_specs += [pl.BlockSpec(blk(g), lambda k, i, c_ref: (k, i, 0)) for g in grads]
    in_specs += [ANY] * n
    out_specs = tuple(pl.BlockSpec((None,) + blk(g), lambda k, i, c_ref: (layer, k, i, 0)) for g in grads)
    return list(pl.pallas_call(
        body, name=name, out_shape=tuple(jax.ShapeDtypeStruct(p.shape, p.dtype) for p in parts),
        grid_spec=pltpu.PrefetchScalarGridSpec(num_scalar_prefetch=1, grid=(N_CHIPS, PAIR_SUM_CHUNKS),
                                               in_specs=in_specs, out_specs=out_specs),
        input_output_aliases={1 + 2 * n + w: w for w in range(n)},
        compiler_params=_params("parallel", "parallel"),
    )(core, *grads, *landed, *parts))


def _adamw(w, g, m, v):
    m = ADAM_B1 * m + (1.0 - ADAM_B1) * g
    v = ADAM_B2 * v + (1.0 - ADAM_B2) * (g * g)
    m_hat = m / (1.0 - ADAM_B1 ** ADAM_STEP)
    v_hat = v / (1.0 - ADAM_B2 ** ADAM_STEP)
    delta = -ADAM_LR * (m_hat / (jnp.sqrt(v_hat) + ADAM_EPS) + ADAM_WD * w)
    return delta, m, v


def reduce_adamw(part, land, chip, w, m, v, *, name):
    _, r, c = w.shape
    tr = 256 if r % 256 == 0 else (r // 2 if r > 256 else r)

    def body(k_ref, own_ref, l0_ref, l1_ref, l2_ref, w_ref, m_ref, v_ref, g_out, d_out, m_out, v_out):
        g = own_ref[...].astype(F32) + l0_ref[...].astype(F32) + l1_ref[...].astype(F32) + l2_ref[...].astype(F32)
        delta, m_new, v_new = _adamw(w_ref[...], g, m_ref[...], v_ref[...])
        g_out[...] = g
        d_out[...] = delta
        m_out[...] = m_new
        v_out[...] = v_new

    row = pl.BlockSpec((None, tr, c), lambda l, i, k_ref: (l, i, 0))

    def slot(j):
        return pl.BlockSpec((None, None, tr, c), lambda l, i, k_ref: (l, j, i, 0))

    return pl.pallas_call(
        body, name=name, out_shape=(jax.ShapeDtypeStruct(w.shape, F32),) * 4,
        grid_spec=pltpu.PrefetchScalarGridSpec(
            num_scalar_prefetch=1, grid=(DEPTH, r // tr),
            in_specs=[pl.BlockSpec((None, None, tr, c), lambda l, i, k_ref: (l, k_ref[0], i, 0)), slot(0), slot(1),
                      slot(2), row, row, row],
            out_specs=(row, row, row, row)),
        compiler_params=_params("parallel", "parallel"),
    )(chip, part, land, land, land, w, m, v)


def gather_small(block, *, name):
    def body(x_ref, out_ref, send_sems, recv_sems, local_sem):
        x, y, c = _place()
        me = 4 * x + 2 * y + c
        mine = pltpu.make_async_copy(x_ref, out_ref.at[me], local_sem)
        mine.start()
        peers = [(x ^ (k >> 2), y ^ ((k >> 1) & 1), c ^ (k & 1)) for k in range(1, N_DEV)]
        copies = [pltpu.make_async_remote_copy(
            src_ref=x_ref, dst_ref=out_ref.at[me], send_sem=send_sems.at[k], recv_sem=recv_sems.at[k],
            device_id=peer, device_id_type=MESH) for k, peer in enumerate(peers)]
        for cp in copies:
            cp.start()
        for k, (px, py, pc) in enumerate(peers):
            pltpu.make_async_remote_copy(
                src_ref=x_ref, dst_ref=out_ref.at[4 * px + 2 * py + pc], send_sem=send_sems.at[k],
                recv_sem=recv_sems.at[k], device_id=(px, py, pc), device_id_type=MESH).wait_recv()
        for cp in copies:
            cp.wait_send()
        mine.wait()

    return pl.pallas_call(
        body, name=name, out_shape=jax.ShapeDtypeStruct((N_DEV,) + block.shape, block.dtype),
        in_specs=[ANY], out_specs=ANY,
        scratch_shapes=[pltpu.SemaphoreType.DMA((7,)), pltpu.SemaphoreType.DMA((7,)), pltpu.SemaphoreType.DMA],
        compiler_params=pltpu.CompilerParams(has_side_effects=True),
    )(block)


def small_adamw(gathered, w, m, v, *, name):
    def body(g_ref, w_ref, m_ref, v_ref, g_out, d_out, m_out, v_out):
        g = g_ref[0]
        for d in range(1, N_DEV):
            g = g + g_ref[d]
        delta, m_new, v_new = _adamw(w_ref[...], g, m_ref[...], v_ref[...])
        g_out[...] = g
        d_out[...] = delta
        m_out[...] = m_new
        v_out[...] = v_new

    return pl.pallas_call(
        body, name=name, out_shape=(jax.ShapeDtypeStruct(w.shape, F32),) * 4,
    )(gathered, w, m, v)


def norm_matmul(x, g, w, *, gate_split, name, ride=()):
    s, d = x.shape
    tm = min(ROW_TILE, s)
    blocked = w.ndim == 3
    n = w.shape[0] if not blocked else w.shape[0] * w.shape[2]

    def body(x_ref, g_ref, w_ref, h_ref, *outs):
        xv = x_ref[...]
        h = ((xv * _rsqrt_ms(xv)) * g_ref[...]).astype(BF16)
        h_ref[...] = h
        if blocked:
            nb = w_ref.shape[2]
            for j in range(w_ref.shape[0]):
                outs[0][:, j * nb:(j + 1) * nb] = _dot(h, w_ref[j]).astype(BF16)
        else:
            p = _dot_nt(h, w_ref[...])
            outs[0][...] = p[:, :gate_split].astype(BF16)
            outs[1][...] = (1.0 / (1.0 + jnp.exp(-p[:, gate_split:]))).astype(BF16)

    row = lambda i: (i, 0)
    fixed = lambda i: (0, 0)
    if blocked:
        out_shape = (jax.ShapeDtypeStruct((s, d), BF16), jax.ShapeDtypeStruct((s, n), BF16))
        out_specs = (pl.BlockSpec((tm, d), row), pl.BlockSpec((tm, n), row))
        w_spec = pl.BlockSpec(w.shape, lambda i: (0, 0, 0))
    else:
        out_shape = (jax.ShapeDtypeStruct((s, d), BF16), jax.ShapeDtypeStruct((s, gate_split), BF16),
                     jax.ShapeDtypeStruct((s, n - gate_split), BF16))
        out_specs = (pl.BlockSpec((tm, d), row), pl.BlockSpec((tm, gate_split), row),
                     pl.BlockSpec((tm, n - gate_split), row))
        w_spec = pl.BlockSpec((n, d), fixed)
    return _call(body, name=name, grid=(s // tm,), out_shape=out_shape, out_specs=out_specs,
                 in_specs=[pl.BlockSpec((tm, d), row), pl.BlockSpec((1, d), fixed), w_spec], args=(x, g, w), ride=ride)


def merge_out_fwd(x, o_sb, o_sw, gates, w_bsb, w_bsw, w_o, *, name, ride=()):
    s, d = x.shape
    tm = min(ROW_TILE, s)

    def body(x_ref, osb_ref, osw_ref, g_ref, wsb_ref, wsw_ref, wo_ref, x1_ref, ysb_ref, ysw_ref, mg_ref):
        y_sb = _dot(osb_ref[...].astype(BF16), wsb_ref[...])
        y_sw = _dot(osw_ref[...].astype(BF16), wsw_ref[...])
        g = g_ref[...].astype(F32)
        merged = (g[:, :d] * y_sb + g[:, d:] * y_sw).astype(BF16)
        ysb_ref[...] = y_sb.astype(BF16)
        ysw_ref[...] = y_sw.astype(BF16)
        mg_ref[...] = merged
        x1_ref[...] = x_ref[...] + _dot(merged, wo_ref[...])

    row = lambda i: (i, 0)
    fixed = lambda i: (0, 0)
    wd = o_sb.shape[1]
    return _call(
        body, name=name, grid=(s // tm,),
        out_shape=(jax.ShapeDtypeStruct((s, d), F32),) + (jax.ShapeDtypeStruct((s, d), BF16),) * 3,
        in_specs=[pl.BlockSpec((tm, d), row), pl.BlockSpec((tm, wd), row), pl.BlockSpec((tm, wd), row),
                  pl.BlockSpec((tm, 2 * d), row), pl.BlockSpec((wd, d), fixed), pl.BlockSpec((wd, d), fixed),
                  pl.BlockSpec((d, d), fixed)],
        out_specs=(pl.BlockSpec((tm, d), row),) * 4, args=(x, o_sb, o_sw, gates, w_bsb, w_bsw, w_o), ride=ride)


def mlp_down_fwd(x1, u, w_down, *, name, ride=()):
    s, d = x1.shape
    f = u.shape[1]
    tm = min(ROW_TILE, s)

    def body(x_ref, u_ref, w_ref, o_ref):
        a = jnp.maximum(u_ref[...].astype(F32), 0.0)
        o_ref[...] = x_ref[...] + _dot((a * a).astype(BF16), w_ref[...])

    row = lambda i: (i, 0)
    return _call(
        body, name=name, grid=(s // tm,), out_shape=(jax.ShapeDtypeStruct((s, d), F32),),
        in_specs=[pl.BlockSpec((tm, d), row), pl.BlockSpec((tm, f), row), pl.BlockSpec((f, d), lambda i: (0, 0))],
        out_specs=(pl.BlockSpec((tm, d), row),), args=(x1, u, w_down), ride=ride)


def loss_head(y, target, *, name):
    s, d = y.shape
    tm = min(ROW_TILE, s)

    def body(y_ref, t_ref, dy_ref, dyb_ref, loss_ref):
        @pl.when(pl.program_id(0) == 0)
        def _():
            loss_ref[...] = jnp.zeros_like(loss_ref)

        e = y_ref[...] - t_ref[...]
        dy = e * (1.0 / d)
        dy_ref[...] = dy
        dyb_ref[...] = dy.astype(BF16)
        per_row = jnp.sum(e * e, axis=1, keepdims=True) * (0.5 / d)
        loss_ref[...] += jnp.sum(per_row, axis=0, keepdims=True)

    row = lambda i: (i, 0)
    return pl.pallas_call(
        body, name=name, grid=(s // tm,),
        out_shape=(jax.ShapeDtypeStruct((s, d), F32), jax.ShapeDtypeStruct((s, d), BF16),
                   jax.ShapeDtypeStruct((1, 1), F32)),
        in_specs=[pl.BlockSpec((tm, d), row), pl.BlockSpec((tm, d), row)],
        out_specs=(pl.BlockSpec((tm, d), row), pl.BlockSpec((tm, d), row), pl.BlockSpec((1, 1), lambda i: (0, 0))),
        compiler_params=_params("arbitrary"),
    )(y, target)


def mlp_bwd_up(dxb, u, w_down, *, name, ride=()):
    s, d = dxb.shape
    f = u.shape[1]
    tm = min(ROW_TILE, s)

    def body(dx_ref, u_ref, w_ref, du_ref):
        da = _dot_nt(dx_ref[...], w_ref[...])
        du_ref[...] = (da * (2.0 * jnp.maximum(u_ref[...].astype(F32), 0.0))).astype(BF16)

    row = lambda i: (i, 0)
    return _call(body, name=name, grid=(s // tm,), out_shape=(jax.ShapeDtypeStruct((s, f), BF16),),
                 in_specs=[pl.BlockSpec((tm, d), row), pl.BlockSpec((tm, f), row),
                           pl.BlockSpec((f, d), lambda i: (0, 0))],
                 out_specs=(pl.BlockSpec((tm, f), row),), args=(dxb, u, w_down), ride=ride)


def matmul_nt_norm_bwd(pieces, w, x, g, dres, *, name, ride=()):
    s = x.shape[0]
    d = x.shape[1]
    tm = min(ROW_TILE, s)
    blocked = w.ndim == 3
    n_pieces = len(pieces)
    widths = [p.shape[1] for p in pieces]

    def body(*refs):
        p_refs = refs[:n_pieces]
        w_ref, x_ref, g_ref, dres_ref, dx_ref, dxb_ref, dg_ref = refs[n_pieces:]

        @pl.when(pl.program_id(0) == 0)
        def _():
            dg_ref[...] = jnp.zeros_like(dg_ref)

        if blocked:
            nb = w_ref.shape[2]
            dh = _dot_nt(p_refs[0][:, :nb], w_ref[0])
            for j in range(1, w_ref.shape[0]):
                dh = dh + _dot_nt(p_refs[0][:, j * nb:(j + 1) * nb], w_ref[j])
        else:
            dh, off = None, 0
            for p_ref, width in zip(p_refs, widths):
                part = _dot(p_ref[...], w_ref[off:off + width, :])
                dh = part if dh is None else dh + part
                off += width
        xv = x_ref[...]
        r = _rsqrt_ms(xv)
        dyg = dh * g_ref[...]
        dx = dres_ref[...] + r * dyg - xv * ((r * r * r) * jnp.mean(dyg * xv, axis=-1, keepdims=True))
        dx_ref[...] = dx
        dxb_ref[...] = dx.astype(BF16)
        dg_ref[...] += jnp.sum(dh * (xv * r), axis=0, keepdims=True)

    row = lambda i: (i, 0)
    fixed = lambda i: (0, 0)
    w_spec = pl.BlockSpec(w.shape, (lambda i: (0, 0, 0)) if blocked else fixed)
    return _call(
        body, name=name, grid=(s // tm,),
        out_shape=(jax.ShapeDtypeStruct((s, d), F32), jax.ShapeDtypeStruct((s, d), BF16),
                   jax.ShapeDtypeStruct((1, d), F32)),
        in_specs=[pl.BlockSpec((tm, width), row) for width in widths] + [
            w_spec, pl.BlockSpec((tm, d), row), pl.BlockSpec((1, d), fixed), pl.BlockSpec((tm, d), row)],
        out_specs=(pl.BlockSpec((tm, d), row), pl.BlockSpec((tm, d), row), pl.BlockSpec((1, d), fixed)),
        args=(*pieces, w, x, g, dres), ride=ride)


def out_bwd(dx1b, w_o, gates, y_sb, y_sw, w_bsb, w_bsw, *, name):
    s, d = dx1b.shape
    wd = w_bsb.shape[0]
    tm = min(ROW_TILE, s)

    def body(dx_ref, wo_ref, g_ref, ysb_ref, ysw_ref, wsb_ref, wsw_ref, dysb_ref, dysw_ref, dosb_ref, dosw_ref, dgl_ref):
        dm = _dot_nt(dx_ref[...], wo_ref[...])
        g = g_ref[...].astype(F32)
        g0, g1 = g[:, :d], g[:, d:]
        dy_sb = (dm * g0).astype(BF16)
        dy_sw = (dm * g1).astype(BF16)
        dysb_ref[...] = dy_sb
        dysw_ref[...] = dy_sw
        dosb_ref[...] = _dot_nt(dy_sb, wsb_ref[...]).astype(BF16)
        dosw_ref[...] = _dot_nt(dy_sw, wsw_ref[...]).astype(BF16)
        dgl_ref[:, :d] = (dm * ysb_ref[...].astype(F32) * (g0 * (1.0 - g0))).astype(BF16)
        dgl_ref[:, d:] = (dm * ysw_ref[...].astype(F32) * (g1 * (1.0 - g1))).astype(BF16)

    row = lambda i: (i, 0)
    fixed = lambda i: (0, 0)
    return pl.pallas_call(
        body, name=name, grid=(s // tm,),
        out_shape=(jax.ShapeDtypeStruct((s, d), BF16), jax.ShapeDtypeStruct((s, d), BF16),
                   jax.ShapeDtypeStruct((s, wd), BF16), jax.ShapeDtypeStruct((s, wd), BF16),
                   jax.ShapeDtypeStruct((s, 2 * d), BF16)),
        in_specs=[pl.BlockSpec((tm, d), row), pl.BlockSpec((d, d), fixed), pl.BlockSpec((tm, 2 * d), row),
                  pl.BlockSpec((tm, d), row), pl.BlockSpec((tm, d), row), pl.BlockSpec((wd, d), fixed),
                  pl.BlockSpec((wd, d), fixed)],
        out_specs=(pl.BlockSpec((tm, d), row), pl.BlockSpec((tm, d), row), pl.BlockSpec((tm, wd), row),
                   pl.BlockSpec((tm, wd), row), pl.BlockSpec((tm, 2 * d), row)),
        compiler_params=_params("parallel"),
    )(dx1b, w_o, gates, y_sb, y_sw, w_bsb, w_bsw)


def matmul_tn(a, pieces, *, a_block, out_cols, relu2, name):
    s, m = a.shape
    widths = [p.shape[1] for p in pieces]
    n = sum(widths)
    n_pieces = len(pieces)
    ts = min(512 if n >= 4096 else 2048, s)
    n_steps = s // ts
    if out_cols is None:
        out_shape = jax.ShapeDtypeStruct((m // a_block, a_block, n), BF16)
        out_spec = pl.BlockSpec((None, a_block, n), lambda i, k: (i, 0, 0))
    else:
        out_shape = jax.ShapeDtypeStruct((n // out_cols, m, out_cols), BF16)
        out_spec = pl.BlockSpec((n // out_cols, a_block, out_cols), lambda i, k: (0, i, 0))

    def body(a_ref, *refs):
        b_refs, o_ref, acc = refs[:n_pieces], refs[n_pieces], refs[n_pieces + 1]
        k = pl.program_id(1)

        @pl.when(k == 0)
        def _():
            acc[...] = jnp.zeros_like(acc)

        av = a_ref[...]
        if relu2:
            af = jnp.maximum(av.astype(F32), 0.0)
            av = af * af
        av = av.astype(BF16)
        off = 0
        for b_ref in b_refs:
            width = b_ref.shape[1]
            acc[:, off:off + width] += _dot_tn(av, b_ref[...].astype(BF16))
            off += width

        @pl.when(k == n_steps - 1)
        def _():
            if out_cols is None:
                o_ref[...] = acc[...].astype(BF16)
            else:
                for j in range(n // out_cols):
                    o_ref[j] = acc[:, j * out_cols:(j + 1) * out_cols].astype(BF16)

    return pl.pallas_call(
        body, name=name, grid=(m // a_block, n_steps), out_shape=out_shape,
        in_specs=[pl.BlockSpec((ts, a_block), lambda i, k: (k, i))] + [
            pl.BlockSpec((ts, width), lambda i, k: (k, 0)) for width in widths],
        out_specs=out_spec, scratch_shapes=[pltpu.VMEM((a_block, n), F32)],
        compiler_params=_params("parallel", "arbitrary"),
    )(a, *pieces)


def matmul_tn_row_blocks(pieces, b, *, n_blocks, name):
    s, n = b.shape
    widths = [p.shape[1] for p in pieces]
    m = sum(widths)
    rows = m // n_blocks
    n_pieces = len(pieces)
    ts = min(512, s)
    n_steps = s // ts
    half = n_blocks // 2

    def body(*refs):
        p_refs, b_ref, o_ref, a_tile, acc = refs[:n_pieces], refs[n_pieces], refs[n_pieces + 1], refs[-2], refs[-1]
        i, k = pl.program_id(0), pl.program_id(1)

        @pl.when(k == 0)
        def _():
            acc[...] = jnp.zeros_like(acc)

        off = 0
        for p_ref, width in zip(p_refs, widths):
            a_tile[:, off:off + width] = p_ref[...]
            off += width
        bv = b_ref[...]
        for side in range(2):
            @pl.when(i == side)
            def _():
                for j in range(half):
                    col = (side * half + j) * rows
                    acc[j] += _dot_tn(a_tile[:, col:col + rows], bv)

        @pl.when(k == n_steps - 1)
        def _():
            o_ref[...] = acc[...].astype(BF16)

    return pl.pallas_call(
        body, name=name, grid=(2, n_steps), out_shape=jax.ShapeDtypeStruct((n_blocks, rows, n), BF16),
        in_specs=[pl.BlockSpec((ts, width), lambda i, k: (k, 0)) for width in widths] + [
            pl.BlockSpec((ts, n), lambda i, k: (k, 0))],
        out_specs=pl.BlockSpec((half, rows, n), lambda i, k: (i, 0, 0)),
        scratch_shapes=[pltpu.VMEM((ts, m), BF16), pltpu.VMEM((half, rows, n), F32)],
        compiler_params=_params("parallel", "arbitrary"),
    )(*pieces, b)


def _softplus(z):
    return jnp.maximum(z, 0.0) + jnp.log(1.0 + jnp.exp(-jnp.abs(z)))


def _suffix_sums(x, tri2):
    groups = x.shape[1] // LANES
    outs, run = [None] * groups, None
    for g in reversed(range(groups)):
        xg = x[:, g * LANES:(g + 1) * LANES]
        hi, lo = _split_bf16(xg)
        inner = _dot(jnp.concatenate([hi, lo], axis=1), tri2)
        outs[g] = inner if run is None else inner + run
        total = jnp.sum(xg, axis=1, keepdims=True)
        run = total if run is None else run + total
    return jnp.concatenate(outs, axis=1), run


def _head_mask(h):
    return (lax.broadcasted_iota(jnp.int32, (1, LANES), 1) // HEAD_DIM) == h


def _stack_heads(x):
    zero = jnp.zeros_like(x)
    return jnp.concatenate([jnp.where(_head_mask(0), x, zero), jnp.where(_head_mask(1), x, zero)], axis=0)


def _unstack_heads(r, t):
    return jnp.where(_head_mask(0), r[:t], r[t:])


def _sb_positions(q0, tk):
    row = lax.broadcasted_iota(jnp.int32, (2 * SB_TQ, tk), 0)
    col = lax.broadcasted_iota(jnp.int32, (2 * SB_TQ, tk), 1)
    return q0 + jnp.where(row >= SB_TQ, row - SB_TQ, row), col


def _sb_first_key(q0):
    return pl.multiple_of(jnp.maximum(q0 + SB_TQ - SB_TK1, 0), SB_TQ)


def _sb_next_key(k_prev):
    return pl.multiple_of(jnp.maximum(k_prev - SB_TK, 0), SB_TQ)


def _sb_rows(q0):
    return pl.ds(pl.multiple_of(2 * q0, 2 * SB_TQ), 2 * SB_TQ)


def sb_attn_fwd(proj, tri2, *, name, ride=()):
    s = proj.shape[0]
    nq = s // SB_TQ
    n_pairs = SB_WIDTH // LANES

    def body(q_ref, k_ref, v_ref, tri_ref, o_ref, c_all):
        def block(qh, k0, tk, live, c):
            z = _dot_nt(qh, k_ref[pl.ds(k0, tk), :])
            sp = _softplus(z)
            tail, total = _suffix_sums(jnp.where(live, -sp, 0.0), tri_ref[...])
            w = jnp.where(live, jnp.exp(z - sp + tail + c), 0.0)
            return _dot(w.astype(BF16), v_ref[pl.ds(k0, tk), :]), c + total

        def load_q(q0):
            return _stack_heads(q_ref[pl.ds(q0, SB_TQ), :]) * SCALE

        def first(qb, carry):
            q0 = pl.multiple_of(qb * SB_TQ, SB_TQ)
            tpos, col = _sb_positions(q0, SB_TK1)
            k0 = _sb_first_key(q0)
            acc, c = block(load_q(q0), k0, SB_TK1, k0 + col < tpos, jnp.zeros((2 * SB_TQ, 1), F32))
            o_ref[pl.ds(q0, SB_TQ), :] = _unstack_heads(acc, SB_TQ)
            c_all[_sb_rows(q0), :] = jnp.broadcast_to(jnp.where(k0 > 0, c, NEG), (2 * SB_TQ, LANES))
            return carry

        lax.fori_loop(0, nq, first, 0, unroll=4)

        @pl.when(jnp.max(c_all[...]) > SB_CUTOFF)
        def _():
            def more(qb, carry):
                q0 = pl.multiple_of(qb * SB_TQ, SB_TQ)
                c0 = c_all[_sb_rows(q0), 0:1]

                @pl.when(jnp.max(c0) > SB_CUTOFF)
                def _():
                    qh = load_q(q0)
                    _, col = _sb_positions(q0, SB_TK)

                    def cond(st):
                        return jnp.logical_and(st[0] > 0, st[3] > SB_CUTOFF)

                    def step(st):
                        k_prev, c, acc, _ = st
                        k0 = _sb_next_key(k_prev)
                        part, c = block(qh, k0, SB_TK, k0 + col < k_prev, c)
                        return k0, c, acc + part, jnp.max(c)

                    st = lax.while_loop(cond, step, (_sb_first_key(q0), c0, jnp.zeros((2 * SB_TQ, LANES), F32),
                                                     jnp.max(c0)))
                    o_ref[pl.ds(q0, SB_TQ), :] += _unstack_heads(st[2], SB_TQ)

                return carry

            lax.fori_loop(0, nq, more, 0)

    def col_spec(j):
        return pl.BlockSpec((s, LANES), lambda p: (0, j * n_pairs + p))

    (o,), rides = _call(
        body, name=name, grid=(n_pairs,), out_shape=(jax.ShapeDtypeStruct((s, SB_WIDTH), F32),),
        in_specs=[col_spec(0), col_spec(1), col_spec(2), pl.BlockSpec((2 * LANES, LANES), lambda p: (0, 0))],
        out_specs=(pl.BlockSpec((s, LANES), lambda p: (0, p)),), scratch_shapes=[pltpu.VMEM((2 * s, LANES), F32)],
        args=(proj, proj, proj, tri2), ride=ride)
    return o, rides


def sb_attn_bwd(proj, tri2, o, do, *, name, ride=()):
    s = proj.shape[0]
    nq = s // SB_TQ
    n_pairs = SB_WIDTH // LANES

    def body(q_ref, k_ref, v_ref, tri_ref, o_ref, do_ref, dq_ref, dk_ref, dv_ref, dq_acc, dk_acc, dv_acc, c_all, e_all):
        dk_acc[...] = jnp.zeros_like(dk_acc)
        dv_acc[...] = jnp.zeros_like(dv_acc)

        def load(q0):
            qh = _stack_heads(q_ref[pl.ds(q0, SB_TQ), :]) * SCALE
            doh_b = _stack_heads(do_ref[pl.ds(q0, SB_TQ), :])
            ov = o_ref[pl.ds(q0, SB_TQ), :]
            dd = jnp.sum(doh_b.astype(F32) * jnp.concatenate([ov, ov], axis=0), axis=1, keepdims=True)
            return qh, doh_b, dd

        def block(qh, doh_b, dd, k0, tk, live, c, ce):
            kt = k_ref[pl.ds(k0, tk), :]
            z = _dot_nt(qh, kt)
            sp = _softplus(z)
            lb = z - sp
            tail, total = _suffix_sums(jnp.where(live, -sp, 0.0), tri_ref[...])
            wb = jnp.where(live, jnp.exp(lb + tail + c), 0.0).astype(BF16)
            e = wb.astype(F32) * _dot_nt(doh_b, v_ref[pl.ds(k0, tk), :])
            e_tail, e_total = _suffix_sums(e, tri_ref[...])
            dz = jnp.where(live, e - jnp.exp(lb) * (dd - ce - e_tail), 0.0)
            dzb = dz.astype(BF16)
            dk_acc[pl.ds(k0, tk), :] += _dot_tn(dzb, qh)
            dv_acc[pl.ds(k0, tk), :] += _dot_tn(wb, doh_b)
            return _dot(dzb, kt), c + total, ce + e_total

        def first(qb, carry):
            q0 = pl.multiple_of(qb * SB_TQ, SB_TQ)
            qh, doh_b, dd = load(q0)
            tpos, col = _sb_positions(q0, SB_TK1)
            k0 = _sb_first_key(q0)
            zero = jnp.zeros((2 * SB_TQ, 1), F32)
            dq, c, ce = block(qh, doh_b, dd, k0, SB_TK1, k0 + col < tpos, zero, zero)
            dq_acc[pl.ds(q0, SB_TQ), :] = _unstack_heads(dq, SB_TQ)
            c_all[_sb_rows(q0), :] = jnp.broadcast_to(jnp.where(k0 > 0, c, NEG), (2 * SB_TQ, LANES))
            e_all[_sb_rows(q0), :] = jnp.broadcast_to(ce, (2 * SB_TQ, LANES))
            return carry

        lax.fori_loop(0, nq, first, 0, unroll=4)

        @pl.when(jnp.max(c_all[...]) > SB_CUTOFF)
        def _():
            def more(qb, carry):
                q0 = pl.multiple_of(qb * SB_TQ, SB_TQ)
                c0 = c_all[_sb_rows(q0), 0:1]

                @pl.when(jnp.max(c0) > SB_CUTOFF)
                def _():
                    qh, doh_b, dd = load(q0)
                    _, col = _sb_positions(q0, SB_TK)

                    def cond(st):
                        return jnp.logical_and(st[0] > 0, st[4] > SB_CUTOFF)

                    def step(st):
                        k_prev, c, ce, dq, _ = st
                        k0 = _sb_next_key(k_prev)
                        part, c, ce = block(qh, doh_b, dd, k0, SB_TK, k0 + col < k_prev, c, ce)
                        return k0, c, ce, dq + part, jnp.max(c)

                    st = lax.while_loop(cond, step, (_sb_first_key(q0), c0, e_all[_sb_rows(q0), 0:1],
                                                     jnp.zeros((2 * SB_TQ, LANES), F32), jnp.max(c0)))
                    dq_acc[pl.ds(q0, SB_TQ), :] += _unstack_heads(st[3], SB_TQ)

                return carry

            lax.fori_loop(0, nq, more, 0)

        dq_ref[...] = (dq_acc[...] * SCALE).astype(BF16)
        dk_ref[...] = dk_acc[...].astype(BF16)
        dv_ref[...] = dv_acc[...].astype(BF16)

    def col_spec(j):
        return pl.BlockSpec((s, LANES), lambda p: (0, j * n_pairs + p))

    pair = pl.BlockSpec((s, LANES), lambda p: (0, p))
    (dq, dk, dv), rides = _call(
        body, name=name, grid=(n_pairs,), out_shape=(jax.ShapeDtypeStruct((s, SB_WIDTH), BF16),) * 3,
        in_specs=[col_spec(0), col_spec(1), col_spec(2), pl.BlockSpec((2 * LANES, LANES), lambda p: (0, 0)), pair, pair],
        out_specs=(pair, pair, pair),
        scratch_shapes=[pltpu.VMEM((s, LANES), F32)] * 3 + [pltpu.VMEM((2 * s, LANES), F32)] * 2,
        args=(proj, proj, proj, tri2, o, do), ride=ride)
    return dq, dk, dv, rides


def _lane_lo():
    return lax.broadcasted_iota(jnp.int32, (1, LANES), 1) < HEAD_DIM


def _swap_halves(x):
    return pltpu.roll(x, HEAD_DIM, 1)


def _rot_half(y):
    first = (lax.broadcasted_iota(jnp.int32, (1, LANES), 1) % HEAD_DIM) < (HEAD_DIM // 2)
    return jnp.where(first, pltpu.roll(y, LANES - HEAD_DIM // 2, 1), pltpu.roll(y, HEAD_DIM // 2, 1))


def _head_mean(v, avg):
    hi, lo = _split_bf16(v)
    return _dot(hi, avg) + _dot(lo, avg)


def _head_avg_matrix():
    lane = jnp.arange(LANES) // HEAD_DIM
    return ((lane[:, None] == lane[None, :]).astype(F32) * (1.0 / HEAD_DIM)).astype(BF16)


def swa_prep_fwd(proj, cos_p, sin_p, gq, gk, *, name):
    s = proj.shape[0]
    tm = min(512, s)
    q_blk = (3 * SB_WIDTH) // SWA_Q_WIDTH
    k_blk = (3 * SB_WIDTH + SWA_Q_WIDTH) // LANES

    def body(q_ref, k_ref, cos_ref, sin_ref, gq_ref, gk_ref, avg_ref, qn_ref, kn_ref):
        cosv, sinv, avg = cos_ref[...], sin_ref[...], avg_ref[...]

        def norm_rope(xv, g):
            y = (xv * lax.rsqrt(_head_mean(xv * xv, avg) + NORM_EPS)) * g
            return y * cosv + _rot_half(y) * sinv

        for j in range(SWA_Q_WIDTH // LANES):
            sl = slice(j * LANES, (j + 1) * LANES)
            qn_ref[:, sl] = norm_rope(q_ref[:, sl].astype(F32), gq_ref[...]).astype(BF16)
        kn_ref[...] = norm_rope(k_ref[...].astype(F32), gk_ref[...]).astype(BF16)

    row = lambda i: (i, 0)
    fixed = lambda i: (0, 0)
    return pl.pallas_call(
        body, name=name, grid=(s // tm,),
        out_shape=(jax.ShapeDtypeStruct((s, SWA_Q_WIDTH), BF16), jax.ShapeDtypeStruct((s, LANES), BF16)),
        in_specs=[pl.BlockSpec((tm, SWA_Q_WIDTH), lambda i: (i, q_blk)), pl.BlockSpec((tm, LANES), lambda i: (i, k_blk)),
                  pl.BlockSpec((tm, LANES), row), pl.BlockSpec((tm, LANES), row),
                  pl.BlockSpec((1, LANES), fixed), pl.BlockSpec((1, LANES), fixed), pl.BlockSpec((LANES, LANES), fixed)],
        out_specs=(pl.BlockSpec((tm, SWA_Q_WIDTH), row), pl.BlockSpec((tm, LANES), row)),
        compiler_params=_params("parallel"),
    )(proj, proj, cos_p, sin_p, gq, gk, _head_avg_matrix())


def swa_prep_bwd(proj, cos_p, sin_p, gq, gk, dqn, dkn, dv, *, name):
    s = proj.shape[0]
    tm = min(512, s)
    q_blk = (3 * SB_WIDTH) // SWA_Q_WIDTH
    k_blk = (3 * SB_WIDTH + SWA_Q_WIDTH) // LANES

    def body(q_ref, k_ref, cos_ref, sin_ref, gq_ref, gk_ref, avg_ref, dqn_ref, dkn_ref, dv_ref, dq_ref, dk_ref, dvb_ref,
             dgq_ref, dgk_ref):
        @pl.when(pl.program_id(0) == 0)
        def _():
            dgq_ref[...] = jnp.zeros_like(dgq_ref)
            dgk_ref[...] = jnp.zeros_like(dgk_ref)

        cosv, sinv, avg = cos_ref[...], sin_ref[...], avg_ref[...]

        def bwd(xv, g, dout):
            dy = dout * cosv + _rot_half(dout * sinv)
            r = lax.rsqrt(_head_mean(xv * xv, avg) + NORM_EPS)
            dyg = dy * g
            dx = r * dyg - xv * ((r * r * r) * _head_mean(dyg * xv, avg))
            return dx, jnp.sum(dy * (xv * r), axis=0, keepdims=True)

        for j in range(SWA_Q_WIDTH // LANES):
            sl = slice(j * LANES, (j + 1) * LANES)
            dx, dg = bwd(q_ref[:, sl].astype(F32), gq_ref[...], dqn_ref[:, sl])
            dq_ref[:, sl] = dx.astype(BF16)
            dgq_ref[:, sl] += dg
        dx, dg = bwd(k_ref[...].astype(F32), gk_ref[...], dkn_ref[...])
        dk_ref[...] = dx.astype(BF16)
        dgk_ref[...] += dg
        dvb_ref[...] = dv_ref[...].astype(BF16)

    row = lambda i: (i, 0)
    fixed = lambda i: (0, 0)
    lane_row = pl.BlockSpec((tm, LANES), row)
    return pl.pallas_call(
        body, name=name, grid=(s // tm,),
        out_shape=(jax.ShapeDtypeStruct((s, SWA_Q_WIDTH), BF16), jax.ShapeDtypeStruct((s, LANES), BF16),
                   jax.ShapeDtypeStruct((s, LANES), BF16),
                   jax.ShapeDtypeStruct((1, SWA_Q_WIDTH), F32), jax.ShapeDtypeStruct((1, LANES), F32)),
        in_specs=[pl.BlockSpec((tm, SWA_Q_WIDTH), lambda i: (i, q_blk)), pl.BlockSpec((tm, LANES), lambda i: (i, k_blk)),
                  lane_row, lane_row, pl.BlockSpec((1, LANES), fixed), pl.BlockSpec((1, LANES), fixed),
                  pl.BlockSpec((LANES, LANES), fixed), pl.BlockSpec((tm, SWA_Q_WIDTH), row), lane_row, lane_row],
        out_specs=(pl.BlockSpec((tm, SWA_Q_WIDTH), row), lane_row, lane_row,
                   pl.BlockSpec((1, SWA_Q_WIDTH), fixed), pl.BlockSpec((1, LANES), fixed)),
        compiler_params=_params("arbitrary"),
    )(proj, proj, cos_p, sin_p, gq, gk, _head_avg_matrix(), dqn, dkn, dv)


def _swa_kv_copies(k_ref, v_ref, kg_ref, vg_ref, second_kv):
    s = k_ref.shape[0]
    rows = min(512, s)
    keep = jnp.logical_xor(_lane_lo(), second_kv)

    def chunk(r, carry):
        sl = pl.ds(pl.multiple_of(r * rows, rows), rows)
        for src, dst in ((k_ref, kg_ref), (v_ref, vg_ref)):
            f = src[sl, :].astype(F32)
            dst[sl, :] = jnp.where(keep, f, _swap_halves(f)).astype(BF16)
        return carry

    lax.fori_loop(0, s // rows, chunk, 0)


def _swa_tile(i, kg_ref, vg_ref):
    q0 = pl.multiple_of(i * SWA_TQ, SWA_TQ)
    k0 = pl.multiple_of(jnp.maximum(i - 1, 0) * SWA_TQ, SWA_TQ)
    kg = kg_ref[pl.ds(k0, SWA_TK), :]
    vg = vg_ref[pl.ds(k0, SWA_TK), :]
    row = lax.broadcasted_iota(jnp.int32, (2 * SWA_TQ, SWA_TK), 0)
    tpos = q0 + jnp.where(row >= SWA_TQ, row - SWA_TQ, row)
    spos = k0 + lax.broadcasted_iota(jnp.int32, (2 * SWA_TQ, SWA_TK), 1)
    valid = jnp.logical_and(spos <= tpos, spos > tpos - WINDOW)
    return q0, k0, kg, vg, valid


def _swa_probs(qh, kg, valid, sink):
    z = jnp.where(valid, _dot_nt(qh, kg) * SCALE, NEG)
    m = jnp.maximum(jnp.max(z, axis=1, keepdims=True), sink)
    pexp = jnp.exp(z - m)
    psink = jnp.exp(sink - m)
    inv = 1.0 / (jnp.sum(pexp, axis=1, keepdims=True) + psink)
    return pexp * inv, psink * inv


def _stacked_sink(sink_row):
    s0 = jnp.sum(jnp.where(_head_mask(0), sink_row, 0.0), axis=1, keepdims=True) * (1.0 / HEAD_DIM)
    s1 = jnp.sum(jnp.where(_head_mask(1), sink_row, 0.0), axis=1, keepdims=True) * (1.0 / HEAD_DIM)
    top = lax.broadcasted_iota(jnp.int32, (2 * SWA_TQ, 1), 0) < SWA_TQ
    return jnp.where(top, s0, s1)


def swa_attn_fwd(qn, kn, proj, sink_p, *, name, ride=()):
    s = qn.shape[0]
    nq = s // SWA_TQ
    n_pairs = SWA_Q_WIDTH // LANES
    v_blk = (3 * SB_WIDTH + SWA_Q_WIDTH + SWA_KV_WIDTH) // LANES

    def body(q_ref, k_ref, v_ref, s_ref, o_ref, kg_ref, vg_ref):
        _swa_kv_copies(k_ref, v_ref, kg_ref, vg_ref, (pl.program_id(0) // 2) == 1)
        sink = _stacked_sink(s_ref[...])

        def tile(i, carry):
            q0, _, kg, vg, valid = _swa_tile(i, kg_ref, vg_ref)
            probs, _ = _swa_probs(_stack_heads(q_ref[pl.ds(q0, SWA_TQ), :]), kg, valid, sink)
            o_ref[pl.ds(q0, SWA_TQ), :] = _unstack_heads(_dot(probs.astype(BF16), vg), SWA_TQ)
            return carry

        lax.fori_loop(0, nq, tile, 0, unroll=4)

    pair = pl.BlockSpec((s, LANES), lambda p: (0, p))
    whole = pl.BlockSpec((s, LANES), lambda p: (0, 0))
    (o,), rides = _call(
        body, name=name, grid=(n_pairs,), out_shape=(jax.ShapeDtypeStruct((s, SWA_Q_WIDTH), F32),),
        in_specs=[pair, whole, pl.BlockSpec((s, LANES), lambda p: (0, v_blk)),
                  pl.BlockSpec((None, 1, LANES), lambda p: (p, 0, 0))],
        out_specs=(pair,), scratch_shapes=[pltpu.VMEM((s, LANES), BF16)] * 2, args=(qn, kn, proj, sink_p), ride=ride)
    return o, rides


def swa_attn_bwd(qn, kn, proj, sink_p, o, do, *, name, ride=()):
    s = qn.shape[0]
    nq = s // SWA_TQ
    n_pairs = SWA_Q_WIDTH // LANES
    v_blk = (3 * SB_WIDTH + SWA_Q_WIDTH + SWA_KV_WIDTH) // LANES
    fold_rows = min(512, s)

    def body(q_ref, k_ref, v_ref, s_ref, o_ref, do_ref, dq_ref, dk_ref, dv_ref, ds_ref, acc_k, acc_v, kg_ref, vg_ref):
        p = pl.program_id(0)
        _swa_kv_copies(k_ref, v_ref, kg_ref, vg_ref, (p // 2) == 1)
        sink = _stacked_sink(s_ref[...])

        @pl.when(p % 2 == 0)
        def _():
            acc_k[...] = jnp.zeros_like(acc_k)
            acc_v[...] = jnp.zeros_like(acc_v)

        ds_ref[...] = jnp.zeros_like(ds_ref)

        def tile(i, carry):
            q0, k0, kg, vg, valid = _swa_tile(i, kg_ref, vg_ref)
            qh = _stack_heads(q_ref[pl.ds(q0, SWA_TQ), :])
            doh_b = _stack_heads(do_ref[pl.ds(q0, SWA_TQ), :])
            ov = o_ref[pl.ds(q0, SWA_TQ), :]
            delta = jnp.sum(doh_b.astype(F32) * jnp.concatenate([ov, ov], axis=0), axis=1, keepdims=True)
            probs, psink = _swa_probs(qh, kg, valid, sink)
            dz = probs * (_dot_nt(doh_b, vg) - delta)
            dzb = (dz * SCALE).astype(BF16)
            dq_ref[pl.ds(q0, SWA_TQ), :] = _unstack_heads(_dot(dzb, kg), SWA_TQ)
            acc_k[pl.ds(k0, SWA_TK), :] += _dot_tn(dzb, qh)
            acc_v[pl.ds(k0, SWA_TK), :] += _dot_tn(probs.astype(BF16), doh_b)
            pd = psink * delta
            ds_ref[...] -= jnp.where(_head_mask(0), jnp.sum(pd[:SWA_TQ], axis=0, keepdims=True),
                                     jnp.sum(pd[SWA_TQ:], axis=0, keepdims=True))
            return carry

        lax.fori_loop(0, nq, tile, 0, unroll=4)

        def fold_into(first_head):
            def fold(r, carry):
                rows = pl.ds(pl.multiple_of(r * fold_rows, fold_rows), fold_rows)
                for acc, out in ((acc_k, dk_ref), (acc_v, dv_ref)):
                    a = acc[rows, :]
                    both = a + _swap_halves(a)
                    if first_head:
                        out[rows, :] = jnp.where(_lane_lo(), both, 0.0)
                    else:
                        out[rows, :] = jnp.where(_lane_lo(), out[rows, :], both)
                return carry

            lax.fori_loop(0, s // fold_rows, fold, 0)

        @pl.when(p == 1)
        def _():
            fold_into(True)

        @pl.when(p == 3)
        def _():
            fold_into(False)

    pair = pl.BlockSpec((s, LANES), lambda p: (0, p))
    whole = pl.BlockSpec((s, LANES), lambda p: (0, 0))
    sink_spec = pl.BlockSpec((None, 1, LANES), lambda p: (p, 0, 0))
    (dq, dk, dv, dsink), rides = _call(
        body, name=name, grid=(n_pairs,),
        out_shape=(jax.ShapeDtypeStruct((s, SWA_Q_WIDTH), F32), jax.ShapeDtypeStruct((s, LANES), F32),
                   jax.ShapeDtypeStruct((s, LANES), F32), jax.ShapeDtypeStruct((n_pairs, 1, LANES), F32)),
        in_specs=[pair, whole, pl.BlockSpec((s, LANES), lambda p: (0, v_blk)), sink_spec, pair, pair],
        out_specs=(pair, whole, whole, sink_spec),
        scratch_shapes=[pltpu.VMEM((s, LANES), F32)] * 2 + [pltpu.VMEM((s, LANES), BF16)] * 2,
        args=(qn, kn, proj, sink_p, o, do), ride=ride)
    return dq, dk, dv, dsink, rides


def _rope_tables(s):
    inv_freq = 1.0 / (ROPE_THETA ** (jnp.arange(0, HEAD_DIM, 2, dtype=F32) / HEAD_DIM))
    ang = jnp.arange(s, dtype=F32)[:, None] * inv_freq[None, :]
    cos, sin = jnp.cos(ang), jnp.sin(ang)
    cos_p = jnp.tile(jnp.concatenate([cos, cos], axis=1), (1, LANES // HEAD_DIM))
    sin_p = jnp.tile(jnp.concatenate([-sin, sin], axis=1), (1, LANES // HEAD_DIM))
    return cos_p, sin_p


def _lane_tile(v, reps):
    return jnp.tile(v.reshape(1, -1), (1, reps))


def _natural(stack, w):
    n, r, c = stack.shape
    if MATRIX_NAMES[w] in ROW_SHARDED or w == W_IN:
        return stack.reshape(n * r, c)
    if w == W_UP:
        return stack
    return jnp.transpose(stack, (1, 0, 2)).reshape(r, n * c)


def _pack_small(tree):
    flat = jnp.concatenate([tree[n].reshape(-1) for n in SMALL_NAMES])
    rows = -(-flat.shape[0] // (8 * LANES)) * 8
    return jnp.pad(flat, (0, rows * LANES - flat.shape[0])).reshape(rows, LANES)


def _unpack_small(packed, shapes):
    flat, out, off = packed.reshape(-1), {}, 0
    for n in SMALL_NAMES:
        size = shapes[n][0] * shapes[n][1]
        out[n] = flat[off:off + size].reshape(shapes[n])
        off += size
    return out


def train_step(x, target, weights, mom_m, mom_v):
    s = x.shape[0]
    cos_p, sin_p = _rope_tables(s)
    tri = (jnp.arange(LANES)[:, None] > jnp.arange(LANES)[None, :]).astype(BF16)
    tri = jnp.concatenate([tri, tri], axis=0)
    local = {n: (jnp.swapaxes(t, 1, 2) if n == "w_in" else t) for n, t in weights.items()}
    local_m = {n: (jnp.swapaxes(t, 1, 2) if n == "w_in" else t) for n, t in mom_m.items()}
    local_v = {n: (jnp.swapaxes(t, 1, 2) if n == "w_in" else t) for n, t in mom_v.items()}
    shards = [[local[n][l].astype(BF16) for n in MATRIX_NAMES] for l in range(DEPTH)]
    core = lax.axis_index("c").astype(jnp.int32).reshape(1)
    chip = (2 * lax.axis_index("x") + lax.axis_index("y")).astype(jnp.int32).reshape(1)

    def gather(l, ws, rows=None, stacks=None):
        return GatherJob([shards[l][w] for w in ws], rows=rows, stacks=stacks)

    def halves(w):
        r = shards[0][w].shape[0] // 2
        return (0, r), (r, r)

    w_in = _natural(exchange_alone(gather(0, [W_IN]), name="gather_w_in0")[0], W_IN)
    saved = []
    for l in range(DEPTH):
        g_mix = weights["mix_norm_g"][l].reshape(1, D_MODEL)
        g_mlp = weights["mlp_norm_g"][l].reshape(1, D_MODEL)
        gq = _lane_tile(weights["q_norm_g"][l], LANES // HEAD_DIM)
        gk = _lane_tile(weights["k_norm_g"][l], LANES // HEAD_DIM)
        sink_p = jnp.repeat(weights["sinks"][l].reshape(SWA_Q_WIDTH // LANES, 2), HEAD_DIM, axis=1)
        sink_p = sink_p.reshape(SWA_Q_WIDTH // LANES, 1, LANES)
        (h, proj, gates), ((s_bsb, s_bsw, s_out),) = norm_matmul(
            x, g_mix, w_in, gate_split=ATTN_WIDTH, name="in_proj", ride=[gather(l, [W_BSB, W_BSW, W_OUT])])
        o_sb, ((s_up,),) = sb_attn_fwd(proj, tri, name="sb_fwd", ride=[gather(l, [W_UP])])
        qn, kn = swa_prep_fwd(proj, cos_p, sin_p, gq, gk, name="swa_prep")
        o_sw, ((s_down,),) = swa_attn_fwd(qn, kn, proj, sink_p, name="swa_fwd",
                                          ride=[gather(l, [W_DOWN], rows=halves(W_DOWN)[0])])
        more = l + 1 < DEPTH
        (x1, y_sb, y_sw, merged), rides = merge_out_fwd(
            x, o_sb, o_sw, gates, _natural(s_bsb, W_BSB), _natural(s_bsw, W_BSW), _natural(s_out, W_OUT),
            name="merge_out" if more else "merge_out_last",
            ride=[gather(l + 1, [W_IN], rows=halves(W_IN)[0])] if more else [])
        (h2, u), ((s_down,),) = norm_matmul(x1, g_mlp, s_up, gate_split=None, name="mlp_up",
                                            ride=[gather(l, [W_DOWN], rows=halves(W_DOWN)[1], stacks=[s_down])])
        mats = [w_in, _natural(s_bsb, W_BSB), _natural(s_bsw, W_BSW), _natural(s_out, W_OUT), s_up,
                _natural(s_down, W_DOWN)]
        if more:
            (x2,), ((s_in,),) = mlp_down_fwd(x1, u, mats[W_DOWN], name="mlp_down",
                                             ride=[gather(l + 1, [W_IN], rows=halves(W_IN)[1], stacks=rides[0])])
            w_in = _natural(s_in, W_IN)
        else:
            (x2,), _ = mlp_down_fwd(x1, u, mats[W_DOWN], name="mlp_down_last")
        saved.append(dict(x=x, h=h, proj=proj, gates=gates, o_sb=o_sb, qn=qn, kn=kn, o_sw=o_sw, y_sb=y_sb, y_sw=y_sw,
                          merged=merged, x1=x1, h2=h2, u=u, g_mix=g_mix, g_mlp=g_mlp, gq=gq, gk=gk, sink_p=sink_p,
                          mats=mats))
        x = x2

    dx, dxb, loss = loss_head(x, target, name="loss_head")

    shard_shapes = [local[n].shape[1:] for n in MATRIX_NAMES]
    parts = [lax.empty((DEPTH, N_CHIPS) + sh, BF16) for sh in shard_shapes]
    lands = [lax.empty((DEPTH, 3) + sh, BF16) for sh in shard_shapes]
    small_grads = {n: [None] * DEPTH for n in SMALL_NAMES}
    half = D_MODEL // 2

    def summed(l, ws, grads, landed):
        new = pair_sum(l, grads, landed, [parts[w] for w in ws], core, name="grad_pair_sum")
        for w, p in zip(ws, new):
            parts[w] = p

    def chip_job(items):
        return ChipJob(items, parts, lands)

    def landed_chip(job, outs):
        for w, a in zip(job.ws, outs):
            lands[w] = a

    in_pending = None
    for l in reversed(range(DEPTH)):
        a = saved[l]
        mats = a["mats"]
        in_jobs = [chip_job([(in_pending, W_IN, rows)]) for rows in halves(W_IN)] if in_pending is not None else []
        (du,), rides = mlp_bwd_up(dxb, a["u"], mats[W_DOWN], name="mlp_bwd_up" if in_jobs else "mlp_bwd_up_first",
                                  ride=in_jobs[:1])
        if in_jobs:
            landed_chip(in_jobs[0], rides[0])
            in_jobs[1] = chip_job([(in_pending, W_IN, halves(W_IN)[1])])
        dw_down = matmul_tn(a["u"], [dxb], a_block=half, out_cols=None, relu2=True, name="dw_down")
        dw_up = matmul_tn(a["h2"], [du], a_block=half, out_cols=du.shape[1] // N_DEV, relu2=False, name="dw_up")
        g_mlp_w = [dw_up, dw_down.reshape((N_DEV,) + shard_shapes[W_DOWN])]
        (dx1, dx1b, dg_mlp), rides = matmul_nt_norm_bwd(
            [du], mats[W_UP], a["x1"], a["g_mlp"], dx, name="mlp_bwd_norm" if in_jobs else "mlp_bwd_norm_first",
            ride=[PairJob(g_mlp_w)] + in_jobs[1:])
        if in_jobs:
            landed_chip(in_jobs[1], rides[1])
        summed(l, [W_UP, W_DOWN], g_mlp_w, rides[0])
        small_grads["mlp_norm_g"][l] = dg_mlp.reshape(D_MODEL)
        dw_out = matmul_tn(a["merged"], [dx1b], a_block=half, out_cols=None, relu2=False, name="dw_out")
        dy_sb, dy_sw, do_sb, do_sw, dgl = out_bwd(dx1b, mats[W_OUT], a["gates"], a["y_sb"], a["y_sw"],
                                                  mats[W_BSB], mats[W_BSW], name="out_bwd")
        dw_bsb = matmul_tn(a["o_sb"], [dy_sb], a_block=half, out_cols=D_MODEL // N_DEV, relu2=False, name="dw_branch_sb")
        dw_bsw = matmul_tn(a["o_sw"], [dy_sw], a_block=half, out_cols=D_MODEL // N_DEV, relu2=False, name="dw_branch_swa")
        g_mix_w = [dw_bsb, dw_bsw, dw_out.reshape((N_DEV,) + shard_shapes[W_OUT])]
        job = chip_job([(l, W_UP), (l, W_DOWN)])
        dq_sb, dk_sb, dv_sb, (outs, landed) = sb_attn_bwd(a["proj"], tri, a["o_sb"], do_sb, name="sb_bwd",
                                                         ride=[job, PairJob(g_mix_w)])
        landed_chip(job, outs)
        summed(l, [W_BSB, W_BSW, W_OUT], g_mix_w, landed)
        job = chip_job([(l, W_BSB), (l, W_BSW), (l, W_OUT)])
        dqn, dkn, dv_sw, dsink, (outs,) = swa_attn_bwd(a["qn"], a["kn"], a["proj"], a["sink_p"], a["o_sw"], do_sw,
                                                      name="swa_bwd", ride=[job])
        landed_chip(job, outs)
        dq_sw, dk_sw, dv_swb, dgq, dgk = swa_prep_bwd(a["proj"], cos_p, sin_p, a["gq"], a["gk"], dqn, dkn, dv_sw,
                                                      name="swa_prep_bwd")
        small_grads["q_norm_g"][l] = dgq.reshape(SWA_Q_WIDTH // HEAD_DIM, HEAD_DIM).sum(0)
        small_grads["k_norm_g"][l] = dgk.reshape(LANES // HEAD_DIM, HEAD_DIM).sum(0)
        small_grads["sinks"][l] = dsink[:, 0, ::HEAD_DIM].reshape(SWA_Q_WIDTH // HEAD_DIM)
        pieces = [dq_sb, dk_sb, dv_sb, dq_sw, dk_sw, dv_swb, dgl]
        g_in = [matmul_tn_row_blocks(pieces, a["h"], n_blocks=N_DEV, name="dw_in")]
        if l > 0:
            (dx, dxb, dg_mix), (landed,) = matmul_nt_norm_bwd(pieces, mats[W_IN], a["x"], a["g_mix"], dx1,
                                                             name="in_proj_bwd", ride=[PairJob(g_in)])
            summed(l, [W_IN], g_in, landed)
            in_pending = l
        else:
            summed(l, [W_IN], g_in, exchange_alone(PairJob(g_in), name="grad_pair_exchange_in0"))
            job = chip_job([(l, W_IN)])
            (dx, dxb, dg_mix), (outs,) = matmul_nt_norm_bwd(pieces, mats[W_IN], a["x"], a["g_mix"], dx1,
                                                           name="in_proj_bwd_last", ride=[job])
            landed_chip(job, outs)
        small_grads["mix_norm_g"][l] = dg_mix.reshape(D_MODEL)

    out_g, out_d, out_m, out_v = {}, {}, {}, {}
    for i, n in enumerate(MATRIX_NAMES):
        outs = reduce_adamw(parts[i], lands[i], chip, local[n], local_m[n], local_v[n], name="adamw_" + n)
        if n == "w_in":
            outs = [jnp.swapaxes(t, 1, 2) for t in outs]
        out_g[n], out_d[n], out_m[n], out_v[n] = outs
    small_shapes = {n: weights[n].shape for n in SMALL_NAMES}
    small_all = gather_small(_pack_small({n: jnp.stack(v) for n, v in small_grads.items()}), name="gather_small_grads")
    sg, sd, sm, sv = small_adamw(small_all, _pack_small(weights), _pack_small(mom_m), _pack_small(mom_v),
                                 name="small_adamw")
    for tree, packed_small in ((out_g, sg), (out_d, sd), (out_m, sm), (out_v, sv)):
        tree.update(_unpack_small(packed_small, small_shapes))
    return loss, dx, (out_g, out_d, out_m, out_v)


def kernel(x, mix_norm_g, w_in, q_norm_g, k_norm_g, sinks, w_branch_sb, w_branch_swa, w_out, mlp_norm_g, w_up, w_down, loss_target, m_mix_norm_g, m_w_in, m_q_norm_g, m_k_norm_g, m_sinks, m_w_branch_sb, m_w_branch_swa, m_w_out, m_mlp_norm_g, m_w_up, m_w_down, v_mix_norm_g, v_w_in, v_q_norm_g, v_k_norm_g, v_sinks, v_w_branch_sb, v_w_branch_swa, v_w_out, v_mlp_norm_g, v_w_up, v_w_down):
    weights = dict(mix_norm_g=mix_norm_g, w_in=w_in, q_norm_g=q_norm_g, k_norm_g=k_norm_g, sinks=sinks,
                   w_branch_sb=w_branch_sb, w_branch_swa=w_branch_swa, w_out=w_out, mlp_norm_g=mlp_norm_g, w_up=w_up,
                   w_down=w_down)
    mom_m = dict(mix_norm_g=m_mix_norm_g, w_in=m_w_in, q_norm_g=m_q_norm_g, k_norm_g=m_k_norm_g, sinks=m_sinks,
                 w_branch_sb=m_w_branch_sb, w_branch_swa=m_w_branch_swa, w_out=m_w_out, mlp_norm_g=m_mlp_norm_g,
                 w_up=m_w_up, w_down=m_w_down)
    mom_v = dict(mix_norm_g=v_mix_norm_g, w_in=v_w_in, q_norm_g=v_q_norm_g, k_norm_g=v_k_norm_g, sinks=v_sinks,
                 w_branch_sb=v_w_branch_sb, w_branch_swa=v_w_branch_swa, w_out=v_w_out, mlp_norm_g=v_mlp_norm_g,
                 w_up=v_w_up, w_down=v_w_down)
    loss_part, grad_x, outs = train_step(x[0], loss_target[0], weights, mom_m, mom_v)
    loss = lax.psum(loss_part[0, 0], MESH_AXES)
    return (loss, grad_x[None], *[outs[0][n] for n in WEIGHT_ORDER], *[outs[1][n] for n in WEIGHT_ORDER],
            *[outs[2][n] for n in WEIGHT_ORDER], *[outs[3][n] for n in WEIGHT_ORDER])
```

```python
import functools
import math

import jax
import jax.numpy as jnp
from jax import lax
from jax.experimental import pallas as pl
from jax.experimental.pallas import tpu as pltpu

F32 = jnp.float32
BF16 = jnp.bfloat16

DEPTH = 4
D_MODEL = 1024
HEAD_DIM = 64
LANES = 128
WINDOW = 128
SB_WIDTH = 512
SWA_Q_WIDTH = 512
SWA_KV_WIDTH = 128
ATTN_WIDTH = 3 * SB_WIDTH + SWA_Q_WIDTH + 2 * SWA_KV_WIDTH
IN_WIDTH = ATTN_WIDTH + 2 * D_MODEL
ROPE_THETA = 10000.0
NORM_EPS = 1e-6
SCALE = HEAD_DIM ** -0.5
NEG = -1e30
N_DEV = 8
N_CHIPS = 4

ADAM_LR = 0.001
ADAM_B1 = 0.9
ADAM_B2 = 0.999
ADAM_EPS = 1e-08
ADAM_WD = 0.01
ADAM_STEP = 10

SB_TQ = 128
SB_TK1 = 384
SB_TK = 256
SB_CUTOFF = -88.0
SWA_TQ = 128
SWA_TK = 256
ROW_TILE = 512
VMEM_LIMIT = 56 * 1024 * 1024

MATRIX_NAMES = ("w_in", "w_branch_sb", "w_branch_swa", "w_out", "w_up", "w_down")
W_IN, W_BSB, W_BSW, W_OUT, W_UP, W_DOWN = range(6)
ROW_SHARDED = ("w_out", "w_down")
SMALL_NAMES = ("mix_norm_g", "q_norm_g", "k_norm_g", "sinks", "mlp_norm_g")
WEIGHT_ORDER = ("mix_norm_g", "w_in", "q_norm_g", "k_norm_g", "sinks", "w_branch_sb", "w_branch_swa", "w_out",
                "mlp_norm_g", "w_up", "w_down")
MESH_AXES = ("x", "y", "c")

ANY = pl.BlockSpec(memory_space=pl.ANY)
MESH = pl.DeviceIdType.MESH


def _params(*sem):
    return pltpu.CompilerParams(dimension_semantics=sem, vmem_limit_bytes=VMEM_LIMIT)


def _dot(a, b):
    return jnp.dot(a, b, preferred_element_type=F32)


def _dot_nt(a, b):
    return lax.dot_general(a, b, (((1,), (1,)), ((), ())), preferred_element_type=F32)


def _dot_tn(a, b):
    return lax.dot_general(a, b, (((0,), (0,)), ((), ())), preferred_element_type=F32)


def _split_bf16(x):
    hi = lax.bitcast_convert_type(lax.bitcast_convert_type(x, jnp.uint32) & jnp.uint32(0xFFFF0000), F32)
    return hi.astype(BF16), (x - hi).astype(BF16)


def _rsqrt_ms(x):
    return lax.rsqrt(jnp.mean(x * x, axis=-1, keepdims=True) + NORM_EPS)


def _place():
    return lax.axis_index("x"), lax.axis_index("y"), lax.axis_index("c")


class _Gather:
    def __init__(self, x_refs, out_refs, send_sems, recv_sems, local_sems, rows=None):
        self.x_refs, self.out_refs = x_refs, out_refs
        self.send_sems, self.recv_sems, self.local_sems = send_sems, recv_sems, local_sems
        self.n = len(x_refs)
        self.rows = rows
        x, y, c = _place()
        self.c = c
        self.me, self.sibling = (x, y, c), (x, y, 1 - c)
        self.chips = [(1 - x, y), (x, 1 - y), (1 - x, 1 - y)]

    def _part(self, ref):
        return ref if self.rows is None else ref.at[pl.ds(*self.rows)]

    def _slot(self, w, blk):
        return self._part(self.out_refs[w].at[4 * blk[0] + 2 * blk[1] + blk[2]])

    def _copy(self, k, w, blk, to, own=False):
        dst = self._slot(w, blk)
        return pltpu.make_async_remote_copy(
            src_ref=self._part(self.x_refs[w]) if own else dst, dst_ref=dst, send_sem=self.send_sems.at[k, w],
            recv_sem=self.recv_sems.at[k, w], device_id=to, device_id_type=MESH)

    def _mine(self, w):
        return pltpu.make_async_copy(self._part(self.x_refs[w]), self._slot(w, self.me), self.local_sems.at[w])

    def _first(self, w):
        return [self._copy(0, w, self.me, self.sibling, own=True)] + [
            self._copy(1 + j, w, self.me, (*chip, self.c), own=True) for j, chip in enumerate(self.chips)]

    def _passed(self, j, w):
        return self._copy(4 + j, w, (*self.chips[j], self.c), self.sibling)

    def start(self):
        for w in range(self.n):
            self._mine(w).start()
            for cp in self._first(w):
                cp.start()

    def relay(self):
        for j, chip in enumerate(self.chips):
            for w in range(self.n):
                self._copy(1 + j, w, (*chip, self.c), self.me).wait_recv()
                self._passed(j, w).start()

    def finish(self):
        for w in range(self.n):
            self._copy(0, w, self.sibling, self.me).wait_recv()
            for j, chip in enumerate(self.chips):
                self._copy(4 + j, w, (*chip, 1 - self.c), self.me).wait_recv()
            for cp in self._first(w):
                cp.wait_send()
            for j in range(3):
                self._passed(j, w).wait_send()
            self._mine(w).wait()


class GatherJob:
    def __init__(self, shards, rows=None, stacks=None):
        n = len(shards)
        self.n, self.rows = n, rows
        self.inputs = list(shards) + (list(stacks) if stacks is not None else [])
        self.out_shapes = [jax.ShapeDtypeStruct((N_DEV,) + s.shape, s.dtype) for s in shards]
        self.aliases = {n + i: i for i in range(n)} if stacks is not None else {}
        self.scratch = [pltpu.SemaphoreType.DMA((7, n)), pltpu.SemaphoreType.DMA((7, n)),
                        pltpu.SemaphoreType.DMA((n,))]

    def bind(self, in_refs, out_refs, scratch_refs):
        return _Gather(in_refs[:self.n], out_refs, *scratch_refs, rows=self.rows)


class _Copies:
    def __init__(self, copies):
        self.copies = copies

    def start(self):
        for cp in self.copies:
            cp.start()

    def relay(self):
        pass

    def finish(self):
        for cp in self.copies:
            cp.wait_recv()
        for cp in self.copies:
            cp.wait_send()


class ChipJob:
    def __init__(self, items, parts, lands):
        self.ws = sorted({item[1] for item in items})
        n = len(self.ws)
        self.items = [(item[0], self.ws.index(item[1]), item[2] if len(item) > 2 else None) for item in items]
        self.inputs = [parts[w] for w in self.ws] + [lands[w] for w in self.ws]
        self.out_shapes = [jax.ShapeDtypeStruct(lands[w].shape, lands[w].dtype) for w in self.ws]
        self.aliases = {n + i: i for i in range(n)}
        self.scratch = [pltpu.SemaphoreType.DMA((3, n)), pltpu.SemaphoreType.DMA((3, n))]

    def bind(self, in_refs, out_refs, scratch_refs):
        send_sems, recv_sems = scratch_refs
        x, y, c = _place()
        chips = [(1 - x, y), (x, 1 - y), (1 - x, 1 - y)]

        def part(ref, rows):
            return ref if rows is None else ref.at[pl.ds(*rows)]

        return _Copies([pltpu.make_async_remote_copy(
            src_ref=part(in_refs[i].at[layer, 2 * px + py], rows), dst_ref=part(out_refs[i].at[layer, j], rows),
            send_sem=send_sems.at[j, i], recv_sem=recv_sems.at[j, i], device_id=(px, py, c), device_id_type=MESH)
            for layer, i, rows in self.items for j, (px, py) in enumerate(chips)])


class PairJob:
    def __init__(self, grads):
        n = len(grads)
        self.inputs = list(grads)
        self.out_shapes = [jax.ShapeDtypeStruct((N_CHIPS,) + g.shape[1:], g.dtype) for g in grads]
        self.aliases = {}
        self.scratch = [pltpu.SemaphoreType.DMA((N_CHIPS, n)), pltpu.SemaphoreType.DMA((N_CHIPS, n))]

    def bind(self, in_refs, out_refs, scratch_refs):
        send_sems, recv_sems = scratch_refs
        x, y, c = _place()
        return _Copies([pltpu.make_async_remote_copy(
            src_ref=in_refs[w].at[2 * k + (1 - c)], dst_ref=out_refs[w].at[k], send_sem=send_sems.at[k, w],
            recv_sem=recv_sems.at[k, w], device_id=(x, y, 1 - c), device_id_type=MESH)
            for w in range(len(in_refs)) for k in range(N_CHIPS)])


def _call(body, *, name, grid, in_specs, out_specs, out_shape, args, scratch_shapes=(), ride=()):
    out_specs, out_shape, in_specs = tuple(out_specs), tuple(out_shape), list(in_specs)
    scratch_shapes = list(scratch_shapes)
    order = ("arbitrary",) * len(grid)
    if not ride:
        outs = pl.pallas_call(body, name=name, grid=grid, in_specs=in_specs, out_specs=out_specs, out_shape=out_shape,
                              scratch_shapes=scratch_shapes, compiler_params=_params(*order))(*args)
        return tuple(outs), []
    n_in, n_out, n_scr = len(in_specs), len(out_specs), len(scratch_shapes)
    n_steps = math.prod(grid)
    relay_early = n_steps >= 8
    relay_at = n_steps - n_steps // 4 if relay_early else n_steps - 1

    def split(refs, pos, counts):
        groups = []
        for k in counts:
            groups.append(refs[pos:pos + k])
            pos += k
        return groups, pos

    def wrapped(*refs):
        ins, pos = refs[:n_in], n_in
        job_in, pos = split(refs, pos, [len(j.inputs) for j in ride])
        outs, pos = refs[pos:pos + n_out], pos + n_out
        job_out, pos = split(refs, pos, [len(j.out_shapes) for j in ride])
        scr, pos = refs[pos:pos + n_scr], pos + n_scr
        job_scr, pos = split(refs, pos, [len(j.scratch) for j in ride])
        bound = [j.bind(i, o, s) for j, i, o, s in zip(ride, job_in, job_out, job_scr)]
        step = pl.program_id(0)
        for axis in range(1, len(grid)):
            step = step * grid[axis] + pl.program_id(axis)

        @pl.when(step == 0)
        def _():
            for b in bound:
                b.start()

        if relay_early:
            @pl.when(step == relay_at)
            def _():
                for b in bound:
                    b.relay()

        body(*ins, *outs, *scr)

        @pl.when(step == n_steps - 1)
        def _():
            if not relay_early:
                for b in bound:
                    b.relay()
            for b in bound:
                b.finish()

    aliases, in_pos, out_pos = {}, n_in, n_out
    for j in ride:
        aliases.update({in_pos + i: out_pos + o for i, o in j.aliases.items()})
        in_pos += len(j.inputs)
        out_pos += len(j.out_shapes)
    results = pl.pallas_call(
        wrapped, name=name, grid=grid, in_specs=in_specs + [ANY] * (in_pos - n_in),
        out_specs=out_specs + (ANY,) * (out_pos - n_out),
        out_shape=out_shape + tuple(s for j in ride for s in j.out_shapes),
        scratch_shapes=scratch_shapes + [s for j in ride for s in j.scratch], input_output_aliases=aliases,
        compiler_params=pltpu.CompilerParams(dimension_semantics=order, vmem_limit_bytes=VMEM_LIMIT,
                                             has_side_effects=True),
    )(*args, *[a for j in ride for a in j.inputs])
    job_results, pos = split(list(results), n_out, [len(j.out_shapes) for j in ride])
    return tuple(results[:n_out]), job_results


def exchange_alone(job, *, name):
    n_in, n_out = len(job.inputs), len(job.out_shapes)

    def body(*refs):
        b = job.bind(refs[:n_in], refs[n_in:n_in + n_out], refs[n_in + n_out:])
        b.start()
        b.relay()
        b.finish()

    return list(pl.pallas_call(
        body, name=name, out_shape=tuple(job.out_shapes), in_specs=[ANY] * n_in, out_specs=(ANY,) * n_out,
        scratch_shapes=job.scratch, input_output_aliases=job.aliases,
        compiler_params=pltpu.CompilerParams(has_side_effects=True),
    )(*job.inputs))


PAIR_SUM_CHUNKS = 1


def pair_sum(layer, grads, landed, parts, core, *, name):
    n = len(grads)

    def body(c_ref, *refs):
        g_refs, l_refs, o_refs = refs[:n], refs[n:2 * n], refs[3 * n:]
        for w in range(n):
            o_refs[w][...] = (g_refs[w][...].astype(F32) + l_refs[w][...].astype(F32)).astype(BF16)

    def blk(g):
        return (None, g.shape[1] // PAIR_SUM_CHUNKS, g.shape[2])

    in_specs = [pl.BlockSpec(blk(g), lambda k, i, c_ref: (2 * k + c_ref[0], i, 0)) for g in grads]
    in_specs += [pl.BlockSpec(blk(g), lambda k, i, c_ref: (k, i, 0)) for g in grads]
    in_specs += [ANY] * n
    out_specs = tuple(pl.BlockSpec((None,) + blk(g), lambda k, i, c_ref: (layer, k, i, 0)) for g in grads)
    return list(pl.pallas_call(
        body, name=name, out_shape=tuple(jax.ShapeDtypeStruct(p.shape, p.dtype) for p in parts),
        grid_spec=pltpu.PrefetchScalarGridSpec(num_scalar_prefetch=1, grid=(N_CHIPS, PAIR_SUM_CHUNKS),
                                               in_specs=in_specs, out_specs=out_specs),
        input_output_aliases={1 + 2 * n + w: w for w in range(n)},
        compiler_params=_params("parallel", "parallel"),
    )(core, *grads, *landed, *parts))


def _adamw(w, g, m, v):
    m = ADAM_B1 * m + (1.0 - ADAM_B1) * g
    v = ADAM_B2 * v + (1.0 - ADAM_B2) * (g * g)
    m_hat = m / (1.0 - ADAM_B1 ** ADAM_STEP)
    v_hat = v / (1.0 - ADAM_B2 ** ADAM_STEP)
    delta = -ADAM_LR * (m_hat / (jnp.sqrt(v_hat) + ADAM_EPS) + ADAM_WD * w)
    return delta, m, v


def reduce_adamw(part, land, chip, w, m, v, *, name):
    _, r, c = w.shape
    tr = 256 if r % 256 == 0 else (r // 2 if r > 256 else r)

    def body(k_ref, own_ref, l0_ref, l1_ref, l2_ref, w_ref, m_ref, v_ref, g_out, d_out, m_out, v_out):
        g = own_ref[...].astype(F32) + l0_ref[...].astype(F32) + l1_ref[...].astype(F32) + l2_ref[...].astype(F32)
        delta, m_new, v_new = _adamw(w_ref[...], g, m_ref[...], v_ref[...])
        g_out[...] = g
        d_out[...] = delta
        m_out[...] = m_new
        v_out[...] = v_new

    row = pl.BlockSpec((None, tr, c), lambda l, i, k_ref: (l, i, 0))

    def slot(j):
        return pl.BlockSpec((None, None, tr, c), lambda l, i, k_ref: (l, j, i, 0))

    return pl.pallas_call(
        body, name=name, out_shape=(jax.ShapeDtypeStruct(w.shape, F32),) * 4,
        grid_spec=pltpu.PrefetchScalarGridSpec(
            num_scalar_prefetch=1, grid=(DEPTH, r // tr),
            in_specs=[pl.BlockSpec((None, None, tr, c), lambda l, i, k_ref: (l, k_ref[0], i, 0)), slot(0), slot(1),
                      slot(2), row, row, row],
            out_specs=(row, row, row, row)),
        compiler_params=_params("parallel", "parallel"),
    )(chip, part, land, land, land, w, m, v)


def gather_small(block, *, name):
    def body(x_ref, out_ref, send_sems, recv_sems, local_sem):
        x, y, c = _place()
        me = 4 * x + 2 * y + c
        mine = pltpu.make_async_copy(x_ref, out_ref.at[me], local_sem)
        mine.start()
        peers = [(x ^ (k >> 2), y ^ ((k >> 1) & 1), c ^ (k & 1)) for k in range(1, N_DEV)]
        copies = [pltpu.make_async_remote_copy(
            src_ref=x_ref, dst_ref=out_ref.at[me], send_sem=send_sems.at[k], recv_sem=recv_sems.at[k],
            device_id=peer, device_id_type=MESH) for k, peer in enumerate(peers)]
        for cp in copies:
            cp.start()
        for k, (px, py, pc) in enumerate(peers):
            pltpu.make_async_remote_copy(
                src_ref=x_ref, dst_ref=out_ref.at[4 * px + 2 * py + pc], send_sem=send_sems.at[k],
                recv_sem=recv_sems.at[k], device_id=(px, py, pc), device_id_type=MESH).wait_recv()
        for cp in copies:
            cp.wait_send()
        mine.wait()

    return pl.pallas_call(
        body, name=name, out_shape=jax.ShapeDtypeStruct((N_DEV,) + block.shape, block.dtype),
        in_specs=[ANY], out_specs=ANY,
        scratch_shapes=[pltpu.SemaphoreType.DMA((7,)), pltpu.SemaphoreType.DMA((7,)), pltpu.SemaphoreType.DMA],
        compiler_params=pltpu.CompilerParams(has_side_effects=True),
    )(block)


def small_adamw(gathered, w, m, v, *, name):
    def body(g_ref, w_ref, m_ref, v_ref, g_out, d_out, m_out, v_out):
        g = g_ref[0]
        for d in range(1, N_DEV):
            g = g + g_ref[d]
        delta, m_new, v_new = _adamw(w_ref[...], g, m_ref[...], v_ref[...])
        g_out[...] = g
        d_out[...] = delta
        m_out[...] = m_new
        v_out[...] = v_new

    return pl.pallas_call(
        body, name=name, out_shape=(jax.ShapeDtypeStruct(w.shape, F32),) * 4,
    )(gathered, w, m, v)


def norm_matmul(x, g, w, *, gate_split, name, ride=()):
    s, d = x.shape
    tm = min(ROW_TILE, s)
    blocked = w.ndim == 3
    n = w.shape[0] if not blocked else w.shape[0] * w.shape[2]

    def body(x_ref, g_ref, w_ref, h_ref, *outs):
        xv = x_ref[...]
        h = ((xv * _rsqrt_ms(xv)) * g_ref[...]).astype(BF16)
        h_ref[...] = h
        if blocked:
            nb = w_ref.shape[2]
            for j in range(w_ref.shape[0]):
                outs[0][:, j * nb:(j + 1) * nb] = _dot(h, w_ref[j]).astype(BF16)
        else:
            p = _dot_nt(h, w_ref[...])
            outs[0][...] = p[:, :gate_split].astype(BF16)
            outs[1][...] = (1.0 / (1.0 + jnp.exp(-p[:, gate_split:]))).astype(BF16)

    row = lambda i: (i, 0)
    fixed = lambda i: (0, 0)
    if blocked:
        out_shape = (jax.ShapeDtypeStruct((s, d), BF16), jax.ShapeDtypeStruct((s, n), BF16))
        out_specs = (pl.BlockSpec((tm, d), row), pl.BlockSpec((tm, n), row))
        w_spec = pl.BlockSpec(w.shape, lambda i: (0, 0, 0))
    else:
        out_shape = (jax.ShapeDtypeStruct((s, d), BF16), jax.ShapeDtypeStruct((s, gate_split), BF16),
                     jax.ShapeDtypeStruct((s, n - gate_split), BF16))
        out_specs = (pl.BlockSpec((tm, d), row), pl.BlockSpec((tm, gate_split), row),
                     pl.BlockSpec((tm, n - gate_split), row))
        w_spec = pl.BlockSpec((n, d), fixed)
    return _call(body, name=name, grid=(s // tm,), out_shape=out_shape, out_specs=out_specs,
                 in_specs=[pl.BlockSpec((tm, d), row), pl.BlockSpec((1, d), fixed), w_spec], args=(x, g, w), ride=ride)


def merge_out_fwd(x, o_sb, o_sw, gates, w_bsb, w_bsw, w_o, *, name, ride=()):
    s, d = x.shape
    tm = min(ROW_TILE, s)

    def body(x_ref, osb_ref, osw_ref, g_ref, wsb_ref, wsw_ref, wo_ref, x1_ref, ysb_ref, ysw_ref, mg_ref):
        y_sb = _dot(osb_ref[...].astype(BF16), wsb_ref[...])
        y_sw = _dot(osw_ref[...].astype(BF16), wsw_ref[...])
        g = g_ref[...].astype(F32)
        merged = (g[:, :d] * y_sb + g[:, d:] * y_sw).astype(BF16)
        ysb_ref[...] = y_sb.astype(BF16)
        ysw_ref[...] = y_sw.astype(BF16)
        mg_ref[...] = merged
        x1_ref[...] = x_ref[...] + _dot(merged, wo_ref[...])

    row = lambda i: (i, 0)
    fixed = lambda i: (0, 0)
    wd = o_sb.shape[1]
    return _call(
        body, name=name, grid=(s // tm,),
        out_shape=(jax.ShapeDtypeStruct((s, d), F32),) + (jax.ShapeDtypeStruct((s, d), BF16),) * 3,
        in_specs=[pl.BlockSpec((tm, d), row), pl.BlockSpec((tm, wd), row), pl.BlockSpec((tm, wd), row),
                  pl.BlockSpec((tm, 2 * d), row), pl.BlockSpec((wd, d), fixed), pl.BlockSpec((wd, d), fixed),
                  pl.BlockSpec((d, d), fixed)],
        out_specs=(pl.BlockSpec((tm, d), row),) * 4, args=(x, o_sb, o_sw, gates, w_bsb, w_bsw, w_o), ride=ride)


def mlp_down_fwd(x1, u, w_down, *, name, ride=()):
    s, d = x1.shape
    f = u.shape[1]
    tm = min(ROW_TILE, s)

    def body(x_ref, u_ref, w_ref, o_ref):
        a = jnp.maximum(u_ref[...].astype(F32), 0.0)
        o_ref[...] = x_ref[...] + _dot((a * a).astype(BF16), w_ref[...])

    row = lambda i: (i, 0)
    return _call(
        body, name=name, grid=(s // tm,), out_shape=(jax.ShapeDtypeStruct((s, d), F32),),
        in_specs=[pl.BlockSpec((tm, d), row), pl.BlockSpec((tm, f), row), pl.BlockSpec((f, d), lambda i: (0, 0))],
        out_specs=(pl.BlockSpec((tm, d), row),), args=(x1, u, w_down), ride=ride)


def loss_head(y, target, *, name):
    s, d = y.shape
    tm = min(ROW_TILE, s)

    def body(y_ref, t_ref, dy_ref, dyb_ref, loss_ref):
        @pl.when(pl.program_id(0) == 0)
        def _():
            loss_ref[...] = jnp.zeros_like(loss_ref)

        e = y_ref[...] - t_ref[...]
        dy = e * (1.0 / d)
        dy_ref[...] = dy
        dyb_ref[...] = dy.astype(BF16)
        per_row = jnp.sum(e * e, axis=1, keepdims=True) * (0.5 / d)
        loss_ref[...] += jnp.sum(per_row, axis=0, keepdims=True)

    row = lambda i: (i, 0)
    return pl.pallas_call(
        body, name=name, grid=(s // tm,),
        out_shape=(jax.ShapeDtypeStruct((s, d), F32), jax.ShapeDtypeStruct((s, d), BF16),
                   jax.ShapeDtypeStruct((1, 1), F32)),
        in_specs=[pl.BlockSpec((tm, d), row), pl.BlockSpec((tm, d), row)],
        out_specs=(pl.BlockSpec((tm, d), row), pl.BlockSpec((tm, d), row), pl.BlockSpec((1, 1), lambda i: (0, 0))),
        compiler_params=_params("arbitrary"),
    )(y, target)


def mlp_bwd_up(dxb, u, w_down, *, name, ride=()):
    s, d = dxb.shape
    f = u.shape[1]
    tm = min(ROW_TILE, s)

    def body(dx_ref, u_ref, w_ref, du_ref):
        da = _dot_nt(dx_ref[...], w_ref[...])
        du_ref[...] = (da * (2.0 * jnp.maximum(u_ref[...].astype(F32), 0.0))).astype(BF16)

    row = lambda i: (i, 0)
    return _call(body, name=name, grid=(s // tm,), out_shape=(jax.ShapeDtypeStruct((s, f), BF16),),
                 in_specs=[pl.BlockSpec((tm, d), row), pl.BlockSpec((tm, f), row),
                           pl.BlockSpec((f, d), lambda i: (0, 0))],
                 out_specs=(pl.BlockSpec((tm, f), row),), args=(dxb, u, w_down), ride=ride)


def matmul_nt_norm_bwd(pieces, w, x, g, dres, *, name, ride=()):
    s = x.shape[0]
    d = x.shape[1]
    tm = min(ROW_TILE, s)
    blocked = w.ndim == 3
    n_pieces = len(pieces)
    widths = [p.shape[1] for p in pieces]

    def body(*refs):
        p_refs = refs[:n_pieces]
        w_ref, x_ref, g_ref, dres_ref, dx_ref, dxb_ref, dg_ref = refs[n_pieces:]

        @pl.when(pl.program_id(0) == 0)
        def _():
            dg_ref[...] = jnp.zeros_like(dg_ref)

        if blocked:
            nb = w_ref.shape[2]
            dh = _dot_nt(p_refs[0][:, :nb], w_ref[0])
            for j in range(1, w_ref.shape[0]):
                dh = dh + _dot_nt(p_refs[0][:, j * nb:(j + 1) * nb], w_ref[j])
        else:
            dh, off = None, 0
            for p_ref, width in zip(p_refs, widths):
                part = _dot(p_ref[...], w_ref[off:off + width, :])
                dh = part if dh is None else dh + part
                off += width
        xv = x_ref[...]
        r = _rsqrt_ms(xv)
        dyg = dh * g_ref[...]
        dx = dres_ref[...] + r * dyg - xv * ((r * r * r) * jnp.mean(dyg * xv, axis=-1, keepdims=True))
        dx_ref[...] = dx
        dxb_ref[...] = dx.astype(BF16)
        dg_ref[...] += jnp.sum(dh * (xv * r), axis=0, keepdims=True)

    row = lambda i: (i, 0)
    fixed = lambda i: (0, 0)
    w_spec = pl.BlockSpec(w.shape, (lambda i: (0, 0, 0)) if blocked else fixed)
    return _call(
        body, name=name, grid=(s // tm,),
        out_shape=(jax.ShapeDtypeStruct((s, d), F32), jax.ShapeDtypeStruct((s, d), BF16),
                   jax.ShapeDtypeStruct((1, d), F32)),
        in_specs=[pl.BlockSpec((tm, width), row) for width in widths] + [
            w_spec, pl.BlockSpec((tm, d), row), pl.BlockSpec((1, d), fixed), pl.BlockSpec((tm, d), row)],
        out_specs=(pl.BlockSpec((tm, d), row), pl.BlockSpec((tm, d), row), pl.BlockSpec((1, d), fixed)),
        args=(*pieces, w, x, g, dres), ride=ride)


def out_bwd(dx1b, w_o, gates, y_sb, y_sw, w_bsb, w_bsw, *, name):
    s, d = dx1b.shape
    wd = w_bsb.shape[0]
    tm = min(ROW_TILE, s)

    def body(dx_ref, wo_ref, g_ref, ysb_ref, ysw_ref, wsb_ref, wsw_ref, dysb_ref, dysw_ref, dosb_ref, dosw_ref, dgl_ref):
        dm = _dot_nt(dx_ref[...], wo_ref[...])
        g = g_ref[...].astype(F32)
        g0, g1 = g[:, :d], g[:, d:]
        dy_sb = (dm * g0).astype(BF16)
        dy_sw = (dm * g1).astype(BF16)
        dysb_ref[...] = dy_sb
        dysw_ref[...] = dy_sw
        dosb_ref[...] = _dot_nt(dy_sb, wsb_ref[...]).astype(BF16)
        dosw_ref[...] = _dot_nt(dy_sw, wsw_ref[...]).astype(BF16)
        dgl_ref[:, :d] = (dm * ysb_ref[...].astype(F32) * (g0 * (1.0 - g0))).astype(BF16)
        dgl_ref[:, d:] = (dm * ysw_ref[...].astype(F32) * (g1 * (1.0 - g1))).astype(BF16)

    row = lambda i: (i, 0)
    fixed = lambda i: (0, 0)
    return pl.pallas_call(
        body, name=name, grid=(s // tm,),
        out_shape=(jax.ShapeDtypeStruct((s, d), BF16), jax.ShapeDtypeStruct((s, d), BF16),
                   jax.ShapeDtypeStruct((s, wd), BF16), jax.ShapeDtypeStruct((s, wd), BF16),
                   jax.ShapeDtypeStruct((s, 2 * d), BF16)),
        in_specs=[pl.BlockSpec((tm, d), row), pl.BlockSpec((d, d), fixed), pl.BlockSpec((tm, 2 * d), row),
                  pl.BlockSpec((tm, d), row), pl.BlockSpec((tm, d), row), pl.BlockSpec((wd, d), fixed),
                  pl.BlockSpec((wd, d), fixed)],
        out_specs=(pl.BlockSpec((tm, d), row), pl.BlockSpec((tm, d), row), pl.BlockSpec((tm, wd), row),
                   pl.BlockSpec((tm, wd), row), pl.BlockSpec((tm, 2 * d), row)),
        compiler_params=_params("parallel"),
    )(dx1b, w_o, gates, y_sb, y_sw, w_bsb, w_bsw)


def matmul_tn(a, pieces, *, a_block, out_cols, relu2, name):
    s, m = a.shape
    widths = [p.shape[1] for p in pieces]
    n = sum(widths)
    n_pieces = len(pieces)
    ts = min(512 if n >= 4096 else 2048, s)
    n_steps = s // ts
    if out_cols is None:
        out_shape = jax.ShapeDtypeStruct((m // a_block, a_block, n), BF16)
        out_spec = pl.BlockSpec((None, a_block, n), lambda i, k: (i, 0, 0))
    else:
        out_shape = jax.ShapeDtypeStruct((n // out_cols, m, out_cols), BF16)
        out_spec = pl.BlockSpec((n // out_cols, a_block, out_cols), lambda i, k: (0, i, 0))

    def body(a_ref, *refs):
        b_refs, o_ref, acc = refs[:n_pieces], refs[n_pieces], refs[n_pieces + 1]
        k = pl.program_id(1)

        @pl.when(k == 0)
        def _():
            acc[...] = jnp.zeros_like(acc)

        av = a_ref[...]
        if relu2:
            af = jnp.maximum(av.astype(F32), 0.0)
            av = af * af
        av = av.astype(BF16)
        off = 0
        for b_ref in b_refs:
            width = b_ref.shape[1]
            acc[:, off:off + width] += _dot_tn(av, b_ref[...].astype(BF16))
            off += width

        @pl.when(k == n_steps - 1)
        def _():
            if out_cols is None:
                o_ref[...] = acc[...].astype(BF16)
            else:
                for j in range(n // out_cols):
                    o_ref[j] = acc[:, j * out_cols:(j + 1) * out_cols].astype(BF16)

    return pl.pallas_call(
        body, name=name, grid=(m // a_block, n_steps), out_shape=out_shape,
        in_specs=[pl.BlockSpec((ts, a_block), lambda i, k: (k, i))] + [
            pl.BlockSpec((ts, width), lambda i, k: (k, 0)) for width in widths],
        out_specs=out_spec, scratch_shapes=[pltpu.VMEM((a_block, n), F32)],
        compiler_params=_params("parallel", "arbitrary"),
    )(a, *pieces)


def matmul_tn_row_blocks(pieces, b, *, n_blocks, name):
    s, n = b.shape
    widths = [p.shape[1] for p in pieces]
    m = sum(widths)
    rows = m // n_blocks
    n_pieces = len(pieces)
    ts = min(512, s)
    n_steps = s // ts
    half = n_blocks // 2

    def body(*refs):
        p_refs, b_ref, o_ref, a_tile, acc = refs[:n_pieces], refs[n_pieces], refs[n_pieces + 1], refs[-2], refs[-1]
        i, k = pl.program_id(0), pl.program_id(1)

        @pl.when(k == 0)
        def _():
            acc[...] = jnp.zeros_like(acc)

        off = 0
        for p_ref, width in zip(p_refs, widths):
            a_tile[:, off:off + width] = p_ref[...]
            off += width
        bv = b_ref[...]
        for side in range(2):
            @pl.when(i == side)
            def _():
                for j in range(half):
                    col = (side * half + j) * rows
                    acc[j] += _dot_tn(a_tile[:, col:col + rows], bv)

        @pl.when(k == n_steps - 1)
        def _():
            o_ref[...] = acc[...].astype(BF16)

    return pl.pallas_call(
        body, name=name, grid=(2, n_steps), out_shape=jax.ShapeDtypeStruct((n_blocks, rows, n), BF16),
        in_specs=[pl.BlockSpec((ts, width), lambda i, k: (k, 0)) for width in widths] + [
            pl.BlockSpec((ts, n), lambda i, k: (k, 0))],
        out_specs=pl.BlockSpec((half, rows, n), lambda i, k: (i, 0, 0)),
        scratch_shapes=[pltpu.VMEM((ts, m), BF16), pltpu.VMEM((half, rows, n), F32)],
        compiler_params=_params("parallel", "arbitrary"),
    )(*pieces, b)


def _softplus(z):
    return jnp.maximum(z, 0.0) + jnp.log(1.0 + jnp.exp(-jnp.abs(z)))


def _suffix_sums(x, tri2):
    groups = x.shape[1] // LANES
    outs, run = [None] * groups, None
    for g in reversed(range(groups)):
        xg = x[:, g * LANES:(g + 1) * LANES]
        hi, lo = _split_bf16(xg)
        inner = _dot(jnp.concatenate([hi, lo], axis=1), tri2)
        outs[g] = inner if run is None else inner + run
        total = jnp.sum(xg, axis=1, keepdims=True)
        run = total if run is None else run + total
    return jnp.concatenate(outs, axis=1), run


def _head_mask(h):
    return (lax.broadcasted_iota(jnp.int32, (1, LANES), 1) // HEAD_DIM) == h


def _stack_heads(x):
    zero = jnp.zeros_like(x)
    return jnp.concatenate([jnp.where(_head_mask(0), x, zero), jnp.where(_head_mask(1), x, zero)], axis=0)


def _unstack_heads(r, t):
    return jnp.where(_head_mask(0), r[:t], r[t:])


def _sb_positions(q0, tk):
    row = lax.broadcasted_iota(jnp.int32, (2 * SB_TQ, tk), 0)
    col = lax.broadcasted_iota(jnp.int32, (2 * SB_TQ, tk), 1)
    return q0 + jnp.where(row >= SB_TQ, row - SB_TQ, row), col


def _sb_first_key(q0):
    return pl.multiple_of(jnp.maximum(q0 + SB_TQ - SB_TK1, 0), SB_TQ)


def _sb_next_key(k_prev):
    return pl.multiple_of(jnp.maximum(k_prev - SB_TK, 0), SB_TQ)


def _sb_rows(q0):
    return pl.ds(pl.multiple_of(2 * q0, 2 * SB_TQ), 2 * SB_TQ)


def _sb_keep(live):
    return lambda x: jnp.where(live, x, 0.0)


def _sb_keep_first(q0, k0, interior):
    if not interior:
        tpos, col = _sb_positions(q0, SB_TK1)
        return _sb_keep(k0 + col < tpos)
    row = lax.broadcasted_iota(jnp.int32, (2 * SB_TQ, SB_TQ), 0)
    own = lax.broadcasted_iota(jnp.int32, (2 * SB_TQ, SB_TQ), 1) < jnp.where(row >= SB_TQ, row - SB_TQ, row)
    past = SB_TK1 - SB_TQ
    return lambda x: jnp.concatenate([x[:, :past], jnp.where(own, x[:, past:], 0.0)], axis=1)


SB_EDGE_TILES = 4
SB_SCAN_GROUP = 4


def _sb_first_pass(nq, first):
    edge = min(nq, SB_EDGE_TILES)
    lax.fori_loop(0, edge, first(False), 0, unroll=4)
    lax.fori_loop(edge, nq, first(True), 0, unroll=4)


def _sb_scan_tiles(c_all, nq, more):
    rows = SB_SCAN_GROUP * 2 * SB_TQ

    def group(g, carry):
        @pl.when(jnp.max(c_all[pl.ds(pl.multiple_of(g * rows, rows), rows), :]) > SB_CUTOFF)
        def _():
            lax.fori_loop(g * SB_SCAN_GROUP, (g + 1) * SB_SCAN_GROUP, more, 0)

        return carry

    lax.fori_loop(0, nq // SB_SCAN_GROUP, group, 0)


def sb_attn_fwd(proj, tri2, *, name, ride=()):
    s = proj.shape[0]
    nq = s // SB_TQ
    n_pairs = SB_WIDTH // LANES

    def body(q_ref, k_ref, v_ref, tri_ref, o_ref, c_all):
        def block(qh, k0, tk, keep, c):
            z = _dot_nt(qh, k_ref[pl.ds(k0, tk), :])
            sp = _softplus(z)
            tail, total = _suffix_sums(keep(-sp), tri_ref[...])
            w = keep(jnp.exp(z - sp + tail + c))
            return _dot(w.astype(BF16), v_ref[pl.ds(k0, tk), :]), c + total

        def load_q(q0):
            return _stack_heads(q_ref[pl.ds(q0, SB_TQ), :]) * SCALE

        def first(interior):
            def run(qb, carry):
                q0 = pl.multiple_of(qb * SB_TQ, SB_TQ)
                k0 = _sb_first_key(q0)
                acc, c = block(load_q(q0), k0, SB_TK1, _sb_keep_first(q0, k0, interior), jnp.zeros((2 * SB_TQ, 1), F32))
                o_ref[pl.ds(q0, SB_TQ), :] = _unstack_heads(acc, SB_TQ)
                c_all[_sb_rows(q0), :] = jnp.broadcast_to(jnp.where(k0 > 0, c, NEG), (2 * SB_TQ, LANES))
                return carry

            return run

        _sb_first_pass(nq, first)

        @pl.when(jnp.max(c_all[...]) > SB_CUTOFF)
        def _():
            def more(qb, carry):
                q0 = pl.multiple_of(qb * SB_TQ, SB_TQ)
                c0 = c_all[_sb_rows(q0), 0:1]

                @pl.when(jnp.max(c0) > SB_CUTOFF)
                def _():
                    qh = load_q(q0)
                    _, col = _sb_positions(q0, SB_TK)

                    def cond(st):
                        return jnp.logical_and(st[0] > 0, st[3] > SB_CUTOFF)

                    def step(st):
                        k_prev, c, acc, _ = st
                        k0 = _sb_next_key(k_prev)
                        part, c = block(qh, k0, SB_TK, _sb_keep(k0 + col < k_prev), c)
                        return k0, c, acc + part, jnp.max(c)

                    st = lax.while_loop(cond, step, (_sb_first_key(q0), c0, jnp.zeros((2 * SB_TQ, LANES), F32),
                                                     jnp.max(c0)))
                    o_ref[pl.ds(q0, SB_TQ), :] += _unstack_heads(st[2], SB_TQ)

                return carry

            _sb_scan_tiles(c_all, nq, more)

    def col_spec(j):
        return pl.BlockSpec((s, LANES), lambda p: (0, j * n_pairs + p))

    (o,), rides = _call(
        body, name=name, grid=(n_pairs,), out_shape=(jax.ShapeDtypeStruct((s, SB_WIDTH), F32),),
        in_specs=[col_spec(0), col_spec(1), col_spec(2), pl.BlockSpec((2 * LANES, LANES), lambda p: (0, 0))],
        out_specs=(pl.BlockSpec((s, LANES), lambda p: (0, p)),), scratch_shapes=[pltpu.VMEM((2 * s, LANES), F32)],
        args=(proj, proj, proj, tri2), ride=ride)
    return o, rides


def sb_attn_bwd(proj, tri2, o, do, *, name, ride=()):
    s = proj.shape[0]
    nq = s // SB_TQ
    n_pairs = SB_WIDTH // LANES

    def body(q_ref, k_ref, v_ref, tri_ref, o_ref, do_ref, dq_ref, dk_ref, dv_ref, dq_acc, dk_acc, dv_acc, c_all, e_all):
        dk_acc[...] = jnp.zeros_like(dk_acc)
        dv_acc[...] = jnp.zeros_like(dv_acc)

        def load(q0):
            qh = _stack_heads(q_ref[pl.ds(q0, SB_TQ), :]) * SCALE
            doh_b = _stack_heads(do_ref[pl.ds(q0, SB_TQ), :])
            ov = o_ref[pl.ds(q0, SB_TQ), :]
            dd = jnp.sum(doh_b.astype(F32) * jnp.concatenate([ov, ov], axis=0), axis=1, keepdims=True)
            return qh, doh_b, dd

        def block(qh, doh_b, dd, k0, tk, keep, c, ce):
            kt = k_ref[pl.ds(k0, tk), :]
            z = _dot_nt(qh, kt)
            sp = _softplus(z)
            lb = z - sp
            tail, total = _suffix_sums(keep(-sp), tri_ref[...])
            wb = keep(jnp.exp(lb + tail + c)).astype(BF16)
            e = wb.astype(F32) * _dot_nt(doh_b, v_ref[pl.ds(k0, tk), :])
            e_tail, e_total = _suffix_sums(e, tri_ref[...])
            dzb = keep(e - jnp.exp(lb) * (dd - ce - e_tail)).astype(BF16)
            dk_acc[pl.ds(k0, tk), :] += _dot_tn(dzb, qh)
            dv_acc[pl.ds(k0, tk), :] += _dot_tn(wb, doh_b)
            return _dot(dzb, kt), c + total, ce + e_total

        def first(interior):
            def run(qb, carry):
                q0 = pl.multiple_of(qb * SB_TQ, SB_TQ)
                qh, doh_b, dd = load(q0)
                k0 = _sb_first_key(q0)
                zero = jnp.zeros((2 * SB_TQ, 1), F32)
                dq, c, ce = block(qh, doh_b, dd, k0, SB_TK1, _sb_keep_first(q0, k0, interior), zero, zero)
                dq_acc[pl.ds(q0, SB_TQ), :] = _unstack_heads(dq, SB_TQ)
                c_all[_sb_rows(q0), :] = jnp.broadcast_to(jnp.where(k0 > 0, c, NEG), (2 * SB_TQ, LANES))
                e_all[_sb_rows(q0), :] = jnp.broadcast_to(ce, (2 * SB_TQ, LANES))
                return carry

            return run

        _sb_first_pass(nq, first)

        @pl.when(jnp.max(c_all[...]) > SB_CUTOFF)
        def _():
            def more(qb, carry):
                q0 = pl.multiple_of(qb * SB_TQ, SB_TQ)
                c0 = c_all[_sb_rows(q0), 0:1]

                @pl.when(jnp.max(c0) > SB_CUTOFF)
                def _():
                    qh, doh_b, dd = load(q0)
                    _, col = _sb_positions(q0, SB_TK)

                    def cond(st):
                        return jnp.logical_and(st[0] > 0, st[4] > SB_CUTOFF)

                    def step(st):
                        k_prev, c, ce, dq, _ = st
                        k0 = _sb_next_key(k_prev)
                        part, c, ce = block(qh, doh_b, dd, k0, SB_TK, _sb_keep(k0 + col < k_prev), c, ce)
                        return k0, c, ce, dq + part, jnp.max(c)

                    st = lax.while_loop(cond, step, (_sb_first_key(q0), c0, e_all[_sb_rows(q0), 0:1],
                                                     jnp.zeros((2 * SB_TQ, LANES), F32), jnp.max(c0)))
                    dq_acc[pl.ds(q0, SB_TQ), :] += _unstack_heads(st[3], SB_TQ)

                return carry

            _sb_scan_tiles(c_all, nq, more)

        dq_ref[...] = (dq_acc[...] * SCALE).astype(BF16)
        dk_ref[...] = dk_acc[...].astype(BF16)
        dv_ref[...] = dv_acc[...].astype(BF16)

    def col_spec(j):
        return pl.BlockSpec((s, LANES), lambda p: (0, j * n_pairs + p))

    pair = pl.BlockSpec((s, LANES), lambda p: (0, p))
    (dq, dk, dv), rides = _call(
        body, name=name, grid=(n_pairs,), out_shape=(jax.ShapeDtypeStruct((s, SB_WIDTH), BF16),) * 3,
        in_specs=[col_spec(0), col_spec(1), col_spec(2), pl.BlockSpec((2 * LANES, LANES), lambda p: (0, 0)), pair, pair],
        out_specs=(pair, pair, pair),
        scratch_shapes=[pltpu.VMEM((s, LANES), F32)] * 3 + [pltpu.VMEM((2 * s, LANES), F32)] * 2,
        args=(proj, proj, proj, tri2, o, do), ride=ride)
    return dq, dk, dv, rides


def _lane_lo():
    return lax.broadcasted_iota(jnp.int32, (1, LANES), 1) < HEAD_DIM


def _swap_halves(x):
    return pltpu.roll(x, HEAD_DIM, 1)


def _rot_half(y):
    first = (lax.broadcasted_iota(jnp.int32, (1, LANES), 1) % HEAD_DIM) < (HEAD_DIM // 2)
    return jnp.where(first, pltpu.roll(y, LANES - HEAD_DIM // 2, 1), pltpu.roll(y, HEAD_DIM // 2, 1))


def _head_mean(v, avg):
    hi, lo = _split_bf16(v)
    return _dot(hi, avg) + _dot(lo, avg)


def _head_avg_matrix():
    lane = jnp.arange(LANES) // HEAD_DIM
    return ((lane[:, None] == lane[None, :]).astype(F32) * (1.0 / HEAD_DIM)).astype(BF16)


def swa_prep_fwd(proj, cos_p, sin_p, gq, gk, *, name):
    s = proj.shape[0]
    tm = min(512, s)
    q_blk = (3 * SB_WIDTH) // SWA_Q_WIDTH
    k_blk = (3 * SB_WIDTH + SWA_Q_WIDTH) // LANES

    def body(q_ref, k_ref, cos_ref, sin_ref, gq_ref, gk_ref, avg_ref, qn_ref, kn_ref):
        cosv, sinv, avg = cos_ref[...], sin_ref[...], avg_ref[...]

        def norm_rope(xv, g):
            y = (xv * lax.rsqrt(_head_mean(xv * xv, avg) + NORM_EPS)) * g
            return y * cosv + _rot_half(y) * sinv

        for j in range(SWA_Q_WIDTH // LANES):
            sl = slice(j * LANES, (j + 1) * LANES)
            qn_ref[:, sl] = norm_rope(q_ref[:, sl].astype(F32), gq_ref[...]).astype(BF16)
        kn_ref[...] = norm_rope(k_ref[...].astype(F32), gk_ref[...]).astype(BF16)

    row = lambda i: (i, 0)
    fixed = lambda i: (0, 0)
    return pl.pallas_call(
        body, name=name, grid=(s // tm,),
        out_shape=(jax.ShapeDtypeStruct((s, SWA_Q_WIDTH), BF16), jax.ShapeDtypeStruct((s, LANES), BF16)),
        in_specs=[pl.BlockSpec((tm, SWA_Q_WIDTH), lambda i: (i, q_blk)), pl.BlockSpec((tm, LANES), lambda i: (i, k_blk)),
                  pl.BlockSpec((tm, LANES), row), pl.BlockSpec((tm, LANES), row),
                  pl.BlockSpec((1, LANES), fixed), pl.BlockSpec((1, LANES), fixed), pl.BlockSpec((LANES, LANES), fixed)],
        out_specs=(pl.BlockSpec((tm, SWA_Q_WIDTH), row), pl.BlockSpec((tm, LANES), row)),
        compiler_params=_params("parallel"),
    )(proj, proj, cos_p, sin_p, gq, gk, _head_avg_matrix())


def swa_prep_bwd(proj, cos_p, sin_p, gq, gk, dqn, dkn, dv, *, name):
    s = proj.shape[0]
    tm = min(512, s)
    q_blk = (3 * SB_WIDTH) // SWA_Q_WIDTH
    k_blk = (3 * SB_WIDTH + SWA_Q_WIDTH) // LANES

    def body(q_ref, k_ref, cos_ref, sin_ref, gq_ref, gk_ref, avg_ref, dqn_ref, dkn_ref, dv_ref, dq_ref, dk_ref, dvb_ref,
             dgq_ref, dgk_ref):
        @pl.when(pl.program_id(0) == 0)
        def _():
            dgq_ref[...] = jnp.zeros_like(dgq_ref)
            dgk_ref[...] = jnp.zeros_like(dgk_ref)

        cosv, sinv, avg = cos_ref[...], sin_ref[...], avg_ref[...]

        def bwd(xv, g, dout):
            dy = dout * cosv + _rot_half(dout * sinv)
            r = lax.rsqrt(_head_mean(xv * xv, avg) + NORM_EPS)
            dyg = dy * g
            dx = r * dyg - xv * ((r * r * r) * _head_mean(dyg * xv, avg))
            return dx, jnp.sum(dy * (xv * r), axis=0, keepdims=True)

        for j in range(SWA_Q_WIDTH // LANES):
            sl = slice(j * LANES, (j + 1) * LANES)
            dx, dg = bwd(q_ref[:, sl].astype(F32), gq_ref[...], dqn_ref[:, sl])
            dq_ref[:, sl] = dx.astype(BF16)
            dgq_ref[:, sl] += dg
        dx, dg = bwd(k_ref[...].astype(F32), gk_ref[...], dkn_ref[...])
        dk_ref[...] = dx.astype(BF16)
        dgk_ref[...] += dg
        dvb_ref[...] = dv_ref[...].astype(BF16)

    row = lambda i: (i, 0)
    fixed = lambda i: (0, 0)
    lane_row = pl.BlockSpec((tm, LANES), row)
    return pl.pallas_call(
        body, name=name, grid=(s // tm,),
        out_shape=(jax.ShapeDtypeStruct((s, SWA_Q_WIDTH), BF16), jax.ShapeDtypeStruct((s, LANES), BF16),
                   jax.ShapeDtypeStruct((s, LANES), BF16),
                   jax.ShapeDtypeStruct((1, SWA_Q_WIDTH), F32), jax.ShapeDtypeStruct((1, LANES), F32)),
        in_specs=[pl.BlockSpec((tm, SWA_Q_WIDTH), lambda i: (i, q_blk)), pl.BlockSpec((tm, LANES), lambda i: (i, k_blk)),
                  lane_row, lane_row, pl.BlockSpec((1, LANES), fixed), pl.BlockSpec((1, LANES), fixed),
                  pl.BlockSpec((LANES, LANES), fixed), pl.BlockSpec((tm, SWA_Q_WIDTH), row), lane_row, lane_row],
        out_specs=(pl.BlockSpec((tm, SWA_Q_WIDTH), row), lane_row, lane_row,
                   pl.BlockSpec((1, SWA_Q_WIDTH), fixed), pl.BlockSpec((1, LANES), fixed)),
        compiler_params=_params("arbitrary"),
    )(proj, proj, cos_p, sin_p, gq, gk, _head_avg_matrix(), dqn, dkn, dv)


def _swa_kv_copies(k_ref, v_ref, kg_ref, vg_ref, second_kv):
    s = k_ref.shape[0]
    rows = min(512, s)
    keep = jnp.logical_xor(_lane_lo(), second_kv)

    def chunk(r, carry):
        sl = pl.ds(pl.multiple_of(r * rows, rows), rows)
        for src, dst in ((k_ref, kg_ref), (v_ref, vg_ref)):
            f = src[sl, :].astype(F32)
            dst[sl, :] = jnp.where(keep, f, _swap_halves(f)).astype(BF16)
        return carry

    lax.fori_loop(0, s // rows, chunk, 0)


def _swa_tile(i, kg_ref, vg_ref):
    q0 = pl.multiple_of(i * SWA_TQ, SWA_TQ)
    k0 = pl.multiple_of(jnp.maximum(i - 1, 0) * SWA_TQ, SWA_TQ)
    kg = kg_ref[pl.ds(k0, SWA_TK), :]
    vg = vg_ref[pl.ds(k0, SWA_TK), :]
    row = lax.broadcasted_iota(jnp.int32, (2 * SWA_TQ, SWA_TK), 0)
    tpos = q0 + jnp.where(row >= SWA_TQ, row - SWA_TQ, row)
    spos = k0 + lax.broadcasted_iota(jnp.int32, (2 * SWA_TQ, SWA_TK), 1)
    valid = jnp.logical_and(spos <= tpos, spos > tpos - WINDOW)
    return q0, k0, kg, vg, valid


def _swa_probs(qh, kg, valid, sink):
    z = jnp.where(valid, _dot_nt(qh, kg) * SCALE, NEG)
    m = jnp.maximum(jnp.max(z, axis=1, keepdims=True), sink)
    pexp = jnp.exp(z - m)
    psink = jnp.exp(sink - m)
    inv = 1.0 / (jnp.sum(pexp, axis=1, keepdims=True) + psink)
    return pexp * inv, psink * inv


def _stacked_sink(sink_row):
    s0 = jnp.sum(jnp.where(_head_mask(0), sink_row, 0.0), axis=1, keepdims=True) * (1.0 / HEAD_DIM)
    s1 = jnp.sum(jnp.where(_head_mask(1), sink_row, 0.0), axis=1, keepdims=True) * (1.0 / HEAD_DIM)
    top = lax.broadcasted_iota(jnp.int32, (2 * SWA_TQ, 1), 0) < SWA_TQ
    return jnp.where(top, s0, s1)


def swa_attn_fwd(qn, kn, proj, sink_p, *, name, ride=()):
    s = qn.shape[0]
    nq = s // SWA_TQ
    n_pairs = SWA_Q_WIDTH // LANES
    v_blk = (3 * SB_WIDTH + SWA_Q_WIDTH + SWA_KV_WIDTH) // LANES

    def body(q_ref, k_ref, v_ref, s_ref, o_ref, kg_ref, vg_ref):
        _swa_kv_copies(k_ref, v_ref, kg_ref, vg_ref, (pl.program_id(0) // 2) == 1)
        sink = _stacked_sink(s_ref[...])

        def tile(i, carry):
            q0, _, kg, vg, valid = _swa_tile(i, kg_ref, vg_ref)
            probs, _ = _swa_probs(_stack_heads(q_ref[pl.ds(q0, SWA_TQ), :]), kg, valid, sink)
            o_ref[pl.ds(q0, SWA_TQ), :] = _unstack_heads(_dot(probs.astype(BF16), vg), SWA_TQ)
            return carry

        lax.fori_loop(0, nq, tile, 0, unroll=4)

    pair = pl.BlockSpec((s, LANES), lambda p: (0, p))
    whole = pl.BlockSpec((s, LANES), lambda p: (0, 0))
    (o,), rides = _call(
        body, name=name, grid=(n_pairs,), out_shape=(jax.ShapeDtypeStruct((s, SWA_Q_WIDTH), F32),),
        in_specs=[pair, whole, pl.BlockSpec((s, LANES), lambda p: (0, v_blk)),
                  pl.BlockSpec((None, 1, LANES), lambda p: (p, 0, 0))],
        out_specs=(pair,), scratch_shapes=[pltpu.VMEM((s, LANES), BF16)] * 2, args=(qn, kn, proj, sink_p), ride=ride)
    return o, rides


def swa_attn_bwd(qn, kn, proj, sink_p, o, do, *, name, ride=()):
    s = qn.shape[0]
    nq = s // SWA_TQ
    n_pairs = SWA_Q_WIDTH // LANES
    v_blk = (3 * SB_WIDTH + SWA_Q_WIDTH + SWA_KV_WIDTH) // LANES
    fold_rows = min(512, s)

    def body(q_ref, k_ref, v_ref, s_ref, o_ref, do_ref, dq_ref, dk_ref, dv_ref, ds_ref, acc_k, acc_v, kg_ref, vg_ref):
        p = pl.program_id(0)
        _swa_kv_copies(k_ref, v_ref, kg_ref, vg_ref, (p // 2) == 1)
        sink = _stacked_sink(s_ref[...])

        @pl.when(p % 2 == 0)
        def _():
            acc_k[...] = jnp.zeros_like(acc_k)
            acc_v[...] = jnp.zeros_like(acc_v)

        ds_ref[...] = jnp.zeros_like(ds_ref)

        def tile(i, carry):
            q0, k0, kg, vg, valid = _swa_tile(i, kg_ref, vg_ref)
            qh = _stack_heads(q_ref[pl.ds(q0, SWA_TQ), :])
            doh_b = _stack_heads(do_ref[pl.ds(q0, SWA_TQ), :])
            ov = o_ref[pl.ds(q0, SWA_TQ), :]
            delta = jnp.sum(doh_b.astype(F32) * jnp.concatenate([ov, ov], axis=0), axis=1, keepdims=True)
            probs, psink = _swa_probs(qh, kg, valid, sink)
            dz = probs * (_dot_nt(doh_b, vg) - delta)
            dzb = (dz * SCALE).astype(BF16)
            dq_ref[pl.ds(q0, SWA_TQ), :] = _unstack_heads(_dot(dzb, kg), SWA_TQ)
            acc_k[pl.ds(k0, SWA_TK), :] += _dot_tn(dzb, qh)
            acc_v[pl.ds(k0, SWA_TK), :] += _dot_tn(probs.astype(BF16), doh_b)
            pd = psink * delta
            ds_ref[...] -= jnp.where(_head_mask(0), jnp.sum(pd[:SWA_TQ], axis=0, keepdims=True),
                                     jnp.sum(pd[SWA_TQ:], axis=0, keepdims=True))
            return carry

        lax.fori_loop(0, nq, tile, 0, unroll=4)

        def fold_into(first_head):
            def fold(r, carry):
                rows = pl.ds(pl.multiple_of(r * fold_rows, fold_rows), fold_rows)
                for acc, out in ((acc_k, dk_ref), (acc_v, dv_ref)):
                    a = acc[rows, :]
                    both = a + _swap_halves(a)
                    if first_head:
                        out[rows, :] = jnp.where(_lane_lo(), both, 0.0)
                    else:
                        out[rows, :] = jnp.where(_lane_lo(), out[rows, :], both)
                return carry

            lax.fori_loop(0, s // fold_rows, fold, 0)

        @pl.when(p == 1)
        def _():
            fold_into(True)

        @pl.when(p == 3)
        def _():
            fold_into(False)

    pair = pl.BlockSpec((s, LANES), lambda p: (0, p))
    whole = pl.BlockSpec((s, LANES), lambda p: (0, 0))
    sink_spec = pl.BlockSpec((None, 1, LANES), lambda p: (p, 0, 0))
    (dq, dk, dv, dsink), rides = _call(
        body, name=name, grid=(n_pairs,),
        out_shape=(jax.ShapeDtypeStruct((s, SWA_Q_WIDTH), F32), jax.ShapeDtypeStruct((s, LANES), F32),
                   jax.ShapeDtypeStruct((s, LANES), F32), jax.ShapeDtypeStruct((n_pairs, 1, LANES), F32)),
        in_specs=[pair, whole, pl.BlockSpec((s, LANES), lambda p: (0, v_blk)), sink_spec, pair, pair],
        out_specs=(pair, whole, whole, sink_spec),
        scratch_shapes=[pltpu.VMEM((s, LANES), F32)] * 2 + [pltpu.VMEM((s, LANES), BF16)] * 2,
        args=(qn, kn, proj, sink_p, o, do), ride=ride)
    return dq, dk, dv, dsink, rides


def _rope_tables(s):
    inv_freq = 1.0 / (ROPE_THETA ** (jnp.arange(0, HEAD_DIM, 2, dtype=F32) / HEAD_DIM))
    ang = jnp.arange(s, dtype=F32)[:, None] * inv_freq[None, :]
    cos, sin = jnp.cos(ang), jnp.sin(ang)
    cos_p = jnp.tile(jnp.concatenate([cos, cos], axis=1), (1, LANES // HEAD_DIM))
    sin_p = jnp.tile(jnp.concatenate([-sin, sin], axis=1), (1, LANES // HEAD_DIM))
    return cos_p, sin_p


def _lane_tile(v, reps):
    return jnp.tile(v.reshape(1, -1), (1, reps))


def _natural(stack, w):
    n, r, c = stack.shape
    if MATRIX_NAMES[w] in ROW_SHARDED or w == W_IN:
        return stack.reshape(n * r, c)
    if w == W_UP:
        return stack
    return jnp.transpose(stack, (1, 0, 2)).reshape(r, n * c)


def _pack_small(tree):
    flat = jnp.concatenate([tree[n].reshape(-1) for n in SMALL_NAMES])
    rows = -(-flat.shape[0] // (8 * LANES)) * 8
    return jnp.pad(flat, (0, rows * LANES - flat.shape[0])).reshape(rows, LANES)


def _unpack_small(packed, shapes):
    flat, out, off = packed.reshape(-1), {}, 0
    for n in SMALL_NAMES:
        size = shapes[n][0] * shapes[n][1]
        out[n] = flat[off:off + size].reshape(shapes[n])
        off += size
    return out


def train_step(x, target, weights, mom_m, mom_v):
    s = x.shape[0]
    cos_p, sin_p = _rope_tables(s)
    tri = (jnp.arange(LANES)[:, None] > jnp.arange(LANES)[None, :]).astype(BF16)
    tri = jnp.concatenate([tri, tri], axis=0)
    local = {n: (jnp.swapaxes(t, 1, 2) if n == "w_in" else t) for n, t in weights.items()}
    local_m = {n: (jnp.swapaxes(t, 1, 2) if n == "w_in" else t) for n, t in mom_m.items()}
    local_v = {n: (jnp.swapaxes(t, 1, 2) if n == "w_in" else t) for n, t in mom_v.items()}
    shards = [[local[n][l].astype(BF16) for n in MATRIX_NAMES] for l in range(DEPTH)]
    core = lax.axis_index("c").astype(jnp.int32).reshape(1)
    chip = (2 * lax.axis_index("x") + lax.axis_index("y")).astype(jnp.int32).reshape(1)

    def gather(l, ws, rows=None, stacks=None):
        return GatherJob([shards[l][w] for w in ws], rows=rows, stacks=stacks)

    def halves(w):
        r = shards[0][w].shape[0] // 2
        return (0, r), (r, r)

    w_in = _natural(exchange_alone(gather(0, [W_IN]), name="gather_w_in0")[0], W_IN)
    saved = []
    for l in range(DEPTH):
        g_mix = weights["mix_norm_g"][l].reshape(1, D_MODEL)
        g_mlp = weights["mlp_norm_g"][l].reshape(1, D_MODEL)
        gq = _lane_tile(weights["q_norm_g"][l], LANES // HEAD_DIM)
        gk = _lane_tile(weights["k_norm_g"][l], LANES // HEAD_DIM)
        sink_p = jnp.repeat(weights["sinks"][l].reshape(SWA_Q_WIDTH // LANES, 2), HEAD_DIM, axis=1)
        sink_p = sink_p.reshape(SWA_Q_WIDTH // LANES, 1, LANES)
        (h, proj, gates), ((s_bsb, s_bsw, s_out),) = norm_matmul(
            x, g_mix, w_in, gate_split=ATTN_WIDTH, name="in_proj", ride=[gather(l, [W_BSB, W_BSW, W_OUT])])
        o_sb, ((s_up,),) = sb_attn_fwd(proj, tri, name="sb_fwd", ride=[gather(l, [W_UP])])
        qn, kn = swa_prep_fwd(proj, cos_p, sin_p, gq, gk, name="swa_prep")
        o_sw, ((s_down,),) = swa_attn_fwd(qn, kn, proj, sink_p, name="swa_fwd",
                                          ride=[gather(l, [W_DOWN], rows=halves(W_DOWN)[0])])
        more = l + 1 < DEPTH
        (x1, y_sb, y_sw, merged), rides = merge_out_fwd(
            x, o_sb, o_sw, gates, _natural(s_bsb, W_BSB), _natural(s_bsw, W_BSW), _natural(s_out, W_OUT),
            name="merge_out" if more else "merge_out_last",
            ride=[gather(l + 1, [W_IN], rows=halves(W_IN)[0])] if more else [])
        (h2, u), ((s_down,),) = norm_matmul(x1, g_mlp, s_up, gate_split=None, name="mlp_up",
                                            ride=[gather(l, [W_DOWN], rows=halves(W_DOWN)[1], stacks=[s_down])])
        mats = [w_in, _natural(s_bsb, W_BSB), _natural(s_bsw, W_BSW), _natural(s_out, W_OUT), s_up,
                _natural(s_down, W_DOWN)]
        if more:
            (x2,), ((s_in,),) = mlp_down_fwd(x1, u, mats[W_DOWN], name="mlp_down",
                                             ride=[gather(l + 1, [W_IN], rows=halves(W_IN)[1], stacks=rides[0])])
            w_in = _natural(s_in, W_IN)
        else:
            (x2,), _ = mlp_down_fwd(x1, u, mats[W_DOWN], name="mlp_down_last")
        saved.append(dict(x=x, h=h, proj=proj, gates=gates, o_sb=o_sb, qn=qn, kn=kn, o_sw=o_sw, y_sb=y_sb, y_sw=y_sw,
                          merged=merged, x1=x1, h2=h2, u=u, g_mix=g_mix, g_mlp=g_mlp, gq=gq, gk=gk, sink_p=sink_p,
                          mats=mats))
        x = x2

    dx, dxb, loss = loss_head(x, target, name="loss_head")

    shard_shapes = [local[n].shape[1:] for n in MATRIX_NAMES]
    parts = [lax.empty((DEPTH, N_CHIPS) + sh, BF16) for sh in shard_shapes]
    lands = [lax.empty((DEPTH, 3) + sh, BF16) for sh in shard_shapes]
    small_grads = {n: [None] * DEPTH for n in SMALL_NAMES}
    half = D_MODEL // 2

    def summed(l, ws, grads, landed):
        new = pair_sum(l, grads, landed, [parts[w] for w in ws], core, name="grad_pair_sum")
        for w, p in zip(ws, new):
            parts[w] = p

    def chip_job(items):
        return ChipJob(items, parts, lands)

    def landed_chip(job, outs):
        for w, a in zip(job.ws, outs):
            lands[w] = a

    in_pending = None
    for l in reversed(range(DEPTH)):
        a = saved[l]
        mats = a["mats"]
        in_jobs = [chip_job([(in_pending, W_IN, rows)]) for rows in halves(W_IN)] if in_pending is not None else []
        (du,), rides = mlp_bwd_up(dxb, a["u"], mats[W_DOWN], name="mlp_bwd_up" if in_jobs else "mlp_bwd_up_first",
                                  ride=in_jobs[:1])
        if in_jobs:
            landed_chip(in_jobs[0], rides[0])
            in_jobs[1] = chip_job([(in_pending, W_IN, halves(W_IN)[1])])
        dw_down = matmul_tn(a["u"], [dxb], a_block=half, out_cols=None, relu2=True, name="dw_down")
        dw_up = matmul_tn(a["h2"], [du], a_block=half, out_cols=du.shape[1] // N_DEV, relu2=False, name="dw_up")
        g_mlp_w = [dw_up, dw_down.reshape((N_DEV,) + shard_shapes[W_DOWN])]
        (dx1, dx1b, dg_mlp), rides = matmul_nt_norm_bwd(
            [du], mats[W_UP], a["x1"], a["g_mlp"], dx, name="mlp_bwd_norm" if in_jobs else "mlp_bwd_norm_first",
            ride=[PairJob(g_mlp_w)] + in_jobs[1:])
        if in_jobs:
            landed_chip(in_jobs[1], rides[1])
        summed(l, [W_UP, W_DOWN], g_mlp_w, rides[0])
        small_grads["mlp_norm_g"][l] = dg_mlp.reshape(D_MODEL)
        dw_out = matmul_tn(a["merged"], [dx1b], a_block=half, out_cols=None, relu2=False, name="dw_out")
        dy_sb, dy_sw, do_sb, do_sw, dgl = out_bwd(dx1b, mats[W_OUT], a["gates"], a["y_sb"], a["y_sw"],
                                                  mats[W_BSB], mats[W_BSW], name="out_bwd")
        dw_bsb = matmul_tn(a["o_sb"], [dy_sb], a_block=half, out_cols=D_MODEL // N_DEV, relu2=False, name="dw_branch_sb")
        dw_bsw = matmul_tn(a["o_sw"], [dy_sw], a_block=half, out_cols=D_MODEL // N_DEV, relu2=False, name="dw_branch_swa")
        g_mix_w = [dw_bsb, dw_bsw, dw_out.reshape((N_DEV,) + shard_shapes[W_OUT])]
        job = chip_job([(l, W_UP), (l, W_DOWN)])
        dq_sb, dk_sb, dv_sb, (outs, landed) = sb_attn_bwd(a["proj"], tri, a["o_sb"], do_sb, name="sb_bwd",
                                                         ride=[job, PairJob(g_mix_w)])
        landed_chip(job, outs)
        summed(l, [W_BSB, W_BSW, W_OUT], g_mix_w, landed)
        job = chip_job([(l, W_BSB), (l, W_BSW), (l, W_OUT)])
        dqn, dkn, dv_sw, dsink, (outs,) = swa_attn_bwd(a["qn"], a["kn"], a["proj"], a["sink_p"], a["o_sw"], do_sw,
                                                      name="swa_bwd", ride=[job])
        landed_chip(job, outs)
        dq_sw, dk_sw, dv_swb, dgq, dgk = swa_prep_bwd(a["proj"], cos_p, sin_p, a["gq"], a["gk"], dqn, dkn, dv_sw,
                                                      name="swa_prep_bwd")
        small_grads["q_norm_g"][l] = dgq.reshape(SWA_Q_WIDTH // HEAD_DIM, HEAD_DIM).sum(0)
        small_grads["k_norm_g"][l] = dgk.reshape(LANES // HEAD_DIM, HEAD_DIM).sum(0)
        small_grads["sinks"][l] = dsink[:, 0, ::HEAD_DIM].reshape(SWA_Q_WIDTH // HEAD_DIM)
        pieces = [dq_sb, dk_sb, dv_sb, dq_sw, dk_sw, dv_swb, dgl]
        g_in = [matmul_tn_row_blocks(pieces, a["h"], n_blocks=N_DEV, name="dw_in")]
        if l > 0:
            (dx, dxb, dg_mix), (landed,) = matmul_nt_norm_bwd(pieces, mats[W_IN], a["x"], a["g_mix"], dx1,
                                                             name="in_proj_bwd", ride=[PairJob(g_in)])
            summed(l, [W_IN], g_in, landed)
            in_pending = l
        else:
            summed(l, [W_IN], g_in, exchange_alone(PairJob(g_in), name="grad_pair_exchange_in0"))
            job = chip_job([(l, W_IN)])
            (dx, dxb, dg_mix), (outs,) = matmul_nt_norm_bwd(pieces, mats[W_IN], a["x"], a["g_mix"], dx1,
                                                           name="in_proj_bwd_last", ride=[job])
            landed_chip(job, outs)
        small_grads["mix_norm_g"][l] = dg_mix.reshape(D_MODEL)

    out_g, out_d, out_m, out_v = {}, {}, {}, {}
    for i, n in enumerate(MATRIX_NAMES):
        outs = reduce_adamw(parts[i], lands[i], chip, local[n], local_m[n], local_v[n], name="adamw_" + n)
        if n == "w_in":
            outs = [jnp.swapaxes(t, 1, 2) for t in outs]
        out_g[n], out_d[n], out_m[n], out_v[n] = outs
    small_shapes = {n: weights[n].shape for n in SMALL_NAMES}
    small_all = gather_small(_pack_small({n: jnp.stack(v) for n, v in small_grads.items()}), name="gather_small_grads")
    sg, sd, sm, sv = small_adamw(small_all, _pack_small(weights), _pack_small(mom_m), _pack_small(mom_v),
                                 name="small_adamw")
    for tree, packed_small in ((out_g, sg), (out_d, sd), (out_m, sm), (out_v, sv)):
        tree.update(_unpack_small(packed_small, small_shapes))
    return loss, dx, (out_g, out_d, out_m, out_v)


def kernel(x, mix_norm_g, w_in, q_norm_g, k_norm_g, sinks, w_branch_sb, w_branch_swa, w_out, mlp_norm_g, w_up, w_down, loss_target, m_mix_norm_g, m_w_in, m_q_norm_g, m_k_norm_g, m_sinks, m_w_branch_sb, m_w_branch_swa, m_w_out, m_mlp_norm_g, m_w_up, m_w_down, v_mix_norm_g, v_w_in, v_q_norm_g, v_k_norm_g, v_sinks, v_w_branch_sb, v_w_branch_swa, v_w_out, v_mlp_norm_g, v_w_up, v_w_down):
    weights = dict(mix_norm_g=mix_norm_g, w_in=w_in, q_norm_g=q_norm_g, k_norm_g=k_norm_g, sinks=sinks,
                   w_branch_sb=w_branch_sb, w_branch_swa=w_branch_swa, w_out=w_out, mlp_norm_g=mlp_norm_g, w_up=w_up,
                   w_down=w_down)
    mom_m = dict(mix_norm_g=m_mix_norm_g, w_in=m_w_in, q_norm_g=m_q_norm_g, k_norm_g=m_k_norm_g, sinks=m_sinks,
                 w_branch_sb=m_w_branch_sb, w_branch_swa=m_w_branch_swa, w_out=m_w_out, mlp_norm_g=m_mlp_norm_g,
                 w_up=m_w_up, w_down=m_w_down)
    mom_v = dict(mix_norm_g=v_mix_norm_g, w_in=v_w_in, q_norm_g=v_q_norm_g, k_norm_g=v_k_norm_g, sinks=v_sinks,
                 w_branch_sb=v_w_branch_sb, w_branch_swa=v_w_branch_swa, w_out=v_w_out, mlp_norm_g=v_mlp_norm_g,
                 w_up=v_w_up, w_down=v_w_down)
    loss_part, grad_x, outs = train_step(x[0], loss_target[0], weights, mom_m, mom_v)
    loss = lax.psum(loss_part[0, 0], MESH_AXES)
    return (loss, grad_x[None], *[outs[0][n] for n in WEIGHT_ORDER], *[outs[1][n] for n in WEIGHT_ORDER],
            *[outs[2][n] for n in WEIGHT_ORDER], *[outs[3][n] for n in WEIGHT_ORDER])
```

```python
import math

import jax
import jax.numpy as jnp
from jax import lax
from jax.experimental import pallas as pl
from jax.experimental.pallas import tpu as pltpu

F32 = jnp.float32
BF16 = jnp.bfloat16

DEPTH = 4
D_MODEL = 1024
HEAD_DIM = 64
LANES = 128
WINDOW = 128
SB_WIDTH = 512
SWA_Q_WIDTH = 512
SWA_KV_WIDTH = 128
ATTN_WIDTH = 3 * SB_WIDTH + SWA_Q_WIDTH + 2 * SWA_KV_WIDTH
ROPE_THETA = 10000.0
NORM_EPS = 1e-6
SCALE = HEAD_DIM ** -0.5
NEG = -1e30
N_DEV = 8
N_CHIPS = 4

ADAM_LR = 0.001
ADAM_B1 = 0.9
ADAM_B2 = 0.999
ADAM_EPS = 1e-08
ADAM_WD = 0.01
ADAM_STEP = 10

SB_TQ = 128
SB_TK1 = 384
SB_TK = 256
SB_CUTOFF = -88.0
SWA_TQ = 128
SWA_TK = 256
ROW_TILE = 512
VMEM_LIMIT = 56 * 1024 * 1024

MATRIX_NAMES = ("w_in", "w_branch_sb", "w_branch_swa", "w_out", "w_up", "w_down")
W_IN, W_BSB, W_BSW, W_OUT, W_UP, W_DOWN = range(6)
ROW_SHARDED = ("w_out", "w_down")
SMALL_NAMES = ("mix_norm_g", "q_norm_g", "k_norm_g", "sinks", "mlp_norm_g")
WEIGHT_ORDER = ("mix_norm_g", "w_in", "q_norm_g", "k_norm_g", "sinks", "w_branch_sb", "w_branch_swa", "w_out",
                "mlp_norm_g", "w_up", "w_down")
MESH_AXES = ("x", "y", "c")

ANY = pl.BlockSpec(memory_space=pl.ANY)
MESH = pl.DeviceIdType.MESH


def _params(*sem):
    return pltpu.CompilerParams(dimension_semantics=sem, vmem_limit_bytes=VMEM_LIMIT)


def _dot(a, b):
    return jnp.dot(a, b, preferred_element_type=F32)


def _dot_nt(a, b):
    return lax.dot_general(a, b, (((1,), (1,)), ((), ())), preferred_element_type=F32)


def _dot_tn(a, b):
    return lax.dot_general(a, b, (((0,), (0,)), ((), ())), preferred_element_type=F32)


def _split_bf16(x):
    hi = lax.bitcast_convert_type(lax.bitcast_convert_type(x, jnp.uint32) & jnp.uint32(0xFFFF0000), F32)
    return hi.astype(BF16), (x - hi).astype(BF16)


def _rsqrt_ms(x):
    return lax.rsqrt(jnp.mean(x * x, axis=-1, keepdims=True) + NORM_EPS)


def _place():
    return lax.axis_index("x"), lax.axis_index("y"), lax.axis_index("c")


class _Gather:
    def __init__(self, x_refs, out_refs, send_sems, recv_sems, local_sems, rows=None):
        self.x_refs, self.out_refs = x_refs, out_refs
        self.send_sems, self.recv_sems, self.local_sems = send_sems, recv_sems, local_sems
        self.n = len(x_refs)
        self.rows = rows
        x, y, c = _place()
        self.c = c
        self.me, self.sibling = (x, y, c), (x, y, 1 - c)
        self.chips = [(1 - x, y), (x, 1 - y), (1 - x, 1 - y)]

    def _part(self, ref):
        return ref if self.rows is None else ref.at[pl.ds(*self.rows)]

    def _slot(self, w, blk):
        return self._part(self.out_refs[w].at[4 * blk[0] + 2 * blk[1] + blk[2]])

    def _copy(self, k, w, blk, to, own=False):
        dst = self._slot(w, blk)
        return pltpu.make_async_remote_copy(
            src_ref=self._part(self.x_refs[w]) if own else dst, dst_ref=dst, send_sem=self.send_sems.at[k, w],
            recv_sem=self.recv_sems.at[k, w], device_id=to, device_id_type=MESH)

    def _mine(self, w):
        return pltpu.make_async_copy(self._part(self.x_refs[w]), self._slot(w, self.me), self.local_sems.at[w])

    def _first(self, w):
        return [self._copy(0, w, self.me, self.sibling, own=True)] + [
            self._copy(1 + j, w, self.me, (*chip, self.c), own=True) for j, chip in enumerate(self.chips)]

    def _passed(self, j, w):
        return self._copy(4 + j, w, (*self.chips[j], self.c), self.sibling)

    def start(self):
        for w in range(self.n):
            self._mine(w).start()
            for cp in self._first(w):
                cp.start()

    def relay(self):
        for j, chip in enumerate(self.chips):
            for w in range(self.n):
                self._copy(1 + j, w, (*chip, self.c), self.me).wait_recv()
                self._passed(j, w).start()

    def finish(self):
        for w in range(self.n):
            self._copy(0, w, self.sibling, self.me).wait_recv()
            for j, chip in enumerate(self.chips):
                self._copy(4 + j, w, (*chip, 1 - self.c), self.me).wait_recv()
            for cp in self._first(w):
                cp.wait_send()
            for j in range(3):
                self._passed(j, w).wait_send()
            self._mine(w).wait()


class GatherJob:
    def __init__(self, shards, rows=None, stacks=None):
        n = len(shards)
        self.n, self.rows = n, rows
        self.inputs = list(shards) + (list(stacks) if stacks is not None else [])
        self.out_shapes = [jax.ShapeDtypeStruct((N_DEV,) + s.shape, s.dtype) for s in shards]
        self.aliases = {n + i: i for i in range(n)} if stacks is not None else {}
        self.scratch = [pltpu.SemaphoreType.DMA((7, n)), pltpu.SemaphoreType.DMA((7, n)),
                        pltpu.SemaphoreType.DMA((n,))]

    def bind(self, in_refs, out_refs, scratch_refs):
        return _Gather(in_refs[:self.n], out_refs, *scratch_refs, rows=self.rows)


class _Copies:
    def __init__(self, copies):
        self.copies = copies

    def start(self):
        for cp in self.copies:
            cp.start()

    def relay(self):
        pass

    def finish(self):
        for cp in self.copies:
            cp.wait_recv()
        for cp in self.copies:
            cp.wait_send()


class ChipJob:
    def __init__(self, items, parts, lands):
        self.ws = sorted({item[1] for item in items})
        n = len(self.ws)
        self.items = [(item[0], self.ws.index(item[1]), item[2] if len(item) > 2 else None) for item in items]
        self.inputs = [parts[w] for w in self.ws] + [lands[w] for w in self.ws]
        self.out_shapes = [jax.ShapeDtypeStruct(lands[w].shape, lands[w].dtype) for w in self.ws]
        self.aliases = {n + i: i for i in range(n)}
        self.scratch = [pltpu.SemaphoreType.DMA((3, n)), pltpu.SemaphoreType.DMA((3, n))]

    def bind(self, in_refs, out_refs, scratch_refs):
        send_sems, recv_sems = scratch_refs
        x, y, c = _place()
        chips = [(1 - x, y), (x, 1 - y), (1 - x, 1 - y)]

        def part(ref, rows):
            return ref if rows is None else ref.at[pl.ds(*rows)]

        return _Copies([pltpu.make_async_remote_copy(
            src_ref=part(in_refs[i].at[layer, 2 * px + py], rows), dst_ref=part(out_refs[i].at[layer, j], rows),
            send_sem=send_sems.at[j, i], recv_sem=recv_sems.at[j, i], device_id=(px, py, c), device_id_type=MESH)
            for layer, i, rows in self.items for j, (px, py) in enumerate(chips)])


class PairJob:
    def __init__(self, grads):
        n = len(grads)
        self.inputs = list(grads)
        self.out_shapes = [jax.ShapeDtypeStruct((N_CHIPS,) + g.shape[1:], g.dtype) for g in grads]
        self.aliases = {}
        self.scratch = [pltpu.SemaphoreType.DMA((N_CHIPS, n)), pltpu.SemaphoreType.DMA((N_CHIPS, n))]

    def bind(self, in_refs, out_refs, scratch_refs):
        send_sems, recv_sems = scratch_refs
        x, y, c = _place()
        return _Copies([pltpu.make_async_remote_copy(
            src_ref=in_refs[w].at[2 * k + (1 - c)], dst_ref=out_refs[w].at[k], send_sem=send_sems.at[k, w],
            recv_sem=recv_sems.at[k, w], device_id=(x, y, 1 - c), device_id_type=MESH)
            for w in range(len(in_refs)) for k in range(N_CHIPS)])


def _call(body, *, name, grid, in_specs, out_specs, out_shape, args, scratch_shapes=(), ride=()):
    out_specs, out_shape, in_specs = tuple(out_specs), tuple(out_shape), list(in_specs)
    scratch_shapes = list(scratch_shapes)
    order = ("arbitrary",) * len(grid)
    if not ride:
        outs = pl.pallas_call(body, name=name, grid=grid, in_specs=in_specs, out_specs=out_specs, out_shape=out_shape,
                              scratch_shapes=scratch_shapes, compiler_params=_params(*order))(*args)
        return tuple(outs), []
    n_in, n_out, n_scr = len(in_specs), len(out_specs), len(scratch_shapes)
    n_steps = math.prod(grid)
    relay_early = n_steps >= 8
    relay_at = n_steps - n_steps // 4 if relay_early else n_steps - 1

    def split(refs, pos, counts):
        groups = []
        for k in counts:
            groups.append(refs[pos:pos + k])
            pos += k
        return groups, pos

    def wrapped(*refs):
        ins, pos = refs[:n_in], n_in
        job_in, pos = split(refs, pos, [len(j.inputs) for j in ride])
        outs, pos = refs[pos:pos + n_out], pos + n_out
        job_out, pos = split(refs, pos, [len(j.out_shapes) for j in ride])
        scr, pos = refs[pos:pos + n_scr], pos + n_scr
        job_scr, pos = split(refs, pos, [len(j.scratch) for j in ride])
        bound = [j.bind(i, o, s) for j, i, o, s in zip(ride, job_in, job_out, job_scr)]
        step = pl.program_id(0)
        for axis in range(1, len(grid)):
            step = step * grid[axis] + pl.program_id(axis)

        @pl.when(step == 0)
        def _():
            for b in bound:
                b.start()

        if relay_early:
            @pl.when(step == relay_at)
            def _():
                for b in bound:
                    b.relay()

        body(*ins, *outs, *scr)

        @pl.when(step == n_steps - 1)
        def _():
            if not relay_early:
                for b in bound:
                    b.relay()
            for b in bound:
                b.finish()

    aliases, in_pos, out_pos = {}, n_in, n_out
    for j in ride:
        aliases.update({in_pos + i: out_pos + o for i, o in j.aliases.items()})
        in_pos += len(j.inputs)
        out_pos += len(j.out_shapes)
    results = pl.pallas_call(
        wrapped, name=name, grid=grid, in_specs=in_specs + [ANY] * (in_pos - n_in),
        out_specs=out_specs + (ANY,) * (out_pos - n_out),
        out_shape=out_shape + tuple(s for j in ride for s in j.out_shapes),
        scratch_shapes=scratch_shapes + [s for j in ride for s in j.scratch], input_output_aliases=aliases,
        compiler_params=pltpu.CompilerParams(dimension_semantics=order, vmem_limit_bytes=VMEM_LIMIT,
                                             has_side_effects=True),
    )(*args, *[a for j in ride for a in j.inputs])
    job_results, pos = split(list(results), n_out, [len(j.out_shapes) for j in ride])
    return tuple(results[:n_out]), job_results


def exchange_alone(job, *, name):
    n_in, n_out = len(job.inputs), len(job.out_shapes)

    def body(*refs):
        b = job.bind(refs[:n_in], refs[n_in:n_in + n_out], refs[n_in + n_out:])
        b.start()
        b.relay()
        b.finish()

    return list(pl.pallas_call(
        body, name=name, out_shape=tuple(job.out_shapes), in_specs=[ANY] * n_in, out_specs=(ANY,) * n_out,
        scratch_shapes=job.scratch, input_output_aliases=job.aliases,
        compiler_params=pltpu.CompilerParams(has_side_effects=True),
    )(*job.inputs))


PAIR_SUM_CHUNKS = 1


def pair_sum(layer, grads, landed, parts, core, *, name):
    n = len(grads)

    def body(c_ref, *refs):
        g_refs, l_refs, o_refs = refs[:n], refs[n:2 * n], refs[3 * n:]
        for w in range(n):
            o_refs[w][...] = (g_refs[w][...].astype(F32) + l_refs[w][...].astype(F32)).astype(BF16)

    def blk(g):
        return (None, g.shape[1] // PAIR_SUM_CHUNKS, g.shape[2])

    in_specs = [pl.BlockSpec(blk(g), lambda k, i, c_ref: (2 * k + c_ref[0], i, 0)) for g in grads]
    in_specs += [pl.BlockSpec(blk(g), lambda k, i, c_ref: (k, i, 0)) for g in grads]
    in_specs += [ANY] * n
    out_specs = tuple(pl.BlockSpec((None,) + blk(g), lambda k, i, c_ref: (layer, k, i, 0)) for g in grads)
    return list(pl.pallas_call(
        body, name=name, out_shape=tuple(jax.ShapeDtypeStruct(p.shape, p.dtype) for p in parts),
        grid_spec=pltpu.PrefetchScalarGridSpec(num_scalar_prefetch=1, grid=(N_CHIPS, PAIR_SUM_CHUNKS),
                                               in_specs=in_specs, out_specs=out_specs),
        input_output_aliases={1 + 2 * n + w: w for w in range(n)},
        compiler_params=_params("parallel", "parallel"),
    )(core, *grads, *landed, *parts))


def _adamw(w, g, m, v):
    m = ADAM_B1 * m + (1.0 - ADAM_B1) * g
    v = ADAM_B2 * v + (1.0 - ADAM_B2) * (g * g)
    m_hat = m / (1.0 - ADAM_B1 ** ADAM_STEP)
    v_hat = v / (1.0 - ADAM_B2 ** ADAM_STEP)
    delta = -ADAM_LR * (m_hat / (jnp.sqrt(v_hat) + ADAM_EPS) + ADAM_WD * w)
    return delta, m, v


def reduce_adamw(part, land, chip, w, m, v, *, name):
    _, r, c = w.shape
    tr = 256 if r % 256 == 0 else (r // 2 if r > 256 else r)

    def body(k_ref, own_ref, l0_ref, l1_ref, l2_ref, w_ref, m_ref, v_ref, g_out, d_out, m_out, v_out):
        g = own_ref[...].astype(F32) + l0_ref[...].astype(F32) + l1_ref[...].astype(F32) + l2_ref[...].astype(F32)
        delta, m_new, v_new = _adamw(w_ref[...], g, m_ref[...], v_ref[...])
        g_out[...] = g
        d_out[...] = delta
        m_out[...] = m_new
        v_out[...] = v_new

    row = pl.BlockSpec((None, tr, c), lambda l, i, k_ref: (l, i, 0))

    def slot(j):
        return pl.BlockSpec((None, None, tr, c), lambda l, i, k_ref: (l, j, i, 0))

    return pl.pallas_call(
        body, name=name, out_shape=(jax.ShapeDtypeStruct(w.shape, F32),) * 4,
        grid_spec=pltpu.PrefetchScalarGridSpec(
            num_scalar_prefetch=1, grid=(DEPTH, r // tr),
            in_specs=[pl.BlockSpec((None, None, tr, c), lambda l, i, k_ref: (l, k_ref[0], i, 0)), slot(0), slot(1),
                      slot(2), row, row, row],
            out_specs=(row, row, row, row)),
        compiler_params=_params("parallel", "parallel"),
    )(chip, part, land, land, land, w, m, v)


def gather_small(block, *, name):
    def body(x_ref, out_ref, send_sems, recv_sems, local_sem):
        x, y, c = _place()
        me = 4 * x + 2 * y + c
        mine = pltpu.make_async_copy(x_ref, out_ref.at[me], local_sem)
        mine.start()
        peers = [(x ^ (k >> 2), y ^ ((k >> 1) & 1), c ^ (k & 1)) for k in range(1, N_DEV)]
        copies = [pltpu.make_async_remote_copy(
            src_ref=x_ref, dst_ref=out_ref.at[me], send_sem=send_sems.at[k], recv_sem=recv_sems.at[k],
            device_id=peer, device_id_type=MESH) for k, peer in enumerate(peers)]
        for cp in copies:
            cp.start()
        for k, (px, py, pc) in enumerate(peers):
            pltpu.make_async_remote_copy(
                src_ref=x_ref, dst_ref=out_ref.at[4 * px + 2 * py + pc], send_sem=send_sems.at[k],
                recv_sem=recv_sems.at[k], device_id=(px, py, pc), device_id_type=MESH).wait_recv()
        for cp in copies:
            cp.wait_send()
        mine.wait()

    return pl.pallas_call(
        body, name=name, out_shape=jax.ShapeDtypeStruct((N_DEV,) + block.shape, block.dtype),
        in_specs=[ANY], out_specs=ANY,
        scratch_shapes=[pltpu.SemaphoreType.DMA((7,)), pltpu.SemaphoreType.DMA((7,)), pltpu.SemaphoreType.DMA],
        compiler_params=pltpu.CompilerParams(has_side_effects=True),
    )(block)


def small_adamw(gathered, w, m, v, *, name):
    def body(g_ref, w_ref, m_ref, v_ref, g_out, d_out, m_out, v_out):
        g = g_ref[0]
        for d in range(1, N_DEV):
            g = g + g_ref[d]
        delta, m_new, v_new = _adamw(w_ref[...], g, m_ref[...], v_ref[...])
        g_out[...] = g
        d_out[...] = delta
        m_out[...] = m_new
        v_out[...] = v_new

    return pl.pallas_call(
        body, name=name, out_shape=(jax.ShapeDtypeStruct(w.shape, F32),) * 4,
    )(gathered, w, m, v)


def norm_matmul(x, g, w, *, gate_split, name, ride=()):
    s, d = x.shape
    tm = min(ROW_TILE, s)
    blocked = w.ndim == 3
    n = w.shape[0] if not blocked else w.shape[0] * w.shape[2]

    def body(x_ref, g_ref, w_ref, h_ref, *outs):
        xv = x_ref[...]
        h = ((xv * _rsqrt_ms(xv)) * g_ref[...]).astype(BF16)
        h_ref[...] = h
        if blocked:
            nb = w_ref.shape[2]
            for j in range(w_ref.shape[0]):
                outs[0][:, j * nb:(j + 1) * nb] = _dot(h, w_ref[j]).astype(BF16)
        else:
            p = _dot_nt(h, w_ref[...])
            outs[0][...] = p[:, :gate_split].astype(BF16)
            outs[1][...] = (1.0 / (1.0 + jnp.exp(-p[:, gate_split:]))).astype(BF16)

    row = lambda i: (i, 0)
    fixed = lambda i: (0, 0)
    if blocked:
        out_shape = (jax.ShapeDtypeStruct((s, d), BF16), jax.ShapeDtypeStruct((s, n), BF16))
        out_specs = (pl.BlockSpec((tm, d), row), pl.BlockSpec((tm, n), row))
        w_spec = pl.BlockSpec(w.shape, lambda i: (0, 0, 0))
    else:
        out_shape = (jax.ShapeDtypeStruct((s, d), BF16), jax.ShapeDtypeStruct((s, gate_split), BF16),
                     jax.ShapeDtypeStruct((s, n - gate_split), BF16))
        out_specs = (pl.BlockSpec((tm, d), row), pl.BlockSpec((tm, gate_split), row),
                     pl.BlockSpec((tm, n - gate_split), row))
        w_spec = pl.BlockSpec((n, d), fixed)
    return _call(body, name=name, grid=(s // tm,), out_shape=out_shape, out_specs=out_specs,
                 in_specs=[pl.BlockSpec((tm, d), row), pl.BlockSpec((1, d), fixed), w_spec], args=(x, g, w), ride=ride)


def merge_out_fwd(x, o_sb, o_sw, gates, w_bsb, w_bsw, w_o, *, name, ride=()):
    s, d = x.shape
    tm = min(ROW_TILE, s)

    def body(x_ref, osb_ref, osw_ref, g_ref, wsb_ref, wsw_ref, wo_ref, x1_ref, ysb_ref, ysw_ref, mg_ref):
        y_sb = _dot(osb_ref[...].astype(BF16), wsb_ref[...])
        y_sw = _dot(osw_ref[...].astype(BF16), wsw_ref[...])
        g = g_ref[...].astype(F32)
        merged = (g[:, :d] * y_sb + g[:, d:] * y_sw).astype(BF16)
        ysb_ref[...] = y_sb.astype(BF16)
        ysw_ref[...] = y_sw.astype(BF16)
        mg_ref[...] = merged
        x1_ref[...] = x_ref[...] + _dot(merged, wo_ref[...])

    row = lambda i: (i, 0)
    fixed = lambda i: (0, 0)
    wd = o_sb.shape[1]
    return _call(
        body, name=name, grid=(s // tm,),
        out_shape=(jax.ShapeDtypeStruct((s, d), F32),) + (jax.ShapeDtypeStruct((s, d), BF16),) * 3,
        in_specs=[pl.BlockSpec((tm, d), row), pl.BlockSpec((tm, wd), row), pl.BlockSpec((tm, wd), row),
                  pl.BlockSpec((tm, 2 * d), row), pl.BlockSpec((wd, d), fixed), pl.BlockSpec((wd, d), fixed),
                  pl.BlockSpec((d, d), fixed)],
        out_specs=(pl.BlockSpec((tm, d), row),) * 4, args=(x, o_sb, o_sw, gates, w_bsb, w_bsw, w_o), ride=ride)


def mlp_down_fwd(x1, u, w_down, *, name, ride=()):
    s, d = x1.shape
    f = u.shape[1]
    tm = min(ROW_TILE, s)

    def body(x_ref, u_ref, w_ref, o_ref):
        a = jnp.maximum(u_ref[...].astype(F32), 0.0)
        o_ref[...] = x_ref[...] + _dot((a * a).astype(BF16), w_ref[...])

    row = lambda i: (i, 0)
    return _call(
        body, name=name, grid=(s // tm,), out_shape=(jax.ShapeDtypeStruct((s, d), F32),),
        in_specs=[pl.BlockSpec((tm, d), row), pl.BlockSpec((tm, f), row), pl.BlockSpec((f, d), lambda i: (0, 0))],
        out_specs=(pl.BlockSpec((tm, d), row),), args=(x1, u, w_down), ride=ride)


def loss_head(y, target, *, name):
    s, d = y.shape
    tm = min(ROW_TILE, s)

    def body(y_ref, t_ref, dy_ref, dyb_ref, loss_ref):
        @pl.when(pl.program_id(0) == 0)
        def _():
            loss_ref[...] = jnp.zeros_like(loss_ref)

        e = y_ref[...] - t_ref[...]
        dy = e * (1.0 / d)
        dy_ref[...] = dy
        dyb_ref[...] = dy.astype(BF16)
        per_row = jnp.sum(e * e, axis=1, keepdims=True) * (0.5 / d)
        loss_ref[...] += jnp.sum(per_row, axis=0, keepdims=True)

    row = lambda i: (i, 0)
    return pl.pallas_call(
        body, name=name, grid=(s // tm,),
        out_shape=(jax.ShapeDtypeStruct((s, d), F32), jax.ShapeDtypeStruct((s, d), BF16),
                   jax.ShapeDtypeStruct((1, 1), F32)),
        in_specs=[pl.BlockSpec((tm, d), row), pl.BlockSpec((tm, d), row)],
        out_specs=(pl.BlockSpec((tm, d), row), pl.BlockSpec((tm, d), row), pl.BlockSpec((1, 1), lambda i: (0, 0))),
        compiler_params=_params("arbitrary"),
    )(y, target)


def mlp_bwd_up(dxb, u, w_down, *, name, ride=()):
    s, d = dxb.shape
    f = u.shape[1]
    tm = min(ROW_TILE, s)

    def body(dx_ref, u_ref, w_ref, du_ref):
        da = _dot_nt(dx_ref[...], w_ref[...])
        du_ref[...] = (da * (2.0 * jnp.maximum(u_ref[...].astype(F32), 0.0))).astype(BF16)

    row = lambda i: (i, 0)
    return _call(body, name=name, grid=(s // tm,), out_shape=(jax.ShapeDtypeStruct((s, f), BF16),),
                 in_specs=[pl.BlockSpec((tm, d), row), pl.BlockSpec((tm, f), row),
                           pl.BlockSpec((f, d), lambda i: (0, 0))],
                 out_specs=(pl.BlockSpec((tm, f), row),), args=(dxb, u, w_down), ride=ride)


def matmul_nt_norm_bwd(pieces, w, x, g, dres, *, name, ride=()):
    s = x.shape[0]
    d = x.shape[1]
    tm = min(ROW_TILE, s)
    blocked = w.ndim == 3
    n_pieces = len(pieces)
    widths = [p.shape[1] for p in pieces]

    def body(*refs):
        p_refs = refs[:n_pieces]
        w_ref, x_ref, g_ref, dres_ref, dx_ref, dxb_ref, dg_ref = refs[n_pieces:]

        @pl.when(pl.program_id(0) == 0)
        def _():
            dg_ref[...] = jnp.zeros_like(dg_ref)

        if blocked:
            nb = w_ref.shape[2]
            dh = _dot_nt(p_refs[0][:, :nb], w_ref[0])
            for j in range(1, w_ref.shape[0]):
                dh = dh + _dot_nt(p_refs[0][:, j * nb:(j + 1) * nb], w_ref[j])
        else:
            dh, off = None, 0
            for p_ref, width in zip(p_refs, widths):
                part = _dot(p_ref[...], w_ref[off:off + width, :])
                dh = part if dh is None else dh + part
                off += width
        xv = x_ref[...]
        r = _rsqrt_ms(xv)
        dyg = dh * g_ref[...]
        dx = dres_ref[...] + r * dyg - xv * ((r * r * r) * jnp.mean(dyg * xv, axis=-1, keepdims=True))
        dx_ref[...] = dx
        dxb_ref[...] = dx.astype(BF16)
        dg_ref[...] += jnp.sum(dh * (xv * r), axis=0, keepdims=True)

    row = lambda i: (i, 0)
    fixed = lambda i: (0, 0)
    w_spec = pl.BlockSpec(w.shape, (lambda i: (0, 0, 0)) if blocked else fixed)
    return _call(
        body, name=name, grid=(s // tm,),
        out_shape=(jax.ShapeDtypeStruct((s, d), F32), jax.ShapeDtypeStruct((s, d), BF16),
                   jax.ShapeDtypeStruct((1, d), F32)),
        in_specs=[pl.BlockSpec((tm, width), row) for width in widths] + [
            w_spec, pl.BlockSpec((tm, d), row), pl.BlockSpec((1, d), fixed), pl.BlockSpec((tm, d), row)],
        out_specs=(pl.BlockSpec((tm, d), row), pl.BlockSpec((tm, d), row), pl.BlockSpec((1, d), fixed)),
        args=(*pieces, w, x, g, dres), ride=ride)


def out_bwd(dx1b, w_o, gates, y_sb, y_sw, w_bsb, w_bsw, *, name):
    s, d = dx1b.shape
    wd = w_bsb.shape[0]
    tm = min(ROW_TILE, s)

    def body(dx_ref, wo_ref, g_ref, ysb_ref, ysw_ref, wsb_ref, wsw_ref, dysb_ref, dysw_ref, dosb_ref, dosw_ref, dgl_ref):
        dm = _dot_nt(dx_ref[...], wo_ref[...])
        g = g_ref[...].astype(F32)
        g0, g1 = g[:, :d], g[:, d:]
        dy_sb = (dm * g0).astype(BF16)
        dy_sw = (dm * g1).astype(BF16)
        dysb_ref[...] = dy_sb
        dysw_ref[...] = dy_sw
        dosb_ref[...] = _dot_nt(dy_sb, wsb_ref[...]).astype(BF16)
        dosw_ref[...] = _dot_nt(dy_sw, wsw_ref[...]).astype(BF16)
        dgl_ref[:, :d] = (dm * ysb_ref[...].astype(F32) * (g0 * (1.0 - g0))).astype(BF16)
        dgl_ref[:, d:] = (dm * ysw_ref[...].astype(F32) * (g1 * (1.0 - g1))).astype(BF16)

    row = lambda i: (i, 0)
    fixed = lambda i: (0, 0)
    return pl.pallas_call(
        body, name=name, grid=(s // tm,),
        out_shape=(jax.ShapeDtypeStruct((s, d), BF16), jax.ShapeDtypeStruct((s, d), BF16),
                   jax.ShapeDtypeStruct((s, wd), BF16), jax.ShapeDtypeStruct((s, wd), BF16),
                   jax.ShapeDtypeStruct((s, 2 * d), BF16)),
        in_specs=[pl.BlockSpec((tm, d), row), pl.BlockSpec((d, d), fixed), pl.BlockSpec((tm, 2 * d), row),
                  pl.BlockSpec((tm, d), row), pl.BlockSpec((tm, d), row), pl.BlockSpec((wd, d), fixed),
                  pl.BlockSpec((wd, d), fixed)],
        out_specs=(pl.BlockSpec((tm, d), row), pl.BlockSpec((tm, d), row), pl.BlockSpec((tm, wd), row),
                   pl.BlockSpec((tm, wd), row), pl.BlockSpec((tm, 2 * d), row)),
        compiler_params=_params("parallel"),
    )(dx1b, w_o, gates, y_sb, y_sw, w_bsb, w_bsw)


def matmul_tn(a, pieces, *, a_block, out_cols, relu2, name):
    s, m = a.shape
    widths = [p.shape[1] for p in pieces]
    n = sum(widths)
    n_pieces = len(pieces)
    ts = min(512 if n >= 4096 else 2048, s)
    n_steps = s // ts
    if out_cols is None:
        out_shape = jax.ShapeDtypeStruct((m // a_block, a_block, n), BF16)
        out_spec = pl.BlockSpec((None, a_block, n), lambda i, k: (i, 0, 0))
    else:
        out_shape = jax.ShapeDtypeStruct((n // out_cols, m, out_cols), BF16)
        out_spec = pl.BlockSpec((n // out_cols, a_block, out_cols), lambda i, k: (0, i, 0))

    def body(a_ref, *refs):
        b_refs, o_ref, acc = refs[:n_pieces], refs[n_pieces], refs[n_pieces + 1]
        k = pl.program_id(1)

        @pl.when(k == 0)
        def _():
            acc[...] = jnp.zeros_like(acc)

        av = a_ref[...]
        if relu2:
            af = jnp.maximum(av.astype(F32), 0.0)
            av = af * af
        av = av.astype(BF16)
        off = 0
        for b_ref in b_refs:
            width = b_ref.shape[1]
            acc[:, off:off + width] += _dot_tn(av, b_ref[...].astype(BF16))
            off += width

        @pl.when(k == n_steps - 1)
        def _():
            if out_cols is None:
                o_ref[...] = acc[...].astype(BF16)
            else:
                for j in range(n // out_cols):
                    o_ref[j] = acc[:, j * out_cols:(j + 1) * out_cols].astype(BF16)

    return pl.pallas_call(
        body, name=name, grid=(m // a_block, n_steps), out_shape=out_shape,
        in_specs=[pl.BlockSpec((ts, a_block), lambda i, k: (k, i))] + [
            pl.BlockSpec((ts, width), lambda i, k: (k, 0)) for width in widths],
        out_specs=out_spec, scratch_shapes=[pltpu.VMEM((a_block, n), F32)],
        compiler_params=_params("parallel", "arbitrary"),
    )(a, *pieces)


def matmul_tn_row_blocks(pieces, b, *, n_blocks, name):
    s, n = b.shape
    widths = [p.shape[1] for p in pieces]
    m = sum(widths)
    rows = m // n_blocks
    n_pieces = len(pieces)
    ts = min(512, s)
    n_steps = s // ts
    half = n_blocks // 2

    def body(*refs):
        p_refs, b_ref, o_ref, a_tile, acc = refs[:n_pieces], refs[n_pieces], refs[n_pieces + 1], refs[-2], refs[-1]
        i, k = pl.program_id(0), pl.program_id(1)

        @pl.when(k == 0)
        def _():
            acc[...] = jnp.zeros_like(acc)

        off = 0
        for p_ref, width in zip(p_refs, widths):
            a_tile[:, off:off + width] = p_ref[...]
            off += width
        bv = b_ref[...]
        for side in range(2):
            @pl.when(i == side)
            def _():
                for j in range(half):
                    col = (side * half + j) * rows
                    acc[j] += _dot_tn(a_tile[:, col:col + rows], bv)

        @pl.when(k == n_steps - 1)
        def _():
            o_ref[...] = acc[...].astype(BF16)

    return pl.pallas_call(
        body, name=name, grid=(2, n_steps), out_shape=jax.ShapeDtypeStruct((n_blocks, rows, n), BF16),
        in_specs=[pl.BlockSpec((ts, width), lambda i, k: (k, 0)) for width in widths] + [
            pl.BlockSpec((ts, n), lambda i, k: (k, 0))],
        out_specs=pl.BlockSpec((half, rows, n), lambda i, k: (i, 0, 0)),
        scratch_shapes=[pltpu.VMEM((ts, m), BF16), pltpu.VMEM((half, rows, n), F32)],
        compiler_params=_params("parallel", "arbitrary"),
    )(*pieces, b)


def _softplus(z):
    return jnp.maximum(z, 0.0) + jnp.log(1.0 + jnp.exp(-jnp.abs(z)))


def _suffix_sums(x, tri2):
    groups = x.shape[1] // LANES
    outs, run = [None] * groups, None
    for g in reversed(range(groups)):
        xg = x[:, g * LANES:(g + 1) * LANES]
        hi, lo = _split_bf16(xg)
        inner = _dot(jnp.concatenate([hi, lo], axis=1), tri2)
        outs[g] = inner if run is None else inner + run
        total = jnp.sum(xg, axis=1, keepdims=True)
        run = total if run is None else run + total
    return jnp.concatenate(outs, axis=1), run


def _head_mask(h):
    return (lax.broadcasted_iota(jnp.int32, (1, LANES), 1) // HEAD_DIM) == h


def _stack_heads(x):
    zero = jnp.zeros_like(x)
    return jnp.concatenate([jnp.where(_head_mask(0), x, zero), jnp.where(_head_mask(1), x, zero)], axis=0)


def _unstack_heads(r, t):
    return jnp.where(_head_mask(0), r[:t], r[t:])


def _sb_positions(q0, tk):
    row = lax.broadcasted_iota(jnp.int32, (2 * SB_TQ, tk), 0)
    col = lax.broadcasted_iota(jnp.int32, (2 * SB_TQ, tk), 1)
    return q0 + jnp.where(row >= SB_TQ, row - SB_TQ, row), col


def _sb_first_key(q0):
    return pl.multiple_of(jnp.maximum(q0 + SB_TQ - SB_TK1, 0), SB_TQ)


def _sb_next_key(k_prev):
    return pl.multiple_of(jnp.maximum(k_prev - SB_TK, 0), SB_TQ)


def _sb_rows(q0):
    return pl.ds(pl.multiple_of(2 * q0, 2 * SB_TQ), 2 * SB_TQ)


def _sb_keep(live):
    return lambda x: jnp.where(live, x, 0.0)


def _sb_keep_first(q0, k0, interior):
    if not interior:
        tpos, col = _sb_positions(q0, SB_TK1)
        return _sb_keep(k0 + col < tpos)
    row = lax.broadcasted_iota(jnp.int32, (2 * SB_TQ, SB_TQ), 0)
    own = lax.broadcasted_iota(jnp.int32, (2 * SB_TQ, SB_TQ), 1) < jnp.where(row >= SB_TQ, row - SB_TQ, row)
    past = SB_TK1 - SB_TQ
    return lambda x: jnp.concatenate([x[:, :past], jnp.where(own, x[:, past:], 0.0)], axis=1)


SB_EDGE_TILES = 8
SB_SCAN_GROUP = 4


def _sb_first_pass(nq, first):
    edge = min(nq, SB_EDGE_TILES)
    lax.fori_loop(0, edge, first(False), 0, unroll=4)
    lax.fori_loop(edge, nq, first(True), 0, unroll=8)


def _sb_scan_tiles(c_all, nq, more):
    rows = SB_SCAN_GROUP * 2 * SB_TQ

    def group(g, carry):
        @pl.when(jnp.max(c_all[pl.ds(pl.multiple_of(g * rows, rows), rows), :]) > SB_CUTOFF)
        def _():
            lax.fori_loop(g * SB_SCAN_GROUP, (g + 1) * SB_SCAN_GROUP, more, 0)

        return carry

    lax.fori_loop(0, nq // SB_SCAN_GROUP, group, 0)


def sb_attn_fwd(proj, tri2, *, name, ride=()):
    s = proj.shape[0]
    nq = s // SB_TQ
    n_pairs = SB_WIDTH // LANES

    def body(q_ref, k_ref, v_ref, tri_ref, o_ref, c_all):
        def block(qh, k0, tk, keep, c):
            z = _dot_nt(qh, k_ref[pl.ds(k0, tk), :])
            sp = _softplus(z)
            tail, total = _suffix_sums(keep(-sp), tri_ref[...])
            w = keep(jnp.exp(z - sp + tail + c))
            return _dot(w.astype(BF16), v_ref[pl.ds(k0, tk), :]), c + total

        def load_q(q0):
            return _stack_heads(q_ref[pl.ds(q0, SB_TQ), :]) * SCALE

        def first(interior):
            def run(qb, carry):
                q0 = pl.multiple_of(qb * SB_TQ, SB_TQ)
                k0 = _sb_first_key(q0)
                acc, c = block(load_q(q0), k0, SB_TK1, _sb_keep_first(q0, k0, interior), jnp.zeros((2 * SB_TQ, 1), F32))
                o_ref[pl.ds(q0, SB_TQ), :] = _unstack_heads(acc, SB_TQ)
                c_all[_sb_rows(q0), :] = jnp.broadcast_to(jnp.where(k0 > 0, c, NEG), (2 * SB_TQ, LANES))
                return carry

            return run

        _sb_first_pass(nq, first)

        @pl.when(jnp.max(c_all[...]) > SB_CUTOFF)
        def _():
            def more(qb, carry):
                q0 = pl.multiple_of(qb * SB_TQ, SB_TQ)
                c0 = c_all[_sb_rows(q0), 0:1]

                @pl.when(jnp.max(c0) > SB_CUTOFF)
                def _():
                    qh = load_q(q0)
                    _, col = _sb_positions(q0, SB_TK)

                    def cond(st):
                        return jnp.logical_and(st[0] > 0, st[3] > SB_CUTOFF)

                    def step(st):
                        k_prev, c, acc, _ = st
                        k0 = _sb_next_key(k_prev)
                        part, c = block(qh, k0, SB_TK, _sb_keep(k0 + col < k_prev), c)
                        return k0, c, acc + part, jnp.max(c)

                    st = lax.while_loop(cond, step, (_sb_first_key(q0), c0, jnp.zeros((2 * SB_TQ, LANES), F32),
                                                     jnp.max(c0)))
                    o_ref[pl.ds(q0, SB_TQ), :] += _unstack_heads(st[2], SB_TQ)

                return carry

            _sb_scan_tiles(c_all, nq, more)

    def col_spec(j):
        return pl.BlockSpec((s, LANES), lambda p: (0, j * n_pairs + p))

    (o,), rides = _call(
        body, name=name, grid=(n_pairs,), out_shape=(jax.ShapeDtypeStruct((s, SB_WIDTH), F32),),
        in_specs=[col_spec(0), col_spec(1), col_spec(2), pl.BlockSpec((2 * LANES, LANES), lambda p: (0, 0))],
        out_specs=(pl.BlockSpec((s, LANES), lambda p: (0, p)),), scratch_shapes=[pltpu.VMEM((2 * s, LANES), F32)],
        args=(proj, proj, proj, tri2), ride=ride)
    return o, rides


def sb_attn_bwd(proj, tri2, o, do, *, name, ride=()):
    s = proj.shape[0]
    nq = s // SB_TQ
    n_pairs = SB_WIDTH // LANES

    def body(q_ref, k_ref, v_ref, tri_ref, o_ref, do_ref, dq_ref, dk_ref, dv_ref, dq_acc, dk_acc, dv_acc, c_all, e_all):
        dk_acc[...] = jnp.zeros_like(dk_acc)
        dv_acc[...] = jnp.zeros_like(dv_acc)

        def load(q0):
            qh = _stack_heads(q_ref[pl.ds(q0, SB_TQ), :]) * SCALE
            doh_b = _stack_heads(do_ref[pl.ds(q0, SB_TQ), :])
            ov = o_ref[pl.ds(q0, SB_TQ), :]
            dd = jnp.sum(doh_b.astype(F32) * jnp.concatenate([ov, ov], axis=0), axis=1, keepdims=True)
            return qh, doh_b, dd

        def block(qh, doh_b, dd, k0, tk, keep, c, ce):
            kt = k_ref[pl.ds(k0, tk), :]
            z = _dot_nt(qh, kt)
            sp = _softplus(z)
            lb = z - sp
            tail, total = _suffix_sums(keep(-sp), tri_ref[...])
            wb = keep(jnp.exp(lb + tail + c)).astype(BF16)
            e = wb.astype(F32) * _dot_nt(doh_b, v_ref[pl.ds(k0, tk), :])
            e_tail, e_total = _suffix_sums(e, tri_ref[...])
            dzb = keep(e - jnp.exp(lb) * (dd - ce - e_tail)).astype(BF16)
            dk_acc[pl.ds(k0, tk), :] += _dot_tn(dzb, qh)
            dv_acc[pl.ds(k0, tk), :] += _dot_tn(wb, doh_b)
            return _dot(dzb, kt), c + total, ce + e_total

        def first(interior):
            def run(qb, carry):
                q0 = pl.multiple_of(qb * SB_TQ, SB_TQ)
                qh, doh_b, dd = load(q0)
                k0 = _sb_first_key(q0)
                zero = jnp.zeros((2 * SB_TQ, 1), F32)
                dq, c, ce = block(qh, doh_b, dd, k0, SB_TK1, _sb_keep_first(q0, k0, interior), zero, zero)
                dq_acc[pl.ds(q0, SB_TQ), :] = _unstack_heads(dq, SB_TQ)
                c_all[_sb_rows(q0), :] = jnp.broadcast_to(jnp.where(k0 > 0, c, NEG), (2 * SB_TQ, LANES))
                e_all[_sb_rows(q0), :] = jnp.broadcast_to(ce, (2 * SB_TQ, LANES))
                return carry

            return run

        _sb_first_pass(nq, first)

        @pl.when(jnp.max(c_all[...]) > SB_CUTOFF)
        def _():
            def more(qb, carry):
                q0 = pl.multiple_of(qb * SB_TQ, SB_TQ)
                c0 = c_all[_sb_rows(q0), 0:1]

                @pl.when(jnp.max(c0) > SB_CUTOFF)
                def _():
                    qh, doh_b, dd = load(q0)
                    _, col = _sb_positions(q0, SB_TK)

                    def cond(st):
                        return jnp.logical_and(st[0] > 0, st[4] > SB_CUTOFF)

                    def step(st):
                        k_prev, c, ce, dq, _ = st
                        k0 = _sb_next_key(k_prev)
                        part, c, ce = block(qh, doh_b, dd, k0, SB_TK, _sb_keep(k0 + col < k_prev), c, ce)
                        return k0, c, ce, dq + part, jnp.max(c)

                    st = lax.while_loop(cond, step, (_sb_first_key(q0), c0, e_all[_sb_rows(q0), 0:1],
                                                     jnp.zeros((2 * SB_TQ, LANES), F32), jnp.max(c0)))
                    dq_acc[pl.ds(q0, SB_TQ), :] += _unstack_heads(st[3], SB_TQ)

                return carry

            _sb_scan_tiles(c_all, nq, more)

        dq_ref[...] = (dq_acc[...] * SCALE).astype(BF16)
        dk_ref[...] = dk_acc[...].astype(BF16)
        dv_ref[...] = dv_acc[...].astype(BF16)

    def col_spec(j):
        return pl.BlockSpec((s, LANES), lambda p: (0, j * n_pairs + p))

    pair = pl.BlockSpec((s, LANES), lambda p: (0, p))
    (dq, dk, dv), rides = _call(
        body, name=name, grid=(n_pairs,), out_shape=(jax.ShapeDtypeStruct((s, SB_WIDTH), BF16),) * 3,
        in_specs=[col_spec(0), col_spec(1), col_spec(2), pl.BlockSpec((2 * LANES, LANES), lambda p: (0, 0)), pair, pair],
        out_specs=(pair, pair, pair),
        scratch_shapes=[pltpu.VMEM((s, LANES), F32)] * 3 + [pltpu.VMEM((2 * s, LANES), F32)] * 2,
        args=(proj, proj, proj, tri2, o, do), ride=ride)
    return dq, dk, dv, rides


def _lane_lo():
    return lax.broadcasted_iota(jnp.int32, (1, LANES), 1) < HEAD_DIM


def _swap_halves(x):
    return pltpu.roll(x, HEAD_DIM, 1)


def _rot_half(y):
    first = (lax.broadcasted_iota(jnp.int32, (1, LANES), 1) % HEAD_DIM) < (HEAD_DIM // 2)
    return jnp.where(first, pltpu.roll(y, LANES - HEAD_DIM // 2, 1), pltpu.roll(y, HEAD_DIM // 2, 1))


def _head_mean(v, avg):
    hi, lo = _split_bf16(v)
    return _dot(hi, avg) + _dot(lo, avg)


def _head_avg_matrix():
    lane = jnp.arange(LANES) // HEAD_DIM
    return ((lane[:, None] == lane[None, :]).astype(F32) * (1.0 / HEAD_DIM)).astype(BF16)


def swa_prep_fwd(proj, cos_p, sin_p, gq, gk, *, name):
    s = proj.shape[0]
    tm = min(512, s)
    q_blk = (3 * SB_WIDTH) // SWA_Q_WIDTH
    k_blk = (3 * SB_WIDTH + SWA_Q_WIDTH) // LANES

    def body(q_ref, k_ref, cos_ref, sin_ref, gq_ref, gk_ref, avg_ref, qn_ref, kn_ref):
        cosv, sinv, avg = cos_ref[...], sin_ref[...], avg_ref[...]

        def norm_rope(xv, g):
            y = (xv * lax.rsqrt(_head_mean(xv * xv, avg) + NORM_EPS)) * g
            return y * cosv + _rot_half(y) * sinv

        for j in range(SWA_Q_WIDTH // LANES):
            sl = slice(j * LANES, (j + 1) * LANES)
            qn_ref[:, sl] = norm_rope(q_ref[:, sl].astype(F32), gq_ref[...]).astype(BF16)
        kn_ref[...] = norm_rope(k_ref[...].astype(F32), gk_ref[...]).astype(BF16)

    row = lambda i: (i, 0)
    fixed = lambda i: (0, 0)
    return pl.pallas_call(
        body, name=name, grid=(s // tm,),
        out_shape=(jax.ShapeDtypeStruct((s, SWA_Q_WIDTH), BF16), jax.ShapeDtypeStruct((s, LANES), BF16)),
        in_specs=[pl.BlockSpec((tm, SWA_Q_WIDTH), lambda i: (i, q_blk)), pl.BlockSpec((tm, LANES), lambda i: (i, k_blk)),
                  pl.BlockSpec((tm, LANES), row), pl.BlockSpec((tm, LANES), row),
                  pl.BlockSpec((1, LANES), fixed), pl.BlockSpec((1, LANES), fixed), pl.BlockSpec((LANES, LANES), fixed)],
        out_specs=(pl.BlockSpec((tm, SWA_Q_WIDTH), row), pl.BlockSpec((tm, LANES), row)),
        compiler_params=_params("parallel"),
    )(proj, proj, cos_p, sin_p, gq, gk, _head_avg_matrix())


def swa_prep_bwd(proj, cos_p, sin_p, gq, gk, dqn, dkn, dv, *, name):
    s = proj.shape[0]
    tm = min(512, s)
    q_blk = (3 * SB_WIDTH) // SWA_Q_WIDTH
    k_blk = (3 * SB_WIDTH + SWA_Q_WIDTH) // LANES

    def body(q_ref, k_ref, cos_ref, sin_ref, gq_ref, gk_ref, avg_ref, dqn_ref, dkn_ref, dv_ref, dq_ref, dk_ref, dvb_ref,
             dgq_ref, dgk_ref):
        @pl.when(pl.program_id(0) == 0)
        def _():
            dgq_ref[...] = jnp.zeros_like(dgq_ref)
            dgk_ref[...] = jnp.zeros_like(dgk_ref)

        cosv, sinv, avg = cos_ref[...], sin_ref[...], avg_ref[...]

        def bwd(xv, g, dout):
            dy = dout * cosv + _rot_half(dout * sinv)
            r = lax.rsqrt(_head_mean(xv * xv, avg) + NORM_EPS)
            dyg = dy * g
            dx = r * dyg - xv * ((r * r * r) * _head_mean(dyg * xv, avg))
            return dx, jnp.sum(dy * (xv * r), axis=0, keepdims=True)

        for j in range(SWA_Q_WIDTH // LANES):
            sl = slice(j * LANES, (j + 1) * LANES)
            dx, dg = bwd(q_ref[:, sl].astype(F32), gq_ref[...], dqn_ref[:, sl])
            dq_ref[:, sl] = dx.astype(BF16)
            dgq_ref[:, sl] += dg
        dx, dg = bwd(k_ref[...].astype(F32), gk_ref[...], dkn_ref[...])
        dk_ref[...] = dx.astype(BF16)
        dgk_ref[...] += dg
        dvb_ref[...] = dv_ref[...].astype(BF16)

    row = lambda i: (i, 0)
    fixed = lambda i: (0, 0)
    lane_row = pl.BlockSpec((tm, LANES), row)
    return pl.pallas_call(
        body, name=name, grid=(s // tm,),
        out_shape=(jax.ShapeDtypeStruct((s, SWA_Q_WIDTH), BF16), jax.ShapeDtypeStruct((s, LANES), BF16),
                   jax.ShapeDtypeStruct((s, LANES), BF16),
                   jax.ShapeDtypeStruct((1, SWA_Q_WIDTH), F32), jax.ShapeDtypeStruct((1, LANES), F32)),
        in_specs=[pl.BlockSpec((tm, SWA_Q_WIDTH), lambda i: (i, q_blk)), pl.BlockSpec((tm, LANES), lambda i: (i, k_blk)),
                  lane_row, lane_row, pl.BlockSpec((1, LANES), fixed), pl.BlockSpec((1, LANES), fixed),
                  pl.BlockSpec((LANES, LANES), fixed), pl.BlockSpec((tm, SWA_Q_WIDTH), row), lane_row, lane_row],
        out_specs=(pl.BlockSpec((tm, SWA_Q_WIDTH), row), lane_row, lane_row,
                   pl.BlockSpec((1, SWA_Q_WIDTH), fixed), pl.BlockSpec((1, LANES), fixed)),
        compiler_params=_params("arbitrary"),
    )(proj, proj, cos_p, sin_p, gq, gk, _head_avg_matrix(), dqn, dkn, dv)


def _swa_kv_copies(k_ref, v_ref, kg_ref, vg_ref, second_kv):
    s = k_ref.shape[0]
    rows = min(512, s)
    keep = jnp.logical_xor(_lane_lo(), second_kv)

    def chunk(r, carry):
        sl = pl.ds(pl.multiple_of(r * rows, rows), rows)
        for src, dst in ((k_ref, kg_ref), (v_ref, vg_ref)):
            f = src[sl, :].astype(F32)
            dst[sl, :] = jnp.where(keep, f, _swap_halves(f)).astype(BF16)
        return carry

    lax.fori_loop(0, s // rows, chunk, 0)


def _swa_tile(i, kg_ref, vg_ref):
    q0 = pl.multiple_of(i * SWA_TQ, SWA_TQ)
    k0 = pl.multiple_of(jnp.maximum(i - 1, 0) * SWA_TQ, SWA_TQ)
    kg = kg_ref[pl.ds(k0, SWA_TK), :]
    vg = vg_ref[pl.ds(k0, SWA_TK), :]
    row = lax.broadcasted_iota(jnp.int32, (2 * SWA_TQ, SWA_TK), 0)
    tpos = q0 + jnp.where(row >= SWA_TQ, row - SWA_TQ, row)
    spos = k0 + lax.broadcasted_iota(jnp.int32, (2 * SWA_TQ, SWA_TK), 1)
    valid = jnp.logical_and(spos <= tpos, spos > tpos - WINDOW)
    return q0, k0, kg, vg, valid


def _swa_probs(qh, kg, valid, sink):
    z = jnp.where(valid, _dot_nt(qh, kg) * SCALE, NEG)
    m = jnp.maximum(jnp.max(z, axis=1, keepdims=True), sink)
    pexp = jnp.exp(z - m)
    psink = jnp.exp(sink - m)
    inv = 1.0 / (jnp.sum(pexp, axis=1, keepdims=True) + psink)
    return pexp * inv, psink * inv


def _stacked_sink(sink_row):
    s0 = jnp.sum(jnp.where(_head_mask(0), sink_row, 0.0), axis=1, keepdims=True) * (1.0 / HEAD_DIM)
    s1 = jnp.sum(jnp.where(_head_mask(1), sink_row, 0.0), axis=1, keepdims=True) * (1.0 / HEAD_DIM)
    top = lax.broadcasted_iota(jnp.int32, (2 * SWA_TQ, 1), 0) < SWA_TQ
    return jnp.where(top, s0, s1)


def swa_attn_fwd(qn, kn, proj, sink_p, *, name, ride=()):
    s = qn.shape[0]
    nq = s // SWA_TQ
    n_pairs = SWA_Q_WIDTH // LANES
    v_blk = (3 * SB_WIDTH + SWA_Q_WIDTH + SWA_KV_WIDTH) // LANES

    def body(q_ref, k_ref, v_ref, s_ref, o_ref, kg_ref, vg_ref):
        _swa_kv_copies(k_ref, v_ref, kg_ref, vg_ref, (pl.program_id(0) // 2) == 1)
        sink = _stacked_sink(s_ref[...])

        def tile(i, carry):
            q0, _, kg, vg, valid = _swa_tile(i, kg_ref, vg_ref)
            probs, _ = _swa_probs(_stack_heads(q_ref[pl.ds(q0, SWA_TQ), :]), kg, valid, sink)
            o_ref[pl.ds(q0, SWA_TQ), :] = _unstack_heads(_dot(probs.astype(BF16), vg), SWA_TQ)
            return carry

        lax.fori_loop(0, nq, tile, 0, unroll=8)

    pair = pl.BlockSpec((s, LANES), lambda p: (0, p))
    whole = pl.BlockSpec((s, LANES), lambda p: (0, 0))
    (o,), rides = _call(
        body, name=name, grid=(n_pairs,), out_shape=(jax.ShapeDtypeStruct((s, SWA_Q_WIDTH), F32),),
        in_specs=[pair, whole, pl.BlockSpec((s, LANES), lambda p: (0, v_blk)),
                  pl.BlockSpec((None, 1, LANES), lambda p: (p, 0, 0))],
        out_specs=(pair,), scratch_shapes=[pltpu.VMEM((s, LANES), BF16)] * 2, args=(qn, kn, proj, sink_p), ride=ride)
    return o, rides


def swa_attn_bwd(qn, kn, proj, sink_p, o, do, *, name, ride=()):
    s = qn.shape[0]
    nq = s // SWA_TQ
    n_pairs = SWA_Q_WIDTH // LANES
    v_blk = (3 * SB_WIDTH + SWA_Q_WIDTH + SWA_KV_WIDTH) // LANES
    fold_rows = min(512, s)

    def body(q_ref, k_ref, v_ref, s_ref, o_ref, do_ref, dq_ref, dk_ref, dv_ref, ds_ref, acc_k, acc_v, kg_ref, vg_ref):
        p = pl.program_id(0)
        _swa_kv_copies(k_ref, v_ref, kg_ref, vg_ref, (p // 2) == 1)
        sink = _stacked_sink(s_ref[...])

        @pl.when(p % 2 == 0)
        def _():
            acc_k[...] = jnp.zeros_like(acc_k)
            acc_v[...] = jnp.zeros_like(acc_v)

        ds_ref[...] = jnp.zeros_like(ds_ref)

        def tile(i, carry):
            q0, k0, kg, vg, valid = _swa_tile(i, kg_ref, vg_ref)
            qh = _stack_heads(q_ref[pl.ds(q0, SWA_TQ), :])
            doh_b = _stack_heads(do_ref[pl.ds(q0, SWA_TQ), :])
            ov = o_ref[pl.ds(q0, SWA_TQ), :]
            delta = jnp.sum(doh_b.astype(F32) * jnp.concatenate([ov, ov], axis=0), axis=1, keepdims=True)
            probs, psink = _swa_probs(qh, kg, valid, sink)
            dz = probs * (_dot_nt(doh_b, vg) - delta)
            dzb = (dz * SCALE).astype(BF16)
            dq_ref[pl.ds(q0, SWA_TQ), :] = _unstack_heads(_dot(dzb, kg), SWA_TQ)
            acc_k[pl.ds(k0, SWA_TK), :] += _dot_tn(dzb, qh)
            acc_v[pl.ds(k0, SWA_TK), :] += _dot_tn(probs.astype(BF16), doh_b)
            pd = psink * delta
            ds_ref[...] -= jnp.where(_head_mask(0), jnp.sum(pd[:SWA_TQ], axis=0, keepdims=True),
                                     jnp.sum(pd[SWA_TQ:], axis=0, keepdims=True))
            return carry

        lax.fori_loop(0, nq, tile, 0, unroll=8)

        def fold_into(first_head):
            def fold(r, carry):
                rows = pl.ds(pl.multiple_of(r * fold_rows, fold_rows), fold_rows)
                for acc, out in ((acc_k, dk_ref), (acc_v, dv_ref)):
                    a = acc[rows, :]
                    both = a + _swap_halves(a)
                    if first_head:
                        out[rows, :] = jnp.where(_lane_lo(), both, 0.0)
                    else:
                        out[rows, :] = jnp.where(_lane_lo(), out[rows, :], both)
                return carry

            lax.fori_loop(0, s // fold_rows, fold, 0)

        @pl.when(p == 1)
        def _():
            fold_into(True)

        @pl.when(p == 3)
        def _():
            fold_into(False)

    pair = pl.BlockSpec((s, LANES), lambda p: (0, p))
    whole = pl.BlockSpec((s, LANES), lambda p: (0, 0))
    sink_spec = pl.BlockSpec((None, 1, LANES), lambda p: (p, 0, 0))
    (dq, dk, dv, dsink), rides = _call(
        body, name=name, grid=(n_pairs,),
        out_shape=(jax.ShapeDtypeStruct((s, SWA_Q_WIDTH), F32), jax.ShapeDtypeStruct((s, LANES), F32),
                   jax.ShapeDtypeStruct((s, LANES), F32), jax.ShapeDtypeStruct((n_pairs, 1, LANES), F32)),
        in_specs=[pair, whole, pl.BlockSpec((s, LANES), lambda p: (0, v_blk)), sink_spec, pair, pair],
        out_specs=(pair, whole, whole, sink_spec),
        scratch_shapes=[pltpu.VMEM((s, LANES), F32)] * 2 + [pltpu.VMEM((s, LANES), BF16)] * 2,
        args=(qn, kn, proj, sink_p, o, do), ride=ride)
    return dq, dk, dv, dsink, rides


def _rope_tables(s):
    inv_freq = 1.0 / (ROPE_THETA ** (jnp.arange(0, HEAD_DIM, 2, dtype=F32) / HEAD_DIM))
    ang = jnp.arange(s, dtype=F32)[:, None] * inv_freq[None, :]
    cos, sin = jnp.cos(ang), jnp.sin(ang)
    cos_p = jnp.tile(jnp.concatenate([cos, cos], axis=1), (1, LANES // HEAD_DIM))
    sin_p = jnp.tile(jnp.concatenate([-sin, sin], axis=1), (1, LANES // HEAD_DIM))
    return cos_p, sin_p


def _lane_tile(v, reps):
    return jnp.tile(v.reshape(1, -1), (1, reps))


def _natural(stack, w):
    n, r, c = stack.shape
    if MATRIX_NAMES[w] in ROW_SHARDED or w == W_IN:
        return stack.reshape(n * r, c)
    if w == W_UP:
        return stack
    return jnp.transpose(stack, (1, 0, 2)).reshape(r, n * c)


def _pack_small(tree):
    flat = jnp.concatenate([tree[n].reshape(-1) for n in SMALL_NAMES])
    rows = -(-flat.shape[0] // (8 * LANES)) * 8
    return jnp.pad(flat, (0, rows * LANES - flat.shape[0])).reshape(rows, LANES)


def _unpack_small(packed, shapes):
    flat, out, off = packed.reshape(-1), {}, 0
    for n in SMALL_NAMES:
        size = shapes[n][0] * shapes[n][1]
        out[n] = flat[off:off + size].reshape(shapes[n])
        off += size
    return out


def train_step(x, target, weights, mom_m, mom_v):
    s = x.shape[0]
    cos_p, sin_p = _rope_tables(s)
    tri = (jnp.arange(LANES)[:, None] > jnp.arange(LANES)[None, :]).astype(BF16)
    tri = jnp.concatenate([tri, tri], axis=0)
    local = {n: (jnp.swapaxes(t, 1, 2) if n == "w_in" else t) for n, t in weights.items()}
    local_m = {n: (jnp.swapaxes(t, 1, 2) if n == "w_in" else t) for n, t in mom_m.items()}
    local_v = {n: (jnp.swapaxes(t, 1, 2) if n == "w_in" else t) for n, t in mom_v.items()}
    shards = [[local[n][l].astype(BF16) for n in MATRIX_NAMES] for l in range(DEPTH)]
    core = lax.axis_index("c").astype(jnp.int32).reshape(1)
    chip = (2 * lax.axis_index("x") + lax.axis_index("y")).astype(jnp.int32).reshape(1)

    def gather(l, ws, rows=None, stacks=None):
        return GatherJob([shards[l][w] for w in ws], rows=rows, stacks=stacks)

    def halves(w):
        r = shards[0][w].shape[0] // 2
        return (0, r), (r, r)

    w_in = _natural(exchange_alone(gather(0, [W_IN]), name="gather_w_in0")[0], W_IN)
    saved = []
    for l in range(DEPTH):
        g_mix = weights["mix_norm_g"][l].reshape(1, D_MODEL)
        g_mlp = weights["mlp_norm_g"][l].reshape(1, D_MODEL)
        gq = _lane_tile(weights["q_norm_g"][l], LANES // HEAD_DIM)
        gk = _lane_tile(weights["k_norm_g"][l], LANES // HEAD_DIM)
        sink_p = jnp.repeat(weights["sinks"][l].reshape(SWA_Q_WIDTH // LANES, 2), HEAD_DIM, axis=1)
        sink_p = sink_p.reshape(SWA_Q_WIDTH // LANES, 1, LANES)
        (h, proj, gates), ((s_bsb, s_bsw, s_out),) = norm_matmul(
            x, g_mix, w_in, gate_split=ATTN_WIDTH, name="in_proj", ride=[gather(l, [W_BSB, W_BSW, W_OUT])])
        o_sb, ((s_up,),) = sb_attn_fwd(proj, tri, name="sb_fwd", ride=[gather(l, [W_UP])])
        qn, kn = swa_prep_fwd(proj, cos_p, sin_p, gq, gk, name="swa_prep")
        o_sw, ((s_down,),) = swa_attn_fwd(qn, kn, proj, sink_p, name="swa_fwd",
                                          ride=[gather(l, [W_DOWN], rows=halves(W_DOWN)[0])])
        more = l + 1 < DEPTH
        (x1, y_sb, y_sw, merged), rides = merge_out_fwd(
            x, o_sb, o_sw, gates, _natural(s_bsb, W_BSB), _natural(s_bsw, W_BSW), _natural(s_out, W_OUT),
            name="merge_out" if more else "merge_out_last",
            ride=[gather(l + 1, [W_IN], rows=halves(W_IN)[0])] if more else [])
        (h2, u), ((s_down,),) = norm_matmul(x1, g_mlp, s_up, gate_split=None, name="mlp_up",
                                            ride=[gather(l, [W_DOWN], rows=halves(W_DOWN)[1], stacks=[s_down])])
        mats = [w_in, _natural(s_bsb, W_BSB), _natural(s_bsw, W_BSW), _natural(s_out, W_OUT), s_up,
                _natural(s_down, W_DOWN)]
        if more:
            (x2,), ((s_in,),) = mlp_down_fwd(x1, u, mats[W_DOWN], name="mlp_down",
                                             ride=[gather(l + 1, [W_IN], rows=halves(W_IN)[1], stacks=rides[0])])
            w_in = _natural(s_in, W_IN)
        else:
            (x2,), _ = mlp_down_fwd(x1, u, mats[W_DOWN], name="mlp_down_last")
        saved.append(dict(x=x, h=h, proj=proj, gates=gates, o_sb=o_sb, qn=qn, kn=kn, o_sw=o_sw, y_sb=y_sb, y_sw=y_sw,
                          merged=merged, x1=x1, h2=h2, u=u, g_mix=g_mix, g_mlp=g_mlp, gq=gq, gk=gk, sink_p=sink_p,
                          mats=mats))
        x = x2

    dx, dxb, loss = loss_head(x, target, name="loss_head")

    shard_shapes = [local[n].shape[1:] for n in MATRIX_NAMES]
    parts = [lax.empty((DEPTH, N_CHIPS) + sh, BF16) for sh in shard_shapes]
    lands = [lax.empty((DEPTH, 3) + sh, BF16) for sh in shard_shapes]
    small_grads = {n: [None] * DEPTH for n in SMALL_NAMES}
    half = D_MODEL // 2

    def summed(l, ws, grads, landed):
        new = pair_sum(l, grads, landed, [parts[w] for w in ws], core, name="grad_pair_sum")
        for w, p in zip(ws, new):
            parts[w] = p

    def chip_job(items):
        return ChipJob(items, parts, lands)

    def landed_chip(job, outs):
        for w, a in zip(job.ws, outs):
            lands[w] = a

    in_pending = None
    for l in reversed(range(DEPTH)):
        a = saved[l]
        mats = a["mats"]
        in_jobs = [chip_job([(in_pending, W_IN, rows)]) for rows in halves(W_IN)] if in_pending is not None else []
        (du,), rides = mlp_bwd_up(dxb, a["u"], mats[W_DOWN], name="mlp_bwd_up" if in_jobs else "mlp_bwd_up_first",
                                  ride=in_jobs[:1])
        if in_jobs:
            landed_chip(in_jobs[0], rides[0])
            in_jobs[1] = chip_job([(in_pending, W_IN, halves(W_IN)[1])])
        dw_down = matmul_tn(a["u"], [dxb], a_block=half, out_cols=None, relu2=True, name="dw_down")
        dw_up = matmul_tn(a["h2"], [du], a_block=half, out_cols=du.shape[1] // N_DEV, relu2=False, name="dw_up")
        g_mlp_w = [dw_up, dw_down.reshape((N_DEV,) + shard_shapes[W_DOWN])]
        (dx1, dx1b, dg_mlp), rides = matmul_nt_norm_bwd(
            [du], mats[W_UP], a["x1"], a["g_mlp"], dx, name="mlp_bwd_norm" if in_jobs else "mlp_bwd_norm_first",
            ride=[PairJob(g_mlp_w)] + in_jobs[1:])
        if in_jobs:
            landed_chip(in_jobs[1], rides[1])
        summed(l, [W_UP, W_DOWN], g_mlp_w, rides[0])
        small_grads["mlp_norm_g"][l] = dg_mlp.reshape(D_MODEL)
        dw_out = matmul_tn(a["merged"], [dx1b], a_block=half, out_cols=None, relu2=False, name="dw_out")
        dy_sb, dy_sw, do_sb, do_sw, dgl = out_bwd(dx1b, mats[W_OUT], a["gates"], a["y_sb"], a["y_sw"],
                                                  mats[W_BSB], mats[W_BSW], name="out_bwd")
        dw_bsb = matmul_tn(a["o_sb"], [dy_sb], a_block=half, out_cols=D_MODEL // N_DEV, relu2=False, name="dw_branch_sb")
        dw_bsw = matmul_tn(a["o_sw"], [dy_sw], a_block=half, out_cols=D_MODEL // N_DEV, relu2=False, name="dw_branch_swa")
        g_mix_w = [dw_bsb, dw_bsw, dw_out.reshape((N_DEV,) + shard_shapes[W_OUT])]
        job = chip_job([(l, W_UP), (l, W_DOWN)])
        dq_sb, dk_sb, dv_sb, (outs, landed) = sb_attn_bwd(a["proj"], tri, a["o_sb"], do_sb, name="sb_bwd",
                                                         ride=[job, PairJob(g_mix_w)])
        landed_chip(job, outs)
        summed(l, [W_BSB, W_BSW, W_OUT], g_mix_w, landed)
        job = chip_job([(l, W_BSB), (l, W_BSW), (l, W_OUT)])
        dqn, dkn, dv_sw, dsink, (outs,) = swa_attn_bwd(a["qn"], a["kn"], a["proj"], a["sink_p"], a["o_sw"], do_sw,
                                                      name="swa_bwd", ride=[job])
        landed_chip(job, outs)
        dq_sw, dk_sw, dv_swb, dgq, dgk = swa_prep_bwd(a["proj"], cos_p, sin_p, a["gq"], a["gk"], dqn, dkn, dv_sw,
                                                      name="swa_prep_bwd")
        small_grads["q_norm_g"][l] = dgq.reshape(SWA_Q_WIDTH // HEAD_DIM, HEAD_DIM).sum(0)
        small_grads["k_norm_g"][l] = dgk.reshape(LANES // HEAD_DIM, HEAD_DIM).sum(0)
        small_grads["sinks"][l] = dsink[:, 0, ::HEAD_DIM].reshape(SWA_Q_WIDTH // HEAD_DIM)
        pieces = [dq_sb, dk_sb, dv_sb, dq_sw, dk_sw, dv_swb, dgl]
        g_in = [matmul_tn_row_blocks(pieces, a["h"], n_blocks=N_DEV, name="dw_in")]
        if l > 0:
            (dx, dxb, dg_mix), (landed,) = matmul_nt_norm_bwd(pieces, mats[W_IN], a["x"], a["g_mix"], dx1,
                                                             name="in_proj_bwd", ride=[PairJob(g_in)])
            summed(l, [W_IN], g_in, landed)
            in_pending = l
        else:
            summed(l, [W_IN], g_in, exchange_alone(PairJob(g_in), name="grad_pair_exchange_in0"))
            job = chip_job([(l, W_IN)])
            (dx, dxb, dg_mix), (outs,) = matmul_nt_norm_bwd(pieces, mats[W_IN], a["x"], a["g_mix"], dx1,
                                                           name="in_proj_bwd_last", ride=[job])
            landed_chip(job, outs)
        small_grads["mix_norm_g"][l] = dg_mix.reshape(D_MODEL)

    out_g, out_d, out_m, out_v = {}, {}, {}, {}
    for i, n in enumerate(MATRIX_NAMES):
        outs = reduce_adamw(parts[i], lands[i], chip, local[n], local_m[n], local_v[n], name="adamw_" + n)
        if n == "w_in":
            outs = [jnp.swapaxes(t, 1, 2) for t in outs]
        out_g[n], out_d[n], out_m[n], out_v[n] = outs
    small_shapes = {n: weights[n].shape for n in SMALL_NAMES}
    small_all = gather_small(_pack_small({n: jnp.stack(v) for n, v in small_grads.items()}), name="gather_small_grads")
    sg, sd, sm, sv = small_adamw(small_all, _pack_small(weights), _pack_small(mom_m), _pack_small(mom_v),
                                 name="small_adamw")
    for tree, packed_small in ((out_g, sg), (out_d, sd), (out_m, sm), (out_v, sv)):
        tree.update(_unpack_small(packed_small, small_shapes))
    return loss, dx, (out_g, out_d, out_m, out_v)


def kernel(x, mix_norm_g, w_in, q_norm_g, k_norm_g, sinks, w_branch_sb, w_branch_swa, w_out, mlp_norm_g, w_up, w_down, loss_target, m_mix_norm_g, m_w_in, m_q_norm_g, m_k_norm_g, m_sinks, m_w_branch_sb, m_w_branch_swa, m_w_out, m_mlp_norm_g, m_w_up, m_w_down, v_mix_norm_g, v_w_in, v_q_norm_g, v_k_norm_g, v_sinks, v_w_branch_sb, v_w_branch_swa, v_w_out, v_mlp_norm_g, v_w_up, v_w_down):
    weights = dict(mix_norm_g=mix_norm_g, w_in=w_in, q_norm_g=q_norm_g, k_norm_g=k_norm_g, sinks=sinks,
                   w_branch_sb=w_branch_sb, w_branch_swa=w_branch_swa, w_out=w_out, mlp_norm_g=mlp_norm_g, w_up=w_up,
                   w_down=w_down)
    mom_m = dict(mix_norm_g=m_mix_norm_g, w_in=m_w_in, q_norm_g=m_q_norm_g, k_norm_g=m_k_norm_g, sinks=m_sinks,
                 w_branch_sb=m_w_branch_sb, w_branch_swa=m_w_branch_swa, w_out=m_w_out, mlp_norm_g=m_mlp_norm_g,
                 w_up=m_w_up, w_down=m_w_down)
    mom_v = dict(mix_norm_g=v_mix_norm_g, w_in=v_w_in, q_norm_g=v_q_norm_g, k_norm_g=v_k_norm_g, sinks=v_sinks,
                 w_branch_sb=v_w_branch_sb, w_branch_swa=v_w_branch_swa, w_out=v_w_out, mlp_norm_g=v_mlp_norm_g,
                 w_up=v_w_up, w_down=v_w_down)
    loss_part, grad_x, outs = train_step(x[0], loss_target[0], weights, mom_m, mom_v)
    loss = lax.psum(loss_part[0, 0], MESH_AXES)
    return (loss, grad_x[None], *[outs[0][n] for n in WEIGHT_ORDER], *[outs[1][n] for n in WEIGHT_ORDER],
            *[outs[2][n] for n in WEIGHT_ORDER], *[outs[3][n] for n in WEIGHT_ORDER])
```

```python
import math

import jax
import jax.numpy as jnp
from jax import lax
from jax.experimental import pallas as pl
from jax.experimental.pallas import tpu as pltpu

F32 = jnp.float32
BF16 = jnp.bfloat16

DEPTH = 4
D_MODEL = 1024
HEAD_DIM = 64
LANES = 128
WINDOW = 128
SB_WIDTH = 512
SWA_Q_WIDTH = 512
SWA_KV_WIDTH = 128
ATTN_WIDTH = 3 * SB_WIDTH + SWA_Q_WIDTH + 2 * SWA_KV_WIDTH
ROPE_THETA = 10000.0
NORM_EPS = 1e-6
SCALE = HEAD_DIM ** -0.5
NEG = -1e30
N_DEV = 8
N_CHIPS = 4

ADAM_LR = 0.001
ADAM_B1 = 0.9
ADAM_B2 = 0.999
ADAM_EPS = 1e-08
ADAM_WD = 0.01
ADAM_STEP = 10

SB_TQ = 128
SB_TK1 = 384
SB_TK = 256
SB_CUTOFF = -88.0
SWA_TQ = 128
SWA_TK = 256
ROW_TILE = 512
VMEM_LIMIT = 56 * 1024 * 1024

MATRIX_NAMES = ("w_in", "w_branch_sb", "w_branch_swa", "w_out", "w_up", "w_down")
W_IN, W_BSB, W_BSW, W_OUT, W_UP, W_DOWN = range(6)
ROW_SHARDED = ("w_out", "w_down")
SMALL_NAMES = ("mix_norm_g", "q_norm_g", "k_norm_g", "sinks", "mlp_norm_g")
WEIGHT_ORDER = ("mix_norm_g", "w_in", "q_norm_g", "k_norm_g", "sinks", "w_branch_sb", "w_branch_swa", "w_out",
                "mlp_norm_g", "w_up", "w_down")
MESH_AXES = ("x", "y", "c")

ANY = pl.BlockSpec(memory_space=pl.ANY)
MESH = pl.DeviceIdType.MESH


def _params(*sem):
    return pltpu.CompilerParams(dimension_semantics=sem, vmem_limit_bytes=VMEM_LIMIT)


def _dot(a, b):
    return jnp.dot(a, b, preferred_element_type=F32)


def _dot_nt(a, b):
    return lax.dot_general(a, b, (((1,), (1,)), ((), ())), preferred_element_type=F32)


def _dot_tn(a, b):
    return lax.dot_general(a, b, (((0,), (0,)), ((), ())), preferred_element_type=F32)


def _split_bf16(x):
    hi = lax.bitcast_convert_type(lax.bitcast_convert_type(x, jnp.uint32) & jnp.uint32(0xFFFF0000), F32)
    return hi.astype(BF16), (x - hi).astype(BF16)


def _rsqrt_ms(x):
    return lax.rsqrt(jnp.mean(x * x, axis=-1, keepdims=True) + NORM_EPS)


def _place():
    return lax.axis_index("x"), lax.axis_index("y"), lax.axis_index("c")


class _Gather:
    def __init__(self, x_refs, out_refs, send_sems, recv_sems, local_sems, rows=None):
        self.x_refs, self.out_refs = x_refs, out_refs
        self.send_sems, self.recv_sems, self.local_sems = send_sems, recv_sems, local_sems
        self.n = len(x_refs)
        self.rows = rows
        x, y, c = _place()
        self.c = c
        self.me, self.sibling = (x, y, c), (x, y, 1 - c)
        self.chips = [(1 - x, y), (x, 1 - y), (1 - x, 1 - y)]

    def _part(self, ref):
        return ref if self.rows is None else ref.at[pl.ds(*self.rows)]

    def _slot(self, w, blk):
        return self._part(self.out_refs[w].at[4 * blk[0] + 2 * blk[1] + blk[2]])

    def _copy(self, k, w, blk, to, own=False):
        dst = self._slot(w, blk)
        return pltpu.make_async_remote_copy(
            src_ref=self._part(self.x_refs[w]) if own else dst, dst_ref=dst, send_sem=self.send_sems.at[k, w],
            recv_sem=self.recv_sems.at[k, w], device_id=to, device_id_type=MESH)

    def _mine(self, w):
        return pltpu.make_async_copy(self._part(self.x_refs[w]), self._slot(w, self.me), self.local_sems.at[w])

    def _first(self, w):
        return [self._copy(0, w, self.me, self.sibling, own=True)] + [
            self._copy(1 + j, w, self.me, (*chip, self.c), own=True) for j, chip in enumerate(self.chips)]

    def _passed(self, j, w):
        return self._copy(4 + j, w, (*self.chips[j], self.c), self.sibling)

    def start(self):
        for w in range(self.n):
            self._mine(w).start()
            for cp in self._first(w):
                cp.start()

    def relay(self):
        for j, chip in enumerate(self.chips):
            for w in range(self.n):
                self._copy(1 + j, w, (*chip, self.c), self.me).wait_recv()
                self._passed(j, w).start()

    def finish(self):
        for w in range(self.n):
            self._copy(0, w, self.sibling, self.me).wait_recv()
            for j, chip in enumerate(self.chips):
                self._copy(4 + j, w, (*chip, 1 - self.c), self.me).wait_recv()
            for cp in self._first(w):
                cp.wait_send()
            for j in range(3):
                self._passed(j, w).wait_send()
            self._mine(w).wait()


class GatherJob:
    def __init__(self, shards, rows=None, stacks=None):
        n = len(shards)
        self.n, self.rows = n, rows
        self.inputs = list(shards) + (list(stacks) if stacks is not None else [])
        self.out_shapes = [jax.ShapeDtypeStruct((N_DEV,) + s.shape, s.dtype) for s in shards]
        self.aliases = {n + i: i for i in range(n)} if stacks is not None else {}
        self.scratch = [pltpu.SemaphoreType.DMA((7, n)), pltpu.SemaphoreType.DMA((7, n)),
                        pltpu.SemaphoreType.DMA((n,))]

    def bind(self, in_refs, out_refs, scratch_refs):
        return _Gather(in_refs[:self.n], out_refs, *scratch_refs, rows=self.rows)


class _Copies:
    def __init__(self, copies):
        self.copies = copies

    def start(self):
        for cp in self.copies:
            cp.start()

    def relay(self):
        pass

    def finish(self):
        for cp in self.copies:
            cp.wait_recv()
        for cp in self.copies:
            cp.wait_send()


class ChipJob:
    def __init__(self, items, parts, lands):
        self.ws = sorted({item[1] for item in items})
        n = len(self.ws)
        self.items = [(item[0], self.ws.index(item[1]), item[2] if len(item) > 2 else None) for item in items]
        self.inputs = [parts[w] for w in self.ws] + [lands[w] for w in self.ws]
        self.out_shapes = [jax.ShapeDtypeStruct(lands[w].shape, lands[w].dtype) for w in self.ws]
        self.aliases = {n + i: i for i in range(n)}
        self.scratch = [pltpu.SemaphoreType.DMA((3, n)), pltpu.SemaphoreType.DMA((3, n))]

    def bind(self, in_refs, out_refs, scratch_refs):
        send_sems, recv_sems = scratch_refs
        x, y, c = _place()
        chips = [(1 - x, y), (x, 1 - y), (1 - x, 1 - y)]

        def part(ref, rows):
            return ref if rows is None else ref.at[pl.ds(*rows)]

        return _Copies([pltpu.make_async_remote_copy(
            src_ref=part(in_refs[i].at[layer, 2 * px + py], rows), dst_ref=part(out_refs[i].at[layer, j], rows),
            send_sem=send_sems.at[j, i], recv_sem=recv_sems.at[j, i], device_id=(px, py, c), device_id_type=MESH)
            for layer, i, rows in self.items for j, (px, py) in enumerate(chips)])


class PairJob:
    def __init__(self, grads):
        n = len(grads)
        self.inputs = list(grads)
        self.out_shapes = [jax.ShapeDtypeStruct((N_CHIPS,) + g.shape[1:], g.dtype) for g in grads]
        self.aliases = {}
        self.scratch = [pltpu.SemaphoreType.DMA((N_CHIPS, n)), pltpu.SemaphoreType.DMA((N_CHIPS, n))]

    def bind(self, in_refs, out_refs, scratch_refs):
        send_sems, recv_sems = scratch_refs
        x, y, c = _place()
        return _Copies([pltpu.make_async_remote_copy(
            src_ref=in_refs[w].at[2 * k + (1 - c)], dst_ref=out_refs[w].at[k], send_sem=send_sems.at[k, w],
            recv_sem=recv_sems.at[k, w], device_id=(x, y, 1 - c), device_id_type=MESH)
            for w in range(len(in_refs)) for k in range(N_CHIPS)])


def _call(body, *, name, grid, in_specs, out_specs, out_shape, args, scratch_shapes=(), ride=()):
    out_specs, out_shape, in_specs = tuple(out_specs), tuple(out_shape), list(in_specs)
    scratch_shapes = list(scratch_shapes)
    order = ("arbitrary",) * len(grid)
    if not ride:
        outs = pl.pallas_call(body, name=name, grid=grid, in_specs=in_specs, out_specs=out_specs, out_shape=out_shape,
                              scratch_shapes=scratch_shapes, compiler_params=_params(*order))(*args)
        return tuple(outs), []
    n_in, n_out, n_scr = len(in_specs), len(out_specs), len(scratch_shapes)
    n_steps = math.prod(grid)
    relay_early = n_steps >= 8
    relay_at = n_steps - n_steps // 4 if relay_early else n_steps - 1

    def split(refs, pos, counts):
        groups = []
        for k in counts:
            groups.append(refs[pos:pos + k])
            pos += k
        return groups, pos

    def wrapped(*refs):
        ins, pos = refs[:n_in], n_in
        job_in, pos = split(refs, pos, [len(j.inputs) for j in ride])
        outs, pos = refs[pos:pos + n_out], pos + n_out
        job_out, pos = split(refs, pos, [len(j.out_shapes) for j in ride])
        scr, pos = refs[pos:pos + n_scr], pos + n_scr
        job_scr, pos = split(refs, pos, [len(j.scratch) for j in ride])
        bound = [j.bind(i, o, s) for j, i, o, s in zip(ride, job_in, job_out, job_scr)]
        step = pl.program_id(0)
        for axis in range(1, len(grid)):
            step = step * grid[axis] + pl.program_id(axis)

        @pl.when(step == 0)
        def _():
            for b in bound:
                b.start()

        if relay_early:
            @pl.when(step == relay_at)
            def _():
                for b in bound:
                    b.relay()

        body(*ins, *outs, *scr)

        @pl.when(step == n_steps - 1)
        def _():
            if not relay_early:
                for b in bound:
                    b.relay()
            for b in bound:
                b.finish()

    aliases, in_pos, out_pos = {}, n_in, n_out
    for j in ride:
        aliases.update({in_pos + i: out_pos + o for i, o in j.aliases.items()})
        in_pos += len(j.inputs)
        out_pos += len(j.out_shapes)
    results = pl.pallas_call(
        wrapped, name=name, grid=grid, in_specs=in_specs + [ANY] * (in_pos - n_in),
        out_specs=out_specs + (ANY,) * (out_pos - n_out),
        out_shape=out_shape + tuple(s for j in ride for s in j.out_shapes),
        scratch_shapes=scratch_shapes + [s for j in ride for s in j.scratch], input_output_aliases=aliases,
        compiler_params=pltpu.CompilerParams(dimension_semantics=order, vmem_limit_bytes=VMEM_LIMIT,
                                             has_side_effects=True),
    )(*args, *[a for j in ride for a in j.inputs])
    job_results, pos = split(list(results), n_out, [len(j.out_shapes) for j in ride])
    return tuple(results[:n_out]), job_results


def exchange_alone(job, *, name):
    n_in, n_out = len(job.inputs), len(job.out_shapes)

    def body(*refs):
        b = job.bind(refs[:n_in], refs[n_in:n_in + n_out], refs[n_in + n_out:])
        b.start()
        b.relay()
        b.finish()

    return list(pl.pallas_call(
        body, name=name, out_shape=tuple(job.out_shapes), in_specs=[ANY] * n_in, out_specs=(ANY,) * n_out,
        scratch_shapes=job.scratch, input_output_aliases=job.aliases,
        compiler_params=pltpu.CompilerParams(has_side_effects=True),
    )(*job.inputs))


PAIR_SUM_CHUNKS = 1


def pair_sum(layer, grads, landed, parts, core, *, name):
    n = len(grads)

    def body(c_ref, *refs):
        g_refs, l_refs, o_refs = refs[:n], refs[n:2 * n], refs[3 * n:]
        for w in range(n):
            o_refs[w][...] = (g_refs[w][...].astype(F32) + l_refs[w][...].astype(F32)).astype(BF16)

    def blk(g):
        return (None, g.shape[1] // PAIR_SUM_CHUNKS, g.shape[2])

    in_specs = [pl.BlockSpec(blk(g), lambda k, i, c_ref: (2 * k + c_ref[0], i, 0)) for g in grads]
    in_specs += [pl.BlockSpec(blk(g), lambda k, i, c_ref: (k, i, 0)) for g in grads]
    in_specs += [ANY] * n
    out_specs = tuple(pl.BlockSpec((None,) + blk(g), lambda k, i, c_ref: (layer, k, i, 0)) for g in grads)
    return list(pl.pallas_call(
        body, name=name, out_shape=tuple(jax.ShapeDtypeStruct(p.shape, p.dtype) for p in parts),
        grid_spec=pltpu.PrefetchScalarGridSpec(num_scalar_prefetch=1, grid=(N_CHIPS, PAIR_SUM_CHUNKS),
                                               in_specs=in_specs, out_specs=out_specs),
        input_output_aliases={1 + 2 * n + w: w for w in range(n)},
        compiler_params=_params("parallel", "parallel"),
    )(core, *grads, *landed, *parts))


def _adamw(w, g, m, v):
    m = ADAM_B1 * m + (1.0 - ADAM_B1) * g
    v = ADAM_B2 * v + (1.0 - ADAM_B2) * (g * g)
    m_hat = m / (1.0 - ADAM_B1 ** ADAM_STEP)
    v_hat = v / (1.0 - ADAM_B2 ** ADAM_STEP)
    delta = -ADAM_LR * (m_hat / (jnp.sqrt(v_hat) + ADAM_EPS) + ADAM_WD * w)
    return delta, m, v


def reduce_adamw(part, land, chip, w, m, v, *, name):
    _, r, c = w.shape
    tr = r

    def body(k_ref, own_ref, l0_ref, l1_ref, l2_ref, w_ref, m_ref, v_ref, g_out, d_out, m_out, v_out):
        g = own_ref[...].astype(F32) + l0_ref[...].astype(F32) + l1_ref[...].astype(F32) + l2_ref[...].astype(F32)
        delta, m_new, v_new = _adamw(w_ref[...], g, m_ref[...], v_ref[...])
        g_out[...] = g
        d_out[...] = delta
        m_out[...] = m_new
        v_out[...] = v_new

    row = pl.BlockSpec((None, tr, c), lambda l, i, k_ref: (l, i, 0))

    def slot(j):
        return pl.BlockSpec((None, None, tr, c), lambda l, i, k_ref: (l, j, i, 0))

    return pl.pallas_call(
        body, name=name, out_shape=(jax.ShapeDtypeStruct(w.shape, F32),) * 4,
        grid_spec=pltpu.PrefetchScalarGridSpec(
            num_scalar_prefetch=1, grid=(DEPTH, r // tr),
            in_specs=[pl.BlockSpec((None, None, tr, c), lambda l, i, k_ref: (l, k_ref[0], i, 0)), slot(0), slot(1),
                      slot(2), row, row, row],
            out_specs=(row, row, row, row)),
        compiler_params=_params("parallel", "parallel"),
    )(chip, part, land, land, land, w, m, v)


def gather_small(block, *, name):
    def body(x_ref, out_ref, send_sems, recv_sems, local_sem):
        x, y, c = _place()
        me = 4 * x + 2 * y + c
        mine = pltpu.make_async_copy(x_ref, out_ref.at[me], local_sem)
        mine.start()
        peers = [(x ^ (k >> 2), y ^ ((k >> 1) & 1), c ^ (k & 1)) for k in range(1, N_DEV)]
        copies = [pltpu.make_async_remote_copy(
            src_ref=x_ref, dst_ref=out_ref.at[me], send_sem=send_sems.at[k], recv_sem=recv_sems.at[k],
            device_id=peer, device_id_type=MESH) for k, peer in enumerate(peers)]
        for cp in copies:
            cp.start()
        for k, (px, py, pc) in enumerate(peers):
            pltpu.make_async_remote_copy(
                src_ref=x_ref, dst_ref=out_ref.at[4 * px + 2 * py + pc], send_sem=send_sems.at[k],
                recv_sem=recv_sems.at[k], device_id=(px, py, pc), device_id_type=MESH).wait_recv()
        for cp in copies:
            cp.wait_send()
        mine.wait()

    return pl.pallas_call(
        body, name=name, out_shape=jax.ShapeDtypeStruct((N_DEV,) + block.shape, block.dtype),
        in_specs=[ANY], out_specs=ANY,
        scratch_shapes=[pltpu.SemaphoreType.DMA((7,)), pltpu.SemaphoreType.DMA((7,)), pltpu.SemaphoreType.DMA],
        compiler_params=pltpu.CompilerParams(has_side_effects=True),
    )(block)


def small_adamw(gathered, w, m, v, *, name):
    def body(g_ref, w_ref, m_ref, v_ref, g_out, d_out, m_out, v_out):
        g = g_ref[0]
        for d in range(1, N_DEV):
            g = g + g_ref[d]
        delta, m_new, v_new = _adamw(w_ref[...], g, m_ref[...], v_ref[...])
        g_out[...] = g
        d_out[...] = delta
        m_out[...] = m_new
        v_out[...] = v_new

    return pl.pallas_call(
        body, name=name, out_shape=(jax.ShapeDtypeStruct(w.shape, F32),) * 4,
    )(gathered, w, m, v)


def norm_matmul(x, g, w, *, gate_split, name, ride=()):
    s, d = x.shape
    tm = min(ROW_TILE, s)
    blocked = w.ndim == 3
    n = w.shape[0] if not blocked else w.shape[0] * w.shape[2]

    def body(x_ref, g_ref, w_ref, h_ref, *outs):
        xv = x_ref[...]
        h = ((xv * _rsqrt_ms(xv)) * g_ref[...]).astype(BF16)
        h_ref[...] = h
        if blocked:
            nb = w_ref.shape[2]
            for j in range(w_ref.shape[0]):
                outs[0][:, j * nb:(j + 1) * nb] = _dot(h, w_ref[j]).astype(BF16)
        else:
            p = _dot_nt(h, w_ref[...])
            outs[0][...] = p[:, :gate_split].astype(BF16)
            outs[1][...] = (1.0 / (1.0 + jnp.exp(-p[:, gate_split:]))).astype(BF16)

    row = lambda i: (i, 0)
    fixed = lambda i: (0, 0)
    if blocked:
        out_shape = (jax.ShapeDtypeStruct((s, d), BF16), jax.ShapeDtypeStruct((s, n), BF16))
        out_specs = (pl.BlockSpec((tm, d), row), pl.BlockSpec((tm, n), row))
        w_spec = pl.BlockSpec(w.shape, lambda i: (0, 0, 0))
    else:
        out_shape = (jax.ShapeDtypeStruct((s, d), BF16), jax.ShapeDtypeStruct((s, gate_split), BF16),
                     jax.ShapeDtypeStruct((s, n - gate_split), BF16))
        out_specs = (pl.BlockSpec((tm, d), row), pl.BlockSpec((tm, gate_split), row),
                     pl.BlockSpec((tm, n - gate_split), row))
        w_spec = pl.BlockSpec((n, d), fixed)
    return _call(body, name=name, grid=(s // tm,), out_shape=out_shape, out_specs=out_specs,
                 in_specs=[pl.BlockSpec((tm, d), row), pl.BlockSpec((1, d), fixed), w_spec], args=(x, g, w), ride=ride)


def merge_out_fwd(x, o_sb, o_sw, gates, w_bsb, w_bsw, w_o, *, name, ride=()):
    s, d = x.shape
    tm = min(ROW_TILE, s)

    def body(x_ref, osb_ref, osw_ref, g_ref, wsb_ref, wsw_ref, wo_ref, x1_ref, ysb_ref, ysw_ref, mg_ref):
        y_sb = _dot(osb_ref[...].astype(BF16), wsb_ref[...])
        y_sw = _dot(osw_ref[...].astype(BF16), wsw_ref[...])
        g = g_ref[...].astype(F32)
        merged = (g[:, :d] * y_sb + g[:, d:] * y_sw).astype(BF16)
        ysb_ref[...] = y_sb.astype(BF16)
        ysw_ref[...] = y_sw.astype(BF16)
        mg_ref[...] = merged
        x1_ref[...] = x_ref[...] + _dot(merged, wo_ref[...])

    row = lambda i: (i, 0)
    fixed = lambda i: (0, 0)
    wd = o_sb.shape[1]
    return _call(
        body, name=name, grid=(s // tm,),
        out_shape=(jax.ShapeDtypeStruct((s, d), F32),) + (jax.ShapeDtypeStruct((s, d), BF16),) * 3,
        in_specs=[pl.BlockSpec((tm, d), row), pl.BlockSpec((tm, wd), row), pl.BlockSpec((tm, wd), row),
                  pl.BlockSpec((tm, 2 * d), row), pl.BlockSpec((wd, d), fixed), pl.BlockSpec((wd, d), fixed),
                  pl.BlockSpec((d, d), fixed)],
        out_specs=(pl.BlockSpec((tm, d), row),) * 4, args=(x, o_sb, o_sw, gates, w_bsb, w_bsw, w_o), ride=ride)


def mlp_down_fwd(x1, u, w_down, *, name, ride=()):
    s, d = x1.shape
    f = u.shape[1]
    tm = min(ROW_TILE, s)

    def body(x_ref, u_ref, w_ref, o_ref):
        a = jnp.maximum(u_ref[...].astype(F32), 0.0)
        o_ref[...] = x_ref[...] + _dot((a * a).astype(BF16), w_ref[...])

    row = lambda i: (i, 0)
    return _call(
        body, name=name, grid=(s // tm,), out_shape=(jax.ShapeDtypeStruct((s, d), F32),),
        in_specs=[pl.BlockSpec((tm, d), row), pl.BlockSpec((tm, f), row), pl.BlockSpec((f, d), lambda i: (0, 0))],
        out_specs=(pl.BlockSpec((tm, d), row),), args=(x1, u, w_down), ride=ride)


def loss_head(y, target, *, name):
    s, d = y.shape
    tm = min(ROW_TILE, s)

    def body(y_ref, t_ref, dy_ref, dyb_ref, loss_ref):
        @pl.when(pl.program_id(0) == 0)
        def _():
            loss_ref[...] = jnp.zeros_like(loss_ref)

        e = y_ref[...] - t_ref[...]
        dy = e * (1.0 / d)
        dy_ref[...] = dy
        dyb_ref[...] = dy.astype(BF16)
        per_row = jnp.sum(e * e, axis=1, keepdims=True) * (0.5 / d)
        loss_ref[...] += jnp.sum(per_row, axis=0, keepdims=True)

    row = lambda i: (i, 0)
    return pl.pallas_call(
        body, name=name, grid=(s // tm,),
        out_shape=(jax.ShapeDtypeStruct((s, d), F32), jax.ShapeDtypeStruct((s, d), BF16),
                   jax.ShapeDtypeStruct((1, 1), F32)),
        in_specs=[pl.BlockSpec((tm, d), row), pl.BlockSpec((tm, d), row)],
        out_specs=(pl.BlockSpec((tm, d), row), pl.BlockSpec((tm, d), row), pl.BlockSpec((1, 1), lambda i: (0, 0))),
        compiler_params=_params("arbitrary"),
    )(y, target)


def mlp_bwd_up(dxb, u, w_down, *, name, ride=()):
    s, d = dxb.shape
    f = u.shape[1]
    tm = min(ROW_TILE, s)

    def body(dx_ref, u_ref, w_ref, du_ref):
        da = _dot_nt(dx_ref[...], w_ref[...])
        du_ref[...] = (da * (2.0 * jnp.maximum(u_ref[...].astype(F32), 0.0))).astype(BF16)

    row = lambda i: (i, 0)
    return _call(body, name=name, grid=(s // tm,), out_shape=(jax.ShapeDtypeStruct((s, f), BF16),),
                 in_specs=[pl.BlockSpec((tm, d), row), pl.BlockSpec((tm, f), row),
                           pl.BlockSpec((f, d), lambda i: (0, 0))],
                 out_specs=(pl.BlockSpec((tm, f), row),), args=(dxb, u, w_down), ride=ride)


def matmul_nt_norm_bwd(pieces, w, x, g, dres, *, name, ride=()):
    s = x.shape[0]
    d = x.shape[1]
    tm = min(ROW_TILE, s)
    blocked = w.ndim == 3
    n_pieces = len(pieces)
    widths = [p.shape[1] for p in pieces]

    def body(*refs):
        p_refs = refs[:n_pieces]
        w_ref, x_ref, g_ref, dres_ref, dx_ref, dxb_ref, dg_ref = refs[n_pieces:]

        @pl.when(pl.program_id(0) == 0)
        def _():
            dg_ref[...] = jnp.zeros_like(dg_ref)

        if blocked:
            nb = w_ref.shape[2]
            dh = _dot_nt(p_refs[0][:, :nb], w_ref[0])
            for j in range(1, w_ref.shape[0]):
                dh = dh + _dot_nt(p_refs[0][:, j * nb:(j + 1) * nb], w_ref[j])
        else:
            dh, off = None, 0
            for p_ref, width in zip(p_refs, widths):
                part = _dot(p_ref[...], w_ref[off:off + width, :])
                dh = part if dh is None else dh + part
                off += width
        xv = x_ref[...]
        r = _rsqrt_ms(xv)
        dyg = dh * g_ref[...]
        dx = dres_ref[...] + r * dyg - xv * ((r * r * r) * jnp.mean(dyg * xv, axis=-1, keepdims=True))
        dx_ref[...] = dx
        dxb_ref[...] = dx.astype(BF16)
        dg_ref[...] += jnp.sum(dh * (xv * r), axis=0, keepdims=True)

    row = lambda i: (i, 0)
    fixed = lambda i: (0, 0)
    w_spec = pl.BlockSpec(w.shape, (lambda i: (0, 0, 0)) if blocked else fixed)
    return _call(
        body, name=name, grid=(s // tm,),
        out_shape=(jax.ShapeDtypeStruct((s, d), F32), jax.ShapeDtypeStruct((s, d), BF16),
                   jax.ShapeDtypeStruct((1, d), F32)),
        in_specs=[pl.BlockSpec((tm, width), row) for width in widths] + [
            w_spec, pl.BlockSpec((tm, d), row), pl.BlockSpec((1, d), fixed), pl.BlockSpec((tm, d), row)],
        out_specs=(pl.BlockSpec((tm, d), row), pl.BlockSpec((tm, d), row), pl.BlockSpec((1, d), fixed)),
        args=(*pieces, w, x, g, dres), ride=ride)


def out_bwd(dx1b, w_o, gates, y_sb, y_sw, w_bsb, w_bsw, *, name):
    s, d = dx1b.shape
    wd = w_bsb.shape[0]
    tm = min(ROW_TILE, s)

    def body(dx_ref, wo_ref, g_ref, ysb_ref, ysw_ref, wsb_ref, wsw_ref, dysb_ref, dysw_ref, dosb_ref, dosw_ref, dgl_ref):
        dm = _dot_nt(dx_ref[...], wo_ref[...])
        g = g_ref[...].astype(F32)
        g0, g1 = g[:, :d], g[:, d:]
        dy_sb = (dm * g0).astype(BF16)
        dy_sw = (dm * g1).astype(BF16)
        dysb_ref[...] = dy_sb
        dysw_ref[...] = dy_sw
        dosb_ref[...] = _dot_nt(dy_sb, wsb_ref[...]).astype(BF16)
        dosw_ref[...] = _dot_nt(dy_sw, wsw_ref[...]).astype(BF16)
        dgl_ref[:, :d] = (dm * ysb_ref[...].astype(F32) * (g0 * (1.0 - g0))).astype(BF16)
        dgl_ref[:, d:] = (dm * ysw_ref[...].astype(F32) * (g1 * (1.0 - g1))).astype(BF16)

    row = lambda i: (i, 0)
    fixed = lambda i: (0, 0)
    return pl.pallas_call(
        body, name=name, grid=(s // tm,),
        out_shape=(jax.ShapeDtypeStruct((s, d), BF16), jax.ShapeDtypeStruct((s, d), BF16),
                   jax.ShapeDtypeStruct((s, wd), BF16), jax.ShapeDtypeStruct((s, wd), BF16),
                   jax.ShapeDtypeStruct((s, 2 * d), BF16)),
        in_specs=[pl.BlockSpec((tm, d), row), pl.BlockSpec((d, d), fixed), pl.BlockSpec((tm, 2 * d), row),
                  pl.BlockSpec((tm, d), row), pl.BlockSpec((tm, d), row), pl.BlockSpec((wd, d), fixed),
                  pl.BlockSpec((wd, d), fixed)],
        out_specs=(pl.BlockSpec((tm, d), row), pl.BlockSpec((tm, d), row), pl.BlockSpec((tm, wd), row),
                   pl.BlockSpec((tm, wd), row), pl.BlockSpec((tm, 2 * d), row)),
        compiler_params=_params("parallel"),
    )(dx1b, w_o, gates, y_sb, y_sw, w_bsb, w_bsw)


def matmul_tn(a, pieces, *, a_block, out_cols, relu2, name):
    s, m = a.shape
    widths = [p.shape[1] for p in pieces]
    n = sum(widths)
    n_pieces = len(pieces)
    ts = min(512 if n >= 4096 else 2048, s)
    n_steps = s // ts
    if out_cols is None:
        out_shape = jax.ShapeDtypeStruct((m // a_block, a_block, n), BF16)
        out_spec = pl.BlockSpec((None, a_block, n), lambda i, k: (i, 0, 0))
    else:
        out_shape = jax.ShapeDtypeStruct((n // out_cols, m, out_cols), BF16)
        out_spec = pl.BlockSpec((n // out_cols, a_block, out_cols), lambda i, k: (0, i, 0))

    def body(a_ref, *refs):
        b_refs, o_ref, acc = refs[:n_pieces], refs[n_pieces], refs[n_pieces + 1]
        k = pl.program_id(1)

        @pl.when(k == 0)
        def _():
            acc[...] = jnp.zeros_like(acc)

        av = a_ref[...]
        if relu2:
            af = jnp.maximum(av.astype(F32), 0.0)
            av = af * af
        av = av.astype(BF16)
        off = 0
        for b_ref in b_refs:
            width = b_ref.shape[1]
            acc[:, off:off + width] += _dot_tn(av, b_ref[...].astype(BF16))
            off += width

        @pl.when(k == n_steps - 1)
        def _():
            if out_cols is None:
                o_ref[...] = acc[...].astype(BF16)
            else:
                for j in range(n // out_cols):
                    o_ref[j] = acc[:, j * out_cols:(j + 1) * out_cols].astype(BF16)

    return pl.pallas_call(
        body, name=name, grid=(m // a_block, n_steps), out_shape=out_shape,
        in_specs=[pl.BlockSpec((ts, a_block), lambda i, k: (k, i))] + [
            pl.BlockSpec((ts, width), lambda i, k: (k, 0)) for width in widths],
        out_specs=out_spec, scratch_shapes=[pltpu.VMEM((a_block, n), F32)],
        compiler_params=_params("parallel", "arbitrary"),
    )(a, *pieces)


def matmul_tn_row_blocks(pieces, b, *, n_blocks, name):
    s, n = b.shape
    widths = [p.shape[1] for p in pieces]
    m = sum(widths)
    rows = m // n_blocks
    n_pieces = len(pieces)
    ts = min(512, s)
    n_steps = s // ts
    half = n_blocks // 2

    def body(*refs):
        p_refs, b_ref, o_ref, a_tile, acc = refs[:n_pieces], refs[n_pieces], refs[n_pieces + 1], refs[-2], refs[-1]
        i, k = pl.program_id(0), pl.program_id(1)

        @pl.when(k == 0)
        def _():
            acc[...] = jnp.zeros_like(acc)

        off = 0
        for p_ref, width in zip(p_refs, widths):
            a_tile[:, off:off + width] = p_ref[...]
            off += width
        bv = b_ref[...]
        for side in range(2):
            @pl.when(i == side)
            def _():
                for j in range(half):
                    col = (side * half + j) * rows
                    acc[j] += _dot_tn(a_tile[:, col:col + rows], bv)

        @pl.when(k == n_steps - 1)
        def _():
            o_ref[...] = acc[...].astype(BF16)

    return pl.pallas_call(
        body, name=name, grid=(2, n_steps), out_shape=jax.ShapeDtypeStruct((n_blocks, rows, n), BF16),
        in_specs=[pl.BlockSpec((ts, width), lambda i, k: (k, 0)) for width in widths] + [
            pl.BlockSpec((ts, n), lambda i, k: (k, 0))],
        out_specs=pl.BlockSpec((half, rows, n), lambda i, k: (i, 0, 0)),
        scratch_shapes=[pltpu.VMEM((ts, m), BF16), pltpu.VMEM((half, rows, n), F32)],
        compiler_params=_params("parallel", "arbitrary"),
    )(*pieces, b)


def _softplus(z):
    return jnp.maximum(z, 0.0) + jnp.log(1.0 + jnp.exp(-jnp.abs(z)))


def _suffix_sums(x, tri2):
    groups = x.shape[1] // LANES
    outs, run = [None] * groups, None
    for g in reversed(range(groups)):
        xg = x[:, g * LANES:(g + 1) * LANES]
        hi, lo = _split_bf16(xg)
        inner = _dot(jnp.concatenate([hi, lo], axis=1), tri2)
        outs[g] = inner if run is None else inner + run
        total = jnp.sum(xg, axis=1, keepdims=True)
        run = total if run is None else run + total
    return jnp.concatenate(outs, axis=1), run


def _head_mask(h):
    return (lax.broadcasted_iota(jnp.int32, (1, LANES), 1) // HEAD_DIM) == h


def _stack_heads(x):
    zero = jnp.zeros_like(x)
    return jnp.concatenate([jnp.where(_head_mask(0), x, zero), jnp.where(_head_mask(1), x, zero)], axis=0)


def _unstack_heads(r, t):
    return jnp.where(_head_mask(0), r[:t], r[t:])


def _sb_positions(q0, tk):
    row = lax.broadcasted_iota(jnp.int32, (2 * SB_TQ, tk), 0)
    col = lax.broadcasted_iota(jnp.int32, (2 * SB_TQ, tk), 1)
    return q0 + jnp.where(row >= SB_TQ, row - SB_TQ, row), col


def _sb_first_key(q0):
    return pl.multiple_of(jnp.maximum(q0 + SB_TQ - SB_TK1, 0), SB_TQ)


def _sb_next_key(k_prev):
    return pl.multiple_of(jnp.maximum(k_prev - SB_TK, 0), SB_TQ)


def _sb_rows(q0):
    return pl.ds(pl.multiple_of(2 * q0, 2 * SB_TQ), 2 * SB_TQ)


def _sb_keep(live):
    return lambda x: jnp.where(live, x, 0.0)


def _sb_keep_first(q0, k0, interior):
    if not interior:
        tpos, col = _sb_positions(q0, SB_TK1)
        return _sb_keep(k0 + col < tpos)
    row = lax.broadcasted_iota(jnp.int32, (2 * SB_TQ, SB_TQ), 0)
    own = lax.broadcasted_iota(jnp.int32, (2 * SB_TQ, SB_TQ), 1) < jnp.where(row >= SB_TQ, row - SB_TQ, row)
    past = SB_TK1 - SB_TQ
    return lambda x: jnp.concatenate([x[:, :past], jnp.where(own, x[:, past:], 0.0)], axis=1)


SB_EDGE_TILES = 8
SB_SCAN_GROUP = 4


def _sb_first_pass(nq, first):
    edge = min(nq, SB_EDGE_TILES)
    lax.fori_loop(0, edge, first(False), 0, unroll=4)
    lax.fori_loop(edge, nq, first(True), 0, unroll=8)


def _sb_scan_tiles(c_all, nq, more):
    rows = SB_SCAN_GROUP * 2 * SB_TQ

    def group(g, carry):
        @pl.when(jnp.max(c_all[pl.ds(pl.multiple_of(g * rows, rows), rows), :]) > SB_CUTOFF)
        def _():
            lax.fori_loop(g * SB_SCAN_GROUP, (g + 1) * SB_SCAN_GROUP, more, 0)

        return carry

    lax.fori_loop(0, nq // SB_SCAN_GROUP, group, 0)


def sb_attn_fwd(proj, tri2, *, name, ride=()):
    s = proj.shape[0]
    nq = s // SB_TQ
    n_pairs = SB_WIDTH // LANES

    def body(q_ref, k_ref, v_ref, tri_ref, o_ref, c_all):
        def block(qh, k0, tk, keep, c):
            z = _dot_nt(qh, k_ref[pl.ds(k0, tk), :])
            sp = _softplus(z)
            tail, total = _suffix_sums(keep(-sp), tri_ref[...])
            w = keep(jnp.exp(z - sp + tail + c))
            return _dot(w.astype(BF16), v_ref[pl.ds(k0, tk), :]), c + total

        def load_q(q0):
            return _stack_heads(q_ref[pl.ds(q0, SB_TQ), :]) * SCALE

        def first(interior):
            def run(qb, carry):
                q0 = pl.multiple_of(qb * SB_TQ, SB_TQ)
                k0 = _sb_first_key(q0)
                acc, c = block(load_q(q0), k0, SB_TK1, _sb_keep_first(q0, k0, interior), jnp.zeros((2 * SB_TQ, 1), F32))
                o_ref[pl.ds(q0, SB_TQ), :] = _unstack_heads(acc, SB_TQ)
                c_all[_sb_rows(q0), :] = jnp.broadcast_to(jnp.where(k0 > 0, c, NEG), (2 * SB_TQ, LANES))
                return carry

            return run

        _sb_first_pass(nq, first)

        @pl.when(jnp.max(c_all[...]) > SB_CUTOFF)
        def _():
            def more(qb, carry):
                q0 = pl.multiple_of(qb * SB_TQ, SB_TQ)
                c0 = c_all[_sb_rows(q0), 0:1]

                @pl.when(jnp.max(c0) > SB_CUTOFF)
                def _():
                    qh = load_q(q0)
                    _, col = _sb_positions(q0, SB_TK)

                    def cond(st):
                        return jnp.logical_and(st[0] > 0, st[3] > SB_CUTOFF)

                    def step(st):
                        k_prev, c, acc, _ = st
                        k0 = _sb_next_key(k_prev)
                        part, c = block(qh, k0, SB_TK, _sb_keep(k0 + col < k_prev), c)
                        return k0, c, acc + part, jnp.max(c)

                    st = lax.while_loop(cond, step, (_sb_first_key(q0), c0, jnp.zeros((2 * SB_TQ, LANES), F32),
                                                     jnp.max(c0)))
                    o_ref[pl.ds(q0, SB_TQ), :] += _unstack_heads(st[2], SB_TQ)

                return carry

            _sb_scan_tiles(c_all, nq, more)

    def col_spec(j):
        return pl.BlockSpec((s, LANES), lambda p: (0, j * n_pairs + p))

    (o,), rides = _call(
        body, name=name, grid=(n_pairs,), out_shape=(jax.ShapeDtypeStruct((s, SB_WIDTH), F32),),
        in_specs=[col_spec(0), col_spec(1), col_spec(2), pl.BlockSpec((2 * LANES, LANES), lambda p: (0, 0))],
        out_specs=(pl.BlockSpec((s, LANES), lambda p: (0, p)),), scratch_shapes=[pltpu.VMEM((2 * s, LANES), F32)],
        args=(proj, proj, proj, tri2), ride=ride)
    return o, rides


def sb_attn_bwd(proj, tri2, o, do, *, name, ride=()):
    s = proj.shape[0]
    nq = s // SB_TQ
    n_pairs = SB_WIDTH // LANES

    def body(q_ref, k_ref, v_ref, tri_ref, o_ref, do_ref, dq_ref, dk_ref, dv_ref, dq_acc, dk_acc, dv_acc, c_all, e_all):
        dk_acc[...] = jnp.zeros_like(dk_acc)
        dv_acc[...] = jnp.zeros_like(dv_acc)

        def load(q0):
            qh = _stack_heads(q_ref[pl.ds(q0, SB_TQ), :]) * SCALE
            doh_b = _stack_heads(do_ref[pl.ds(q0, SB_TQ), :])
            ov = o_ref[pl.ds(q0, SB_TQ), :]
            dd = jnp.sum(doh_b.astype(F32) * jnp.concatenate([ov, ov], axis=0), axis=1, keepdims=True)
            return qh, doh_b, dd

        def block(qh, doh_b, dd, k0, tk, keep, c, ce):
            kt = k_ref[pl.ds(k0, tk), :]
            z = _dot_nt(qh, kt)
            sp = _softplus(z)
            lb = z - sp
            tail, total = _suffix_sums(keep(-sp), tri_ref[...])
            wb = keep(jnp.exp(lb + tail + c)).astype(BF16)
            e = wb.astype(F32) * _dot_nt(doh_b, v_ref[pl.ds(k0, tk), :])
            e_tail, e_total = _suffix_sums(e, tri_ref[...])
            dzb = keep(e - jnp.exp(lb) * (dd - ce - e_tail)).astype(BF16)
            dk_acc[pl.ds(k0, tk), :] += _dot_tn(dzb, qh)
            dv_acc[pl.ds(k0, tk), :] += _dot_tn(wb, doh_b)
            return _dot(dzb, kt), c + total, ce + e_total

        def first(interior):
            def run(qb, carry):
                q0 = pl.multiple_of(qb * SB_TQ, SB_TQ)
                qh, doh_b, dd = load(q0)
                k0 = _sb_first_key(q0)
                zero = jnp.zeros((2 * SB_TQ, 1), F32)
                dq, c, ce = block(qh, doh_b, dd, k0, SB_TK1, _sb_keep_first(q0, k0, interior), zero, zero)
                dq_acc[pl.ds(q0, SB_TQ), :] = _unstack_heads(dq, SB_TQ)
                c_all[_sb_rows(q0), :] = jnp.broadcast_to(jnp.where(k0 > 0, c, NEG), (2 * SB_TQ, LANES))
                e_all[_sb_rows(q0), :] = jnp.broadcast_to(ce, (2 * SB_TQ, LANES))
                return carry

            return run

        _sb_first_pass(nq, first)

        @pl.when(jnp.max(c_all[...]) > SB_CUTOFF)
        def _():
            def more(qb, carry):
                q0 = pl.multiple_of(qb * SB_TQ, SB_TQ)
                c0 = c_all[_sb_rows(q0), 0:1]

                @pl.when(jnp.max(c0) > SB_CUTOFF)
                def _():
                    qh, doh_b, dd = load(q0)
                    _, col = _sb_positions(q0, SB_TK)

                    def cond(st):
                        return jnp.logical_and(st[0] > 0, st[4] > SB_CUTOFF)

                    def step(st):
                        k_prev, c, ce, dq, _ = st
                        k0 = _sb_next_key(k_prev)
                        part, c, ce = block(qh, doh_b, dd, k0, SB_TK, _sb_keep(k0 + col < k_prev), c, ce)
                        return k0, c, ce, dq + part, jnp.max(c)

                    st = lax.while_loop(cond, step, (_sb_first_key(q0), c0, e_all[_sb_rows(q0), 0:1],
                                                     jnp.zeros((2 * SB_TQ, LANES), F32), jnp.max(c0)))
                    dq_acc[pl.ds(q0, SB_TQ), :] += _unstack_heads(st[3], SB_TQ)

                return carry

            _sb_scan_tiles(c_all, nq, more)

        dq_ref[...] = (dq_acc[...] * SCALE).astype(BF16)
        dk_ref[...] = dk_acc[...].astype(BF16)
        dv_ref[...] = dv_acc[...].astype(BF16)

    def col_spec(j):
        return pl.BlockSpec((s, LANES), lambda p: (0, j * n_pairs + p))

    pair = pl.BlockSpec((s, LANES), lambda p: (0, p))
    (dq, dk, dv), rides = _call(
        body, name=name, grid=(n_pairs,), out_shape=(jax.ShapeDtypeStruct((s, SB_WIDTH), BF16),) * 3,
        in_specs=[col_spec(0), col_spec(1), col_spec(2), pl.BlockSpec((2 * LANES, LANES), lambda p: (0, 0)), pair, pair],
        out_specs=(pair, pair, pair),
        scratch_shapes=[pltpu.VMEM((s, LANES), F32)] * 3 + [pltpu.VMEM((2 * s, LANES), F32)] * 2,
        args=(proj, proj, proj, tri2, o, do), ride=ride)
    return dq, dk, dv, rides


def _lane_lo():
    return lax.broadcasted_iota(jnp.int32, (1, LANES), 1) < HEAD_DIM


def _swap_halves(x):
    return pltpu.roll(x, HEAD_DIM, 1)


def _rot_half(y):
    first = (lax.broadcasted_iota(jnp.int32, (1, LANES), 1) % HEAD_DIM) < (HEAD_DIM // 2)
    return jnp.where(first, pltpu.roll(y, LANES - HEAD_DIM // 2, 1), pltpu.roll(y, HEAD_DIM // 2, 1))


def _head_mean(v, avg):
    hi, lo = _split_bf16(v)
    return _dot(hi, avg) + _dot(lo, avg)


def _head_avg_matrix():
    lane = jnp.arange(LANES) // HEAD_DIM
    return ((lane[:, None] == lane[None, :]).astype(F32) * (1.0 / HEAD_DIM)).astype(BF16)


def swa_prep_fwd(proj, cos_p, sin_p, gq, gk, *, name):
    s = proj.shape[0]
    tm = min(512, s)
    q_blk = (3 * SB_WIDTH) // SWA_Q_WIDTH
    k_blk = (3 * SB_WIDTH + SWA_Q_WIDTH) // LANES

    def body(q_ref, k_ref, cos_ref, sin_ref, gq_ref, gk_ref, avg_ref, qn_ref, kn_ref):
        cosv, sinv, avg = cos_ref[...], sin_ref[...], avg_ref[...]

        def norm_rope(xv, g):
            y = (xv * lax.rsqrt(_head_mean(xv * xv, avg) + NORM_EPS)) * g
            return y * cosv + _rot_half(y) * sinv

        for j in range(SWA_Q_WIDTH // LANES):
            sl = slice(j * LANES, (j + 1) * LANES)
            qn_ref[:, sl] = norm_rope(q_ref[:, sl].astype(F32), gq_ref[...]).astype(BF16)
        kn_ref[...] = norm_rope(k_ref[...].astype(F32), gk_ref[...]).astype(BF16)

    row = lambda i: (i, 0)
    fixed = lambda i: (0, 0)
    return pl.pallas_call(
        body, name=name, grid=(s // tm,),
        out_shape=(jax.ShapeDtypeStruct((s, SWA_Q_WIDTH), BF16), jax.ShapeDtypeStruct((s, LANES), BF16)),
        in_specs=[pl.BlockSpec((tm, SWA_Q_WIDTH), lambda i: (i, q_blk)), pl.BlockSpec((tm, LANES), lambda i: (i, k_blk)),
                  pl.BlockSpec((tm, LANES), row), pl.BlockSpec((tm, LANES), row),
                  pl.BlockSpec((1, LANES), fixed), pl.BlockSpec((1, LANES), fixed), pl.BlockSpec((LANES, LANES), fixed)],
        out_specs=(pl.BlockSpec((tm, SWA_Q_WIDTH), row), pl.BlockSpec((tm, LANES), row)),
        compiler_params=_params("parallel"),
    )(proj, proj, cos_p, sin_p, gq, gk, _head_avg_matrix())


def swa_prep_bwd(proj, cos_p, sin_p, gq, gk, dqn, dkn, dv, *, name):
    s = proj.shape[0]
    tm = min(512, s)
    q_blk = (3 * SB_WIDTH) // SWA_Q_WIDTH
    k_blk = (3 * SB_WIDTH + SWA_Q_WIDTH) // LANES

    def body(q_ref, k_ref, cos_ref, sin_ref, gq_ref, gk_ref, avg_ref, dqn_ref, dkn_ref, dv_ref, dq_ref, dk_ref, dvb_ref,
             dgq_ref, dgk_ref):
        @pl.when(pl.program_id(0) == 0)
        def _():
            dgq_ref[...] = jnp.zeros_like(dgq_ref)
            dgk_ref[...] = jnp.zeros_like(dgk_ref)

        cosv, sinv, avg = cos_ref[...], sin_ref[...], avg_ref[...]

        def bwd(xv, g, dout):
            dy = dout * cosv + _rot_half(dout * sinv)
            r = lax.rsqrt(_head_mean(xv * xv, avg) + NORM_EPS)
            dyg = dy * g
            dx = r * dyg - xv * ((r * r * r) * _head_mean(dyg * xv, avg))
            return dx, jnp.sum(dy * (xv * r), axis=0, keepdims=True)

        for j in range(SWA_Q_WIDTH // LANES):
            sl = slice(j * LANES, (j + 1) * LANES)
            dx, dg = bwd(q_ref[:, sl].astype(F32), gq_ref[...], dqn_ref[:, sl])
            dq_ref[:, sl] = dx.astype(BF16)
            dgq_ref[:, sl] += dg
        dx, dg = bwd(k_ref[...].astype(F32), gk_ref[...], dkn_ref[...])
        dk_ref[...] = dx.astype(BF16)
        dgk_ref[...] += dg
        dvb_ref[...] = dv_ref[...].astype(BF16)

    row = lambda i: (i, 0)
    fixed = lambda i: (0, 0)
    lane_row = pl.BlockSpec((tm, LANES), row)
    return pl.pallas_call(
        body, name=name, grid=(s // tm,),
        out_shape=(jax.ShapeDtypeStruct((s, SWA_Q_WIDTH), BF16), jax.ShapeDtypeStruct((s, LANES), BF16),
                   jax.ShapeDtypeStruct((s, LANES), BF16),
                   jax.ShapeDtypeStruct((1, SWA_Q_WIDTH), F32), jax.ShapeDtypeStruct((1, LANES), F32)),
        in_specs=[pl.BlockSpec((tm, SWA_Q_WIDTH), lambda i: (i, q_blk)), pl.BlockSpec((tm, LANES), lambda i: (i, k_blk)),
                  lane_row, lane_row, pl.BlockSpec((1, LANES), fixed), pl.BlockSpec((1, LANES), fixed),
                  pl.BlockSpec((LANES, LANES), fixed), pl.BlockSpec((tm, SWA_Q_WIDTH), row), lane_row, lane_row],
        out_specs=(pl.BlockSpec((tm, SWA_Q_WIDTH), row), lane_row, lane_row,
                   pl.BlockSpec((1, SWA_Q_WIDTH), fixed), pl.BlockSpec((1, LANES), fixed)),
        compiler_params=_params("arbitrary"),
    )(proj, proj, cos_p, sin_p, gq, gk, _head_avg_matrix(), dqn, dkn, dv)


def _swa_kv_copies(k_ref, v_ref, kg_ref, vg_ref, second_kv):
    s = k_ref.shape[0]
    rows = min(512, s)
    keep = jnp.logical_xor(_lane_lo(), second_kv)

    def chunk(r, carry):
        sl = pl.ds(pl.multiple_of(r * rows, rows), rows)
        for src, dst in ((k_ref, kg_ref), (v_ref, vg_ref)):
            f = src[sl, :].astype(F32)
            dst[sl, :] = jnp.where(keep, f, _swap_halves(f)).astype(BF16)
        return carry

    lax.fori_loop(0, s // rows, chunk, 0)


def _swa_tile(i, kg_ref, vg_ref):
    q0 = pl.multiple_of(i * SWA_TQ, SWA_TQ)
    k0 = pl.multiple_of(jnp.maximum(i - 1, 0) * SWA_TQ, SWA_TQ)
    kg = kg_ref[pl.ds(k0, SWA_TK), :]
    vg = vg_ref[pl.ds(k0, SWA_TK), :]
    row = lax.broadcasted_iota(jnp.int32, (2 * SWA_TQ, SWA_TK), 0)
    tpos = q0 + jnp.where(row >= SWA_TQ, row - SWA_TQ, row)
    spos = k0 + lax.broadcasted_iota(jnp.int32, (2 * SWA_TQ, SWA_TK), 1)
    valid = jnp.logical_and(spos <= tpos, spos > tpos - WINDOW)
    return q0, k0, kg, vg, valid


def _swa_probs(qh, kg, valid, sink):
    z = jnp.where(valid, _dot_nt(qh, kg) * SCALE, NEG)
    m = jnp.maximum(jnp.max(z, axis=1, keepdims=True), sink)
    pexp = jnp.exp(z - m)
    psink = jnp.exp(sink - m)
    inv = 1.0 / (jnp.sum(pexp, axis=1, keepdims=True) + psink)
    return pexp * inv, psink * inv


def _stacked_sink(sink_row):
    s0 = jnp.sum(jnp.where(_head_mask(0), sink_row, 0.0), axis=1, keepdims=True) * (1.0 / HEAD_DIM)
    s1 = jnp.sum(jnp.where(_head_mask(1), sink_row, 0.0), axis=1, keepdims=True) * (1.0 / HEAD_DIM)
    top = lax.broadcasted_iota(jnp.int32, (2 * SWA_TQ, 1), 0) < SWA_TQ
    return jnp.where(top, s0, s1)


def swa_attn_fwd(qn, kn, proj, sink_p, *, name, ride=()):
    s = qn.shape[0]
    nq = s // SWA_TQ
    n_pairs = SWA_Q_WIDTH // LANES
    v_blk = (3 * SB_WIDTH + SWA_Q_WIDTH + SWA_KV_WIDTH) // LANES

    def body(q_ref, k_ref, v_ref, s_ref, o_ref, kg_ref, vg_ref):
        _swa_kv_copies(k_ref, v_ref, kg_ref, vg_ref, (pl.program_id(0) // 2) == 1)
        sink = _stacked_sink(s_ref[...])

        def tile(i, carry):
            q0, _, kg, vg, valid = _swa_tile(i, kg_ref, vg_ref)
            probs, _ = _swa_probs(_stack_heads(q_ref[pl.ds(q0, SWA_TQ), :]), kg, valid, sink)
            o_ref[pl.ds(q0, SWA_TQ), :] = _unstack_heads(_dot(probs.astype(BF16), vg), SWA_TQ)
            return carry

        lax.fori_loop(0, nq, tile, 0, unroll=8)

    pair = pl.BlockSpec((s, LANES), lambda p: (0, p))
    whole = pl.BlockSpec((s, LANES), lambda p: (0, 0))
    (o,), rides = _call(
        body, name=name, grid=(n_pairs,), out_shape=(jax.ShapeDtypeStruct((s, SWA_Q_WIDTH), F32),),
        in_specs=[pair, whole, pl.BlockSpec((s, LANES), lambda p: (0, v_blk)),
                  pl.BlockSpec((None, 1, LANES), lambda p: (p, 0, 0))],
        out_specs=(pair,), scratch_shapes=[pltpu.VMEM((s, LANES), BF16)] * 2, args=(qn, kn, proj, sink_p), ride=ride)
    return o, rides


def swa_attn_bwd(qn, kn, proj, sink_p, o, do, *, name, ride=()):
    s = qn.shape[0]
    nq = s // SWA_TQ
    n_pairs = SWA_Q_WIDTH // LANES
    v_blk = (3 * SB_WIDTH + SWA_Q_WIDTH + SWA_KV_WIDTH) // LANES
    fold_rows = min(512, s)

    def body(q_ref, k_ref, v_ref, s_ref, o_ref, do_ref, dq_ref, dk_ref, dv_ref, ds_ref, acc_k, acc_v, kg_ref, vg_ref):
        p = pl.program_id(0)
        _swa_kv_copies(k_ref, v_ref, kg_ref, vg_ref, (p // 2) == 1)
        sink = _stacked_sink(s_ref[...])

        @pl.when(p % 2 == 0)
        def _():
            acc_k[...] = jnp.zeros_like(acc_k)
            acc_v[...] = jnp.zeros_like(acc_v)

        ds_ref[...] = jnp.zeros_like(ds_ref)

        def tile(i, carry):
            q0, k0, kg, vg, valid = _swa_tile(i, kg_ref, vg_ref)
            qh = _stack_heads(q_ref[pl.ds(q0, SWA_TQ), :])
            doh_b = _stack_heads(do_ref[pl.ds(q0, SWA_TQ), :])
            ov = o_ref[pl.ds(q0, SWA_TQ), :]
            delta = jnp.sum(doh_b.astype(F32) * jnp.concatenate([ov, ov], axis=0), axis=1, keepdims=True)
            probs, psink = _swa_probs(qh, kg, valid, sink)
            dz = probs * (_dot_nt(doh_b, vg) - delta)
            dzb = (dz * SCALE).astype(BF16)
            dq_ref[pl.ds(q0, SWA_TQ), :] = _unstack_heads(_dot(dzb, kg), SWA_TQ)
            acc_k[pl.ds(k0, SWA_TK), :] += _dot_tn(dzb, qh)
            acc_v[pl.ds(k0, SWA_TK), :] += _dot_tn(probs.astype(BF16), doh_b)
            pd = psink * delta
            ds_ref[...] -= jnp.where(_head_mask(0), jnp.sum(pd[:SWA_TQ], axis=0, keepdims=True),
                                     jnp.sum(pd[SWA_TQ:], axis=0, keepdims=True))
            return carry

        lax.fori_loop(0, nq, tile, 0, unroll=8)

        def fold_into(first_head):
            def fold(r, carry):
                rows = pl.ds(pl.multiple_of(r * fold_rows, fold_rows), fold_rows)
                for acc, out in ((acc_k, dk_ref), (acc_v, dv_ref)):
                    a = acc[rows, :]
                    both = a + _swap_halves(a)
                    if first_head:
                        out[rows, :] = jnp.where(_lane_lo(), both, 0.0)
                    else:
                        out[rows, :] = jnp.where(_lane_lo(), out[rows, :], both)
                return carry

            lax.fori_loop(0, s // fold_rows, fold, 0)

        @pl.when(p == 1)
        def _():
            fold_into(True)

        @pl.when(p == 3)
        def _():
            fold_into(False)

    pair = pl.BlockSpec((s, LANES), lambda p: (0, p))
    whole = pl.BlockSpec((s, LANES), lambda p: (0, 0))
    sink_spec = pl.BlockSpec((None, 1, LANES), lambda p: (p, 0, 0))
    (dq, dk, dv, dsink), rides = _call(
        body, name=name, grid=(n_pairs,),
        out_shape=(jax.ShapeDtypeStruct((s, SWA_Q_WIDTH), F32), jax.ShapeDtypeStruct((s, LANES), F32),
                   jax.ShapeDtypeStruct((s, LANES), F32), jax.ShapeDtypeStruct((n_pairs, 1, LANES), F32)),
        in_specs=[pair, whole, pl.BlockSpec((s, LANES), lambda p: (0, v_blk)), sink_spec, pair, pair],
        out_specs=(pair, whole, whole, sink_spec),
        scratch_shapes=[pltpu.VMEM((s, LANES), F32)] * 2 + [pltpu.VMEM((s, LANES), BF16)] * 2,
        args=(qn, kn, proj, sink_p, o, do), ride=ride)
    return dq, dk, dv, dsink, rides


def _rope_tables(s):
    inv_freq = 1.0 / (ROPE_THETA ** (jnp.arange(0, HEAD_DIM, 2, dtype=F32) / HEAD_DIM))
    ang = jnp.arange(s, dtype=F32)[:, None] * inv_freq[None, :]
    cos, sin = jnp.cos(ang), jnp.sin(ang)
    cos_p = jnp.tile(jnp.concatenate([cos, cos], axis=1), (1, LANES // HEAD_DIM))
    sin_p = jnp.tile(jnp.concatenate([-sin, sin], axis=1), (1, LANES // HEAD_DIM))
    return cos_p, sin_p


def _lane_tile(v, reps):
    return jnp.tile(v.reshape(1, -1), (1, reps))


def _natural(stack, w):
    n, r, c = stack.shape
    if MATRIX_NAMES[w] in ROW_SHARDED or w == W_IN:
        return stack.reshape(n * r, c)
    if w == W_UP:
        return stack
    return jnp.transpose(stack, (1, 0, 2)).reshape(r, n * c)


def _pack_small(tree):
    flat = jnp.concatenate([tree[n].reshape(-1) for n in SMALL_NAMES])
    rows = -(-flat.shape[0] // (8 * LANES)) * 8
    return jnp.pad(flat, (0, rows * LANES - flat.shape[0])).reshape(rows, LANES)


def _unpack_small(packed, shapes):
    flat, out, off = packed.reshape(-1), {}, 0
    for n in SMALL_NAMES:
        size = shapes[n][0] * shapes[n][1]
        out[n] = flat[off:off + size].reshape(shapes[n])
        off += size
    return out


def train_step(x, target, weights, mom_m, mom_v):
    s = x.shape[0]
    cos_p, sin_p = _rope_tables(s)
    tri = (jnp.arange(LANES)[:, None] > jnp.arange(LANES)[None, :]).astype(BF16)
    tri = jnp.concatenate([tri, tri], axis=0)
    local = {n: (jnp.swapaxes(t, 1, 2) if n == "w_in" else t) for n, t in weights.items()}
    local_m = {n: (jnp.swapaxes(t, 1, 2) if n == "w_in" else t) for n, t in mom_m.items()}
    local_v = {n: (jnp.swapaxes(t, 1, 2) if n == "w_in" else t) for n, t in mom_v.items()}
    shards = [[local[n][l].astype(BF16) for n in MATRIX_NAMES] for l in range(DEPTH)]
    core = lax.axis_index("c").astype(jnp.int32).reshape(1)
    chip = (2 * lax.axis_index("x") + lax.axis_index("y")).astype(jnp.int32).reshape(1)

    def gather(l, ws, rows=None, stacks=None):
        return GatherJob([shards[l][w] for w in ws], rows=rows, stacks=stacks)

    def halves(w):
        r = shards[0][w].shape[0] // 2
        return (0, r), (r, r)

    w_in = _natural(exchange_alone(gather(0, [W_IN]), name="gather_w_in0")[0], W_IN)
    saved = []
    for l in range(DEPTH):
        g_mix = weights["mix_norm_g"][l].reshape(1, D_MODEL)
        g_mlp = weights["mlp_norm_g"][l].reshape(1, D_MODEL)
        gq = _lane_tile(weights["q_norm_g"][l], LANES // HEAD_DIM)
        gk = _lane_tile(weights["k_norm_g"][l], LANES // HEAD_DIM)
        sink_p = jnp.repeat(weights["sinks"][l].reshape(SWA_Q_WIDTH // LANES, 2), HEAD_DIM, axis=1)
        sink_p = sink_p.reshape(SWA_Q_WIDTH // LANES, 1, LANES)
        (h, proj, gates), ((s_bsb, s_bsw, s_out),) = norm_matmul(
            x, g_mix, w_in, gate_split=ATTN_WIDTH, name="in_proj", ride=[gather(l, [W_BSB, W_BSW, W_OUT])])
        o_sb, ((s_up,),) = sb_attn_fwd(proj, tri, name="sb_fwd", ride=[gather(l, [W_UP])])
        qn, kn = swa_prep_fwd(proj, cos_p, sin_p, gq, gk, name="swa_prep")
        o_sw, ((s_down,),) = swa_attn_fwd(qn, kn, proj, sink_p, name="swa_fwd",
                                          ride=[gather(l, [W_DOWN], rows=halves(W_DOWN)[0])])
        more = l + 1 < DEPTH
        (x1, y_sb, y_sw, merged), rides = merge_out_fwd(
            x, o_sb, o_sw, gates, _natural(s_bsb, W_BSB), _natural(s_bsw, W_BSW), _natural(s_out, W_OUT),
            name="merge_out" if more else "merge_out_last",
            ride=[gather(l + 1, [W_IN], rows=halves(W_IN)[0])] if more else [])
        (h2, u), ((s_down,),) = norm_matmul(x1, g_mlp, s_up, gate_split=None, name="mlp_up",
                                            ride=[gather(l, [W_DOWN], rows=halves(W_DOWN)[1], stacks=[s_down])])
        mats = [w_in, _natural(s_bsb, W_BSB), _natural(s_bsw, W_BSW), _natural(s_out, W_OUT), s_up,
                _natural(s_down, W_DOWN)]
        if more:
            (x2,), ((s_in,),) = mlp_down_fwd(x1, u, mats[W_DOWN], name="mlp_down",
                                             ride=[gather(l + 1, [W_IN], rows=halves(W_IN)[1], stacks=rides[0])])
            w_in = _natural(s_in, W_IN)
        else:
            (x2,), _ = mlp_down_fwd(x1, u, mats[W_DOWN], name="mlp_down_last")
        saved.append(dict(x=x, h=h, proj=proj, gates=gates, o_sb=o_sb, qn=qn, kn=kn, o_sw=o_sw, y_sb=y_sb, y_sw=y_sw,
                          merged=merged, x1=x1, h2=h2, u=u, g_mix=g_mix, g_mlp=g_mlp, gq=gq, gk=gk, sink_p=sink_p,
                          mats=mats))
        x = x2

    dx, dxb, loss = loss_head(x, target, name="loss_head")

    shard_shapes = [local[n].shape[1:] for n in MATRIX_NAMES]
    parts = [lax.empty((DEPTH, N_CHIPS) + sh, BF16) for sh in shard_shapes]
    lands = [lax.empty((DEPTH, 3) + sh, BF16) for sh in shard_shapes]
    small_grads = {n: [None] * DEPTH for n in SMALL_NAMES}
    half = D_MODEL // 2

    def summed(l, ws, grads, landed):
        new = pair_sum(l, grads, landed, [parts[w] for w in ws], core, name="grad_pair_sum")
        for w, p in zip(ws, new):
            parts[w] = p

    def chip_job(items):
        return ChipJob(items, parts, lands)

    def landed_chip(job, outs):
        for w, a in zip(job.ws, outs):
            lands[w] = a

    in_pending = None
    for l in reversed(range(DEPTH)):
        a = saved[l]
        mats = a["mats"]
        in_jobs = [chip_job([(in_pending, W_IN, rows)]) for rows in halves(W_IN)] if in_pending is not None else []
        (du,), rides = mlp_bwd_up(dxb, a["u"], mats[W_DOWN], name="mlp_bwd_up" if in_jobs else "mlp_bwd_up_first",
                                  ride=in_jobs[:1])
        if in_jobs:
            landed_chip(in_jobs[0], rides[0])
            in_jobs[1] = chip_job([(in_pending, W_IN, halves(W_IN)[1])])
        dw_down = matmul_tn(a["u"], [dxb], a_block=half, out_cols=None, relu2=True, name="dw_down")
        dw_up = matmul_tn(a["h2"], [du], a_block=half, out_cols=du.shape[1] // N_DEV, relu2=False, name="dw_up")
        g_mlp_w = [dw_up, dw_down.reshape((N_DEV,) + shard_shapes[W_DOWN])]
        (dx1, dx1b, dg_mlp), rides = matmul_nt_norm_bwd(
            [du], mats[W_UP], a["x1"], a["g_mlp"], dx, name="mlp_bwd_norm" if in_jobs else "mlp_bwd_norm_first",
            ride=[PairJob(g_mlp_w)] + in_jobs[1:])
        if in_jobs:
            landed_chip(in_jobs[1], rides[1])
        summed(l, [W_UP, W_DOWN], g_mlp_w, rides[0])
        small_grads["mlp_norm_g"][l] = dg_mlp.reshape(D_MODEL)
        dw_out = matmul_tn(a["merged"], [dx1b], a_block=half, out_cols=None, relu2=False, name="dw_out")
        dy_sb, dy_sw, do_sb, do_sw, dgl = out_bwd(dx1b, mats[W_OUT], a["gates"], a["y_sb"], a["y_sw"],
                                                  mats[W_BSB], mats[W_BSW], name="out_bwd")
        dw_bsb = matmul_tn(a["o_sb"], [dy_sb], a_block=half, out_cols=D_MODEL // N_DEV, relu2=False, name="dw_branch_sb")
        dw_bsw = matmul_tn(a["o_sw"], [dy_sw], a_block=half, out_cols=D_MODEL // N_DEV, relu2=False, name="dw_branch_swa")
        g_mix_w = [dw_bsb, dw_bsw, dw_out.reshape((N_DEV,) + shard_shapes[W_OUT])]
        job = chip_job([(l, W_UP), (l, W_DOWN)])
        dq_sb, dk_sb, dv_sb, (outs, landed) = sb_attn_bwd(a["proj"], tri, a["o_sb"], do_sb, name="sb_bwd",
                                                         ride=[job, PairJob(g_mix_w)])
        landed_chip(job, outs)
        summed(l, [W_BSB, W_BSW, W_OUT], g_mix_w, landed)
        job = chip_job([(l, W_BSB), (l, W_BSW), (l, W_OUT)])
        dqn, dkn, dv_sw, dsink, (outs,) = swa_attn_bwd(a["qn"], a["kn"], a["proj"], a["sink_p"], a["o_sw"], do_sw,
                                                      name="swa_bwd", ride=[job])
        landed_chip(job, outs)
        dq_sw, dk_sw, dv_swb, dgq, dgk = swa_prep_bwd(a["proj"], cos_p, sin_p, a["gq"], a["gk"], dqn, dkn, dv_sw,
                                                      name="swa_prep_bwd")
        small_grads["q_norm_g"][l] = dgq.reshape(SWA_Q_WIDTH // HEAD_DIM, HEAD_DIM).sum(0)
        small_grads["k_norm_g"][l] = dgk.reshape(LANES // HEAD_DIM, HEAD_DIM).sum(0)
        small_grads["sinks"][l] = dsink[:, 0, ::HEAD_DIM].reshape(SWA_Q_WIDTH // HEAD_DIM)
        pieces = [dq_sb, dk_sb, dv_sb, dq_sw, dk_sw, dv_swb, dgl]
        g_in = [matmul_tn_row_blocks(pieces, a["h"], n_blocks=N_DEV, name="dw_in")]
        if l > 0:
            (dx, dxb, dg_mix), (landed,) = matmul_nt_norm_bwd(pieces, mats[W_IN], a["x"], a["g_mix"], dx1,
                                                             name="in_proj_bwd", ride=[PairJob(g_in)])
            summed(l, [W_IN], g_in, landed)
            in_pending = l
        else:
            summed(l, [W_IN], g_in, exchange_alone(PairJob(g_in), name="grad_pair_exchange_in0"))
            job = chip_job([(l, W_IN)])
            (dx, dxb, dg_mix), (outs,) = matmul_nt_norm_bwd(pieces, mats[W_IN], a["x"], a["g_mix"], dx1,
                                                           name="in_proj_bwd_last", ride=[job])
            landed_chip(job, outs)
        small_grads["mix_norm_g"][l] = dg_mix.reshape(D_MODEL)

    out_g, out_d, out_m, out_v = {}, {}, {}, {}
    for i, n in enumerate(MATRIX_NAMES):
        outs = reduce_adamw(parts[i], lands[i], chip, local[n], local_m[n], local_v[n], name="adamw_" + n)
        if n == "w_in":
            outs = [jnp.swapaxes(t, 1, 2) for t in outs]
        out_g[n], out_d[n], out_m[n], out_v[n] = outs
    small_shapes = {n: weights[n].shape for n in SMALL_NAMES}
    small_all = gather_small(_pack_small({n: jnp.stack(v) for n, v in small_grads.items()}), name="gather_small_grads")
    sg, sd, sm, sv = small_adamw(small_all, _pack_small(weights), _pack_small(mom_m), _pack_small(mom_v),
                                 name="small_adamw")
    for tree, packed_small in ((out_g, sg), (out_d, sd), (out_m, sm), (out_v, sv)):
        tree.update(_unpack_small(packed_small, small_shapes))
    return loss, dx, (out_g, out_d, out_m, out_v)


def kernel(x, mix_norm_g, w_in, q_norm_g, k_norm_g, sinks, w_branch_sb, w_branch_swa, w_out, mlp_norm_g, w_up, w_down, loss_target, m_mix_norm_g, m_w_in, m_q_norm_g, m_k_norm_g, m_sinks, m_w_branch_sb, m_w_branch_swa, m_w_out, m_mlp_norm_g, m_w_up, m_w_down, v_mix_norm_g, v_w_in, v_q_norm_g, v_k_norm_g, v_sinks, v_w_branch_sb, v_w_branch_swa, v_w_out, v_mlp_norm_g, v_w_up, v_w_down):
    weights = dict(mix_norm_g=mix_norm_g, w_in=w_in, q_norm_g=q_norm_g, k_norm_g=k_norm_g, sinks=sinks,
                   w_branch_sb=w_branch_sb, w_branch_swa=w_branch_swa, w_out=w_out, mlp_norm_g=mlp_norm_g, w_up=w_up,
                   w_down=w_down)
    mom_m = dict(mix_norm_g=m_mix_norm_g, w_in=m_w_in, q_norm_g=m_q_norm_g, k_norm_g=m_k_norm_g, sinks=m_sinks,
                 w_branch_sb=m_w_branch_sb, w_branch_swa=m_w_branch_swa, w_out=m_w_out, mlp_norm_g=m_mlp_norm_g,
                 w_up=m_w_up, w_down=m_w_down)
    mom_v = dict(mix_norm_g=v_mix_norm_g, w_in=v_w_in, q_norm_g=v_q_norm_g, k_norm_g=v_k_norm_g, sinks=v_sinks,
                 w_branch_sb=v_w_branch_sb, w_branch_swa=v_w_branch_swa, w_out=v_w_out, mlp_norm_g=v_mlp_norm_g,
                 w_up=v_w_up, w_down=v_w_down)
    loss_part, grad_x, outs = train_step(x[0], loss_target[0], weights, mom_m, mom_v)
    loss = lax.psum(loss_part[0, 0], MESH_AXES)
    return (loss, grad_x[None], *[outs[0][n] for n in WEIGHT_ORDER], *[outs[1][n] for n in WEIGHT_ORDER],
            *[outs[2][n] for n in WEIGHT_ORDER], *[outs[3][n] for n in WEIGHT_ORDER])
```

```python
import math

import jax
import jax.numpy as jnp
from jax import lax
from jax.experimental import pallas as pl
from jax.experimental.pallas import tpu as pltpu

F32 = jnp.float32
BF16 = jnp.bfloat16

DEPTH = 4
D_MODEL = 1024
HEAD_DIM = 64
LANES = 128
WINDOW = 128
SB_WIDTH = 512
SWA_Q_WIDTH = 512
SWA_KV_WIDTH = 128
ATTN_WIDTH = 3 * SB_WIDTH + SWA_Q_WIDTH + 2 * SWA_KV_WIDTH
ROPE_THETA = 10000.0
NORM_EPS = 1e-6
SCALE = HEAD_DIM ** -0.5
NEG = -1e30
N_DEV = 8
N_CHIPS = 4

ADAM_LR = 0.001
ADAM_B1 = 0.9
ADAM_B2 = 0.999
ADAM_EPS = 1e-08
ADAM_WD = 0.01
ADAM_STEP = 10

SB_TQ = 128
SB_TK1 = 384
SB_TK = 256
SB_CUTOFF = -88.0
SWA_TQ = 128
SWA_TK = 256
ROW_TILE = 512
VMEM_LIMIT = 56 * 1024 * 1024

MATRIX_NAMES = ("w_in", "w_branch_sb", "w_branch_swa", "w_out", "w_up", "w_down")
W_IN, W_BSB, W_BSW, W_OUT, W_UP, W_DOWN = range(6)
ROW_SHARDED = ("w_out", "w_down")
SMALL_NAMES = ("mix_norm_g", "q_norm_g", "k_norm_g", "sinks", "mlp_norm_g")
WEIGHT_ORDER = ("mix_norm_g", "w_in", "q_norm_g", "k_norm_g", "sinks", "w_branch_sb", "w_branch_swa", "w_out",
                "mlp_norm_g", "w_up", "w_down")
MESH_AXES = ("x", "y", "c")

ANY = pl.BlockSpec(memory_space=pl.ANY)
MESH = pl.DeviceIdType.MESH


def _params(*sem):
    return pltpu.CompilerParams(dimension_semantics=sem, vmem_limit_bytes=VMEM_LIMIT)


def _dot(a, b):
    return jnp.dot(a, b, preferred_element_type=F32)


def _dot_nt(a, b):
    return lax.dot_general(a, b, (((1,), (1,)), ((), ())), preferred_element_type=F32)


def _dot_tn(a, b):
    return lax.dot_general(a, b, (((0,), (0,)), ((), ())), preferred_element_type=F32)


def _split_bf16(x):
    hi = lax.bitcast_convert_type(lax.bitcast_convert_type(x, jnp.uint32) & jnp.uint32(0xFFFF0000), F32)
    return hi.astype(BF16), (x - hi).astype(BF16)


def _rsqrt_ms(x):
    return lax.rsqrt(jnp.mean(x * x, axis=-1, keepdims=True) + NORM_EPS)


def _place():
    return lax.axis_index("x"), lax.axis_index("y"), lax.axis_index("c")


class _Gather:
    def __init__(self, x_refs, out_refs, send_sems, recv_sems, local_sems, rows=None):
        self.x_refs, self.out_refs = x_refs, out_refs
        self.send_sems, self.recv_sems, self.local_sems = send_sems, recv_sems, local_sems
        self.n = len(x_refs)
        self.rows = rows
        x, y, c = _place()
        self.c = c
        self.me, self.sibling = (x, y, c), (x, y, 1 - c)
        self.chips = [(1 - x, y), (x, 1 - y), (1 - x, 1 - y)]

    def _part(self, ref):
        return ref if self.rows is None else ref.at[pl.ds(*self.rows)]

    def _slot(self, w, blk):
        return self._part(self.out_refs[w].at[4 * blk[0] + 2 * blk[1] + blk[2]])

    def _copy(self, k, w, blk, to, own=False):
        dst = self._slot(w, blk)
        return pltpu.make_async_remote_copy(
            src_ref=self._part(self.x_refs[w]) if own else dst, dst_ref=dst, send_sem=self.send_sems.at[k, w],
            recv_sem=self.recv_sems.at[k, w], device_id=to, device_id_type=MESH)

    def _mine(self, w):
        return pltpu.make_async_copy(self._part(self.x_refs[w]), self._slot(w, self.me), self.local_sems.at[w])

    def _first(self, w):
        return [self._copy(0, w, self.me, self.sibling, own=True)] + [
            self._copy(1 + j, w, self.me, (*chip, self.c), own=True) for j, chip in enumerate(self.chips)]

    def _passed(self, j, w):
        return self._copy(4 + j, w, (*self.chips[j], self.c), self.sibling)

    def start(self):
        for w in range(self.n):
            self._mine(w).start()
            for cp in self._first(w):
                cp.start()

    def relay(self):
        for j, chip in enumerate(self.chips):
            for w in range(self.n):
                self._copy(1 + j, w, (*chip, self.c), self.me).wait_recv()
                self._passed(j, w).start()

    def finish(self):
        for w in range(self.n):
            self._copy(0, w, self.sibling, self.me).wait_recv()
            for j, chip in enumerate(self.chips):
                self._copy(4 + j, w, (*chip, 1 - self.c), self.me).wait_recv()
            for cp in self._first(w):
                cp.wait_send()
            for j in range(3):
                self._passed(j, w).wait_send()
            self._mine(w).wait()


class GatherJob:
    def __init__(self, shards, rows=None, stacks=None):
        n = len(shards)
        self.n, self.rows = n, rows
        self.inputs = list(shards) + (list(stacks) if stacks is not None else [])
        self.out_shapes = [jax.ShapeDtypeStruct((N_DEV,) + s.shape, s.dtype) for s in shards]
        self.aliases = {n + i: i for i in range(n)} if stacks is not None else {}
        self.scratch = [pltpu.SemaphoreType.DMA((7, n)), pltpu.SemaphoreType.DMA((7, n)),
                        pltpu.SemaphoreType.DMA((n,))]

    def bind(self, in_refs, out_refs, scratch_refs):
        return _Gather(in_refs[:self.n], out_refs, *scratch_refs, rows=self.rows)


class _Copies:
    def __init__(self, copies):
        self.copies = copies

    def start(self):
        for cp in self.copies:
            cp.start()

    def relay(self):
        pass

    def finish(self):
        for cp in self.copies:
            cp.wait_recv()
        for cp in self.copies:
            cp.wait_send()


class ChipJob:
    def __init__(self, items, parts, lands):
        self.ws = sorted({item[1] for item in items})
        n = len(self.ws)
        self.items = [(item[0], self.ws.index(item[1]), item[2] if len(item) > 2 else None) for item in items]
        self.inputs = [parts[w] for w in self.ws] + [lands[w] for w in self.ws]
        self.out_shapes = [jax.ShapeDtypeStruct(lands[w].shape, lands[w].dtype) for w in self.ws]
        self.aliases = {n + i: i for i in range(n)}
        self.scratch = [pltpu.SemaphoreType.DMA((3, n)), pltpu.SemaphoreType.DMA((3, n))]

    def bind(self, in_refs, out_refs, scratch_refs):
        send_sems, recv_sems = scratch_refs
        x, y, c = _place()
        chips = [(1 - x, y), (x, 1 - y), (1 - x, 1 - y)]

        def part(ref, rows):
            return ref if rows is None else ref.at[pl.ds(*rows)]

        return _Copies([pltpu.make_async_remote_copy(
            src_ref=part(in_refs[i].at[layer, 2 * px + py], rows), dst_ref=part(out_refs[i].at[layer, j], rows),
            send_sem=send_sems.at[j, i], recv_sem=recv_sems.at[j, i], device_id=(px, py, c), device_id_type=MESH)
            for layer, i, rows in self.items for j, (px, py) in enumerate(chips)])


class PairJob:
    def __init__(self, grads):
        n = len(grads)
        self.inputs = list(grads)
        self.out_shapes = [jax.ShapeDtypeStruct((N_CHIPS,) + g.shape[1:], g.dtype) for g in grads]
        self.aliases = {}
        self.scratch = [pltpu.SemaphoreType.DMA((N_CHIPS, n)), pltpu.SemaphoreType.DMA((N_CHIPS, n))]

    def bind(self, in_refs, out_refs, scratch_refs):
        send_sems, recv_sems = scratch_refs
        x, y, c = _place()
        return _Copies([pltpu.make_async_remote_copy(
            src_ref=in_refs[w].at[2 * k + (1 - c)], dst_ref=out_refs[w].at[k], send_sem=send_sems.at[k, w],
            recv_sem=recv_sems.at[k, w], device_id=(x, y, 1 - c), device_id_type=MESH)
            for w in range(len(in_refs)) for k in range(N_CHIPS)])


def _call(body, *, name, grid, in_specs, out_specs, out_shape, args, scratch_shapes=(), ride=()):
    out_specs, out_shape, in_specs = tuple(out_specs), tuple(out_shape), list(in_specs)
    scratch_shapes = list(scratch_shapes)
    order = ("arbitrary",) * len(grid)
    if not ride:
        outs = pl.pallas_call(body, name=name, grid=grid, in_specs=in_specs, out_specs=out_specs, out_shape=out_shape,
                              scratch_shapes=scratch_shapes, compiler_params=_params(*order))(*args)
        return tuple(outs), []
    n_in, n_out, n_scr = len(in_specs), len(out_specs), len(scratch_shapes)
    n_steps = math.prod(grid)
    relay_early = n_steps >= 8
    relay_at = n_steps - n_steps // 4 if relay_early else n_steps - 1

    def split(refs, pos, counts):
        groups = []
        for k in counts:
            groups.append(refs[pos:pos + k])
            pos += k
        return groups, pos

    def wrapped(*refs):
        ins, pos = refs[:n_in], n_in
        job_in, pos = split(refs, pos, [len(j.inputs) for j in ride])
        outs, pos = refs[pos:pos + n_out], pos + n_out
        job_out, pos = split(refs, pos, [len(j.out_shapes) for j in ride])
        scr, pos = refs[pos:pos + n_scr], pos + n_scr
        job_scr, pos = split(refs, pos, [len(j.scratch) for j in ride])
        bound = [j.bind(i, o, s) for j, i, o, s in zip(ride, job_in, job_out, job_scr)]
        step = pl.program_id(0)
        for axis in range(1, len(grid)):
            step = step * grid[axis] + pl.program_id(axis)

        @pl.when(step == 0)
        def _():
            for b in bound:
                b.start()

        if relay_early:
            @pl.when(step == relay_at)
            def _():
                for b in bound:
                    b.relay()

        body(*ins, *outs, *scr)

        @pl.when(step == n_steps - 1)
        def _():
            if not relay_early:
                for b in bound:
                    b.relay()
            for b in bound:
                b.finish()

    aliases, in_pos, out_pos = {}, n_in, n_out
    for j in ride:
        aliases.update({in_pos + i: out_pos + o for i, o in j.aliases.items()})
        in_pos += len(j.inputs)
        out_pos += len(j.out_shapes)
    results = pl.pallas_call(
        wrapped, name=name, grid=grid, in_specs=in_specs + [ANY] * (in_pos - n_in),
        out_specs=out_specs + (ANY,) * (out_pos - n_out),
        out_shape=out_shape + tuple(s for j in ride for s in j.out_shapes),
        scratch_shapes=scratch_shapes + [s for j in ride for s in j.scratch], input_output_aliases=aliases,
        compiler_params=pltpu.CompilerParams(dimension_semantics=order, vmem_limit_bytes=VMEM_LIMIT,
                                             has_side_effects=True),
    )(*args, *[a for j in ride for a in j.inputs])
    job_results, pos = split(list(results), n_out, [len(j.out_shapes) for j in ride])
    return tuple(results[:n_out]), job_results


def exchange_alone(job, *, name):
    n_in, n_out = len(job.inputs), len(job.out_shapes)

    def body(*refs):
        b = job.bind(refs[:n_in], refs[n_in:n_in + n_out], refs[n_in + n_out:])
        b.start()
        b.relay()
        b.finish()

    return list(pl.pallas_call(
        body, name=name, out_shape=tuple(job.out_shapes), in_specs=[ANY] * n_in, out_specs=(ANY,) * n_out,
        scratch_shapes=job.scratch, input_output_aliases=job.aliases,
        compiler_params=pltpu.CompilerParams(has_side_effects=True),
    )(*job.inputs))


PAIR_SUM_CHUNKS = 1


def pair_sum(layer, grads, landed, parts, core, *, name):
    n = len(grads)

    def body(c_ref, *refs):
        g_refs, l_refs, o_refs = refs[:n], refs[n:2 * n], refs[3 * n:]
        for w in range(n):
            o_refs[w][...] = (g_refs[w][...].astype(F32) + l_refs[w][...].astype(F32)).astype(BF16)

    def blk(g):
        return (None, g.shape[1] // PAIR_SUM_CHUNKS, g.shape[2])

    in_specs = [pl.BlockSpec(blk(g), lambda k, i, c_ref: (2 * k + c_ref[0], i, 0)) for g in grads]
    in_specs += [pl.BlockSpec(blk(g), lambda k, i, c_ref: (k, i, 0)) for g in grads]
    in_specs += [ANY] * n
    out_specs = tuple(pl.BlockSpec((None,) + blk(g), lambda k, i, c_ref: (layer, k, i, 0)) for g in grads)
    return list(pl.pallas_call(
        body, name=name, out_shape=tuple(jax.ShapeDtypeStruct(p.shape, p.dtype) for p in parts),
        grid_spec=pltpu.PrefetchScalarGridSpec(num_scalar_prefetch=1, grid=(N_CHIPS, PAIR_SUM_CHUNKS),
                                               in_specs=in_specs, out_specs=out_specs),
        input_output_aliases={1 + 2 * n + w: w for w in range(n)},
        compiler_params=_params("parallel", "parallel"),
    )(core, *grads, *landed, *parts))


def _adamw(w, g, m, v):
    m = ADAM_B1 * m + (1.0 - ADAM_B1) * g
    v = ADAM_B2 * v + (1.0 - ADAM_B2) * (g * g)
    m_hat = m / (1.0 - ADAM_B1 ** ADAM_STEP)
    v_hat = v / (1.0 - ADAM_B2 ** ADAM_STEP)
    delta = -ADAM_LR * (m_hat / (jnp.sqrt(v_hat) + ADAM_EPS) + ADAM_WD * w)
    return delta, m, v


def reduce_adamw(part, land, chip, w, m, v, *, name):
    _, r, c = w.shape
    tr = 256 if r % 256 == 0 else (r // 2 if r > 256 else r)

    def body(k_ref, own_ref, l0_ref, l1_ref, l2_ref, w_ref, m_ref, v_ref, g_out, d_out, m_out, v_out):
        g = own_ref[...].astype(F32) + l0_ref[...].astype(F32) + l1_ref[...].astype(F32) + l2_ref[...].astype(F32)
        delta, m_new, v_new = _adamw(w_ref[...], g, m_ref[...], v_ref[...])
        g_out[...] = g
        d_out[...] = delta
        m_out[...] = m_new
        v_out[...] = v_new

    row = pl.BlockSpec((None, tr, c), lambda l, i, k_ref: (l, i, 0))

    def slot(j):
        return pl.BlockSpec((None, None, tr, c), lambda l, i, k_ref: (l, j, i, 0))

    return pl.pallas_call(
        body, name=name, out_shape=(jax.ShapeDtypeStruct(w.shape, F32),) * 4,
        grid_spec=pltpu.PrefetchScalarGridSpec(
            num_scalar_prefetch=1, grid=(DEPTH, r // tr),
            in_specs=[pl.BlockSpec((None, None, tr, c), lambda l, i, k_ref: (l, k_ref[0], i, 0)), slot(0), slot(1),
                      slot(2), row, row, row],
            out_specs=(row, row, row, row)),
        compiler_params=_params("parallel", "parallel"),
    )(chip, part, land, land, land, w, m, v)


def gather_small(block, *, name):
    def body(x_ref, out_ref, send_sems, recv_sems, local_sem):
        x, y, c = _place()
        me = 4 * x + 2 * y + c
        mine = pltpu.make_async_copy(x_ref, out_ref.at[me], local_sem)
        mine.start()
        peers = [(x ^ (k >> 2), y ^ ((k >> 1) & 1), c ^ (k & 1)) for k in range(1, N_DEV)]
        copies = [pltpu.make_async_remote_copy(
            src_ref=x_ref, dst_ref=out_ref.at[me], send_sem=send_sems.at[k], recv_sem=recv_sems.at[k],
            device_id=peer, device_id_type=MESH) for k, peer in enumerate(peers)]
        for cp in copies:
            cp.start()
        for k, (px, py, pc) in enumerate(peers):
            pltpu.make_async_remote_copy(
                src_ref=x_ref, dst_ref=out_ref.at[4 * px + 2 * py + pc], send_sem=send_sems.at[k],
                recv_sem=recv_sems.at[k], device_id=(px, py, pc), device_id_type=MESH).wait_recv()
        for cp in copies:
            cp.wait_send()
        mine.wait()

    return pl.pallas_call(
        body, name=name, out_shape=jax.ShapeDtypeStruct((N_DEV,) + block.shape, block.dtype),
        in_specs=[ANY], out_specs=ANY,
        scratch_shapes=[pltpu.SemaphoreType.DMA((7,)), pltpu.SemaphoreType.DMA((7,)), pltpu.SemaphoreType.DMA],
        compiler_params=pltpu.CompilerParams(has_side_effects=True),
    )(block)


def small_adamw(gathered, w, m, v, *, name):
    def body(g_ref, w_ref, m_ref, v_ref, g_out, d_out, m_out, v_out):
        g = g_ref[0]
        for d in range(1, N_DEV):
            g = g + g_ref[d]
        delta, m_new, v_new = _adamw(w_ref[...], g, m_ref[...], v_ref[...])
        g_out[...] = g
        d_out[...] = delta
        m_out[...] = m_new
        v_out[...] = v_new

    return pl.pallas_call(
        body, name=name, out_shape=(jax.ShapeDtypeStruct(w.shape, F32),) * 4,
    )(gathered, w, m, v)


def norm_matmul(x, g, w, *, gate_split, name, ride=()):
    s, d = x.shape
    tm = min(ROW_TILE, s)
    blocked = w.ndim == 3
    n = w.shape[0] if not blocked else w.shape[0] * w.shape[2]

    def body(x_ref, g_ref, w_ref, h_ref, *outs):
        xv = x_ref[...]
        h = ((xv * _rsqrt_ms(xv)) * g_ref[...]).astype(BF16)
        h_ref[...] = h
        if blocked:
            nb = w_ref.shape[2]
            for j in range(w_ref.shape[0]):
                outs[0][:, j * nb:(j + 1) * nb] = _dot(h, w_ref[j]).astype(BF16)
        else:
            p = _dot_nt(h, w_ref[...])
            outs[0][...] = p[:, :gate_split].astype(BF16)
            outs[1][...] = (1.0 / (1.0 + jnp.exp(-p[:, gate_split:]))).astype(BF16)

    row = lambda i: (i, 0)
    fixed = lambda i: (0, 0)
    if blocked:
        out_shape = (jax.ShapeDtypeStruct((s, d), BF16), jax.ShapeDtypeStruct((s, n), BF16))
        out_specs = (pl.BlockSpec((tm, d), row), pl.BlockSpec((tm, n), row))
        w_spec = pl.BlockSpec(w.shape, lambda i: (0, 0, 0))
    else:
        out_shape = (jax.ShapeDtypeStruct((s, d), BF16), jax.ShapeDtypeStruct((s, gate_split), BF16),
                     jax.ShapeDtypeStruct((s, n - gate_split), BF16))
        out_specs = (pl.BlockSpec((tm, d), row), pl.BlockSpec((tm, gate_split), row),
                     pl.BlockSpec((tm, n - gate_split), row))
        w_spec = pl.BlockSpec((n, d), fixed)
    return _call(body, name=name, grid=(s // tm,), out_shape=out_shape, out_specs=out_specs,
                 in_specs=[pl.BlockSpec((tm, d), row), pl.BlockSpec((1, d), fixed), w_spec], args=(x, g, w), ride=ride)


def merge_out_fwd(x, o_sb, o_sw, gates, w_bsb, w_bsw, w_o, *, name, ride=()):
    s, d = x.shape
    tm = min(ROW_TILE, s)

    def body(x_ref, osb_ref, osw_ref, g_ref, wsb_ref, wsw_ref, wo_ref, x1_ref, ysb_ref, ysw_ref, mg_ref):
        y_sb = _dot(osb_ref[...].astype(BF16), wsb_ref[...])
        y_sw = _dot(osw_ref[...].astype(BF16), wsw_ref[...])
        g = g_ref[...].astype(F32)
        merged = (g[:, :d] * y_sb + g[:, d:] * y_sw).astype(BF16)
        ysb_ref[...] = y_sb.astype(BF16)
        ysw_ref[...] = y_sw.astype(BF16)
        mg_ref[...] = merged
        x1_ref[...] = x_ref[...] + _dot(merged, wo_ref[...])

    row = lambda i: (i, 0)
    fixed = lambda i: (0, 0)
    wd = o_sb.shape[1]
    return _call(
        body, name=name, grid=(s // tm,),
        out_shape=(jax.ShapeDtypeStruct((s, d), F32),) + (jax.ShapeDtypeStruct((s, d), BF16),) * 3,
        in_specs=[pl.BlockSpec((tm, d), row), pl.BlockSpec((tm, wd), row), pl.BlockSpec((tm, wd), row),
                  pl.BlockSpec((tm, 2 * d), row), pl.BlockSpec((wd, d), fixed), pl.BlockSpec((wd, d), fixed),
                  pl.BlockSpec((d, d), fixed)],
        out_specs=(pl.BlockSpec((tm, d), row),) * 4, args=(x, o_sb, o_sw, gates, w_bsb, w_bsw, w_o), ride=ride)


def mlp_down_fwd(x1, u, w_down, *, name, ride=()):
    s, d = x1.shape
    f = u.shape[1]
    tm = min(ROW_TILE, s)

    def body(x_ref, u_ref, w_ref, o_ref):
        a = jnp.maximum(u_ref[...].astype(F32), 0.0)
        o_ref[...] = x_ref[...] + _dot((a * a).astype(BF16), w_ref[...])

    row = lambda i: (i, 0)
    return _call(
        body, name=name, grid=(s // tm,), out_shape=(jax.ShapeDtypeStruct((s, d), F32),),
        in_specs=[pl.BlockSpec((tm, d), row), pl.BlockSpec((tm, f), row), pl.BlockSpec((f, d), lambda i: (0, 0))],
        out_specs=(pl.BlockSpec((tm, d), row),), args=(x1, u, w_down), ride=ride)


def loss_head(y, target, *, name):
    s, d = y.shape
    tm = min(ROW_TILE, s)

    def body(y_ref, t_ref, dy_ref, dyb_ref, loss_ref):
        @pl.when(pl.program_id(0) == 0)
        def _():
            loss_ref[...] = jnp.zeros_like(loss_ref)

        e = y_ref[...] - t_ref[...]
        dy = e * (1.0 / d)
        dy_ref[...] = dy
        dyb_ref[...] = dy.astype(BF16)
        per_row = jnp.sum(e * e, axis=1, keepdims=True) * (0.5 / d)
        loss_ref[...] += jnp.sum(per_row, axis=0, keepdims=True)

    row = lambda i: (i, 0)
    return pl.pallas_call(
        body, name=name, grid=(s // tm,),
        out_shape=(jax.ShapeDtypeStruct((s, d), F32), jax.ShapeDtypeStruct((s, d), BF16),
                   jax.ShapeDtypeStruct((1, 1), F32)),
        in_specs=[pl.BlockSpec((tm, d), row), pl.BlockSpec((tm, d), row)],
        out_specs=(pl.BlockSpec((tm, d), row), pl.BlockSpec((tm, d), row), pl.BlockSpec((1, 1), lambda i: (0, 0))),
        compiler_params=_params("arbitrary"),
    )(y, target)


def mlp_bwd_up(dxb, u, w_down, *, name, ride=()):
    s, d = dxb.shape
    f = u.shape[1]
    tm = min(ROW_TILE, s)

    def body(dx_ref, u_ref, w_ref, du_ref):
        da = _dot_nt(dx_ref[...], w_ref[...])
        du_ref[...] = (da * (2.0 * jnp.maximum(u_ref[...].astype(F32), 0.0))).astype(BF16)

    row = lambda i: (i, 0)
    return _call(body, name=name, grid=(s // tm,), out_shape=(jax.ShapeDtypeStruct((s, f), BF16),),
                 in_specs=[pl.BlockSpec((tm, d), row), pl.BlockSpec((tm, f), row),
                           pl.BlockSpec((f, d), lambda i: (0, 0))],
                 out_specs=(pl.BlockSpec((tm, f), row),), args=(dxb, u, w_down), ride=ride)


def matmul_nt_norm_bwd(pieces, w, x, g, dres, *, name, ride=()):
    s = x.shape[0]
    d = x.shape[1]
    tm = min(ROW_TILE, s)
    blocked = w.ndim == 3
    n_pieces = len(pieces)
    widths = [p.shape[1] for p in pieces]

    def body(*refs):
        p_refs = refs[:n_pieces]
        w_ref, x_ref, g_ref, dres_ref, dx_ref, dxb_ref, dg_ref = refs[n_pieces:]

        @pl.when(pl.program_id(0) == 0)
        def _():
            dg_ref[...] = jnp.zeros_like(dg_ref)

        if blocked:
            nb = w_ref.shape[2]
            dh = _dot_nt(p_refs[0][:, :nb], w_ref[0])
            for j in range(1, w_ref.shape[0]):
                dh = dh + _dot_nt(p_refs[0][:, j * nb:(j + 1) * nb], w_ref[j])
        else:
            dh, off = None, 0
            for p_ref, width in zip(p_refs, widths):
                part = _dot(p_ref[...], w_ref[off:off + width, :])
                dh = part if dh is None else dh + part
                off += width
        xv = x_ref[...]
        r = _rsqrt_ms(xv)
        dyg = dh * g_ref[...]
        dx = dres_ref[...] + r * dyg - xv * ((r * r * r) * jnp.mean(dyg * xv, axis=-1, keepdims=True))
        dx_ref[...] = dx
        dxb_ref[...] = dx.astype(BF16)
        dg_ref[...] += jnp.sum(dh * (xv * r), axis=0, keepdims=True)

    row = lambda i: (i, 0)
    fixed = lambda i: (0, 0)
    w_spec = pl.BlockSpec(w.shape, (lambda i: (0, 0, 0)) if blocked else fixed)
    return _call(
        body, name=name, grid=(s // tm,),
        out_shape=(jax.ShapeDtypeStruct((s, d), F32), jax.ShapeDtypeStruct((s, d), BF16),
                   jax.ShapeDtypeStruct((1, d), F32)),
        in_specs=[pl.BlockSpec((tm, width), row) for width in widths] + [
            w_spec, pl.BlockSpec((tm, d), row), pl.BlockSpec((1, d), fixed), pl.BlockSpec((tm, d), row)],
        out_specs=(pl.BlockSpec((tm, d), row), pl.BlockSpec((tm, d), row), pl.BlockSpec((1, d), fixed)),
        args=(*pieces, w, x, g, dres), ride=ride)


def out_bwd(dx1b, w_o, gates, y_sb, y_sw, w_bsb, w_bsw, *, name):
    s, d = dx1b.shape
    wd = w_bsb.shape[0]
    tm = min(ROW_TILE, s)

    def body(dx_ref, wo_ref, g_ref, ysb_ref, ysw_ref, wsb_ref, wsw_ref, dysb_ref, dysw_ref, dosb_ref, dosw_ref, dgl_ref):
        dm = _dot_nt(dx_ref[...], wo_ref[...])
        g = g_ref[...].astype(F32)
        g0, g1 = g[:, :d], g[:, d:]
        dy_sb = (dm * g0).astype(BF16)
        dy_sw = (dm * g1).astype(BF16)
        dysb_ref[...] = dy_sb
        dysw_ref[...] = dy_sw
        dosb_ref[...] = _dot_nt(dy_sb, wsb_ref[...]).astype(BF16)
        dosw_ref[...] = _dot_nt(dy_sw, wsw_ref[...]).astype(BF16)
        dgl_ref[:, :d] = (dm * ysb_ref[...].astype(F32) * (g0 * (1.0 - g0))).astype(BF16)
        dgl_ref[:, d:] = (dm * ysw_ref[...].astype(F32) * (g1 * (1.0 - g1))).astype(BF16)

    row = lambda i: (i, 0)
    fixed = lambda i: (0, 0)
    return pl.pallas_call(
        body, name=name, grid=(s // tm,),
        out_shape=(jax.ShapeDtypeStruct((s, d), BF16), jax.ShapeDtypeStruct((s, d), BF16),
                   jax.ShapeDtypeStruct((s, wd), BF16), jax.ShapeDtypeStruct((s, wd), BF16),
                   jax.ShapeDtypeStruct((s, 2 * d), BF16)),
        in_specs=[pl.BlockSpec((tm, d), row), pl.BlockSpec((d, d), fixed), pl.BlockSpec((tm, 2 * d), row),
                  pl.BlockSpec((tm, d), row), pl.BlockSpec((tm, d), row), pl.BlockSpec((wd, d), fixed),
                  pl.BlockSpec((wd, d), fixed)],
        out_specs=(pl.BlockSpec((tm, d), row), pl.BlockSpec((tm, d), row), pl.BlockSpec((tm, wd), row),
                   pl.BlockSpec((tm, wd), row), pl.BlockSpec((tm, 2 * d), row)),
        compiler_params=_params("parallel"),
    )(dx1b, w_o, gates, y_sb, y_sw, w_bsb, w_bsw)


def matmul_tn(a, pieces, *, a_block, out_cols, relu2, name):
    s, m = a.shape
    widths = [p.shape[1] for p in pieces]
    n = sum(widths)
    n_pieces = len(pieces)
    ts = min(512 if n >= 4096 else 2048, s)
    n_steps = s // ts
    if out_cols is None:
        out_shape = jax.ShapeDtypeStruct((m // a_block, a_block, n), BF16)
        out_spec = pl.BlockSpec((None, a_block, n), lambda i, k: (i, 0, 0))
    else:
        out_shape = jax.ShapeDtypeStruct((n // out_cols, m, out_cols), BF16)
        out_spec = pl.BlockSpec((n // out_cols, a_block, out_cols), lambda i, k: (0, i, 0))

    def body(a_ref, *refs):
        b_refs, o_ref, acc = refs[:n_pieces], refs[n_pieces], refs[n_pieces + 1]
        k = pl.program_id(1)

        @pl.when(k == 0)
        def _():
            acc[...] = jnp.zeros_like(acc)

        av = a_ref[...]
        if relu2:
            af = jnp.maximum(av.astype(F32), 0.0)
            av = af * af
        av = av.astype(BF16)
        off = 0
        for b_ref in b_refs:
            width = b_ref.shape[1]
            acc[:, off:off + width] += _dot_tn(av, b_ref[...].astype(BF16))
            off += width

        @pl.when(k == n_steps - 1)
        def _():
            if out_cols is None:
                o_ref[...] = acc[...].astype(BF16)
            else:
                for j in range(n // out_cols):
                    o_ref[j] = acc[:, j * out_cols:(j + 1) * out_cols].astype(BF16)

    return pl.pallas_call(
        body, name=name, grid=(m // a_block, n_steps), out_shape=out_shape,
        in_specs=[pl.BlockSpec((ts, a_block), lambda i, k: (k, i))] + [
            pl.BlockSpec((ts, width), lambda i, k: (k, 0)) for width in widths],
        out_specs=out_spec, scratch_shapes=[pltpu.VMEM((a_block, n), F32)],
        compiler_params=_params("parallel", "arbitrary"),
    )(a, *pieces)


def matmul_tn_row_blocks(pieces, b, *, n_blocks, name):
    s, n = b.shape
    widths = [p.shape[1] for p in pieces]
    m = sum(widths)
    rows = m // n_blocks
    n_pieces = len(pieces)
    ts = min(512, s)
    n_steps = s // ts
    half = n_blocks // 2

    def body(*refs):
        p_refs, b_ref, o_ref, a_tile, acc = refs[:n_pieces], refs[n_pieces], refs[n_pieces + 1], refs[-2], refs[-1]
        i, k = pl.program_id(0), pl.program_id(1)

        @pl.when(k == 0)
        def _():
            acc[...] = jnp.zeros_like(acc)

        off = 0
        for p_ref, width in zip(p_refs, widths):
            a_tile[:, off:off + width] = p_ref[...]
            off += width
        bv = b_ref[...]
        for side in range(2):
            @pl.when(i == side)
            def _():
                for j in range(half):
                    col = (side * half + j) * rows
                    acc[j] += _dot_tn(a_tile[:, col:col + rows], bv)

        @pl.when(k == n_steps - 1)
        def _():
            o_ref[...] = acc[...].astype(BF16)

    return pl.pallas_call(
        body, name=name, grid=(2, n_steps), out_shape=jax.ShapeDtypeStruct((n_blocks, rows, n), BF16),
        in_specs=[pl.BlockSpec((ts, width), lambda i, k: (k, 0)) for width in widths] + [
            pl.BlockSpec((ts, n), lambda i, k: (k, 0))],
        out_specs=pl.BlockSpec((half, rows, n), lambda i, k: (i, 0, 0)),
        scratch_shapes=[pltpu.VMEM((ts, m), BF16), pltpu.VMEM((half, rows, n), F32)],
        compiler_params=_params("parallel", "arbitrary"),
    )(*pieces, b)


def _softplus(z):
    return jnp.maximum(z, 0.0) + jnp.log(1.0 + jnp.exp(-jnp.abs(z)))


def _suffix_sums(x, tri2):
    groups = x.shape[1] // LANES
    outs, run = [None] * groups, None
    for g in reversed(range(groups)):
        xg = x[:, g * LANES:(g + 1) * LANES]
        hi, lo = _split_bf16(xg)
        inner = _dot(jnp.concatenate([hi, lo], axis=1), tri2)
        outs[g] = inner if run is None else inner + run
        total = jnp.sum(xg, axis=1, keepdims=True)
        run = total if run is None else run + total
    return jnp.concatenate(outs, axis=1), run


def _head_mask(h):
    return (lax.broadcasted_iota(jnp.int32, (1, LANES), 1) // HEAD_DIM) == h


def _stack_heads(x):
    zero = jnp.zeros_like(x)
    return jnp.concatenate([jnp.where(_head_mask(0), x, zero), jnp.where(_head_mask(1), x, zero)], axis=0)


def _unstack_heads(r, t):
    return jnp.where(_head_mask(0), r[:t], r[t:])


def _sb_positions(q0, tk):
    row = lax.broadcasted_iota(jnp.int32, (2 * SB_TQ, tk), 0)
    col = lax.broadcasted_iota(jnp.int32, (2 * SB_TQ, tk), 1)
    return q0 + jnp.where(row >= SB_TQ, row - SB_TQ, row), col


def _sb_first_key(q0):
    return pl.multiple_of(jnp.maximum(q0 + SB_TQ - SB_TK1, 0), SB_TQ)


def _sb_next_key(k_prev):
    return pl.multiple_of(jnp.maximum(k_prev - SB_TK, 0), SB_TQ)


def _sb_rows(q0):
    return pl.ds(pl.multiple_of(2 * q0, 2 * SB_TQ), 2 * SB_TQ)


def _sb_keep(live):
    return lambda x: jnp.where(live, x, 0.0)


def _sb_keep_first(q0, k0, interior):
    if not interior:
        tpos, col = _sb_positions(q0, SB_TK1)
        return _sb_keep(k0 + col < tpos)
    row = lax.broadcasted_iota(jnp.int32, (2 * SB_TQ, SB_TQ), 0)
    own = lax.broadcasted_iota(jnp.int32, (2 * SB_TQ, SB_TQ), 1) < jnp.where(row >= SB_TQ, row - SB_TQ, row)
    past = SB_TK1 - SB_TQ
    return lambda x: jnp.concatenate([x[:, :past], jnp.where(own, x[:, past:], 0.0)], axis=1)


SB_EDGE_TILES = 8
SB_SCAN_GROUP = 4


def _sb_first_pass(nq, first):
    edge = min(nq, SB_EDGE_TILES)
    lax.fori_loop(0, edge, first(False), 0, unroll=4)
    lax.fori_loop(edge, nq, first(True), 0, unroll=8)


def _sb_scan_tiles(c_all, nq, more):
    rows = SB_SCAN_GROUP * 2 * SB_TQ

    def group(g, carry):
        @pl.when(jnp.max(c_all[pl.ds(pl.multiple_of(g * rows, rows), rows), :]) > SB_CUTOFF)
        def _():
            lax.fori_loop(g * SB_SCAN_GROUP, (g + 1) * SB_SCAN_GROUP, more, 0)

        return carry

    lax.fori_loop(0, nq // SB_SCAN_GROUP, group, 0)


def sb_attn_fwd(proj, tri2, *, name, ride=()):
    s = proj.shape[0]
    nq = s // SB_TQ
    n_pairs = SB_WIDTH // LANES

    def body(q_ref, k_ref, v_ref, tri_ref, o_ref, c_all):
        def block(qh, k0, tk, keep, c):
            z = _dot_nt(qh, k_ref[pl.ds(k0, tk), :])
            sp = _softplus(z)
            tail, total = _suffix_sums(keep(-sp), tri_ref[...])
            w = keep(jnp.exp(z - sp + tail + c))
            return _dot(w.astype(BF16), v_ref[pl.ds(k0, tk), :]), c + total

        def load_q(q0):
            return _stack_heads(q_ref[pl.ds(q0, SB_TQ), :]) * SCALE

        def first(interior):
            def run(qb, carry):
                q0 = pl.multiple_of(qb * SB_TQ, SB_TQ)
                k0 = _sb_first_key(q0)
                acc, c = block(load_q(q0), k0, SB_TK1, _sb_keep_first(q0, k0, interior), jnp.zeros((2 * SB_TQ, 1), F32))
                o_ref[pl.ds(q0, SB_TQ), :] = _unstack_heads(acc, SB_TQ)
                c_all[_sb_rows(q0), :] = jnp.broadcast_to(jnp.where(k0 > 0, c, NEG), (2 * SB_TQ, LANES))
                return carry

            return run

        _sb_first_pass(nq, first)

        @pl.when(jnp.max(c_all[...]) > SB_CUTOFF)
        def _():
            def more(qb, carry):
                q0 = pl.multiple_of(qb * SB_TQ, SB_TQ)
                c0 = c_all[_sb_rows(q0), 0:1]

                @pl.when(jnp.max(c0) > SB_CUTOFF)
                def _():
                    qh = load_q(q0)
                    _, col = _sb_positions(q0, SB_TK)

                    def cond(st):
                        return jnp.logical_and(st[0] > 0, st[3] > SB_CUTOFF)

                    def step(st):
                        k_prev, c, acc, _ = st
                        k0 = _sb_next_key(k_prev)
                        part, c = block(qh, k0, SB_TK, _sb_keep(k0 + col < k_prev), c)
                        return k0, c, acc + part, jnp.max(c)

                    st = lax.while_loop(cond, step, (_sb_first_key(q0), c0, jnp.zeros((2 * SB_TQ, LANES), F32),
                                                     jnp.max(c0)))
                    o_ref[pl.ds(q0, SB_TQ), :] += _unstack_heads(st[2], SB_TQ)

                return carry

            _sb_scan_tiles(c_all, nq, more)

    def col_spec(j):
        return pl.BlockSpec((s, LANES), lambda p: (0, j * n_pairs + p))

    (o,), rides = _call(
        body, name=name, grid=(n_pairs,), out_shape=(jax.ShapeDtypeStruct((s, SB_WIDTH), F32),),
        in_specs=[col_spec(0), col_spec(1), col_spec(2), pl.BlockSpec((2 * LANES, LANES), lambda p: (0, 0))],
        out_specs=(pl.BlockSpec((s, LANES), lambda p: (0, p)),), scratch_shapes=[pltpu.VMEM((2 * s, LANES), F32)],
        args=(proj, proj, proj, tri2), ride=ride)
    return o, rides


def sb_attn_bwd(proj, tri2, o, do, *, name, ride=()):
    s = proj.shape[0]
    nq = s // SB_TQ
    n_pairs = SB_WIDTH // LANES

    def body(q_ref, k_ref, v_ref, tri_ref, o_ref, do_ref, dq_ref, dk_ref, dv_ref, dq_acc, dk_acc, dv_acc, c_all, e_all):
        dk_acc[...] = jnp.zeros_like(dk_acc)
        dv_acc[...] = jnp.zeros_like(dv_acc)

        def load(q0):
            qh = _stack_heads(q_ref[pl.ds(q0, SB_TQ), :]) * SCALE
            doh_b = _stack_heads(do_ref[pl.ds(q0, SB_TQ), :])
            ov = o_ref[pl.ds(q0, SB_TQ), :]
            dd = jnp.sum(doh_b.astype(F32) * jnp.concatenate([ov, ov], axis=0), axis=1, keepdims=True)
            return qh, doh_b, dd

        def block(qh, doh_b, dd, k0, tk, keep, c, ce):
            kt = k_ref[pl.ds(k0, tk), :]
            z = _dot_nt(qh, kt)
            sp = _softplus(z)
            lb = z - sp
            tail, total = _suffix_sums(keep(-sp), tri_ref[...])
            wb = keep(jnp.exp(lb + tail + c)).astype(BF16)
            e = wb.astype(F32) * _dot_nt(doh_b, v_ref[pl.ds(k0, tk), :])
            e_tail, e_total = _suffix_sums(e, tri_ref[...])
            dzb = keep(e - jnp.exp(lb) * (dd - ce - e_tail)).astype(BF16)
            dk_acc[pl.ds(k0, tk), :] += _dot_tn(dzb, qh)
            dv_acc[pl.ds(k0, tk), :] += _dot_tn(wb, doh_b)
            return _dot(dzb, kt), c + total, ce + e_total

        def first(interior):
            def run(qb, carry):
                q0 = pl.multiple_of(qb * SB_TQ, SB_TQ)
                qh, doh_b, dd = load(q0)
                k0 = _sb_first_key(q0)
                zero = jnp.zeros((2 * SB_TQ, 1), F32)
                dq, c, ce = block(qh, doh_b, dd, k0, SB_TK1, _sb_keep_first(q0, k0, interior), zero, zero)
                dq_acc[pl.ds(q0, SB_TQ), :] = _unstack_heads(dq, SB_TQ)
                c_all[_sb_rows(q0), :] = jnp.broadcast_to(jnp.where(k0 > 0, c, NEG), (2 * SB_TQ, LANES))
                e_all[_sb_rows(q0), :] = jnp.broadcast_to(ce, (2 * SB_TQ, LANES))
                return carry

            return run

        _sb_first_pass(nq, first)

        @pl.when(jnp.max(c_all[...]) > SB_CUTOFF)
        def _():
            def more(qb, carry):
                q0 = pl.multiple_of(qb * SB_TQ, SB_TQ)
                c0 = c_all[_sb_rows(q0), 0:1]

                @pl.when(jnp.max(c0) > SB_CUTOFF)
                def _():
                    qh, doh_b, dd = load(q0)
                    _, col = _sb_positions(q0, SB_TK)

                    def cond(st):
                        return jnp.logical_and(st[0] > 0, st[4] > SB_CUTOFF)

                    def step(st):
                        k_prev, c, ce, dq, _ = st
                        k0 = _sb_next_key(k_prev)
                        part, c, ce = block(qh, doh_b, dd, k0, SB_TK, _sb_keep(k0 + col < k_prev), c, ce)
                        return k0, c, ce, dq + part, jnp.max(c)

                    st = lax.while_loop(cond, step, (_sb_first_key(q0), c0, e_all[_sb_rows(q0), 0:1],
                                                     jnp.zeros((2 * SB_TQ, LANES), F32), jnp.max(c0)))
                    dq_acc[pl.ds(q0, SB_TQ), :] += _unstack_heads(st[3], SB_TQ)

                return carry

            _sb_scan_tiles(c_all, nq, more)

        dq_ref[...] = (dq_acc[...] * SCALE).astype(BF16)
        dk_ref[...] = dk_acc[...].astype(BF16)
        dv_ref[...] = dv_acc[...].astype(BF16)

    def col_spec(j):
        return pl.BlockSpec((s, LANES), lambda p: (0, j * n_pairs + p))

    pair = pl.BlockSpec((s, LANES), lambda p: (0, p))
    (dq, dk, dv), rides = _call(
        body, name=name, grid=(n_pairs,), out_shape=(jax.ShapeDtypeStruct((s, SB_WIDTH), BF16),) * 3,
        in_specs=[col_spec(0), col_spec(1), col_spec(2), pl.BlockSpec((2 * LANES, LANES), lambda p: (0, 0)), pair, pair],
        out_specs=(pair, pair, pair),
        scratch_shapes=[pltpu.VMEM((s, LANES), F32)] * 3 + [pltpu.VMEM((2 * s, LANES), F32)] * 2,
        args=(proj, proj, proj, tri2, o, do), ride=ride)
    return dq, dk, dv, rides


def _lane_lo():
    return lax.broadcasted_iota(jnp.int32, (1, LANES), 1) < HEAD_DIM


def _swap_halves(x):
    return pltpu.roll(x, HEAD_DIM, 1)


def _rot_half(y):
    first = (lax.broadcasted_iota(jnp.int32, (1, LANES), 1) % HEAD_DIM) < (HEAD_DIM // 2)
    return jnp.where(first, pltpu.roll(y, LANES - HEAD_DIM // 2, 1), pltpu.roll(y, HEAD_DIM // 2, 1))


def _head_mean(v, avg):
    hi, lo = _split_bf16(v)
    return _dot(hi, avg) + _dot(lo, avg)


def _head_avg_matrix():
    lane = jnp.arange(LANES) // HEAD_DIM
    return ((lane[:, None] == lane[None, :]).astype(F32) * (1.0 / HEAD_DIM)).astype(BF16)


def swa_prep_fwd(proj, cos_p, sin_p, gq, gk, *, name):
    s = proj.shape[0]
    tm = min(512, s)
    q_blk = (3 * SB_WIDTH) // SWA_Q_WIDTH
    k_blk = (3 * SB_WIDTH + SWA_Q_WIDTH) // LANES

    def body(q_ref, k_ref, cos_ref, sin_ref, gq_ref, gk_ref, avg_ref, qn_ref, kn_ref):
        cosv, sinv, avg = cos_ref[...], sin_ref[...], avg_ref[...]

        def norm_rope(xv, g):
            y = (xv * lax.rsqrt(_head_mean(xv * xv, avg) + NORM_EPS)) * g
            return y * cosv + _rot_half(y) * sinv

        for j in range(SWA_Q_WIDTH // LANES):
            sl = slice(j * LANES, (j + 1) * LANES)
            qn_ref[:, sl] = norm_rope(q_ref[:, sl].astype(F32), gq_ref[...]).astype(BF16)
        kn_ref[...] = norm_rope(k_ref[...].astype(F32), gk_ref[...]).astype(BF16)

    row = lambda i: (i, 0)
    fixed = lambda i: (0, 0)
    return pl.pallas_call(
        body, name=name, grid=(s // tm,),
        out_shape=(jax.ShapeDtypeStruct((s, SWA_Q_WIDTH), BF16), jax.ShapeDtypeStruct((s, LANES), BF16)),
        in_specs=[pl.BlockSpec((tm, SWA_Q_WIDTH), lambda i: (i, q_blk)), pl.BlockSpec((tm, LANES), lambda i: (i, k_blk)),
                  pl.BlockSpec((tm, LANES), row), pl.BlockSpec((tm, LANES), row),
                  pl.BlockSpec((1, LANES), fixed), pl.BlockSpec((1, LANES), fixed), pl.BlockSpec((LANES, LANES), fixed)],
        out_specs=(pl.BlockSpec((tm, SWA_Q_WIDTH), row), pl.BlockSpec((tm, LANES), row)),
        compiler_params=_params("parallel"),
    )(proj, proj, cos_p, sin_p, gq, gk, _head_avg_matrix())


def swa_prep_bwd(proj, cos_p, sin_p, gq, gk, dqn, dkn, dv, *, name):
    s = proj.shape[0]
    tm = min(512, s)
    q_blk = (3 * SB_WIDTH) // SWA_Q_WIDTH
    k_blk = (3 * SB_WIDTH + SWA_Q_WIDTH) // LANES

    def body(q_ref, k_ref, cos_ref, sin_ref, gq_ref, gk_ref, avg_ref, dqn_ref, dkn_ref, dv_ref, dq_ref, dk_ref, dvb_ref,
             dgq_ref, dgk_ref):
        @pl.when(pl.program_id(0) == 0)
        def _():
            dgq_ref[...] = jnp.zeros_like(dgq_ref)
            dgk_ref[...] = jnp.zeros_like(dgk_ref)

        cosv, sinv, avg = cos_ref[...], sin_ref[...], avg_ref[...]

        def bwd(xv, g, dout):
            dy = dout * cosv + _rot_half(dout * sinv)
            r = lax.rsqrt(_head_mean(xv * xv, avg) + NORM_EPS)
            dyg = dy * g
            dx = r * dyg - xv * ((r * r * r) * _head_mean(dyg * xv, avg))
            return dx, jnp.sum(dy * (xv * r), axis=0, keepdims=True)

        for j in range(SWA_Q_WIDTH // LANES):
            sl = slice(j * LANES, (j + 1) * LANES)
            dx, dg = bwd(q_ref[:, sl].astype(F32), gq_ref[...], dqn_ref[:, sl])
            dq_ref[:, sl] = dx.astype(BF16)
            dgq_ref[:, sl] += dg
        dx, dg = bwd(k_ref[...].astype(F32), gk_ref[...], dkn_ref[...])
        dk_ref[...] = dx.astype(BF16)
        dgk_ref[...] += dg
        dvb_ref[...] = dv_ref[...].astype(BF16)

    row = lambda i: (i, 0)
    fixed = lambda i: (0, 0)
    lane_row = pl.BlockSpec((tm, LANES), row)
    return pl.pallas_call(
        body, name=name, grid=(s // tm,),
        out_shape=(jax.ShapeDtypeStruct((s, SWA_Q_WIDTH), BF16), jax.ShapeDtypeStruct((s, LANES), BF16),
                   jax.ShapeDtypeStruct((s, LANES), BF16),
                   jax.ShapeDtypeStruct((1, SWA_Q_WIDTH), F32), jax.ShapeDtypeStruct((1, LANES), F32)),
        in_specs=[pl.BlockSpec((tm, SWA_Q_WIDTH), lambda i: (i, q_blk)), pl.BlockSpec((tm, LANES), lambda i: (i, k_blk)),
                  lane_row, lane_row, pl.BlockSpec((1, LANES), fixed), pl.BlockSpec((1, LANES), fixed),
                  pl.BlockSpec((LANES, LANES), fixed), pl.BlockSpec((tm, SWA_Q_WIDTH), row), lane_row, lane_row],
        out_specs=(pl.BlockSpec((tm, SWA_Q_WIDTH), row), lane_row, lane_row,
                   pl.BlockSpec((1, SWA_Q_WIDTH), fixed), pl.BlockSpec((1, LANES), fixed)),
        compiler_params=_params("arbitrary"),
    )(proj, proj, cos_p, sin_p, gq, gk, _head_avg_matrix(), dqn, dkn, dv)


def _swa_kv_copies(k_ref, v_ref, kg_ref, vg_ref, second_kv):
    s = k_ref.shape[0]
    rows = min(512, s)
    keep = jnp.logical_xor(_lane_lo(), second_kv)

    def chunk(r, carry):
        sl = pl.ds(pl.multiple_of(r * rows, rows), rows)
        for src, dst in ((k_ref, kg_ref), (v_ref, vg_ref)):
            f = src[sl, :].astype(F32)
            dst[sl, :] = jnp.where(keep, f, _swap_halves(f)).astype(BF16)
        return carry

    lax.fori_loop(0, s // rows, chunk, 0)


def _swa_tile(i, kg_ref, vg_ref):
    q0 = pl.multiple_of(i * SWA_TQ, SWA_TQ)
    k0 = pl.multiple_of(jnp.maximum(i - 1, 0) * SWA_TQ, SWA_TQ)
    kg = kg_ref[pl.ds(k0, SWA_TK), :]
    vg = vg_ref[pl.ds(k0, SWA_TK), :]
    row = lax.broadcasted_iota(jnp.int32, (2 * SWA_TQ, SWA_TK), 0)
    tpos = q0 + jnp.where(row >= SWA_TQ, row - SWA_TQ, row)
    spos = k0 + lax.broadcasted_iota(jnp.int32, (2 * SWA_TQ, SWA_TK), 1)
    valid = jnp.logical_and(spos <= tpos, spos > tpos - WINDOW)
    return q0, k0, kg, vg, valid


def _swa_probs(qh, kg, valid, sink):
    z = jnp.where(valid, _dot_nt(qh, kg) * SCALE, NEG)
    m = jnp.maximum(jnp.max(z, axis=1, keepdims=True), sink)
    pexp = jnp.exp(z - m)
    psink = jnp.exp(sink - m)
    inv = 1.0 / (jnp.sum(pexp, axis=1, keepdims=True) + psink)
    return pexp * inv, psink * inv


def _stacked_sink(sink_row):
    s0 = jnp.sum(jnp.where(_head_mask(0), sink_row, 0.0), axis=1, keepdims=True) * (1.0 / HEAD_DIM)
    s1 = jnp.sum(jnp.where(_head_mask(1), sink_row, 0.0), axis=1, keepdims=True) * (1.0 / HEAD_DIM)
    top = lax.broadcasted_iota(jnp.int32, (2 * SWA_TQ, 1), 0) < SWA_TQ
    return jnp.where(top, s0, s1)


def swa_attn_fwd(qn, kn, proj, sink_p, *, name, ride=()):
    s = qn.shape[0]
    nq = s // SWA_TQ
    n_pairs = SWA_Q_WIDTH // LANES
    v_blk = (3 * SB_WIDTH + SWA_Q_WIDTH + SWA_KV_WIDTH) // LANES

    def body(q_ref, k_ref, v_ref, s_ref, o_ref, kg_ref, vg_ref):
        _swa_kv_copies(k_ref, v_ref, kg_ref, vg_ref, (pl.program_id(0) // 2) == 1)
        sink = _stacked_sink(s_ref[...])

        def tile(i, carry):
            q0, _, kg, vg, valid = _swa_tile(i, kg_ref, vg_ref)
            probs, _ = _swa_probs(_stack_heads(q_ref[pl.ds(q0, SWA_TQ), :]), kg, valid, sink)
            o_ref[pl.ds(q0, SWA_TQ), :] = _unstack_heads(_dot(probs.astype(BF16), vg), SWA_TQ)
            return carry

        lax.fori_loop(0, nq, tile, 0, unroll=8)

    pair = pl.BlockSpec((s, LANES), lambda p: (0, p))
    whole = pl.BlockSpec((s, LANES), lambda p: (0, 0))
    (o,), rides = _call(
        body, name=name, grid=(n_pairs,), out_shape=(jax.ShapeDtypeStruct((s, SWA_Q_WIDTH), F32),),
        in_specs=[pair, whole, pl.BlockSpec((s, LANES), lambda p: (0, v_blk)),
                  pl.BlockSpec((None, 1, LANES), lambda p: (p, 0, 0))],
        out_specs=(pair,), scratch_shapes=[pltpu.VMEM((s, LANES), BF16)] * 2, args=(qn, kn, proj, sink_p), ride=ride)
    return o, rides


def swa_attn_bwd(qn, kn, proj, sink_p, o, do, *, name, ride=()):
    s = qn.shape[0]
    nq = s // SWA_TQ
    n_pairs = SWA_Q_WIDTH // LANES
    v_blk = (3 * SB_WIDTH + SWA_Q_WIDTH + SWA_KV_WIDTH) // LANES
    fold_rows = min(512, s)

    def body(q_ref, k_ref, v_ref, s_ref, o_ref, do_ref, dq_ref, dk_ref, dv_ref, ds_ref, acc_k, acc_v, kg_ref, vg_ref):
        p = pl.program_id(0)
        _swa_kv_copies(k_ref, v_ref, kg_ref, vg_ref, (p // 2) == 1)
        sink = _stacked_sink(s_ref[...])

        @pl.when(p % 2 == 0)
        def _():
            acc_k[...] = jnp.zeros_like(acc_k)
            acc_v[...] = jnp.zeros_like(acc_v)

        ds_ref[...] = jnp.zeros_like(ds_ref)

        def tile(i, carry):
            q0, k0, kg, vg, valid = _swa_tile(i, kg_ref, vg_ref)
            qh = _stack_heads(q_ref[pl.ds(q0, SWA_TQ), :])
            doh_b = _stack_heads(do_ref[pl.ds(q0, SWA_TQ), :])
            ov = o_ref[pl.ds(q0, SWA_TQ), :]
            delta = jnp.sum(doh_b.astype(F32) * jnp.concatenate([ov, ov], axis=0), axis=1, keepdims=True)
            probs, psink = _swa_probs(qh, kg, valid, sink)
            dz = probs * (_dot_nt(doh_b, vg) - delta)
            dzb = (dz * SCALE).astype(BF16)
            dq_ref[pl.ds(q0, SWA_TQ), :] = _unstack_heads(_dot(dzb, kg), SWA_TQ)
            acc_k[pl.ds(k0, SWA_TK), :] += _dot_tn(dzb, qh)
            acc_v[pl.ds(k0, SWA_TK), :] += _dot_tn(probs.astype(BF16), doh_b)
            pd = psink * delta
            ds_ref[...] -= jnp.where(_head_mask(0), jnp.sum(pd[:SWA_TQ], axis=0, keepdims=True),
                                     jnp.sum(pd[SWA_TQ:], axis=0, keepdims=True))
            return carry

        lax.fori_loop(0, nq, tile, 0, unroll=8)

        def fold_into(first_head):
            def fold(r, carry):
                rows = pl.ds(pl.multiple_of(r * fold_rows, fold_rows), fold_rows)
                for acc, out in ((acc_k, dk_ref), (acc_v, dv_ref)):
                    a = acc[rows, :]
                    both = a + _swap_halves(a)
                    if first_head:
                        out[rows, :] = jnp.where(_lane_lo(), both, 0.0)
                    else:
                        out[rows, :] = jnp.where(_lane_lo(), out[rows, :], both)
                return carry

            lax.fori_loop(0, s // fold_rows, fold, 0)

        @pl.when(p == 1)
        def _():
            fold_into(True)

        @pl.when(p == 3)
        def _():
            fold_into(False)

    pair = pl.BlockSpec((s, LANES), lambda p: (0, p))
    whole = pl.BlockSpec((s, LANES), lambda p: (0, 0))
    sink_spec = pl.BlockSpec((None, 1, LANES), lambda p: (p, 0, 0))
    (dq, dk, dv, dsink), rides = _call(
        body, name=name, grid=(n_pairs,),
        out_shape=(jax.ShapeDtypeStruct((s, SWA_Q_WIDTH), F32), jax.ShapeDtypeStruct((s, LANES), F32),
                   jax.ShapeDtypeStruct((s, LANES), F32), jax.ShapeDtypeStruct((n_pairs, 1, LANES), F32)),
        in_specs=[pair, whole, pl.BlockSpec((s, LANES), lambda p: (0, v_blk)), sink_spec, pair, pair],
        out_specs=(pair, whole, whole, sink_spec),
        scratch_shapes=[pltpu.VMEM((s, LANES), F32)] * 2 + [pltpu.VMEM((s, LANES), BF16)] * 2,
        args=(qn, kn, proj, sink_p, o, do), ride=ride)
    return dq, dk, dv, dsink, rides


def _rope_tables(s):
    inv_freq = 1.0 / (ROPE_THETA ** (jnp.arange(0, HEAD_DIM, 2, dtype=F32) / HEAD_DIM))
    ang = jnp.arange(s, dtype=F32)[:, None] * inv_freq[None, :]
    cos, sin = jnp.cos(ang), jnp.sin(ang)
    cos_p = jnp.tile(jnp.concatenate([cos, cos], axis=1), (1, LANES // HEAD_DIM))
    sin_p = jnp.tile(jnp.concatenate([-sin, sin], axis=1), (1, LANES // HEAD_DIM))
    return cos_p, sin_p


def _lane_tile(v, reps):
    return jnp.tile(v.reshape(1, -1), (1, reps))


def _natural(stack, w):
    n, r, c = stack.shape
    if MATRIX_NAMES[w] in ROW_SHARDED or w == W_IN:
        return stack.reshape(n * r, c)
    if w == W_UP:
        return stack
    return jnp.transpose(stack, (1, 0, 2)).reshape(r, n * c)


def _pack_small(tree):
    flat = jnp.concatenate([tree[n].reshape(-1) for n in SMALL_NAMES])
    rows = -(-flat.shape[0] // (8 * LANES)) * 8
    return jnp.pad(flat, (0, rows * LANES - flat.shape[0])).reshape(rows, LANES)


def _unpack_small(packed, shapes):
    flat, out, off = packed.reshape(-1), {}, 0
    for n in SMALL_NAMES:
        size = shapes[n][0] * shapes[n][1]
        out[n] = flat[off:off + size].reshape(shapes[n])
        off += size
    return out


def train_step(x, target, weights, mom_m, mom_v):
    s = x.shape[0]
    cos_p, sin_p = _rope_tables(s)
    tri = (jnp.arange(LANES)[:, None] > jnp.arange(LANES)[None, :]).astype(BF16)
    tri = jnp.concatenate([tri, tri], axis=0)
    local = {n: (jnp.swapaxes(t, 1, 2) if n == "w_in" else t) for n, t in weights.items()}
    local_m = {n: (jnp.swapaxes(t, 1, 2) if n == "w_in" else t) for n, t in mom_m.items()}
    local_v = {n: (jnp.swapaxes(t, 1, 2) if n == "w_in" else t) for n, t in mom_v.items()}
    shards = [[local[n][l].astype(BF16) for n in MATRIX_NAMES] for l in range(DEPTH)]
    core = lax.axis_index("c").astype(jnp.int32).reshape(1)
    chip = (2 * lax.axis_index("x") + lax.axis_index("y")).astype(jnp.int32).reshape(1)

    def gather(l, ws, rows=None, stacks=None):
        return GatherJob([shards[l][w] for w in ws], rows=rows, stacks=stacks)

    def halves(w):
        r = shards[0][w].shape[0] // 2
        return (0, r), (r, r)

    w_in = _natural(exchange_alone(gather(0, [W_IN]), name="gather_w_in0")[0], W_IN)
    saved = []
    for l in range(DEPTH):
        g_mix = weights["mix_norm_g"][l].reshape(1, D_MODEL)
        g_mlp = weights["mlp_norm_g"][l].reshape(1, D_MODEL)
        gq = _lane_tile(weights["q_norm_g"][l], LANES // HEAD_DIM)
        gk = _lane_tile(weights["k_norm_g"][l], LANES // HEAD_DIM)
        sink_p = jnp.repeat(weights["sinks"][l].reshape(SWA_Q_WIDTH // LANES, 2), HEAD_DIM, axis=1)
        sink_p = sink_p.reshape(SWA_Q_WIDTH // LANES, 1, LANES)
        (h, proj, gates), ((s_bsb, s_bsw, s_out), (s_down,)) = norm_matmul(
            x, g_mix, w_in, gate_split=ATTN_WIDTH, name="in_proj",
            ride=[gather(l, [W_BSB, W_BSW, W_OUT]), gather(l, [W_DOWN], rows=halves(W_DOWN)[0])])
        o_sb, ((s_up,),) = sb_attn_fwd(proj, tri, name="sb_fwd", ride=[gather(l, [W_UP])])
        qn, kn = swa_prep_fwd(proj, cos_p, sin_p, gq, gk, name="swa_prep")
        o_sw, _ = swa_attn_fwd(qn, kn, proj, sink_p, name="swa_fwd")
        more = l + 1 < DEPTH
        (x1, y_sb, y_sw, merged), rides = merge_out_fwd(
            x, o_sb, o_sw, gates, _natural(s_bsb, W_BSB), _natural(s_bsw, W_BSW), _natural(s_out, W_OUT),
            name="merge_out" if more else "merge_out_last",
            ride=[gather(l + 1, [W_IN], rows=halves(W_IN)[0])] if more else [])
        (h2, u), ((s_down,),) = norm_matmul(x1, g_mlp, s_up, gate_split=None, name="mlp_up",
                                            ride=[gather(l, [W_DOWN], rows=halves(W_DOWN)[1], stacks=[s_down])])
        mats = [w_in, _natural(s_bsb, W_BSB), _natural(s_bsw, W_BSW), _natural(s_out, W_OUT), s_up,
                _natural(s_down, W_DOWN)]
        if more:
            (x2,), ((s_in,),) = mlp_down_fwd(x1, u, mats[W_DOWN], name="mlp_down",
                                             ride=[gather(l + 1, [W_IN], rows=halves(W_IN)[1], stacks=rides[0])])
            w_in = _natural(s_in, W_IN)
        else:
            (x2,), _ = mlp_down_fwd(x1, u, mats[W_DOWN], name="mlp_down_last")
        saved.append(dict(x=x, h=h, proj=proj, gates=gates, o_sb=o_sb, qn=qn, kn=kn, o_sw=o_sw, y_sb=y_sb, y_sw=y_sw,
                          merged=merged, x1=x1, h2=h2, u=u, g_mix=g_mix, g_mlp=g_mlp, gq=gq, gk=gk, sink_p=sink_p,
                          mats=mats))
        x = x2

    dx, dxb, loss = loss_head(x, target, name="loss_head")

    shard_shapes = [local[n].shape[1:] for n in MATRIX_NAMES]
    parts = [lax.empty((DEPTH, N_CHIPS) + sh, BF16) for sh in shard_shapes]
    lands = [lax.empty((DEPTH, 3) + sh, BF16) for sh in shard_shapes]
    small_grads = {n: [None] * DEPTH for n in SMALL_NAMES}
    half = D_MODEL // 2

    def summed(l, ws, grads, landed):
        new = pair_sum(l, grads, landed, [parts[w] for w in ws], core, name="grad_pair_sum")
        for w, p in zip(ws, new):
            parts[w] = p

    def chip_job(items):
        return ChipJob(items, parts, lands)

    def landed_chip(job, outs):
        for w, a in zip(job.ws, outs):
            lands[w] = a

    in_pending = None
    for l in reversed(range(DEPTH)):
        a = saved[l]
        mats = a["mats"]
        in_jobs = [chip_job([(in_pending, W_IN, rows)]) for rows in halves(W_IN)] if in_pending is not None else []
        (du,), rides = mlp_bwd_up(dxb, a["u"], mats[W_DOWN], name="mlp_bwd_up" if in_jobs else "mlp_bwd_up_first",
                                  ride=in_jobs[:1])
        if in_jobs:
            landed_chip(in_jobs[0], rides[0])
            in_jobs[1] = chip_job([(in_pending, W_IN, halves(W_IN)[1])])
        dw_down = matmul_tn(a["u"], [dxb], a_block=half, out_cols=None, relu2=True, name="dw_down")
        dw_up = matmul_tn(a["h2"], [du], a_block=half, out_cols=du.shape[1] // N_DEV, relu2=False, name="dw_up")
        g_mlp_w = [dw_up, dw_down.reshape((N_DEV,) + shard_shapes[W_DOWN])]
        (dx1, dx1b, dg_mlp), rides = matmul_nt_norm_bwd(
            [du], mats[W_UP], a["x1"], a["g_mlp"], dx, name="mlp_bwd_norm" if in_jobs else "mlp_bwd_norm_first",
            ride=[PairJob(g_mlp_w)] + in_jobs[1:])
        if in_jobs:
            landed_chip(in_jobs[1], rides[1])
        summed(l, [W_UP, W_DOWN], g_mlp_w, rides[0])
        small_grads["mlp_norm_g"][l] = dg_mlp.reshape(D_MODEL)
        dw_out = matmul_tn(a["merged"], [dx1b], a_block=half, out_cols=None, relu2=False, name="dw_out")
        dy_sb, dy_sw, do_sb, do_sw, dgl = out_bwd(dx1b, mats[W_OUT], a["gates"], a["y_sb"], a["y_sw"],
                                                  mats[W_BSB], mats[W_BSW], name="out_bwd")
        dw_bsb = matmul_tn(a["o_sb"], [dy_sb], a_block=half, out_cols=D_MODEL // N_DEV, relu2=False, name="dw_branch_sb")
        dw_bsw = matmul_tn(a["o_sw"], [dy_sw], a_block=half, out_cols=D_MODEL // N_DEV, relu2=False, name="dw_branch_swa")
        g_mix_w = [dw_bsb, dw_bsw, dw_out.reshape((N_DEV,) + shard_shapes[W_OUT])]
        job = chip_job([(l, W_UP), (l, W_DOWN)])
        dq_sb, dk_sb, dv_sb, (outs, landed) = sb_attn_bwd(a["proj"], tri, a["o_sb"], do_sb, name="sb_bwd",
                                                         ride=[job, PairJob(g_mix_w)])
        landed_chip(job, outs)
        summed(l, [W_BSB, W_BSW, W_OUT], g_mix_w, landed)
        dqn, dkn, dv_sw, dsink, _ = swa_attn_bwd(a["qn"], a["kn"], a["proj"], a["sink_p"], a["o_sw"], do_sw,
                                                name="swa_bwd")
        dq_sw, dk_sw, dv_swb, dgq, dgk = swa_prep_bwd(a["proj"], cos_p, sin_p, a["gq"], a["gk"], dqn, dkn, dv_sw,
                                                      name="swa_prep_bwd")
        small_grads["q_norm_g"][l] = dgq.reshape(SWA_Q_WIDTH // HEAD_DIM, HEAD_DIM).sum(0)
        small_grads["k_norm_g"][l] = dgk.reshape(LANES // HEAD_DIM, HEAD_DIM).sum(0)
        small_grads["sinks"][l] = dsink[:, 0, ::HEAD_DIM].reshape(SWA_Q_WIDTH // HEAD_DIM)
        pieces = [dq_sb, dk_sb, dv_sb, dq_sw, dk_sw, dv_swb, dgl]
        g_in = [matmul_tn_row_blocks(pieces, a["h"], n_blocks=N_DEV, name="dw_in")]
        mix_items = [(l, W_BSB), (l, W_BSW), (l, W_OUT)]
        if l > 0:
            job = chip_job(mix_items)
            (dx, dxb, dg_mix), (outs, landed) = matmul_nt_norm_bwd(pieces, mats[W_IN], a["x"], a["g_mix"], dx1,
                                                                  name="in_proj_bwd", ride=[job, PairJob(g_in)])
            landed_chip(job, outs)
            summed(l, [W_IN], g_in, landed)
            in_pending = l
        else:
            summed(l, [W_IN], g_in, exchange_alone(PairJob(g_in), name="grad_pair_exchange_in0"))
            job = chip_job(mix_items + [(l, W_IN)])
            (dx, dxb, dg_mix), (outs,) = matmul_nt_norm_bwd(pieces, mats[W_IN], a["x"], a["g_mix"], dx1,
                                                           name="in_proj_bwd_last", ride=[job])
            landed_chip(job, outs)
        small_grads["mix_norm_g"][l] = dg_mix.reshape(D_MODEL)

    out_g, out_d, out_m, out_v = {}, {}, {}, {}
    for i, n in enumerate(MATRIX_NAMES):
        outs = reduce_adamw(parts[i], lands[i], chip, local[n], local_m[n], local_v[n], name="adamw_" + n)
        if n == "w_in":
            outs = [jnp.swapaxes(t, 1, 2) for t in outs]
        out_g[n], out_d[n], out_m[n], out_v[n] = outs
    small_shapes = {n: weights[n].shape for n in SMALL_NAMES}
    small_all = gather_small(_pack_small({n: jnp.stack(v) for n, v in small_grads.items()}), name="gather_small_grads")
    sg, sd, sm, sv = small_adamw(small_all, _pack_small(weights), _pack_small(mom_m), _pack_small(mom_v),
                                 name="small_adamw")
    for tree, packed_small in ((out_g, sg), (out_d, sd), (out_m, sm), (out_v, sv)):
        tree.update(_unpack_small(packed_small, small_shapes))
    return loss, dx, (out_g, out_d, out_m, out_v)


def kernel(x, mix_norm_g, w_in, q_norm_g, k_norm_g, sinks, w_branch_sb, w_branch_swa, w_out, mlp_norm_g, w_up, w_down, loss_target, m_mix_norm_g, m_w_in, m_q_norm_g, m_k_norm_g, m_sinks, m_w_branch_sb, m_w_branch_swa, m_w_out, m_mlp_norm_g, m_w_up, m_w_down, v_mix_norm_g, v_w_in, v_q_norm_g, v_k_norm_g, v_sinks, v_w_branch_sb, v_w_branch_swa, v_w_out, v_mlp_norm_g, v_w_up, v_w_down):
    weights = dict(mix_norm_g=mix_norm_g, w_in=w_in, q_norm_g=q_norm_g, k_norm_g=k_norm_g, sinks=sinks,
                   w_branch_sb=w_branch_sb, w_branch_swa=w_branch_swa, w_out=w_out, mlp_norm_g=mlp_norm_g, w_up=w_up,
                   w_down=w_down)
    mom_m = dict(mix_norm_g=m_mix_norm_g, w_in=m_w_in, q_norm_g=m_q_norm_g, k_norm_g=m_k_norm_g, sinks=m_sinks,
                 w_branch_sb=m_w_branch_sb, w_branch_swa=m_w_branch_swa, w_out=m_w_out, mlp_norm_g=m_mlp_norm_g,
                 w_up=m_w_up, w_down=m_w_down)
    mom_v = dict(mix_norm_g=v_mix_norm_g, w_in=v_w_in, q_norm_g=v_q_norm_g, k_norm_g=v_k_norm_g, sinks=v_sinks,
                 w_branch_sb=v_w_branch_sb, w_branch_swa=v_w_branch_swa, w_out=v_w_out, mlp_norm_g=v_mlp_norm_g,
                 w_up=v_w_up, w_down=v_w_down)
    loss_part, grad_x, outs = train_step(x[0], loss_target[0], weights, mom_m, mom_v)
    loss = lax.psum(loss_part[0, 0], MESH_AXES)
    return (loss, grad_x[None], *[outs[0][n] for n in WEIGHT_ORDER], *[outs[1][n] for n in WEIGHT_ORDER],
            *[outs[2][n] for n in WEIGHT_ORDER], *[outs[3][n] for n in WEIGHT_ORDER])
```

```python
import math

import jax
import jax.numpy as jnp
from jax import lax
from jax.experimental import pallas as pl
from jax.experimental.pallas import tpu as pltpu

F32 = jnp.float32
BF16 = jnp.bfloat16

DEPTH = 4
D_MODEL = 1024
HEAD_DIM = 64
LANES = 128
WINDOW = 128
SB_WIDTH = 512
SWA_Q_WIDTH = 512
SWA_KV_WIDTH = 128
ATTN_WIDTH = 3 * SB_WIDTH + SWA_Q_WIDTH + 2 * SWA_KV_WIDTH
ROPE_THETA = 10000.0
NORM_EPS = 1e-6
SCALE = HEAD_DIM ** -0.5
NEG = -1e30
N_DEV = 8
N_CHIPS = 4

ADAM_LR = 0.001
ADAM_B1 = 0.9
ADAM_B2 = 0.999
ADAM_EPS = 1e-08
ADAM_WD = 0.01
ADAM_STEP = 10

SB_TQ = 128
SB_TK1 = 384
SB_TK = 256
SB_CUTOFF = -88.0
SWA_TQ = 128
SWA_TK = 256
ROW_TILE = 512
VMEM_LIMIT = 56 * 1024 * 1024

MATRIX_NAMES = ("w_in", "w_branch_sb", "w_branch_swa", "w_out", "w_up", "w_down")
W_IN, W_BSB, W_BSW, W_OUT, W_UP, W_DOWN = range(6)
ROW_SHARDED = ("w_out", "w_down")
SMALL_NAMES = ("mix_norm_g", "q_norm_g", "k_norm_g", "sinks", "mlp_norm_g")
WEIGHT_ORDER = ("mix_norm_g", "w_in", "q_norm_g", "k_norm_g", "sinks", "w_branch_sb", "w_branch_swa", "w_out",
                "mlp_norm_g", "w_up", "w_down")
MESH_AXES = ("x", "y", "c")

ANY = pl.BlockSpec(memory_space=pl.ANY)
MESH = pl.DeviceIdType.MESH


def _params(*sem):
    return pltpu.CompilerParams(dimension_semantics=sem, vmem_limit_bytes=VMEM_LIMIT)


def _dot(a, b):
    return jnp.dot(a, b, preferred_element_type=F32)


def _dot_nt(a, b):
    return lax.dot_general(a, b, (((1,), (1,)), ((), ())), preferred_element_type=F32)


def _dot_tn(a, b):
    return lax.dot_general(a, b, (((0,), (0,)), ((), ())), preferred_element_type=F32)


def _split_bf16(x):
    hi = lax.bitcast_convert_type(lax.bitcast_convert_type(x, jnp.uint32) & jnp.uint32(0xFFFF0000), F32)
    return hi.astype(BF16), (x - hi).astype(BF16)


def _rsqrt_ms(x):
    return lax.rsqrt(jnp.mean(x * x, axis=-1, keepdims=True) + NORM_EPS)


def _place():
    return lax.axis_index("x"), lax.axis_index("y"), lax.axis_index("c")


class _Gather:
    def __init__(self, x_refs, out_refs, send_sems, recv_sems, local_sems, rows=None):
        self.x_refs, self.out_refs = x_refs, out_refs
        self.send_sems, self.recv_sems, self.local_sems = send_sems, recv_sems, local_sems
        self.n = len(x_refs)
        self.rows = rows
        x, y, c = _place()
        self.c = c
        self.me, self.sibling = (x, y, c), (x, y, 1 - c)
        self.chips = [(1 - x, y), (x, 1 - y), (1 - x, 1 - y)]

    def _part(self, ref):
        return ref if self.rows is None else ref.at[pl.ds(*self.rows)]

    def _slot(self, w, blk):
        return self._part(self.out_refs[w].at[4 * blk[0] + 2 * blk[1] + blk[2]])

    def _copy(self, k, w, blk, to, own=False):
        dst = self._slot(w, blk)
        return pltpu.make_async_remote_copy(
            src_ref=self._part(self.x_refs[w]) if own else dst, dst_ref=dst, send_sem=self.send_sems.at[k, w],
            recv_sem=self.recv_sems.at[k, w], device_id=to, device_id_type=MESH)

    def _mine(self, w):
        return pltpu.make_async_copy(self._part(self.x_refs[w]), self._slot(w, self.me), self.local_sems.at[w])

    def _first(self, w):
        return [self._copy(0, w, self.me, self.sibling, own=True)] + [
            self._copy(1 + j, w, self.me, (*chip, self.c), own=True) for j, chip in enumerate(self.chips)]

    def _passed(self, j, w):
        return self._copy(4 + j, w, (*self.chips[j], self.c), self.sibling)

    def start(self):
        for w in range(self.n):
            self._mine(w).start()
            for cp in self._first(w):
                cp.start()

    def relay(self):
        for j, chip in enumerate(self.chips):
            for w in range(self.n):
                self._copy(1 + j, w, (*chip, self.c), self.me).wait_recv()
                self._passed(j, w).start()

    def finish(self):
        for w in range(self.n):
            self._copy(0, w, self.sibling, self.me).wait_recv()
            for j, chip in enumerate(self.chips):
                self._copy(4 + j, w, (*chip, 1 - self.c), self.me).wait_recv()
            for cp in self._first(w):
                cp.wait_send()
            for j in range(3):
                self._passed(j, w).wait_send()
            self._mine(w).wait()


class GatherJob:
    def __init__(self, shards, rows=None, stacks=None):
        n = len(shards)
        self.n, self.rows = n, rows
        self.inputs = list(shards) + (list(stacks) if stacks is not None else [])
        self.out_shapes = [jax.ShapeDtypeStruct((N_DEV,) + s.shape, s.dtype) for s in shards]
        self.aliases = {n + i: i for i in range(n)} if stacks is not None else {}
        self.scratch = [pltpu.SemaphoreType.DMA((7, n)), pltpu.SemaphoreType.DMA((7, n)),
                        pltpu.SemaphoreType.DMA((n,))]

    def bind(self, in_refs, out_refs, scratch_refs):
        return _Gather(in_refs[:self.n], out_refs, *scratch_refs, rows=self.rows)


class _Copies:
    def __init__(self, copies):
        self.copies = copies

    def start(self):
        for cp in self.copies:
            cp.start()

    def relay(self):
        pass

    def finish(self):
        for cp in self.copies:
            cp.wait_recv()
        for cp in self.copies:
            cp.wait_send()


class ChipJob:
    def __init__(self, items, parts, lands):
        self.ws = sorted({item[1] for item in items})
        n = len(self.ws)
        self.items = [(item[0], self.ws.index(item[1]), item[2] if len(item) > 2 else None) for item in items]
        self.inputs = [parts[w] for w in self.ws] + [lands[w] for w in self.ws]
        self.out_shapes = [jax.ShapeDtypeStruct(lands[w].shape, lands[w].dtype) for w in self.ws]
        self.aliases = {n + i: i for i in range(n)}
        self.scratch = [pltpu.SemaphoreType.DMA((3, n)), pltpu.SemaphoreType.DMA((3, n))]

    def bind(self, in_refs, out_refs, scratch_refs):
        send_sems, recv_sems = scratch_refs
        x, y, c = _place()
        chips = [(1 - x, y), (x, 1 - y), (1 - x, 1 - y)]

        def part(ref, rows):
            return ref if rows is None else ref.at[pl.ds(*rows)]

        return _Copies([pltpu.make_async_remote_copy(
            src_ref=part(in_refs[i].at[layer, 2 * px + py], rows), dst_ref=part(out_refs[i].at[layer, j], rows),
            send_sem=send_sems.at[j, i], recv_sem=recv_sems.at[j, i], device_id=(px, py, c), device_id_type=MESH)
            for layer, i, rows in self.items for j, (px, py) in enumerate(chips)])


class PairJob:
    def __init__(self, grads):
        n = len(grads)
        self.inputs = list(grads)
        self.out_shapes = [jax.ShapeDtypeStruct((N_CHIPS,) + g.shape[1:], g.dtype) for g in grads]
        self.aliases = {}
        self.scratch = [pltpu.SemaphoreType.DMA((N_CHIPS, n)), pltpu.SemaphoreType.DMA((N_CHIPS, n))]

    def bind(self, in_refs, out_refs, scratch_refs):
        send_sems, recv_sems = scratch_refs
        x, y, c = _place()
        return _Copies([pltpu.make_async_remote_copy(
            src_ref=in_refs[w].at[2 * k + (1 - c)], dst_ref=out_refs[w].at[k], send_sem=send_sems.at[k, w],
            recv_sem=recv_sems.at[k, w], device_id=(x, y, 1 - c), device_id_type=MESH)
            for w in range(len(in_refs)) for k in range(N_CHIPS)])


def _call(body, *, name, grid, in_specs, out_specs, out_shape, args, scratch_shapes=(), ride=()):
    out_specs, out_shape, in_specs = tuple(out_specs), tuple(out_shape), list(in_specs)
    scratch_shapes = list(scratch_shapes)
    order = ("arbitrary",) * len(grid)
    if not ride:
        outs = pl.pallas_call(body, name=name, grid=grid, in_specs=in_specs, out_specs=out_specs, out_shape=out_shape,
                              scratch_shapes=scratch_shapes, compiler_params=_params(*order))(*args)
        return tuple(outs), []
    n_in, n_out, n_scr = len(in_specs), len(out_specs), len(scratch_shapes)
    n_steps = math.prod(grid)
    relay_early = n_steps >= 8
    relay_at = n_steps - n_steps // 4 if relay_early else n_steps - 1

    def split(refs, pos, counts):
        groups = []
        for k in counts:
            groups.append(refs[pos:pos + k])
            pos += k
        return groups, pos

    def wrapped(*refs):
        ins, pos = refs[:n_in], n_in
        job_in, pos = split(refs, pos, [len(j.inputs) for j in ride])
        outs, pos = refs[pos:pos + n_out], pos + n_out
        job_out, pos = split(refs, pos, [len(j.out_shapes) for j in ride])
        scr, pos = refs[pos:pos + n_scr], pos + n_scr
        job_scr, pos = split(refs, pos, [len(j.scratch) for j in ride])
        bound = [j.bind(i, o, s) for j, i, o, s in zip(ride, job_in, job_out, job_scr)]
        step = pl.program_id(0)
        for axis in range(1, len(grid)):
            step = step * grid[axis] + pl.program_id(axis)

        @pl.when(step == 0)
        def _():
            for b in bound:
                b.start()

        if relay_early:
            @pl.when(step == relay_at)
            def _():
                for b in bound:
                    b.relay()

        body(*ins, *outs, *scr)

        @pl.when(step == n_steps - 1)
        def _():
            if not relay_early:
                for b in bound:
                    b.relay()
            for b in bound:
                b.finish()

    aliases, in_pos, out_pos = {}, n_in, n_out
    for j in ride:
        aliases.update({in_pos + i: out_pos + o for i, o in j.aliases.items()})
        in_pos += len(j.inputs)
        out_pos += len(j.out_shapes)
    results = pl.pallas_call(
        wrapped, name=name, grid=grid, in_specs=in_specs + [ANY] * (in_pos - n_in),
        out_specs=out_specs + (ANY,) * (out_pos - n_out),
        out_shape=out_shape + tuple(s for j in ride for s in j.out_shapes),
        scratch_shapes=scratch_shapes + [s for j in ride for s in j.scratch], input_output_aliases=aliases,
        compiler_params=pltpu.CompilerParams(dimension_semantics=order, vmem_limit_bytes=VMEM_LIMIT,
                                             has_side_effects=True),
    )(*args, *[a for j in ride for a in j.inputs])
    job_results, pos = split(list(results), n_out, [len(j.out_shapes) for j in ride])
    return tuple(results[:n_out]), job_results


def exchange_alone(job, *, name):
    n_in, n_out = len(job.inputs), len(job.out_shapes)

    def body(*refs):
        b = job.bind(refs[:n_in], refs[n_in:n_in + n_out], refs[n_in + n_out:])
        b.start()
        b.relay()
        b.finish()

    return list(pl.pallas_call(
        body, name=name, out_shape=tuple(job.out_shapes), in_specs=[ANY] * n_in, out_specs=(ANY,) * n_out,
        scratch_shapes=job.scratch, input_output_aliases=job.aliases,
        compiler_params=pltpu.CompilerParams(has_side_effects=True),
    )(*job.inputs))


PAIR_SUM_CHUNKS = 1


def pair_sum(layer, grads, landed, parts, core, *, name):
    n = len(grads)

    def body(c_ref, *refs):
        g_refs, l_refs, o_refs = refs[:n], refs[n:2 * n], refs[3 * n:]
        for w in range(n):
            o_refs[w][...] = (g_refs[w][...].astype(F32) + l_refs[w][...].astype(F32)).astype(BF16)

    def blk(g):
        return (None, g.shape[1] // PAIR_SUM_CHUNKS, g.shape[2])

    in_specs = [pl.BlockSpec(blk(g), lambda k, i, c_ref: (2 * k + c_ref[0], i, 0)) for g in grads]
    in_specs += [pl.BlockSpec(blk(g), lambda k, i, c_ref: (k, i, 0)) for g in grads]
    in_specs += [ANY] * n
    out_specs = tuple(pl.BlockSpec((None,) + blk(g), lambda k, i, c_ref: (layer, k, i, 0)) for g in grads)
    return list(pl.pallas_call(
        body, name=name, out_shape=tuple(jax.ShapeDtypeStruct(p.shape, p.dtype) for p in parts),
        grid_spec=pltpu.PrefetchScalarGridSpec(num_scalar_prefetch=1, grid=(N_CHIPS, PAIR_SUM_CHUNKS),
                                               in_specs=in_specs, out_specs=out_specs),
        input_output_aliases={1 + 2 * n + w: w for w in range(n)},
        compiler_params=_params("parallel", "parallel"),
    )(core, *grads, *landed, *parts))


def _adamw(w, g, m, v):
    m = ADAM_B1 * m + (1.0 - ADAM_B1) * g
    v = ADAM_B2 * v + (1.0 - ADAM_B2) * (g * g)
    m_hat = m / (1.0 - ADAM_B1 ** ADAM_STEP)
    v_hat = v / (1.0 - ADAM_B2 ** ADAM_STEP)
    delta = -ADAM_LR * (m_hat / (jnp.sqrt(v_hat) + ADAM_EPS) + ADAM_WD * w)
    return delta, m, v


def reduce_adamw(part, land, chip, w, m, v, *, name):
    _, r, c = w.shape
    tr = 256 if r % 256 == 0 else (r // 2 if r > 256 else r)

    def body(k_ref, own_ref, l0_ref, l1_ref, l2_ref, w_ref, m_ref, v_ref, g_out, d_out, m_out, v_out):
        g = own_ref[...].astype(F32) + l0_ref[...].astype(F32) + l1_ref[...].astype(F32) + l2_ref[...].astype(F32)
        delta, m_new, v_new = _adamw(w_ref[...], g, m_ref[...], v_ref[...])
        g_out[...] = g
        d_out[...] = delta
        m_out[...] = m_new
        v_out[...] = v_new

    row = pl.BlockSpec((None, tr, c), lambda l, i, k_ref: (l, i, 0))

    def slot(j):
        return pl.BlockSpec((None, None, tr, c), lambda l, i, k_ref: (l, j, i, 0))

    return pl.pallas_call(
        body, name=name, out_shape=(jax.ShapeDtypeStruct(w.shape, F32),) * 4,
        grid_spec=pltpu.PrefetchScalarGridSpec(
            num_scalar_prefetch=1, grid=(DEPTH, r // tr),
            in_specs=[pl.BlockSpec((None, None, tr, c), lambda l, i, k_ref: (l, k_ref[0], i, 0)), slot(0), slot(1),
                      slot(2), row, row, row],
            out_specs=(row, row, row, row)),
        compiler_params=_params("parallel", "parallel"),
    )(chip, part, land, land, land, w, m, v)


def gather_small(block, *, name):
    def body(x_ref, out_ref, send_sems, recv_sems, local_sem):
        x, y, c = _place()
        me = 4 * x + 2 * y + c
        mine = pltpu.make_async_copy(x_ref, out_ref.at[me], local_sem)
        mine.start()
        peers = [(x ^ (k >> 2), y ^ ((k >> 1) & 1), c ^ (k & 1)) for k in range(1, N_DEV)]
        copies = [pltpu.make_async_remote_copy(
            src_ref=x_ref, dst_ref=out_ref.at[me], send_sem=send_sems.at[k], recv_sem=recv_sems.at[k],
            device_id=peer, device_id_type=MESH) for k, peer in enumerate(peers)]
        for cp in copies:
            cp.start()
        for k, (px, py, pc) in enumerate(peers):
            pltpu.make_async_remote_copy(
                src_ref=x_ref, dst_ref=out_ref.at[4 * px + 2 * py + pc], send_sem=send_sems.at[k],
                recv_sem=recv_sems.at[k], device_id=(px, py, pc), device_id_type=MESH).wait_recv()
        for cp in copies:
            cp.wait_send()
        mine.wait()

    return pl.pallas_call(
        body, name=name, out_shape=jax.ShapeDtypeStruct((N_DEV,) + block.shape, block.dtype),
        in_specs=[ANY], out_specs=ANY,
        scratch_shapes=[pltpu.SemaphoreType.DMA((7,)), pltpu.SemaphoreType.DMA((7,)), pltpu.SemaphoreType.DMA],
        compiler_params=pltpu.CompilerParams(has_side_effects=True),
    )(block)


def small_adamw(gathered, w, m, v, *, name):
    def body(g_ref, w_ref, m_ref, v_ref, g_out, d_out, m_out, v_out):
        g = g_ref[0]
        for d in range(1, N_DEV):
            g = g + g_ref[d]
        delta, m_new, v_new = _adamw(w_ref[...], g, m_ref[...], v_ref[...])
        g_out[...] = g
        d_out[...] = delta
        m_out[...] = m_new
        v_out[...] = v_new

    return pl.pallas_call(
        body, name=name, out_shape=(jax.ShapeDtypeStruct(w.shape, F32),) * 4,
    )(gathered, w, m, v)


def norm_matmul(x, g, w, *, gate_split, name, ride=()):
    s, d = x.shape
    tm = min(ROW_TILE, s)
    blocked = w.ndim == 3
    n = w.shape[0] if not blocked else w.shape[0] * w.shape[2]

    def body(x_ref, g_ref, w_ref, h_ref, *outs):
        xv = x_ref[...]
        h = ((xv * _rsqrt_ms(xv)) * g_ref[...]).astype(BF16)
        h_ref[...] = h
        if blocked:
            nb = w_ref.shape[2]
            for j in range(w_ref.shape[0]):
                outs[0][:, j * nb:(j + 1) * nb] = _dot(h, w_ref[j]).astype(BF16)
        else:
            p = _dot_nt(h, w_ref[...])
            outs[0][...] = p[:, :gate_split].astype(BF16)
            outs[1][...] = (1.0 / (1.0 + jnp.exp(-p[:, gate_split:]))).astype(BF16)

    row = lambda i: (i, 0)
    fixed = lambda i: (0, 0)
    if blocked:
        out_shape = (jax.ShapeDtypeStruct((s, d), BF16), jax.ShapeDtypeStruct((s, n), BF16))
        out_specs = (pl.BlockSpec((tm, d), row), pl.BlockSpec((tm, n), row))
        w_spec = pl.BlockSpec(w.shape, lambda i: (0, 0, 0))
    else:
        out_shape = (jax.ShapeDtypeStruct((s, d), BF16), jax.ShapeDtypeStruct((s, gate_split), BF16),
                     jax.ShapeDtypeStruct((s, n - gate_split), BF16))
        out_specs = (pl.BlockSpec((tm, d), row), pl.BlockSpec((tm, gate_split), row),
                     pl.BlockSpec((tm, n - gate_split), row))
        w_spec = pl.BlockSpec((n, d), fixed)
    return _call(body, name=name, grid=(s // tm,), out_shape=out_shape, out_specs=out_specs,
                 in_specs=[pl.BlockSpec((tm, d), row), pl.BlockSpec((1, d), fixed), w_spec], args=(x, g, w), ride=ride)


def merge_out_fwd(x, o_sb, o_sw, gates, w_bsb, w_bsw, w_o, *, name, ride=()):
    s, d = x.shape
    tm = min(ROW_TILE, s)

    def body(x_ref, osb_ref, osw_ref, g_ref, wsb_ref, wsw_ref, wo_ref, x1_ref, ysb_ref, ysw_ref, mg_ref):
        y_sb = _dot(osb_ref[...].astype(BF16), wsb_ref[...])
        y_sw = _dot(osw_ref[...].astype(BF16), wsw_ref[...])
        g = g_ref[...].astype(F32)
        merged = (g[:, :d] * y_sb + g[:, d:] * y_sw).astype(BF16)
        ysb_ref[...] = y_sb.astype(BF16)
        ysw_ref[...] = y_sw.astype(BF16)
        mg_ref[...] = merged
        x1_ref[...] = x_ref[...] + _dot(merged, wo_ref[...])

    row = lambda i: (i, 0)
    fixed = lambda i: (0, 0)
    wd = o_sb.shape[1]
    return _call(
        body, name=name, grid=(s // tm,),
        out_shape=(jax.ShapeDtypeStruct((s, d), F32),) + (jax.ShapeDtypeStruct((s, d), BF16),) * 3,
        in_specs=[pl.BlockSpec((tm, d), row), pl.BlockSpec((tm, wd), row), pl.BlockSpec((tm, wd), row),
                  pl.BlockSpec((tm, 2 * d), row), pl.BlockSpec((wd, d), fixed), pl.BlockSpec((wd, d), fixed),
                  pl.BlockSpec((d, d), fixed)],
        out_specs=(pl.BlockSpec((tm, d), row),) * 4, args=(x, o_sb, o_sw, gates, w_bsb, w_bsw, w_o), ride=ride)


def mlp_down_fwd(x1, u, w_down, *, name, ride=()):
    s, d = x1.shape
    f = u.shape[1]
    tm = min(ROW_TILE, s)

    def body(x_ref, u_ref, w_ref, o_ref):
        a = jnp.maximum(u_ref[...].astype(F32), 0.0)
        o_ref[...] = x_ref[...] + _dot((a * a).astype(BF16), w_ref[...])

    row = lambda i: (i, 0)
    return _call(
        body, name=name, grid=(s // tm,), out_shape=(jax.ShapeDtypeStruct((s, d), F32),),
        in_specs=[pl.BlockSpec((tm, d), row), pl.BlockSpec((tm, f), row), pl.BlockSpec((f, d), lambda i: (0, 0))],
        out_specs=(pl.BlockSpec((tm, d), row),), args=(x1, u, w_down), ride=ride)


def loss_head(y, target, *, name):
    s, d = y.shape
    tm = min(ROW_TILE, s)

    def body(y_ref, t_ref, dy_ref, dyb_ref, loss_ref):
        @pl.when(pl.program_id(0) == 0)
        def _():
            loss_ref[...] = jnp.zeros_like(loss_ref)

        e = y_ref[...] - t_ref[...]
        dy = e * (1.0 / d)
        dy_ref[...] = dy
        dyb_ref[...] = dy.astype(BF16)
        per_row = jnp.sum(e * e, axis=1, keepdims=True) * (0.5 / d)
        loss_ref[...] += jnp.sum(per_row, axis=0, keepdims=True)

    row = lambda i: (i, 0)
    return pl.pallas_call(
        body, name=name, grid=(s // tm,),
        out_shape=(jax.ShapeDtypeStruct((s, d), F32), jax.ShapeDtypeStruct((s, d), BF16),
                   jax.ShapeDtypeStruct((1, 1), F32)),
        in_specs=[pl.BlockSpec((tm, d), row), pl.BlockSpec((tm, d), row)],
        out_specs=(pl.BlockSpec((tm, d), row), pl.BlockSpec((tm, d), row), pl.BlockSpec((1, 1), lambda i: (0, 0))),
        compiler_params=_params("arbitrary"),
    )(y, target)


def mlp_bwd_up(dxb, u, w_down, *, name, ride=()):
    s, d = dxb.shape
    f = u.shape[1]
    tm = min(ROW_TILE, s)

    def body(dx_ref, u_ref, w_ref, du_ref):
        da = _dot_nt(dx_ref[...], w_ref[...])
        du_ref[...] = (da * (2.0 * jnp.maximum(u_ref[...].astype(F32), 0.0))).astype(BF16)

    row = lambda i: (i, 0)
    return _call(body, name=name, grid=(s // tm,), out_shape=(jax.ShapeDtypeStruct((s, f), BF16),),
                 in_specs=[pl.BlockSpec((tm, d), row), pl.BlockSpec((tm, f), row),
                           pl.BlockSpec((f, d), lambda i: (0, 0))],
                 out_specs=(pl.BlockSpec((tm, f), row),), args=(dxb, u, w_down), ride=ride)


def matmul_nt_norm_bwd(pieces, w, x, g, dres, *, name, ride=()):
    s = x.shape[0]
    d = x.shape[1]
    tm = min(ROW_TILE, s)
    blocked = w.ndim == 3
    n_pieces = len(pieces)
    widths = [p.shape[1] for p in pieces]

    def body(*refs):
        p_refs = refs[:n_pieces]
        w_ref, x_ref, g_ref, dres_ref, dx_ref, dxb_ref, dg_ref = refs[n_pieces:]

        @pl.when(pl.program_id(0) == 0)
        def _():
            dg_ref[...] = jnp.zeros_like(dg_ref)

        if blocked:
            nb = w_ref.shape[2]
            dh = _dot_nt(p_refs[0][:, :nb], w_ref[0])
            for j in range(1, w_ref.shape[0]):
                dh = dh + _dot_nt(p_refs[0][:, j * nb:(j + 1) * nb], w_ref[j])
        else:
            dh, off = None, 0
            for p_ref, width in zip(p_refs, widths):
                part = _dot(p_ref[...], w_ref[off:off + width, :])
                dh = part if dh is None else dh + part
                off += width
        xv = x_ref[...]
        r = _rsqrt_ms(xv)
        dyg = dh * g_ref[...]
        dx = dres_ref[...] + r * dyg - xv * ((r * r * r) * jnp.mean(dyg * xv, axis=-1, keepdims=True))
        dx_ref[...] = dx
        dxb_ref[...] = dx.astype(BF16)
        dg_ref[...] += jnp.sum(dh * (xv * r), axis=0, keepdims=True)

    row = lambda i: (i, 0)
    fixed = lambda i: (0, 0)
    w_spec = pl.BlockSpec(w.shape, (lambda i: (0, 0, 0)) if blocked else fixed)
    return _call(
        body, name=name, grid=(s // tm,),
        out_shape=(jax.ShapeDtypeStruct((s, d), F32), jax.ShapeDtypeStruct((s, d), BF16),
                   jax.ShapeDtypeStruct((1, d), F32)),
        in_specs=[pl.BlockSpec((tm, width), row) for width in widths] + [
            w_spec, pl.BlockSpec((tm, d), row), pl.BlockSpec((1, d), fixed), pl.BlockSpec((tm, d), row)],
        out_specs=(pl.BlockSpec((tm, d), row), pl.BlockSpec((tm, d), row), pl.BlockSpec((1, d), fixed)),
        args=(*pieces, w, x, g, dres), ride=ride)


def out_bwd(dx1b, w_o, gates, y_sb, y_sw, w_bsb, w_bsw, *, name):
    s, d = dx1b.shape
    wd = w_bsb.shape[0]
    tm = min(ROW_TILE, s)

    def body(dx_ref, wo_ref, g_ref, ysb_ref, ysw_ref, wsb_ref, wsw_ref, dysb_ref, dysw_ref, dosb_ref, dosw_ref, dgl_ref):
        dm = _dot_nt(dx_ref[...], wo_ref[...])
        g = g_ref[...].astype(F32)
        g0, g1 = g[:, :d], g[:, d:]
        dy_sb = (dm * g0).astype(BF16)
        dy_sw = (dm * g1).astype(BF16)
        dysb_ref[...] = dy_sb
        dysw_ref[...] = dy_sw
        dosb_ref[...] = _dot_nt(dy_sb, wsb_ref[...]).astype(BF16)
        dosw_ref[...] = _dot_nt(dy_sw, wsw_ref[...]).astype(BF16)
        dgl_ref[:, :d] = (dm * ysb_ref[...].astype(F32) * (g0 * (1.0 - g0))).astype(BF16)
        dgl_ref[:, d:] = (dm * ysw_ref[...].astype(F32) * (g1 * (1.0 - g1))).astype(BF16)

    row = lambda i: (i, 0)
    fixed = lambda i: (0, 0)
    return pl.pallas_call(
        body, name=name, grid=(s // tm,),
        out_shape=(jax.ShapeDtypeStruct((s, d), BF16), jax.ShapeDtypeStruct((s, d), BF16),
                   jax.ShapeDtypeStruct((s, wd), BF16), jax.ShapeDtypeStruct((s, wd), BF16),
                   jax.ShapeDtypeStruct((s, 2 * d), BF16)),
        in_specs=[pl.BlockSpec((tm, d), row), pl.BlockSpec((d, d), fixed), pl.BlockSpec((tm, 2 * d), row),
                  pl.BlockSpec((tm, d), row), pl.BlockSpec((tm, d), row), pl.BlockSpec((wd, d), fixed),
                  pl.BlockSpec((wd, d), fixed)],
        out_specs=(pl.BlockSpec((tm, d), row), pl.BlockSpec((tm, d), row), pl.BlockSpec((tm, wd), row),
                   pl.BlockSpec((tm, wd), row), pl.BlockSpec((tm, 2 * d), row)),
        compiler_params=_params("parallel"),
    )(dx1b, w_o, gates, y_sb, y_sw, w_bsb, w_bsw)


def matmul_tn(a, pieces, *, a_block, out_cols, relu2, name):
    s, m = a.shape
    widths = [p.shape[1] for p in pieces]
    n = sum(widths)
    n_pieces = len(pieces)
    ts = min(512 if n >= 4096 else 2048, s)
    n_steps = s // ts
    if out_cols is None:
        out_shape = jax.ShapeDtypeStruct((m // a_block, a_block, n), BF16)
        out_spec = pl.BlockSpec((None, a_block, n), lambda i, k: (i, 0, 0))
    else:
        out_shape = jax.ShapeDtypeStruct((n // out_cols, m, out_cols), BF16)
        out_spec = pl.BlockSpec((n // out_cols, a_block, out_cols), lambda i, k: (0, i, 0))

    def body(a_ref, *refs):
        b_refs, o_ref, acc = refs[:n_pieces], refs[n_pieces], refs[n_pieces + 1]
        k = pl.program_id(1)

        @pl.when(k == 0)
        def _():
            acc[...] = jnp.zeros_like(acc)

        av = a_ref[...]
        if relu2:
            af = jnp.maximum(av.astype(F32), 0.0)
            av = af * af
        av = av.astype(BF16)
        off = 0
        for b_ref in b_refs:
            width = b_ref.shape[1]
            acc[:, off:off + width] += _dot_tn(av, b_ref[...].astype(BF16))
            off += width

        @pl.when(k == n_steps - 1)
        def _():
            if out_cols is None:
                o_ref[...] = acc[...].astype(BF16)
            else:
                for j in range(n // out_cols):
                    o_ref[j] = acc[:, j * out_cols:(j + 1) * out_cols].astype(BF16)

    return pl.pallas_call(
        body, name=name, grid=(m // a_block, n_steps), out_shape=out_shape,
        in_specs=[pl.BlockSpec((ts, a_block), lambda i, k: (k, i))] + [
            pl.BlockSpec((ts, width), lambda i, k: (k, 0)) for width in widths],
        out_specs=out_spec, scratch_shapes=[pltpu.VMEM((a_block, n), F32)],
        compiler_params=_params("parallel", "arbitrary"),
    )(a, *pieces)


def matmul_tn_row_blocks(pieces, b, *, n_blocks, name):
    s, n = b.shape
    widths = [p.shape[1] for p in pieces]
    m = sum(widths)
    rows = m // n_blocks
    n_pieces = len(pieces)
    ts = min(512, s)
    n_steps = s // ts
    half = n_blocks // 2

    def body(*refs):
        p_refs, b_ref, o_ref, a_tile, acc = refs[:n_pieces], refs[n_pieces], refs[n_pieces + 1], refs[-2], refs[-1]
        i, k = pl.program_id(0), pl.program_id(1)

        @pl.when(k == 0)
        def _():
            acc[...] = jnp.zeros_like(acc)

        off = 0
        for p_ref, width in zip(p_refs, widths):
            a_tile[:, off:off + width] = p_ref[...]
            off += width
        bv = b_ref[...]
        for side in range(2):
            @pl.when(i == side)
            def _():
                for j in range(half):
                    col = (side * half + j) * rows
                    acc[j] += _dot_tn(a_tile[:, col:col + rows], bv)

        @pl.when(k == n_steps - 1)
        def _():
            o_ref[...] = acc[...].astype(BF16)

    return pl.pallas_call(
        body, name=name, grid=(2, n_steps), out_shape=jax.ShapeDtypeStruct((n_blocks, rows, n), BF16),
        in_specs=[pl.BlockSpec((ts, width), lambda i, k: (k, 0)) for width in widths] + [
            pl.BlockSpec((ts, n), lambda i, k: (k, 0))],
        out_specs=pl.BlockSpec((half, rows, n), lambda i, k: (i, 0, 0)),
        scratch_shapes=[pltpu.VMEM((ts, m), BF16), pltpu.VMEM((half, rows, n), F32)],
        compiler_params=_params("parallel", "arbitrary"),
    )(*pieces, b)


def _softplus(z):
    return jnp.maximum(z, 0.0) + jnp.log(1.0 + jnp.exp(-jnp.abs(z)))


def _suffix_sums(x, tri2):
    groups = x.shape[1] // LANES
    outs, run = [None] * groups, None
    for g in reversed(range(groups)):
        xg = x[:, g * LANES:(g + 1) * LANES]
        hi, lo = _split_bf16(xg)
        inner = _dot(jnp.concatenate([hi, lo], axis=1), tri2)
        outs[g] = inner if run is None else inner + run
        total = jnp.sum(xg, axis=1, keepdims=True)
        run = total if run is None else run + total
    return jnp.concatenate(outs, axis=1), run


def _head_mask(h):
    return (lax.broadcasted_iota(jnp.int32, (1, LANES), 1) // HEAD_DIM) == h


def _stack_heads(x):
    zero = jnp.zeros_like(x)
    return jnp.concatenate([jnp.where(_head_mask(0), x, zero), jnp.where(_head_mask(1), x, zero)], axis=0)


def _unstack_heads(r, t):
    return jnp.where(_head_mask(0), r[:t], r[t:])


def _sb_positions(q0, tk):
    row = lax.broadcasted_iota(jnp.int32, (2 * SB_TQ, tk), 0)
    col = lax.broadcasted_iota(jnp.int32, (2 * SB_TQ, tk), 1)
    return q0 + jnp.where(row >= SB_TQ, row - SB_TQ, row), col


def _sb_first_key(q0):
    return pl.multiple_of(jnp.maximum(q0 + SB_TQ - SB_TK1, 0), SB_TQ)


def _sb_next_key(k_prev):
    return pl.multiple_of(jnp.maximum(k_prev - SB_TK, 0), SB_TQ)


def _sb_rows(q0):
    return pl.ds(pl.multiple_of(2 * q0, 2 * SB_TQ), 2 * SB_TQ)


def _sb_keep(live):
    return lambda x: jnp.where(live, x, 0.0)


def _sb_keep_first(q0, k0, interior):
    if not interior:
        tpos, col = _sb_positions(q0, SB_TK1)
        return _sb_keep(k0 + col < tpos)
    row = lax.broadcasted_iota(jnp.int32, (2 * SB_TQ, SB_TQ), 0)
    own = lax.broadcasted_iota(jnp.int32, (2 * SB_TQ, SB_TQ), 1) < jnp.where(row >= SB_TQ, row - SB_TQ, row)
    past = SB_TK1 - SB_TQ
    return lambda x: jnp.concatenate([x[:, :past], jnp.where(own, x[:, past:], 0.0)], axis=1)


SB_EDGE_TILES = 8
SB_SCAN_GROUP = 4


def _sb_first_pass(nq, first):
    edge = min(nq, SB_EDGE_TILES)
    lax.fori_loop(0, edge, first(False), 0, unroll=4)
    lax.fori_loop(edge, nq, first(True), 0, unroll=8)


def _sb_scan_tiles(c_all, nq, more):
    rows = SB_SCAN_GROUP * 2 * SB_TQ

    def group(g, carry):
        @pl.when(jnp.max(c_all[pl.ds(pl.multiple_of(g * rows, rows), rows), :]) > SB_CUTOFF)
        def _():
            lax.fori_loop(g * SB_SCAN_GROUP, (g + 1) * SB_SCAN_GROUP, more, 0)

        return carry

    lax.fori_loop(0, nq // SB_SCAN_GROUP, group, 0)


def sb_attn_fwd(proj, tri2, *, name, ride=()):
    s = proj.shape[0]
    nq = s // SB_TQ
    n_pairs = SB_WIDTH // LANES

    def body(q_ref, k_ref, v_ref, tri_ref, o_ref, c_all):
        def block(qh, k0, tk, keep, c):
            z = _dot_nt(qh, k_ref[pl.ds(k0, tk), :])
            sp = _softplus(z)
            tail, total = _suffix_sums(keep(-sp), tri_ref[...])
            w = keep(jnp.exp(z - sp + tail + c))
            return _dot(w.astype(BF16), v_ref[pl.ds(k0, tk), :]), c + total

        def load_q(q0):
            return _stack_heads(q_ref[pl.ds(q0, SB_TQ), :]) * SCALE

        def first(interior):
            def run(qb, carry):
                q0 = pl.multiple_of(qb * SB_TQ, SB_TQ)
                k0 = _sb_first_key(q0)
                acc, c = block(load_q(q0), k0, SB_TK1, _sb_keep_first(q0, k0, interior), jnp.zeros((2 * SB_TQ, 1), F32))
                o_ref[pl.ds(q0, SB_TQ), :] = _unstack_heads(acc, SB_TQ)
                c_all[_sb_rows(q0), :] = jnp.broadcast_to(jnp.where(k0 > 0, c, NEG), (2 * SB_TQ, LANES))
                return carry

            return run

        _sb_first_pass(nq, first)

        @pl.when(jnp.max(c_all[...]) > SB_CUTOFF)
        def _():
            def more(qb, carry):
                q0 = pl.multiple_of(qb * SB_TQ, SB_TQ)
                c0 = c_all[_sb_rows(q0), 0:1]

                @pl.when(jnp.max(c0) > SB_CUTOFF)
                def _():
                    qh = load_q(q0)
                    _, col = _sb_positions(q0, SB_TK)

                    def cond(st):
                        return jnp.logical_and(st[0] > 0, st[3] > SB_CUTOFF)

                    def step(st):
                        k_prev, c, acc, _ = st
                        k0 = _sb_next_key(k_prev)
                        part, c = block(qh, k0, SB_TK, _sb_keep(k0 + col < k_prev), c)
                        return k0, c, acc + part, jnp.max(c)

                    st = lax.while_loop(cond, step, (_sb_first_key(q0), c0, jnp.zeros((2 * SB_TQ, LANES), F32),
                                                     jnp.max(c0)))
                    o_ref[pl.ds(q0, SB_TQ), :] += _unstack_heads(st[2], SB_TQ)

                return carry

            _sb_scan_tiles(c_all, nq, more)

    def col_spec(j):
        return pl.BlockSpec((s, LANES), lambda p: (0, j * n_pairs + p))

    (o,), rides = _call(
        body, name=name, grid=(n_pairs,), out_shape=(jax.ShapeDtypeStruct((s, SB_WIDTH), F32),),
        in_specs=[col_spec(0), col_spec(1), col_spec(2), pl.BlockSpec((2 * LANES, LANES), lambda p: (0, 0))],
        out_specs=(pl.BlockSpec((s, LANES), lambda p: (0, p)),), scratch_shapes=[pltpu.VMEM((2 * s, LANES), F32)],
        args=(proj, proj, proj, tri2), ride=ride)
    return o, rides


def sb_attn_bwd(proj, tri2, o, do, *, name, ride=()):
    s = proj.shape[0]
    nq = s // SB_TQ
    n_pairs = SB_WIDTH // LANES

    def body(q_ref, k_ref, v_ref, tri_ref, o_ref, do_ref, dq_ref, dk_ref, dv_ref, dq_acc, dk_acc, dv_acc, c_all, e_all):
        dk_acc[...] = jnp.zeros_like(dk_acc)
        dv_acc[...] = jnp.zeros_like(dv_acc)

        def load(q0):
            qh = _stack_heads(q_ref[pl.ds(q0, SB_TQ), :]) * SCALE
            doh_b = _stack_heads(do_ref[pl.ds(q0, SB_TQ), :])
            ov = o_ref[pl.ds(q0, SB_TQ), :]
            dd = jnp.sum(doh_b.astype(F32) * jnp.concatenate([ov, ov], axis=0), axis=1, keepdims=True)
            return qh, doh_b, dd

        def block(qh, doh_b, dd, k0, tk, keep, c, ce):
            kt = k_ref[pl.ds(k0, tk), :]
            z = _dot_nt(qh, kt)
            sp = _softplus(z)
            lb = z - sp
            tail, total = _suffix_sums(keep(-sp), tri_ref[...])
            wb = keep(jnp.exp(lb + tail + c)).astype(BF16)
            e = wb.astype(F32) * _dot_nt(doh_b, v_ref[pl.ds(k0, tk), :])
            e_tail, e_total = _suffix_sums(e, tri_ref[...])
            dzb = keep(e - jnp.exp(lb) * (dd - ce - e_tail)).astype(BF16)
            dk_acc[pl.ds(k0, tk), :] += _dot_tn(dzb, qh)
            dv_acc[pl.ds(k0, tk), :] += _dot_tn(wb, doh_b)
            return _dot(dzb, kt), c + total, ce + e_total

        def first(interior):
            def run(qb, carry):
                q0 = pl.multiple_of(qb * SB_TQ, SB_TQ)
                qh, doh_b, dd = load(q0)
                k0 = _sb_first_key(q0)
                zero = jnp.zeros((2 * SB_TQ, 1), F32)
                dq, c, ce = block(qh, doh_b, dd, k0, SB_TK1, _sb_keep_first(q0, k0, interior), zero, zero)
                dq_acc[pl.ds(q0, SB_TQ), :] = _unstack_heads(dq, SB_TQ)
                c_all[_sb_rows(q0), :] = jnp.broadcast_to(jnp.where(k0 > 0, c, NEG), (2 * SB_TQ, LANES))
                e_all[_sb_rows(q0), :] = jnp.broadcast_to(ce, (2 * SB_TQ, LANES))
                return carry

            return run

        _sb_first_pass(nq, first)

        @pl.when(jnp.max(c_all[...]) > SB_CUTOFF)
        def _():
            def more(qb, carry):
                q0 = pl.multiple_of(qb * SB_TQ, SB_TQ)
                c0 = c_all[_sb_rows(q0), 0:1]

                @pl.when(jnp.max(c0) > SB_CUTOFF)
                def _():
                    qh, doh_b, dd = load(q0)
                    _, col = _sb_positions(q0, SB_TK)

                    def cond(st):
                        return jnp.logical_and(st[0] > 0, st[4] > SB_CUTOFF)

                    def step(st):
                        k_prev, c, ce, dq, _ = st
                        k0 = _sb_next_key(k_prev)
                        part, c, ce = block(qh, doh_b, dd, k0, SB_TK, _sb_keep(k0 + col < k_prev), c, ce)
                        return k0, c, ce, dq + part, jnp.max(c)

                    st = lax.while_loop(cond, step, (_sb_first_key(q0), c0, e_all[_sb_rows(q0), 0:1],
                                                     jnp.zeros((2 * SB_TQ, LANES), F32), jnp.max(c0)))
                    dq_acc[pl.ds(q0, SB_TQ), :] += _unstack_heads(st[3], SB_TQ)

                return carry

            _sb_scan_tiles(c_all, nq, more)

        dq_ref[...] = (dq_acc[...] * SCALE).astype(BF16)
        dk_ref[...] = dk_acc[...].astype(BF16)
        dv_ref[...] = dv_acc[...].astype(BF16)

    def col_spec(j):
        return pl.BlockSpec((s, LANES), lambda p: (0, j * n_pairs + p))

    pair = pl.BlockSpec((s, LANES), lambda p: (0, p))
    (dq, dk, dv), rides = _call(
        body, name=name, grid=(n_pairs,), out_shape=(jax.ShapeDtypeStruct((s, SB_WIDTH), BF16),) * 3,
        in_specs=[col_spec(0), col_spec(1), col_spec(2), pl.BlockSpec((2 * LANES, LANES), lambda p: (0, 0)), pair, pair],
        out_specs=(pair, pair, pair),
        scratch_shapes=[pltpu.VMEM((s, LANES), F32)] * 3 + [pltpu.VMEM((2 * s, LANES), F32)] * 2,
        args=(proj, proj, proj, tri2, o, do), ride=ride)
    return dq, dk, dv, rides


def _lane_lo():
    return lax.broadcasted_iota(jnp.int32, (1, LANES), 1) < HEAD_DIM


def _swap_halves(x):
    return pltpu.roll(x, HEAD_DIM, 1)


def _rot_half(y):
    first = (lax.broadcasted_iota(jnp.int32, (1, LANES), 1) % HEAD_DIM) < (HEAD_DIM // 2)
    return jnp.where(first, pltpu.roll(y, LANES - HEAD_DIM // 2, 1), pltpu.roll(y, HEAD_DIM // 2, 1))


def _head_mean(v, avg):
    hi, lo = _split_bf16(v)
    return _dot(hi, avg) + _dot(lo, avg)


def _head_avg_matrix():
    lane = jnp.arange(LANES) // HEAD_DIM
    return ((lane[:, None] == lane[None, :]).astype(F32) * (1.0 / HEAD_DIM)).astype(BF16)


def swa_prep_fwd(proj, cos_p, sin_p, gq, gk, *, name):
    s = proj.shape[0]
    tm = min(512, s)
    q_blk = (3 * SB_WIDTH) // SWA_Q_WIDTH
    k_blk = (3 * SB_WIDTH + SWA_Q_WIDTH) // LANES

    def body(q_ref, k_ref, cos_ref, sin_ref, gq_ref, gk_ref, avg_ref, qn_ref, kn_ref):
        cosv, sinv, avg = cos_ref[...], sin_ref[...], avg_ref[...]

        def norm_rope(xv, g):
            y = (xv * lax.rsqrt(_head_mean(xv * xv, avg) + NORM_EPS)) * g
            return y * cosv + _rot_half(y) * sinv

        for j in range(SWA_Q_WIDTH // LANES):
            sl = slice(j * LANES, (j + 1) * LANES)
            qn_ref[:, sl] = norm_rope(q_ref[:, sl].astype(F32), gq_ref[...]).astype(BF16)
        kn_ref[...] = norm_rope(k_ref[...].astype(F32), gk_ref[...]).astype(BF16)

    row = lambda i: (i, 0)
    fixed = lambda i: (0, 0)
    return pl.pallas_call(
        body, name=name, grid=(s // tm,),
        out_shape=(jax.ShapeDtypeStruct((s, SWA_Q_WIDTH), BF16), jax.ShapeDtypeStruct((s, LANES), BF16)),
        in_specs=[pl.BlockSpec((tm, SWA_Q_WIDTH), lambda i: (i, q_blk)), pl.BlockSpec((tm, LANES), lambda i: (i, k_blk)),
                  pl.BlockSpec((tm, LANES), row), pl.BlockSpec((tm, LANES), row),
                  pl.BlockSpec((1, LANES), fixed), pl.BlockSpec((1, LANES), fixed), pl.BlockSpec((LANES, LANES), fixed)],
        out_specs=(pl.BlockSpec((tm, SWA_Q_WIDTH), row), pl.BlockSpec((tm, LANES), row)),
        compiler_params=_params("parallel"),
    )(proj, proj, cos_p, sin_p, gq, gk, _head_avg_matrix())


def swa_prep_bwd(proj, cos_p, sin_p, gq, gk, dqn, dkn, dv, *, name):
    s = proj.shape[0]
    tm = min(512, s)
    q_blk = (3 * SB_WIDTH) // SWA_Q_WIDTH
    k_blk = (3 * SB_WIDTH + SWA_Q_WIDTH) // LANES

    def body(q_ref, k_ref, cos_ref, sin_ref, gq_ref, gk_ref, avg_ref, dqn_ref, dkn_ref, dv_ref, dq_ref, dk_ref, dvb_ref,
             dgq_ref, dgk_ref):
        @pl.when(pl.program_id(0) == 0)
        def _():
            dgq_ref[...] = jnp.zeros_like(dgq_ref)
            dgk_ref[...] = jnp.zeros_like(dgk_ref)

        cosv, sinv, avg = cos_ref[...], sin_ref[...], avg_ref[...]

        def bwd(xv, g, dout):
            dy = dout * cosv + _rot_half(dout * sinv)
            r = lax.rsqrt(_head_mean(xv * xv, avg) + NORM_EPS)
            dyg = dy * g
            dx = r * dyg - xv * ((r * r * r) * _head_mean(dyg * xv, avg))
            return dx, jnp.sum(dy * (xv * r), axis=0, keepdims=True)

        for j in range(SWA_Q_WIDTH // LANES):
            sl = slice(j * LANES, (j + 1) * LANES)
            dx, dg = bwd(q_ref[:, sl].astype(F32), gq_ref[...], dqn_ref[:, sl])
            dq_ref[:, sl] = dx.astype(BF16)
            dgq_ref[:, sl] += dg
        dx, dg = bwd(k_ref[...].astype(F32), gk_ref[...], dkn_ref[...])
        dk_ref[...] = dx.astype(BF16)
        dgk_ref[...] += dg
        dvb_ref[...] = dv_ref[...].astype(BF16)

    row = lambda i: (i, 0)
    fixed = lambda i: (0, 0)
    lane_row = pl.BlockSpec((tm, LANES), row)
    return pl.pallas_call(
        body, name=name, grid=(s // tm,),
        out_shape=(jax.ShapeDtypeStruct((s, SWA_Q_WIDTH), BF16), jax.ShapeDtypeStruct((s, LANES), BF16),
                   jax.ShapeDtypeStruct((s, LANES), BF16),
                   jax.ShapeDtypeStruct((1, SWA_Q_WIDTH), F32), jax.ShapeDtypeStruct((1, LANES), F32)),
        in_specs=[pl.BlockSpec((tm, SWA_Q_WIDTH), lambda i: (i, q_blk)), pl.BlockSpec((tm, LANES), lambda i: (i, k_blk)),
                  lane_row, lane_row, pl.BlockSpec((1, LANES), fixed), pl.BlockSpec((1, LANES), fixed),
                  pl.BlockSpec((LANES, LANES), fixed), pl.BlockSpec((tm, SWA_Q_WIDTH), row), lane_row, lane_row],
        out_specs=(pl.BlockSpec((tm, SWA_Q_WIDTH), row), lane_row, lane_row,
                   pl.BlockSpec((1, SWA_Q_WIDTH), fixed), pl.BlockSpec((1, LANES), fixed)),
        compiler_params=_params("arbitrary"),
    )(proj, proj, cos_p, sin_p, gq, gk, _head_avg_matrix(), dqn, dkn, dv)


def _swa_kv_copies(k_ref, v_ref, kg_ref, vg_ref, second_kv):
    s = k_ref.shape[0]
    rows = min(512, s)
    keep = jnp.logical_xor(_lane_lo(), second_kv)

    def chunk(r, carry):
        sl = pl.ds(pl.multiple_of(r * rows, rows), rows)
        for src, dst in ((k_ref, kg_ref), (v_ref, vg_ref)):
            f = src[sl, :].astype(F32)
            dst[sl, :] = jnp.where(keep, f, _swap_halves(f)).astype(BF16)
        return carry

    lax.fori_loop(0, s // rows, chunk, 0)


def _swa_tile(i, kg_ref, vg_ref):
    q0 = pl.multiple_of(i * SWA_TQ, SWA_TQ)
    k0 = pl.multiple_of(jnp.maximum(i - 1, 0) * SWA_TQ, SWA_TQ)
    kg = kg_ref[pl.ds(k0, SWA_TK), :]
    vg = vg_ref[pl.ds(k0, SWA_TK), :]
    row = lax.broadcasted_iota(jnp.int32, (2 * SWA_TQ, SWA_TK), 0)
    tpos = q0 + jnp.where(row >= SWA_TQ, row - SWA_TQ, row)
    spos = k0 + lax.broadcasted_iota(jnp.int32, (2 * SWA_TQ, SWA_TK), 1)
    valid = jnp.logical_and(spos <= tpos, spos > tpos - WINDOW)
    return q0, k0, kg, vg, valid


def _swa_probs(qh, kg, valid, sink):
    z = jnp.where(valid, _dot_nt(qh, kg) * SCALE, NEG)
    m = jnp.maximum(jnp.max(z, axis=1, keepdims=True), sink)
    pexp = jnp.exp(z - m)
    psink = jnp.exp(sink - m)
    inv = 1.0 / (jnp.sum(pexp, axis=1, keepdims=True) + psink)
    return pexp * inv, psink * inv


def _stacked_sink(sink_row):
    s0 = jnp.sum(jnp.where(_head_mask(0), sink_row, 0.0), axis=1, keepdims=True) * (1.0 / HEAD_DIM)
    s1 = jnp.sum(jnp.where(_head_mask(1), sink_row, 0.0), axis=1, keepdims=True) * (1.0 / HEAD_DIM)
    top = lax.broadcasted_iota(jnp.int32, (2 * SWA_TQ, 1), 0) < SWA_TQ
    return jnp.where(top, s0, s1)


def swa_attn_fwd(qn, kn, proj, sink_p, *, name, ride=()):
    s = qn.shape[0]
    nq = s // SWA_TQ
    n_pairs = SWA_Q_WIDTH // LANES
    v_blk = (3 * SB_WIDTH + SWA_Q_WIDTH + SWA_KV_WIDTH) // LANES

    def body(q_ref, k_ref, v_ref, s_ref, o_ref, kg_ref, vg_ref):
        _swa_kv_copies(k_ref, v_ref, kg_ref, vg_ref, (pl.program_id(0) // 2) == 1)
        sink = _stacked_sink(s_ref[...])

        def tile(i, carry):
            q0, _, kg, vg, valid = _swa_tile(i, kg_ref, vg_ref)
            probs, _ = _swa_probs(_stack_heads(q_ref[pl.ds(q0, SWA_TQ), :]), kg, valid, sink)
            o_ref[pl.ds(q0, SWA_TQ), :] = _unstack_heads(_dot(probs.astype(BF16), vg), SWA_TQ)
            return carry

        lax.fori_loop(0, nq, tile, 0, unroll=8)

    pair = pl.BlockSpec((s, LANES), lambda p: (0, p))
    whole = pl.BlockSpec((s, LANES), lambda p: (0, 0))
    (o,), rides = _call(
        body, name=name, grid=(n_pairs,), out_shape=(jax.ShapeDtypeStruct((s, SWA_Q_WIDTH), F32),),
        in_specs=[pair, whole, pl.BlockSpec((s, LANES), lambda p: (0, v_blk)),
                  pl.BlockSpec((None, 1, LANES), lambda p: (p, 0, 0))],
        out_specs=(pair,), scratch_shapes=[pltpu.VMEM((s, LANES), BF16)] * 2, args=(qn, kn, proj, sink_p), ride=ride)
    return o, rides


def swa_attn_bwd(qn, kn, proj, sink_p, o, do, *, name, ride=()):
    s = qn.shape[0]
    nq = s // SWA_TQ
    n_pairs = SWA_Q_WIDTH // LANES
    v_blk = (3 * SB_WIDTH + SWA_Q_WIDTH + SWA_KV_WIDTH) // LANES
    fold_rows = min(512, s)

    def body(q_ref, k_ref, v_ref, s_ref, o_ref, do_ref, dq_ref, dk_ref, dv_ref, ds_ref, acc_k, acc_v, kg_ref, vg_ref):
        p = pl.program_id(0)
        _swa_kv_copies(k_ref, v_ref, kg_ref, vg_ref, (p // 2) == 1)
        sink = _stacked_sink(s_ref[...])

        @pl.when(p % 2 == 0)
        def _():
            acc_k[...] = jnp.zeros_like(acc_k)
            acc_v[...] = jnp.zeros_like(acc_v)

        ds_ref[...] = jnp.zeros_like(ds_ref)

        def tile(i, carry):
            q0, k0, kg, vg, valid = _swa_tile(i, kg_ref, vg_ref)
            qh = _stack_heads(q_ref[pl.ds(q0, SWA_TQ), :])
            doh_b = _stack_heads(do_ref[pl.ds(q0, SWA_TQ), :])
            ov = o_ref[pl.ds(q0, SWA_TQ), :]
            delta = jnp.sum(doh_b.astype(F32) * jnp.concatenate([ov, ov], axis=0), axis=1, keepdims=True)
            probs, psink = _swa_probs(qh, kg, valid, sink)
            dz = probs * (_dot_nt(doh_b, vg) - delta)
            dzb = (dz * SCALE).astype(BF16)
            dq_ref[pl.ds(q0, SWA_TQ), :] = _unstack_heads(_dot(dzb, kg), SWA_TQ)
            acc_k[pl.ds(k0, SWA_TK), :] += _dot_tn(dzb, qh)
            acc_v[pl.ds(k0, SWA_TK), :] += _dot_tn(probs.astype(BF16), doh_b)
            pd = psink * delta
            ds_ref[...] -= jnp.where(_head_mask(0), jnp.sum(pd[:SWA_TQ], axis=0, keepdims=True),
                                     jnp.sum(pd[SWA_TQ:], axis=0, keepdims=True))
            return carry

        lax.fori_loop(0, nq, tile, 0, unroll=8)

        def fold_into(first_head):
            def fold(r, carry):
                rows = pl.ds(pl.multiple_of(r * fold_rows, fold_rows), fold_rows)
                for acc, out in ((acc_k, dk_ref), (acc_v, dv_ref)):
                    a = acc[rows, :]
                    both = a + _swap_halves(a)
                    if first_head:
                        out[rows, :] = jnp.where(_lane_lo(), both, 0.0)
                    else:
                        out[rows, :] = jnp.where(_lane_lo(), out[rows, :], both)
                return carry

            lax.fori_loop(0, s // fold_rows, fold, 0)

        @pl.when(p == 1)
        def _():
            fold_into(True)

        @pl.when(p == 3)
        def _():
            fold_into(False)

    pair = pl.BlockSpec((s, LANES), lambda p: (0, p))
    whole = pl.BlockSpec((s, LANES), lambda p: (0, 0))
    sink_spec = pl.BlockSpec((None, 1, LANES), lambda p: (p, 0, 0))
    (dq, dk, dv, dsink), rides = _call(
        body, name=name, grid=(n_pairs,),
        out_shape=(jax.ShapeDtypeStruct((s, SWA_Q_WIDTH), F32), jax.ShapeDtypeStruct((s, LANES), F32),
                   jax.ShapeDtypeStruct((s, LANES), F32), jax.ShapeDtypeStruct((n_pairs, 1, LANES), F32)),
        in_specs=[pair, whole, pl.BlockSpec((s, LANES), lambda p: (0, v_blk)), sink_spec, pair, pair],
        out_specs=(pair, whole, whole, sink_spec),
        scratch_shapes=[pltpu.VMEM((s, LANES), F32)] * 2 + [pltpu.VMEM((s, LANES), BF16)] * 2,
        args=(qn, kn, proj, sink_p, o, do), ride=ride)
    return dq, dk, dv, dsink, rides


def _rope_tables(s):
    inv_freq = 1.0 / (ROPE_THETA ** (jnp.arange(0, HEAD_DIM, 2, dtype=F32) / HEAD_DIM))
    ang = jnp.arange(s, dtype=F32)[:, None] * inv_freq[None, :]
    cos, sin = jnp.cos(ang), jnp.sin(ang)
    cos_p = jnp.tile(jnp.concatenate([cos, cos], axis=1), (1, LANES // HEAD_DIM))
    sin_p = jnp.tile(jnp.concatenate([-sin, sin], axis=1), (1, LANES // HEAD_DIM))
    return cos_p, sin_p


def _lane_tile(v, reps):
    return jnp.tile(v.reshape(1, -1), (1, reps))


def _natural(stack, w):
    n, r, c = stack.shape
    if MATRIX_NAMES[w] in ROW_SHARDED or w == W_IN:
        return stack.reshape(n * r, c)
    if w == W_UP:
        return stack
    return jnp.transpose(stack, (1, 0, 2)).reshape(r, n * c)


def _pack_small(tree):
    flat = jnp.concatenate([tree[n].reshape(-1) for n in SMALL_NAMES])
    rows = -(-flat.shape[0] // (8 * LANES)) * 8
    return jnp.pad(flat, (0, rows * LANES - flat.shape[0])).reshape(rows, LANES)


def _unpack_small(packed, shapes):
    flat, out, off = packed.reshape(-1), {}, 0
    for n in SMALL_NAMES:
        size = shapes[n][0] * shapes[n][1]
        out[n] = flat[off:off + size].reshape(shapes[n])
        off += size
    return out


def train_step(x, target, weights, mom_m, mom_v):
    s = x.shape[0]
    cos_p, sin_p = _rope_tables(s)
    tri = (jnp.arange(LANES)[:, None] > jnp.arange(LANES)[None, :]).astype(BF16)
    tri = jnp.concatenate([tri, tri], axis=0)
    local = {n: (jnp.swapaxes(t, 1, 2) if n == "w_in" else t) for n, t in weights.items()}
    local_m = {n: (jnp.swapaxes(t, 1, 2) if n == "w_in" else t) for n, t in mom_m.items()}
    local_v = {n: (jnp.swapaxes(t, 1, 2) if n == "w_in" else t) for n, t in mom_v.items()}
    shards = [[local[n][l].astype(BF16) for n in MATRIX_NAMES] for l in range(DEPTH)]
    core = lax.axis_index("c").astype(jnp.int32).reshape(1)
    chip = (2 * lax.axis_index("x") + lax.axis_index("y")).astype(jnp.int32).reshape(1)

    def gather(l, ws, rows=None, stacks=None):
        return GatherJob([shards[l][w] for w in ws], rows=rows, stacks=stacks)

    def halves(w):
        r = shards[0][w].shape[0] // 2
        return (0, r), (r, r)

    w_in = _natural(exchange_alone(gather(0, [W_IN]), name="gather_w_in0")[0], W_IN)
    saved = []
    for l in range(DEPTH):
        g_mix = weights["mix_norm_g"][l].reshape(1, D_MODEL)
        g_mlp = weights["mlp_norm_g"][l].reshape(1, D_MODEL)
        gq = _lane_tile(weights["q_norm_g"][l], LANES // HEAD_DIM)
        gk = _lane_tile(weights["k_norm_g"][l], LANES // HEAD_DIM)
        sink_p = jnp.repeat(weights["sinks"][l].reshape(SWA_Q_WIDTH // LANES, 2), HEAD_DIM, axis=1)
        sink_p = sink_p.reshape(SWA_Q_WIDTH // LANES, 1, LANES)
        (h, proj, gates), _ = norm_matmul(x, g_mix, w_in, gate_split=ATTN_WIDTH, name="in_proj")
        o_sb, ((s_up,), (s_bsb, s_bsw, s_out)) = sb_attn_fwd(
            proj, tri, name="sb_fwd", ride=[gather(l, [W_UP]), gather(l, [W_BSB, W_BSW, W_OUT])])
        qn, kn = swa_prep_fwd(proj, cos_p, sin_p, gq, gk, name="swa_prep")
        o_sw, ((s_down,),) = swa_attn_fwd(qn, kn, proj, sink_p, name="swa_fwd",
                                          ride=[gather(l, [W_DOWN], rows=halves(W_DOWN)[0])])
        more = l + 1 < DEPTH
        (x1, y_sb, y_sw, merged), rides = merge_out_fwd(
            x, o_sb, o_sw, gates, _natural(s_bsb, W_BSB), _natural(s_bsw, W_BSW), _natural(s_out, W_OUT),
            name="merge_out" if more else "merge_out_last",
            ride=[gather(l + 1, [W_IN], rows=halves(W_IN)[0])] if more else [])
        (h2, u), ((s_down,),) = norm_matmul(x1, g_mlp, s_up, gate_split=None, name="mlp_up",
                                            ride=[gather(l, [W_DOWN], rows=halves(W_DOWN)[1], stacks=[s_down])])
        mats = [w_in, _natural(s_bsb, W_BSB), _natural(s_bsw, W_BSW), _natural(s_out, W_OUT), s_up,
                _natural(s_down, W_DOWN)]
        if more:
            (x2,), ((s_in,),) = mlp_down_fwd(x1, u, mats[W_DOWN], name="mlp_down",
                                             ride=[gather(l + 1, [W_IN], rows=halves(W_IN)[1], stacks=rides[0])])
            w_in = _natural(s_in, W_IN)
        else:
            (x2,), _ = mlp_down_fwd(x1, u, mats[W_DOWN], name="mlp_down_last")
        saved.append(dict(x=x, h=h, proj=proj, gates=gates, o_sb=o_sb, qn=qn, kn=kn, o_sw=o_sw, y_sb=y_sb, y_sw=y_sw,
                          merged=merged, x1=x1, h2=h2, u=u, g_mix=g_mix, g_mlp=g_mlp, gq=gq, gk=gk, sink_p=sink_p,
                          mats=mats))
        x = x2

    dx, dxb, loss = loss_head(x, target, name="loss_head")

    shard_shapes = [local[n].shape[1:] for n in MATRIX_NAMES]
    parts = [lax.empty((DEPTH, N_CHIPS) + sh, BF16) for sh in shard_shapes]
    lands = [lax.empty((DEPTH, 3) + sh, BF16) for sh in shard_shapes]
    small_grads = {n: [None] * DEPTH for n in SMALL_NAMES}
    half = D_MODEL // 2

    def summed(l, ws, grads, landed):
        new = pair_sum(l, grads, landed, [parts[w] for w in ws], core, name="grad_pair_sum")
        for w, p in zip(ws, new):
            parts[w] = p

    def chip_job(items):
        return ChipJob(items, parts, lands)

    def landed_chip(job, outs):
        for w, a in zip(job.ws, outs):
            lands[w] = a

    in_pending = None
    for l in reversed(range(DEPTH)):
        a = saved[l]
        mats = a["mats"]
        in_jobs = [chip_job([(in_pending, W_IN, rows)]) for rows in halves(W_IN)] if in_pending is not None else []
        (du,), rides = mlp_bwd_up(dxb, a["u"], mats[W_DOWN], name="mlp_bwd_up" if in_jobs else "mlp_bwd_up_first",
                                  ride=in_jobs[:1])
        if in_jobs:
            landed_chip(in_jobs[0], rides[0])
            in_jobs[1] = chip_job([(in_pending, W_IN, halves(W_IN)[1])])
        dw_down = matmul_tn(a["u"], [dxb], a_block=half, out_cols=None, relu2=True, name="dw_down")
        dw_up = matmul_tn(a["h2"], [du], a_block=half, out_cols=du.shape[1] // N_DEV, relu2=False, name="dw_up")
        g_mlp_w = [dw_up, dw_down.reshape((N_DEV,) + shard_shapes[W_DOWN])]
        (dx1, dx1b, dg_mlp), rides = matmul_nt_norm_bwd(
            [du], mats[W_UP], a["x1"], a["g_mlp"], dx, name="mlp_bwd_norm" if in_jobs else "mlp_bwd_norm_first",
            ride=[PairJob(g_mlp_w)] + in_jobs[1:])
        if in_jobs:
            landed_chip(in_jobs[1], rides[1])
        summed(l, [W_UP, W_DOWN], g_mlp_w, rides[0])
        small_grads["mlp_norm_g"][l] = dg_mlp.reshape(D_MODEL)
        dw_out = matmul_tn(a["merged"], [dx1b], a_block=half, out_cols=None, relu2=False, name="dw_out")
        dy_sb, dy_sw, do_sb, do_sw, dgl = out_bwd(dx1b, mats[W_OUT], a["gates"], a["y_sb"], a["y_sw"],
                                                  mats[W_BSB], mats[W_BSW], name="out_bwd")
        dw_bsb = matmul_tn(a["o_sb"], [dy_sb], a_block=half, out_cols=D_MODEL // N_DEV, relu2=False, name="dw_branch_sb")
        dw_bsw = matmul_tn(a["o_sw"], [dy_sw], a_block=half, out_cols=D_MODEL // N_DEV, relu2=False, name="dw_branch_swa")
        g_mix_w = [dw_bsb, dw_bsw, dw_out.reshape((N_DEV,) + shard_shapes[W_OUT])]
        job = chip_job([(l, W_UP), (l, W_DOWN)])
        dq_sb, dk_sb, dv_sb, (outs, landed) = sb_attn_bwd(a["proj"], tri, a["o_sb"], do_sb, name="sb_bwd",
                                                         ride=[job, PairJob(g_mix_w)])
        landed_chip(job, outs)
        summed(l, [W_BSB, W_BSW, W_OUT], g_mix_w, landed)
        job = chip_job([(l, W_BSB), (l, W_BSW), (l, W_OUT)])
        dqn, dkn, dv_sw, dsink, (outs,) = swa_attn_bwd(a["qn"], a["kn"], a["proj"], a["sink_p"], a["o_sw"], do_sw,
                                                      name="swa_bwd", ride=[job])
        landed_chip(job, outs)
        dq_sw, dk_sw, dv_swb, dgq, dgk = swa_prep_bwd(a["proj"], cos_p, sin_p, a["gq"], a["gk"], dqn, dkn, dv_sw,
                                                      name="swa_prep_bwd")
        small_grads["q_norm_g"][l] = dgq.reshape(SWA_Q_WIDTH // HEAD_DIM, HEAD_DIM).sum(0)
        small_grads["k_norm_g"][l] = dgk.reshape(LANES // HEAD_DIM, HEAD_DIM).sum(0)
        small_grads["sinks"][l] = dsink[:, 0, ::HEAD_DIM].reshape(SWA_Q_WIDTH // HEAD_DIM)
        pieces = [dq_sb, dk_sb, dv_sb, dq_sw, dk_sw, dv_swb, dgl]
        g_in = [matmul_tn_row_blocks(pieces, a["h"], n_blocks=N_DEV, name="dw_in")]
        if l > 0:
            (dx, dxb, dg_mix), (landed,) = matmul_nt_norm_bwd(pieces, mats[W_IN], a["x"], a["g_mix"], dx1,
                                                             name="in_proj_bwd", ride=[PairJob(g_in)])
            summed(l, [W_IN], g_in, landed)
            in_pending = l
        else:
            summed(l, [W_IN], g_in, exchange_alone(PairJob(g_in), name="grad_pair_exchange_in0"))
            job = chip_job([(l, W_IN)])
            (dx, dxb, dg_mix), (outs,) = matmul_nt_norm_bwd(pieces, mats[W_IN], a["x"], a["g_mix"], dx1,
                                                           name="in_proj_bwd_last", ride=[job])
            landed_chip(job, outs)
        small_grads["mix_norm_g"][l] = dg_mix.reshape(D_MODEL)

    out_g, out_d, out_m, out_v = {}, {}, {}, {}
    for i, n in enumerate(MATRIX_NAMES):
        outs = reduce_adamw(parts[i], lands[i], chip, local[n], local_m[n], local_v[n], name="adamw_" + n)
        if n == "w_in":
            outs = [jnp.swapaxes(t, 1, 2) for t in outs]
        out_g[n], out_d[n], out_m[n], out_v[n] = outs
    small_shapes = {n: weights[n].shape for n in SMALL_NAMES}
    small_all = gather_small(_pack_small({n: jnp.stack(v) for n, v in small_grads.items()}), name="gather_small_grads")
    sg, sd, sm, sv = small_adamw(small_all, _pack_small(weights), _pack_small(mom_m), _pack_small(mom_v),
                                 name="small_adamw")
    for tree, packed_small in ((out_g, sg), (out_d, sd), (out_m, sm), (out_v, sv)):
        tree.update(_unpack_small(packed_small, small_shapes))
    return loss, dx, (out_g, out_d, out_m, out_v)


def kernel(x, mix_norm_g, w_in, q_norm_g, k_norm_g, sinks, w_branch_sb, w_branch_swa, w_out, mlp_norm_g, w_up, w_down, loss_target, m_mix_norm_g, m_w_in, m_q_norm_g, m_k_norm_g, m_sinks, m_w_branch_sb, m_w_branch_swa, m_w_out, m_mlp_norm_g, m_w_up, m_w_down, v_mix_norm_g, v_w_in, v_q_norm_g, v_k_norm_g, v_sinks, v_w_branch_sb, v_w_branch_swa, v_w_out, v_mlp_norm_g, v_w_up, v_w_down):
    weights = dict(mix_norm_g=mix_norm_g, w_in=w_in, q_norm_g=q_norm_g, k_norm_g=k_norm_g, sinks=sinks,
                   w_branch_sb=w_branch_sb, w_branch_swa=w_branch_swa, w_out=w_out, mlp_norm_g=mlp_norm_g, w_up=w_up,
                   w_down=w_down)
    mom_m = dict(mix_norm_g=m_mix_norm_g, w_in=m_w_in, q_norm_g=m_q_norm_g, k_norm_g=m_k_norm_g, sinks=m_sinks,
                 w_branch_sb=m_w_branch_sb, w_branch_swa=m_w_branch_swa, w_out=m_w_out, mlp_norm_g=m_mlp_norm_g,
                 w_up=m_w_up, w_down=m_w_down)
    mom_v = dict(mix_norm_g=v_mix_norm_g, w_in=v_w_in, q_norm_g=v_q_norm_g, k_norm_g=v_k_norm_g, sinks=v_sinks,
                 w_branch_sb=v_w_branch_sb, w_branch_swa=v_w_branch_swa, w_out=v_w_out, mlp_norm_g=v_mlp_norm_g,
                 w_up=v_w_up, w_down=v_w_down)
    loss_part, grad_x, outs = train_step(x[0], loss_target[0], weights, mom_m, mom_v)
    loss = lax.psum(loss_part[0, 0], MESH_AXES)
    return (loss, grad_x[None], *[outs[0][n] for n in WEIGHT_ORDER], *[outs[1][n] for n in WEIGHT_ORDER],
            *[outs[2][n] for n in WEIGHT_ORDER], *[outs[3][n] for n in WEIGHT_ORDER])
```
